```python
import numpy as np
import jax
import jax.numpy as jnp
from jax import lax

D_MODEL = 1024
BATCH = 8
SEQ = 16384
DEPTH = 4

HEAD_DIM = 64
N_MIXERS = 2
A_Q_HEADS = 16
A_KV_HEADS = 4
A_HALF_WINDOW = 128
B_GROUPS = ((128, 1), (512, 4), (2048, 16))
B_Q_HEADS = 8
B_KV_HEADS = 2
D_FF = -(-8 * D_MODEL // (3 * 256)) * 256
A_QKV = (A_Q_HEADS + 2 * A_KV_HEADS) * HEAD_DIM
B_QKV = len(B_GROUPS) * (B_Q_HEADS + 2 * B_KV_HEADS) * HEAD_DIM
A_OUT = A_Q_HEADS * HEAD_DIM
B_OUT = B_Q_HEADS * HEAD_DIM
N_A_LAYERS = (DEPTH + 1) // 2
N_B_LAYERS = DEPTH // 2
RMS_EPS = 1e-6
NEG = -1e30

kernel_name = "hybrid_window_dilated_alibi_encoder"


def rmsnorm(x, g):
    xf = x.astype(jnp.float32)
    y = xf * lax.rsqrt(jnp.mean(xf * xf, axis=-1, keepdims=True) + RMS_EPS)
    return (y * g.astype(jnp.float32)).astype(x.dtype)


def alibi_slopes(n):
    return jnp.asarray(2.0 ** (-8.0 * np.arange(1, n + 1) / n), dtype=jnp.float32)


def banded_attention(q, k, v, half_window, slopes, dist_scale, sink=None):
    n, L, hkv, g, dh = q.shape
    blk = half_window
    nb = -(-L // blk)
    lp = nb * blk
    qb = jnp.pad(q, ((0, 0), (0, lp - L), (0, 0), (0, 0), (0, 0))).reshape(n, nb, blk, hkv, g, dh)
    pad_kv = ((0, 0), (blk, lp - L + blk), (0, 0), (0, 0))
    kb = jnp.pad(k, pad_kv).reshape(n, nb + 2, blk, hkv, dh)
    vb = jnp.pad(v, pad_kv).reshape(n, nb + 2, blk, hkv, dh)
    kw = jnp.concatenate([kb[:, :-2], kb[:, 1:-1], kb[:, 2:]], axis=2)
    vw = jnp.concatenate([vb[:, :-2], vb[:, 1:-1], vb[:, 2:]], axis=2)
    rel = jnp.arange(3 * blk)[None, :] - blk - jnp.arange(blk)[:, None]
    kpos = jnp.arange(nb)[:, None] * blk - blk + jnp.arange(3 * blk)[None, :]
    mask = (jnp.abs(rel) <= half_window)[None] & ((kpos >= 0) & (kpos < L))[:, None, :]
    dist = (dist_scale * jnp.abs(rel)).astype(jnp.float32)
    bias = -slopes.astype(jnp.float32)[:, :, None, None] * dist
    s = jnp.einsum('nbqhgd,nbkhd->nbhgqk', qb, kw,
                   preferred_element_type=jnp.float32) * (dh ** -0.5) + bias
    s = jnp.where(mask[None, :, None, None], s, NEG)
    m = s.max(axis=-1)
    if sink is not None:
        sk = sink.astype(jnp.float32)[:, :, None]
        m = jnp.maximum(m, sk)
    p = jnp.exp(s - m[..., None])
    l = p.sum(axis=-1)
    if sink is not None:
        l = l + jnp.exp(sk - m)
    o = jnp.einsum('nbhgqk,nbkhd->nbqhgd', p, vw.astype(jnp.float32))
    o = o / jnp.moveaxis(l, -1, 2)[..., None]
    lse = jnp.moveaxis(m + jnp.log(l), -1, 2)
    o = o.reshape(n, lp, hkv, g, dh)[:, :L].astype(q.dtype)
    lse = lse.reshape(n, lp, hkv, g)[:, :L]
    return o, lse


def to_strided(t, dil):
    b, s = t.shape[:2]
    rest = t.shape[2:]
    t = t.reshape(b, s // dil, dil, *rest)
    return jnp.moveaxis(t, 2, 1).reshape(b * dil, s // dil, *rest)


def from_strided(t, b, dil):
    l = t.shape[1]
    rest = t.shape[2:]
    t = t.reshape(b, dil, l, *rest)
    return jnp.moveaxis(t, 1, 2).reshape(b, l * dil, *rest)


def windowed_gqa(h, w_in, w_out, sink):
    b, s, _ = h.shape
    gq = A_Q_HEADS // A_KV_HEADS
    dq, dk = A_Q_HEADS * HEAD_DIM, A_KV_HEADS * HEAD_DIM
    qkv = h @ w_in
    q = qkv[..., :dq].reshape(b, s, A_KV_HEADS, gq, HEAD_DIM)
    k = qkv[..., dq:dq + dk].reshape(b, s, A_KV_HEADS, HEAD_DIM)
    v = qkv[..., dq + dk:].reshape(b, s, A_KV_HEADS, HEAD_DIM)
    slopes = alibi_slopes(A_Q_HEADS).reshape(A_KV_HEADS, gq)
    o, _ = banded_attention(q, k, v, A_HALF_WINDOW, slopes, 1, sink.reshape(A_KV_HEADS, gq))
    return o.reshape(b, s, A_OUT) @ w_out


def dilated_attention(h, w_in, w_out):
    b, s, _ = h.shape
    ng = len(B_GROUPS)
    gq = B_Q_HEADS // B_KV_HEADS
    dq, dk = ng * B_Q_HEADS * HEAD_DIM, ng * B_KV_HEADS * HEAD_DIM
    qkv = h @ w_in
    q = qkv[..., :dq].reshape(b, s, ng, B_KV_HEADS, gq, HEAD_DIM)
    k = qkv[..., dq:dq + dk].reshape(b, s, ng, B_KV_HEADS, HEAD_DIM)
    v = qkv[..., dq + dk:].reshape(b, s, ng, B_KV_HEADS, HEAD_DIM)
    slopes = alibi_slopes(ng * B_Q_HEADS).reshape(ng, B_KV_HEADS, gq)
    outs, lses = [], []
    for gi, (window, dil) in enumerate(B_GROUPS):
        half = window // (2 * dil)
        o, lse = banded_attention(to_strided(q[:, :, gi], dil), to_strided(k[:, :, gi], dil),
                                  to_strided(v[:, :, gi], dil), half, slopes[gi], dil)
        outs.append(from_strided(o, b, dil))
        lses.append(from_strided(lse, b, dil))
    alpha = jax.nn.softmax(jnp.stack(lses), axis=0)
    o = jnp.einsum('nbshg,nbshgd->bshgd', alpha,
                   jnp.stack(outs).astype(jnp.float32)).astype(h.dtype)
    return o.reshape(b, s, B_OUT) @ w_out


def swiglu(h, w_in, w_out):
    gu = h @ w_in
    gate, up = jnp.split(gu, 2, axis=-1)
    return (jax.nn.silu(gate) * up) @ w_out


def _fwd_setup_inputs(seed: int = 0) -> dict:
    key = jax.random.key(seed)
    ks = jax.random.split(key, 16)
    D = D_MODEL
    nrm = lambda k, shape, scale: jax.random.normal(k, shape, jnp.float32) * scale
    return {
        "x": nrm(ks[0], (BATCH, SEQ, D), 1.0),
        "c": nrm(ks[1], (BATCH, D), 1.0),
        "ada_w": nrm(ks[2], (DEPTH, D, 6 * D), 0.5 * D ** -0.5),
        "ada_b": nrm(ks[3], (DEPTH, 6 * D), 0.1),
        "norm_mix": 1.0 + nrm(ks[4], (DEPTH, D), 0.05),
        "norm_ffn": 1.0 + nrm(ks[5], (DEPTH, D), 0.05),
        "ffn_w_in": nrm(ks[6], (DEPTH, D, 2 * D_FF), D ** -0.5),
        "ffn_w_out": nrm(ks[7], (DEPTH, D_FF, D), D_FF ** -0.5),
        "a_w_in": nrm(ks[8], (N_A_LAYERS, D, A_QKV), D ** -0.5),
        "a_w_out": nrm(ks[9], (N_A_LAYERS, A_OUT, D), A_OUT ** -0.5),
        "a_sink": nrm(ks[10], (N_A_LAYERS, A_Q_HEADS), 0.5),
        "b_w_in": nrm(ks[11], (N_B_LAYERS, D, B_QKV), D ** -0.5),
        "b_w_out": nrm(ks[12], (N_B_LAYERS, B_OUT, D), B_OUT ** -0.5),
        "final_norm": 1.0 + nrm(ks[13], (D,), 0.05),
    }


def _fwd_reference(x, c, ada_w, ada_b, norm_mix, norm_ffn, ffn_w_in, ffn_w_out,
              a_w_in, a_w_out, a_sink, b_w_in, b_w_out, final_norm):
    cond = jax.nn.silu(c)
    for i in range(DEPTH):
        mod = (cond @ ada_w[i] + ada_b[i])[:, None, :]
        sh1, sc1, g1, sh2, sc2, g2 = jnp.split(mod, 6, axis=-1)
        h = rmsnorm(x, norm_mix[i]) * (1 + sc1) + sh1
        j = i // N_MIXERS
        if i % N_MIXERS == 0:
            y = windowed_gqa(h, a_w_in[j], a_w_out[j], a_sink[j])
        else:
            y = dilated_attention(h, b_w_in[j], b_w_out[j])
        x = x + g1 * y
        h = rmsnorm(x, norm_ffn[i]) * (1 + sc2) + sh2
        x = x + g2 * swiglu(h, ffn_w_in[i], ffn_w_out[i])
    return rmsnorm(x, final_norm)


import jax as _jax
import jax.numpy as _jnp

TWIN_FORMAT = 'train_step'
FWD_PARAMS = ['x', 'c', 'ada_w', 'ada_b', 'norm_mix', 'norm_ffn', 'ffn_w_in', 'ffn_w_out', 'a_w_in', 'a_w_out', 'a_sink', 'b_w_in', 'b_w_out', 'final_norm']
TWIN_WEIGHTS = ['ada_w', 'ada_b', 'norm_mix', 'norm_ffn', 'ffn_w_in', 'ffn_w_out', 'a_w_in', 'a_w_out', 'a_sink', 'b_w_in', 'b_w_out', 'final_norm']
TWIN_DIFF_INPUT = 'x'
TWIN_INPUTS = ['x', 'c', 'ada_w', 'ada_b', 'norm_mix', 'norm_ffn', 'ffn_w_in', 'ffn_w_out', 'a_w_in', 'a_w_out', 'a_sink', 'b_w_in', 'b_w_out', 'final_norm', 'loss_target', 'm_ada_w', 'm_ada_b', 'm_norm_mix', 'm_norm_ffn', 'm_ffn_w_in', 'm_ffn_w_out', 'm_a_w_in', 'm_a_w_out', 'm_a_sink', 'm_b_w_in', 'm_b_w_out', 'm_final_norm', 'v_ada_w', 'v_ada_b', 'v_norm_mix', 'v_norm_ffn', 'v_ffn_w_in', 'v_ffn_w_out', 'v_a_w_in', 'v_a_w_out', 'v_a_sink', 'v_b_w_in', 'v_b_w_out', 'v_final_norm']
TWIN_OUTPUTS = ['loss', 'grad_x', 'grad_ada_w', 'grad_ada_b', 'grad_norm_mix', 'grad_norm_ffn', 'grad_ffn_w_in', 'grad_ffn_w_out', 'grad_a_w_in', 'grad_a_w_out', 'grad_a_sink', 'grad_b_w_in', 'grad_b_w_out', 'grad_final_norm', 'delta_ada_w', 'delta_ada_b', 'delta_norm_mix', 'delta_norm_ffn', 'delta_ffn_w_in', 'delta_ffn_w_out', 'delta_a_w_in', 'delta_a_w_out', 'delta_a_sink', 'delta_b_w_in', 'delta_b_w_out', 'delta_final_norm', 'new_m_ada_w', 'new_m_ada_b', 'new_m_norm_mix', 'new_m_norm_ffn', 'new_m_ffn_w_in', 'new_m_ffn_w_out', 'new_m_a_w_in', 'new_m_a_w_out', 'new_m_a_sink', 'new_m_b_w_in', 'new_m_b_w_out', 'new_m_final_norm', 'new_v_ada_w', 'new_v_ada_b', 'new_v_norm_mix', 'new_v_norm_ffn', 'new_v_ffn_w_in', 'new_v_ffn_w_out', 'new_v_a_w_in', 'new_v_a_w_out', 'new_v_a_sink', 'new_v_b_w_in', 'new_v_b_w_out', 'new_v_final_norm']
TWIN_LEAF_KINDS = {'loss': 'loss', 'grad_x': 'grad_x', 'grad_ada_w': 'grad_w', 'grad_ada_b': 'grad_w', 'grad_norm_mix': 'grad_w', 'grad_norm_ffn': 'grad_w', 'grad_ffn_w_in': 'grad_w', 'grad_ffn_w_out': 'grad_w', 'grad_a_w_in': 'grad_w', 'grad_a_w_out': 'grad_w', 'grad_a_sink': 'grad_w', 'grad_b_w_in': 'grad_w', 'grad_b_w_out': 'grad_w', 'grad_final_norm': 'grad_w', 'delta_ada_w': 'delta_w', 'delta_ada_b': 'delta_w', 'delta_norm_mix': 'delta_w', 'delta_norm_ffn': 'delta_w', 'delta_ffn_w_in': 'delta_w', 'delta_ffn_w_out': 'delta_w', 'delta_a_w_in': 'delta_w', 'delta_a_w_out': 'delta_w', 'delta_a_sink': 'delta_w', 'delta_b_w_in': 'delta_w', 'delta_b_w_out': 'delta_w', 'delta_final_norm': 'delta_w', 'new_m_ada_w': 'new_m', 'new_m_ada_b': 'new_m', 'new_m_norm_mix': 'new_m', 'new_m_norm_ffn': 'new_m', 'new_m_ffn_w_in': 'new_m', 'new_m_ffn_w_out': 'new_m', 'new_m_a_w_in': 'new_m', 'new_m_a_w_out': 'new_m', 'new_m_a_sink': 'new_m', 'new_m_b_w_in': 'new_m', 'new_m_b_w_out': 'new_m', 'new_m_final_norm': 'new_m', 'new_v_ada_w': 'new_v', 'new_v_ada_b': 'new_v', 'new_v_norm_mix': 'new_v', 'new_v_norm_ffn': 'new_v', 'new_v_ffn_w_in': 'new_v', 'new_v_ffn_w_out': 'new_v', 'new_v_a_w_in': 'new_v', 'new_v_a_w_out': 'new_v', 'new_v_a_sink': 'new_v', 'new_v_b_w_in': 'new_v', 'new_v_b_w_out': 'new_v', 'new_v_final_norm': 'new_v'}


def _forward(args):
    return _fwd_reference(*[args[k] for k in FWD_PARAMS])


def _output_shape():
    def fwd():
        inp = _fwd_setup_inputs(0)
        return _fwd_reference(*[inp[k] for k in FWD_PARAMS])
    out = _jax.eval_shape(fwd)
    return out.shape, out.dtype

N_MICROBATCH = 1
ADAM_LR = 0.001
ADAM_B1 = 0.9
ADAM_B2 = 0.999
ADAM_EPS = 1e-08
ADAM_WD = 0.01
ADAM_STEP = 10
PER_EXAMPLE_BATCH_AXIS = {'x': 0, 'c': 0, 'loss_target': 0}
SHARED_INPUTS = []
_WEIGHT_DTYPES = {'ada_w': _jnp.float32, 'ada_b': _jnp.float32, 'norm_mix': _jnp.float32, 'norm_ffn': _jnp.float32, 'ffn_w_in': _jnp.float32, 'ffn_w_out': _jnp.float32, 'a_w_in': _jnp.float32, 'a_w_out': _jnp.float32, 'a_sink': _jnp.float32, 'b_w_in': _jnp.float32, 'b_w_out': _jnp.float32, 'final_norm': _jnp.float32}
MOMENT_SCALE = {'ada_w': 1.190406e-01, 'ada_b': 2.518889e-01, 'norm_mix': 5.598853e-02, 'norm_ffn': 1.120181e-01, 'ffn_w_in': 4.958305e-02, 'ffn_w_out': 8.126633e-02, 'a_w_in': 6.002471e-02, 'a_w_out': 6.379596e-02, 'a_sink': 4.169011e-02, 'b_w_in': 4.297705e-02, 'b_w_out': 4.159735e-02, 'final_norm': 1.281568e+02}


def _to_microbatches(a, axis):
    t = _jnp.moveaxis(a, axis, 0)
    t = t.reshape((N_MICROBATCH, t.shape[0] // N_MICROBATCH) + t.shape[1:])
    return _jnp.moveaxis(t, 1, axis + 1)


def setup_inputs(seed: int = 0) -> dict:
    inp = _fwd_setup_inputs(seed)
    key = _jax.random.fold_in(_jax.random.key(seed), 7919)
    shape, _ = _output_shape()
    out = dict(inp)
    out["loss_target"] = _jax.random.normal(_jax.random.fold_in(key, 0), shape, _jnp.float32)
    for i, name in enumerate(TWIN_WEIGHTS):
        w = inp[name].astype(_jnp.float32)
        if MOMENT_SCALE is None:
            s = _jnp.sqrt(_jnp.mean(_jnp.square(w)) + 1e-30)
        else:
            s = MOMENT_SCALE[name]
        km, kv = _jax.random.split(_jax.random.fold_in(key, i + 1))
        out[name] = w
        out["m_" + name] = s * _jax.random.normal(km, w.shape, _jnp.float32)
        out["v_" + name] = (s * s) * _jax.random.uniform(kv, w.shape, _jnp.float32, 0.5, 1.5)
    if N_MICROBATCH > 1:
        for name, axis in PER_EXAMPLE_BATCH_AXIS.items():
            out[name] = _to_microbatches(out[name], axis)
    return {'x': out['x'], 'c': out['c'], 'ada_w': out['ada_w'], 'ada_b': out['ada_b'], 'norm_mix': out['norm_mix'], 'norm_ffn': out['norm_ffn'], 'ffn_w_in': out['ffn_w_in'], 'ffn_w_out': out['ffn_w_out'], 'a_w_in': out['a_w_in'], 'a_w_out': out['a_w_out'], 'a_sink': out['a_sink'], 'b_w_in': out['b_w_in'], 'b_w_out': out['b_w_out'], 'final_norm': out['final_norm'], 'loss_target': out['loss_target'], 'm_ada_w': out['m_ada_w'], 'm_ada_b': out['m_ada_b'], 'm_norm_mix': out['m_norm_mix'], 'm_norm_ffn': out['m_norm_ffn'], 'm_ffn_w_in': out['m_ffn_w_in'], 'm_ffn_w_out': out['m_ffn_w_out'], 'm_a_w_in': out['m_a_w_in'], 'm_a_w_out': out['m_a_w_out'], 'm_a_sink': out['m_a_sink'], 'm_b_w_in': out['m_b_w_in'], 'm_b_w_out': out['m_b_w_out'], 'm_final_norm': out['m_final_norm'], 'v_ada_w': out['v_ada_w'], 'v_ada_b': out['v_ada_b'], 'v_norm_mix': out['v_norm_mix'], 'v_norm_ffn': out['v_norm_ffn'], 'v_ffn_w_in': out['v_ffn_w_in'], 'v_ffn_w_out': out['v_ffn_w_out'], 'v_a_w_in': out['v_a_w_in'], 'v_a_w_out': out['v_a_w_out'], 'v_a_sink': out['v_a_sink'], 'v_b_w_in': out['v_b_w_in'], 'v_b_w_out': out['v_b_w_out'], 'v_final_norm': out['v_final_norm']}


def _loss(weights, diff, rest, loss_target):
    with _jax.named_scope("forward"):
        args = {**rest, TWIN_DIFF_INPUT: diff, **{k: w.astype(_WEIGHT_DTYPES[k]) for k, w in weights.items()}}
        y = _forward(args)
    with _jax.named_scope("loss_head"):
        err = _jnp.square(y.astype(_jnp.float32) - loss_target)
        return 0.5 * _jnp.sum(_jnp.mean(err, axis=-1)) if err.ndim else 0.5 * err


def _adamw(w, g, m, v):
    m = ADAM_B1 * m + (1.0 - ADAM_B1) * g
    v = ADAM_B2 * v + (1.0 - ADAM_B2) * _jnp.square(g)
    m_hat = m / (1.0 - ADAM_B1 ** ADAM_STEP)
    v_hat = v / (1.0 - ADAM_B2 ** ADAM_STEP)
    delta = -ADAM_LR * (m_hat / (_jnp.sqrt(v_hat) + ADAM_EPS) + ADAM_WD * w)
    return delta, m, v


def reference(x, c, ada_w, ada_b, norm_mix, norm_ffn, ffn_w_in, ffn_w_out, a_w_in, a_w_out, a_sink, b_w_in, b_w_out, final_norm, loss_target, m_ada_w, m_ada_b, m_norm_mix, m_norm_ffn, m_ffn_w_in, m_ffn_w_out, m_a_w_in, m_a_w_out, m_a_sink, m_b_w_in, m_b_w_out, m_final_norm, v_ada_w, v_ada_b, v_norm_mix, v_norm_ffn, v_ffn_w_in, v_ffn_w_out, v_a_w_in, v_a_w_out, v_a_sink, v_b_w_in, v_b_w_out, v_final_norm):
    given = dict(x=x, c=c, ada_w=ada_w, ada_b=ada_b, norm_mix=norm_mix, norm_ffn=norm_ffn, ffn_w_in=ffn_w_in, ffn_w_out=ffn_w_out, a_w_in=a_w_in, a_w_out=a_w_out, a_sink=a_sink, b_w_in=b_w_in, b_w_out=b_w_out, final_norm=final_norm, loss_target=loss_target, m_ada_w=m_ada_w, m_ada_b=m_ada_b, m_norm_mix=m_norm_mix, m_norm_ffn=m_norm_ffn, m_ffn_w_in=m_ffn_w_in, m_ffn_w_out=m_ffn_w_out, m_a_w_in=m_a_w_in, m_a_w_out=m_a_w_out, m_a_sink=m_a_sink, m_b_w_in=m_b_w_in, m_b_w_out=m_b_w_out, m_final_norm=m_final_norm, v_ada_w=v_ada_w, v_ada_b=v_ada_b, v_norm_mix=v_norm_mix, v_norm_ffn=v_norm_ffn, v_ffn_w_in=v_ffn_w_in, v_ffn_w_out=v_ffn_w_out, v_a_w_in=v_a_w_in, v_a_w_out=v_a_w_out, v_a_sink=v_a_sink, v_b_w_in=v_b_w_in, v_b_w_out=v_b_w_out, v_final_norm=v_final_norm)
    weights = {n: given[n] for n in TWIN_WEIGHTS}
    shared = {n: given[n] for n in SHARED_INPUTS}
    per_example = {n: given[n] for n in ['x', 'c']}
    grad_fn = _jax.value_and_grad(_loss, argnums=(0, 1))

    def one_microbatch(ex, loss_target):
        ex = dict(ex)
        diff = ex.pop(TWIN_DIFF_INPUT)
        return grad_fn(weights, diff, {**shared, **ex}, loss_target)

    if N_MICROBATCH == 1:
        loss, (grad_w, grad_x) = one_microbatch(per_example, given["loss_target"])
    else:
        def body(carry, xs):
            loss_sum, grad_sum = carry
            l_k, (gw_k, gx_k) = one_microbatch(xs[0], xs[1])
            with _jax.named_scope("update"):
                return (loss_sum + l_k, _jax.tree.map(_jnp.add, grad_sum, gw_k)), gx_k

        init = (_jnp.zeros((), _jnp.float32), _jax.tree.map(_jnp.zeros_like, weights))
        (loss, grad_w), grad_x = _jax.lax.scan(body, init, (per_example, given["loss_target"]))
    with _jax.named_scope("update"):
        delta_w, new_m, new_v = {}, {}, {}
        for n in TWIN_WEIGHTS:
            delta_w[n], new_m[n], new_v[n] = _adamw(weights[n], grad_w[n], given["m_" + n], given["v_" + n])
    return (loss, grad_x, *[grad_w[n] for n in TWIN_WEIGHTS], *[delta_w[n] for n in TWIN_WEIGHTS],
            *[new_m[n] for n in TWIN_WEIGHTS], *[new_v[n] for n in TWIN_WEIGHTS])
```

```python
import functools
import math

import numpy as np
import jax
import jax.numpy as jnp
from jax import lax
from jax.experimental import pallas as pl
from jax.experimental.pallas import tpu as pltpu

D = 1024
HEAD_DIM = 64
D_FF = 2816
DEPTH = 4
N_DEV = 8
A_QKV = 1536
B_QKV = 2304
A_HALF = 128
B_HALF = 64
B_DILS = (1, 4, 16)
RMS_EPS = 1e-6
NEG = -1e30
ADAM_LR = 0.001
ADAM_B1 = 0.9
ADAM_B2 = 0.999
ADAM_EPS = 1e-08
ADAM_WD = 0.01
ADAM_STEP = 10

LANES = 128
TQ = 128
VMEM_LIMIT = 56 * 1024 * 1024
MESH = pl.DeviceIdType.MESH
F32 = jnp.float32
BF16 = jnp.bfloat16

SEGMENTS = ([("ffn_in", l, 704) for l in range(4)] + [("ffn_out", l, 352) for l in range(4)]
            + [("a_in", j, 192) for j in range(2)] + [("a_out", j, 128) for j in range(2)]
            + [("b_in", j, 288) for j in range(2)] + [("b_out", j, 64) for j in range(2)])
SEG_ROWS = [s[2] for s in SEGMENTS]
SEG_OFF = [sum(SEG_ROWS[:i]) for i in range(len(SEGMENTS))]
SHARD_ROWS = sum(SEG_ROWS)
STAT_ROWS = 40


def _nn(a, b):
    return jnp.dot(a, b, preferred_element_type=F32)


def _nt(a, b):
    return lax.dot_general(a, b, (((1,), (1,)), ((), ())), preferred_element_type=F32)


def _tn(a, b):
    return lax.dot_general(a, b, (((0,), (0,)), ((), ())), preferred_element_type=F32)


def _params(dims=None, vmem=None):
    kw = {}
    if dims is not None:
        kw["dimension_semantics"] = dims
    if vmem is not None:
        kw["vmem_limit_bytes"] = vmem
    return pltpu.CompilerParams(**kw)


def _my_index():
    return 4 * lax.axis_index("x") + 2 * lax.axis_index("y") + lax.axis_index("c")


def _peer(k):
    x, y, c = lax.axis_index("x"), lax.axis_index("y"), lax.axis_index("c")
    px, py, pc = x ^ ((k >> 2) & 1), y ^ ((k >> 1) & 1), c ^ (k & 1)
    return (px, py, pc), 4 * px + 2 * py + pc


def _const_spec(shape):
    nd = len(shape)
    return pl.BlockSpec(shape, lambda *_: (0,) * nd)


def _cond_exchange(c_tile, ada_w, ada_b_mine):
    ncol = ada_w.shape[-1]

    def body(c_ref, w_ref, b_ref, cond_ref, parts_ref, call_ref, mine_ref, send_sems, recv_sems):
        me = _my_index()
        call_ref[me] = c_ref[...]
        copies = []
        for k in range(1, N_DEV):
            dev, _ = _peer(k)
            cp = pltpu.make_async_remote_copy(src_ref=c_ref, dst_ref=call_ref.at[me], send_sem=send_sems.at[0, k - 1],
                                              recv_sem=recv_sems.at[0, k - 1], device_id=dev, device_id_type=MESH)
            cp.start()
            copies.append(cp)
        for k in range(1, N_DEV):
            _, pidx = _peer(k)
            pltpu.make_async_remote_copy(src_ref=c_ref, dst_ref=call_ref.at[pidx], send_sem=send_sems.at[0, k - 1],
                                         recv_sem=recv_sems.at[0, k - 1], device_id=_peer(k)[0], device_id_type=MESH).wait_recv()
        for cp in copies:
            cp.wait_send()
        row = lax.broadcasted_iota(jnp.int32, (N_DEV, D), 0)
        cmat = jnp.zeros((N_DEV, D), F32)
        for j in range(N_DEV):
            cmat = jnp.where(row == j, call_ref[j], cmat)
        cond = cmat * jax.nn.sigmoid(cmat)
        cond_ref[...] = cond
        cb = cond.astype(BF16)
        for l in range(DEPTH):
            mine_ref[l] = _nn(cb, w_ref[l].astype(BF16)) + b_ref[pl.ds(l, 1), :]
        parts_ref[me] = mine_ref[...]
        copies = []
        for k in range(1, N_DEV):
            dev, _ = _peer(k)
            cp = pltpu.make_async_remote_copy(src_ref=mine_ref, dst_ref=parts_ref.at[me], send_sem=send_sems.at[1, k - 1],
                                              recv_sem=recv_sems.at[1, k - 1], device_id=dev, device_id_type=MESH)
            cp.start()
            copies.append(cp)
        for k in range(1, N_DEV):
            dev, pidx = _peer(k)
            pltpu.make_async_remote_copy(src_ref=mine_ref, dst_ref=parts_ref.at[pidx], send_sem=send_sems.at[1, k - 1],
                                         recv_sem=recv_sems.at[1, k - 1], device_id=dev, device_id_type=MESH).wait_recv()
        for cp in copies:
            cp.wait_send()

    vm = pl.BlockSpec(memory_space=pltpu.VMEM)
    return pl.pallas_call(
        body, name="cond_exchange",
        out_shape=(jax.ShapeDtypeStruct((N_DEV, D), F32), jax.ShapeDtypeStruct((N_DEV, DEPTH, N_DEV, ncol), F32)),
        in_specs=[vm, vm, vm], out_specs=(vm, vm),
        scratch_shapes=[pltpu.VMEM((N_DEV, N_DEV, D), F32), pltpu.VMEM((DEPTH, N_DEV, ncol), F32),
                        pltpu.SemaphoreType.DMA((2, N_DEV - 1)), pltpu.SemaphoreType.DMA((2, N_DEV - 1))],
        compiler_params=_params(vmem=VMEM_LIMIT),
    )(c_tile, ada_w, ada_b_mine)[:2]


def _all_gather_weights(shards):
    n = len(shards)
    big = max(range(n), key=lambda s: shards[s].shape[0])
    assert N_DEV * shards[big].shape[0] >= SHARD_ROWS

    def body(*refs):
        ins, outs = refs[:n], refs[n:2 * n]
        local_sems, send_sems, recv_sems = refs[2 * n:]
        me = _my_index()
        local = []
        for s in range(n):
            rows = ins[s].shape[0]
            cp = pltpu.make_async_copy(ins[s], outs[s].at[pl.ds(me * rows, rows)], local_sems.at[s])
            cp.start()
            local.append(cp)
        for k in range(1, N_DEV):
            dev, _ = _peer(k)
            for s in range(n):
                rows = ins[s].shape[0]
                pltpu.make_async_remote_copy(src_ref=ins[s], dst_ref=outs[s].at[pl.ds(me * rows, rows)],
                                             send_sem=send_sems.at[k - 1], recv_sem=recv_sems.at[k - 1],
                                             device_id=dev, device_id_type=MESH).start()
        whole = outs[big].at[pl.ds(0, SHARD_ROWS)]
        for k in range(1, N_DEV):
            dev, _ = _peer(k)
            w = pltpu.make_async_remote_copy(src_ref=whole, dst_ref=whole, send_sem=send_sems.at[k - 1],
                                             recv_sem=recv_sems.at[k - 1], device_id=dev, device_id_type=MESH)
            w.wait_send()
            w.wait_recv()
        for cp in local:
            cp.wait()

    hbm = pl.BlockSpec(memory_space=pl.ANY)
    return pl.pallas_call(
        body, name="weight_all_gather",
        out_shape=tuple(jax.ShapeDtypeStruct((N_DEV * s.shape[0], D), s.dtype) for s in shards),
        in_specs=[hbm] * n, out_specs=tuple([hbm] * n),
        scratch_shapes=[pltpu.SemaphoreType.DMA((n,)), pltpu.SemaphoreType.DMA((N_DEV - 1,)),
                        pltpu.SemaphoreType.DMA((N_DEV - 1,))],
    )(*shards)


def _grad_exchange(fulls):
    n = len(fulls)

    def body(*refs):
        ins, out = refs[:n], refs[n]
        local_sems, send_sems, recv_sems = refs[n + 1:]
        me = _my_index()
        local = []
        for s in range(n):
            rows = SEG_ROWS[s]
            cp = pltpu.make_async_copy(ins[s].at[pl.ds(me * rows, rows)], out.at[me, pl.ds(SEG_OFF[s], rows)], local_sems.at[s])
            cp.start()
            local.append(cp)
        for k in range(1, N_DEV):
            dev, pidx = _peer(k)
            for s in range(n):
                rows = SEG_ROWS[s]
                pltpu.make_async_remote_copy(src_ref=ins[s].at[pl.ds(pidx * rows, rows)],
                                             dst_ref=out.at[me, pl.ds(SEG_OFF[s], rows)],
                                             send_sem=send_sems.at[k - 1], recv_sem=recv_sems.at[k - 1],
                                             device_id=dev, device_id_type=MESH).start()
        for k in range(1, N_DEV):
            dev, pidx = _peer(k)
            w = pltpu.make_async_remote_copy(src_ref=out.at[me], dst_ref=out.at[pidx], send_sem=send_sems.at[k - 1],
                                             recv_sem=recv_sems.at[k - 1], device_id=dev, device_id_type=MESH)
            w.wait_send()
            w.wait_recv()
        for cp in local:
            cp.wait()

    hbm = pl.BlockSpec(memory_space=pl.ANY)
    return pl.pallas_call(
        body, name="grad_exchange",
        out_shape=jax.ShapeDtypeStruct((N_DEV, SHARD_ROWS, D), BF16),
        in_specs=[hbm] * n, out_specs=hbm,
        scratch_shapes=[pltpu.SemaphoreType.DMA((n,)), pltpu.SemaphoreType.DMA((N_DEV - 1,)),
                        pltpu.SemaphoreType.DMA((N_DEV - 1,))],
    )(*fulls)


def _norm_mod(x, nw, sc, sh):
    ms = jnp.mean(x * x, axis=-1, keepdims=True)
    xh = x * lax.rsqrt(ms + RMS_EPS)
    return xh, (xh * nw) * (1.0 + sc) + sh


def _proj(x, nw, sc, sh, wt, *, ffn, name):
    S, N = x.shape[0], wt.shape[0]
    tm = 256 if ffn else 512

    def body(x_ref, nw_ref, sc_ref, sh_ref, w_ref, h_ref, out_ref, *act_ref):
        _, h = _norm_mod(x_ref[...], nw_ref[...], sc_ref[...], sh_ref[...])
        hb = h.astype(BF16)
        h_ref[...] = hb
        if ffn:
            gate = _nt(hb, w_ref[pl.ds(0, D_FF), :])
            up = _nt(hb, w_ref[pl.ds(D_FF, D_FF), :])
            out_ref[:, pl.ds(0, D_FF)] = gate.astype(BF16)
            out_ref[:, pl.ds(D_FF, D_FF)] = up.astype(BF16)
            act_ref[0][...] = ((gate * jax.nn.sigmoid(gate)) * up).astype(BF16)
        else:
            out_ref[...] = _nt(hb, w_ref[...]).astype(BF16)

    row = lambda w: pl.BlockSpec((tm, w), lambda i: (i, 0))
    out_shape = [jax.ShapeDtypeStruct((S, D), BF16), jax.ShapeDtypeStruct((S, N), BF16)]
    out_specs = [row(D), row(N)]
    if ffn:
        out_shape.append(jax.ShapeDtypeStruct((S, D_FF), BF16))
        out_specs.append(row(D_FF))
    vec = _const_spec((1, D))
    return pl.pallas_call(
        body, name=name, grid=(S // tm,), out_shape=tuple(out_shape),
        in_specs=[row(D), vec, vec, vec, _const_spec((N, D))], out_specs=tuple(out_specs),
        compiler_params=_params(("parallel",), VMEM_LIMIT),
    )(x, nw, sc, sh, wt)


def _gated_residual(a, w, x, g, *, w_is_transposed, name):
    S, K = a.shape
    tm = 512

    def body(a_ref, w_ref, x_ref, g_ref, xo_ref, y_ref):
        y = _nt(a_ref[...], w_ref[...]) if w_is_transposed else _nn(a_ref[...], w_ref[...])
        y_ref[...] = y.astype(BF16)
        xo_ref[...] = x_ref[...] + g_ref[...] * y

    row = lambda w_: pl.BlockSpec((tm, w_), lambda i: (i, 0))
    return pl.pallas_call(
        body, name=name, grid=(S // tm,),
        out_shape=(jax.ShapeDtypeStruct((S, D), F32), jax.ShapeDtypeStruct((S, D), BF16)),
        in_specs=[row(K), _const_spec(w.shape), row(D), _const_spec((1, D))], out_specs=(row(D), row(D)),
        compiler_params=_params(("parallel",), VMEM_LIMIT),
    )(a, w, x, g)


def _alibi_bias(slopes, half, dil):
    tk = TQ + 2 * half
    rel = np.arange(tk)[None, :] - half - np.arange(TQ)[:, None]
    band = np.abs(rel) <= half
    dist = (dil * np.abs(rel)).astype(np.float32)
    tabs = [np.where(band, -np.float32(s) * dist, np.float32(NEG)).astype(np.float32) for s in slopes]
    return jnp.asarray(np.concatenate(tabs, axis=0))


def _slopes(n):
    return (2.0 ** (-8.0 * np.arange(1, n + 1) / n)).astype(np.float32)


def _attn_specs(C, r, half, qoff, koff, voff, L):
    cb = C // LANES
    per = TQ // half
    nkb = per + 2
    nblk = L // half

    def qmap(t):
        return lambda u, rho, i: (i, rho * cb + qoff // LANES + 4 * u + t)

    def kmap(j, off):
        return lambda u, rho, i: (jnp.clip(i * per - 1 + j, 0, nblk - 1), rho * cb + off // LANES + u)

    specs = [pl.BlockSpec((TQ, LANES), qmap(t)) for t in range(4)]
    specs += [pl.BlockSpec((half, LANES), kmap(j, koff)) for j in range(nkb)]
    specs += [pl.BlockSpec((half, LANES), kmap(j, voff)) for j in range(nkb)]
    return specs, nkb


def _head_masks():
    lane = lax.broadcasted_iota(jnp.int32, (TQ, LANES), 1)
    lo = lane < HEAD_DIM
    return lo, jnp.logical_not(lo)


def _key_valid(i, half, nblk):
    tk = TQ + 2 * half
    kidx = lax.broadcasted_iota(jnp.int32, (1, tk), 1)
    kb = i * (TQ // half) - 1 + jnp.right_shift(kidx, int(math.log2(half)))
    return jnp.logical_and(kb >= 0, kb < nblk)


def _attn_fwd(qkv, bias, sink, *, C, r, half, qoff, koff, voff, n_units, out_dtype, name):
    S = qkv.shape[0]
    L = S // r
    nq, nblk = L // TQ, L // half
    specs, nkb = _attn_specs(C, r, half, qoff, koff, voff, L)
    tk = nkb * half
    use_sink = sink is not None

    def body(*refs):
        q_refs, k_refs, v_refs = refs[:4], refs[4:4 + nkb], refs[4 + nkb:4 + 2 * nkb]
        rest = refs[4 + 2 * nkb:]
        bias_ref = rest[0]
        sink_ref = rest[1] if use_sink else None
        o_ref, lse_ref = rest[-2], rest[-1]
        i = pl.program_id(2)
        k2 = jnp.concatenate([kr[...] for kr in k_refs], axis=0)
        v2 = jnp.concatenate([vr[...] for vr in v_refs], axis=0)
        lo, hi = _head_masks()
        kvalid = _key_valid(i, half, nblk)
        for t in range(4):
            qf = q_refs[t][...].astype(F32) * (HEAD_DIM ** -0.5)
            b = t // 2
            outs, lses = [], []
            for a in range(2):
                h = 2 * t + a
                qm = jnp.where(lo if a == 0 else hi, qf, 0.0)
                if a != b:
                    qm = pltpu.roll(qm, HEAD_DIM, 1)
                s = _nt(qm.astype(BF16), k2) + bias_ref[pl.ds(h * TQ, TQ), :]
                s = jnp.where(kvalid, s, NEG)
                m = jnp.max(s, axis=-1, keepdims=True)
                if use_sink:
                    sk = jnp.max(sink_ref[pl.ds(h, 1), :], axis=-1, keepdims=True)
                    m = jnp.maximum(m, sk)
                p = jnp.exp(s - m)
                l = jnp.sum(p, axis=-1, keepdims=True)
                if use_sink:
                    l = l + jnp.exp(sk - m)
                pv = _nn(p.astype(BF16), v2)
                if a != b:
                    pv = pltpu.roll(pv, HEAD_DIM, 1)
                outs.append(pv / l)
                lses.append(jnp.broadcast_to(m + jnp.log(l), (TQ, LANES)))
            o_ref[:, pl.ds(t * LANES, LANES)] = jnp.where(lo, outs[0], outs[1]).astype(out_dtype)
            lse_ref[:, pl.ds(t * LANES, LANES)] = jnp.where(lo, lses[0], lses[1])

    in_specs = specs + [pl.BlockSpec((8 * TQ, tk), lambda u, rho, i: (u, 0))]
    args = [qkv.reshape(L, r * C)] * (4 + 2 * nkb) + [bias]
    if use_sink:
        in_specs.append(pl.BlockSpec((8, LANES), lambda u, rho, i: (u, 0)))
        args.append(sink)
    wide = pl.BlockSpec((TQ, 4 * LANES), lambda u, rho, i: (i, rho * n_units + u))
    o, lse = pl.pallas_call(
        body, name=name, grid=(n_units, r, nq),
        out_shape=(jax.ShapeDtypeStruct((L, r * n_units * 512), out_dtype), jax.ShapeDtypeStruct((L, r * n_units * 512), F32)),
        in_specs=in_specs, out_specs=(wide, wide),
        compiler_params=_params(("parallel", "parallel", "parallel"), VMEM_LIMIT),
    )(*args)
    return o.reshape(S, n_units * 512), lse.reshape(S, n_units * 512)


def _attn_bwd(qkv, bias, sink, o, do, lse, *, C, r, half, qoff, koff, voff, n_units, name):
    S = qkv.shape[0]
    L = S // r
    nq, nblk = L // TQ, L // half
    specs, nkb = _attn_specs(C, r, half, qoff, koff, voff, L)
    tk = nkb * half
    use_sink = sink is not None

    def body(*refs):
        q_refs, k_refs, v_refs = refs[:4], refs[4:4 + nkb], refs[4 + nkb:4 + 2 * nkb]
        rest = list(refs[4 + 2 * nkb:])
        bias_ref = rest.pop(0)
        sink_ref = rest.pop(0) if use_sink else None
        o_ref, do_ref, lse_ref = rest[:3]
        rest = rest[3:]
        dq_ref, dk_ref, dv_ref = rest[:3]
        rest = rest[3:]
        dsink_ref = rest.pop(0) if use_sink else None
        acck, accv = rest
        i = pl.program_id(2)

        @pl.when(i == 0)
        def _():
            acck[...] = jnp.zeros_like(acck)
            accv[...] = jnp.zeros_like(accv)

        if use_sink:
            @pl.when(jnp.logical_and(i == 0, pl.program_id(1) == 0))
            def _():
                dsink_ref[...] = jnp.zeros_like(dsink_ref)

        k2 = jnp.concatenate([kr[...] for kr in k_refs], axis=0)
        v2 = jnp.concatenate([vr[...] for vr in v_refs], axis=0)
        lo, hi = _head_masks()
        kvalid = _key_valid(i, half, nblk)
        dk_t = jnp.zeros((tk, LANES), F32)
        dv_t = jnp.zeros((tk, LANES), F32)
        for t in range(4):
            cols = pl.ds(t * LANES, LANES)
            qf = q_refs[t][...].astype(F32) * (HEAD_DIM ** -0.5)
            do2 = do_ref[:, cols].astype(F32)
            o2 = o_ref[:, cols].astype(F32)
            lse2 = lse_ref[:, cols]
            b = t // 2
            dq2 = jnp.zeros((TQ, LANES), F32)
            for a in range(2):
                h = 2 * t + a
                sel = lo if a == 0 else hi
                qm = jnp.where(sel, qf, 0.0)
                dom = jnp.where(sel, do2, 0.0)
                delta = jnp.sum(dom * o2, axis=-1, keepdims=True)
                lse_a = jnp.max(jnp.where(sel, lse2, -jnp.inf), axis=-1, keepdims=True)
                if a != b:
                    qm = pltpu.roll(qm, HEAD_DIM, 1)
                    dom = pltpu.roll(dom, HEAD_DIM, 1)
                qb, dob = qm.astype(BF16), dom.astype(BF16)
                s = _nt(qb, k2) + bias_ref[pl.ds(h * TQ, TQ), :]
                s = jnp.where(kvalid, s, NEG)
                p = jnp.exp(s - lse_a)
                dp = _nt(dob, v2)
                dsb = (p * (dp - delta)).astype(BF16)
                dqh = jnp.where(lo if b == 0 else hi, _nn(dsb, k2), 0.0)
                if a != b:
                    dqh = pltpu.roll(dqh, HEAD_DIM, 1)
                dq2 = dq2 + dqh
                dk_t = dk_t + _tn(dsb, qb)
                dv_t = dv_t + _tn(p.astype(BF16), dob)
                if use_sink:
                    sk = jnp.max(sink_ref[pl.ds(h, 1), :], axis=-1, keepdims=True)
                    part = -jnp.sum(jnp.exp(sk - lse_a) * delta, axis=0, keepdims=True)
                    dsink_ref[pl.ds(h, 1), :] = dsink_ref[pl.ds(h, 1), :] + part
            dq_ref[:, cols] = (dq2 * (HEAD_DIM ** -0.5)).astype(BF16)
        start = pl.multiple_of(i * TQ, TQ)
        acck[pl.ds(start, tk), :] = acck[pl.ds(start, tk), :] + dk_t
        accv[pl.ds(start, tk), :] = accv[pl.ds(start, tk), :] + dv_t

        @pl.when(i == nq - 1)
        def _():
            dk_ref[...] = acck[pl.ds(half, L), :].astype(BF16)
            dv_ref[...] = accv[pl.ds(half, L), :].astype(BF16)

    wide = pl.BlockSpec((TQ, 4 * LANES), lambda u, rho, i: (i, rho * n_units + u))
    tall = pl.BlockSpec((L, LANES), lambda u, rho, i: (0, rho * n_units + u))
    in_specs = specs + [pl.BlockSpec((8 * TQ, tk), lambda u, rho, i: (u, 0))]
    args = [qkv.reshape(L, r * C)] * (4 + 2 * nkb) + [bias]
    if use_sink:
        in_specs.append(pl.BlockSpec((8, LANES), lambda u, rho, i: (u, 0)))
        args.append(sink)
    in_specs += [wide, wide, wide]
    w = r * n_units * 512
    args += [o.reshape(L, w), do.reshape(L, w), lse.reshape(L, w)]
    out_shape = [jax.ShapeDtypeStruct((L, w), BF16), jax.ShapeDtypeStruct((L, w // 4), BF16), jax.ShapeDtypeStruct((L, w // 4), BF16)]
    out_specs = [wide, tall, tall]
    if use_sink:
        out_shape.append(jax.ShapeDtypeStruct((n_units * 8, LANES), F32))
        out_specs.append(pl.BlockSpec((8, LANES), lambda u, rho, i: (u, 0)))
    res = pl.pallas_call(
        body, name=name, grid=(n_units, r, nq), out_shape=tuple(out_shape), in_specs=in_specs, out_specs=tuple(out_specs),
        scratch_shapes=[pltpu.VMEM((L + 2 * half, LANES), F32), pltpu.VMEM((L + 2 * half, LANES), F32)],
        compiler_params=_params(("arbitrary", "arbitrary", "arbitrary"), VMEM_LIMIT),
    )(*args)
    dq, dk, dv = res[0].reshape(S, n_units * 512), res[1].reshape(S, n_units * 128), res[2].reshape(S, n_units * 128)
    return dq, dk, dv, (res[3] if use_sink else None)


def _merge_groups(os_, lses):
    S, W = os_[0].shape
    tm = 512

    def body(o0, o1, o2, l0, l1, l2, o_ref, lse_ref):
        ls = [l0[...], l1[...], l2[...]]
        mx = jnp.maximum(jnp.maximum(ls[0], ls[1]), ls[2])
        es = [jnp.exp(l - mx) for l in ls]
        den = es[0] + es[1] + es[2]
        o = (es[0] / den) * o0[...] + (es[1] / den) * o1[...] + (es[2] / den) * o2[...]
        o_ref[...] = o.astype(BF16)
        lse_ref[...] = mx + jnp.log(den)

    row = pl.BlockSpec((tm, W), lambda i: (i, 0))
    return pl.pallas_call(
        body, name="merge_groups", grid=(S // tm,),
        out_shape=(jax.ShapeDtypeStruct((S, W), BF16), jax.ShapeDtypeStruct((S, W), F32)),
        in_specs=[row] * 6, out_specs=(row, row), compiler_params=_params(("parallel",), VMEM_LIMIT),
    )(*os_, *lses)


def _loss_head(x, target, fnw):
    S = x.shape[0]
    tm = 512

    def body(x_ref, t_ref, w_ref, dx_ref, st_ref):
        @pl.when(pl.program_id(0) == 0)
        def _():
            st_ref[...] = jnp.zeros_like(st_ref)

        xv = x_ref[...]
        rstd = lax.rsqrt(jnp.mean(xv * xv, axis=-1, keepdims=True) + RMS_EPS)
        xh = xv * rstd
        err = xh * w_ref[...] - t_ref[...]
        dy = err * (1.0 / D)
        dxh = dy * w_ref[...]
        dx_ref[...] = rstd * (dxh - xh * jnp.mean(dxh * xh, axis=-1, keepdims=True))
        st_ref[pl.ds(0, 1), :] = st_ref[pl.ds(0, 1), :] + jnp.sum(dy * xh, axis=0, keepdims=True)
        st_ref[pl.ds(1, 1), :] = st_ref[pl.ds(1, 1), :] + jnp.sum(err * err, axis=0, keepdims=True)

    row = pl.BlockSpec((tm, D), lambda i: (i, 0))
    return pl.pallas_call(
        body, name="loss_head", grid=(S // tm,),
        out_shape=(jax.ShapeDtypeStruct((S, D), F32), jax.ShapeDtypeStruct((8, D), F32)),
        in_specs=[row, row, _const_spec((1, D))], out_specs=(row, _const_spec((8, D))),
        compiler_params=_params(("arbitrary",), VMEM_LIMIT),
    )(x, target, fnw)


def _gate_bwd(dx, y, g, w, gu, *, w_is_transposed, name):
    S = dx.shape[0]
    K = w.shape[1] if w_is_transposed else w.shape[0]
    ffn = gu is not None
    tm = 256 if ffn else 512
    wout = 2 * K if ffn else K

    def body(dx_ref, y_ref, g_ref, w_ref, *rest):
        if ffn:
            gu_ref, da_ref, dyb_ref, st_ref = rest
        else:
            da_ref, dyb_ref, st_ref = rest

        @pl.when(pl.program_id(0) == 0)
        def _():
            st_ref[...] = jnp.zeros_like(st_ref)

        dxv = dx_ref[...]
        st_ref[pl.ds(0, 1), :] = st_ref[pl.ds(0, 1), :] + jnp.sum(dxv * y_ref[...].astype(F32), axis=0, keepdims=True)
        dyb = (dxv * g_ref[...]).astype(BF16)
        dyb_ref[...] = dyb
        da = _nn(dyb, w_ref[...]) if w_is_transposed else _nt(dyb, w_ref[...])
        if ffn:
            gate = gu_ref[:, pl.ds(0, K)].astype(F32)
            up = gu_ref[:, pl.ds(K, K)].astype(F32)
            sig = jax.nn.sigmoid(gate)
            da_ref[:, pl.ds(0, K)] = (da * up * (sig * (1.0 + gate * (1.0 - sig)))).astype(BF16)
            da_ref[:, pl.ds(K, K)] = (da * (gate * sig)).astype(BF16)
        else:
            da_ref[...] = da.astype(BF16)

    row = lambda w_: pl.BlockSpec((tm, w_), lambda i: (i, 0))
    in_specs = [row(D), row(D), _const_spec((1, D)), _const_spec(w.shape)]
    args = [dx, y, g, w]
    if ffn:
        in_specs.append(row(wout))
        args.append(gu)
    return pl.pallas_call(
        body, name=name, grid=(S // tm,),
        out_shape=(jax.ShapeDtypeStruct((S, wout), BF16), jax.ShapeDtypeStruct((S, D), BF16), jax.ShapeDtypeStruct((8, D), F32)),
        in_specs=in_specs, out_specs=(row(wout), row(D), _const_spec((8, D))),
        compiler_params=_params(("arbitrary",), VMEM_LIMIT),
    )(*args)


def _norm_bwd(dy, wt, x, dres, nw, sc, *, name):
    S, N = dy.shape
    tm = 256 if N > 4096 else 512

    def body(dy_ref, w_ref, x_ref, dres_ref, nw_ref, sc_ref, dx_ref, st_ref):
        @pl.when(pl.program_id(0) == 0)
        def _():
            st_ref[...] = jnp.zeros_like(st_ref)

        dh = _nn(dy_ref[...], w_ref[...])
        xv = x_ref[...]
        rstd = lax.rsqrt(jnp.mean(xv * xv, axis=-1, keepdims=True) + RMS_EPS)
        xh = xv * rstd
        nwv, scale = nw_ref[...], 1.0 + sc_ref[...]
        dxh = dh * (nwv * scale)
        dx_ref[...] = dres_ref[...] + rstd * (dxh - xh * jnp.mean(dxh * xh, axis=-1, keepdims=True))
        dhx = dh * xh
        st_ref[pl.ds(0, 1), :] = st_ref[pl.ds(0, 1), :] + jnp.sum(dh, axis=0, keepdims=True)
        st_ref[pl.ds(1, 1), :] = st_ref[pl.ds(1, 1), :] + jnp.sum(dhx * nwv, axis=0, keepdims=True)
        st_ref[pl.ds(2, 1), :] = st_ref[pl.ds(2, 1), :] + jnp.sum(dhx * scale, axis=0, keepdims=True)

    row = lambda w_: pl.BlockSpec((tm, w_), lambda i: (i, 0))
    vec = _const_spec((1, D))
    return pl.pallas_call(
        body, name=name, grid=(S // tm,),
        out_shape=(jax.ShapeDtypeStruct((S, D), F32), jax.ShapeDtypeStruct((8, D), F32)),
        in_specs=[row(N), _const_spec((N, D)), row(D), row(D), vec, vec], out_specs=(row(D), _const_spec((8, D))),
        compiler_params=_params(("arbitrary",), VMEM_LIMIT),
    )(dy, wt, x, dres, nw, sc)


def _weight_grad(a, b, *, transpose_out, name):
    S, N = b.shape
    nb = N // 2 if N > 4096 else N
    tk = 512

    def body(a_ref, b_ref, out_ref, acc):
        k = pl.program_id(1)

        @pl.when(k == 0)
        def _():
            acc[...] = jnp.zeros_like(acc)

        acc[...] += _tn(a_ref[...], b_ref[...])

        @pl.when(k == pl.num_programs(1) - 1)
        def _():
            out_ref[...] = (acc[...].T if transpose_out else acc[...]).astype(BF16)

    out_block = pl.BlockSpec((nb, D), lambda n, k: (n, 0)) if transpose_out else pl.BlockSpec((D, nb), lambda n, k: (0, n))
    return pl.pallas_call(
        body, name=name, grid=(N // nb, S // tk),
        out_shape=jax.ShapeDtypeStruct((N, D) if transpose_out else (D, N), BF16),
        in_specs=[pl.BlockSpec((tk, D), lambda n, k: (k, 0)), pl.BlockSpec((tk, nb), lambda n, k: (k, n))],
        out_specs=out_block, scratch_shapes=[pltpu.VMEM((D, nb), F32)],
        compiler_params=_params(("parallel", "arbitrary"), VMEM_LIMIT),
    )(a, b)


def _adamw(w, g, m, v):
    m = ADAM_B1 * m + (1.0 - ADAM_B1) * g
    v = ADAM_B2 * v + (1.0 - ADAM_B2) * (g * g)
    m_hat = m / (1.0 - ADAM_B1 ** ADAM_STEP)
    v_hat = v / (1.0 - ADAM_B2 ** ADAM_STEP)
    delta = -ADAM_LR * (m_hat / (jnp.sqrt(v_hat) + ADAM_EPS) + ADAM_WD * w)
    return delta, m, v


def _adam_shard(parts, w, m, v):
    R = w.shape[0]
    tr = 192

    def body(p_ref, w_ref, m_ref, v_ref, g_out, d_out, m_out, v_out):
        g = p_ref[0].astype(F32)
        for j in range(1, N_DEV):
            g = g + p_ref[j].astype(F32)
        delta, mn, vn = _adamw(w_ref[...], g, m_ref[...], v_ref[...])
        g_out[...] = g
        d_out[...] = delta
        m_out[...] = mn
        v_out[...] = vn

    row = pl.BlockSpec((tr, D), lambda i: (i, 0))
    shp = jax.ShapeDtypeStruct((R, D), F32)
    return pl.pallas_call(
        body, name="adam_shard", grid=(R // tr,), out_shape=(shp,) * 4,
        in_specs=[pl.BlockSpec((N_DEV, tr, D), lambda i: (0, i, 0)), row, row, row], out_specs=(row,) * 4,
        compiler_params=_params(("parallel",), VMEM_LIMIT),
    )(parts, w, m, v)


def _adam_ada_w(cond_t, dmod, w, m, v):
    ncol = w.shape[-1]
    tr = 512

    def body(c_ref, d_ref, w_ref, m_ref, v_ref, g_out, d_out, m_out, v_out):
        g = _nn(c_ref[...], d_ref[0])
        delta, mn, vn = _adamw(w_ref[0], g, m_ref[0], v_ref[0])
        g_out[0] = g
        d_out[0] = delta
        m_out[0] = mn
        v_out[0] = vn

    blk = pl.BlockSpec((1, tr, ncol), lambda l, i: (l, i, 0))
    shp = jax.ShapeDtypeStruct(w.shape, F32)
    return pl.pallas_call(
        body, name="adam_ada_w", grid=(DEPTH, D // tr), out_shape=(shp,) * 4,
        in_specs=[pl.BlockSpec((tr, LANES), lambda l, i: (i, 0)), pl.BlockSpec((1, LANES, ncol), lambda l, i: (l, 0, 0)), blk, blk, blk],
        out_specs=(blk,) * 4, compiler_params=_params(("parallel", "parallel"), VMEM_LIMIT),
    )(cond_t, dmod, w, m, v)


def _small_exchange(stats, w, m, v):
    loss_row = 33

    def body(s_ref, w_ref, m_ref, v_ref, all_ref, g_out, d_out, m_out, v_out, loss_out, send_sems, recv_sems):
        me = _my_index()
        all_ref[me] = s_ref[...]
        copies = []
        for k in range(1, N_DEV):
            dev, _ = _peer(k)
            cp = pltpu.make_async_remote_copy(src_ref=s_ref, dst_ref=all_ref.at[me], send_sem=send_sems.at[k - 1],
                                              recv_sem=recv_sems.at[k - 1], device_id=dev, device_id_type=MESH)
            cp.start()
            copies.append(cp)
        for k in range(1, N_DEV):
            dev, pidx = _peer(k)
            pltpu.make_async_remote_copy(src_ref=s_ref, dst_ref=all_ref.at[pidx], send_sem=send_sems.at[k - 1],
                                         recv_sem=recv_sems.at[k - 1], device_id=dev, device_id_type=MESH).wait_recv()
        for cp in copies:
            cp.wait_send()
        g = all_ref[0]
        for j in range(1, N_DEV):
            g = g + all_ref[j]
        delta, mn, vn = _adamw(w_ref[...], g, m_ref[...], v_ref[...])
        g_out[...] = g
        d_out[...] = delta
        m_out[...] = mn
        v_out[...] = vn
        tot = jnp.sum(g[loss_row:loss_row + 1, :], axis=-1, keepdims=True) * (0.5 / D)
        loss_out[...] = jnp.broadcast_to(tot, loss_out.shape)

    vm = pl.BlockSpec(memory_space=pltpu.VMEM)
    shp = jax.ShapeDtypeStruct((STAT_ROWS, D), F32)
    return pl.pallas_call(
        body, name="small_exchange",
        out_shape=(jax.ShapeDtypeStruct((N_DEV, STAT_ROWS, D), F32), shp, shp, shp, shp, jax.ShapeDtypeStruct((8, LANES), F32)),
        in_specs=[vm] * 4, out_specs=(vm,) * 6,
        scratch_shapes=[pltpu.SemaphoreType.DMA((N_DEV - 1,)), pltpu.SemaphoreType.DMA((N_DEV - 1,))],
        compiler_params=_params(vmem=VMEM_LIMIT),
    )(stats, w, m, v)


def _to_rows(name, a):
    if name in ("ffn_in", "a_in", "b_in"):
        return a.T
    if name == "b_out":
        return a.T.reshape(-1, D)
    return a


def _from_rows(name, a):
    if name in ("ffn_in", "a_in", "b_in"):
        return a.T
    if name == "b_out":
        return a.reshape(-1, 512).T
    return a


def _pack_small(ada_b, norm_mix, norm_ffn, final_norm, sink):
    pad = jnp.zeros((1, D), F32)
    sink_row = jnp.pad(sink.reshape(1, -1), ((0, 0), (0, D - sink.size)))
    rows = [ada_b.reshape(24, D), norm_mix, norm_ffn, final_norm.reshape(1, D), pad, sink_row]
    rows.append(jnp.zeros((STAT_ROWS - 35, D), F32))
    return jnp.concatenate(rows, axis=0)


def _unpack_small(a):
    return a[0:24].reshape(4, 6 * D), a[24:28], a[28:32], a[32], a[34, :32].reshape(2, 16)


def kernel(x, c, ada_w, ada_b, norm_mix, norm_ffn, ffn_w_in, ffn_w_out, a_w_in, a_w_out, a_sink, b_w_in, b_w_out, final_norm, loss_target, m_ada_w, m_ada_b, m_norm_mix, m_norm_ffn, m_ffn_w_in, m_ffn_w_out, m_a_w_in, m_a_w_out, m_a_sink, m_b_w_in, m_b_w_out, m_final_norm, v_ada_w, v_ada_b, v_norm_mix, v_norm_ffn, v_ffn_w_in, v_ffn_w_out, v_a_w_in, v_a_w_out, v_a_sink, v_b_w_in, v_b_w_out, v_final_norm):
    S = x.shape[1]
    x0 = x.reshape(S, D)
    target = loss_target.reshape(S, D)
    me = _my_index()
    ncol = ada_w.shape[-1]

    ada_b_mine = lax.dynamic_slice_in_dim(ada_b, me * ncol, ncol, axis=1)
    cond_all, parts = _cond_exchange(jnp.broadcast_to(c.reshape(1, D), (8, D)), ada_w, ada_b_mine)
    mod = lax.dynamic_index_in_dim(parts, me, axis=2, keepdims=False)
    mod = jnp.transpose(mod, (1, 0, 2)).reshape(DEPTH, 6, 1, D)

    weights = {"ffn_in": ffn_w_in, "ffn_out": ffn_w_out, "a_in": a_w_in, "a_out": a_w_out, "b_in": b_w_in, "b_out": b_w_out}
    shards = [_to_rows(n, weights[n][l]).astype(BF16) for n, l, _ in SEGMENTS]
    gathered = _all_gather_weights(shards)
    W = {(n, l): g for (n, l, _), g in zip(SEGMENTS, gathered)}
    for j in range(2):
        W[("b_out", j)] = W[("b_out", j)].reshape(D, 512)

    a_slopes, b_slopes = _slopes(16), _slopes(24)
    bias_a = _alibi_bias(a_slopes, A_HALF, 1)
    bias_b = [_alibi_bias(b_slopes[8 * g:8 * g + 8], B_HALF, dil) for g, dil in enumerate(B_DILS)]
    a_geom = dict(C=A_QKV, r=1, half=A_HALF, qoff=0, koff=1024, voff=1280, n_units=2)
    b_geom = [dict(C=B_QKV, r=dil, half=B_HALF, qoff=512 * g, koff=1536 + 128 * g, voff=1920 + 128 * g, n_units=1)
              for g, dil in enumerate(B_DILS)]

    saved = []
    xcur = x0
    for i in range(DEPTH):
        j = i // 2
        sh1, sc1, g1, sh2, sc2, g2 = [mod[i, q] for q in range(6)]
        nm, nf = norm_mix[i].reshape(1, D), norm_ffn[i].reshape(1, D)
        if i % 2 == 0:
            sink_rep = jnp.broadcast_to(a_sink[j].reshape(16, 1), (16, LANES))
            h1, qkv = _proj(xcur, nm, sc1, sh1, W[("a_in", j)], ffn=False, name="proj_a")
            o, lse = _attn_fwd(qkv, bias_a, sink_rep, out_dtype=BF16, name="attn_a_fwd", **a_geom)
            x1, y1 = _gated_residual(o, W[("a_out", j)], xcur, g1, w_is_transposed=False, name="out_a")
        else:
            sink_rep = None
            h1, qkv = _proj(xcur, nm, sc1, sh1, W[("b_in", j)], ffn=False, name="proj_b")
            outs = [_attn_fwd(qkv, bias_b[g], None, out_dtype=F32, name="attn_b%d_fwd" % g, **b_geom[g]) for g in range(3)]
            o, lse = _merge_groups([t[0] for t in outs], [t[1] for t in outs])
            x1, y1 = _gated_residual(o, W[("b_out", j)], xcur, g1, w_is_transposed=True, name="out_b")
        h2, gu, act = _proj(x1, nf, sc2, sh2, W[("ffn_in", i)], ffn=True, name="ffn_in")
        x2, y2 = _gated_residual(act, W[("ffn_out", i)], x1, g2, w_is_transposed=False, name="ffn_out")
        saved.append(dict(x0=xcur, h1=h1, qkv=qkv, o=o, lse=lse, y1=y1, x1=x1, h2=h2, gu=gu, act=act, y2=y2, sink=sink_rep))
        xcur = x2

    dx, head_stats = _loss_head(xcur, target, final_norm.reshape(1, D))

    dW = {}
    dmod_rows = [None] * DEPTH
    dnm, dnf, dsink = [None] * DEPTH, [None] * DEPTH, [None] * 2
    for i in reversed(range(DEPTH)):
        j = i // 2
        sv = saved[i]
        sh1, sc1, g1, sh2, sc2, g2 = [mod[i, q] for q in range(6)]
        nm, nf = norm_mix[i].reshape(1, D), norm_ffn[i].reshape(1, D)
        dgu, dy2, st_g2 = _gate_bwd(dx, sv["y2"], g2, W[("ffn_out", i)], sv["gu"], w_is_transposed=False, name="ffn_out_bwd")
        dW[("ffn_out", i)] = _weight_grad(dy2, sv["act"], transpose_out=True, name="dw_ffn_out")
        dW[("ffn_in", i)] = _weight_grad(sv["h2"], dgu, transpose_out=True, name="dw_ffn_in")
        dx1, st_f = _norm_bwd(dgu, W[("ffn_in", i)], sv["x1"], dx, nf, sc2, name="ffn_in_bwd")
        if i % 2 == 0:
            do, dy1, st_g1 = _gate_bwd(dx1, sv["y1"], g1, W[("a_out", j)], None, w_is_transposed=False, name="out_a_bwd")
            dW[("a_out", j)] = _weight_grad(dy1, sv["o"], transpose_out=True, name="dw_a_out")
            dq, dk, dv, ds = _attn_bwd(sv["qkv"], bias_a, sv["sink"], sv["o"], do, sv["lse"], name="attn_a_bwd", **a_geom)
            dsink[j] = ds[:, 0]
            dqkv = jnp.concatenate([dq, dk, dv], axis=1)
            dW[("a_in", j)] = _weight_grad(sv["h1"], dqkv, transpose_out=True, name="dw_a_in")
            dx0, st_m = _norm_bwd(dqkv, W[("a_in", j)], sv["x0"], dx1, nm, sc1, name="proj_a_bwd")
        else:
            do, dy1, st_g1 = _gate_bwd(dx1, sv["y1"], g1, W[("b_out", j)], None, w_is_transposed=True, name="out_b_bwd")
            dW[("b_out", j)] = _weight_grad(dy1, sv["o"], transpose_out=False, name="dw_b_out").reshape(N_DEV * 64, D)
            gr = [_attn_bwd(sv["qkv"], bias_b[g], None, sv["o"], do, sv["lse"], name="attn_b%d_bwd" % g, **b_geom[g]) for g in range(3)]
            dqkv = jnp.concatenate([t[0] for t in gr] + [t[1] for t in gr] + [t[2] for t in gr], axis=1)
            dW[("b_in", j)] = _weight_grad(sv["h1"], dqkv, transpose_out=True, name="dw_b_in")
            dx0, st_m = _norm_bwd(dqkv, W[("b_in", j)], sv["x0"], dx1, nm, sc1, name="proj_b_bwd")
        dmod_rows[i] = jnp.stack([st_m[0], st_m[1], st_g1[0], st_f[0], st_f[1], st_g2[0]])
        dnm[i], dnf[i] = st_m[2], st_f[2]
        dx = dx0
    grad_x = dx.reshape(1, S, D)

    parts_g = _grad_exchange([dW[(n, l)] for n, l, _ in SEGMENTS])
    masters = {"ffn_in": (ffn_w_in, m_ffn_w_in, v_ffn_w_in), "ffn_out": (ffn_w_out, m_ffn_w_out, v_ffn_w_out),
               "a_in": (a_w_in, m_a_w_in, v_a_w_in), "a_out": (a_w_out, m_a_w_out, v_a_w_out),
               "b_in": (b_w_in, m_b_w_in, v_b_w_in), "b_out": (b_w_out, m_b_w_out, v_b_w_out)}
    rows_wmv = [jnp.concatenate([_to_rows(n, masters[n][q][l]) for n, l, _ in SEGMENTS], axis=0) for q in range(3)]
    res_rows = _adam_shard(parts_g, *rows_wmv)
    big = {}
    for q, kind in enumerate(("grad", "delta", "m", "v")):
        for n in masters:
            layers = [_from_rows(n, res_rows[q][SEG_OFF[s]:SEG_OFF[s] + SEG_ROWS[s]])
                      for s, (sn, l, _) in enumerate(SEGMENTS) if sn == n]
            big[(kind, n)] = jnp.stack(layers)

    sink_row = jnp.pad(jnp.concatenate(dsink).reshape(1, 32), ((0, 0), (0, D - 32)))
    stats = jnp.concatenate([jnp.concatenate(dmod_rows, axis=0), jnp.stack(dnm), jnp.stack(dnf), head_stats[0:2], sink_row,
                             jnp.zeros((STAT_ROWS - 35, D), F32)], axis=0)
    small = [_pack_small(*t) for t in ((ada_b, norm_mix, norm_ffn, final_norm, a_sink),
                                       (m_ada_b, m_norm_mix, m_norm_ffn, m_final_norm, m_a_sink),
                                       (v_ada_b, v_norm_mix, v_norm_ffn, v_final_norm, v_a_sink))]
    all_stats, sg, sd, sm, sv_, loss_tile = _small_exchange(stats, *small)
    loss = loss_tile[0, 0]
    dmod_all = all_stats[:, 0:24].reshape(N_DEV, DEPTH, 6 * D)
    dmod_mine = lax.dynamic_slice_in_dim(dmod_all, me * ncol, ncol, axis=2)
    dmod_pad = jnp.pad(jnp.transpose(dmod_mine, (1, 0, 2)), ((0, 0), (0, LANES - N_DEV), (0, 0))).astype(BF16)
    cond_t = jnp.pad(cond_all.T, ((0, 0), (0, LANES - N_DEV))).astype(BF16)
    ada = _adam_ada_w(cond_t, dmod_pad, ada_w, m_ada_w, v_ada_w)

    outs = [loss, grad_x]
    small_res = [_unpack_small(t) for t in (sg, sd, sm, sv_)]
    for q, kind in enumerate(("grad", "delta", "m", "v")):
        ab, nm_, nf_, fn, sk = small_res[q]
        outs += [ada[q], ab, nm_, nf_, big[(kind, "ffn_in")], big[(kind, "ffn_out")], big[(kind, "a_in")], big[(kind, "a_out")],
                 sk, big[(kind, "b_in")], big[(kind, "b_out")], fn]
    return tuple(outs)
```

```python
import functools
import math

import numpy as np
import jax
import jax.numpy as jnp
from jax import lax
from jax.experimental import pallas as pl
from jax.experimental.pallas import tpu as pltpu

D = 1024
HEAD_DIM = 64
D_FF = 2816
DEPTH = 4
N_DEV = 8
A_QKV = 1536
B_QKV = 2304
A_HALF = 128
B_HALF = 64
B_DILS = (1, 4, 16)
RMS_EPS = 1e-6
NEG = -1e30
ADAM_LR = 0.001
ADAM_B1 = 0.9
ADAM_B2 = 0.999
ADAM_EPS = 1e-08
ADAM_WD = 0.01
ADAM_STEP = 10

LANES = 128
TQ = 128
VMEM_LIMIT = 56 * 1024 * 1024
MESH = pl.DeviceIdType.MESH
F32 = jnp.float32
BF16 = jnp.bfloat16

SEGMENTS = ([("ffn_in", l, 704) for l in range(4)] + [("ffn_out", l, 352) for l in range(4)]
            + [("a_in", j, 192) for j in range(2)] + [("a_out", j, 128) for j in range(2)]
            + [("b_in", j, 288) for j in range(2)] + [("b_out", j, 64) for j in range(2)])
SEG_ROWS = [s[2] for s in SEGMENTS]
SEG_OFF = [sum(SEG_ROWS[:i]) for i in range(len(SEGMENTS))]
SHARD_ROWS = sum(SEG_ROWS)
STAT_ROWS = 40


def _nn(a, b):
    return jnp.dot(a, b, preferred_element_type=F32)


def _nt(a, b):
    return lax.dot_general(a, b, (((1,), (1,)), ((), ())), preferred_element_type=F32)


def _tn(a, b):
    return lax.dot_general(a, b, (((0,), (0,)), ((), ())), preferred_element_type=F32)


def _params(dims=None, vmem=None):
    kw = {}
    if dims is not None:
        kw["dimension_semantics"] = dims
    if vmem is not None:
        kw["vmem_limit_bytes"] = vmem
    return pltpu.CompilerParams(**kw)


def _my_index():
    return 4 * lax.axis_index("x") + 2 * lax.axis_index("y") + lax.axis_index("c")


def _peer(k):
    x, y, c = lax.axis_index("x"), lax.axis_index("y"), lax.axis_index("c")
    px, py, pc = x ^ ((k >> 2) & 1), y ^ ((k >> 1) & 1), c ^ (k & 1)
    return (px, py, pc), 4 * px + 2 * py + pc


def _const_spec(shape):
    nd = len(shape)
    return pl.BlockSpec(shape, lambda *_: (0,) * nd)


def _cond_exchange(c_tile, ada_w, ada_b_mine):
    ncol = ada_w.shape[-1]

    def body(c_ref, w_ref, b_ref, cond_ref, parts_ref, call_ref, mine_ref, send_sems, recv_sems):
        me = _my_index()
        call_ref[me] = c_ref[...]
        copies = []
        for k in range(1, N_DEV):
            dev, _ = _peer(k)
            cp = pltpu.make_async_remote_copy(src_ref=c_ref, dst_ref=call_ref.at[me], send_sem=send_sems.at[0, k - 1],
                                              recv_sem=recv_sems.at[0, k - 1], device_id=dev, device_id_type=MESH)
            cp.start()
            copies.append(cp)
        for k in range(1, N_DEV):
            _, pidx = _peer(k)
            pltpu.make_async_remote_copy(src_ref=c_ref, dst_ref=call_ref.at[pidx], send_sem=send_sems.at[0, k - 1],
                                         recv_sem=recv_sems.at[0, k - 1], device_id=_peer(k)[0], device_id_type=MESH).wait_recv()
        for cp in copies:
            cp.wait_send()
        row = lax.broadcasted_iota(jnp.int32, (N_DEV, D), 0)
        cmat = jnp.zeros((N_DEV, D), F32)
        for j in range(N_DEV):
            cmat = jnp.where(row == j, call_ref[j], cmat)
        cond = cmat * jax.nn.sigmoid(cmat)
        cond_ref[...] = cond
        cb = cond.astype(BF16)
        for l in range(DEPTH):
            mine_ref[l] = _nn(cb, w_ref[l].astype(BF16)) + b_ref[pl.ds(l, 1), :]
        parts_ref[me] = mine_ref[...]
        copies = []
        for k in range(1, N_DEV):
            dev, _ = _peer(k)
            cp = pltpu.make_async_remote_copy(src_ref=mine_ref, dst_ref=parts_ref.at[me], send_sem=send_sems.at[1, k - 1],
                                              recv_sem=recv_sems.at[1, k - 1], device_id=dev, device_id_type=MESH)
            cp.start()
            copies.append(cp)
        for k in range(1, N_DEV):
            dev, pidx = _peer(k)
            pltpu.make_async_remote_copy(src_ref=mine_ref, dst_ref=parts_ref.at[pidx], send_sem=send_sems.at[1, k - 1],
                                         recv_sem=recv_sems.at[1, k - 1], device_id=dev, device_id_type=MESH).wait_recv()
        for cp in copies:
            cp.wait_send()

    vm = pl.BlockSpec(memory_space=pltpu.VMEM)
    return pl.pallas_call(
        body, name="cond_exchange",
        out_shape=(jax.ShapeDtypeStruct((N_DEV, D), F32), jax.ShapeDtypeStruct((N_DEV, DEPTH, N_DEV, ncol), F32)),
        in_specs=[vm, vm, vm], out_specs=(vm, vm),
        scratch_shapes=[pltpu.VMEM((N_DEV, N_DEV, D), F32), pltpu.VMEM((DEPTH, N_DEV, ncol), F32),
                        pltpu.SemaphoreType.DMA((2, N_DEV - 1)), pltpu.SemaphoreType.DMA((2, N_DEV - 1))],
        compiler_params=_params(vmem=VMEM_LIMIT),
    )(c_tile, ada_w, ada_b_mine)[:2]


def _all_gather_weights(shards):
    n = len(shards)
    big = max(range(n), key=lambda s: shards[s].shape[0])
    assert N_DEV * shards[big].shape[0] >= SHARD_ROWS

    def body(*refs):
        ins, outs = refs[:n], refs[n:2 * n]
        local_sems, send_sems, recv_sems = refs[2 * n:]
        me = _my_index()
        local = []
        for s in range(n):
            rows = ins[s].shape[0]
            cp = pltpu.make_async_copy(ins[s], outs[s].at[pl.ds(me * rows, rows)], local_sems.at[s])
            cp.start()
            local.append(cp)
        for k in range(1, N_DEV):
            dev, _ = _peer(k)
            for s in range(n):
                rows = ins[s].shape[0]
                pltpu.make_async_remote_copy(src_ref=ins[s], dst_ref=outs[s].at[pl.ds(me * rows, rows)],
                                             send_sem=send_sems.at[k - 1], recv_sem=recv_sems.at[k - 1],
                                             device_id=dev, device_id_type=MESH).start()
        whole = outs[big].at[pl.ds(0, SHARD_ROWS)]
        for k in range(1, N_DEV):
            dev, _ = _peer(k)
            w = pltpu.make_async_remote_copy(src_ref=whole, dst_ref=whole, send_sem=send_sems.at[k - 1],
                                             recv_sem=recv_sems.at[k - 1], device_id=dev, device_id_type=MESH)
            w.wait_send()
            w.wait_recv()
        for cp in local:
            cp.wait()

    hbm = pl.BlockSpec(memory_space=pl.ANY)
    return pl.pallas_call(
        body, name="weight_all_gather",
        out_shape=tuple(jax.ShapeDtypeStruct((N_DEV * s.shape[0], D), s.dtype) for s in shards),
        in_specs=[hbm] * n, out_specs=tuple([hbm] * n),
        scratch_shapes=[pltpu.SemaphoreType.DMA((n,)), pltpu.SemaphoreType.DMA((N_DEV - 1,)),
                        pltpu.SemaphoreType.DMA((N_DEV - 1,))],
    )(*shards)


def _grad_exchange(fulls):
    n = len(fulls)

    def body(*refs):
        ins, out = refs[:n], refs[n]
        local_sems, send_sems, recv_sems = refs[n + 1:]
        me = _my_index()
        local = []
        for s in range(n):
            rows = SEG_ROWS[s]
            cp = pltpu.make_async_copy(ins[s].at[pl.ds(me * rows, rows)], out.at[me, pl.ds(SEG_OFF[s], rows)], local_sems.at[s])
            cp.start()
            local.append(cp)
        for k in range(1, N_DEV):
            dev, pidx = _peer(k)
            for s in range(n):
                rows = SEG_ROWS[s]
                pltpu.make_async_remote_copy(src_ref=ins[s].at[pl.ds(pidx * rows, rows)],
                                             dst_ref=out.at[me, pl.ds(SEG_OFF[s], rows)],
                                             send_sem=send_sems.at[k - 1], recv_sem=recv_sems.at[k - 1],
                                             device_id=dev, device_id_type=MESH).start()
        for k in range(1, N_DEV):
            dev, pidx = _peer(k)
            w = pltpu.make_async_remote_copy(src_ref=out.at[me], dst_ref=out.at[pidx], send_sem=send_sems.at[k - 1],
                                             recv_sem=recv_sems.at[k - 1], device_id=dev, device_id_type=MESH)
            w.wait_send()
            w.wait_recv()
        for cp in local:
            cp.wait()

    hbm = pl.BlockSpec(memory_space=pl.ANY)
    return pl.pallas_call(
        body, name="grad_exchange",
        out_shape=jax.ShapeDtypeStruct((N_DEV, SHARD_ROWS, D), BF16),
        in_specs=[hbm] * n, out_specs=hbm,
        scratch_shapes=[pltpu.SemaphoreType.DMA((n,)), pltpu.SemaphoreType.DMA((N_DEV - 1,)),
                        pltpu.SemaphoreType.DMA((N_DEV - 1,))],
    )(*fulls)


def _norm_mod(x, nw, sc, sh):
    ms = jnp.mean(x * x, axis=-1, keepdims=True)
    xh = x * lax.rsqrt(ms + RMS_EPS)
    return xh, (xh * nw) * (1.0 + sc) + sh


def _proj(x, nw, sc, sh, wt, *, ffn, name):
    S, N = x.shape[0], wt.shape[0]
    tm = 256 if ffn else 512

    def body(x_ref, nw_ref, sc_ref, sh_ref, w_ref, h_ref, out_ref, *act_ref):
        _, h = _norm_mod(x_ref[...], nw_ref[...], sc_ref[...], sh_ref[...])
        hb = h.astype(BF16)
        h_ref[...] = hb
        if ffn:
            gate = _nt(hb, w_ref[pl.ds(0, D_FF), :])
            up = _nt(hb, w_ref[pl.ds(D_FF, D_FF), :])
            out_ref[:, pl.ds(0, D_FF)] = gate.astype(BF16)
            out_ref[:, pl.ds(D_FF, D_FF)] = up.astype(BF16)
            act_ref[0][...] = ((gate * jax.nn.sigmoid(gate)) * up).astype(BF16)
        else:
            out_ref[...] = _nt(hb, w_ref[...]).astype(BF16)

    row = lambda w: pl.BlockSpec((tm, w), lambda i: (i, 0))
    out_shape = [jax.ShapeDtypeStruct((S, D), BF16), jax.ShapeDtypeStruct((S, N), BF16)]
    out_specs = [row(D), row(N)]
    if ffn:
        out_shape.append(jax.ShapeDtypeStruct((S, D_FF), BF16))
        out_specs.append(row(D_FF))
    vec = _const_spec((1, D))
    return pl.pallas_call(
        body, name=name, grid=(S // tm,), out_shape=tuple(out_shape),
        in_specs=[row(D), vec, vec, vec, _const_spec((N, D))], out_specs=tuple(out_specs),
        compiler_params=_params(("parallel",), VMEM_LIMIT),
    )(x, nw, sc, sh, wt)


def _gated_residual(a, w, x, g, *, w_is_transposed, name):
    S, K = a.shape
    tm = 512

    def body(a_ref, w_ref, x_ref, g_ref, xo_ref, y_ref):
        y = _nt(a_ref[...], w_ref[...]) if w_is_transposed else _nn(a_ref[...], w_ref[...])
        y_ref[...] = y.astype(BF16)
        xo_ref[...] = x_ref[...] + g_ref[...] * y

    row = lambda w_: pl.BlockSpec((tm, w_), lambda i: (i, 0))
    return pl.pallas_call(
        body, name=name, grid=(S // tm,),
        out_shape=(jax.ShapeDtypeStruct((S, D), F32), jax.ShapeDtypeStruct((S, D), BF16)),
        in_specs=[row(K), _const_spec(w.shape), row(D), _const_spec((1, D))], out_specs=(row(D), row(D)),
        compiler_params=_params(("parallel",), VMEM_LIMIT),
    )(a, w, x, g)


def _alibi_bias(slopes, half, dil):
    tk = TQ + 2 * half
    rel = np.arange(tk)[:, None] - half - np.arange(TQ)[None, :]
    band = np.abs(rel) <= half
    dist = (dil * np.abs(rel)).astype(np.float32)
    tabs = [np.where(band, -np.float32(s) * dist, np.float32(NEG)).astype(np.float32) for s in slopes]
    units = [np.concatenate(tabs[u:u + 8], axis=1) for u in range(0, len(tabs), 8)]
    out = []
    for variant in range(4):
        for tab in units:
            tab = tab.copy()
            if variant & 1:
                tab[:half] = NEG
            if variant & 2:
                tab[tk - half:] = NEG
            out.append(tab)
    return jnp.asarray(np.concatenate(out, axis=0))


def _bias_spec(tk, n_units, nq):
    def index(u, rho, i):
        variant = jnp.where(i == 0, 1, 0) + jnp.where(i == nq - 1, 2, 0)
        return (variant * n_units + u, 0)
    return pl.BlockSpec((tk, 8 * TQ), index)


def _slopes(n):
    return (2.0 ** (-8.0 * np.arange(1, n + 1) / n)).astype(np.float32)


def _attn_specs(C, r, half, qoff, koff, voff, L):
    cb = C // LANES
    per = TQ // half
    nkb = per + 2
    nblk = L // half

    def qmap(t):
        return lambda u, rho, i: (i, rho * cb + qoff // LANES + 4 * u + t)

    def kmap(j, off):
        return lambda u, rho, i: (jnp.clip(i * per - 1 + j, 0, nblk - 1), rho * cb + off // LANES + u)

    specs = [pl.BlockSpec((TQ, LANES), qmap(t)) for t in range(4)]
    specs += [pl.BlockSpec((half, LANES), kmap(j, koff)) for j in range(nkb)]
    specs += [pl.BlockSpec((half, LANES), kmap(j, voff)) for j in range(nkb)]
    return specs, nkb


def _head_masks():
    lane = lax.broadcasted_iota(jnp.int32, (TQ, LANES), 1)
    lo = lane < HEAD_DIM
    return lo, jnp.logical_not(lo)


def _stack_heads(refs_or_vals, lo, hi, scale):
    blocks = []
    for t in range(4):
        xf = refs_or_vals[t]
        if scale != 1.0:
            xf = xf * scale
        for a in range(2):
            xm = jnp.where(lo if a == 0 else hi, xf, 0.0)
            if a != t // 2:
                xm = pltpu.roll(xm, HEAD_DIM, 1)
            blocks.append(xm.astype(BF16))
    return jnp.concatenate(blocks, axis=0)


def _tile_from_columns(x8t, t):
    r0 = HEAD_DIM * (t // 2)
    top = x8t[r0:r0 + HEAD_DIM, 2 * t * TQ:(2 * t + 1) * TQ]
    bot = x8t[r0:r0 + HEAD_DIM, (2 * t + 1) * TQ:(2 * t + 2) * TQ]
    return jnp.concatenate([top, bot], axis=0).T


def _attn_fwd(qkv, bias, sink, *, C, r, half, qoff, koff, voff, n_units, out_dtype, name):
    S = qkv.shape[0]
    L = S // r
    nq, nblk = L // TQ, L // half
    specs, nkb = _attn_specs(C, r, half, qoff, koff, voff, L)
    tk = nkb * half
    use_sink = sink is not None

    def body(*refs):
        q_refs, k_refs, v_refs = refs[:4], refs[4:4 + nkb], refs[4 + nkb:4 + 2 * nkb]
        rest = refs[4 + 2 * nkb:]
        bias_ref = rest[0]
        sink_ref = rest[1] if use_sink else None
        o_ref, lse_ref = rest[-2], rest[-1]
        i = pl.program_id(2)
        k2 = jnp.concatenate([kr[...] for kr in k_refs], axis=0)
        v2t = jnp.concatenate([vr[...] for vr in v_refs], axis=0).astype(F32).T.astype(BF16)
        v2t = jnp.concatenate([v2t, jnp.ones((16, tk), BF16)], axis=0)
        lo, hi = _head_masks()
        q8 = _stack_heads([qr[...].astype(F32) for qr in q_refs], lo, hi, HEAD_DIM ** -0.5)
        s = _nt(k2, q8) + bias_ref[...]
        m = jnp.max(s, axis=0, keepdims=True)
        if use_sink:
            sk = sink_ref[pl.ds(0, 1), :]
            m = jnp.maximum(m, sk)
        pv = _nn(v2t, jnp.exp(s - m).astype(BF16))
        l = pv[LANES:LANES + 1]
        if use_sink:
            l = l + jnp.exp(sk - m)
        o8t = pv[:LANES] / l
        lse8 = jnp.broadcast_to(m + jnp.log(l), (LANES, 8 * TQ))
        for t in range(4):
            cols = pl.ds(t * LANES, LANES)
            o_ref[:, cols] = _tile_from_columns(o8t, t).astype(out_dtype)
            lse_ref[:, cols] = _tile_from_columns(lse8, t)

    in_specs = specs + [_bias_spec(tk, n_units, nq)]
    args = [qkv.reshape(L, r * C)] * (4 + 2 * nkb) + [bias]
    if use_sink:
        in_specs.append(pl.BlockSpec((8, 8 * TQ), lambda u, rho, i: (u, 0)))
        args.append(sink)
    wide = pl.BlockSpec((TQ, 4 * LANES), lambda u, rho, i: (i, rho * n_units + u))
    o, lse = pl.pallas_call(
        body, name=name, grid=(n_units, r, nq),
        out_shape=(jax.ShapeDtypeStruct((L, r * n_units * 512), out_dtype), jax.ShapeDtypeStruct((L, r * n_units * 512), F32)),
        in_specs=in_specs, out_specs=(wide, wide),
        compiler_params=_params(("parallel", "parallel", "parallel"), VMEM_LIMIT),
    )(*args)
    return o.reshape(S, n_units * 512), lse.reshape(S, n_units * 512)


def _attn_bwd(qkv, bias, sink, o, do, lse, *, C, r, half, qoff, koff, voff, n_units, name):
    S = qkv.shape[0]
    L = S // r
    nq, nblk = L // TQ, L // half
    specs, nkb = _attn_specs(C, r, half, qoff, koff, voff, L)
    tk = nkb * half
    use_sink = sink is not None

    def body(*refs):
        q_refs, k_refs, v_refs = refs[:4], refs[4:4 + nkb], refs[4 + nkb:4 + 2 * nkb]
        rest = list(refs[4 + 2 * nkb:])
        bias_ref = rest.pop(0)
        sink_ref = rest.pop(0) if use_sink else None
        o_ref, do_ref, lse_ref = rest[:3]
        rest = rest[3:]
        dq_ref, dk_ref, dv_ref = rest[:3]
        rest = rest[3:]
        dsink_ref = rest.pop(0) if use_sink else None
        acck, accv = rest
        i = pl.program_id(2)

        @pl.when(i == 0)
        def _():
            acck[...] = jnp.zeros_like(acck)
            accv[...] = jnp.zeros_like(accv)

        if use_sink:
            @pl.when(jnp.logical_and(i == 0, pl.program_id(1) == 0))
            def _():
                dsink_ref[...] = jnp.zeros_like(dsink_ref)

        k2 = jnp.concatenate([kr[...] for kr in k_refs], axis=0)
        v2 = jnp.concatenate([vr[...] for vr in v_refs], axis=0)
        k2t = k2.astype(F32).T.astype(BF16)
        lo, hi = _head_masks()
        q8 = _stack_heads([qr[...].astype(F32) for qr in q_refs], lo, hi, HEAD_DIM ** -0.5)
        dos = [do_ref[:, pl.ds(t * LANES, LANES)].astype(F32) for t in range(4)]
        do8 = _stack_heads(dos, lo, hi, 1.0)
        deltas, lses = [], []
        for t in range(4):
            cols = pl.ds(t * LANES, LANES)
            prod_t = (dos[t] * o_ref[:, cols].astype(F32)).T
            lse_t = lse_ref[:, cols].T
            for a in range(2):
                rows = slice(a * HEAD_DIM, (a + 1) * HEAD_DIM)
                deltas.append(jnp.sum(prod_t[rows], axis=0, keepdims=True))
                lses.append(lse_t[a * HEAD_DIM:a * HEAD_DIM + 1])
        delta8 = jnp.concatenate(deltas, axis=1)
        lse8 = jnp.concatenate(lses, axis=1)
        s = _nt(k2, q8) + bias_ref[...]
        p = jnp.exp(s - lse8)
        dp = _nt(v2, do8)
        dsb = (p * (dp - delta8)).astype(BF16)
        dq8t = _nn(k2t, dsb)
        for t in range(4):
            dq_ref[:, pl.ds(t * LANES, LANES)] = (_tile_from_columns(dq8t, t) * (HEAD_DIM ** -0.5)).astype(BF16)
        start = pl.multiple_of(i * TQ, TQ)
        acck[pl.ds(start, tk), :] = acck[pl.ds(start, tk), :] + _nn(dsb, q8)
        accv[pl.ds(start, tk), :] = accv[pl.ds(start, tk), :] + _nn(p.astype(BF16), do8)
        if use_sink:
            e = jnp.exp(sink_ref[pl.ds(0, 1), :] - lse8) * delta8
            for h in range(8):
                part = -jnp.sum(e[:, h * TQ:(h + 1) * TQ], axis=1, keepdims=True)
                dsink_ref[pl.ds(h, 1), :] = dsink_ref[pl.ds(h, 1), :] + part

        @pl.when(i == nq - 1)
        def _():
            dk_ref[...] = acck[pl.ds(half, L), :].astype(BF16)
            dv_ref[...] = accv[pl.ds(half, L), :].astype(BF16)

    wide = pl.BlockSpec((TQ, 4 * LANES), lambda u, rho, i: (i, rho * n_units + u))
    tall = pl.BlockSpec((L, LANES), lambda u, rho, i: (0, rho * n_units + u))
    in_specs = specs + [_bias_spec(tk, n_units, nq)]
    args = [qkv.reshape(L, r * C)] * (4 + 2 * nkb) + [bias]
    if use_sink:
        in_specs.append(pl.BlockSpec((8, 8 * TQ), lambda u, rho, i: (u, 0)))
        args.append(sink)
    in_specs += [wide, wide, wide]
    w = r * n_units * 512
    args += [o.reshape(L, w), do.reshape(L, w), lse.reshape(L, w)]
    out_shape = [jax.ShapeDtypeStruct((L, w), BF16), jax.ShapeDtypeStruct((L, w // 4), BF16), jax.ShapeDtypeStruct((L, w // 4), BF16)]
    out_specs = [wide, tall, tall]
    if use_sink:
        out_shape.append(jax.ShapeDtypeStruct((n_units * 8, LANES), F32))
        out_specs.append(pl.BlockSpec((8, LANES), lambda u, rho, i: (u, 0)))
    res = pl.pallas_call(
        body, name=name, grid=(n_units, r, nq), out_shape=tuple(out_shape), in_specs=in_specs, out_specs=tuple(out_specs),
        scratch_shapes=[pltpu.VMEM((L + 2 * half, LANES), F32), pltpu.VMEM((L + 2 * half, LANES), F32)],
        compiler_params=_params(("arbitrary", "arbitrary", "arbitrary"), VMEM_LIMIT),
    )(*args)
    dq, dk, dv = res[0].reshape(S, n_units * 512), res[1].reshape(S, n_units * 128), res[2].reshape(S, n_units * 128)
    return dq, dk, dv, (res[3] if use_sink else None)


def _merge_groups(os_, lses):
    S, W = os_[0].shape
    tm = 512

    def body(o0, o1, o2, l0, l1, l2, o_ref, lse_ref):
        ls = [l0[...], l1[...], l2[...]]
        mx = jnp.maximum(jnp.maximum(ls[0], ls[1]), ls[2])
        es = [jnp.exp(l - mx) for l in ls]
        den = es[0] + es[1] + es[2]
        o = (es[0] / den) * o0[...] + (es[1] / den) * o1[...] + (es[2] / den) * o2[...]
        o_ref[...] = o.astype(BF16)
        lse_ref[...] = mx + jnp.log(den)

    row = pl.BlockSpec((tm, W), lambda i: (i, 0))
    return pl.pallas_call(
        body, name="merge_groups", grid=(S // tm,),
        out_shape=(jax.ShapeDtypeStruct((S, W), BF16), jax.ShapeDtypeStruct((S, W), F32)),
        in_specs=[row] * 6, out_specs=(row, row), compiler_params=_params(("parallel",), VMEM_LIMIT),
    )(*os_, *lses)


def _loss_head(x, target, fnw):
    S = x.shape[0]
    tm = 512

    def body(x_ref, t_ref, w_ref, dx_ref, st_ref):
        @pl.when(pl.program_id(0) == 0)
        def _():
            st_ref[...] = jnp.zeros_like(st_ref)

        xv = x_ref[...]
        rstd = lax.rsqrt(jnp.mean(xv * xv, axis=-1, keepdims=True) + RMS_EPS)
        xh = xv * rstd
        err = xh * w_ref[...] - t_ref[...]
        dy = err * (1.0 / D)
        dxh = dy * w_ref[...]
        dx_ref[...] = rstd * (dxh - xh * jnp.mean(dxh * xh, axis=-1, keepdims=True))
        st_ref[pl.ds(0, 1), :] = st_ref[pl.ds(0, 1), :] + jnp.sum(dy * xh, axis=0, keepdims=True)
        st_ref[pl.ds(1, 1), :] = st_ref[pl.ds(1, 1), :] + jnp.sum(err * err, axis=0, keepdims=True)

    row = pl.BlockSpec((tm, D), lambda i: (i, 0))
    return pl.pallas_call(
        body, name="loss_head", grid=(S // tm,),
        out_shape=(jax.ShapeDtypeStruct((S, D), F32), jax.ShapeDtypeStruct((8, D), F32)),
        in_specs=[row, row, _const_spec((1, D))], out_specs=(row, _const_spec((8, D))),
        compiler_params=_params(("arbitrary",), VMEM_LIMIT),
    )(x, target, fnw)


def _gate_bwd(dx, y, g, w, gu, *, w_is_transposed, name):
    S = dx.shape[0]
    K = w.shape[1] if w_is_transposed else w.shape[0]
    ffn = gu is not None
    tm = 256 if ffn else 512
    wout = 2 * K if ffn else K

    def body(dx_ref, y_ref, g_ref, w_ref, *rest):
        if ffn:
            gu_ref, da_ref, dyb_ref, st_ref = rest
        else:
            da_ref, dyb_ref, st_ref = rest

        @pl.when(pl.program_id(0) == 0)
        def _():
            st_ref[...] = jnp.zeros_like(st_ref)

        dxv = dx_ref[...]
        st_ref[pl.ds(0, 1), :] = st_ref[pl.ds(0, 1), :] + jnp.sum(dxv * y_ref[...].astype(F32), axis=0, keepdims=True)
        dyb = (dxv * g_ref[...]).astype(BF16)
        dyb_ref[...] = dyb
        da = _nn(dyb, w_ref[...]) if w_is_transposed else _nt(dyb, w_ref[...])
        if ffn:
            gate = gu_ref[:, pl.ds(0, K)].astype(F32)
            up = gu_ref[:, pl.ds(K, K)].astype(F32)
            sig = jax.nn.sigmoid(gate)
            da_ref[:, pl.ds(0, K)] = (da * up * (sig * (1.0 + gate * (1.0 - sig)))).astype(BF16)
            da_ref[:, pl.ds(K, K)] = (da * (gate * sig)).astype(BF16)
        else:
            da_ref[...] = da.astype(BF16)

    row = lambda w_: pl.BlockSpec((tm, w_), lambda i: (i, 0))
    in_specs = [row(D), row(D), _const_spec((1, D)), _const_spec(w.shape)]
    args = [dx, y, g, w]
    if ffn:
        in_specs.append(row(wout))
        args.append(gu)
    return pl.pallas_call(
        body, name=name, grid=(S // tm,),
        out_shape=(jax.ShapeDtypeStruct((S, wout), BF16), jax.ShapeDtypeStruct((S, D), BF16), jax.ShapeDtypeStruct((8, D), F32)),
        in_specs=in_specs, out_specs=(row(wout), row(D), _const_spec((8, D))),
        compiler_params=_params(("arbitrary",), VMEM_LIMIT),
    )(*args)


def _norm_bwd(dy, wt, x, dres, nw, sc, *, name):
    S, N = dy.shape
    tm = 256 if N > 4096 else 512

    def body(dy_ref, w_ref, x_ref, dres_ref, nw_ref, sc_ref, dx_ref, st_ref):
        @pl.when(pl.program_id(0) == 0)
        def _():
            st_ref[...] = jnp.zeros_like(st_ref)

        dh = _nn(dy_ref[...], w_ref[...])
        xv = x_ref[...]
        rstd = lax.rsqrt(jnp.mean(xv * xv, axis=-1, keepdims=True) + RMS_EPS)
        xh = xv * rstd
        nwv, scale = nw_ref[...], 1.0 + sc_ref[...]
        dxh = dh * (nwv * scale)
        dx_ref[...] = dres_ref[...] + rstd * (dxh - xh * jnp.mean(dxh * xh, axis=-1, keepdims=True))
        dhx = dh * xh
        st_ref[pl.ds(0, 1), :] = st_ref[pl.ds(0, 1), :] + jnp.sum(dh, axis=0, keepdims=True)
        st_ref[pl.ds(1, 1), :] = st_ref[pl.ds(1, 1), :] + jnp.sum(dhx * nwv, axis=0, keepdims=True)
        st_ref[pl.ds(2, 1), :] = st_ref[pl.ds(2, 1), :] + jnp.sum(dhx * scale, axis=0, keepdims=True)

    row = lambda w_: pl.BlockSpec((tm, w_), lambda i: (i, 0))
    vec = _const_spec((1, D))
    return pl.pallas_call(
        body, name=name, grid=(S // tm,),
        out_shape=(jax.ShapeDtypeStruct((S, D), F32), jax.ShapeDtypeStruct((8, D), F32)),
        in_specs=[row(N), _const_spec((N, D)), row(D), row(D), vec, vec], out_specs=(row(D), _const_spec((8, D))),
        compiler_params=_params(("arbitrary",), VMEM_LIMIT),
    )(dy, wt, x, dres, nw, sc)


def _weight_grad(a, b, *, transpose_out, name):
    S, N = b.shape
    nb = N // 2 if N > 4096 else N
    tk = 512

    def body(a_ref, b_ref, out_ref, acc):
        k = pl.program_id(1)

        @pl.when(k == 0)
        def _():
            acc[...] = jnp.zeros_like(acc)

        acc[...] += _tn(a_ref[...], b_ref[...])

        @pl.when(k == pl.num_programs(1) - 1)
        def _():
            out_ref[...] = (acc[...].T if transpose_out else acc[...]).astype(BF16)

    out_block = pl.BlockSpec((nb, D), lambda n, k: (n, 0)) if transpose_out else pl.BlockSpec((D, nb), lambda n, k: (0, n))
    return pl.pallas_call(
        body, name=name, grid=(N // nb, S // tk),
        out_shape=jax.ShapeDtypeStruct((N, D) if transpose_out else (D, N), BF16),
        in_specs=[pl.BlockSpec((tk, D), lambda n, k: (k, 0)), pl.BlockSpec((tk, nb), lambda n, k: (k, n))],
        out_specs=out_block, scratch_shapes=[pltpu.VMEM((D, nb), F32)],
        compiler_params=_params(("parallel", "arbitrary"), VMEM_LIMIT),
    )(a, b)


def _adamw(w, g, m, v):
    m = ADAM_B1 * m + (1.0 - ADAM_B1) * g
    v = ADAM_B2 * v + (1.0 - ADAM_B2) * (g * g)
    m_hat = m / (1.0 - ADAM_B1 ** ADAM_STEP)
    v_hat = v / (1.0 - ADAM_B2 ** ADAM_STEP)
    delta = -ADAM_LR * (m_hat / (jnp.sqrt(v_hat) + ADAM_EPS) + ADAM_WD * w)
    return delta, m, v


def _adam_shard(parts, w, m, v):
    R = w.shape[0]
    tr = 192

    def body(p_ref, w_ref, m_ref, v_ref, g_out, d_out, m_out, v_out):
        g = p_ref[0].astype(F32)
        for j in range(1, N_DEV):
            g = g + p_ref[j].astype(F32)
        delta, mn, vn = _adamw(w_ref[...], g, m_ref[...], v_ref[...])
        g_out[...] = g
        d_out[...] = delta
        m_out[...] = mn
        v_out[...] = vn

    row = pl.BlockSpec((tr, D), lambda i: (i, 0))
    shp = jax.ShapeDtypeStruct((R, D), F32)
    return pl.pallas_call(
        body, name="adam_shard", grid=(R // tr,), out_shape=(shp,) * 4,
        in_specs=[pl.BlockSpec((N_DEV, tr, D), lambda i: (0, i, 0)), row, row, row], out_specs=(row,) * 4,
        compiler_params=_params(("parallel",), VMEM_LIMIT),
    )(parts, w, m, v)


def _adam_ada_w(cond_t, dmod, w, m, v):
    ncol = w.shape[-1]
    tr = 512

    def body(c_ref, d_ref, w_ref, m_ref, v_ref, g_out, d_out, m_out, v_out):
        g = _nn(c_ref[...], d_ref[0])
        delta, mn, vn = _adamw(w_ref[0], g, m_ref[0], v_ref[0])
        g_out[0] = g
        d_out[0] = delta
        m_out[0] = mn
        v_out[0] = vn

    blk = pl.BlockSpec((1, tr, ncol), lambda l, i: (l, i, 0))
    shp = jax.ShapeDtypeStruct(w.shape, F32)
    return pl.pallas_call(
        body, name="adam_ada_w", grid=(DEPTH, D // tr), out_shape=(shp,) * 4,
        in_specs=[pl.BlockSpec((tr, LANES), lambda l, i: (i, 0)), pl.BlockSpec((1, LANES, ncol), lambda l, i: (l, 0, 0)), blk, blk, blk],
        out_specs=(blk,) * 4, compiler_params=_params(("parallel", "parallel"), VMEM_LIMIT),
    )(cond_t, dmod, w, m, v)


def _small_exchange(stats, w, m, v):
    loss_row = 33

    def body(s_ref, w_ref, m_ref, v_ref, all_ref, g_out, d_out, m_out, v_out, loss_out, send_sems, recv_sems):
        me = _my_index()
        all_ref[me] = s_ref[...]
        copies = []
        for k in range(1, N_DEV):
            dev, _ = _peer(k)
            cp = pltpu.make_async_remote_copy(src_ref=s_ref, dst_ref=all_ref.at[me], send_sem=send_sems.at[k - 1],
                                              recv_sem=recv_sems.at[k - 1], device_id=dev, device_id_type=MESH)
            cp.start()
            copies.append(cp)
        for k in range(1, N_DEV):
            dev, pidx = _peer(k)
            pltpu.make_async_remote_copy(src_ref=s_ref, dst_ref=all_ref.at[pidx], send_sem=send_sems.at[k - 1],
                                         recv_sem=recv_sems.at[k - 1], device_id=dev, device_id_type=MESH).wait_recv()
        for cp in copies:
            cp.wait_send()
        g = all_ref[0]
        for j in range(1, N_DEV):
            g = g + all_ref[j]
        delta, mn, vn = _adamw(w_ref[...], g, m_ref[...], v_ref[...])
        g_out[...] = g
        d_out[...] = delta
        m_out[...] = mn
        v_out[...] = vn
        tot = jnp.sum(g[loss_row:loss_row + 1, :], axis=-1, keepdims=True) * (0.5 / D)
        loss_out[...] = jnp.broadcast_to(tot, loss_out.shape)

    vm = pl.BlockSpec(memory_space=pltpu.VMEM)
    shp = jax.ShapeDtypeStruct((STAT_ROWS, D), F32)
    return pl.pallas_call(
        body, name="small_exchange",
        out_shape=(jax.ShapeDtypeStruct((N_DEV, STAT_ROWS, D), F32), shp, shp, shp, shp, jax.ShapeDtypeStruct((8, LANES), F32)),
        in_specs=[vm] * 4, out_specs=(vm,) * 6,
        scratch_shapes=[pltpu.SemaphoreType.DMA((N_DEV - 1,)), pltpu.SemaphoreType.DMA((N_DEV - 1,))],
        compiler_params=_params(vmem=VMEM_LIMIT),
    )(stats, w, m, v)


def _to_rows(name, a):
    if name in ("ffn_in", "a_in", "b_in"):
        return a.T
    if name == "b_out":
        return a.T.reshape(-1, D)
    return a


def _from_rows(name, a):
    if name in ("ffn_in", "a_in", "b_in"):
        return a.T
    if name == "b_out":
        return a.reshape(-1, 512).T
    return a


def _pack_small(ada_b, norm_mix, norm_ffn, final_norm, sink):
    pad = jnp.zeros((1, D), F32)
    sink_row = jnp.pad(sink.reshape(1, -1), ((0, 0), (0, D - sink.size)))
    rows = [ada_b.reshape(24, D), norm_mix, norm_ffn, final_norm.reshape(1, D), pad, sink_row]
    rows.append(jnp.zeros((STAT_ROWS - 35, D), F32))
    return jnp.concatenate(rows, axis=0)


def _unpack_small(a):
    return a[0:24].reshape(4, 6 * D), a[24:28], a[28:32], a[32], a[34, :32].reshape(2, 16)


def kernel(x, c, ada_w, ada_b, norm_mix, norm_ffn, ffn_w_in, ffn_w_out, a_w_in, a_w_out, a_sink, b_w_in, b_w_out, final_norm, loss_target, m_ada_w, m_ada_b, m_norm_mix, m_norm_ffn, m_ffn_w_in, m_ffn_w_out, m_a_w_in, m_a_w_out, m_a_sink, m_b_w_in, m_b_w_out, m_final_norm, v_ada_w, v_ada_b, v_norm_mix, v_norm_ffn, v_ffn_w_in, v_ffn_w_out, v_a_w_in, v_a_w_out, v_a_sink, v_b_w_in, v_b_w_out, v_final_norm):
    S = x.shape[1]
    x0 = x.reshape(S, D)
    target = loss_target.reshape(S, D)
    me = _my_index()
    ncol = ada_w.shape[-1]

    ada_b_mine = lax.dynamic_slice_in_dim(ada_b, me * ncol, ncol, axis=1)
    cond_all, parts = _cond_exchange(jnp.broadcast_to(c.reshape(1, D), (8, D)), ada_w, ada_b_mine)
    mod = lax.dynamic_index_in_dim(parts, me, axis=2, keepdims=False)
    mod = jnp.transpose(mod, (1, 0, 2)).reshape(DEPTH, 6, 1, D)

    weights = {"ffn_in": ffn_w_in, "ffn_out": ffn_w_out, "a_in": a_w_in, "a_out": a_w_out, "b_in": b_w_in, "b_out": b_w_out}
    shards = [_to_rows(n, weights[n][l]).astype(BF16) for n, l, _ in SEGMENTS]
    gathered = _all_gather_weights(shards)
    W = {(n, l): g for (n, l, _), g in zip(SEGMENTS, gathered)}
    for j in range(2):
        W[("b_out", j)] = W[("b_out", j)].reshape(D, 512)

    a_slopes, b_slopes = _slopes(16), _slopes(24)
    bias_a = _alibi_bias(a_slopes, A_HALF, 1)
    bias_b = [_alibi_bias(b_slopes[8 * g:8 * g + 8], B_HALF, dil) for g, dil in enumerate(B_DILS)]
    a_geom = dict(C=A_QKV, r=1, half=A_HALF, qoff=0, koff=1024, voff=1280, n_units=2)
    b_geom = [dict(C=B_QKV, r=dil, half=B_HALF, qoff=512 * g, koff=1536 + 128 * g, voff=1920 + 128 * g, n_units=1)
              for g, dil in enumerate(B_DILS)]

    saved = []
    xcur = x0
    for i in range(DEPTH):
        j = i // 2
        sh1, sc1, g1, sh2, sc2, g2 = [mod[i, q] for q in range(6)]
        nm, nf = norm_mix[i].reshape(1, D), norm_ffn[i].reshape(1, D)
        if i % 2 == 0:
            sink_rep = jnp.repeat(jnp.repeat(a_sink[j], TQ).reshape(2, 1, 8 * TQ), 8, axis=1).reshape(16, 8 * TQ)
            h1, qkv = _proj(xcur, nm, sc1, sh1, W[("a_in", j)], ffn=False, name="proj_a")
            o, lse = _attn_fwd(qkv, bias_a, sink_rep, out_dtype=BF16, name="attn_a_fwd", **a_geom)
            x1, y1 = _gated_residual(o, W[("a_out", j)], xcur, g1, w_is_transposed=False, name="out_a")
        else:
            sink_rep = None
            h1, qkv = _proj(xcur, nm, sc1, sh1, W[("b_in", j)], ffn=False, name="proj_b")
            outs = [_attn_fwd(qkv, bias_b[g], None, out_dtype=F32, name="attn_b%d_fwd" % g, **b_geom[g]) for g in range(3)]
            o, lse = _merge_groups([t[0] for t in outs], [t[1] for t in outs])
            x1, y1 = _gated_residual(o, W[("b_out", j)], xcur, g1, w_is_transposed=True, name="out_b")
        h2, gu, act = _proj(x1, nf, sc2, sh2, W[("ffn_in", i)], ffn=True, name="ffn_in")
        x2, y2 = _gated_residual(act, W[("ffn_out", i)], x1, g2, w_is_transposed=False, name="ffn_out")
        saved.append(dict(x0=xcur, h1=h1, qkv=qkv, o=o, lse=lse, y1=y1, x1=x1, h2=h2, gu=gu, act=act, y2=y2, sink=sink_rep))
        xcur = x2

    dx, head_stats = _loss_head(xcur, target, final_norm.reshape(1, D))

    dW = {}
    dmod_rows = [None] * DEPTH
    dnm, dnf, dsink = [None] * DEPTH, [None] * DEPTH, [None] * 2
    for i in reversed(range(DEPTH)):
        j = i // 2
        sv = saved[i]
        sh1, sc1, g1, sh2, sc2, g2 = [mod[i, q] for q in range(6)]
        nm, nf = norm_mix[i].reshape(1, D), norm_ffn[i].reshape(1, D)
        dgu, dy2, st_g2 = _gate_bwd(dx, sv["y2"], g2, W[("ffn_out", i)], sv["gu"], w_is_transposed=False, name="ffn_out_bwd")
        dW[("ffn_out", i)] = _weight_grad(dy2, sv["act"], transpose_out=True, name="dw_ffn_out")
        dW[("ffn_in", i)] = _weight_grad(sv["h2"], dgu, transpose_out=True, name="dw_ffn_in")
        dx1, st_f = _norm_bwd(dgu, W[("ffn_in", i)], sv["x1"], dx, nf, sc2, name="ffn_in_bwd")
        if i % 2 == 0:
            do, dy1, st_g1 = _gate_bwd(dx1, sv["y1"], g1, W[("a_out", j)], None, w_is_transposed=False, name="out_a_bwd")
            dW[("a_out", j)] = _weight_grad(dy1, sv["o"], transpose_out=True, name="dw_a_out")
            dq, dk, dv, ds = _attn_bwd(sv["qkv"], bias_a, sv["sink"], sv["o"], do, sv["lse"], name="attn_a_bwd", **a_geom)
            dsink[j] = ds[:, 0]
            dqkv = jnp.concatenate([dq, dk, dv], axis=1)
            dW[("a_in", j)] = _weight_grad(sv["h1"], dqkv, transpose_out=True, name="dw_a_in")
            dx0, st_m = _norm_bwd(dqkv, W[("a_in", j)], sv["x0"], dx1, nm, sc1, name="proj_a_bwd")
        else:
            do, dy1, st_g1 = _gate_bwd(dx1, sv["y1"], g1, W[("b_out", j)], None, w_is_transposed=True, name="out_b_bwd")
            dW[("b_out", j)] = _weight_grad(dy1, sv["o"], transpose_out=False, name="dw_b_out").reshape(N_DEV * 64, D)
            gr = [_attn_bwd(sv["qkv"], bias_b[g], None, sv["o"], do, sv["lse"], name="attn_b%d_bwd" % g, **b_geom[g]) for g in range(3)]
            dqkv = jnp.concatenate([t[0] for t in gr] + [t[1] for t in gr] + [t[2] for t in gr], axis=1)
            dW[("b_in", j)] = _weight_grad(sv["h1"], dqkv, transpose_out=True, name="dw_b_in")
            dx0, st_m = _norm_bwd(dqkv, W[("b_in", j)], sv["x0"], dx1, nm, sc1, name="proj_b_bwd")
        dmod_rows[i] = jnp.stack([st_m[0], st_m[1], st_g1[0], st_f[0], st_f[1], st_g2[0]])
        dnm[i], dnf[i] = st_m[2], st_f[2]
        dx = dx0
    grad_x = dx.reshape(1, S, D)

    parts_g = _grad_exchange([dW[(n, l)] for n, l, _ in SEGMENTS])
    masters = {"ffn_in": (ffn_w_in, m_ffn_w_in, v_ffn_w_in), "ffn_out": (ffn_w_out, m_ffn_w_out, v_ffn_w_out),
               "a_in": (a_w_in, m_a_w_in, v_a_w_in), "a_out": (a_w_out, m_a_w_out, v_a_w_out),
               "b_in": (b_w_in, m_b_w_in, v_b_w_in), "b_out": (b_w_out, m_b_w_out, v_b_w_out)}
    rows_wmv = [jnp.concatenate([_to_rows(n, masters[n][q][l]) for n, l, _ in SEGMENTS], axis=0) for q in range(3)]
    res_rows = _adam_shard(parts_g, *rows_wmv)
    big = {}
    for q, kind in enumerate(("grad", "delta", "m", "v")):
        for n in masters:
            layers = [_from_rows(n, res_rows[q][SEG_OFF[s]:SEG_OFF[s] + SEG_ROWS[s]])
                      for s, (sn, l, _) in enumerate(SEGMENTS) if sn == n]
            big[(kind, n)] = jnp.stack(layers)

    sink_row = jnp.pad(jnp.concatenate(dsink).reshape(1, 32), ((0, 0), (0, D - 32)))
    stats = jnp.concatenate([jnp.concatenate(dmod_rows, axis=0), jnp.stack(dnm), jnp.stack(dnf), head_stats[0:2], sink_row,
                             jnp.zeros((STAT_ROWS - 35, D), F32)], axis=0)
    small = [_pack_small(*t) for t in ((ada_b, norm_mix, norm_ffn, final_norm, a_sink),
                                       (m_ada_b, m_norm_mix, m_norm_ffn, m_final_norm, m_a_sink),
                                       (v_ada_b, v_norm_mix, v_norm_ffn, v_final_norm, v_a_sink))]
    all_stats, sg, sd, sm, sv_, loss_tile = _small_exchange(stats, *small)
    loss = loss_tile[0, 0]
    dmod_all = all_stats[:, 0:24].reshape(N_DEV, DEPTH, 6 * D)
    dmod_mine = lax.dynamic_slice_in_dim(dmod_all, me * ncol, ncol, axis=2)
    dmod_pad = jnp.pad(jnp.transpose(dmod_mine, (1, 0, 2)), ((0, 0), (0, LANES - N_DEV), (0, 0))).astype(BF16)
    cond_t = jnp.pad(cond_all.T, ((0, 0), (0, LANES - N_DEV))).astype(BF16)
    ada = _adam_ada_w(cond_t, dmod_pad, ada_w, m_ada_w, v_ada_w)

    outs = [loss, grad_x]
    small_res = [_unpack_small(t) for t in (sg, sd, sm, sv_)]
    for q, kind in enumerate(("grad", "delta", "m", "v")):
        ab, nm_, nf_, fn, sk = small_res[q]
        outs += [ada[q], ab, nm_, nf_, big[(kind, "ffn_in")], big[(kind, "ffn_out")], big[(kind, "a_in")], big[(kind, "a_out")],
                 sk, big[(kind, "b_in")], big[(kind, "b_out")], fn]
    return tuple(outs)
```

```python
import functools
import math

import numpy as np
import jax
import jax.numpy as jnp
from jax import lax
from jax.experimental import pallas as pl
from jax.experimental.pallas import tpu as pltpu

D = 1024
HEAD_DIM = 64
D_FF = 2816
DEPTH = 4
N_DEV = 8
A_QKV = 1536
B_QKV = 2304
A_HALF = 128
B_HALF = 64
B_DILS = (1, 4, 16)
RMS_EPS = 1e-6
NEG = -1e30
ADAM_LR = 0.001
ADAM_B1 = 0.9
ADAM_B2 = 0.999
ADAM_EPS = 1e-08
ADAM_WD = 0.01
ADAM_STEP = 10

LANES = 128
TQ = 128
VMEM_LIMIT = 56 * 1024 * 1024
MESH = pl.DeviceIdType.MESH
F32 = jnp.float32
BF16 = jnp.bfloat16

SEGMENTS = ([("ffn_in", l, 704) for l in range(4)] + [("ffn_out", l, 352) for l in range(4)]
            + [("a_in", j, 192) for j in range(2)] + [("a_out", j, 128) for j in range(2)]
            + [("b_in", j, 288) for j in range(2)] + [("b_out", j, 64) for j in range(2)])
SEG_ROWS = [s[2] for s in SEGMENTS]
SEG_OFF = [sum(SEG_ROWS[:i]) for i in range(len(SEGMENTS))]
SHARD_ROWS = sum(SEG_ROWS)
STAT_ROWS = 40


def _nn(a, b):
    return jnp.dot(a, b, preferred_element_type=F32)


def _nt(a, b):
    return lax.dot_general(a, b, (((1,), (1,)), ((), ())), preferred_element_type=F32)


def _tn(a, b):
    return lax.dot_general(a, b, (((0,), (0,)), ((), ())), preferred_element_type=F32)


def _params(dims=None, vmem=None):
    kw = {}
    if dims is not None:
        kw["dimension_semantics"] = dims
    if vmem is not None:
        kw["vmem_limit_bytes"] = vmem
    return pltpu.CompilerParams(**kw)


def _my_index():
    return 4 * lax.axis_index("x") + 2 * lax.axis_index("y") + lax.axis_index("c")


def _peer(k):
    x, y, c = lax.axis_index("x"), lax.axis_index("y"), lax.axis_index("c")
    px, py, pc = x ^ ((k >> 2) & 1), y ^ ((k >> 1) & 1), c ^ (k & 1)
    return (px, py, pc), 4 * px + 2 * py + pc


def _const_spec(shape):
    nd = len(shape)
    return pl.BlockSpec(shape, lambda *_: (0,) * nd)


def _cond_exchange(c_tile, ada_w, ada_b_mine):
    ncol = ada_w.shape[-1]

    def body(c_ref, w_ref, b_ref, cond_ref, parts_ref, call_ref, mine_ref, send_sems, recv_sems):
        me = _my_index()
        call_ref[me] = c_ref[...]
        copies = []
        for k in range(1, N_DEV):
            dev, _ = _peer(k)
            cp = pltpu.make_async_remote_copy(src_ref=c_ref, dst_ref=call_ref.at[me], send_sem=send_sems.at[0, k - 1],
                                              recv_sem=recv_sems.at[0, k - 1], device_id=dev, device_id_type=MESH)
            cp.start()
            copies.append(cp)
        for k in range(1, N_DEV):
            _, pidx = _peer(k)
            pltpu.make_async_remote_copy(src_ref=c_ref, dst_ref=call_ref.at[pidx], send_sem=send_sems.at[0, k - 1],
                                         recv_sem=recv_sems.at[0, k - 1], device_id=_peer(k)[0], device_id_type=MESH).wait_recv()
        for cp in copies:
            cp.wait_send()
        row = lax.broadcasted_iota(jnp.int32, (N_DEV, D), 0)
        cmat = jnp.zeros((N_DEV, D), F32)
        for j in range(N_DEV):
            cmat = jnp.where(row == j, call_ref[j], cmat)
        cond = cmat * jax.nn.sigmoid(cmat)
        cond_ref[...] = cond
        cb = cond.astype(BF16)
        for l in range(DEPTH):
            mine_ref[l] = _nn(cb, w_ref[l].astype(BF16)) + b_ref[pl.ds(l, 1), :]
        parts_ref[me] = mine_ref[...]
        copies = []
        for k in range(1, N_DEV):
            dev, _ = _peer(k)
            cp = pltpu.make_async_remote_copy(src_ref=mine_ref, dst_ref=parts_ref.at[me], send_sem=send_sems.at[1, k - 1],
                                              recv_sem=recv_sems.at[1, k - 1], device_id=dev, device_id_type=MESH)
            cp.start()
            copies.append(cp)
        for k in range(1, N_DEV):
            dev, pidx = _peer(k)
            pltpu.make_async_remote_copy(src_ref=mine_ref, dst_ref=parts_ref.at[pidx], send_sem=send_sems.at[1, k - 1],
                                         recv_sem=recv_sems.at[1, k - 1], device_id=dev, device_id_type=MESH).wait_recv()
        for cp in copies:
            cp.wait_send()

    vm = pl.BlockSpec(memory_space=pltpu.VMEM)
    return pl.pallas_call(
        body, name="cond_exchange",
        out_shape=(jax.ShapeDtypeStruct((N_DEV, D), F32), jax.ShapeDtypeStruct((N_DEV, DEPTH, N_DEV, ncol), F32)),
        in_specs=[vm, vm, vm], out_specs=(vm, vm),
        scratch_shapes=[pltpu.VMEM((N_DEV, N_DEV, D), F32), pltpu.VMEM((DEPTH, N_DEV, ncol), F32),
                        pltpu.SemaphoreType.DMA((2, N_DEV - 1)), pltpu.SemaphoreType.DMA((2, N_DEV - 1))],
        compiler_params=_params(vmem=VMEM_LIMIT),
    )(c_tile, ada_w, ada_b_mine)[:2]


def _all_gather_weights(shards):
    n = len(shards)
    big = max(range(n), key=lambda s: shards[s].shape[0])
    assert N_DEV * shards[big].shape[0] >= SHARD_ROWS

    def body(*refs):
        ins, outs = refs[:n], refs[n:2 * n]
        local_sems, send_sems, recv_sems = refs[2 * n:]
        me = _my_index()
        local = []
        for s in range(n):
            rows = ins[s].shape[0]
            cp = pltpu.make_async_copy(ins[s], outs[s].at[pl.ds(me * rows, rows)], local_sems.at[s])
            cp.start()
            local.append(cp)
        for k in range(1, N_DEV):
            dev, _ = _peer(k)
            for s in range(n):
                rows = ins[s].shape[0]
                pltpu.make_async_remote_copy(src_ref=ins[s], dst_ref=outs[s].at[pl.ds(me * rows, rows)],
                                             send_sem=send_sems.at[k - 1], recv_sem=recv_sems.at[k - 1],
                                             device_id=dev, device_id_type=MESH).start()
        whole = outs[big].at[pl.ds(0, SHARD_ROWS)]
        for k in range(1, N_DEV):
            dev, _ = _peer(k)
            w = pltpu.make_async_remote_copy(src_ref=whole, dst_ref=whole, send_sem=send_sems.at[k - 1],
                                             recv_sem=recv_sems.at[k - 1], device_id=dev, device_id_type=MESH)
            w.wait_send()
            w.wait_recv()
        for cp in local:
            cp.wait()

    hbm = pl.BlockSpec(memory_space=pl.ANY)
    return pl.pallas_call(
        body, name="weight_all_gather",
        out_shape=tuple(jax.ShapeDtypeStruct((N_DEV * s.shape[0], D), s.dtype) for s in shards),
        in_specs=[hbm] * n, out_specs=tuple([hbm] * n),
        scratch_shapes=[pltpu.SemaphoreType.DMA((n,)), pltpu.SemaphoreType.DMA((N_DEV - 1,)),
                        pltpu.SemaphoreType.DMA((N_DEV - 1,))],
    )(*shards)


def _grad_exchange(fulls):
    n = len(fulls)

    def body(*refs):
        ins, out = refs[:n], refs[n]
        local_sems, send_sems, recv_sems = refs[n + 1:]
        me = _my_index()
        local = []
        for s in range(n):
            rows = SEG_ROWS[s]
            cp = pltpu.make_async_copy(ins[s].at[pl.ds(me * rows, rows)], out.at[me, pl.ds(SEG_OFF[s], rows)], local_sems.at[s])
            cp.start()
            local.append(cp)
        for k in range(1, N_DEV):
            dev, pidx = _peer(k)
            for s in range(n):
                rows = SEG_ROWS[s]
                pltpu.make_async_remote_copy(src_ref=ins[s].at[pl.ds(pidx * rows, rows)],
                                             dst_ref=out.at[me, pl.ds(SEG_OFF[s], rows)],
                                             send_sem=send_sems.at[k - 1], recv_sem=recv_sems.at[k - 1],
                                             device_id=dev, device_id_type=MESH).start()
        for k in range(1, N_DEV):
            dev, pidx = _peer(k)
            w = pltpu.make_async_remote_copy(src_ref=out.at[me], dst_ref=out.at[pidx], send_sem=send_sems.at[k - 1],
                                             recv_sem=recv_sems.at[k - 1], device_id=dev, device_id_type=MESH)
            w.wait_send()
            w.wait_recv()
        for cp in local:
            cp.wait()

    hbm = pl.BlockSpec(memory_space=pl.ANY)
    return pl.pallas_call(
        body, name="grad_exchange",
        out_shape=jax.ShapeDtypeStruct((N_DEV, SHARD_ROWS, D), BF16),
        in_specs=[hbm] * n, out_specs=hbm,
        scratch_shapes=[pltpu.SemaphoreType.DMA((n,)), pltpu.SemaphoreType.DMA((N_DEV - 1,)),
                        pltpu.SemaphoreType.DMA((N_DEV - 1,))],
    )(*fulls)


def _norm_mod(x, nw, sc, sh):
    ms = jnp.mean(x * x, axis=-1, keepdims=True)
    xh = x * lax.rsqrt(ms + RMS_EPS)
    return xh, (xh * nw) * (1.0 + sc) + sh


def _proj(x, nw, sc, sh, wt, *, ffn, name):
    S, N = x.shape[0], wt.shape[0]
    tm = 256 if ffn else 512

    def body(x_ref, nw_ref, sc_ref, sh_ref, w_ref, h_ref, out_ref, *act_ref):
        _, h = _norm_mod(x_ref[...], nw_ref[...], sc_ref[...], sh_ref[...])
        hb = h.astype(BF16)
        h_ref[...] = hb
        if ffn:
            gate = _nt(hb, w_ref[pl.ds(0, D_FF), :])
            up = _nt(hb, w_ref[pl.ds(D_FF, D_FF), :])
            out_ref[:, pl.ds(0, D_FF)] = gate.astype(BF16)
            out_ref[:, pl.ds(D_FF, D_FF)] = up.astype(BF16)
            act_ref[0][...] = ((gate * jax.nn.sigmoid(gate)) * up).astype(BF16)
        else:
            out_ref[...] = _nt(hb, w_ref[...]).astype(BF16)

    row = lambda w: pl.BlockSpec((tm, w), lambda i: (i, 0))
    out_shape = [jax.ShapeDtypeStruct((S, D), BF16), jax.ShapeDtypeStruct((S, N), BF16)]
    out_specs = [row(D), row(N)]
    if ffn:
        out_shape.append(jax.ShapeDtypeStruct((S, D_FF), BF16))
        out_specs.append(row(D_FF))
    vec = _const_spec((1, D))
    return pl.pallas_call(
        body, name=name, grid=(S // tm,), out_shape=tuple(out_shape),
        in_specs=[row(D), vec, vec, vec, _const_spec((N, D))], out_specs=tuple(out_specs),
        compiler_params=_params(("parallel",), VMEM_LIMIT),
    )(x, nw, sc, sh, wt)


def _gated_residual(a, w, x, g, *, w_is_transposed, name):
    S, K = a.shape
    tm = 512

    def body(a_ref, w_ref, x_ref, g_ref, xo_ref, y_ref):
        y = _nt(a_ref[...], w_ref[...]) if w_is_transposed else _nn(a_ref[...], w_ref[...])
        y_ref[...] = y.astype(BF16)
        xo_ref[...] = x_ref[...] + g_ref[...] * y

    row = lambda w_: pl.BlockSpec((tm, w_), lambda i: (i, 0))
    return pl.pallas_call(
        body, name=name, grid=(S // tm,),
        out_shape=(jax.ShapeDtypeStruct((S, D), F32), jax.ShapeDtypeStruct((S, D), BF16)),
        in_specs=[row(K), _const_spec(w.shape), row(D), _const_spec((1, D))], out_specs=(row(D), row(D)),
        compiler_params=_params(("parallel",), VMEM_LIMIT),
    )(a, w, x, g)


CHUNK = 1024


def _tile_rows(r):
    return min(TQ, CHUNK // r)


def _alibi_bias(slopes, half, dil):
    tq = _tile_rows(dil)
    tk = tq + 2 * half
    rel = np.arange(tk)[:, None] - half - np.arange(tq)[None, :]
    band = np.abs(rel) <= half
    dist = (dil * np.abs(rel)).astype(np.float32)
    tabs = [np.where(band, -np.float32(s) * dist, np.float32(NEG)).astype(np.float32) for s in slopes]
    out = []
    for u in range(0, len(tabs), 8):
        tab = np.concatenate(tabs[u:u + 8], axis=1)
        first, last = tab.copy(), tab.copy()
        first[:half] = NEG
        last[tk - half:] = NEG
        out += [tab, first, last]
    return jnp.asarray(np.concatenate(out, axis=0))


def _slopes(n):
    return (2.0 ** (-8.0 * np.arange(1, n + 1) / n)).astype(np.float32)


def _head_masks(tq):
    lane = lax.broadcasted_iota(jnp.int32, (tq, LANES), 1)
    lo = lane < HEAD_DIM
    return lo, jnp.logical_not(lo)


def _stack_heads(tiles, lo, hi, scale):
    blocks = []
    for t in range(4):
        xf = tiles[t] if scale == 1.0 else tiles[t] * scale
        for a in range(2):
            xm = jnp.where(lo if a == 0 else hi, xf, 0.0)
            if a != t // 2:
                xm = pltpu.roll(xm, HEAD_DIM, 1)
            blocks.append(xm.astype(BF16))
    return jnp.concatenate(blocks, axis=0)


def _tile_from_columns(x8t, t, tq):
    r0 = HEAD_DIM * (t // 2)
    top = x8t[r0:r0 + HEAD_DIM, 2 * t * tq:(2 * t + 1) * tq]
    bot = x8t[r0:r0 + HEAD_DIM, (2 * t + 1) * tq:(2 * t + 2) * tq]
    return jnp.concatenate([top, bot], axis=0).T


def _attn_layout(S, C, r, half, qoff, koff, voff):
    hb = half * r
    per = CHUNK // hb
    nhb = S // hb
    main = lambda off: pl.BlockSpec((CHUNK, LANES), lambda u, i: (i, off // LANES + u))
    prev = lambda off: pl.BlockSpec((hb, LANES), lambda u, i: (jnp.maximum(i * per - 1, 0), off // LANES + u))
    nxt = lambda off: pl.BlockSpec((hb, LANES), lambda u, i: (jnp.minimum((i + 1) * per, nhb - 1), off // LANES + u))
    specs = [pl.BlockSpec((CHUNK, 4 * LANES), lambda u, i: (i, qoff // (4 * LANES) + u))]
    specs += [prev(koff), main(koff), nxt(koff), prev(voff), main(voff), nxt(voff)]
    return specs, hb


def _stage(dst, srcs):
    row = 0
    for src in srcs:
        n = src.shape[0]
        dst[pl.ds(row, n), :] = src[...].astype(F32)
        row += n


def _rows(start, n, r):
    return pl.ds(start, n, stride=r) if r > 1 else pl.ds(start, n)


def _attn_fwd(qkv, bias, sink, *, C, r, half, qoff, koff, voff, n_units, out_dtype, name):
    S = qkv.shape[0]
    tq = _tile_rows(r)
    tk = tq + 2 * half
    tiles = CHUNK // (r * tq)
    nsteps = S // CHUNK
    specs, hb = _attn_layout(S, C, r, half, qoff, koff, voff)
    use_sink = sink is not None

    def body(*refs):
        q_ref, kp, km, kn, vp, vm, vn, bias_ref = refs[:8]
        rest = list(refs[8:])
        sink_ref = rest.pop(0) if use_sink else None
        o_ref, lse_ref, qs, ks, vs, os_, ls = rest
        i = pl.program_id(1)
        for t in range(4):
            qs[t] = q_ref[:, pl.ds(t * LANES, LANES)].astype(F32)
        _stage(ks, [kp, km, kn])
        _stage(vs, [vp, vm, vn])
        lo, hi = _head_masks(tq)
        ones = jnp.ones((16, tk), BF16)
        if use_sink:
            sk = sink_ref[pl.ds(0, 1), :]

        def chain(n, carry):
            rho, c = n // tiles, n % tiles
            start = c * (tq * r) + rho
            if r == 1:
                start = pl.multiple_of(start, tq)
            variant = jnp.where(jnp.logical_and(i == 0, c == 0), 1, 0) + jnp.where(
                jnp.logical_and(i == nsteps - 1, c == tiles - 1), 2, 0)
            k2 = ks[_rows(start, tk, r), :].astype(BF16)
            v2t = jnp.concatenate([vs[_rows(start, tk, r), :].T.astype(BF16), ones], axis=0)
            q8 = _stack_heads([qs[t, _rows(start, tq, r), :] for t in range(4)], lo, hi, HEAD_DIM ** -0.5)
            s = _nt(k2, q8) + bias_ref[pl.ds(pl.multiple_of(variant * tk, 8), tk), :]
            m = jnp.max(s, axis=0, keepdims=True)
            if use_sink:
                m = jnp.maximum(m, sk)
            pv = _nn(v2t, jnp.exp(s - m).astype(BF16))
            l = pv[LANES:LANES + 1]
            if use_sink:
                l = l + jnp.exp(sk - m)
            o8t = pv[:LANES] / l
            lse8 = jnp.broadcast_to(m + jnp.log(l), (LANES, 8 * tq))
            for t in range(4):
                os_[t, _rows(start, tq, r), :] = _tile_from_columns(o8t, t, tq)
                ls[t, _rows(start, tq, r), :] = _tile_from_columns(lse8, t, tq)
            return carry

        lax.fori_loop(0, r * tiles, chain, 0)
        for t in range(4):
            o_ref[:, pl.ds(t * LANES, LANES)] = os_[t].astype(out_dtype)
            lse_ref[:, pl.ds(t * LANES, LANES)] = ls[t]

    in_specs = specs + [pl.BlockSpec((3 * tk, 8 * tq), lambda u, i: (u, 0))]
    args = [qkv] * 7 + [bias]
    if use_sink:
        in_specs.append(pl.BlockSpec((8, 8 * tq), lambda u, i: (u, 0)))
        args.append(sink)
    wide = pl.BlockSpec((CHUNK, 4 * LANES), lambda u, i: (i, u))
    win = hb + CHUNK + hb
    return pl.pallas_call(
        body, name=name, grid=(n_units, nsteps),
        out_shape=(jax.ShapeDtypeStruct((S, n_units * 512), out_dtype), jax.ShapeDtypeStruct((S, n_units * 512), F32)),
        in_specs=in_specs, out_specs=(wide, wide),
        scratch_shapes=[pltpu.VMEM((4, CHUNK, LANES), F32), pltpu.VMEM((win, LANES), F32), pltpu.VMEM((win, LANES), F32),
                        pltpu.VMEM((4, CHUNK, LANES), F32), pltpu.VMEM((4, CHUNK, LANES), F32)],
        compiler_params=_params(("parallel", "parallel"), VMEM_LIMIT),
    )(*args)


def _attn_bwd(qkv, bias, sink, o, do, lse, *, C, r, half, qoff, koff, voff, n_units, name):
    S = qkv.shape[0]
    tq = _tile_rows(r)
    tk = tq + 2 * half
    tiles = CHUNK // (r * tq)
    nsteps = S // CHUNK
    specs, hb = _attn_layout(S, C, r, half, qoff, koff, voff)
    use_sink = sink is not None

    def body(*refs):
        q_ref, kp, km, kn, vp, vm, vn, bias_ref = refs[:8]
        rest = list(refs[8:])
        sink_ref = rest.pop(0) if use_sink else None
        o_ref, do_ref, lse_ref, dq_ref, dk_hbm, dv_hbm = rest[:6]
        rest = rest[6:]
        dsink_ref = rest.pop(0) if use_sink else None
        qs, ks, vs, os_, dos, ls, dqs, acck, accv, sem = rest
        u, i = pl.program_id(0), pl.program_id(1)

        @pl.when(i == 0)
        def _():
            acck[...] = jnp.zeros_like(acck)
            accv[...] = jnp.zeros_like(accv)
            if use_sink:
                dsink_ref[...] = jnp.zeros_like(dsink_ref)

        for t in range(4):
            cols = pl.ds(t * LANES, LANES)
            qs[t] = q_ref[:, cols].astype(F32)
            os_[t] = o_ref[:, cols].astype(F32)
            dos[t] = do_ref[:, cols].astype(F32)
            ls[t] = lse_ref[:, cols]
        _stage(ks, [kp, km, kn])
        _stage(vs, [vp, vm, vn])
        lo, hi = _head_masks(tq)
        base = pl.multiple_of(i * CHUNK, CHUNK)
        if use_sink:
            sk = sink_ref[pl.ds(0, 1), :]

        def chain(n, carry):
            rho, c = n // tiles, n % tiles
            start = c * (tq * r) + rho
            if r == 1:
                start = pl.multiple_of(start, tq)
            variant = jnp.where(jnp.logical_and(i == 0, c == 0), 1, 0) + jnp.where(
                jnp.logical_and(i == nsteps - 1, c == tiles - 1), 2, 0)
            k2 = ks[_rows(start, tk, r), :].astype(BF16)
            v2 = vs[_rows(start, tk, r), :].astype(BF16)
            k2t = ks[_rows(start, tk, r), :].T.astype(BF16)
            q8 = _stack_heads([qs[t, _rows(start, tq, r), :] for t in range(4)], lo, hi, HEAD_DIM ** -0.5)
            do_tiles = [dos[t, _rows(start, tq, r), :] for t in range(4)]
            do8 = _stack_heads(do_tiles, lo, hi, 1.0)
            deltas, lses = [], []
            for t in range(4):
                prod_t = (do_tiles[t] * os_[t, _rows(start, tq, r), :]).T
                lse_t = ls[t, _rows(start, tq, r), :].T
                for a in range(2):
                    deltas.append(jnp.sum(prod_t[a * HEAD_DIM:(a + 1) * HEAD_DIM], axis=0, keepdims=True))
                    lses.append(lse_t[a * HEAD_DIM:a * HEAD_DIM + 1])
            delta8 = jnp.concatenate(deltas, axis=1)
            lse8 = jnp.concatenate(lses, axis=1)
            s = _nt(k2, q8) + bias_ref[pl.ds(pl.multiple_of(variant * tk, 8), tk), :]
            p = jnp.exp(s - lse8)
            dp = _nt(v2, do8)
            dsb = (p * (dp - delta8)).astype(BF16)
            dq8t = _nn(k2t, dsb)
            for t in range(4):
                dqs[t, _rows(start, tq, r), :] = _tile_from_columns(dq8t, t, tq) * (HEAD_DIM ** -0.5)
            arow = base + start
            if r == 1:
                arow = pl.multiple_of(arow, tq)
            acck[_rows(arow, tk, r), :] = acck[_rows(arow, tk, r), :] + _nn(dsb, q8)
            accv[_rows(arow, tk, r), :] = accv[_rows(arow, tk, r), :] + _nn(p.astype(BF16), do8)
            if use_sink:
                e = jnp.exp(sk - lse8) * delta8
                for h in range(8):
                    part = -jnp.sum(e[:, h * tq:(h + 1) * tq], axis=1, keepdims=True)
                    dsink_ref[pl.ds(h, 1), :] = dsink_ref[pl.ds(h, 1), :] + part
            return carry

        lax.fori_loop(0, r * tiles, chain, 0)
        for t in range(4):
            dq_ref[:, pl.ds(t * LANES, LANES)] = dqs[t].astype(BF16)

        @pl.when(i == nsteps - 1)
        def _():
            ck = pltpu.make_async_copy(acck.at[pl.ds(hb, S)], dk_hbm.at[u], sem.at[0])
            cv = pltpu.make_async_copy(accv.at[pl.ds(hb, S)], dv_hbm.at[u], sem.at[1])
            ck.start()
            cv.start()
            ck.wait()
            cv.wait()

    wide = pl.BlockSpec((CHUNK, 4 * LANES), lambda u, i: (i, u))
    hbm = pl.BlockSpec(memory_space=pl.ANY)
    in_specs = specs + [pl.BlockSpec((3 * tk, 8 * tq), lambda u, i: (u, 0))]
    args = [qkv] * 7 + [bias]
    if use_sink:
        in_specs.append(pl.BlockSpec((8, 8 * tq), lambda u, i: (u, 0)))
        args.append(sink)
    in_specs += [wide, wide, wide]
    args += [o, do, lse]
    out_shape = [jax.ShapeDtypeStruct((S, n_units * 512), BF16), jax.ShapeDtypeStruct((n_units, S, LANES), F32),
                 jax.ShapeDtypeStruct((n_units, S, LANES), F32)]
    out_specs = [wide, hbm, hbm]
    if use_sink:
        out_shape.append(jax.ShapeDtypeStruct((n_units * 8, LANES), F32))
        out_specs.append(pl.BlockSpec((8, LANES), lambda u, i: (u, 0)))
    win = hb + CHUNK + hb
    big = lambda: pltpu.VMEM((4, CHUNK, LANES), F32)
    res = pl.pallas_call(
        body, name=name, grid=(n_units, nsteps), out_shape=tuple(out_shape), in_specs=in_specs, out_specs=tuple(out_specs),
        scratch_shapes=[big(), pltpu.VMEM((win, LANES), F32), pltpu.VMEM((win, LANES), F32), big(), big(), big(), big(),
                        pltpu.VMEM((S + 2 * hb, LANES), F32), pltpu.VMEM((S + 2 * hb, LANES), F32), pltpu.SemaphoreType.DMA((2,))],
        compiler_params=_params(("arbitrary", "arbitrary"), VMEM_LIMIT),
    )(*args)
    return res[0], res[1], res[2], (res[3] if use_sink else None)


def _merge_groups(os_, lses):
    S, W = os_[0].shape
    tm = 512

    def body(o0, o1, o2, l0, l1, l2, o_ref, lse_ref):
        ls = [l0[...], l1[...], l2[...]]
        mx = jnp.maximum(jnp.maximum(ls[0], ls[1]), ls[2])
        es = [jnp.exp(l - mx) for l in ls]
        den = es[0] + es[1] + es[2]
        o = (es[0] / den) * o0[...] + (es[1] / den) * o1[...] + (es[2] / den) * o2[...]
        o_ref[...] = o.astype(BF16)
        lse_ref[...] = mx + jnp.log(den)

    row = pl.BlockSpec((tm, W), lambda i: (i, 0))
    return pl.pallas_call(
        body, name="merge_groups", grid=(S // tm,),
        out_shape=(jax.ShapeDtypeStruct((S, W), BF16), jax.ShapeDtypeStruct((S, W), F32)),
        in_specs=[row] * 6, out_specs=(row, row), compiler_params=_params(("parallel",), VMEM_LIMIT),
    )(*os_, *lses)


def _loss_head(x, target, fnw):
    S = x.shape[0]
    tm = 512

    def body(x_ref, t_ref, w_ref, dx_ref, st_ref):
        @pl.when(pl.program_id(0) == 0)
        def _():
            st_ref[...] = jnp.zeros_like(st_ref)

        xv = x_ref[...]
        rstd = lax.rsqrt(jnp.mean(xv * xv, axis=-1, keepdims=True) + RMS_EPS)
        xh = xv * rstd
        err = xh * w_ref[...] - t_ref[...]
        dy = err * (1.0 / D)
        dxh = dy * w_ref[...]
        dx_ref[...] = rstd * (dxh - xh * jnp.mean(dxh * xh, axis=-1, keepdims=True))
        st_ref[pl.ds(0, 1), :] = st_ref[pl.ds(0, 1), :] + jnp.sum(dy * xh, axis=0, keepdims=True)
        st_ref[pl.ds(1, 1), :] = st_ref[pl.ds(1, 1), :] + jnp.sum(err * err, axis=0, keepdims=True)

    row = pl.BlockSpec((tm, D), lambda i: (i, 0))
    return pl.pallas_call(
        body, name="loss_head", grid=(S // tm,),
        out_shape=(jax.ShapeDtypeStruct((S, D), F32), jax.ShapeDtypeStruct((8, D), F32)),
        in_specs=[row, row, _const_spec((1, D))], out_specs=(row, _const_spec((8, D))),
        compiler_params=_params(("arbitrary",), VMEM_LIMIT),
    )(x, target, fnw)


def _gate_bwd(dx, y, g, w, gu, *, w_is_transposed, name):
    S = dx.shape[0]
    K = w.shape[1] if w_is_transposed else w.shape[0]
    ffn = gu is not None
    tm = 256 if ffn else 512
    wout = 2 * K if ffn else K

    def body(dx_ref, y_ref, g_ref, w_ref, *rest):
        if ffn:
            gu_ref, da_ref, dyb_ref, st_ref = rest
        else:
            da_ref, dyb_ref, st_ref = rest

        @pl.when(pl.program_id(0) == 0)
        def _():
            st_ref[...] = jnp.zeros_like(st_ref)

        dxv = dx_ref[...]
        st_ref[pl.ds(0, 1), :] = st_ref[pl.ds(0, 1), :] + jnp.sum(dxv * y_ref[...].astype(F32), axis=0, keepdims=True)
        dyb = (dxv * g_ref[...]).astype(BF16)
        dyb_ref[...] = dyb
        da = _nn(dyb, w_ref[...]) if w_is_transposed else _nt(dyb, w_ref[...])
        if ffn:
            gate = gu_ref[:, pl.ds(0, K)].astype(F32)
            up = gu_ref[:, pl.ds(K, K)].astype(F32)
            sig = jax.nn.sigmoid(gate)
            da_ref[:, pl.ds(0, K)] = (da * up * (sig * (1.0 + gate * (1.0 - sig)))).astype(BF16)
            da_ref[:, pl.ds(K, K)] = (da * (gate * sig)).astype(BF16)
        else:
            da_ref[...] = da.astype(BF16)

    row = lambda w_: pl.BlockSpec((tm, w_), lambda i: (i, 0))
    in_specs = [row(D), row(D), _const_spec((1, D)), _const_spec(w.shape)]
    args = [dx, y, g, w]
    if ffn:
        in_specs.append(row(wout))
        args.append(gu)
    return pl.pallas_call(
        body, name=name, grid=(S // tm,),
        out_shape=(jax.ShapeDtypeStruct((S, wout), BF16), jax.ShapeDtypeStruct((S, D), BF16), jax.ShapeDtypeStruct((8, D), F32)),
        in_specs=in_specs, out_specs=(row(wout), row(D), _const_spec((8, D))),
        compiler_params=_params(("arbitrary",), VMEM_LIMIT),
    )(*args)


def _norm_bwd(dy, wt, x, dres, nw, sc, *, name):
    S, N = dy.shape
    tm = 256 if N > 4096 else 512

    def body(dy_ref, w_ref, x_ref, dres_ref, nw_ref, sc_ref, dx_ref, st_ref):
        @pl.when(pl.program_id(0) == 0)
        def _():
            st_ref[...] = jnp.zeros_like(st_ref)

        dh = _nn(dy_ref[...], w_ref[...])
        xv = x_ref[...]
        rstd = lax.rsqrt(jnp.mean(xv * xv, axis=-1, keepdims=True) + RMS_EPS)
        xh = xv * rstd
        nwv, scale = nw_ref[...], 1.0 + sc_ref[...]
        dxh = dh * (nwv * scale)
        dx_ref[...] = dres_ref[...] + rstd * (dxh - xh * jnp.mean(dxh * xh, axis=-1, keepdims=True))
        dhx = dh * xh
        st_ref[pl.ds(0, 1), :] = st_ref[pl.ds(0, 1), :] + jnp.sum(dh, axis=0, keepdims=True)
        st_ref[pl.ds(1, 1), :] = st_ref[pl.ds(1, 1), :] + jnp.sum(dhx * nwv, axis=0, keepdims=True)
        st_ref[pl.ds(2, 1), :] = st_ref[pl.ds(2, 1), :] + jnp.sum(dhx * scale, axis=0, keepdims=True)

    row = lambda w_: pl.BlockSpec((tm, w_), lambda i: (i, 0))
    vec = _const_spec((1, D))
    return pl.pallas_call(
        body, name=name, grid=(S // tm,),
        out_shape=(jax.ShapeDtypeStruct((S, D), F32), jax.ShapeDtypeStruct((8, D), F32)),
        in_specs=[row(N), _const_spec((N, D)), row(D), row(D), vec, vec], out_specs=(row(D), _const_spec((8, D))),
        compiler_params=_params(("arbitrary",), VMEM_LIMIT),
    )(dy, wt, x, dres, nw, sc)


def _weight_grad(a, b, *, transpose_out, name):
    S, N = b.shape
    nb = N // 2 if N > 4096 else N
    tk = 512

    def body(a_ref, b_ref, out_ref, acc):
        k = pl.program_id(1)

        @pl.when(k == 0)
        def _():
            acc[...] = jnp.zeros_like(acc)

        acc[...] += _tn(a_ref[...], b_ref[...])

        @pl.when(k == pl.num_programs(1) - 1)
        def _():
            out_ref[...] = (acc[...].T if transpose_out else acc[...]).astype(BF16)

    out_block = pl.BlockSpec((nb, D), lambda n, k: (n, 0)) if transpose_out else pl.BlockSpec((D, nb), lambda n, k: (0, n))
    return pl.pallas_call(
        body, name=name, grid=(N // nb, S // tk),
        out_shape=jax.ShapeDtypeStruct((N, D) if transpose_out else (D, N), BF16),
        in_specs=[pl.BlockSpec((tk, D), lambda n, k: (k, 0)), pl.BlockSpec((tk, nb), lambda n, k: (k, n))],
        out_specs=out_block, scratch_shapes=[pltpu.VMEM((D, nb), F32)],
        compiler_params=_params(("parallel", "arbitrary"), VMEM_LIMIT),
    )(a, b)


def _adamw(w, g, m, v):
    m = ADAM_B1 * m + (1.0 - ADAM_B1) * g
    v = ADAM_B2 * v + (1.0 - ADAM_B2) * (g * g)
    m_hat = m / (1.0 - ADAM_B1 ** ADAM_STEP)
    v_hat = v / (1.0 - ADAM_B2 ** ADAM_STEP)
    delta = -ADAM_LR * (m_hat / (jnp.sqrt(v_hat) + ADAM_EPS) + ADAM_WD * w)
    return delta, m, v


def _adam_shard(parts, w, m, v):
    R = w.shape[0]
    tr = 192

    def body(p_ref, w_ref, m_ref, v_ref, g_out, d_out, m_out, v_out):
        g = p_ref[0].astype(F32)
        for j in range(1, N_DEV):
            g = g + p_ref[j].astype(F32)
        delta, mn, vn = _adamw(w_ref[...], g, m_ref[...], v_ref[...])
        g_out[...] = g
        d_out[...] = delta
        m_out[...] = mn
        v_out[...] = vn

    row = pl.BlockSpec((tr, D), lambda i: (i, 0))
    shp = jax.ShapeDtypeStruct((R, D), F32)
    return pl.pallas_call(
        body, name="adam_shard", grid=(R // tr,), out_shape=(shp,) * 4,
        in_specs=[pl.BlockSpec((N_DEV, tr, D), lambda i: (0, i, 0)), row, row, row], out_specs=(row,) * 4,
        compiler_params=_params(("parallel",), VMEM_LIMIT),
    )(parts, w, m, v)


def _adam_ada_w(cond_t, dmod, w, m, v):
    ncol = w.shape[-1]
    tr = 512

    def body(c_ref, d_ref, w_ref, m_ref, v_ref, g_out, d_out, m_out, v_out):
        g = _nn(c_ref[...], d_ref[0])
        delta, mn, vn = _adamw(w_ref[0], g, m_ref[0], v_ref[0])
        g_out[0] = g
        d_out[0] = delta
        m_out[0] = mn
        v_out[0] = vn

    blk = pl.BlockSpec((1, tr, ncol), lambda l, i: (l, i, 0))
    shp = jax.ShapeDtypeStruct(w.shape, F32)
    return pl.pallas_call(
        body, name="adam_ada_w", grid=(DEPTH, D // tr), out_shape=(shp,) * 4,
        in_specs=[pl.BlockSpec((tr, LANES), lambda l, i: (i, 0)), pl.BlockSpec((1, LANES, ncol), lambda l, i: (l, 0, 0)), blk, blk, blk],
        out_specs=(blk,) * 4, compiler_params=_params(("parallel", "parallel"), VMEM_LIMIT),
    )(cond_t, dmod, w, m, v)


def _small_exchange(stats, w, m, v):
    loss_row = 33

    def body(s_ref, w_ref, m_ref, v_ref, all_ref, g_out, d_out, m_out, v_out, loss_out, send_sems, recv_sems):
        me = _my_index()
        all_ref[me] = s_ref[...]
        copies = []
        for k in range(1, N_DEV):
            dev, _ = _peer(k)
            cp = pltpu.make_async_remote_copy(src_ref=s_ref, dst_ref=all_ref.at[me], send_sem=send_sems.at[k - 1],
                                              recv_sem=recv_sems.at[k - 1], device_id=dev, device_id_type=MESH)
            cp.start()
            copies.append(cp)
        for k in range(1, N_DEV):
            dev, pidx = _peer(k)
            pltpu.make_async_remote_copy(src_ref=s_ref, dst_ref=all_ref.at[pidx], send_sem=send_sems.at[k - 1],
                                         recv_sem=recv_sems.at[k - 1], device_id=dev, device_id_type=MESH).wait_recv()
        for cp in copies:
            cp.wait_send()
        g = all_ref[0]
        for j in range(1, N_DEV):
            g = g + all_ref[j]
        delta, mn, vn = _adamw(w_ref[...], g, m_ref[...], v_ref[...])
        g_out[...] = g
        d_out[...] = delta
        m_out[...] = mn
        v_out[...] = vn
        tot = jnp.sum(g[loss_row:loss_row + 1, :], axis=-1, keepdims=True) * (0.5 / D)
        loss_out[...] = jnp.broadcast_to(tot, loss_out.shape)

    vm = pl.BlockSpec(memory_space=pltpu.VMEM)
    shp = jax.ShapeDtypeStruct((STAT_ROWS, D), F32)
    return pl.pallas_call(
        body, name="small_exchange",
        out_shape=(jax.ShapeDtypeStruct((N_DEV, STAT_ROWS, D), F32), shp, shp, shp, shp, jax.ShapeDtypeStruct((8, LANES), F32)),
        in_specs=[vm] * 4, out_specs=(vm,) * 6,
        scratch_shapes=[pltpu.SemaphoreType.DMA((N_DEV - 1,)), pltpu.SemaphoreType.DMA((N_DEV - 1,))],
        compiler_params=_params(vmem=VMEM_LIMIT),
    )(stats, w, m, v)


def _to_rows(name, a):
    if name in ("ffn_in", "a_in", "b_in"):
        return a.T
    if name == "b_out":
        return a.T.reshape(-1, D)
    return a


def _from_rows(name, a):
    if name in ("ffn_in", "a_in", "b_in"):
        return a.T
    if name == "b_out":
        return a.reshape(-1, 512).T
    return a


def _pack_small(ada_b, norm_mix, norm_ffn, final_norm, sink):
    pad = jnp.zeros((1, D), F32)
    sink_row = jnp.pad(sink.reshape(1, -1), ((0, 0), (0, D - sink.size)))
    rows = [ada_b.reshape(24, D), norm_mix, norm_ffn, final_norm.reshape(1, D), pad, sink_row]
    rows.append(jnp.zeros((STAT_ROWS - 35, D), F32))
    return jnp.concatenate(rows, axis=0)


def _unpack_small(a):
    return a[0:24].reshape(4, 6 * D), a[24:28], a[28:32], a[32], a[34, :32].reshape(2, 16)


def kernel(x, c, ada_w, ada_b, norm_mix, norm_ffn, ffn_w_in, ffn_w_out, a_w_in, a_w_out, a_sink, b_w_in, b_w_out, final_norm, loss_target, m_ada_w, m_ada_b, m_norm_mix, m_norm_ffn, m_ffn_w_in, m_ffn_w_out, m_a_w_in, m_a_w_out, m_a_sink, m_b_w_in, m_b_w_out, m_final_norm, v_ada_w, v_ada_b, v_norm_mix, v_norm_ffn, v_ffn_w_in, v_ffn_w_out, v_a_w_in, v_a_w_out, v_a_sink, v_b_w_in, v_b_w_out, v_final_norm):
    S = x.shape[1]
    x0 = x.reshape(S, D)
    target = loss_target.reshape(S, D)
    me = _my_index()
    ncol = ada_w.shape[-1]

    ada_b_mine = lax.dynamic_slice_in_dim(ada_b, me * ncol, ncol, axis=1)
    cond_all, parts = _cond_exchange(jnp.broadcast_to(c.reshape(1, D), (8, D)), ada_w, ada_b_mine)
    mod = lax.dynamic_index_in_dim(parts, me, axis=2, keepdims=False)
    mod = jnp.transpose(mod, (1, 0, 2)).reshape(DEPTH, 6, 1, D)

    weights = {"ffn_in": ffn_w_in, "ffn_out": ffn_w_out, "a_in": a_w_in, "a_out": a_w_out, "b_in": b_w_in, "b_out": b_w_out}
    shards = [_to_rows(n, weights[n][l]).astype(BF16) for n, l, _ in SEGMENTS]
    gathered = _all_gather_weights(shards)
    W = {(n, l): g for (n, l, _), g in zip(SEGMENTS, gathered)}
    for j in range(2):
        W[("b_out", j)] = W[("b_out", j)].reshape(D, 512)

    a_slopes, b_slopes = _slopes(16), _slopes(24)
    bias_a = _alibi_bias(a_slopes, A_HALF, 1)
    bias_b = [_alibi_bias(b_slopes[8 * g:8 * g + 8], B_HALF, dil) for g, dil in enumerate(B_DILS)]
    a_geom = dict(C=A_QKV, r=1, half=A_HALF, qoff=0, koff=1024, voff=1280, n_units=2)
    b_geom = [dict(C=B_QKV, r=dil, half=B_HALF, qoff=512 * g, koff=1536 + 128 * g, voff=1920 + 128 * g, n_units=1)
              for g, dil in enumerate(B_DILS)]

    saved = []
    xcur = x0
    for i in range(DEPTH):
        j = i // 2
        sh1, sc1, g1, sh2, sc2, g2 = [mod[i, q] for q in range(6)]
        nm, nf = norm_mix[i].reshape(1, D), norm_ffn[i].reshape(1, D)
        if i % 2 == 0:
            sink_rep = jnp.repeat(jnp.repeat(a_sink[j], TQ).reshape(2, 1, 8 * TQ), 8, axis=1).reshape(16, 8 * TQ)
            h1, qkv = _proj(xcur, nm, sc1, sh1, W[("a_in", j)], ffn=False, name="proj_a")
            o, lse = _attn_fwd(qkv, bias_a, sink_rep, out_dtype=BF16, name="attn_a_fwd", **a_geom)
            x1, y1 = _gated_residual(o, W[("a_out", j)], xcur, g1, w_is_transposed=False, name="out_a")
        else:
            sink_rep = None
            h1, qkv = _proj(xcur, nm, sc1, sh1, W[("b_in", j)], ffn=False, name="proj_b")
            outs = [_attn_fwd(qkv, bias_b[g], None, out_dtype=F32, name="attn_b%d_fwd" % g, **b_geom[g]) for g in range(3)]
            o, lse = _merge_groups([t[0] for t in outs], [t[1] for t in outs])
            x1, y1 = _gated_residual(o, W[("b_out", j)], xcur, g1, w_is_transposed=True, name="out_b")
        h2, gu, act = _proj(x1, nf, sc2, sh2, W[("ffn_in", i)], ffn=True, name="ffn_in")
        x2, y2 = _gated_residual(act, W[("ffn_out", i)], x1, g2, w_is_transposed=False, name="ffn_out")
        saved.append(dict(x0=xcur, h1=h1, qkv=qkv, o=o, lse=lse, y1=y1, x1=x1, h2=h2, gu=gu, act=act, y2=y2, sink=sink_rep))
        xcur = x2

    dx, head_stats = _loss_head(xcur, target, final_norm.reshape(1, D))

    dW = {}
    dmod_rows = [None] * DEPTH
    dnm, dnf, dsink = [None] * DEPTH, [None] * DEPTH, [None] * 2
    for i in reversed(range(DEPTH)):
        j = i // 2
        sv = saved[i]
        sh1, sc1, g1, sh2, sc2, g2 = [mod[i, q] for q in range(6)]
        nm, nf = norm_mix[i].reshape(1, D), norm_ffn[i].reshape(1, D)
        dgu, dy2, st_g2 = _gate_bwd(dx, sv["y2"], g2, W[("ffn_out", i)], sv["gu"], w_is_transposed=False, name="ffn_out_bwd")
        dW[("ffn_out", i)] = _weight_grad(dy2, sv["act"], transpose_out=True, name="dw_ffn_out")
        dW[("ffn_in", i)] = _weight_grad(sv["h2"], dgu, transpose_out=True, name="dw_ffn_in")
        dx1, st_f = _norm_bwd(dgu, W[("ffn_in", i)], sv["x1"], dx, nf, sc2, name="ffn_in_bwd")
        if i % 2 == 0:
            do, dy1, st_g1 = _gate_bwd(dx1, sv["y1"], g1, W[("a_out", j)], None, w_is_transposed=False, name="out_a_bwd")
            dW[("a_out", j)] = _weight_grad(dy1, sv["o"], transpose_out=True, name="dw_a_out")
            dq, dk, dv, ds = _attn_bwd(sv["qkv"], bias_a, sv["sink"], sv["o"], do, sv["lse"], name="attn_a_bwd", **a_geom)
            dsink[j] = ds[:, 0]
            dqkv = jnp.concatenate([dq, dk[0].astype(BF16), dk[1].astype(BF16), dv[0].astype(BF16), dv[1].astype(BF16)], axis=1)
            dW[("a_in", j)] = _weight_grad(sv["h1"], dqkv, transpose_out=True, name="dw_a_in")
            dx0, st_m = _norm_bwd(dqkv, W[("a_in", j)], sv["x0"], dx1, nm, sc1, name="proj_a_bwd")
        else:
            do, dy1, st_g1 = _gate_bwd(dx1, sv["y1"], g1, W[("b_out", j)], None, w_is_transposed=True, name="out_b_bwd")
            dW[("b_out", j)] = _weight_grad(dy1, sv["o"], transpose_out=False, name="dw_b_out").reshape(N_DEV * 64, D)
            gr = [_attn_bwd(sv["qkv"], bias_b[g], None, sv["o"], do, sv["lse"], name="attn_b%d_bwd" % g, **b_geom[g]) for g in range(3)]
            dqkv = jnp.concatenate([t[0] for t in gr] + [t[1][0].astype(BF16) for t in gr] + [t[2][0].astype(BF16) for t in gr], axis=1)
            dW[("b_in", j)] = _weight_grad(sv["h1"], dqkv, transpose_out=True, name="dw_b_in")
            dx0, st_m = _norm_bwd(dqkv, W[("b_in", j)], sv["x0"], dx1, nm, sc1, name="proj_b_bwd")
        dmod_rows[i] = jnp.stack([st_m[0], st_m[1], st_g1[0], st_f[0], st_f[1], st_g2[0]])
        dnm[i], dnf[i] = st_m[2], st_f[2]
        dx = dx0
    grad_x = dx.reshape(1, S, D)

    parts_g = _grad_exchange([dW[(n, l)] for n, l, _ in SEGMENTS])
    masters = {"ffn_in": (ffn_w_in, m_ffn_w_in, v_ffn_w_in), "ffn_out": (ffn_w_out, m_ffn_w_out, v_ffn_w_out),
               "a_in": (a_w_in, m_a_w_in, v_a_w_in), "a_out": (a_w_out, m_a_w_out, v_a_w_out),
               "b_in": (b_w_in, m_b_w_in, v_b_w_in), "b_out": (b_w_out, m_b_w_out, v_b_w_out)}
    rows_wmv = [jnp.concatenate([_to_rows(n, masters[n][q][l]) for n, l, _ in SEGMENTS], axis=0) for q in range(3)]
    res_rows = _adam_shard(parts_g, *rows_wmv)
    big = {}
    for q, kind in enumerate(("grad", "delta", "m", "v")):
        for n in masters:
            layers = [_from_rows(n, res_rows[q][SEG_OFF[s]:SEG_OFF[s] + SEG_ROWS[s]])
                      for s, (sn, l, _) in enumerate(SEGMENTS) if sn == n]
            big[(kind, n)] = jnp.stack(layers)

    sink_row = jnp.pad(jnp.concatenate(dsink).reshape(1, 32), ((0, 0), (0, D - 32)))
    stats = jnp.concatenate([jnp.concatenate(dmod_rows, axis=0), jnp.stack(dnm), jnp.stack(dnf), head_stats[0:2], sink_row,
                             jnp.zeros((STAT_ROWS - 35, D), F32)], axis=0)
    small = [_pack_small(*t) for t in ((ada_b, norm_mix, norm_ffn, final_norm, a_sink),
                                       (m_ada_b, m_norm_mix, m_norm_ffn, m_final_norm, m_a_sink),
                                       (v_ada_b, v_norm_mix, v_norm_ffn, v_final_norm, v_a_sink))]
    all_stats, sg, sd, sm, sv_, loss_tile = _small_exchange(stats, *small)
    loss = loss_tile[0, 0]
    dmod_all = all_stats[:, 0:24].reshape(N_DEV, DEPTH, 6 * D)
    dmod_mine = lax.dynamic_slice_in_dim(dmod_all, me * ncol, ncol, axis=2)
    dmod_pad = jnp.pad(jnp.transpose(dmod_mine, (1, 0, 2)), ((0, 0), (0, LANES - N_DEV), (0, 0))).astype(BF16)
    cond_t = jnp.pad(cond_all.T, ((0, 0), (0, LANES - N_DEV))).astype(BF16)
    ada = _adam_ada_w(cond_t, dmod_pad, ada_w, m_ada_w, v_ada_w)

    outs = [loss, grad_x]
    small_res = [_unpack_small(t) for t in (sg, sd, sm, sv_)]
    for q, kind in enumerate(("grad", "delta", "m", "v")):
        ab, nm_, nf_, fn, sk = small_res[q]
        outs += [ada[q], ab, nm_, nf_, big[(kind, "ffn_in")], big[(kind, "ffn_out")], big[(kind, "a_in")], big[(kind, "a_out")],
                 sk, big[(kind, "b_in")], big[(kind, "b_out")], fn]
    return tuple(outs)
```

```python
import functools
import math

import numpy as np
import jax
import jax.numpy as jnp
from jax import lax
from jax.experimental import pallas as pl
from jax.experimental.pallas import tpu as pltpu

D = 1024
HEAD_DIM = 64
D_FF = 2816
DEPTH = 4
N_DEV = 8
A_QKV = 1536
B_QKV = 2304
A_HALF = 128
B_HALF = 64
B_DILS = (1, 4, 16)
RMS_EPS = 1e-6
NEG = -1e30
ADAM_LR = 0.001
ADAM_B1 = 0.9
ADAM_B2 = 0.999
ADAM_EPS = 1e-08
ADAM_WD = 0.01
ADAM_STEP = 10

LANES = 128
TQ = 128
VMEM_LIMIT = 56 * 1024 * 1024
MESH = pl.DeviceIdType.MESH
F32 = jnp.float32
BF16 = jnp.bfloat16

SEGMENTS = ([("ffn_in", l, 704) for l in range(4)] + [("ffn_out", l, 352) for l in range(4)]
            + [("a_in", j, 192) for j in range(2)] + [("a_out", j, 128) for j in range(2)]
            + [("b_in", j, 288) for j in range(2)] + [("b_out", j, 64) for j in range(2)])
SEG_ROWS = [s[2] for s in SEGMENTS]
SEG_OFF = [sum(SEG_ROWS[:i]) for i in range(len(SEGMENTS))]
SHARD_ROWS = sum(SEG_ROWS)
STAT_ROWS = 56


def _nn(a, b):
    return jnp.dot(a, b, preferred_element_type=F32)


def _nt(a, b):
    return lax.dot_general(a, b, (((1,), (1,)), ((), ())), preferred_element_type=F32)


def _tn(a, b):
    return lax.dot_general(a, b, (((0,), (0,)), ((), ())), preferred_element_type=F32)


def _params(dims=None, vmem=None):
    kw = {}
    if dims is not None:
        kw["dimension_semantics"] = dims
    if vmem is not None:
        kw["vmem_limit_bytes"] = vmem
    return pltpu.CompilerParams(**kw)


def _my_index():
    return 4 * lax.axis_index("x") + 2 * lax.axis_index("y") + lax.axis_index("c")


def _peer(k):
    x, y, c = lax.axis_index("x"), lax.axis_index("y"), lax.axis_index("c")
    px, py, pc = x ^ ((k >> 2) & 1), y ^ ((k >> 1) & 1), c ^ (k & 1)
    return (px, py, pc), 4 * px + 2 * py + pc


def _const_spec(shape):
    nd = len(shape)
    return pl.BlockSpec(shape, lambda *_: (0,) * nd)


def _cond_exchange(c_tile, ada_w, ada_b_mine):
    ncol = ada_w.shape[-1]

    def body(c_ref, w_ref, b_ref, cond_ref, parts_ref, call_ref, mine_ref, send_sems, recv_sems):
        me = _my_index()
        call_ref[me] = c_ref[...]
        copies = []
        for k in range(1, N_DEV):
            dev, _ = _peer(k)
            cp = pltpu.make_async_remote_copy(src_ref=c_ref, dst_ref=call_ref.at[me], send_sem=send_sems.at[0, k - 1],
                                              recv_sem=recv_sems.at[0, k - 1], device_id=dev, device_id_type=MESH)
            cp.start()
            copies.append(cp)
        for k in range(1, N_DEV):
            _, pidx = _peer(k)
            pltpu.make_async_remote_copy(src_ref=c_ref, dst_ref=call_ref.at[pidx], send_sem=send_sems.at[0, k - 1],
                                         recv_sem=recv_sems.at[0, k - 1], device_id=_peer(k)[0], device_id_type=MESH).wait_recv()
        for cp in copies:
            cp.wait_send()
        row = lax.broadcasted_iota(jnp.int32, (N_DEV, D), 0)
        cmat = jnp.zeros((N_DEV, D), F32)
        for j in range(N_DEV):
            cmat = jnp.where(row == j, call_ref[j], cmat)
        cond = cmat * jax.nn.sigmoid(cmat)
        cond_ref[...] = cond
        cb = cond.astype(BF16)
        for l in range(DEPTH):
            mine_ref[l] = _nn(cb, w_ref[l].astype(BF16)) + b_ref[pl.ds(l, 1), :]
        parts_ref[me] = mine_ref[...]
        copies = []
        for k in range(1, N_DEV):
            dev, _ = _peer(k)
            cp = pltpu.make_async_remote_copy(src_ref=mine_ref, dst_ref=parts_ref.at[me], send_sem=send_sems.at[1, k - 1],
                                              recv_sem=recv_sems.at[1, k - 1], device_id=dev, device_id_type=MESH)
            cp.start()
            copies.append(cp)
        for k in range(1, N_DEV):
            dev, pidx = _peer(k)
            pltpu.make_async_remote_copy(src_ref=mine_ref, dst_ref=parts_ref.at[pidx], send_sem=send_sems.at[1, k - 1],
                                         recv_sem=recv_sems.at[1, k - 1], device_id=dev, device_id_type=MESH).wait_recv()
        for cp in copies:
            cp.wait_send()

    vm = pl.BlockSpec(memory_space=pltpu.VMEM)
    return pl.pallas_call(
        body, name="cond_exchange",
        out_shape=(jax.ShapeDtypeStruct((N_DEV, D), F32), jax.ShapeDtypeStruct((N_DEV, DEPTH, N_DEV, ncol), F32)),
        in_specs=[vm, vm, vm], out_specs=(vm, vm),
        scratch_shapes=[pltpu.VMEM((N_DEV, N_DEV, D), F32), pltpu.VMEM((DEPTH, N_DEV, ncol), F32),
                        pltpu.SemaphoreType.DMA((2, N_DEV - 1)), pltpu.SemaphoreType.DMA((2, N_DEV - 1))],
        compiler_params=_params(vmem=VMEM_LIMIT),
    )(c_tile, ada_w, ada_b_mine)[:2]


def _all_gather_weights(shards):
    n = len(shards)
    big = max(range(n), key=lambda s: shards[s].shape[0])
    assert N_DEV * shards[big].shape[0] >= SHARD_ROWS

    def body(*refs):
        ins, outs = refs[:n], refs[n:2 * n]
        local_sems, send_sems, recv_sems = refs[2 * n:]
        me = _my_index()
        local = []
        for s in range(n):
            rows = ins[s].shape[0]
            cp = pltpu.make_async_copy(ins[s], outs[s].at[pl.ds(me * rows, rows)], local_sems.at[s])
            cp.start()
            local.append(cp)
        for k in range(1, N_DEV):
            dev, _ = _peer(k)
            for s in range(n):
                rows = ins[s].shape[0]
                pltpu.make_async_remote_copy(src_ref=ins[s], dst_ref=outs[s].at[pl.ds(me * rows, rows)],
                                             send_sem=send_sems.at[k - 1], recv_sem=recv_sems.at[k - 1],
                                             device_id=dev, device_id_type=MESH).start()
        whole = outs[big].at[pl.ds(0, SHARD_ROWS)]
        for k in range(1, N_DEV):
            dev, _ = _peer(k)
            w = pltpu.make_async_remote_copy(src_ref=whole, dst_ref=whole, send_sem=send_sems.at[k - 1],
                                             recv_sem=recv_sems.at[k - 1], device_id=dev, device_id_type=MESH)
            w.wait_send()
            w.wait_recv()
        for cp in local:
            cp.wait()

    hbm = pl.BlockSpec(memory_space=pl.ANY)
    return pl.pallas_call(
        body, name="weight_all_gather",
        out_shape=tuple(jax.ShapeDtypeStruct((N_DEV * s.shape[0], D), s.dtype) for s in shards),
        in_specs=[hbm] * n, out_specs=tuple([hbm] * n),
        scratch_shapes=[pltpu.SemaphoreType.DMA((n,)), pltpu.SemaphoreType.DMA((N_DEV - 1,)),
                        pltpu.SemaphoreType.DMA((N_DEV - 1,))],
    )(*shards)


def _grad_exchange(fulls):
    n = len(fulls)

    def body(*refs):
        ins, out = refs[:n], refs[n]
        local_sems, send_sems, recv_sems = refs[n + 1:]
        me = _my_index()
        local = []
        for s in range(n):
            rows = SEG_ROWS[s]
            cp = pltpu.make_async_copy(ins[s].at[pl.ds(me * rows, rows)], out.at[me, pl.ds(SEG_OFF[s], rows)], local_sems.at[s])
            cp.start()
            local.append(cp)
        for k in range(1, N_DEV):
            dev, pidx = _peer(k)
            for s in range(n):
                rows = SEG_ROWS[s]
                pltpu.make_async_remote_copy(src_ref=ins[s].at[pl.ds(pidx * rows, rows)],
                                             dst_ref=out.at[me, pl.ds(SEG_OFF[s], rows)],
                                             send_sem=send_sems.at[k - 1], recv_sem=recv_sems.at[k - 1],
                                             device_id=dev, device_id_type=MESH).start()
        for k in range(1, N_DEV):
            dev, pidx = _peer(k)
            w = pltpu.make_async_remote_copy(src_ref=out.at[me], dst_ref=out.at[pidx], send_sem=send_sems.at[k - 1],
                                             recv_sem=recv_sems.at[k - 1], device_id=dev, device_id_type=MESH)
            w.wait_send()
            w.wait_recv()
        for cp in local:
            cp.wait()

    hbm = pl.BlockSpec(memory_space=pl.ANY)
    return pl.pallas_call(
        body, name="grad_exchange",
        out_shape=jax.ShapeDtypeStruct((N_DEV, SHARD_ROWS, D), BF16),
        in_specs=[hbm] * n, out_specs=hbm,
        scratch_shapes=[pltpu.SemaphoreType.DMA((n,)), pltpu.SemaphoreType.DMA((N_DEV - 1,)),
                        pltpu.SemaphoreType.DMA((N_DEV - 1,))],
    )(*fulls)


def _norm_mod(x, nw, sc, sh):
    ms = jnp.mean(x * x, axis=-1, keepdims=True)
    xh = x * lax.rsqrt(ms + RMS_EPS)
    return xh, (xh * nw) * (1.0 + sc) + sh


def _proj(x, nw, sc, sh, wt, *, ffn, name):
    S, N = x.shape[0], wt.shape[0]
    tm = 256 if ffn else 512

    def body(x_ref, nw_ref, sc_ref, sh_ref, w_ref, h_ref, out_ref, *act_ref):
        _, h = _norm_mod(x_ref[...], nw_ref[...], sc_ref[...], sh_ref[...])
        hb = h.astype(BF16)
        h_ref[...] = hb
        if ffn:
            gate = _nt(hb, w_ref[pl.ds(0, D_FF), :])
            up = _nt(hb, w_ref[pl.ds(D_FF, D_FF), :])
            out_ref[:, pl.ds(0, D_FF)] = gate.astype(BF16)
            out_ref[:, pl.ds(D_FF, D_FF)] = up.astype(BF16)
            act_ref[0][...] = ((gate * jax.nn.sigmoid(gate)) * up).astype(BF16)
        else:
            out_ref[...] = _nt(hb, w_ref[...]).astype(BF16)

    row = lambda w: pl.BlockSpec((tm, w), lambda i: (i, 0))
    out_shape = [jax.ShapeDtypeStruct((S, D), BF16), jax.ShapeDtypeStruct((S, N), BF16)]
    out_specs = [row(D), row(N)]
    if ffn:
        out_shape.append(jax.ShapeDtypeStruct((S, D_FF), BF16))
        out_specs.append(row(D_FF))
    vec = _const_spec((1, D))
    return pl.pallas_call(
        body, name=name, grid=(S // tm,), out_shape=tuple(out_shape),
        in_specs=[row(D), vec, vec, vec, _const_spec((N, D))], out_specs=tuple(out_specs),
        compiler_params=_params(("parallel",), VMEM_LIMIT),
    )(x, nw, sc, sh, wt)


def _gated_residual(a, w, x, g, *, w_is_transposed, name):
    S, K = a.shape
    tm = 512

    def body(a_ref, w_ref, x_ref, g_ref, xo_ref, y_ref):
        y = _nt(a_ref[...], w_ref[...]) if w_is_transposed else _nn(a_ref[...], w_ref[...])
        y_ref[...] = y.astype(BF16)
        xo_ref[...] = x_ref[...] + g_ref[...] * y

    row = lambda w_: pl.BlockSpec((tm, w_), lambda i: (i, 0))
    return pl.pallas_call(
        body, name=name, grid=(S // tm,),
        out_shape=(jax.ShapeDtypeStruct((S, D), F32), jax.ShapeDtypeStruct((S, D), BF16)),
        in_specs=[row(K), _const_spec(w.shape), row(D), _const_spec((1, D))], out_specs=(row(D), row(D)),
        compiler_params=_params(("parallel",), VMEM_LIMIT),
    )(a, w, x, g)


CHUNK = 1024


def _tile_rows(r, chunk=CHUNK):
    return min(TQ, chunk // r)


def _alibi_bias(slopes, half, dil, chunk=CHUNK, both=False):
    tq = _tile_rows(dil, chunk)
    tk = tq + 2 * half
    rel = np.arange(tk)[:, None] - half - np.arange(tq)[None, :]
    band = np.abs(rel) <= half
    dist = (dil * np.abs(rel)).astype(np.float32)
    tabs = [np.where(band, -np.float32(s) * dist, np.float32(NEG)).astype(np.float32) for s in slopes]
    out = []
    for u in range(0, len(tabs), 8):
        tab = np.concatenate(tabs[u:u + 8], axis=1)
        first, last = tab.copy(), tab.copy()
        first[:half] = NEG
        last[tk - half:] = NEG
        out += [tab, first, last]
        if both:
            last = last.copy()
            last[:half] = NEG
            out.append(last)
    return jnp.asarray(np.concatenate(out, axis=0))


def _slopes(n):
    return (2.0 ** (-8.0 * np.arange(1, n + 1) / n)).astype(np.float32)


def _head_masks(tq):
    lane = lax.broadcasted_iota(jnp.int32, (tq, LANES), 1)
    lo = lane < HEAD_DIM
    return lo, jnp.logical_not(lo)


def _stack_heads(tiles, lo, hi, scale):
    blocks = []
    for t in range(4):
        xf = tiles[t] if scale == 1.0 else tiles[t] * scale
        for a in range(2):
            xm = jnp.where(lo if a == 0 else hi, xf, 0.0)
            if a != t // 2:
                xm = pltpu.roll(xm, HEAD_DIM, 1)
            blocks.append(xm.astype(BF16))
    return jnp.concatenate(blocks, axis=0)


def _tile_from_columns(x8t, t, tq):
    r0 = HEAD_DIM * (t // 2)
    top = x8t[r0:r0 + HEAD_DIM, 2 * t * tq:(2 * t + 1) * tq]
    bot = x8t[r0:r0 + HEAD_DIM, (2 * t + 1) * tq:(2 * t + 2) * tq]
    return jnp.concatenate([top, bot], axis=0).T


def _attn_layout(S, C, r, half, qoff, koff, voff, chunk):
    hb = half * r
    per = chunk // hb
    nhb = S // hb
    main = lambda off: pl.BlockSpec((chunk, LANES), lambda u, i: (i, off // LANES + u))
    prev = lambda off: pl.BlockSpec((hb, LANES), lambda u, i: (jnp.maximum(i * per - 1, 0), off // LANES + u))
    nxt = lambda off: pl.BlockSpec((hb, LANES), lambda u, i: (jnp.minimum((i + 1) * per, nhb - 1), off // LANES + u))
    specs = [pl.BlockSpec((chunk, 4 * LANES), lambda u, i: (i, qoff // (4 * LANES) + u))]
    specs += [prev(koff), main(koff), nxt(koff), prev(voff), main(voff), nxt(voff)]
    return specs, hb


def _stage(dst, srcs):
    row = 0
    for src in srcs:
        n = src.shape[0]
        dst[pl.ds(row, n), :] = src[...].astype(F32)
        row += n


def _rows(start, n, r):
    return pl.ds(start, n, stride=r) if r > 1 else pl.ds(start, n)


def _attn_fwd(qkv, bias, sink, *, C, r, half, qoff, koff, voff, n_units, out_dtype, name):
    S = qkv.shape[0]
    chunk = max(CHUNK, TQ * r)
    tq = _tile_rows(r, chunk)
    tk = tq + 2 * half
    tiles = chunk // (r * tq)
    nsteps = S // chunk
    specs, hb = _attn_layout(S, C, r, half, qoff, koff, voff, chunk)
    use_sink = sink is not None

    def body(*refs):
        q_ref, kp, km, kn, vp, vm, vn, bias_ref = refs[:8]
        rest = list(refs[8:])
        sink_ref = rest.pop(0) if use_sink else None
        o_ref, lse_ref, qs, ks, vs, os_, ls = rest
        i = pl.program_id(1)
        for t in range(4):
            qs[t] = q_ref[:, pl.ds(t * LANES, LANES)].astype(F32)
        _stage(ks, [kp, km, kn])
        _stage(vs, [vp, vm, vn])
        lo, hi = _head_masks(tq)
        ones = jnp.ones((16, tk), BF16)
        if use_sink:
            sk = sink_ref[pl.ds(0, 1), :]

        def chain(n, carry):
            rho, c = n // tiles, n % tiles
            start = c * (tq * r) + rho
            if r == 1:
                start = pl.multiple_of(start, tq)
            variant = jnp.where(jnp.logical_and(i == 0, c == 0), 1, 0) + jnp.where(
                jnp.logical_and(i == nsteps - 1, c == tiles - 1), 2, 0)
            k2 = ks[_rows(start, tk, r), :].astype(BF16)
            v2t = jnp.concatenate([vs[_rows(start, tk, r), :].T.astype(BF16), ones], axis=0)
            q8 = _stack_heads([qs[t, _rows(start, tq, r), :] for t in range(4)], lo, hi, HEAD_DIM ** -0.5)
            s = _nt(k2, q8) + bias_ref[pl.ds(pl.multiple_of(variant * tk, 8), tk), :]
            m = jnp.max(s, axis=0, keepdims=True)
            if use_sink:
                m = jnp.maximum(m, sk)
            pv = _nn(v2t, jnp.exp(s - m).astype(BF16))
            l = pv[LANES:LANES + 1]
            if use_sink:
                l = l + jnp.exp(sk - m)
            o8t = pv[:LANES] / l
            lse8 = jnp.broadcast_to(m + jnp.log(l), (LANES, 8 * tq))
            for t in range(4):
                os_[t, _rows(start, tq, r), :] = _tile_from_columns(o8t, t, tq)
                ls[t, _rows(start, tq, r), :] = _tile_from_columns(lse8, t, tq)
            return carry

        lax.fori_loop(0, r * tiles, chain, 0, unroll=2)
        for t in range(4):
            o_ref[:, pl.ds(t * LANES, LANES)] = os_[t].astype(out_dtype)
            lse_ref[:, pl.ds(t * LANES, LANES)] = ls[t]

    in_specs = specs + [pl.BlockSpec((bias.shape[0] // n_units, 8 * tq), lambda u, i: (u, 0))]
    args = [qkv] * 7 + [bias]
    if use_sink:
        in_specs.append(pl.BlockSpec((8, 8 * tq), lambda u, i: (u, 0)))
        args.append(sink)
    wide = pl.BlockSpec((chunk, 4 * LANES), lambda u, i: (i, u))
    win = hb + chunk + hb
    return pl.pallas_call(
        body, name=name, grid=(n_units, nsteps),
        out_shape=(jax.ShapeDtypeStruct((S, n_units * 512), out_dtype), jax.ShapeDtypeStruct((S, n_units * 512), F32)),
        in_specs=in_specs, out_specs=(wide, wide),
        scratch_shapes=[pltpu.VMEM((4, chunk, LANES), F32), pltpu.VMEM((win, LANES), F32), pltpu.VMEM((win, LANES), F32),
                        pltpu.VMEM((4, chunk, LANES), F32), pltpu.VMEM((4, chunk, LANES), F32)],
        compiler_params=_params(("parallel", "parallel"), VMEM_LIMIT),
    )(*args)


def _attn_bwd(qkv, bias, sink, o, do, lse, *, C, r, half, qoff, koff, voff, n_units, name):
    S = qkv.shape[0]
    tq = _tile_rows(r)
    tk = tq + 2 * half
    tiles = CHUNK // (r * tq)
    nsteps = S // CHUNK
    specs, hb = _attn_layout(S, C, r, half, qoff, koff, voff, CHUNK)
    use_sink = sink is not None

    def body(*refs):
        q_ref, kp, km, kn, vp, vm, vn, bias_ref = refs[:8]
        rest = list(refs[8:])
        sink_ref = rest.pop(0) if use_sink else None
        o_ref, do_ref, lse_ref, dq_ref, dk_hbm, dv_hbm = rest[:6]
        rest = rest[6:]
        dsink_ref = rest.pop(0) if use_sink else None
        qs, ks, vs, os_, dos, ls, dqs, acck, accv, sem = rest
        u, i = pl.program_id(0), pl.program_id(1)

        @pl.when(i == 0)
        def _():
            acck[...] = jnp.zeros_like(acck)
            accv[...] = jnp.zeros_like(accv)
            if use_sink:
                dsink_ref[...] = jnp.zeros_like(dsink_ref)

        for t in range(4):
            cols = pl.ds(t * LANES, LANES)
            qs[t] = q_ref[:, cols].astype(F32)
            os_[t] = o_ref[:, cols].astype(F32)
            dos[t] = do_ref[:, cols].astype(F32)
            ls[t] = lse_ref[:, cols]
        _stage(ks, [kp, km, kn])
        _stage(vs, [vp, vm, vn])
        lo, hi = _head_masks(tq)
        base = pl.multiple_of(i * CHUNK, CHUNK)
        if use_sink:
            sk = sink_ref[pl.ds(0, 1), :]

        def chain(n, carry):
            rho, c = n // tiles, n % tiles
            start = c * (tq * r) + rho
            if r == 1:
                start = pl.multiple_of(start, tq)
            variant = jnp.where(jnp.logical_and(i == 0, c == 0), 1, 0) + jnp.where(
                jnp.logical_and(i == nsteps - 1, c == tiles - 1), 2, 0)
            k2 = ks[_rows(start, tk, r), :].astype(BF16)
            v2 = vs[_rows(start, tk, r), :].astype(BF16)
            k2t = ks[_rows(start, tk, r), :].T.astype(BF16)
            q8 = _stack_heads([qs[t, _rows(start, tq, r), :] for t in range(4)], lo, hi, HEAD_DIM ** -0.5)
            do_tiles = [dos[t, _rows(start, tq, r), :] for t in range(4)]
            do8 = _stack_heads(do_tiles, lo, hi, 1.0)
            deltas, lses = [], []
            for t in range(4):
                prod_t = (do_tiles[t] * os_[t, _rows(start, tq, r), :]).T
                lse_t = ls[t, _rows(start, tq, r), :].T
                for a in range(2):
                    deltas.append(jnp.sum(prod_t[a * HEAD_DIM:(a + 1) * HEAD_DIM], axis=0, keepdims=True))
                    lses.append(lse_t[a * HEAD_DIM:a * HEAD_DIM + 1])
            delta8 = jnp.concatenate(deltas, axis=1)
            lse8 = jnp.concatenate(lses, axis=1)
            s = _nt(k2, q8) + bias_ref[pl.ds(pl.multiple_of(variant * tk, 8), tk), :]
            p = jnp.exp(s - lse8)
            dp = _nt(v2, do8)
            dsb = (p * (dp - delta8)).astype(BF16)
            dq8t = _nn(k2t, dsb)
            for t in range(4):
                dqs[t, _rows(start, tq, r), :] = _tile_from_columns(dq8t, t, tq) * (HEAD_DIM ** -0.5)
            arow = base + start
            if r == 1:
                arow = pl.multiple_of(arow, tq)
            acck[_rows(arow, tk, r), :] = acck[_rows(arow, tk, r), :] + _nn(dsb, q8)
            accv[_rows(arow, tk, r), :] = accv[_rows(arow, tk, r), :] + _nn(p.astype(BF16), do8)
            if use_sink:
                e = jnp.exp(sk - lse8) * delta8
                for h in range(8):
                    part = -jnp.sum(e[:, h * tq:(h + 1) * tq], axis=1, keepdims=True)
                    dsink_ref[pl.ds(h, 1), :] = dsink_ref[pl.ds(h, 1), :] + part
            return carry

        lax.fori_loop(0, r * tiles, chain, 0, unroll=2)
        for t in range(4):
            dq_ref[:, pl.ds(t * LANES, LANES)] = dqs[t].astype(BF16)

        @pl.when(i == nsteps - 1)
        def _():
            ck = pltpu.make_async_copy(acck.at[pl.ds(hb, S)], dk_hbm.at[u], sem.at[0])
            cv = pltpu.make_async_copy(accv.at[pl.ds(hb, S)], dv_hbm.at[u], sem.at[1])
            ck.start()
            cv.start()
            ck.wait()
            cv.wait()

    wide = pl.BlockSpec((CHUNK, 4 * LANES), lambda u, i: (i, u))
    hbm = pl.BlockSpec(memory_space=pl.ANY)
    in_specs = specs + [pl.BlockSpec((3 * tk, 8 * tq), lambda u, i: (u, 0))]
    args = [qkv] * 7 + [bias]
    if use_sink:
        in_specs.append(pl.BlockSpec((8, 8 * tq), lambda u, i: (u, 0)))
        args.append(sink)
    in_specs += [wide, wide, wide]
    args += [o, do, lse]
    out_shape = [jax.ShapeDtypeStruct((S, n_units * 512), BF16), jax.ShapeDtypeStruct((n_units, S, LANES), F32),
                 jax.ShapeDtypeStruct((n_units, S, LANES), F32)]
    out_specs = [wide, hbm, hbm]
    if use_sink:
        out_shape.append(jax.ShapeDtypeStruct((n_units * 8, LANES), F32))
        out_specs.append(pl.BlockSpec((8, LANES), lambda u, i: (u, 0)))
    win = hb + CHUNK + hb
    big = lambda: pltpu.VMEM((4, CHUNK, LANES), F32)
    res = pl.pallas_call(
        body, name=name, grid=(n_units, nsteps), out_shape=tuple(out_shape), in_specs=in_specs, out_specs=tuple(out_specs),
        scratch_shapes=[big(), pltpu.VMEM((win, LANES), F32), pltpu.VMEM((win, LANES), F32), big(), big(), big(), big(),
                        pltpu.VMEM((S + 2 * hb, LANES), F32), pltpu.VMEM((S + 2 * hb, LANES), F32), pltpu.SemaphoreType.DMA((2,))],
        compiler_params=_params(("arbitrary", "arbitrary"), VMEM_LIMIT),
    )(*args)
    return res[0], res[1], res[2], (res[3] if use_sink else None)


def _merge_groups(os_, lses):
    S, W = os_[0].shape
    tm = 512

    def body(o0, o1, o2, l0, l1, l2, o_ref, lse_ref):
        ls = [l0[...], l1[...], l2[...]]
        mx = jnp.maximum(jnp.maximum(ls[0], ls[1]), ls[2])
        es = [jnp.exp(l - mx) for l in ls]
        den = es[0] + es[1] + es[2]
        o = (es[0] / den) * o0[...] + (es[1] / den) * o1[...] + (es[2] / den) * o2[...]
        o_ref[...] = o.astype(BF16)
        lse_ref[...] = mx + jnp.log(den)

    row = pl.BlockSpec((tm, W), lambda i: (i, 0))
    return pl.pallas_call(
        body, name="merge_groups", grid=(S // tm,),
        out_shape=(jax.ShapeDtypeStruct((S, W), BF16), jax.ShapeDtypeStruct((S, W), F32)),
        in_specs=[row] * 6, out_specs=(row, row), compiler_params=_params(("parallel",), VMEM_LIMIT),
    )(*os_, *lses)


def _loss_head(x, target, fnw):
    S = x.shape[0]
    tm = 512

    def body(x_ref, t_ref, w_ref, dx_ref, st_ref):
        @pl.when(pl.program_id(0) == 0)
        def _():
            st_ref[...] = jnp.zeros_like(st_ref)

        xv = x_ref[...]
        rstd = lax.rsqrt(jnp.mean(xv * xv, axis=-1, keepdims=True) + RMS_EPS)
        xh = xv * rstd
        err = xh * w_ref[...] - t_ref[...]
        dy = err * (1.0 / D)
        dxh = dy * w_ref[...]
        dx_ref[...] = rstd * (dxh - xh * jnp.mean(dxh * xh, axis=-1, keepdims=True))
        st_ref[pl.ds(0, 1), :] = st_ref[pl.ds(0, 1), :] + jnp.sum(dy * xh, axis=0, keepdims=True)
        st_ref[pl.ds(1, 1), :] = st_ref[pl.ds(1, 1), :] + jnp.sum(err * err, axis=0, keepdims=True)

    row = pl.BlockSpec((tm, D), lambda i: (i, 0))
    return pl.pallas_call(
        body, name="loss_head", grid=(S // tm,),
        out_shape=(jax.ShapeDtypeStruct((S, D), F32), jax.ShapeDtypeStruct((8, D), F32)),
        in_specs=[row, row, _const_spec((1, D))], out_specs=(row, _const_spec((8, D))),
        compiler_params=_params(("arbitrary",), VMEM_LIMIT),
    )(x, target, fnw)


def _gate_bwd(dx, y, g, w, gu, *, w_is_transposed, name):
    S = dx.shape[0]
    K = w.shape[1] if w_is_transposed else w.shape[0]
    ffn = gu is not None
    tm = 256 if ffn else 512
    wout = 2 * K if ffn else K

    def body(dx_ref, y_ref, g_ref, w_ref, *rest):
        if ffn:
            gu_ref, da_ref, dyb_ref, st_ref = rest
        else:
            da_ref, dyb_ref, st_ref = rest

        @pl.when(pl.program_id(0) == 0)
        def _():
            st_ref[...] = jnp.zeros_like(st_ref)

        dxv = dx_ref[...]
        st_ref[pl.ds(0, 1), :] = st_ref[pl.ds(0, 1), :] + jnp.sum(dxv * y_ref[...].astype(F32), axis=0, keepdims=True)
        dyb = (dxv * g_ref[...]).astype(BF16)
        dyb_ref[...] = dyb
        da = _nn(dyb, w_ref[...]) if w_is_transposed else _nt(dyb, w_ref[...])
        if ffn:
            gate = gu_ref[:, pl.ds(0, K)].astype(F32)
            up = gu_ref[:, pl.ds(K, K)].astype(F32)
            sig = jax.nn.sigmoid(gate)
            da_ref[:, pl.ds(0, K)] = (da * up * (sig * (1.0 + gate * (1.0 - sig)))).astype(BF16)
            da_ref[:, pl.ds(K, K)] = (da * (gate * sig)).astype(BF16)
        else:
            da_ref[...] = da.astype(BF16)

    row = lambda w_: pl.BlockSpec((tm, w_), lambda i: (i, 0))
    in_specs = [row(D), row(D), _const_spec((1, D)), _const_spec(w.shape)]
    args = [dx, y, g, w]
    if ffn:
        in_specs.append(row(wout))
        args.append(gu)
    return pl.pallas_call(
        body, name=name, grid=(S // tm,),
        out_shape=(jax.ShapeDtypeStruct((S, wout), BF16), jax.ShapeDtypeStruct((S, D), BF16), jax.ShapeDtypeStruct((8, D), F32)),
        in_specs=in_specs, out_specs=(row(wout), row(D), _const_spec((8, D))),
        compiler_params=_params(("arbitrary",), VMEM_LIMIT),
    )(*args)


def _norm_bwd(dy, wt, x, dres, nw, sc, *, name):
    S, N = dy.shape
    tm = 256 if N > 4096 else 512

    def body(dy_ref, w_ref, x_ref, dres_ref, nw_ref, sc_ref, dx_ref, st_ref):
        @pl.when(pl.program_id(0) == 0)
        def _():
            st_ref[...] = jnp.zeros_like(st_ref)

        dh = _nn(dy_ref[...], w_ref[...])
        xv = x_ref[...]
        rstd = lax.rsqrt(jnp.mean(xv * xv, axis=-1, keepdims=True) + RMS_EPS)
        xh = xv * rstd
        nwv, scale = nw_ref[...], 1.0 + sc_ref[...]
        dxh = dh * (nwv * scale)
        dx_ref[...] = dres_ref[...] + rstd * (dxh - xh * jnp.mean(dxh * xh, axis=-1, keepdims=True))
        dhx = dh * xh
        st_ref[pl.ds(0, 1), :] = st_ref[pl.ds(0, 1), :] + jnp.sum(dh, axis=0, keepdims=True)
        st_ref[pl.ds(1, 1), :] = st_ref[pl.ds(1, 1), :] + jnp.sum(dhx * nwv, axis=0, keepdims=True)
        st_ref[pl.ds(2, 1), :] = st_ref[pl.ds(2, 1), :] + jnp.sum(dhx * scale, axis=0, keepdims=True)

    row = lambda w_: pl.BlockSpec((tm, w_), lambda i: (i, 0))
    vec = _const_spec((1, D))
    return pl.pallas_call(
        body, name=name, grid=(S // tm,),
        out_shape=(jax.ShapeDtypeStruct((S, D), F32), jax.ShapeDtypeStruct((8, D), F32)),
        in_specs=[row(N), _const_spec((N, D)), row(D), row(D), vec, vec], out_specs=(row(D), _const_spec((8, D))),
        compiler_params=_params(("arbitrary",), VMEM_LIMIT),
    )(dy, wt, x, dres, nw, sc)


def _weight_grad(a, b, *, transpose_out, name):
    S, N = b.shape
    nb = N // 2 if N > 4096 else N
    tk = 512

    def body(a_ref, b_ref, out_ref, acc):
        k = pl.program_id(1)

        @pl.when(k == 0)
        def _():
            acc[...] = jnp.zeros_like(acc)

        acc[...] += _tn(a_ref[...], b_ref[...])

        @pl.when(k == pl.num_programs(1) - 1)
        def _():
            out_ref[...] = (acc[...].T if transpose_out else acc[...]).astype(BF16)

    out_block = pl.BlockSpec((nb, D), lambda n, k: (n, 0)) if transpose_out else pl.BlockSpec((D, nb), lambda n, k: (0, n))
    return pl.pallas_call(
        body, name=name, grid=(N // nb, S // tk),
        out_shape=jax.ShapeDtypeStruct((N, D) if transpose_out else (D, N), BF16),
        in_specs=[pl.BlockSpec((tk, D), lambda n, k: (k, 0)), pl.BlockSpec((tk, nb), lambda n, k: (k, n))],
        out_specs=out_block, scratch_shapes=[pltpu.VMEM((D, nb), F32)],
        compiler_params=_params(("parallel", "arbitrary"), VMEM_LIMIT),
    )(a, b)


def _adamw(w, g, m, v):
    m = ADAM_B1 * m + (1.0 - ADAM_B1) * g
    v = ADAM_B2 * v + (1.0 - ADAM_B2) * (g * g)
    m_hat = m / (1.0 - ADAM_B1 ** ADAM_STEP)
    v_hat = v / (1.0 - ADAM_B2 ** ADAM_STEP)
    delta = -ADAM_LR * (m_hat / (jnp.sqrt(v_hat) + ADAM_EPS) + ADAM_WD * w)
    return delta, m, v


def _adam_shard(parts, w, m, v):
    R = w.shape[0]
    tr = 192

    def body(p_ref, w_ref, m_ref, v_ref, g_out, d_out, m_out, v_out):
        g = p_ref[0].astype(F32)
        for j in range(1, N_DEV):
            g = g + p_ref[j].astype(F32)
        delta, mn, vn = _adamw(w_ref[...], g, m_ref[...], v_ref[...])
        g_out[...] = g
        d_out[...] = delta
        m_out[...] = mn
        v_out[...] = vn

    row = pl.BlockSpec((tr, D), lambda i: (i, 0))
    shp = jax.ShapeDtypeStruct((R, D), F32)
    return pl.pallas_call(
        body, name="adam_shard", grid=(R // tr,), out_shape=(shp,) * 4,
        in_specs=[pl.BlockSpec((N_DEV, tr, D), lambda i: (0, i, 0)), row, row, row], out_specs=(row,) * 4,
        compiler_params=_params(("parallel",), VMEM_LIMIT),
    )(parts, w, m, v)


def _adam_ada_w(cond_t, dmod, w, m, v):
    ncol = w.shape[-1]
    tr = 512

    def body(c_ref, d_ref, w_ref, m_ref, v_ref, g_out, d_out, m_out, v_out):
        g = _nn(c_ref[...], d_ref[0])
        delta, mn, vn = _adamw(w_ref[0], g, m_ref[0], v_ref[0])
        g_out[0] = g
        d_out[0] = delta
        m_out[0] = mn
        v_out[0] = vn

    blk = pl.BlockSpec((1, tr, ncol), lambda l, i: (l, i, 0))
    shp = jax.ShapeDtypeStruct(w.shape, F32)
    return pl.pallas_call(
        body, name="adam_ada_w", grid=(DEPTH, D // tr), out_shape=(shp,) * 4,
        in_specs=[pl.BlockSpec((tr, LANES), lambda l, i: (i, 0)), pl.BlockSpec((1, LANES, ncol), lambda l, i: (l, 0, 0)), blk, blk, blk],
        out_specs=(blk,) * 4, compiler_params=_params(("parallel", "parallel"), VMEM_LIMIT),
    )(cond_t, dmod, w, m, v)


TILE_ROWS = 168


def _stat_sources():
    pairs = []
    for i in range(DEPTH):
        b = 32 * i
        for q, src in enumerate((b, b + 1, b + 8, b + 16, b + 17, b + 24)):
            pairs.append((6 * i + q, src))
        pairs.append((24 + i, b + 2))
        pairs.append((32 + i, b + 18))
    pairs += [(40, 128), (41, 129)]
    return pairs


def _small_exchange(tiles, w, m, v):
    loss_row, sink_row, sink_src = 41, 48, 136

    def body(s_ref, w_ref, m_ref, v_ref, dmod_out, g_out, d_out, m_out, v_out, loss_out, all_ref, tot_ref, send_sems, recv_sems):
        me = _my_index()
        all_ref[me] = s_ref[...]
        copies = []
        for k in range(1, N_DEV):
            dev, _ = _peer(k)
            cp = pltpu.make_async_remote_copy(src_ref=s_ref, dst_ref=all_ref.at[me], send_sem=send_sems.at[k - 1],
                                              recv_sem=recv_sems.at[k - 1], device_id=dev, device_id_type=MESH)
            cp.start()
            copies.append(cp)
        for k in range(1, N_DEV):
            dev, pidx = _peer(k)
            pltpu.make_async_remote_copy(src_ref=s_ref, dst_ref=all_ref.at[pidx], send_sem=send_sems.at[k - 1],
                                         recv_sem=recv_sems.at[k - 1], device_id=dev, device_id_type=MESH).wait_recv()
        for cp in copies:
            cp.wait_send()
        tot = all_ref[0]
        for j in range(1, N_DEV):
            tot = tot + all_ref[j]
        tot_ref[...] = tot
        g_out[...] = jnp.zeros_like(g_out)
        for dst, src in _stat_sources():
            g_out[pl.ds(dst, 1), :] = tot_ref[pl.ds(src, 1), :]
            if dst < 24:
                for j in range(N_DEV):
                    dmod_out[j, pl.ds(dst, 1), :] = all_ref[j, pl.ds(src, 1), :]
        lane = lax.broadcasted_iota(jnp.int32, (1, D), 1)
        sink = jnp.zeros((1, D), F32)
        for h in range(32):
            sink = jnp.where(lane == h, tot_ref[pl.ds(sink_src + h, 1), :], sink)
        g_out[pl.ds(sink_row, 1), :] = sink
        g = g_out[...]
        delta, mn, vn = _adamw(w_ref[...], g, m_ref[...], v_ref[...])
        d_out[...] = delta
        m_out[...] = mn
        v_out[...] = vn
        loss = jnp.sum(g[loss_row:loss_row + 1, :], axis=-1, keepdims=True) * (0.5 / D)
        loss_out[...] = jnp.broadcast_to(loss, loss_out.shape)

    vm = pl.BlockSpec(memory_space=pltpu.VMEM)
    shp = jax.ShapeDtypeStruct((STAT_ROWS, D), F32)
    return pl.pallas_call(
        body, name="small_exchange",
        out_shape=(jax.ShapeDtypeStruct((N_DEV, 24, D), F32), shp, shp, shp, shp, jax.ShapeDtypeStruct((8, LANES), F32)),
        in_specs=[vm] * 4, out_specs=(vm,) * 6,
        scratch_shapes=[pltpu.VMEM((N_DEV, TILE_ROWS, D), F32), pltpu.VMEM((TILE_ROWS, D), F32),
                        pltpu.SemaphoreType.DMA((N_DEV - 1,)), pltpu.SemaphoreType.DMA((N_DEV - 1,))],
        compiler_params=_params(vmem=VMEM_LIMIT),
    )(tiles, w, m, v)


def _to_rows(name, a):
    if name in ("ffn_in", "a_in", "b_in"):
        return a.T
    if name == "b_out":
        return a.T.reshape(-1, D)
    return a


def _from_rows(name, a):
    if name in ("ffn_in", "a_in", "b_in"):
        return a.T
    if name == "b_out":
        return a.reshape(-1, 512).T
    return a


def _rows8(a):
    return jnp.pad(a, ((0, 8 - a.shape[0]), (0, 0)))


def _pack_small(ada_b, norm_mix, norm_ffn, final_norm, sink):
    sink_row = jnp.pad(sink.reshape(1, -1), ((0, 0), (0, D - sink.size)))
    return jnp.concatenate([ada_b.reshape(24, D), _rows8(norm_mix), _rows8(norm_ffn), _rows8(final_norm.reshape(1, D)),
                            _rows8(sink_row)], axis=0)


def _unpack_small(a):
    return a[0:24].reshape(4, 6 * D), a[24:28], a[32:36], a[40], a[48, :32].reshape(2, 16)


def kernel(x, c, ada_w, ada_b, norm_mix, norm_ffn, ffn_w_in, ffn_w_out, a_w_in, a_w_out, a_sink, b_w_in, b_w_out, final_norm, loss_target, m_ada_w, m_ada_b, m_norm_mix, m_norm_ffn, m_ffn_w_in, m_ffn_w_out, m_a_w_in, m_a_w_out, m_a_sink, m_b_w_in, m_b_w_out, m_final_norm, v_ada_w, v_ada_b, v_norm_mix, v_norm_ffn, v_ffn_w_in, v_ffn_w_out, v_a_w_in, v_a_w_out, v_a_sink, v_b_w_in, v_b_w_out, v_final_norm):
    S = x.shape[1]
    x0 = x.reshape(S, D)
    target = loss_target.reshape(S, D)
    me = _my_index()
    ncol = ada_w.shape[-1]

    ada_b_mine = lax.dynamic_slice_in_dim(ada_b, me * ncol, ncol, axis=1)
    cond_all, parts = _cond_exchange(jnp.broadcast_to(c.reshape(1, D), (8, D)), ada_w, ada_b_mine)
    mod = lax.dynamic_index_in_dim(parts, me, axis=2, keepdims=False)
    mod = jnp.transpose(mod, (1, 0, 2)).reshape(DEPTH, 6, 1, D)

    weights = {"ffn_in": ffn_w_in, "ffn_out": ffn_w_out, "a_in": a_w_in, "a_out": a_w_out, "b_in": b_w_in, "b_out": b_w_out}
    shards = [_to_rows(n, weights[n][l]).astype(BF16) for n, l, _ in SEGMENTS]
    gathered = _all_gather_weights(shards)
    W = {(n, l): g for (n, l, _), g in zip(SEGMENTS, gathered)}
    for j in range(2):
        W[("b_out", j)] = W[("b_out", j)].reshape(D, 512)

    a_slopes, b_slopes = _slopes(16), _slopes(24)
    bias_a = _alibi_bias(a_slopes, A_HALF, 1)
    bias_b = [_alibi_bias(b_slopes[8 * g:8 * g + 8], B_HALF, dil) for g, dil in enumerate(B_DILS)]
    bias_b_fwd = [_alibi_bias(b_slopes[8 * g:8 * g + 8], B_HALF, dil, max(CHUNK, TQ * dil), both=True) for g, dil in enumerate(B_DILS)]
    a_geom = dict(C=A_QKV, r=1, half=A_HALF, qoff=0, koff=1024, voff=1280, n_units=2)
    b_geom = [dict(C=B_QKV, r=dil, half=B_HALF, qoff=512 * g, koff=1536 + 128 * g, voff=1920 + 128 * g, n_units=1)
              for g, dil in enumerate(B_DILS)]

    saved = []
    xcur = x0
    for i in range(DEPTH):
        j = i // 2
        sh1, sc1, g1, sh2, sc2, g2 = [mod[i, q] for q in range(6)]
        nm, nf = norm_mix[i].reshape(1, D), norm_ffn[i].reshape(1, D)
        if i % 2 == 0:
            sink_rep = jnp.repeat(jnp.repeat(a_sink[j], TQ).reshape(2, 1, 8 * TQ), 8, axis=1).reshape(16, 8 * TQ)
            h1, qkv = _proj(xcur, nm, sc1, sh1, W[("a_in", j)], ffn=False, name="proj_a")
            o, lse = _attn_fwd(qkv, bias_a, sink_rep, out_dtype=BF16, name="attn_a_fwd", **a_geom)
            x1, y1 = _gated_residual(o, W[("a_out", j)], xcur, g1, w_is_transposed=False, name="out_a")
        else:
            sink_rep = None
            h1, qkv = _proj(xcur, nm, sc1, sh1, W[("b_in", j)], ffn=False, name="proj_b")
            outs = [_attn_fwd(qkv, bias_b_fwd[g], None, out_dtype=F32, name="attn_b%d_fwd" % g, **b_geom[g]) for g in range(3)]
            o, lse = _merge_groups([t[0] for t in outs], [t[1] for t in outs])
            x1, y1 = _gated_residual(o, W[("b_out", j)], xcur, g1, w_is_transposed=True, name="out_b")
        h2, gu, act = _proj(x1, nf, sc2, sh2, W[("ffn_in", i)], ffn=True, name="ffn_in")
        x2, y2 = _gated_residual(act, W[("ffn_out", i)], x1, g2, w_is_transposed=False, name="ffn_out")
        saved.append(dict(x0=xcur, h1=h1, qkv=qkv, o=o, lse=lse, y1=y1, x1=x1, h2=h2, gu=gu, act=act, y2=y2, sink=sink_rep))
        xcur = x2

    dx, head_stats = _loss_head(xcur, target, final_norm.reshape(1, D))

    dW = {}
    stat_tiles, dsink = [None] * DEPTH, [None] * 2
    for i in reversed(range(DEPTH)):
        j = i // 2
        sv = saved[i]
        sh1, sc1, g1, sh2, sc2, g2 = [mod[i, q] for q in range(6)]
        nm, nf = norm_mix[i].reshape(1, D), norm_ffn[i].reshape(1, D)
        dgu, dy2, st_g2 = _gate_bwd(dx, sv["y2"], g2, W[("ffn_out", i)], sv["gu"], w_is_transposed=False, name="ffn_out_bwd")
        dW[("ffn_out", i)] = _weight_grad(dy2, sv["act"], transpose_out=True, name="dw_ffn_out")
        dW[("ffn_in", i)] = _weight_grad(sv["h2"], dgu, transpose_out=True, name="dw_ffn_in")
        dx1, st_f = _norm_bwd(dgu, W[("ffn_in", i)], sv["x1"], dx, nf, sc2, name="ffn_in_bwd")
        if i % 2 == 0:
            do, dy1, st_g1 = _gate_bwd(dx1, sv["y1"], g1, W[("a_out", j)], None, w_is_transposed=False, name="out_a_bwd")
            dW[("a_out", j)] = _weight_grad(dy1, sv["o"], transpose_out=True, name="dw_a_out")
            dq, dk, dv, ds = _attn_bwd(sv["qkv"], bias_a, sv["sink"], sv["o"], do, sv["lse"], name="attn_a_bwd", **a_geom)
            dsink[j] = ds
            dqkv = jnp.concatenate([dq, dk[0].astype(BF16), dk[1].astype(BF16), dv[0].astype(BF16), dv[1].astype(BF16)], axis=1)
            dW[("a_in", j)] = _weight_grad(sv["h1"], dqkv, transpose_out=True, name="dw_a_in")
            dx0, st_m = _norm_bwd(dqkv, W[("a_in", j)], sv["x0"], dx1, nm, sc1, name="proj_a_bwd")
        else:
            do, dy1, st_g1 = _gate_bwd(dx1, sv["y1"], g1, W[("b_out", j)], None, w_is_transposed=True, name="out_b_bwd")
            dW[("b_out", j)] = _weight_grad(dy1, sv["o"], transpose_out=False, name="dw_b_out").reshape(N_DEV * 64, D)
            gr = [_attn_bwd(sv["qkv"], bias_b[g], None, sv["o"], do, sv["lse"], name="attn_b%d_bwd" % g, **b_geom[g]) for g in range(3)]
            dqkv = jnp.concatenate([t[0] for t in gr] + [t[1][0].astype(BF16) for t in gr] + [t[2][0].astype(BF16) for t in gr], axis=1)
            dW[("b_in", j)] = _weight_grad(sv["h1"], dqkv, transpose_out=True, name="dw_b_in")
            dx0, st_m = _norm_bwd(dqkv, W[("b_in", j)], sv["x0"], dx1, nm, sc1, name="proj_b_bwd")
        stat_tiles[i] = [st_m, st_g1, st_f, st_g2]
        dx = dx0
    grad_x = dx.reshape(1, S, D)

    parts_g = _grad_exchange([dW[(n, l)] for n, l, _ in SEGMENTS])
    masters = {"ffn_in": (ffn_w_in, m_ffn_w_in, v_ffn_w_in), "ffn_out": (ffn_w_out, m_ffn_w_out, v_ffn_w_out),
               "a_in": (a_w_in, m_a_w_in, v_a_w_in), "a_out": (a_w_out, m_a_w_out, v_a_w_out),
               "b_in": (b_w_in, m_b_w_in, v_b_w_in), "b_out": (b_w_out, m_b_w_out, v_b_w_out)}
    rows_wmv = [jnp.concatenate([_to_rows(n, masters[n][q][l]) for n, l, _ in SEGMENTS], axis=0) for q in range(3)]
    res_rows = _adam_shard(parts_g, *rows_wmv)
    big = {}
    for q, kind in enumerate(("grad", "delta", "m", "v")):
        for n in masters:
            layers = [_from_rows(n, res_rows[q][SEG_OFF[s]:SEG_OFF[s] + SEG_ROWS[s]])
                      for s, (sn, l, _) in enumerate(SEGMENTS) if sn == n]
            big[(kind, n)] = jnp.stack(layers)

    tiles = jnp.concatenate([t for i in range(DEPTH) for t in stat_tiles[i]] + [head_stats]
                            + [jnp.pad(ds, ((0, 0), (0, D - LANES))) for ds in dsink], axis=0)
    small = [_pack_small(*t) for t in ((ada_b, norm_mix, norm_ffn, final_norm, a_sink),
                                       (m_ada_b, m_norm_mix, m_norm_ffn, m_final_norm, m_a_sink),
                                       (v_ada_b, v_norm_mix, v_norm_ffn, v_final_norm, v_a_sink))]
    dmod_all, sg, sd, sm, sv_, loss_tile = _small_exchange(tiles, *small)
    loss = loss_tile[0, 0]
    dmod_all = dmod_all.reshape(N_DEV, DEPTH, 6 * D)
    dmod_mine = lax.dynamic_slice_in_dim(dmod_all, me * ncol, ncol, axis=2)
    dmod_pad = jnp.pad(jnp.transpose(dmod_mine, (1, 0, 2)), ((0, 0), (0, LANES - N_DEV), (0, 0))).astype(BF16)
    cond_t = jnp.pad(cond_all.T, ((0, 0), (0, LANES - N_DEV))).astype(BF16)
    ada = _adam_ada_w(cond_t, dmod_pad, ada_w, m_ada_w, v_ada_w)

    outs = [loss, grad_x]
    small_res = [_unpack_small(t) for t in (sg, sd, sm, sv_)]
    for q, kind in enumerate(("grad", "delta", "m", "v")):
        ab, nm_, nf_, fn, sk = small_res[q]
        outs += [ada[q], ab, nm_, nf_, big[(kind, "ffn_in")], big[(kind, "ffn_out")], big[(kind, "a_in")], big[(kind, "a_out")],
                 sk, big[(kind, "b_in")], big[(kind, "b_out")], fn]
    return tuple(outs)
```

```python
import functools
import math

import numpy as np
import jax
import jax.numpy as jnp
from jax import lax
from jax.experimental import pallas as pl
from jax.experimental.pallas import tpu as pltpu

D = 1024
HEAD_DIM = 64
D_FF = 2816
DEPTH = 4
N_DEV = 8
A_QKV = 1536
B_QKV = 2304
A_HALF = 128
B_HALF = 64
B_DILS = (1, 4, 16)
RMS_EPS = 1e-6
NEG = -1e30
ADAM_LR = 0.001
ADAM_B1 = 0.9
ADAM_B2 = 0.999
ADAM_EPS = 1e-08
ADAM_WD = 0.01
ADAM_STEP = 10

LANES = 128
TQ = 128
VMEM_LIMIT = 56 * 1024 * 1024
MESH = pl.DeviceIdType.MESH
F32 = jnp.float32
BF16 = jnp.bfloat16

SEGMENTS = ([("ffn_in", l, 704) for l in range(4)] + [("ffn_out", l, 352) for l in range(4)]
            + [("a_in", j, 192) for j in range(2)] + [("a_out", j, 128) for j in range(2)]
            + [("b_in", j, 288) for j in range(2)] + [("b_out", j, 64) for j in range(2)])
def _layer_segments(i):
    mixer = "a" if i % 2 == 0 else "b"
    return [s for s in SEGMENTS if (s[0].startswith("ffn") and s[1] == i) or (s[0].startswith(mixer + "_") and s[1] == i // 2)]


def _offsets(segs):
    rows = [s[2] for s in segs]
    return [sum(rows[:k]) for k in range(len(rows))], sum(rows)
STAT_ROWS = 56


def _nn(a, b):
    return jnp.dot(a, b, preferred_element_type=F32)


def _nt(a, b):
    return lax.dot_general(a, b, (((1,), (1,)), ((), ())), preferred_element_type=F32)


def _tn(a, b):
    return lax.dot_general(a, b, (((0,), (0,)), ((), ())), preferred_element_type=F32)


def _params(dims=None, vmem=None):
    kw = {}
    if dims is not None:
        kw["dimension_semantics"] = dims
    if vmem is not None:
        kw["vmem_limit_bytes"] = vmem
    return pltpu.CompilerParams(**kw)


def _my_index():
    return 4 * lax.axis_index("x") + 2 * lax.axis_index("y") + lax.axis_index("c")


def _peer(k):
    x, y, c = lax.axis_index("x"), lax.axis_index("y"), lax.axis_index("c")
    px, py, pc = x ^ ((k >> 2) & 1), y ^ ((k >> 1) & 1), c ^ (k & 1)
    return (px, py, pc), 4 * px + 2 * py + pc


def _const_spec(shape):
    nd = len(shape)
    return pl.BlockSpec(shape, lambda *_: (0,) * nd)


def _cond_exchange(c_tile, ada_w, ada_b_mine):
    ncol = ada_w.shape[-1]

    def body(c_ref, w_ref, b_ref, cond_ref, parts_ref, call_ref, mine_ref, send_sems, recv_sems):
        me = _my_index()
        call_ref[me] = c_ref[...]
        copies = []
        for k in range(1, N_DEV):
            dev, _ = _peer(k)
            cp = pltpu.make_async_remote_copy(src_ref=c_ref, dst_ref=call_ref.at[me], send_sem=send_sems.at[0, k - 1],
                                              recv_sem=recv_sems.at[0, k - 1], device_id=dev, device_id_type=MESH)
            cp.start()
            copies.append(cp)
        for k in range(1, N_DEV):
            _, pidx = _peer(k)
            pltpu.make_async_remote_copy(src_ref=c_ref, dst_ref=call_ref.at[pidx], send_sem=send_sems.at[0, k - 1],
                                         recv_sem=recv_sems.at[0, k - 1], device_id=_peer(k)[0], device_id_type=MESH).wait_recv()
        for cp in copies:
            cp.wait_send()
        row = lax.broadcasted_iota(jnp.int32, (N_DEV, D), 0)
        cmat = jnp.zeros((N_DEV, D), F32)
        for j in range(N_DEV):
            cmat = jnp.where(row == j, call_ref[j], cmat)
        cond = cmat * jax.nn.sigmoid(cmat)
        cond_ref[...] = cond
        cb = cond.astype(BF16)
        for l in range(DEPTH):
            mine_ref[l] = _nn(cb, w_ref[l].astype(BF16)) + b_ref[pl.ds(l, 1), :]
        parts_ref[me] = mine_ref[...]
        copies = []
        for k in range(1, N_DEV):
            dev, _ = _peer(k)
            cp = pltpu.make_async_remote_copy(src_ref=mine_ref, dst_ref=parts_ref.at[me], send_sem=send_sems.at[1, k - 1],
                                              recv_sem=recv_sems.at[1, k - 1], device_id=dev, device_id_type=MESH)
            cp.start()
            copies.append(cp)
        for k in range(1, N_DEV):
            dev, pidx = _peer(k)
            pltpu.make_async_remote_copy(src_ref=mine_ref, dst_ref=parts_ref.at[pidx], send_sem=send_sems.at[1, k - 1],
                                         recv_sem=recv_sems.at[1, k - 1], device_id=dev, device_id_type=MESH).wait_recv()
        for cp in copies:
            cp.wait_send()

    vm = pl.BlockSpec(memory_space=pltpu.VMEM)
    return pl.pallas_call(
        body, name="cond_exchange",
        out_shape=(jax.ShapeDtypeStruct((N_DEV, D), F32), jax.ShapeDtypeStruct((N_DEV, DEPTH, N_DEV, ncol), F32)),
        in_specs=[vm, vm, vm], out_specs=(vm, vm),
        scratch_shapes=[pltpu.VMEM((N_DEV, N_DEV, D), F32), pltpu.VMEM((DEPTH, N_DEV, ncol), F32),
                        pltpu.SemaphoreType.DMA((2, N_DEV - 1)), pltpu.SemaphoreType.DMA((2, N_DEV - 1))],
        compiler_params=_params(vmem=VMEM_LIMIT),
    )(c_tile, ada_w, ada_b_mine)[:2]


def _all_gather_weights(shards):
    n = len(shards)
    big = max(range(n), key=lambda s: shards[s].shape[0])
    total = sum(sh.shape[0] for sh in shards)
    assert N_DEV * shards[big].shape[0] >= total

    def body(*refs):
        ins, outs = refs[:n], refs[n:2 * n]
        local_sems, send_sems, recv_sems = refs[2 * n:]
        me = _my_index()
        local = []
        for s in range(n):
            rows = ins[s].shape[0]
            cp = pltpu.make_async_copy(ins[s], outs[s].at[pl.ds(me * rows, rows)], local_sems.at[s])
            cp.start()
            local.append(cp)
        for k in range(1, N_DEV):
            dev, _ = _peer(k)
            for s in range(n):
                rows = ins[s].shape[0]
                pltpu.make_async_remote_copy(src_ref=ins[s], dst_ref=outs[s].at[pl.ds(me * rows, rows)],
                                             send_sem=send_sems.at[k - 1], recv_sem=recv_sems.at[k - 1],
                                             device_id=dev, device_id_type=MESH).start()
        whole = outs[big].at[pl.ds(0, total)]
        for k in range(1, N_DEV):
            dev, _ = _peer(k)
            w = pltpu.make_async_remote_copy(src_ref=whole, dst_ref=whole, send_sem=send_sems.at[k - 1],
                                             recv_sem=recv_sems.at[k - 1], device_id=dev, device_id_type=MESH)
            w.wait_send()
            w.wait_recv()
        for cp in local:
            cp.wait()

    hbm = pl.BlockSpec(memory_space=pl.ANY)
    return pl.pallas_call(
        body, name="weight_all_gather",
        out_shape=tuple(jax.ShapeDtypeStruct((N_DEV * s.shape[0], D), s.dtype) for s in shards),
        in_specs=[hbm] * n, out_specs=tuple([hbm] * n),
        scratch_shapes=[pltpu.SemaphoreType.DMA((n,)), pltpu.SemaphoreType.DMA((N_DEV - 1,)),
                        pltpu.SemaphoreType.DMA((N_DEV - 1,))],
    )(*shards)


HBM = pl.BlockSpec(memory_space=pltpu.HBM)
SEM = pl.BlockSpec(memory_space=pltpu.SEMAPHORE)
EFFECT = pltpu.SideEffectType.DATAFLOW_SIDE_EFFECTING


def _exchange_start(srcs, landing, rows, offs, to_peer_rows, name):
    n = len(srcs)

    def body(*refs):
        src_refs, land_ref = refs[:n], refs[n]
        send_sems, recv_sems = refs[n + 1], refs[n + 2]
        token = refs[-1]
        me = _my_index()
        for k in range(1, N_DEV):
            dev, pidx = _peer(k)
            for q in range(n):
                src = src_refs[q].at[pl.ds(pidx * rows[q], rows[q])] if to_peer_rows else src_refs[q]
                pltpu.make_async_remote_copy(src_ref=src, dst_ref=land_ref.at[me, pl.ds(offs[q], rows[q])],
                                             send_sem=send_sems.at[k - 1], recv_sem=recv_sems.at[k - 1],
                                             device_id=dev, device_id_type=MESH).start()
        token[...] = jnp.zeros_like(token)

    arrays = list(srcs) + [landing]
    return pl.pallas_call(
        body, name=name,
        out_shape=(pltpu.SemaphoreType.DMA((N_DEV - 1,)), pltpu.SemaphoreType.DMA((N_DEV - 1,)),
                   *[pltpu.HBM(a.shape, a.dtype) for a in arrays], jax.ShapeDtypeStruct((8, LANES), F32)),
        in_specs=[HBM] * (n + 1), out_specs=(SEM, SEM, *[HBM] * (n + 1), pl.BlockSpec(memory_space=pltpu.VMEM)),
        input_output_aliases={q: 2 + q for q in range(n + 1)},
        compiler_params=pltpu.CompilerParams(has_side_effects=EFFECT),
    )(*[pltpu.with_memory_space_constraint(a, pltpu.HBM) for a in arrays])


def _exchange_wait(started, after, name):
    send_sems, recv_sems = started[0], started[1]
    arrays = list(started[2:-1])
    n1 = len(arrays)

    def body(*refs):
        land_ref = refs[n1 - 1]
        sends, recvs = refs[n1], refs[n1 + 1]
        for k in range(1, N_DEV):
            dev, _ = _peer(k)
            w = pltpu.make_async_remote_copy(src_ref=land_ref.at[0], dst_ref=land_ref.at[0], send_sem=sends.at[k - 1],
                                             recv_sem=recvs.at[k - 1], device_id=dev, device_id_type=MESH)
            w.wait_send()
            w.wait_recv()

    return pl.pallas_call(
        body, name=name, out_shape=tuple(pltpu.HBM(a.shape, a.dtype) for a in arrays),
        in_specs=[HBM] * n1 + [SEM, SEM, pl.BlockSpec(memory_space=pl.ANY)], out_specs=tuple([HBM] * n1),
        input_output_aliases={q: q for q in range(n1)},
        compiler_params=pltpu.CompilerParams(has_side_effects=EFFECT),
    )(*arrays, send_sems, recv_sems, after)[n1 - 1]


def _norm_mod(x, nw, sc, sh):
    ms = jnp.mean(x * x, axis=-1, keepdims=True)
    xh = x * lax.rsqrt(ms + RMS_EPS)
    return xh, (xh * nw) * (1.0 + sc) + sh


def _proj(x, nw, sc, sh, wt, *, ffn, name):
    S, N = x.shape[0], wt.shape[0]
    tm = 256 if ffn else 512

    def body(x_ref, nw_ref, sc_ref, sh_ref, w_ref, h_ref, out_ref, *act_ref):
        _, h = _norm_mod(x_ref[...], nw_ref[...], sc_ref[...], sh_ref[...])
        hb = h.astype(BF16)
        h_ref[...] = hb
        if ffn:
            gate = _nt(hb, w_ref[pl.ds(0, D_FF), :])
            up = _nt(hb, w_ref[pl.ds(D_FF, D_FF), :])
            out_ref[:, pl.ds(0, D_FF)] = gate.astype(BF16)
            out_ref[:, pl.ds(D_FF, D_FF)] = up.astype(BF16)
            act_ref[0][...] = ((gate * jax.nn.sigmoid(gate)) * up).astype(BF16)
        else:
            out_ref[...] = _nt(hb, w_ref[...]).astype(BF16)

    row = lambda w: pl.BlockSpec((tm, w), lambda i: (i, 0))
    out_shape = [jax.ShapeDtypeStruct((S, D), BF16), jax.ShapeDtypeStruct((S, N), BF16)]
    out_specs = [row(D), row(N)]
    if ffn:
        out_shape.append(jax.ShapeDtypeStruct((S, D_FF), BF16))
        out_specs.append(row(D_FF))
    vec = _const_spec((1, D))
    return pl.pallas_call(
        body, name=name, grid=(S // tm,), out_shape=tuple(out_shape),
        in_specs=[row(D), vec, vec, vec, _const_spec((N, D))], out_specs=tuple(out_specs),
        compiler_params=_params(("parallel",), VMEM_LIMIT),
    )(x, nw, sc, sh, wt)


def _gated_residual(a, w, x, g, *, w_is_transposed, name):
    S, K = a.shape
    tm = 512

    def body(a_ref, w_ref, x_ref, g_ref, xo_ref, y_ref):
        y = _nt(a_ref[...], w_ref[...]) if w_is_transposed else _nn(a_ref[...], w_ref[...])
        y_ref[...] = y.astype(BF16)
        xo_ref[...] = x_ref[...] + g_ref[...] * y

    row = lambda w_: pl.BlockSpec((tm, w_), lambda i: (i, 0))
    return pl.pallas_call(
        body, name=name, grid=(S // tm,),
        out_shape=(jax.ShapeDtypeStruct((S, D), F32), jax.ShapeDtypeStruct((S, D), BF16)),
        in_specs=[row(K), _const_spec(w.shape), row(D), _const_spec((1, D))], out_specs=(row(D), row(D)),
        compiler_params=_params(("parallel",), VMEM_LIMIT),
    )(a, w, x, g)


CHUNK = 1024


def _tile_rows(r, chunk=CHUNK):
    return min(TQ, chunk // r)


def _alibi_bias(slopes, half, dil, chunk=CHUNK, both=False):
    tq = _tile_rows(dil, chunk)
    tk = tq + 2 * half
    rel = np.arange(tk)[:, None] - half - np.arange(tq)[None, :]
    band = np.abs(rel) <= half
    dist = (dil * np.abs(rel)).astype(np.float32)
    tabs = [np.where(band, -np.float32(s) * dist, np.float32(NEG)).astype(np.float32) for s in slopes]
    out = []
    for u in range(0, len(tabs), 8):
        tab = np.concatenate(tabs[u:u + 8], axis=1)
        first, last = tab.copy(), tab.copy()
        first[:half] = NEG
        last[tk - half:] = NEG
        out += [tab, first, last]
        if both:
            last = last.copy()
            last[:half] = NEG
            out.append(last)
    return jnp.asarray(np.concatenate(out, axis=0))


def _slopes(n):
    return (2.0 ** (-8.0 * np.arange(1, n + 1) / n)).astype(np.float32)


def _head_masks(tq):
    lane = lax.broadcasted_iota(jnp.int32, (tq, LANES), 1)
    lo = lane < HEAD_DIM
    return lo, jnp.logical_not(lo)


def _stack_heads(tiles, lo, hi, scale):
    blocks = []
    for t in range(4):
        xf = tiles[t] if scale == 1.0 else tiles[t] * scale
        for a in range(2):
            xm = jnp.where(lo if a == 0 else hi, xf, 0.0)
            if a != t // 2:
                xm = pltpu.roll(xm, HEAD_DIM, 1)
            blocks.append(xm.astype(BF16))
    return jnp.concatenate(blocks, axis=0)


def _tile_from_columns(x8t, t, tq):
    r0 = HEAD_DIM * (t // 2)
    top = x8t[r0:r0 + HEAD_DIM, 2 * t * tq:(2 * t + 1) * tq]
    bot = x8t[r0:r0 + HEAD_DIM, (2 * t + 1) * tq:(2 * t + 2) * tq]
    return jnp.concatenate([top, bot], axis=0).T


def _attn_layout(S, C, r, half, qoff, koff, voff, chunk):
    hb = half * r
    per = chunk // hb
    nhb = S // hb
    main = lambda off: pl.BlockSpec((chunk, LANES), lambda u, i: (i, off // LANES + u))
    prev = lambda off: pl.BlockSpec((hb, LANES), lambda u, i: (jnp.maximum(i * per - 1, 0), off // LANES + u))
    nxt = lambda off: pl.BlockSpec((hb, LANES), lambda u, i: (jnp.minimum((i + 1) * per, nhb - 1), off // LANES + u))
    specs = [pl.BlockSpec((chunk, 4 * LANES), lambda u, i: (i, qoff // (4 * LANES) + u))]
    specs += [prev(koff), main(koff), nxt(koff), prev(voff), main(voff), nxt(voff)]
    return specs, hb


def _stage(dst, srcs):
    row = 0
    for src in srcs:
        n = src.shape[0]
        dst[pl.ds(row, n), :] = src[...].astype(F32)
        row += n


def _rows(start, n, r):
    return pl.ds(start, n, stride=r) if r > 1 else pl.ds(start, n)


def _attn_fwd(qkv, bias, sink, *, C, r, half, qoff, koff, voff, n_units, out_dtype, name):
    S = qkv.shape[0]
    chunk = max(CHUNK, TQ * r)
    tq = _tile_rows(r, chunk)
    tk = tq + 2 * half
    tiles = chunk // (r * tq)
    nsteps = S // chunk
    specs, hb = _attn_layout(S, C, r, half, qoff, koff, voff, chunk)
    use_sink = sink is not None

    def body(*refs):
        q_ref, kp, km, kn, vp, vm, vn, bias_ref = refs[:8]
        rest = list(refs[8:])
        sink_ref = rest.pop(0) if use_sink else None
        o_ref, lse_ref, qs, ks, vs, os_, ls = rest
        i = pl.program_id(1)
        for t in range(4):
            qs[t] = q_ref[:, pl.ds(t * LANES, LANES)].astype(F32)
        _stage(ks, [kp, km, kn])
        _stage(vs, [vp, vm, vn])
        lo, hi = _head_masks(tq)
        ones = jnp.ones((16, tk), BF16)
        if use_sink:
            sk = sink_ref[pl.ds(0, 1), :]

        def chain(n, carry):
            rho, c = n // tiles, n % tiles
            start = c * (tq * r) + rho
            if r == 1:
                start = pl.multiple_of(start, tq)
            variant = jnp.where(jnp.logical_and(i == 0, c == 0), 1, 0) + jnp.where(
                jnp.logical_and(i == nsteps - 1, c == tiles - 1), 2, 0)
            k2 = ks[_rows(start, tk, r), :].astype(BF16)
            v2t = jnp.concatenate([vs[_rows(start, tk, r), :].T.astype(BF16), ones], axis=0)
            q8 = _stack_heads([qs[t, _rows(start, tq, r), :] for t in range(4)], lo, hi, HEAD_DIM ** -0.5)
            s = _nt(k2, q8) + bias_ref[pl.ds(pl.multiple_of(variant * tk, 8), tk), :]
            m = jnp.max(s, axis=0, keepdims=True)
            if use_sink:
                m = jnp.maximum(m, sk)
            pv = _nn(v2t, jnp.exp(s - m).astype(BF16))
            l = pv[LANES:LANES + 1]
            if use_sink:
                l = l + jnp.exp(sk - m)
            o8t = pv[:LANES] / l
            lse8 = jnp.broadcast_to(m + jnp.log(l), (LANES, 8 * tq))
            for t in range(4):
                os_[t, _rows(start, tq, r), :] = _tile_from_columns(o8t, t, tq)
                ls[t, _rows(start, tq, r), :] = _tile_from_columns(lse8, t, tq)
            return carry

        lax.fori_loop(0, r * tiles, chain, 0, unroll=2)
        for t in range(4):
            o_ref[:, pl.ds(t * LANES, LANES)] = os_[t].astype(out_dtype)
            lse_ref[:, pl.ds(t * LANES, LANES)] = ls[t]

    in_specs = specs + [pl.BlockSpec((bias.shape[0] // n_units, 8 * tq), lambda u, i: (u, 0))]
    args = [qkv] * 7 + [bias]
    if use_sink:
        in_specs.append(pl.BlockSpec((8, 8 * tq), lambda u, i: (u, 0)))
        args.append(sink)
    wide = pl.BlockSpec((chunk, 4 * LANES), lambda u, i: (i, u))
    win = hb + chunk + hb
    return pl.pallas_call(
        body, name=name, grid=(n_units, nsteps),
        out_shape=(jax.ShapeDtypeStruct((S, n_units * 512), out_dtype), jax.ShapeDtypeStruct((S, n_units * 512), F32)),
        in_specs=in_specs, out_specs=(wide, wide),
        scratch_shapes=[pltpu.VMEM((4, chunk, LANES), F32), pltpu.VMEM((win, LANES), F32), pltpu.VMEM((win, LANES), F32),
                        pltpu.VMEM((4, chunk, LANES), F32), pltpu.VMEM((4, chunk, LANES), F32)],
        compiler_params=_params(("parallel", "parallel"), VMEM_LIMIT),
    )(*args)


def _attn_bwd(qkv, bias, sink, o, do, lse, *, C, r, half, qoff, koff, voff, n_units, name):
    S = qkv.shape[0]
    tq = _tile_rows(r)
    tk = tq + 2 * half
    tiles = CHUNK // (r * tq)
    nsteps = S // CHUNK
    specs, hb = _attn_layout(S, C, r, half, qoff, koff, voff, CHUNK)
    use_sink = sink is not None

    def body(*refs):
        q_ref, kp, km, kn, vp, vm, vn, bias_ref = refs[:8]
        rest = list(refs[8:])
        sink_ref = rest.pop(0) if use_sink else None
        o_ref, do_ref, lse_ref, dq_ref, dk_hbm, dv_hbm = rest[:6]
        rest = rest[6:]
        dsink_ref = rest.pop(0) if use_sink else None
        qs, ks, vs, os_, dos, ls, dqs, acck, accv, sem = rest
        u, i = pl.program_id(0), pl.program_id(1)

        @pl.when(i == 0)
        def _():
            acck[...] = jnp.zeros_like(acck)
            accv[...] = jnp.zeros_like(accv)
            if use_sink:
                dsink_ref[...] = jnp.zeros_like(dsink_ref)

        for t in range(4):
            cols = pl.ds(t * LANES, LANES)
            qs[t] = q_ref[:, cols].astype(F32)
            os_[t] = o_ref[:, cols].astype(F32)
            dos[t] = do_ref[:, cols].astype(F32)
            ls[t] = lse_ref[:, cols]
        _stage(ks, [kp, km, kn])
        _stage(vs, [vp, vm, vn])
        lo, hi = _head_masks(tq)
        base = pl.multiple_of(i * CHUNK, CHUNK)
        if use_sink:
            sk = sink_ref[pl.ds(0, 1), :]

        def chain(n, carry):
            rho, c = n // tiles, n % tiles
            start = c * (tq * r) + rho
            if r == 1:
                start = pl.multiple_of(start, tq)
            variant = jnp.where(jnp.logical_and(i == 0, c == 0), 1, 0) + jnp.where(
                jnp.logical_and(i == nsteps - 1, c == tiles - 1), 2, 0)
            k2 = ks[_rows(start, tk, r), :].astype(BF16)
            v2 = vs[_rows(start, tk, r), :].astype(BF16)
            k2t = ks[_rows(start, tk, r), :].T.astype(BF16)
            q8 = _stack_heads([qs[t, _rows(start, tq, r), :] for t in range(4)], lo, hi, HEAD_DIM ** -0.5)
            do_tiles = [dos[t, _rows(start, tq, r), :] for t in range(4)]
            do8 = _stack_heads(do_tiles, lo, hi, 1.0)
            deltas, lses = [], []
            for t in range(4):
                prod_t = (do_tiles[t] * os_[t, _rows(start, tq, r), :]).T
                lse_t = ls[t, _rows(start, tq, r), :].T
                for a in range(2):
                    deltas.append(jnp.sum(prod_t[a * HEAD_DIM:(a + 1) * HEAD_DIM], axis=0, keepdims=True))
                    lses.append(lse_t[a * HEAD_DIM:a * HEAD_DIM + 1])
            delta8 = jnp.concatenate(deltas, axis=1)
            lse8 = jnp.concatenate(lses, axis=1)
            s = _nt(k2, q8) + bias_ref[pl.ds(pl.multiple_of(variant * tk, 8), tk), :]
            p = jnp.exp(s - lse8)
            dp = _nt(v2, do8)
            dsb = (p * (dp - delta8)).astype(BF16)
            dq8t = _nn(k2t, dsb)
            for t in range(4):
                dqs[t, _rows(start, tq, r), :] = _tile_from_columns(dq8t, t, tq) * (HEAD_DIM ** -0.5)
            arow = base + start
            if r == 1:
                arow = pl.multiple_of(arow, tq)
            acck[_rows(arow, tk, r), :] = acck[_rows(arow, tk, r), :] + _nn(dsb, q8)
            accv[_rows(arow, tk, r), :] = accv[_rows(arow, tk, r), :] + _nn(p.astype(BF16), do8)
            if use_sink:
                e = jnp.exp(sk - lse8) * delta8
                for h in range(8):
                    part = -jnp.sum(e[:, h * tq:(h + 1) * tq], axis=1, keepdims=True)
                    dsink_ref[pl.ds(h, 1), :] = dsink_ref[pl.ds(h, 1), :] + part
            return carry

        lax.fori_loop(0, r * tiles, chain, 0, unroll=2)
        for t in range(4):
            dq_ref[:, pl.ds(t * LANES, LANES)] = dqs[t].astype(BF16)

        @pl.when(i == nsteps - 1)
        def _():
            ck = pltpu.make_async_copy(acck.at[pl.ds(hb, S)], dk_hbm.at[u], sem.at[0])
            cv = pltpu.make_async_copy(accv.at[pl.ds(hb, S)], dv_hbm.at[u], sem.at[1])
            ck.start()
            cv.start()
            ck.wait()
            cv.wait()

    wide = pl.BlockSpec((CHUNK, 4 * LANES), lambda u, i: (i, u))
    hbm = pl.BlockSpec(memory_space=pl.ANY)
    in_specs = specs + [pl.BlockSpec((3 * tk, 8 * tq), lambda u, i: (u, 0))]
    args = [qkv] * 7 + [bias]
    if use_sink:
        in_specs.append(pl.BlockSpec((8, 8 * tq), lambda u, i: (u, 0)))
        args.append(sink)
    in_specs += [wide, wide, wide]
    args += [o, do, lse]
    out_shape = [jax.ShapeDtypeStruct((S, n_units * 512), BF16), jax.ShapeDtypeStruct((n_units, S, LANES), F32),
                 jax.ShapeDtypeStruct((n_units, S, LANES), F32)]
    out_specs = [wide, hbm, hbm]
    if use_sink:
        out_shape.append(jax.ShapeDtypeStruct((n_units * 8, LANES), F32))
        out_specs.append(pl.BlockSpec((8, LANES), lambda u, i: (u, 0)))
    win = hb + CHUNK + hb
    big = lambda: pltpu.VMEM((4, CHUNK, LANES), F32)
    res = pl.pallas_call(
        body, name=name, grid=(n_units, nsteps), out_shape=tuple(out_shape), in_specs=in_specs, out_specs=tuple(out_specs),
        scratch_shapes=[big(), pltpu.VMEM((win, LANES), F32), pltpu.VMEM((win, LANES), F32), big(), big(), big(), big(),
                        pltpu.VMEM((S + 2 * hb, LANES), F32), pltpu.VMEM((S + 2 * hb, LANES), F32), pltpu.SemaphoreType.DMA((2,))],
        compiler_params=_params(("arbitrary", "arbitrary"), VMEM_LIMIT),
    )(*args)
    return res[0], res[1], res[2], (res[3] if use_sink else None)


def _merge_groups(os_, lses):
    S, W = os_[0].shape
    tm = 512

    def body(o0, o1, o2, l0, l1, l2, o_ref, lse_ref):
        ls = [l0[...], l1[...], l2[...]]
        mx = jnp.maximum(jnp.maximum(ls[0], ls[1]), ls[2])
        es = [jnp.exp(l - mx) for l in ls]
        den = es[0] + es[1] + es[2]
        o = (es[0] / den) * o0[...] + (es[1] / den) * o1[...] + (es[2] / den) * o2[...]
        o_ref[...] = o.astype(BF16)
        lse_ref[...] = mx + jnp.log(den)

    row = pl.BlockSpec((tm, W), lambda i: (i, 0))
    return pl.pallas_call(
        body, name="merge_groups", grid=(S // tm,),
        out_shape=(jax.ShapeDtypeStruct((S, W), BF16), jax.ShapeDtypeStruct((S, W), F32)),
        in_specs=[row] * 6, out_specs=(row, row), compiler_params=_params(("parallel",), VMEM_LIMIT),
    )(*os_, *lses)


def _loss_head(x, target, fnw):
    S = x.shape[0]
    tm = 512

    def body(x_ref, t_ref, w_ref, dx_ref, st_ref):
        @pl.when(pl.program_id(0) == 0)
        def _():
            st_ref[...] = jnp.zeros_like(st_ref)

        xv = x_ref[...]
        rstd = lax.rsqrt(jnp.mean(xv * xv, axis=-1, keepdims=True) + RMS_EPS)
        xh = xv * rstd
        err = xh * w_ref[...] - t_ref[...]
        dy = err * (1.0 / D)
        dxh = dy * w_ref[...]
        dx_ref[...] = rstd * (dxh - xh * jnp.mean(dxh * xh, axis=-1, keepdims=True))
        st_ref[pl.ds(0, 1), :] = st_ref[pl.ds(0, 1), :] + jnp.sum(dy * xh, axis=0, keepdims=True)
        st_ref[pl.ds(1, 1), :] = st_ref[pl.ds(1, 1), :] + jnp.sum(err * err, axis=0, keepdims=True)

    row = pl.BlockSpec((tm, D), lambda i: (i, 0))
    return pl.pallas_call(
        body, name="loss_head", grid=(S // tm,),
        out_shape=(jax.ShapeDtypeStruct((S, D), F32), jax.ShapeDtypeStruct((8, D), F32)),
        in_specs=[row, row, _const_spec((1, D))], out_specs=(row, _const_spec((8, D))),
        compiler_params=_params(("arbitrary",), VMEM_LIMIT),
    )(x, target, fnw)


def _gate_bwd(dx, y, g, w, gu, *, w_is_transposed, name):
    S = dx.shape[0]
    K = w.shape[1] if w_is_transposed else w.shape[0]
    ffn = gu is not None
    tm = 256 if ffn else 512
    wout = 2 * K if ffn else K

    def body(dx_ref, y_ref, g_ref, w_ref, *rest):
        if ffn:
            gu_ref, da_ref, dyb_ref, st_ref = rest
        else:
            da_ref, dyb_ref, st_ref = rest

        @pl.when(pl.program_id(0) == 0)
        def _():
            st_ref[...] = jnp.zeros_like(st_ref)

        dxv = dx_ref[...]
        st_ref[pl.ds(0, 1), :] = st_ref[pl.ds(0, 1), :] + jnp.sum(dxv * y_ref[...].astype(F32), axis=0, keepdims=True)
        dyb = (dxv * g_ref[...]).astype(BF16)
        dyb_ref[...] = dyb
        da = _nn(dyb, w_ref[...]) if w_is_transposed else _nt(dyb, w_ref[...])
        if ffn:
            gate = gu_ref[:, pl.ds(0, K)].astype(F32)
            up = gu_ref[:, pl.ds(K, K)].astype(F32)
            sig = jax.nn.sigmoid(gate)
            da_ref[:, pl.ds(0, K)] = (da * up * (sig * (1.0 + gate * (1.0 - sig)))).astype(BF16)
            da_ref[:, pl.ds(K, K)] = (da * (gate * sig)).astype(BF16)
        else:
            da_ref[...] = da.astype(BF16)

    row = lambda w_: pl.BlockSpec((tm, w_), lambda i: (i, 0))
    in_specs = [row(D), row(D), _const_spec((1, D)), _const_spec(w.shape)]
    args = [dx, y, g, w]
    if ffn:
        in_specs.append(row(wout))
        args.append(gu)
    return pl.pallas_call(
        body, name=name, grid=(S // tm,),
        out_shape=(jax.ShapeDtypeStruct((S, wout), BF16), jax.ShapeDtypeStruct((S, D), BF16), jax.ShapeDtypeStruct((8, D), F32)),
        in_specs=in_specs, out_specs=(row(wout), row(D), _const_spec((8, D))),
        compiler_params=_params(("arbitrary",), VMEM_LIMIT),
    )(*args)


def _norm_bwd(dy, wt, x, dres, nw, sc, *, name):
    S, N = dy.shape
    tm = 256 if N > 4096 else 512

    def body(dy_ref, w_ref, x_ref, dres_ref, nw_ref, sc_ref, dx_ref, st_ref):
        @pl.when(pl.program_id(0) == 0)
        def _():
            st_ref[...] = jnp.zeros_like(st_ref)

        dh = _nn(dy_ref[...], w_ref[...])
        xv = x_ref[...]
        rstd = lax.rsqrt(jnp.mean(xv * xv, axis=-1, keepdims=True) + RMS_EPS)
        xh = xv * rstd
        nwv, scale = nw_ref[...], 1.0 + sc_ref[...]
        dxh = dh * (nwv * scale)
        dx_ref[...] = dres_ref[...] + rstd * (dxh - xh * jnp.mean(dxh * xh, axis=-1, keepdims=True))
        dhx = dh * xh
        st_ref[pl.ds(0, 1), :] = st_ref[pl.ds(0, 1), :] + jnp.sum(dh, axis=0, keepdims=True)
        st_ref[pl.ds(1, 1), :] = st_ref[pl.ds(1, 1), :] + jnp.sum(dhx * nwv, axis=0, keepdims=True)
        st_ref[pl.ds(2, 1), :] = st_ref[pl.ds(2, 1), :] + jnp.sum(dhx * scale, axis=0, keepdims=True)

    row = lambda w_: pl.BlockSpec((tm, w_), lambda i: (i, 0))
    vec = _const_spec((1, D))
    return pl.pallas_call(
        body, name=name, grid=(S // tm,),
        out_shape=(jax.ShapeDtypeStruct((S, D), F32), jax.ShapeDtypeStruct((8, D), F32)),
        in_specs=[row(N), _const_spec((N, D)), row(D), row(D), vec, vec], out_specs=(row(D), _const_spec((8, D))),
        compiler_params=_params(("arbitrary",), VMEM_LIMIT),
    )(dy, wt, x, dres, nw, sc)


def _weight_grad(a, b, *, transpose_out, name):
    S, N = b.shape
    nb = N // 2 if N > 4096 else N
    tk = 512

    def body(a_ref, b_ref, out_ref, acc):
        k = pl.program_id(1)

        @pl.when(k == 0)
        def _():
            acc[...] = jnp.zeros_like(acc)

        acc[...] += _tn(a_ref[...], b_ref[...])

        @pl.when(k == pl.num_programs(1) - 1)
        def _():
            out_ref[...] = (acc[...].T if transpose_out else acc[...]).astype(BF16)

    out_block = pl.BlockSpec((nb, D), lambda n, k: (n, 0)) if transpose_out else pl.BlockSpec((D, nb), lambda n, k: (0, n))
    return pl.pallas_call(
        body, name=name, grid=(N // nb, S // tk),
        out_shape=jax.ShapeDtypeStruct((N, D) if transpose_out else (D, N), BF16),
        in_specs=[pl.BlockSpec((tk, D), lambda n, k: (k, 0)), pl.BlockSpec((tk, nb), lambda n, k: (k, n))],
        out_specs=out_block, scratch_shapes=[pltpu.VMEM((D, nb), F32)],
        compiler_params=_params(("parallel", "arbitrary"), VMEM_LIMIT),
    )(a, b)


def _adamw(w, g, m, v):
    m = ADAM_B1 * m + (1.0 - ADAM_B1) * g
    v = ADAM_B2 * v + (1.0 - ADAM_B2) * (g * g)
    m_hat = m / (1.0 - ADAM_B1 ** ADAM_STEP)
    v_hat = v / (1.0 - ADAM_B2 ** ADAM_STEP)
    delta = -ADAM_LR * (m_hat / (jnp.sqrt(v_hat) + ADAM_EPS) + ADAM_WD * w)
    return delta, m, v


def _adam_shard(parts, w, m, v, name):
    R = w.shape[0]
    tr = max(t for t in (16, 32, 64, 128, 192, 256) if R % t == 0)

    def body(p_ref, w_ref, m_ref, v_ref, g_out, d_out, m_out, v_out):
        g = p_ref[0].astype(F32)
        for j in range(1, N_DEV):
            g = g + p_ref[j].astype(F32)
        delta, mn, vn = _adamw(w_ref[...], g, m_ref[...], v_ref[...])
        g_out[...] = g
        d_out[...] = delta
        m_out[...] = mn
        v_out[...] = vn

    row = pl.BlockSpec((tr, D), lambda i: (i, 0))
    shp = jax.ShapeDtypeStruct((R, D), F32)
    return pl.pallas_call(
        body, name=name, grid=(R // tr,), out_shape=(shp,) * 4,
        in_specs=[pl.BlockSpec((N_DEV, tr, D), lambda i: (0, i, 0)), row, row, row], out_specs=(row,) * 4,
        compiler_params=_params(("parallel",), VMEM_LIMIT),
    )(parts, w, m, v)


def _adam_ada_w(cond_t, dmod, w, m, v):
    ncol = w.shape[-1]
    tr = 512

    def body(c_ref, d_ref, w_ref, m_ref, v_ref, g_out, d_out, m_out, v_out):
        g = _nn(c_ref[...], d_ref[0])
        delta, mn, vn = _adamw(w_ref[0], g, m_ref[0], v_ref[0])
        g_out[0] = g
        d_out[0] = delta
        m_out[0] = mn
        v_out[0] = vn

    blk = pl.BlockSpec((1, tr, ncol), lambda l, i: (l, i, 0))
    shp = jax.ShapeDtypeStruct(w.shape, F32)
    return pl.pallas_call(
        body, name="adam_ada_w", grid=(DEPTH, D // tr), out_shape=(shp,) * 4,
        in_specs=[pl.BlockSpec((tr, LANES), lambda l, i: (i, 0)), pl.BlockSpec((1, LANES, ncol), lambda l, i: (l, 0, 0)), blk, blk, blk],
        out_specs=(blk,) * 4, compiler_params=_params(("parallel", "parallel"), VMEM_LIMIT),
    )(cond_t, dmod, w, m, v)


TILE_ROWS = 168


def _stat_sources():
    pairs = []
    for i in range(DEPTH):
        b = 32 * i
        for q, src in enumerate((b, b + 1, b + 8, b + 16, b + 17, b + 24)):
            pairs.append((6 * i + q, src))
        pairs.append((24 + i, b + 2))
        pairs.append((32 + i, b + 18))
    pairs += [(40, 128), (41, 129)]
    return pairs


def _small_exchange(tiles, w, m, v):
    loss_row, sink_row, sink_src = 41, 48, 136

    def body(s_ref, w_ref, m_ref, v_ref, dmod_out, g_out, d_out, m_out, v_out, loss_out, all_ref, tot_ref, send_sems, recv_sems):
        me = _my_index()
        all_ref[me] = s_ref[...]
        copies = []
        for k in range(1, N_DEV):
            dev, _ = _peer(k)
            cp = pltpu.make_async_remote_copy(src_ref=s_ref, dst_ref=all_ref.at[me], send_sem=send_sems.at[k - 1],
                                              recv_sem=recv_sems.at[k - 1], device_id=dev, device_id_type=MESH)
            cp.start()
            copies.append(cp)
        for k in range(1, N_DEV):
            dev, pidx = _peer(k)
            pltpu.make_async_remote_copy(src_ref=s_ref, dst_ref=all_ref.at[pidx], send_sem=send_sems.at[k - 1],
                                         recv_sem=recv_sems.at[k - 1], device_id=dev, device_id_type=MESH).wait_recv()
        for cp in copies:
            cp.wait_send()
        tot = all_ref[0]
        for j in range(1, N_DEV):
            tot = tot + all_ref[j]
        tot_ref[...] = tot
        g_out[...] = jnp.zeros_like(g_out)
        for dst, src in _stat_sources():
            g_out[pl.ds(dst, 1), :] = tot_ref[pl.ds(src, 1), :]
            if dst < 24:
                for j in range(N_DEV):
                    dmod_out[j, pl.ds(dst, 1), :] = all_ref[j, pl.ds(src, 1), :]
        lane = lax.broadcasted_iota(jnp.int32, (1, D), 1)
        sink = jnp.zeros((1, D), F32)
        for h in range(32):
            sink = jnp.where(lane == h, tot_ref[pl.ds(sink_src + h, 1), :], sink)
        g_out[pl.ds(sink_row, 1), :] = sink
        g = g_out[...]
        delta, mn, vn = _adamw(w_ref[...], g, m_ref[...], v_ref[...])
        d_out[...] = delta
        m_out[...] = mn
        v_out[...] = vn
        loss = jnp.sum(g[loss_row:loss_row + 1, :], axis=-1, keepdims=True) * (0.5 / D)
        loss_out[...] = jnp.broadcast_to(loss, loss_out.shape)

    vm = pl.BlockSpec(memory_space=pltpu.VMEM)
    shp = jax.ShapeDtypeStruct((STAT_ROWS, D), F32)
    return pl.pallas_call(
        body, name="small_exchange",
        out_shape=(jax.ShapeDtypeStruct((N_DEV, 24, D), F32), shp, shp, shp, shp, jax.ShapeDtypeStruct((8, LANES), F32)),
        in_specs=[vm] * 4, out_specs=(vm,) * 6,
        scratch_shapes=[pltpu.VMEM((N_DEV, TILE_ROWS, D), F32), pltpu.VMEM((TILE_ROWS, D), F32),
                        pltpu.SemaphoreType.DMA((N_DEV - 1,)), pltpu.SemaphoreType.DMA((N_DEV - 1,))],
        compiler_params=_params(vmem=VMEM_LIMIT),
    )(tiles, w, m, v)


def _to_rows(name, a):
    if name in ("ffn_in", "a_in", "b_in"):
        return a.T
    if name == "b_out":
        return a.T.reshape(-1, D)
    return a


def _from_rows(name, a):
    if name in ("ffn_in", "a_in", "b_in"):
        return a.T
    if name == "b_out":
        return a.reshape(-1, 512).T
    return a


def _rows8(a):
    return jnp.pad(a, ((0, 8 - a.shape[0]), (0, 0)))


def _pack_small(ada_b, norm_mix, norm_ffn, final_norm, sink):
    sink_row = jnp.pad(sink.reshape(1, -1), ((0, 0), (0, D - sink.size)))
    return jnp.concatenate([ada_b.reshape(24, D), _rows8(norm_mix), _rows8(norm_ffn), _rows8(final_norm.reshape(1, D)),
                            _rows8(sink_row)], axis=0)


def _unpack_small(a):
    return a[0:24].reshape(4, 6 * D), a[24:28], a[32:36], a[40], a[48, :32].reshape(2, 16)


def kernel(x, c, ada_w, ada_b, norm_mix, norm_ffn, ffn_w_in, ffn_w_out, a_w_in, a_w_out, a_sink, b_w_in, b_w_out, final_norm, loss_target, m_ada_w, m_ada_b, m_norm_mix, m_norm_ffn, m_ffn_w_in, m_ffn_w_out, m_a_w_in, m_a_w_out, m_a_sink, m_b_w_in, m_b_w_out, m_final_norm, v_ada_w, v_ada_b, v_norm_mix, v_norm_ffn, v_ffn_w_in, v_ffn_w_out, v_a_w_in, v_a_w_out, v_a_sink, v_b_w_in, v_b_w_out, v_final_norm):
    S = x.shape[1]
    x0 = x.reshape(S, D)
    target = loss_target.reshape(S, D)
    me = _my_index()
    ncol = ada_w.shape[-1]

    ada_b_mine = lax.dynamic_slice_in_dim(ada_b, me * ncol, ncol, axis=1)
    cond_all, parts = _cond_exchange(jnp.broadcast_to(c.reshape(1, D), (8, D)), ada_w, ada_b_mine)
    mod = lax.dynamic_index_in_dim(parts, me, axis=2, keepdims=False)
    mod = jnp.transpose(mod, (1, 0, 2)).reshape(DEPTH, 6, 1, D)

    weights = {"ffn_in": ffn_w_in, "ffn_out": ffn_w_out, "a_in": a_w_in, "a_out": a_w_out, "b_in": b_w_in, "b_out": b_w_out}
    shard = {(n, l): _to_rows(n, weights[n][l]).astype(BF16) for n, l, _ in SEGMENTS}
    first = _layer_segments(0)
    W = {(n, l): g for (n, l, _), g in zip(first, _all_gather_weights([shard[(n, l)] for n, l, _ in first]))}
    later = [sg for i in range(1, DEPTH) for sg in _layer_segments(i)]
    later_off, later_rows = _offsets(later)
    mine = jnp.concatenate([shard[(n, l)] for n, l, _ in later], axis=0)
    zone = lax.dynamic_update_slice(lax.empty((N_DEV, later_rows, D), BF16), mine[None], (me, 0, 0))
    gather = _exchange_start([mine], zone, [later_rows], [0], False, "weight_gather_start")
    gather_token = gather[-1][0:1, 0:1]

    a_slopes, b_slopes = _slopes(16), _slopes(24)
    bias_a = _alibi_bias(a_slopes, A_HALF, 1)
    bias_b = [_alibi_bias(b_slopes[8 * g:8 * g + 8], B_HALF, dil) for g, dil in enumerate(B_DILS)]
    bias_b_fwd = [_alibi_bias(b_slopes[8 * g:8 * g + 8], B_HALF, dil, max(CHUNK, TQ * dil), both=True) for g, dil in enumerate(B_DILS)]
    a_geom = dict(C=A_QKV, r=1, half=A_HALF, qoff=0, koff=1024, voff=1280, n_units=2)
    b_geom = [dict(C=B_QKV, r=dil, half=B_HALF, qoff=512 * g, koff=1536 + 128 * g, voff=1920 + 128 * g, n_units=1)
              for g, dil in enumerate(B_DILS)]

    saved = []
    xcur = x0
    for i in range(DEPTH):
        j = i // 2
        sh1, sc1, g1, sh2, sc2, g2 = [mod[i, q] for q in range(6)]
        nm, nf = norm_mix[i].reshape(1, D), norm_ffn[i].reshape(1, D)
        if i == 0:
            nm = nm + gather_token
        if i == 1:
            zone = _exchange_wait(gather, xcur, "weight_gather_wait")
            for (n, l, rows), off in zip(later, later_off):
                W[(n, l)] = zone[:, off:off + rows].reshape(N_DEV * rows, D)
            for j2 in range(2):
                W[("b_out", j2)] = W[("b_out", j2)].reshape(D, 512)
        if i % 2 == 0:
            sink_rep = jnp.repeat(jnp.repeat(a_sink[j], TQ).reshape(2, 1, 8 * TQ), 8, axis=1).reshape(16, 8 * TQ)
            h1, qkv = _proj(xcur, nm, sc1, sh1, W[("a_in", j)], ffn=False, name="proj_a")
            o, lse = _attn_fwd(qkv, bias_a, sink_rep, out_dtype=BF16, name="attn_a_fwd", **a_geom)
            x1, y1 = _gated_residual(o, W[("a_out", j)], xcur, g1, w_is_transposed=False, name="out_a")
        else:
            sink_rep = None
            h1, qkv = _proj(xcur, nm, sc1, sh1, W[("b_in", j)], ffn=False, name="proj_b")
            outs = [_attn_fwd(qkv, bias_b_fwd[g], None, out_dtype=F32, name="attn_b%d_fwd" % g, **b_geom[g]) for g in range(3)]
            o, lse = _merge_groups([t[0] for t in outs], [t[1] for t in outs])
            x1, y1 = _gated_residual(o, W[("b_out", j)], xcur, g1, w_is_transposed=True, name="out_b")
        h2, gu, act = _proj(x1, nf, sc2, sh2, W[("ffn_in", i)], ffn=True, name="ffn_in")
        x2, y2 = _gated_residual(act, W[("ffn_out", i)], x1, g2, w_is_transposed=False, name="ffn_out")
        saved.append(dict(x0=xcur, h1=h1, qkv=qkv, o=o, lse=lse, y1=y1, x1=x1, h2=h2, gu=gu, act=act, y2=y2, sink=sink_rep))
        xcur = x2

    dx, head_stats = _loss_head(xcur, target, final_norm.reshape(1, D))

    dW = {}
    stat_tiles, dsink = [None] * DEPTH, [None] * 2
    exchanges = [None] * DEPTH
    start_token = None
    for i in reversed(range(DEPTH)):
        j = i // 2
        sv = saved[i]
        sh1, sc1, g1, sh2, sc2, g2 = [mod[i, q] for q in range(6)]
        if start_token is not None:
            g2 = g2 + start_token
        nm, nf = norm_mix[i].reshape(1, D), norm_ffn[i].reshape(1, D)
        dgu, dy2, st_g2 = _gate_bwd(dx, sv["y2"], g2, W[("ffn_out", i)], sv["gu"], w_is_transposed=False, name="ffn_out_bwd")
        dW[("ffn_out", i)] = _weight_grad(dy2, sv["act"], transpose_out=True, name="dw_ffn_out")
        dW[("ffn_in", i)] = _weight_grad(sv["h2"], dgu, transpose_out=True, name="dw_ffn_in")
        dx1, st_f = _norm_bwd(dgu, W[("ffn_in", i)], sv["x1"], dx, nf, sc2, name="ffn_in_bwd")
        if i % 2 == 0:
            do, dy1, st_g1 = _gate_bwd(dx1, sv["y1"], g1, W[("a_out", j)], None, w_is_transposed=False, name="out_a_bwd")
            dW[("a_out", j)] = _weight_grad(dy1, sv["o"], transpose_out=True, name="dw_a_out")
            dq, dk, dv, ds = _attn_bwd(sv["qkv"], bias_a, sv["sink"], sv["o"], do, sv["lse"], name="attn_a_bwd", **a_geom)
            dsink[j] = ds
            dqkv = jnp.concatenate([dq, dk[0].astype(BF16), dk[1].astype(BF16), dv[0].astype(BF16), dv[1].astype(BF16)], axis=1)
            dW[("a_in", j)] = _weight_grad(sv["h1"], dqkv, transpose_out=True, name="dw_a_in")
            dx0, st_m = _norm_bwd(dqkv, W[("a_in", j)], sv["x0"], dx1, nm, sc1, name="proj_a_bwd")
        else:
            do, dy1, st_g1 = _gate_bwd(dx1, sv["y1"], g1, W[("b_out", j)], None, w_is_transposed=True, name="out_b_bwd")
            dW[("b_out", j)] = _weight_grad(dy1, sv["o"], transpose_out=False, name="dw_b_out").reshape(N_DEV * 64, D)
            gr = [_attn_bwd(sv["qkv"], bias_b[g], None, sv["o"], do, sv["lse"], name="attn_b%d_bwd" % g, **b_geom[g]) for g in range(3)]
            dqkv = jnp.concatenate([t[0] for t in gr] + [t[1][0].astype(BF16) for t in gr] + [t[2][0].astype(BF16) for t in gr], axis=1)
            dW[("b_in", j)] = _weight_grad(sv["h1"], dqkv, transpose_out=True, name="dw_b_in")
            dx0, st_m = _norm_bwd(dqkv, W[("b_in", j)], sv["x0"], dx1, nm, sc1, name="proj_b_bwd")
        stat_tiles[i] = [st_m, st_g1, st_f, st_g2]
        segs = _layer_segments(i)
        offs, total = _offsets(segs)
        own = jnp.concatenate([lax.dynamic_slice_in_dim(dW[(n, l)], me * rows, rows, axis=0) for n, l, rows in segs], axis=0)
        zone = lax.dynamic_update_slice(lax.empty((N_DEV, total, D), BF16), own[None], (me, 0, 0))
        exchanges[i] = _exchange_start([dW[(n, l)] for n, l, _ in segs], zone, [sg[2] for sg in segs], offs, True,
                                       "grad_exchange_start_%d" % i)
        start_token = exchanges[i][-1][0:1, 0:1]
        dx = dx0
    grad_x = dx.reshape(1, S, D)

    masters = {"ffn_in": (ffn_w_in, m_ffn_w_in, v_ffn_w_in), "ffn_out": (ffn_w_out, m_ffn_w_out, v_ffn_w_out),
               "a_in": (a_w_in, m_a_w_in, v_a_w_in), "a_out": (a_w_out, m_a_w_out, v_a_w_out),
               "b_in": (b_w_in, m_b_w_in, v_b_w_in), "b_out": (b_w_out, m_b_w_out, v_b_w_out)}
    pieces = {}
    after = dx
    for i in reversed(range(DEPTH)):
        segs = _layer_segments(i)
        offs, total = _offsets(segs)
        parts_g = _exchange_wait(exchanges[i], after, "grad_exchange_wait_%d" % i)
        rows_wmv = [jnp.concatenate([_to_rows(n, masters[n][q][l]) for n, l, _ in segs], axis=0) for q in range(3)]
        res_rows = _adam_shard(parts_g, *rows_wmv, name="adam_a" if i % 2 == 0 else "adam_b")
        after = res_rows[0]
        for q, kind in enumerate(("grad", "delta", "m", "v")):
            for (n, l, rows), off in zip(segs, offs):
                pieces[(kind, n, l)] = _from_rows(n, res_rows[q][off:off + rows])
    big = {(kind, n): jnp.stack([pieces[(kind, n, l)] for l in range(4 if n.startswith("ffn") else 2)])
           for kind in ("grad", "delta", "m", "v") for n in masters}

    tiles = jnp.concatenate([t for i in range(DEPTH) for t in stat_tiles[i]] + [head_stats]
                            + [jnp.pad(ds, ((0, 0), (0, D - LANES))) for ds in dsink], axis=0)
    small = [_pack_small(*t) for t in ((ada_b, norm_mix, norm_ffn, final_norm, a_sink),
                                       (m_ada_b, m_norm_mix, m_norm_ffn, m_final_norm, m_a_sink),
                                       (v_ada_b, v_norm_mix, v_norm_ffn, v_final_norm, v_a_sink))]
    dmod_all, sg, sd, sm, sv_, loss_tile = _small_exchange(tiles, *small)
    loss = loss_tile[0, 0]
    dmod_all = dmod_all.reshape(N_DEV, DEPTH, 6 * D)
    dmod_mine = lax.dynamic_slice_in_dim(dmod_all, me * ncol, ncol, axis=2)
    dmod_pad = jnp.pad(jnp.transpose(dmod_mine, (1, 0, 2)), ((0, 0), (0, LANES - N_DEV), (0, 0))).astype(BF16)
    cond_t = jnp.pad(cond_all.T, ((0, 0), (0, LANES - N_DEV))).astype(BF16)
    ada = _adam_ada_w(cond_t, dmod_pad, ada_w, m_ada_w, v_ada_w)

    outs = [loss, grad_x]
    small_res = [_unpack_small(t) for t in (sg, sd, sm, sv_)]
    for q, kind in enumerate(("grad", "delta", "m", "v")):
        ab, nm_, nf_, fn, sk = small_res[q]
        outs += [ada[q], ab, nm_, nf_, big[(kind, "ffn_in")], big[(kind, "ffn_out")], big[(kind, "a_in")], big[(kind, "a_out")],
                 sk, big[(kind, "b_in")], big[(kind, "b_out")], fn]
    return tuple(outs)
```

```python
import functools
import math

import numpy as np
import jax
import jax.numpy as jnp
from jax import lax
from jax.experimental import pallas as pl
from jax.experimental.pallas import tpu as pltpu

D = 1024
HEAD_DIM = 64
D_FF = 2816
DEPTH = 4
N_DEV = 8
A_QKV = 1536
B_QKV = 2304
A_HALF = 128
B_HALF = 64
B_DILS = (1, 4, 16)
RMS_EPS = 1e-6
NEG = -1e30
ADAM_LR = 0.001
ADAM_B1 = 0.9
ADAM_B2 = 0.999
ADAM_EPS = 1e-08
ADAM_WD = 0.01
ADAM_STEP = 10

LANES = 128
TQ = 128
VMEM_LIMIT = 56 * 1024 * 1024
MESH = pl.DeviceIdType.MESH
F32 = jnp.float32
BF16 = jnp.bfloat16

SEGMENTS = ([("ffn_in", l, 704) for l in range(4)] + [("ffn_out", l, 352) for l in range(4)]
            + [("a_in", j, 192) for j in range(2)] + [("a_out", j, 128) for j in range(2)]
            + [("b_in", j, 288) for j in range(2)] + [("b_out", j, 64) for j in range(2)])
def _layer_segments(i):
    mixer = "a" if i % 2 == 0 else "b"
    return [s for s in SEGMENTS if (s[0].startswith("ffn") and s[1] == i) or (s[0].startswith(mixer + "_") and s[1] == i // 2)]


def _offsets(segs):
    rows = [s[2] for s in segs]
    return [sum(rows[:k]) for k in range(len(rows))], sum(rows)
STAT_ROWS = 56


def _nn(a, b):
    return jnp.dot(a, b, preferred_element_type=F32)


def _nt(a, b):
    return lax.dot_general(a, b, (((1,), (1,)), ((), ())), preferred_element_type=F32)


def _tn(a, b):
    return lax.dot_general(a, b, (((0,), (0,)), ((), ())), preferred_element_type=F32)


def _params(dims=None, vmem=None):
    kw = {}
    if dims is not None:
        kw["dimension_semantics"] = dims
    if vmem is not None:
        kw["vmem_limit_bytes"] = vmem
    return pltpu.CompilerParams(**kw)


def _my_index():
    return 4 * lax.axis_index("x") + 2 * lax.axis_index("y") + lax.axis_index("c")


def _peer(k):
    x, y, c = lax.axis_index("x"), lax.axis_index("y"), lax.axis_index("c")
    px, py, pc = x ^ ((k >> 2) & 1), y ^ ((k >> 1) & 1), c ^ (k & 1)
    return (px, py, pc), 4 * px + 2 * py + pc


def _const_spec(shape):
    nd = len(shape)
    return pl.BlockSpec(shape, lambda *_: (0,) * nd)


def _cond_exchange(c_tile, ada_w, ada_b_mine):
    ncol = ada_w.shape[-1]

    def body(c_ref, w_ref, b_ref, cond_ref, parts_ref, call_ref, mine_ref, send_sems, recv_sems):
        me = _my_index()
        call_ref[me] = c_ref[...]
        copies = []
        for k in range(1, N_DEV):
            dev, _ = _peer(k)
            cp = pltpu.make_async_remote_copy(src_ref=c_ref, dst_ref=call_ref.at[me], send_sem=send_sems.at[0, k - 1],
                                              recv_sem=recv_sems.at[0, k - 1], device_id=dev, device_id_type=MESH)
            cp.start()
            copies.append(cp)
        for k in range(1, N_DEV):
            _, pidx = _peer(k)
            pltpu.make_async_remote_copy(src_ref=c_ref, dst_ref=call_ref.at[pidx], send_sem=send_sems.at[0, k - 1],
                                         recv_sem=recv_sems.at[0, k - 1], device_id=_peer(k)[0], device_id_type=MESH).wait_recv()
        for cp in copies:
            cp.wait_send()
        row = lax.broadcasted_iota(jnp.int32, (N_DEV, D), 0)
        cmat = jnp.zeros((N_DEV, D), F32)
        for j in range(N_DEV):
            cmat = jnp.where(row == j, call_ref[j], cmat)
        cond = cmat * jax.nn.sigmoid(cmat)
        cond_ref[...] = cond
        cb = cond.astype(BF16)
        for l in range(DEPTH):
            mine_ref[l] = _nn(cb, w_ref[l].astype(BF16)) + b_ref[pl.ds(l, 1), :]
        parts_ref[me] = mine_ref[...]
        copies = []
        for k in range(1, N_DEV):
            dev, _ = _peer(k)
            cp = pltpu.make_async_remote_copy(src_ref=mine_ref, dst_ref=parts_ref.at[me], send_sem=send_sems.at[1, k - 1],
                                              recv_sem=recv_sems.at[1, k - 1], device_id=dev, device_id_type=MESH)
            cp.start()
            copies.append(cp)
        for k in range(1, N_DEV):
            dev, pidx = _peer(k)
            pltpu.make_async_remote_copy(src_ref=mine_ref, dst_ref=parts_ref.at[pidx], send_sem=send_sems.at[1, k - 1],
                                         recv_sem=recv_sems.at[1, k - 1], device_id=dev, device_id_type=MESH).wait_recv()
        for cp in copies:
            cp.wait_send()

    vm = pl.BlockSpec(memory_space=pltpu.VMEM)
    return pl.pallas_call(
        body, name="cond_exchange",
        out_shape=(jax.ShapeDtypeStruct((N_DEV, D), F32), jax.ShapeDtypeStruct((N_DEV, DEPTH, N_DEV, ncol), F32)),
        in_specs=[vm, vm, vm], out_specs=(vm, vm),
        scratch_shapes=[pltpu.VMEM((N_DEV, N_DEV, D), F32), pltpu.VMEM((DEPTH, N_DEV, ncol), F32),
                        pltpu.SemaphoreType.DMA((2, N_DEV - 1)), pltpu.SemaphoreType.DMA((2, N_DEV - 1))],
        compiler_params=_params(vmem=VMEM_LIMIT),
    )(c_tile, ada_w, ada_b_mine)[:2]


def _all_gather_weights(shards):
    n = len(shards)
    big = max(range(n), key=lambda s: shards[s].shape[0])
    total = sum(sh.shape[0] for sh in shards)
    assert N_DEV * shards[big].shape[0] >= total

    def body(*refs):
        ins, outs = refs[:n], refs[n:2 * n]
        local_sems, send_sems, recv_sems = refs[2 * n:]
        me = _my_index()
        local = []
        for s in range(n):
            rows = ins[s].shape[0]
            cp = pltpu.make_async_copy(ins[s], outs[s].at[pl.ds(me * rows, rows)], local_sems.at[s])
            cp.start()
            local.append(cp)
        for k in range(1, N_DEV):
            dev, _ = _peer(k)
            for s in range(n):
                rows = ins[s].shape[0]
                pltpu.make_async_remote_copy(src_ref=ins[s], dst_ref=outs[s].at[pl.ds(me * rows, rows)],
                                             send_sem=send_sems.at[k - 1], recv_sem=recv_sems.at[k - 1],
                                             device_id=dev, device_id_type=MESH).start()
        whole = outs[big].at[pl.ds(0, total)]
        for k in range(1, N_DEV):
            dev, _ = _peer(k)
            w = pltpu.make_async_remote_copy(src_ref=whole, dst_ref=whole, send_sem=send_sems.at[k - 1],
                                             recv_sem=recv_sems.at[k - 1], device_id=dev, device_id_type=MESH)
            w.wait_send()
            w.wait_recv()
        for cp in local:
            cp.wait()

    hbm = pl.BlockSpec(memory_space=pl.ANY)
    return pl.pallas_call(
        body, name="weight_all_gather",
        out_shape=tuple(jax.ShapeDtypeStruct((N_DEV * s.shape[0], D), s.dtype) for s in shards),
        in_specs=[hbm] * n, out_specs=tuple([hbm] * n),
        scratch_shapes=[pltpu.SemaphoreType.DMA((n,)), pltpu.SemaphoreType.DMA((N_DEV - 1,)),
                        pltpu.SemaphoreType.DMA((N_DEV - 1,))],
    )(*shards)


HBM = pl.BlockSpec(memory_space=pltpu.HBM)
SEM = pl.BlockSpec(memory_space=pltpu.SEMAPHORE)
EFFECT = pltpu.SideEffectType.DATAFLOW_SIDE_EFFECTING


def _exchange_start(srcs, landing, rows, offs, to_peer_rows, name):
    n = len(srcs)

    def body(*refs):
        src_refs, land_ref = refs[:n], refs[n]
        send_sems, recv_sems = refs[n + 1], refs[n + 2]
        token = refs[-1]
        me = _my_index()
        for k in range(1, N_DEV):
            dev, pidx = _peer(k)
            for q in range(n):
                src = src_refs[q].at[pl.ds(pidx * rows[q], rows[q])] if to_peer_rows else src_refs[q]
                pltpu.make_async_remote_copy(src_ref=src, dst_ref=land_ref.at[me, pl.ds(offs[q], rows[q])],
                                             send_sem=send_sems.at[k - 1], recv_sem=recv_sems.at[k - 1],
                                             device_id=dev, device_id_type=MESH).start()
        token[...] = jnp.zeros_like(token)

    arrays = list(srcs) + [landing]
    return pl.pallas_call(
        body, name=name,
        out_shape=(pltpu.SemaphoreType.DMA((N_DEV - 1,)), pltpu.SemaphoreType.DMA((N_DEV - 1,)),
                   *[pltpu.HBM(a.shape, a.dtype) for a in arrays], jax.ShapeDtypeStruct((8, LANES), F32)),
        in_specs=[HBM] * (n + 1), out_specs=(SEM, SEM, *[HBM] * (n + 1), pl.BlockSpec(memory_space=pltpu.VMEM)),
        input_output_aliases={q: 2 + q for q in range(n + 1)},
        compiler_params=pltpu.CompilerParams(has_side_effects=EFFECT),
    )(*[pltpu.with_memory_space_constraint(a, pltpu.HBM) for a in arrays])


def _exchange_wait(started, after, name):
    send_sems, recv_sems = started[0], started[1]
    arrays = list(started[2:-1])
    n1 = len(arrays)

    def body(*refs):
        land_ref = refs[n1 - 1]
        sends, recvs = refs[n1], refs[n1 + 1]
        for k in range(1, N_DEV):
            dev, _ = _peer(k)
            w = pltpu.make_async_remote_copy(src_ref=land_ref.at[0], dst_ref=land_ref.at[0], send_sem=sends.at[k - 1],
                                             recv_sem=recvs.at[k - 1], device_id=dev, device_id_type=MESH)
            w.wait_send()
            w.wait_recv()

    return pl.pallas_call(
        body, name=name, out_shape=tuple(pltpu.HBM(a.shape, a.dtype) for a in arrays),
        in_specs=[HBM] * n1 + [SEM, SEM, pl.BlockSpec(memory_space=pl.ANY)], out_specs=tuple([HBM] * n1),
        input_output_aliases={q: q for q in range(n1)},
        compiler_params=pltpu.CompilerParams(has_side_effects=EFFECT),
    )(*arrays, send_sems, recv_sems, after)[n1 - 1]


def _norm_mod(x, nw, sc, sh):
    ms = jnp.mean(x * x, axis=-1, keepdims=True)
    xh = x * lax.rsqrt(ms + RMS_EPS)
    return xh, (xh * nw) * (1.0 + sc) + sh


def _proj(x, nw, sc, sh, wt, *, ffn, name):
    S, N = x.shape[0], wt.shape[0]
    tm = 256 if ffn else 512

    def body(x_ref, nw_ref, sc_ref, sh_ref, w_ref, h_ref, out_ref, *act_ref):
        _, h = _norm_mod(x_ref[...], nw_ref[...], sc_ref[...], sh_ref[...])
        hb = h.astype(BF16)
        h_ref[...] = hb
        if ffn:
            gate = _nt(hb, w_ref[pl.ds(0, D_FF), :])
            up = _nt(hb, w_ref[pl.ds(D_FF, D_FF), :])
            out_ref[:, pl.ds(0, D_FF)] = gate.astype(BF16)
            out_ref[:, pl.ds(D_FF, D_FF)] = up.astype(BF16)
            act_ref[0][...] = ((gate * jax.nn.sigmoid(gate)) * up).astype(BF16)
        else:
            out_ref[...] = _nt(hb, w_ref[...]).astype(BF16)

    row = lambda w: pl.BlockSpec((tm, w), lambda i: (i, 0))
    out_shape = [jax.ShapeDtypeStruct((S, D), BF16), jax.ShapeDtypeStruct((S, N), BF16)]
    out_specs = [row(D), row(N)]
    if ffn:
        out_shape.append(jax.ShapeDtypeStruct((S, D_FF), BF16))
        out_specs.append(row(D_FF))
    vec = _const_spec((1, D))
    return pl.pallas_call(
        body, name=name, grid=(S // tm,), out_shape=tuple(out_shape),
        in_specs=[row(D), vec, vec, vec, _const_spec((N, D))], out_specs=tuple(out_specs),
        compiler_params=_params(("parallel",), VMEM_LIMIT),
    )(x, nw, sc, sh, wt)


def _gated_residual(a, w, x, g, *, w_is_transposed, name):
    S, K = a.shape
    tm = 512

    def body(a_ref, w_ref, x_ref, g_ref, xo_ref, y_ref):
        y = _nt(a_ref[...], w_ref[...]) if w_is_transposed else _nn(a_ref[...], w_ref[...])
        y_ref[...] = y.astype(BF16)
        xo_ref[...] = x_ref[...] + g_ref[...] * y

    row = lambda w_: pl.BlockSpec((tm, w_), lambda i: (i, 0))
    return pl.pallas_call(
        body, name=name, grid=(S // tm,),
        out_shape=(jax.ShapeDtypeStruct((S, D), F32), jax.ShapeDtypeStruct((S, D), BF16)),
        in_specs=[row(K), _const_spec(w.shape), row(D), _const_spec((1, D))], out_specs=(row(D), row(D)),
        compiler_params=_params(("parallel",), VMEM_LIMIT),
    )(a, w, x, g)


CHUNK = 1024


def _tile_rows(r, chunk=CHUNK):
    return min(TQ, chunk // r)


def _alibi_bias(slopes, half, dil, chunk=CHUNK, both=False):
    tq = _tile_rows(dil, chunk)
    tk = tq + 2 * half
    rel = np.arange(tk)[:, None] - half - np.arange(tq)[None, :]
    band = np.abs(rel) <= half
    dist = (dil * np.abs(rel)).astype(np.float32)
    tabs = [np.where(band, -np.float32(s) * dist, np.float32(NEG)).astype(np.float32) for s in slopes]
    out = []
    for u in range(0, len(tabs), 8):
        tab = np.concatenate(tabs[u:u + 8], axis=1)
        first, last = tab.copy(), tab.copy()
        first[:half] = NEG
        last[tk - half:] = NEG
        out += [tab, first, last]
        if both:
            last = last.copy()
            last[:half] = NEG
            out.append(last)
    return jnp.asarray(np.concatenate(out, axis=0))


def _slopes(n):
    return (2.0 ** (-8.0 * np.arange(1, n + 1) / n)).astype(np.float32)


def _head_masks(tq):
    lane = lax.broadcasted_iota(jnp.int32, (tq, LANES), 1)
    lo = lane < HEAD_DIM
    return lo, jnp.logical_not(lo)


def _stack_heads(tiles, lo, hi, scale):
    blocks = []
    for t in range(4):
        xf = tiles[t] if scale == 1.0 else tiles[t] * scale
        for a in range(2):
            xm = jnp.where(lo if a == 0 else hi, xf, 0.0)
            if a != t // 2:
                xm = pltpu.roll(xm, HEAD_DIM, 1)
            blocks.append(xm.astype(BF16))
    return jnp.concatenate(blocks, axis=0)


def _tile_from_columns(x8t, t, tq):
    r0 = HEAD_DIM * (t // 2)
    top = x8t[r0:r0 + HEAD_DIM, 2 * t * tq:(2 * t + 1) * tq]
    bot = x8t[r0:r0 + HEAD_DIM, (2 * t + 1) * tq:(2 * t + 2) * tq]
    return jnp.concatenate([top, bot], axis=0).T


def _attn_layout(S, C, r, half, qoff, koff, voff, chunk):
    hb = half * r
    per = chunk // hb
    nhb = S // hb
    main = lambda off: pl.BlockSpec((chunk, LANES), lambda u, i: (i, off // LANES + u))
    prev = lambda off: pl.BlockSpec((hb, LANES), lambda u, i: (jnp.maximum(i * per - 1, 0), off // LANES + u))
    nxt = lambda off: pl.BlockSpec((hb, LANES), lambda u, i: (jnp.minimum((i + 1) * per, nhb - 1), off // LANES + u))
    specs = [pl.BlockSpec((chunk, 4 * LANES), lambda u, i: (i, qoff // (4 * LANES) + u))]
    specs += [prev(koff), main(koff), nxt(koff), prev(voff), main(voff), nxt(voff)]
    return specs, hb


def _stage(dst, srcs):
    row = 0
    for src in srcs:
        n = src.shape[0]
        dst[pl.ds(row, n), :] = src[...].astype(F32)
        row += n


def _rows(start, n, r):
    return pl.ds(start, n, stride=r) if r > 1 else pl.ds(start, n)


def _attn_fwd(qkv, bias, sink, *, C, r, half, qoff, koff, voff, n_units, out_dtype, name):
    S = qkv.shape[0]
    chunk = max(CHUNK, TQ * r)
    tq = _tile_rows(r, chunk)
    tk = tq + 2 * half
    tiles = chunk // (r * tq)
    nsteps = S // chunk
    specs, hb = _attn_layout(S, C, r, half, qoff, koff, voff, chunk)
    use_sink = sink is not None

    def body(*refs):
        q_ref, kp, km, kn, vp, vm, vn, bias_ref = refs[:8]
        rest = list(refs[8:])
        sink_ref = rest.pop(0) if use_sink else None
        o_ref, lse_ref, qs, ks, vs, os_, ls = rest
        i = pl.program_id(1)
        for t in range(4):
            qs[t] = q_ref[:, pl.ds(t * LANES, LANES)].astype(F32)
        _stage(ks, [kp, km, kn])
        _stage(vs, [vp, vm, vn])
        lo, hi = _head_masks(tq)
        ones = jnp.ones((16, tk), BF16)
        if use_sink:
            sk = sink_ref[pl.ds(0, 1), :]

        def chain(n, carry):
            rho, c = n // tiles, n % tiles
            start = c * (tq * r) + rho
            if r == 1:
                start = pl.multiple_of(start, tq)
            variant = jnp.where(jnp.logical_and(i == 0, c == 0), 1, 0) + jnp.where(
                jnp.logical_and(i == nsteps - 1, c == tiles - 1), 2, 0)
            k2 = ks[_rows(start, tk, r), :].astype(BF16)
            v2t = jnp.concatenate([vs[_rows(start, tk, r), :].T.astype(BF16), ones], axis=0)
            q8 = _stack_heads([qs[t, _rows(start, tq, r), :] for t in range(4)], lo, hi, HEAD_DIM ** -0.5)
            s = _nt(k2, q8) + bias_ref[pl.ds(pl.multiple_of(variant * tk, 8), tk), :]
            m = jnp.max(s, axis=0, keepdims=True)
            if use_sink:
                m = jnp.maximum(m, sk)
            pv = _nn(v2t, jnp.exp(s - m).astype(BF16))
            l = pv[LANES:LANES + 1]
            if use_sink:
                l = l + jnp.exp(sk - m)
            o8t = pv[:LANES] / l
            lse8 = jnp.broadcast_to(m + jnp.log(l), (LANES, 8 * tq))
            for t in range(4):
                os_[t, _rows(start, tq, r), :] = _tile_from_columns(o8t, t, tq)
                ls[t, _rows(start, tq, r), :] = _tile_from_columns(lse8, t, tq)
            return carry

        lax.fori_loop(0, r * tiles, chain, 0, unroll=2)
        for t in range(4):
            o_ref[:, pl.ds(t * LANES, LANES)] = os_[t].astype(out_dtype)
            lse_ref[:, pl.ds(t * LANES, LANES)] = ls[t]

    in_specs = specs + [pl.BlockSpec((bias.shape[0] // n_units, 8 * tq), lambda u, i: (u, 0))]
    args = [qkv] * 7 + [bias]
    if use_sink:
        in_specs.append(pl.BlockSpec((8, 8 * tq), lambda u, i: (u, 0)))
        args.append(sink)
    wide = pl.BlockSpec((chunk, 4 * LANES), lambda u, i: (i, u))
    win = hb + chunk + hb
    return pl.pallas_call(
        body, name=name, grid=(n_units, nsteps),
        out_shape=(jax.ShapeDtypeStruct((S, n_units * 512), out_dtype), jax.ShapeDtypeStruct((S, n_units * 512), F32)),
        in_specs=in_specs, out_specs=(wide, wide),
        scratch_shapes=[pltpu.VMEM((4, chunk, LANES), F32), pltpu.VMEM((win, LANES), F32), pltpu.VMEM((win, LANES), F32),
                        pltpu.VMEM((4, chunk, LANES), F32), pltpu.VMEM((4, chunk, LANES), F32)],
        compiler_params=_params(("parallel", "parallel"), VMEM_LIMIT),
    )(*args)


def _attn_bwd(qkv, bias, sink, o, do, lse, *, C, r, half, qoff, koff, voff, n_units, name):
    S = qkv.shape[0]
    tq = _tile_rows(r)
    tk = tq + 2 * half
    tiles = CHUNK // (r * tq)
    nsteps = S // CHUNK
    specs, hb = _attn_layout(S, C, r, half, qoff, koff, voff, CHUNK)
    use_sink = sink is not None

    def body(*refs):
        q_ref, kp, km, kn, vp, vm, vn, bias_ref = refs[:8]
        rest = list(refs[8:])
        sink_ref = rest.pop(0) if use_sink else None
        o_ref, do_ref, lse_ref, dq_ref, dk_hbm, dv_hbm = rest[:6]
        rest = rest[6:]
        dsink_ref = rest.pop(0) if use_sink else None
        qs, ks, vs, os_, dos, ls, dqs, acck, accv, sem = rest
        u, i = pl.program_id(0), pl.program_id(1)

        @pl.when(i == 0)
        def _():
            acck[...] = jnp.zeros_like(acck)
            accv[...] = jnp.zeros_like(accv)
            if use_sink:
                dsink_ref[...] = jnp.zeros_like(dsink_ref)

        for t in range(4):
            cols = pl.ds(t * LANES, LANES)
            qs[t] = q_ref[:, cols].astype(F32)
            os_[t] = o_ref[:, cols].astype(F32)
            dos[t] = do_ref[:, cols].astype(F32)
            ls[t] = lse_ref[:, cols]
        _stage(ks, [kp, km, kn])
        _stage(vs, [vp, vm, vn])
        lo, hi = _head_masks(tq)
        base = pl.multiple_of(i * CHUNK, CHUNK)
        if use_sink:
            sk = sink_ref[pl.ds(0, 1), :]

        def chain(n, carry):
            rho, c = n // tiles, n % tiles
            start = c * (tq * r) + rho
            if r == 1:
                start = pl.multiple_of(start, tq)
            variant = jnp.where(jnp.logical_and(i == 0, c == 0), 1, 0) + jnp.where(
                jnp.logical_and(i == nsteps - 1, c == tiles - 1), 2, 0)
            k2 = ks[_rows(start, tk, r), :].astype(BF16)
            v2 = vs[_rows(start, tk, r), :].astype(BF16)
            k2t = ks[_rows(start, tk, r), :].T.astype(BF16)
            q8 = _stack_heads([qs[t, _rows(start, tq, r), :] for t in range(4)], lo, hi, HEAD_DIM ** -0.5)
            do_tiles = [dos[t, _rows(start, tq, r), :] for t in range(4)]
            do8 = _stack_heads(do_tiles, lo, hi, 1.0)
            deltas, lses = [], []
            for t in range(4):
                prod_t = (do_tiles[t] * os_[t, _rows(start, tq, r), :]).T
                lse_t = ls[t, _rows(start, tq, r), :].T
                for a in range(2):
                    deltas.append(jnp.sum(prod_t[a * HEAD_DIM:(a + 1) * HEAD_DIM], axis=0, keepdims=True))
                    lses.append(lse_t[a * HEAD_DIM:a * HEAD_DIM + 1])
            delta8 = jnp.concatenate(deltas, axis=1)
            lse8 = jnp.concatenate(lses, axis=1)
            s = _nt(k2, q8) + bias_ref[pl.ds(pl.multiple_of(variant * tk, 8), tk), :]
            p = jnp.exp(s - lse8)
            dp = _nt(v2, do8)
            dsb = (p * (dp - delta8)).astype(BF16)
            dq8t = _nn(k2t, dsb)
            for t in range(4):
                dqs[t, _rows(start, tq, r), :] = _tile_from_columns(dq8t, t, tq) * (HEAD_DIM ** -0.5)
            arow = base + start
            if r == 1:
                arow = pl.multiple_of(arow, tq)
            acck[_rows(arow, tk, r), :] = acck[_rows(arow, tk, r), :] + _nn(dsb, q8)
            accv[_rows(arow, tk, r), :] = accv[_rows(arow, tk, r), :] + _nn(p.astype(BF16), do8)
            if use_sink:
                e = jnp.exp(sk - lse8) * delta8
                for h in range(8):
                    part = -jnp.sum(e[:, h * tq:(h + 1) * tq], axis=1, keepdims=True)
                    dsink_ref[pl.ds(h, 1), :] = dsink_ref[pl.ds(h, 1), :] + part
            return carry

        lax.fori_loop(0, r * tiles, chain, 0, unroll=2)
        for t in range(4):
            dq_ref[:, pl.ds(t * LANES, LANES)] = dqs[t].astype(BF16)

        @pl.when(i == nsteps - 1)
        def _():
            ck = pltpu.make_async_copy(acck.at[pl.ds(hb, S)], dk_hbm.at[u], sem.at[0])
            cv = pltpu.make_async_copy(accv.at[pl.ds(hb, S)], dv_hbm.at[u], sem.at[1])
            ck.start()
            cv.start()
            ck.wait()
            cv.wait()

    wide = pl.BlockSpec((CHUNK, 4 * LANES), lambda u, i: (i, u))
    hbm = pl.BlockSpec(memory_space=pl.ANY)
    in_specs = specs + [pl.BlockSpec((3 * tk, 8 * tq), lambda u, i: (u, 0))]
    args = [qkv] * 7 + [bias]
    if use_sink:
        in_specs.append(pl.BlockSpec((8, 8 * tq), lambda u, i: (u, 0)))
        args.append(sink)
    in_specs += [wide, wide, wide]
    args += [o, do, lse]
    out_shape = [jax.ShapeDtypeStruct((S, n_units * 512), BF16), jax.ShapeDtypeStruct((n_units, S, LANES), F32),
                 jax.ShapeDtypeStruct((n_units, S, LANES), F32)]
    out_specs = [wide, hbm, hbm]
    if use_sink:
        out_shape.append(jax.ShapeDtypeStruct((n_units * 8, LANES), F32))
        out_specs.append(pl.BlockSpec((8, LANES), lambda u, i: (u, 0)))
    win = hb + CHUNK + hb
    big = lambda: pltpu.VMEM((4, CHUNK, LANES), F32)
    res = pl.pallas_call(
        body, name=name, grid=(n_units, nsteps), out_shape=tuple(out_shape), in_specs=in_specs, out_specs=tuple(out_specs),
        scratch_shapes=[big(), pltpu.VMEM((win, LANES), F32), pltpu.VMEM((win, LANES), F32), big(), big(), big(), big(),
                        pltpu.VMEM((S + 2 * hb, LANES), F32), pltpu.VMEM((S + 2 * hb, LANES), F32), pltpu.SemaphoreType.DMA((2,))],
        compiler_params=_params(("arbitrary", "arbitrary"), VMEM_LIMIT),
    )(*args)
    return res[0], res[1], res[2], (res[3] if use_sink else None)


def _merge_groups(os_, lses):
    S, W = os_[0].shape
    tm = 512

    def body(o0, o1, o2, l0, l1, l2, o_ref, lse_ref):
        ls = [l0[...], l1[...], l2[...]]
        mx = jnp.maximum(jnp.maximum(ls[0], ls[1]), ls[2])
        es = [jnp.exp(l - mx) for l in ls]
        den = es[0] + es[1] + es[2]
        o = (es[0] / den) * o0[...] + (es[1] / den) * o1[...] + (es[2] / den) * o2[...]
        o_ref[...] = o.astype(BF16)
        lse_ref[...] = mx + jnp.log(den)

    row = pl.BlockSpec((tm, W), lambda i: (i, 0))
    return pl.pallas_call(
        body, name="merge_groups", grid=(S // tm,),
        out_shape=(jax.ShapeDtypeStruct((S, W), BF16), jax.ShapeDtypeStruct((S, W), F32)),
        in_specs=[row] * 6, out_specs=(row, row), compiler_params=_params(("parallel",), VMEM_LIMIT),
    )(*os_, *lses)


def _loss_head(x, target, fnw):
    S = x.shape[0]
    tm = 512

    def body(x_ref, t_ref, w_ref, dx_ref, st_ref):
        @pl.when(pl.program_id(0) == 0)
        def _():
            st_ref[...] = jnp.zeros_like(st_ref)

        xv = x_ref[...]
        rstd = lax.rsqrt(jnp.mean(xv * xv, axis=-1, keepdims=True) + RMS_EPS)
        xh = xv * rstd
        err = xh * w_ref[...] - t_ref[...]
        dy = err * (1.0 / D)
        dxh = dy * w_ref[...]
        dx_ref[...] = rstd * (dxh - xh * jnp.mean(dxh * xh, axis=-1, keepdims=True))
        st_ref[pl.ds(0, 1), :] = st_ref[pl.ds(0, 1), :] + jnp.sum(dy * xh, axis=0, keepdims=True)
        st_ref[pl.ds(1, 1), :] = st_ref[pl.ds(1, 1), :] + jnp.sum(err * err, axis=0, keepdims=True)

    row = pl.BlockSpec((tm, D), lambda i: (i, 0))
    return pl.pallas_call(
        body, name="loss_head", grid=(S // tm,),
        out_shape=(jax.ShapeDtypeStruct((S, D), F32), jax.ShapeDtypeStruct((8, D), F32)),
        in_specs=[row, row, _const_spec((1, D))], out_specs=(row, _const_spec((8, D))),
        compiler_params=_params(("arbitrary",), VMEM_LIMIT),
    )(x, target, fnw)


def _gate_bwd(dx, y, g, w, gu, *, w_is_transposed, name):
    S = dx.shape[0]
    K = w.shape[1] if w_is_transposed else w.shape[0]
    ffn = gu is not None
    tm = 256 if ffn else 512
    wout = 2 * K if ffn else K

    def body(dx_ref, y_ref, g_ref, w_ref, *rest):
        if ffn:
            gu_ref, da_ref, dyb_ref, st_ref = rest
        else:
            da_ref, dyb_ref, st_ref = rest

        @pl.when(pl.program_id(0) == 0)
        def _():
            st_ref[...] = jnp.zeros_like(st_ref)

        dxv = dx_ref[...]
        st_ref[pl.ds(0, 1), :] = st_ref[pl.ds(0, 1), :] + jnp.sum(dxv * y_ref[...].astype(F32), axis=0, keepdims=True)
        dyb = (dxv * g_ref[...]).astype(BF16)
        dyb_ref[...] = dyb
        da = _nn(dyb, w_ref[...]) if w_is_transposed else _nt(dyb, w_ref[...])
        if ffn:
            gate = gu_ref[:, pl.ds(0, K)].astype(F32)
            up = gu_ref[:, pl.ds(K, K)].astype(F32)
            sig = jax.nn.sigmoid(gate)
            da_ref[:, pl.ds(0, K)] = (da * up * (sig * (1.0 + gate * (1.0 - sig)))).astype(BF16)
            da_ref[:, pl.ds(K, K)] = (da * (gate * sig)).astype(BF16)
        else:
            da_ref[...] = da.astype(BF16)

    row = lambda w_: pl.BlockSpec((tm, w_), lambda i: (i, 0))
    in_specs = [row(D), row(D), _const_spec((1, D)), _const_spec(w.shape)]
    args = [dx, y, g, w]
    if ffn:
        in_specs.append(row(wout))
        args.append(gu)
    return pl.pallas_call(
        body, name=name, grid=(S // tm,),
        out_shape=(jax.ShapeDtypeStruct((S, wout), BF16), jax.ShapeDtypeStruct((S, D), BF16), jax.ShapeDtypeStruct((8, D), F32)),
        in_specs=in_specs, out_specs=(row(wout), row(D), _const_spec((8, D))),
        compiler_params=_params(("arbitrary",), VMEM_LIMIT),
    )(*args)


def _norm_bwd(dy, wt, x, dres, nw, sc, *, name):
    S, N = dy.shape
    tm = 256 if N > 4096 else 512

    def body(dy_ref, w_ref, x_ref, dres_ref, nw_ref, sc_ref, dx_ref, st_ref):
        @pl.when(pl.program_id(0) == 0)
        def _():
            st_ref[...] = jnp.zeros_like(st_ref)

        dh = _nn(dy_ref[...], w_ref[...])
        xv = x_ref[...]
        rstd = lax.rsqrt(jnp.mean(xv * xv, axis=-1, keepdims=True) + RMS_EPS)
        xh = xv * rstd
        nwv, scale = nw_ref[...], 1.0 + sc_ref[...]
        dxh = dh * (nwv * scale)
        dx_ref[...] = dres_ref[...] + rstd * (dxh - xh * jnp.mean(dxh * xh, axis=-1, keepdims=True))
        dhx = dh * xh
        st_ref[pl.ds(0, 1), :] = st_ref[pl.ds(0, 1), :] + jnp.sum(dh, axis=0, keepdims=True)
        st_ref[pl.ds(1, 1), :] = st_ref[pl.ds(1, 1), :] + jnp.sum(dhx * nwv, axis=0, keepdims=True)
        st_ref[pl.ds(2, 1), :] = st_ref[pl.ds(2, 1), :] + jnp.sum(dhx * scale, axis=0, keepdims=True)

    row = lambda w_: pl.BlockSpec((tm, w_), lambda i: (i, 0))
    vec = _const_spec((1, D))
    return pl.pallas_call(
        body, name=name, grid=(S // tm,),
        out_shape=(jax.ShapeDtypeStruct((S, D), F32), jax.ShapeDtypeStruct((8, D), F32)),
        in_specs=[row(N), _const_spec((N, D)), row(D), row(D), vec, vec], out_specs=(row(D), _const_spec((8, D))),
        compiler_params=_params(("arbitrary",), VMEM_LIMIT),
    )(dy, wt, x, dres, nw, sc)


def _weight_grad(a, b, *, transpose_out, name):
    S, N = b.shape
    nb = N // 2 if N > 4096 else N
    tk = 512

    def body(a_ref, b_ref, out_ref, acc):
        k = pl.program_id(1)

        @pl.when(k == 0)
        def _():
            acc[...] = jnp.zeros_like(acc)

        acc[...] += _tn(a_ref[...], b_ref[...])

        @pl.when(k == pl.num_programs(1) - 1)
        def _():
            out_ref[...] = (acc[...].T if transpose_out else acc[...]).astype(BF16)

    out_block = pl.BlockSpec((nb, D), lambda n, k: (n, 0)) if transpose_out else pl.BlockSpec((D, nb), lambda n, k: (0, n))
    return pl.pallas_call(
        body, name=name, grid=(N // nb, S // tk),
        out_shape=jax.ShapeDtypeStruct((N, D) if transpose_out else (D, N), BF16),
        in_specs=[pl.BlockSpec((tk, D), lambda n, k: (k, 0)), pl.BlockSpec((tk, nb), lambda n, k: (k, n))],
        out_specs=out_block, scratch_shapes=[pltpu.VMEM((D, nb), F32)],
        compiler_params=_params(("parallel", "arbitrary"), VMEM_LIMIT),
    )(a, b)


def _adamw(w, g, m, v):
    m = ADAM_B1 * m + (1.0 - ADAM_B1) * g
    v = ADAM_B2 * v + (1.0 - ADAM_B2) * (g * g)
    m_hat = m / (1.0 - ADAM_B1 ** ADAM_STEP)
    v_hat = v / (1.0 - ADAM_B2 ** ADAM_STEP)
    delta = -ADAM_LR * (m_hat / (jnp.sqrt(v_hat) + ADAM_EPS) + ADAM_WD * w)
    return delta, m, v


def _adam_shard(parts, own, w, m, v, name):
    R = w.shape[0]
    tr = max(t for t in (16, 32, 64, 128, 192, 256) if R % t == 0)

    def body(p_ref, o_ref, w_ref, m_ref, v_ref, g_out, d_out, m_out, v_out):
        me = _my_index()
        g = jnp.zeros((tr, D), F32)
        for j in range(N_DEV):
            g = g + jnp.where(me == j, o_ref[...], p_ref[j]).astype(F32)
        delta, mn, vn = _adamw(w_ref[...], g, m_ref[...], v_ref[...])
        g_out[...] = g
        d_out[...] = delta
        m_out[...] = mn
        v_out[...] = vn

    row = pl.BlockSpec((tr, D), lambda i: (i, 0))
    shp = jax.ShapeDtypeStruct((R, D), F32)
    return pl.pallas_call(
        body, name=name, grid=(R // tr,), out_shape=(shp,) * 4,
        in_specs=[pl.BlockSpec((N_DEV, tr, D), lambda i: (0, i, 0)), row, row, row, row], out_specs=(row,) * 4,
        compiler_params=_params(("parallel",), VMEM_LIMIT),
    )(parts, own, w, m, v)


def _adam_ada_w(cond_t, dmod, w, m, v):
    ncol = w.shape[-1]
    tr = 512

    def body(c_ref, d_ref, w_ref, m_ref, v_ref, g_out, d_out, m_out, v_out):
        g = _nn(c_ref[...], d_ref[0])
        delta, mn, vn = _adamw(w_ref[0], g, m_ref[0], v_ref[0])
        g_out[0] = g
        d_out[0] = delta
        m_out[0] = mn
        v_out[0] = vn

    blk = pl.BlockSpec((1, tr, ncol), lambda l, i: (l, i, 0))
    shp = jax.ShapeDtypeStruct(w.shape, F32)
    return pl.pallas_call(
        body, name="adam_ada_w", grid=(DEPTH, D // tr), out_shape=(shp,) * 4,
        in_specs=[pl.BlockSpec((tr, LANES), lambda l, i: (i, 0)), pl.BlockSpec((1, LANES, ncol), lambda l, i: (l, 0, 0)), blk, blk, blk],
        out_specs=(blk,) * 4, compiler_params=_params(("parallel", "parallel"), VMEM_LIMIT),
    )(cond_t, dmod, w, m, v)


TILE_ROWS = 168


def _stat_sources():
    pairs = []
    for i in range(DEPTH):
        b = 32 * i
        for q, src in enumerate((b, b + 1, b + 8, b + 16, b + 17, b + 24)):
            pairs.append((6 * i + q, src))
        pairs.append((24 + i, b + 2))
        pairs.append((32 + i, b + 18))
    pairs += [(40, 128), (41, 129)]
    return pairs


def _small_exchange(tiles, w, m, v):
    loss_row, sink_row, sink_src = 41, 48, 136

    def body(s_ref, w_ref, m_ref, v_ref, dmod_out, g_out, d_out, m_out, v_out, loss_out, all_ref, tot_ref, send_sems, recv_sems):
        me = _my_index()
        all_ref[me] = s_ref[...]
        copies = []
        for k in range(1, N_DEV):
            dev, _ = _peer(k)
            cp = pltpu.make_async_remote_copy(src_ref=s_ref, dst_ref=all_ref.at[me], send_sem=send_sems.at[k - 1],
                                              recv_sem=recv_sems.at[k - 1], device_id=dev, device_id_type=MESH)
            cp.start()
            copies.append(cp)
        for k in range(1, N_DEV):
            dev, pidx = _peer(k)
            pltpu.make_async_remote_copy(src_ref=s_ref, dst_ref=all_ref.at[pidx], send_sem=send_sems.at[k - 1],
                                         recv_sem=recv_sems.at[k - 1], device_id=dev, device_id_type=MESH).wait_recv()
        for cp in copies:
            cp.wait_send()
        tot = all_ref[0]
        for j in range(1, N_DEV):
            tot = tot + all_ref[j]
        tot_ref[...] = tot
        g_out[...] = jnp.zeros_like(g_out)
        for dst, src in _stat_sources():
            g_out[pl.ds(dst, 1), :] = tot_ref[pl.ds(src, 1), :]
            if dst < 24:
                for j in range(N_DEV):
                    dmod_out[j, pl.ds(dst, 1), :] = all_ref[j, pl.ds(src, 1), :]
        lane = lax.broadcasted_iota(jnp.int32, (1, D), 1)
        sink = jnp.zeros((1, D), F32)
        for h in range(32):
            sink = jnp.where(lane == h, tot_ref[pl.ds(sink_src + h, 1), :], sink)
        g_out[pl.ds(sink_row, 1), :] = sink
        g = g_out[...]
        delta, mn, vn = _adamw(w_ref[...], g, m_ref[...], v_ref[...])
        d_out[...] = delta
        m_out[...] = mn
        v_out[...] = vn
        loss = jnp.sum(g[loss_row:loss_row + 1, :], axis=-1, keepdims=True) * (0.5 / D)
        loss_out[...] = jnp.broadcast_to(loss, loss_out.shape)

    vm = pl.BlockSpec(memory_space=pltpu.VMEM)
    shp = jax.ShapeDtypeStruct((STAT_ROWS, D), F32)
    return pl.pallas_call(
        body, name="small_exchange",
        out_shape=(jax.ShapeDtypeStruct((N_DEV, 24, D), F32), shp, shp, shp, shp, jax.ShapeDtypeStruct((8, LANES), F32)),
        in_specs=[vm] * 4, out_specs=(vm,) * 6,
        scratch_shapes=[pltpu.VMEM((N_DEV, TILE_ROWS, D), F32), pltpu.VMEM((TILE_ROWS, D), F32),
                        pltpu.SemaphoreType.DMA((N_DEV - 1,)), pltpu.SemaphoreType.DMA((N_DEV - 1,))],
        compiler_params=_params(vmem=VMEM_LIMIT),
    )(tiles, w, m, v)


def _to_rows(name, a):
    if name in ("ffn_in", "a_in", "b_in"):
        return a.T
    if name == "b_out":
        return a.T.reshape(-1, D)
    return a


def _from_rows(name, a):
    if name in ("ffn_in", "a_in", "b_in"):
        return a.T
    if name == "b_out":
        return a.reshape(-1, 512).T
    return a


def _rows8(a):
    return jnp.pad(a, ((0, 8 - a.shape[0]), (0, 0)))


def _pack_small(ada_b, norm_mix, norm_ffn, final_norm, sink):
    sink_row = jnp.pad(sink.reshape(1, -1), ((0, 0), (0, D - sink.size)))
    return jnp.concatenate([ada_b.reshape(24, D), _rows8(norm_mix), _rows8(norm_ffn), _rows8(final_norm.reshape(1, D)),
                            _rows8(sink_row)], axis=0)


def _unpack_small(a):
    return a[0:24].reshape(4, 6 * D), a[24:28], a[32:36], a[40], a[48, :32].reshape(2, 16)


def kernel(x, c, ada_w, ada_b, norm_mix, norm_ffn, ffn_w_in, ffn_w_out, a_w_in, a_w_out, a_sink, b_w_in, b_w_out, final_norm, loss_target, m_ada_w, m_ada_b, m_norm_mix, m_norm_ffn, m_ffn_w_in, m_ffn_w_out, m_a_w_in, m_a_w_out, m_a_sink, m_b_w_in, m_b_w_out, m_final_norm, v_ada_w, v_ada_b, v_norm_mix, v_norm_ffn, v_ffn_w_in, v_ffn_w_out, v_a_w_in, v_a_w_out, v_a_sink, v_b_w_in, v_b_w_out, v_final_norm):
    S = x.shape[1]
    x0 = x.reshape(S, D)
    target = loss_target.reshape(S, D)
    me = _my_index()
    ncol = ada_w.shape[-1]

    ada_b_mine = lax.dynamic_slice_in_dim(ada_b, me * ncol, ncol, axis=1)
    cond_all, parts = _cond_exchange(jnp.broadcast_to(c.reshape(1, D), (8, D)), ada_w, ada_b_mine)
    mod = lax.dynamic_index_in_dim(parts, me, axis=2, keepdims=False)
    mod = jnp.transpose(mod, (1, 0, 2)).reshape(DEPTH, 6, 1, D)

    weights = {"ffn_in": ffn_w_in, "ffn_out": ffn_w_out, "a_in": a_w_in, "a_out": a_w_out, "b_in": b_w_in, "b_out": b_w_out}
    shard = {(n, l): _to_rows(n, weights[n][l]).astype(BF16) for n, l, _ in SEGMENTS}
    first = [sg for sg in _layer_segments(0) if not sg[0].startswith("ffn")]
    gathered0 = _all_gather_weights([shard[(n, l)] for n, l, _ in first])
    W = {(n, l): g for (n, l, _), g in zip(first, gathered0)}
    groups = [[sg for sg in _layer_segments(0) if sg[0].startswith("ffn")], [sg for i in range(1, DEPTH) for sg in _layer_segments(i)]]
    gathers, order = [], gathered0[0]
    for q, segs in enumerate(groups):
        mine = jnp.concatenate([shard[(n, l)] for n, l, _ in segs], axis=0)
        mine, order = lax.optimization_barrier((mine, order))
        zone = lax.empty((N_DEV, mine.shape[0], D), BF16)
        gathers.append(_exchange_start([mine], zone, [mine.shape[0]], [0], False, "weight_gather_start_%d" % q))
        order = gathers[-1][-1]
    gather_token = order[0:1, 0:1]

    def finish_gather(q, after):
        zone = _exchange_wait(gathers[q], after, "weight_gather_wait_%d" % q)
        offs, _ = _offsets(groups[q])
        for (n, l, rows), off in zip(groups[q], offs):
            full = lax.dynamic_update_slice(zone[:, off:off + rows], shard[(n, l)][None], (me, 0, 0))
            W[(n, l)] = full.reshape(D, 512) if n == "b_out" else full.reshape(N_DEV * rows, D)

    a_slopes, b_slopes = _slopes(16), _slopes(24)
    bias_a = _alibi_bias(a_slopes, A_HALF, 1)
    bias_b = [_alibi_bias(b_slopes[8 * g:8 * g + 8], B_HALF, dil) for g, dil in enumerate(B_DILS)]
    bias_b_fwd = [_alibi_bias(b_slopes[8 * g:8 * g + 8], B_HALF, dil, max(CHUNK, TQ * dil), both=True) for g, dil in enumerate(B_DILS)]
    a_geom = dict(C=A_QKV, r=1, half=A_HALF, qoff=0, koff=1024, voff=1280, n_units=2)
    b_geom = [dict(C=B_QKV, r=dil, half=B_HALF, qoff=512 * g, koff=1536 + 128 * g, voff=1920 + 128 * g, n_units=1)
              for g, dil in enumerate(B_DILS)]

    saved = []
    xcur = x0
    for i in range(DEPTH):
        j = i // 2
        sh1, sc1, g1, sh2, sc2, g2 = [mod[i, q] for q in range(6)]
        nm, nf = norm_mix[i].reshape(1, D), norm_ffn[i].reshape(1, D)
        if i == 0:
            nm = nm + gather_token
        if i == 1:
            finish_gather(1, xcur)
        if i % 2 == 0:
            sink_rep = jnp.repeat(jnp.repeat(a_sink[j], TQ).reshape(2, 1, 8 * TQ), 8, axis=1).reshape(16, 8 * TQ)
            h1, qkv = _proj(xcur, nm, sc1, sh1, W[("a_in", j)], ffn=False, name="proj_a")
            o, lse = _attn_fwd(qkv, bias_a, sink_rep, out_dtype=BF16, name="attn_a_fwd", **a_geom)
            x1, y1 = _gated_residual(o, W[("a_out", j)], xcur, g1, w_is_transposed=False, name="out_a")
        else:
            sink_rep = None
            h1, qkv = _proj(xcur, nm, sc1, sh1, W[("b_in", j)], ffn=False, name="proj_b")
            outs = [_attn_fwd(qkv, bias_b_fwd[g], None, out_dtype=F32, name="attn_b%d_fwd" % g, **b_geom[g]) for g in range(3)]
            o, lse = _merge_groups([t[0] for t in outs], [t[1] for t in outs])
            x1, y1 = _gated_residual(o, W[("b_out", j)], xcur, g1, w_is_transposed=True, name="out_b")
        if i == 0:
            finish_gather(0, x1)
        h2, gu, act = _proj(x1, nf, sc2, sh2, W[("ffn_in", i)], ffn=True, name="ffn_in")
        x2, y2 = _gated_residual(act, W[("ffn_out", i)], x1, g2, w_is_transposed=False, name="ffn_out")
        saved.append(dict(x0=xcur, h1=h1, qkv=qkv, o=o, lse=lse, y1=y1, x1=x1, h2=h2, gu=gu, act=act, y2=y2, sink=sink_rep))
        xcur = x2

    dx, head_stats = _loss_head(xcur, target, final_norm.reshape(1, D))

    dW = {}
    stat_tiles, dsink = [None] * DEPTH, [None] * 2
    exchanges = []
    start_token = None

    def start_exchange(segs):
        offs, total = _offsets(segs)
        own = jnp.concatenate([lax.dynamic_slice_in_dim(dW[(n, l)], me * rows, rows, axis=0) for n, l, rows in segs], axis=0)
        started = _exchange_start([dW[(n, l)] for n, l, _ in segs], lax.empty((N_DEV, total, D), BF16), [sg[2] for sg in segs],
                                  offs, True, "grad_exchange_start_%d" % len(exchanges))
        exchanges.append((segs, started, own))
        return started[-1][0:1, 0:1]

    for i in reversed(range(DEPTH)):
        j = i // 2
        sv = saved[i]
        sh1, sc1, g1, sh2, sc2, g2 = [mod[i, q] for q in range(6)]
        if start_token is not None:
            g2 = g2 + start_token
            start_token = None
        nm, nf = norm_mix[i].reshape(1, D), norm_ffn[i].reshape(1, D)
        dgu, dy2, st_g2 = _gate_bwd(dx, sv["y2"], g2, W[("ffn_out", i)], sv["gu"], w_is_transposed=False, name="ffn_out_bwd")
        dW[("ffn_out", i)] = _weight_grad(dy2, sv["act"], transpose_out=True, name="dw_ffn_out")
        dW[("ffn_in", i)] = _weight_grad(sv["h2"], dgu, transpose_out=True, name="dw_ffn_in")
        if i == 0:
            g1 = g1 + start_exchange([sg for sg in _layer_segments(0) if sg[0].startswith("ffn")])
        dx1, st_f = _norm_bwd(dgu, W[("ffn_in", i)], sv["x1"], dx, nf, sc2, name="ffn_in_bwd")
        if i % 2 == 0:
            do, dy1, st_g1 = _gate_bwd(dx1, sv["y1"], g1, W[("a_out", j)], None, w_is_transposed=False, name="out_a_bwd")
            dW[("a_out", j)] = _weight_grad(dy1, sv["o"], transpose_out=True, name="dw_a_out")
            dq, dk, dv, ds = _attn_bwd(sv["qkv"], bias_a, sv["sink"], sv["o"], do, sv["lse"], name="attn_a_bwd", **a_geom)
            dsink[j] = ds
            dqkv = jnp.concatenate([dq, dk[0].astype(BF16), dk[1].astype(BF16), dv[0].astype(BF16), dv[1].astype(BF16)], axis=1)
            dW[("a_in", j)] = _weight_grad(sv["h1"], dqkv, transpose_out=True, name="dw_a_in")
            dx0, st_m = _norm_bwd(dqkv, W[("a_in", j)], sv["x0"], dx1, nm, sc1, name="proj_a_bwd")
        else:
            do, dy1, st_g1 = _gate_bwd(dx1, sv["y1"], g1, W[("b_out", j)], None, w_is_transposed=True, name="out_b_bwd")
            dW[("b_out", j)] = _weight_grad(dy1, sv["o"], transpose_out=False, name="dw_b_out").reshape(N_DEV * 64, D)
            gr = [_attn_bwd(sv["qkv"], bias_b[g], None, sv["o"], do, sv["lse"], name="attn_b%d_bwd" % g, **b_geom[g]) for g in range(3)]
            dqkv = jnp.concatenate([t[0] for t in gr] + [t[1][0].astype(BF16) for t in gr] + [t[2][0].astype(BF16) for t in gr], axis=1)
            dW[("b_in", j)] = _weight_grad(sv["h1"], dqkv, transpose_out=True, name="dw_b_in")
            dx0, st_m = _norm_bwd(dqkv, W[("b_in", j)], sv["x0"], dx1, nm, sc1, name="proj_b_bwd")
        stat_tiles[i] = [st_m, st_g1, st_f, st_g2]
        if i > 0:
            start_token = start_exchange(_layer_segments(i))
        else:
            start_exchange([sg for sg in _layer_segments(0) if not sg[0].startswith("ffn")])
        dx = dx0
    grad_x = dx.reshape(1, S, D)

    masters = {"ffn_in": (ffn_w_in, m_ffn_w_in, v_ffn_w_in), "ffn_out": (ffn_w_out, m_ffn_w_out, v_ffn_w_out),
               "a_in": (a_w_in, m_a_w_in, v_a_w_in), "a_out": (a_w_out, m_a_w_out, v_a_w_out),
               "b_in": (b_w_in, m_b_w_in, v_b_w_in), "b_out": (b_w_out, m_b_w_out, v_b_w_out)}
    pieces = {}
    after = dx
    for segs, started, own in exchanges:
        offs, total = _offsets(segs)
        parts_g = _exchange_wait(started, after, "grad_exchange_wait_%d" % total)
        rows_wmv = [jnp.concatenate([_to_rows(n, masters[n][q][l]) for n, l, _ in segs], axis=0) for q in range(3)]
        res_rows = _adam_shard(parts_g, own, *rows_wmv, name="adam_%d" % total)
        after = res_rows[0]
        for q, kind in enumerate(("grad", "delta", "m", "v")):
            for (n, l, rows), off in zip(segs, offs):
                pieces[(kind, n, l)] = _from_rows(n, res_rows[q][off:off + rows])
    big = {(kind, n): jnp.stack([pieces[(kind, n, l)] for l in range(4 if n.startswith("ffn") else 2)])
           for kind in ("grad", "delta", "m", "v") for n in masters}

    tiles = jnp.concatenate([t for i in range(DEPTH) for t in stat_tiles[i]] + [head_stats]
                            + [jnp.pad(ds, ((0, 0), (0, D - LANES))) for ds in dsink], axis=0)
    small = [_pack_small(*t) for t in ((ada_b, norm_mix, norm_ffn, final_norm, a_sink),
                                       (m_ada_b, m_norm_mix, m_norm_ffn, m_final_norm, m_a_sink),
                                       (v_ada_b, v_norm_mix, v_norm_ffn, v_final_norm, v_a_sink))]
    dmod_all, sg, sd, sm, sv_, loss_tile = _small_exchange(tiles, *small)
    loss = loss_tile[0, 0]
    dmod_all = dmod_all.reshape(N_DEV, DEPTH, 6 * D)
    dmod_mine = lax.dynamic_slice_in_dim(dmod_all, me * ncol, ncol, axis=2)
    dmod_pad = jnp.pad(jnp.transpose(dmod_mine, (1, 0, 2)), ((0, 0), (0, LANES - N_DEV), (0, 0))).astype(BF16)
    cond_t = jnp.pad(cond_all.T, ((0, 0), (0, LANES - N_DEV))).astype(BF16)
    ada = _adam_ada_w(cond_t, dmod_pad, ada_w, m_ada_w, v_ada_w)

    outs = [loss, grad_x]
    small_res = [_unpack_small(t) for t in (sg, sd, sm, sv_)]
    for q, kind in enumerate(("grad", "delta", "m", "v")):
        ab, nm_, nf_, fn, sk = small_res[q]
        outs += [ada[q], ab, nm_, nf_, big[(kind, "ffn_in")], big[(kind, "ffn_out")], big[(kind, "a_in")], big[(kind, "a_out")],
                 sk, big[(kind, "b_in")], big[(kind, "b_out")], fn]
    return tuple(outs)
```

```python
import functools
import math

import numpy as np
import jax
import jax.numpy as jnp
from jax import lax
from jax.experimental import pallas as pl
from jax.experimental.pallas import tpu as pltpu

D = 1024
HEAD_DIM = 64
D_FF = 2816
DEPTH = 4
N_DEV = 8
A_QKV = 1536
B_QKV = 2304
A_HALF = 128
B_HALF = 64
B_DILS = (1, 4, 16)
RMS_EPS = 1e-6
NEG = -1e30
ADAM_LR = 0.001
ADAM_B1 = 0.9
ADAM_B2 = 0.999
ADAM_EPS = 1e-08
ADAM_WD = 0.01
ADAM_STEP = 10

LANES = 128
TQ = 128
VMEM_LIMIT = 56 * 1024 * 1024
MESH = pl.DeviceIdType.MESH
F32 = jnp.float32
BF16 = jnp.bfloat16

SEGMENTS = ([("ffn_in", l, 704) for l in range(4)] + [("ffn_out", l, 352) for l in range(4)]
            + [("a_in", j, 192) for j in range(2)] + [("a_out", j, 128) for j in range(2)]
            + [("b_in", j, 288) for j in range(2)] + [("b_out", j, 64) for j in range(2)])
def _layer_segments(i):
    mixer = "a" if i % 2 == 0 else "b"
    return [s for s in SEGMENTS if (s[0].startswith("ffn") and s[1] == i) or (s[0].startswith(mixer + "_") and s[1] == i // 2)]


def _offsets(segs):
    rows = [s[2] for s in segs]
    return [sum(rows[:k]) for k in range(len(rows))], sum(rows)
STAT_ROWS = 56


def _nn(a, b):
    return jnp.dot(a, b, preferred_element_type=F32)


def _nt(a, b):
    return lax.dot_general(a, b, (((1,), (1,)), ((), ())), preferred_element_type=F32)


def _tn(a, b):
    return lax.dot_general(a, b, (((0,), (0,)), ((), ())), preferred_element_type=F32)


def _params(dims=None, vmem=None):
    kw = {}
    if dims is not None:
        kw["dimension_semantics"] = dims
    if vmem is not None:
        kw["vmem_limit_bytes"] = vmem
    return pltpu.CompilerParams(**kw)


def _my_index():
    return 4 * lax.axis_index("x") + 2 * lax.axis_index("y") + lax.axis_index("c")


def _peer(k):
    x, y, c = lax.axis_index("x"), lax.axis_index("y"), lax.axis_index("c")
    px, py, pc = x ^ ((k >> 2) & 1), y ^ ((k >> 1) & 1), c ^ (k & 1)
    return (px, py, pc), 4 * px + 2 * py + pc


def _const_spec(shape):
    nd = len(shape)
    return pl.BlockSpec(shape, lambda *_: (0,) * nd)


def _cond_exchange(c_tile, ada_w, ada_b_mine):
    ncol = ada_w.shape[-1]

    def body(c_ref, w_ref, b_ref, cond_ref, parts_ref, call_ref, mine_ref, send_sems, recv_sems):
        me = _my_index()
        call_ref[me] = c_ref[...]
        copies = []
        for k in range(1, N_DEV):
            dev, _ = _peer(k)
            cp = pltpu.make_async_remote_copy(src_ref=c_ref, dst_ref=call_ref.at[me], send_sem=send_sems.at[0, k - 1],
                                              recv_sem=recv_sems.at[0, k - 1], device_id=dev, device_id_type=MESH)
            cp.start()
            copies.append(cp)
        for k in range(1, N_DEV):
            _, pidx = _peer(k)
            pltpu.make_async_remote_copy(src_ref=c_ref, dst_ref=call_ref.at[pidx], send_sem=send_sems.at[0, k - 1],
                                         recv_sem=recv_sems.at[0, k - 1], device_id=_peer(k)[0], device_id_type=MESH).wait_recv()
        for cp in copies:
            cp.wait_send()
        row = lax.broadcasted_iota(jnp.int32, (N_DEV, D), 0)
        cmat = jnp.zeros((N_DEV, D), F32)
        for j in range(N_DEV):
            cmat = jnp.where(row == j, call_ref[j], cmat)
        cond = cmat * jax.nn.sigmoid(cmat)
        cond_ref[...] = cond
        cb = cond.astype(BF16)
        for l in range(DEPTH):
            mine_ref[l] = _nn(cb, w_ref[l].astype(BF16)) + b_ref[pl.ds(l, 1), :]
        parts_ref[me] = mine_ref[...]
        copies = []
        for k in range(1, N_DEV):
            dev, _ = _peer(k)
            cp = pltpu.make_async_remote_copy(src_ref=mine_ref, dst_ref=parts_ref.at[me], send_sem=send_sems.at[1, k - 1],
                                              recv_sem=recv_sems.at[1, k - 1], device_id=dev, device_id_type=MESH)
            cp.start()
            copies.append(cp)
        for k in range(1, N_DEV):
            dev, pidx = _peer(k)
            pltpu.make_async_remote_copy(src_ref=mine_ref, dst_ref=parts_ref.at[pidx], send_sem=send_sems.at[1, k - 1],
                                         recv_sem=recv_sems.at[1, k - 1], device_id=dev, device_id_type=MESH).wait_recv()
        for cp in copies:
            cp.wait_send()

    vm = pl.BlockSpec(memory_space=pltpu.VMEM)
    return pl.pallas_call(
        body, name="cond_exchange",
        out_shape=(jax.ShapeDtypeStruct((N_DEV, D), F32), jax.ShapeDtypeStruct((N_DEV, DEPTH, N_DEV, ncol), F32)),
        in_specs=[vm, vm, vm], out_specs=(vm, vm),
        scratch_shapes=[pltpu.VMEM((N_DEV, N_DEV, D), F32), pltpu.VMEM((DEPTH, N_DEV, ncol), F32),
                        pltpu.SemaphoreType.DMA((2, N_DEV - 1)), pltpu.SemaphoreType.DMA((2, N_DEV - 1))],
        compiler_params=_params(vmem=VMEM_LIMIT),
    )(c_tile, ada_w, ada_b_mine)[:2]


def _all_gather_weights(shards):
    n = len(shards)
    big = max(range(n), key=lambda s: shards[s].shape[0])
    total = sum(sh.shape[0] for sh in shards)
    assert N_DEV * shards[big].shape[0] >= total

    def body(*refs):
        ins, outs = refs[:n], refs[n:2 * n]
        local_sems, send_sems, recv_sems = refs[2 * n:]
        me = _my_index()
        local = []
        for s in range(n):
            rows = ins[s].shape[0]
            cp = pltpu.make_async_copy(ins[s], outs[s].at[pl.ds(me * rows, rows)], local_sems.at[s])
            cp.start()
            local.append(cp)
        for k in range(1, N_DEV):
            dev, _ = _peer(k)
            for s in range(n):
                rows = ins[s].shape[0]
                pltpu.make_async_remote_copy(src_ref=ins[s], dst_ref=outs[s].at[pl.ds(me * rows, rows)],
                                             send_sem=send_sems.at[k - 1], recv_sem=recv_sems.at[k - 1],
                                             device_id=dev, device_id_type=MESH).start()
        whole = outs[big].at[pl.ds(0, total)]
        for k in range(1, N_DEV):
            dev, _ = _peer(k)
            w = pltpu.make_async_remote_copy(src_ref=whole, dst_ref=whole, send_sem=send_sems.at[k - 1],
                                             recv_sem=recv_sems.at[k - 1], device_id=dev, device_id_type=MESH)
            w.wait_send()
            w.wait_recv()
        for cp in local:
            cp.wait()

    hbm = pl.BlockSpec(memory_space=pl.ANY)
    return pl.pallas_call(
        body, name="weight_all_gather",
        out_shape=tuple(jax.ShapeDtypeStruct((N_DEV * s.shape[0], D), s.dtype) for s in shards),
        in_specs=[hbm] * n, out_specs=tuple([hbm] * n),
        scratch_shapes=[pltpu.SemaphoreType.DMA((n,)), pltpu.SemaphoreType.DMA((N_DEV - 1,)),
                        pltpu.SemaphoreType.DMA((N_DEV - 1,))],
    )(*shards)


HBM = pl.BlockSpec(memory_space=pltpu.HBM)
SEM = pl.BlockSpec(memory_space=pltpu.SEMAPHORE)
EFFECT = pltpu.SideEffectType.DATAFLOW_SIDE_EFFECTING


def _exchange_start(srcs, landing, rows, offs, to_peer_rows, after, name):
    n = len(srcs)

    def body(*refs):
        src_refs, land_ref = refs[:n], refs[n]
        send_sems, recv_sems = refs[n + 2], refs[n + 3]
        token = refs[-1]
        me = _my_index()
        for k in range(1, N_DEV):
            dev, pidx = _peer(k)
            for q in range(n):
                src = src_refs[q].at[pl.ds(pidx * rows[q], rows[q])] if to_peer_rows else src_refs[q]
                pltpu.make_async_remote_copy(src_ref=src, dst_ref=land_ref.at[me, pl.ds(offs[q], rows[q])],
                                             send_sem=send_sems.at[k - 1], recv_sem=recv_sems.at[k - 1],
                                             device_id=dev, device_id_type=MESH).start()
        token[...] = jnp.zeros_like(token)

    arrays = list(srcs) + [landing]
    return pl.pallas_call(
        body, name=name,
        out_shape=(pltpu.SemaphoreType.DMA((N_DEV - 1,)), pltpu.SemaphoreType.DMA((N_DEV - 1,)),
                   *[pltpu.HBM(a.shape, a.dtype) for a in arrays], jax.ShapeDtypeStruct((8, LANES), F32)),
        in_specs=[HBM] * (n + 1) + [pl.BlockSpec(memory_space=pl.ANY)],
        out_specs=(SEM, SEM, *[HBM] * (n + 1), pl.BlockSpec(memory_space=pltpu.VMEM)),
        input_output_aliases={q: 2 + q for q in range(n + 1)},
        compiler_params=pltpu.CompilerParams(has_side_effects=EFFECT),
    )(*[pltpu.with_memory_space_constraint(a, pltpu.HBM) for a in arrays], after)


def _exchange_wait(started, after, name):
    send_sems, recv_sems = started[0], started[1]
    arrays = list(started[2:-1])
    n1 = len(arrays)

    def body(*refs):
        land_ref = refs[n1 - 1]
        sends, recvs = refs[n1], refs[n1 + 1]
        for k in range(1, N_DEV):
            dev, _ = _peer(k)
            w = pltpu.make_async_remote_copy(src_ref=land_ref.at[0], dst_ref=land_ref.at[0], send_sem=sends.at[k - 1],
                                             recv_sem=recvs.at[k - 1], device_id=dev, device_id_type=MESH)
            w.wait_send()
            w.wait_recv()

    return pl.pallas_call(
        body, name=name, out_shape=tuple(pltpu.HBM(a.shape, a.dtype) for a in arrays),
        in_specs=[HBM] * n1 + [SEM, SEM, pl.BlockSpec(memory_space=pl.ANY)], out_specs=tuple([HBM] * n1),
        input_output_aliases={q: q for q in range(n1)},
        compiler_params=pltpu.CompilerParams(has_side_effects=EFFECT),
    )(*arrays, send_sems, recv_sems, after)[n1 - 1]


def _norm_mod(x, nw, sc, sh):
    ms = jnp.mean(x * x, axis=-1, keepdims=True)
    xh = x * lax.rsqrt(ms + RMS_EPS)
    return xh, (xh * nw) * (1.0 + sc) + sh


def _proj(x, nw, sc, sh, wt, *, ffn, name):
    S, N = x.shape[0], wt.shape[0]
    tm = 256 if ffn else 512

    def body(x_ref, nw_ref, sc_ref, sh_ref, w_ref, h_ref, out_ref, *act_ref):
        _, h = _norm_mod(x_ref[...], nw_ref[...], sc_ref[...], sh_ref[...])
        hb = h.astype(BF16)
        h_ref[...] = hb
        if ffn:
            gate = _nt(hb, w_ref[pl.ds(0, D_FF), :])
            up = _nt(hb, w_ref[pl.ds(D_FF, D_FF), :])
            out_ref[:, pl.ds(0, D_FF)] = gate.astype(BF16)
            out_ref[:, pl.ds(D_FF, D_FF)] = up.astype(BF16)
            act_ref[0][...] = ((gate * jax.nn.sigmoid(gate)) * up).astype(BF16)
        else:
            out_ref[...] = _nt(hb, w_ref[...]).astype(BF16)

    row = lambda w: pl.BlockSpec((tm, w), lambda i: (i, 0))
    out_shape = [jax.ShapeDtypeStruct((S, D), BF16), jax.ShapeDtypeStruct((S, N), BF16)]
    out_specs = [row(D), row(N)]
    if ffn:
        out_shape.append(jax.ShapeDtypeStruct((S, D_FF), BF16))
        out_specs.append(row(D_FF))
    vec = _const_spec((1, D))
    return pl.pallas_call(
        body, name=name, grid=(S // tm,), out_shape=tuple(out_shape),
        in_specs=[row(D), vec, vec, vec, _const_spec((N, D))], out_specs=tuple(out_specs),
        compiler_params=_params(("parallel",), VMEM_LIMIT),
    )(x, nw, sc, sh, wt)


def _gated_residual(a, w, x, g, *, w_is_transposed, name):
    S, K = a.shape
    tm = 512

    def body(a_ref, w_ref, x_ref, g_ref, xo_ref, y_ref):
        y = _nt(a_ref[...], w_ref[...]) if w_is_transposed else _nn(a_ref[...], w_ref[...])
        y_ref[...] = y.astype(BF16)
        xo_ref[...] = x_ref[...] + g_ref[...] * y

    row = lambda w_: pl.BlockSpec((tm, w_), lambda i: (i, 0))
    return pl.pallas_call(
        body, name=name, grid=(S // tm,),
        out_shape=(jax.ShapeDtypeStruct((S, D), F32), jax.ShapeDtypeStruct((S, D), BF16)),
        in_specs=[row(K), _const_spec(w.shape), row(D), _const_spec((1, D))], out_specs=(row(D), row(D)),
        compiler_params=_params(("parallel",), VMEM_LIMIT),
    )(a, w, x, g)


CHUNK = 1024


def _tile_rows(r, chunk=CHUNK):
    return min(TQ, chunk // r)


def _alibi_bias(slopes, half, dil, chunk=CHUNK, both=False):
    tq = _tile_rows(dil, chunk)
    tk = tq + 2 * half
    rel = np.arange(tk)[:, None] - half - np.arange(tq)[None, :]
    band = np.abs(rel) <= half
    dist = (dil * np.abs(rel)).astype(np.float32)
    tabs = [np.where(band, -np.float32(s) * dist, np.float32(NEG)).astype(np.float32) for s in slopes]
    out = []
    for u in range(0, len(tabs), 8):
        tab = np.concatenate(tabs[u:u + 8], axis=1)
        first, last = tab.copy(), tab.copy()
        first[:half] = NEG
        last[tk - half:] = NEG
        out += [tab, first, last]
        if both:
            last = last.copy()
            last[:half] = NEG
            out.append(last)
    return jnp.asarray(np.concatenate(out, axis=0))


def _slopes(n):
    return (2.0 ** (-8.0 * np.arange(1, n + 1) / n)).astype(np.float32)


def _head_masks(tq):
    lane = lax.broadcasted_iota(jnp.int32, (tq, LANES), 1)
    lo = lane < HEAD_DIM
    return lo, jnp.logical_not(lo)


def _stack_heads(tiles, lo, hi, scale):
    blocks = []
    for t in range(4):
        xf = tiles[t] if scale == 1.0 else tiles[t] * scale
        for a in range(2):
            xm = jnp.where(lo if a == 0 else hi, xf, 0.0)
            if a != t // 2:
                xm = pltpu.roll(xm, HEAD_DIM, 1)
            blocks.append(xm.astype(BF16))
    return jnp.concatenate(blocks, axis=0)


def _tile_from_columns(x8t, t, tq):
    r0 = HEAD_DIM * (t // 2)
    top = x8t[r0:r0 + HEAD_DIM, 2 * t * tq:(2 * t + 1) * tq]
    bot = x8t[r0:r0 + HEAD_DIM, (2 * t + 1) * tq:(2 * t + 2) * tq]
    return jnp.concatenate([top, bot], axis=0).T


def _attn_layout(S, C, r, half, qoff, koff, voff, chunk):
    hb = half * r
    per = chunk // hb
    nhb = S // hb
    main = lambda off: pl.BlockSpec((chunk, LANES), lambda u, i: (i, off // LANES + u))
    prev = lambda off: pl.BlockSpec((hb, LANES), lambda u, i: (jnp.maximum(i * per - 1, 0), off // LANES + u))
    nxt = lambda off: pl.BlockSpec((hb, LANES), lambda u, i: (jnp.minimum((i + 1) * per, nhb - 1), off // LANES + u))
    specs = [pl.BlockSpec((chunk, 4 * LANES), lambda u, i: (i, qoff // (4 * LANES) + u))]
    specs += [prev(koff), main(koff), nxt(koff), prev(voff), main(voff), nxt(voff)]
    return specs, hb


def _stage(dst, srcs):
    row = 0
    for src in srcs:
        n = src.shape[0]
        dst[pl.ds(row, n), :] = src[...].astype(F32)
        row += n


def _rows(start, n, r):
    return pl.ds(start, n, stride=r) if r > 1 else pl.ds(start, n)


def _attn_fwd(qkv, bias, sink, *, C, r, half, qoff, koff, voff, n_units, out_dtype, name):
    S = qkv.shape[0]
    chunk = max(CHUNK, TQ * r)
    tq = _tile_rows(r, chunk)
    tk = tq + 2 * half
    tiles = chunk // (r * tq)
    nsteps = S // chunk
    specs, hb = _attn_layout(S, C, r, half, qoff, koff, voff, chunk)
    use_sink = sink is not None

    def body(*refs):
        q_ref, kp, km, kn, vp, vm, vn, bias_ref = refs[:8]
        rest = list(refs[8:])
        sink_ref = rest.pop(0) if use_sink else None
        o_ref, lse_ref, qs, ks, vs, os_, ls = rest
        i = pl.program_id(1)
        for t in range(4):
            qs[t] = q_ref[:, pl.ds(t * LANES, LANES)].astype(F32)
        _stage(ks, [kp, km, kn])
        _stage(vs, [vp, vm, vn])
        lo, hi = _head_masks(tq)
        ones = jnp.ones((16, tk), BF16)
        if use_sink:
            sk = sink_ref[pl.ds(0, 1), :]

        def chain(n, carry):
            rho, c = n // tiles, n % tiles
            start = c * (tq * r) + rho
            if r == 1:
                start = pl.multiple_of(start, tq)
            variant = jnp.where(jnp.logical_and(i == 0, c == 0), 1, 0) + jnp.where(
                jnp.logical_and(i == nsteps - 1, c == tiles - 1), 2, 0)
            k2 = ks[_rows(start, tk, r), :].astype(BF16)
            v2t = jnp.concatenate([vs[_rows(start, tk, r), :].T.astype(BF16), ones], axis=0)
            q8 = _stack_heads([qs[t, _rows(start, tq, r), :] for t in range(4)], lo, hi, HEAD_DIM ** -0.5)
            s = _nt(k2, q8) + bias_ref[pl.ds(pl.multiple_of(variant * tk, 8), tk), :]
            m = jnp.max(s, axis=0, keepdims=True)
            if use_sink:
                m = jnp.maximum(m, sk)
            pv = _nn(v2t, jnp.exp(s - m).astype(BF16))
            l = pv[LANES:LANES + 1]
            if use_sink:
                l = l + jnp.exp(sk - m)
            o8t = pv[:LANES] / l
            lse8 = jnp.broadcast_to(m + jnp.log(l), (LANES, 8 * tq))
            for t in range(4):
                os_[t, _rows(start, tq, r), :] = _tile_from_columns(o8t, t, tq)
                ls[t, _rows(start, tq, r), :] = _tile_from_columns(lse8, t, tq)
            return carry

        lax.fori_loop(0, r * tiles, chain, 0, unroll=2)
        for t in range(4):
            o_ref[:, pl.ds(t * LANES, LANES)] = os_[t].astype(out_dtype)
            lse_ref[:, pl.ds(t * LANES, LANES)] = ls[t]

    in_specs = specs + [pl.BlockSpec((bias.shape[0] // n_units, 8 * tq), lambda u, i: (u, 0))]
    args = [qkv] * 7 + [bias]
    if use_sink:
        in_specs.append(pl.BlockSpec((8, 8 * tq), lambda u, i: (u, 0)))
        args.append(sink)
    wide = pl.BlockSpec((chunk, 4 * LANES), lambda u, i: (i, u))
    win = hb + chunk + hb
    return pl.pallas_call(
        body, name=name, grid=(n_units, nsteps),
        out_shape=(jax.ShapeDtypeStruct((S, n_units * 512), out_dtype), jax.ShapeDtypeStruct((S, n_units * 512), F32)),
        in_specs=in_specs, out_specs=(wide, wide),
        scratch_shapes=[pltpu.VMEM((4, chunk, LANES), F32), pltpu.VMEM((win, LANES), F32), pltpu.VMEM((win, LANES), F32),
                        pltpu.VMEM((4, chunk, LANES), F32), pltpu.VMEM((4, chunk, LANES), F32)],
        compiler_params=_params(("parallel", "parallel"), VMEM_LIMIT),
    )(*args)


def _attn_bwd(qkv, bias, sink, o, do, lse, *, C, r, half, qoff, koff, voff, n_units, name):
    S = qkv.shape[0]
    tq = _tile_rows(r)
    tk = tq + 2 * half
    tiles = CHUNK // (r * tq)
    nsteps = S // CHUNK
    specs, hb = _attn_layout(S, C, r, half, qoff, koff, voff, CHUNK)
    use_sink = sink is not None

    def body(*refs):
        q_ref, kp, km, kn, vp, vm, vn, bias_ref = refs[:8]
        rest = list(refs[8:])
        sink_ref = rest.pop(0) if use_sink else None
        o_ref, do_ref, lse_ref, dq_ref, dk_hbm, dv_hbm = rest[:6]
        rest = rest[6:]
        dsink_ref = rest.pop(0) if use_sink else None
        qs, ks, vs, os_, dos, ls, dqs, acck, accv, sem = rest
        u, i = pl.program_id(0), pl.program_id(1)

        @pl.when(i == 0)
        def _():
            acck[...] = jnp.zeros_like(acck)
            accv[...] = jnp.zeros_like(accv)
            if use_sink:
                dsink_ref[...] = jnp.zeros_like(dsink_ref)

        for t in range(4):
            cols = pl.ds(t * LANES, LANES)
            qs[t] = q_ref[:, cols].astype(F32)
            os_[t] = o_ref[:, cols].astype(F32)
            dos[t] = do_ref[:, cols].astype(F32)
            ls[t] = lse_ref[:, cols]
        _stage(ks, [kp, km, kn])
        _stage(vs, [vp, vm, vn])
        lo, hi = _head_masks(tq)
        base = pl.multiple_of(i * CHUNK, CHUNK)
        if use_sink:
            sk = sink_ref[pl.ds(0, 1), :]

        def chain(n, carry):
            rho, c = n // tiles, n % tiles
            start = c * (tq * r) + rho
            if r == 1:
                start = pl.multiple_of(start, tq)
            variant = jnp.where(jnp.logical_and(i == 0, c == 0), 1, 0) + jnp.where(
                jnp.logical_and(i == nsteps - 1, c == tiles - 1), 2, 0)
            k2 = ks[_rows(start, tk, r), :].astype(BF16)
            v2 = vs[_rows(start, tk, r), :].astype(BF16)
            k2t = ks[_rows(start, tk, r), :].T.astype(BF16)
            q8 = _stack_heads([qs[t, _rows(start, tq, r), :] for t in range(4)], lo, hi, HEAD_DIM ** -0.5)
            do_tiles = [dos[t, _rows(start, tq, r), :] for t in range(4)]
            do8 = _stack_heads(do_tiles, lo, hi, 1.0)
            deltas, lses = [], []
            for t in range(4):
                prod_t = (do_tiles[t] * os_[t, _rows(start, tq, r), :]).T
                lse_t = ls[t, _rows(start, tq, r), :].T
                for a in range(2):
                    deltas.append(jnp.sum(prod_t[a * HEAD_DIM:(a + 1) * HEAD_DIM], axis=0, keepdims=True))
                    lses.append(lse_t[a * HEAD_DIM:a * HEAD_DIM + 1])
            delta8 = jnp.concatenate(deltas, axis=1)
            lse8 = jnp.concatenate(lses, axis=1)
            s = _nt(k2, q8) + bias_ref[pl.ds(pl.multiple_of(variant * tk, 8), tk), :]
            p = jnp.exp(s - lse8)
            dp = _nt(v2, do8)
            dsb = (p * (dp - delta8)).astype(BF16)
            dq8t = _nn(k2t, dsb)
            for t in range(4):
                dqs[t, _rows(start, tq, r), :] = _tile_from_columns(dq8t, t, tq) * (HEAD_DIM ** -0.5)
            arow = base + start
            if r == 1:
                arow = pl.multiple_of(arow, tq)
            acck[_rows(arow, tk, r), :] = acck[_rows(arow, tk, r), :] + _nn(dsb, q8)
            accv[_rows(arow, tk, r), :] = accv[_rows(arow, tk, r), :] + _nn(p.astype(BF16), do8)
            if use_sink:
                e = jnp.exp(sk - lse8) * delta8
                for h in range(8):
                    part = -jnp.sum(e[:, h * tq:(h + 1) * tq], axis=1, keepdims=True)
                    dsink_ref[pl.ds(h, 1), :] = dsink_ref[pl.ds(h, 1), :] + part
            return carry

        lax.fori_loop(0, r * tiles, chain, 0, unroll=2)
        for t in range(4):
            dq_ref[:, pl.ds(t * LANES, LANES)] = dqs[t].astype(BF16)

        @pl.when(i == nsteps - 1)
        def _():
            ck = pltpu.make_async_copy(acck.at[pl.ds(hb, S)], dk_hbm.at[u], sem.at[0])
            cv = pltpu.make_async_copy(accv.at[pl.ds(hb, S)], dv_hbm.at[u], sem.at[1])
            ck.start()
            cv.start()
            ck.wait()
            cv.wait()

    wide = pl.BlockSpec((CHUNK, 4 * LANES), lambda u, i: (i, u))
    hbm = pl.BlockSpec(memory_space=pl.ANY)
    in_specs = specs + [pl.BlockSpec((3 * tk, 8 * tq), lambda u, i: (u, 0))]
    args = [qkv] * 7 + [bias]
    if use_sink:
        in_specs.append(pl.BlockSpec((8, 8 * tq), lambda u, i: (u, 0)))
        args.append(sink)
    in_specs += [wide, wide, wide]
    args += [o, do, lse]
    out_shape = [jax.ShapeDtypeStruct((S, n_units * 512), BF16), jax.ShapeDtypeStruct((n_units, S, LANES), F32),
                 jax.ShapeDtypeStruct((n_units, S, LANES), F32)]
    out_specs = [wide, hbm, hbm]
    if use_sink:
        out_shape.append(jax.ShapeDtypeStruct((n_units * 8, LANES), F32))
        out_specs.append(pl.BlockSpec((8, LANES), lambda u, i: (u, 0)))
    win = hb + CHUNK + hb
    big = lambda: pltpu.VMEM((4, CHUNK, LANES), F32)
    res = pl.pallas_call(
        body, name=name, grid=(n_units, nsteps), out_shape=tuple(out_shape), in_specs=in_specs, out_specs=tuple(out_specs),
        scratch_shapes=[big(), pltpu.VMEM((win, LANES), F32), pltpu.VMEM((win, LANES), F32), big(), big(), big(), big(),
                        pltpu.VMEM((S + 2 * hb, LANES), F32), pltpu.VMEM((S + 2 * hb, LANES), F32), pltpu.SemaphoreType.DMA((2,))],
        compiler_params=_params(("arbitrary", "arbitrary"), VMEM_LIMIT),
    )(*args)
    return res[0], res[1], res[2], (res[3] if use_sink else None)


def _merge_groups(os_, lses):
    S, W = os_[0].shape
    tm = 512

    def body(o0, o1, o2, l0, l1, l2, o_ref, lse_ref):
        ls = [l0[...], l1[...], l2[...]]
        mx = jnp.maximum(jnp.maximum(ls[0], ls[1]), ls[2])
        es = [jnp.exp(l - mx) for l in ls]
        den = es[0] + es[1] + es[2]
        o = (es[0] / den) * o0[...] + (es[1] / den) * o1[...] + (es[2] / den) * o2[...]
        o_ref[...] = o.astype(BF16)
        lse_ref[...] = mx + jnp.log(den)

    row = pl.BlockSpec((tm, W), lambda i: (i, 0))
    return pl.pallas_call(
        body, name="merge_groups", grid=(S // tm,),
        out_shape=(jax.ShapeDtypeStruct((S, W), BF16), jax.ShapeDtypeStruct((S, W), F32)),
        in_specs=[row] * 6, out_specs=(row, row), compiler_params=_params(("parallel",), VMEM_LIMIT),
    )(*os_, *lses)


def _loss_head(x, target, fnw):
    S = x.shape[0]
    tm = 512

    def body(x_ref, t_ref, w_ref, dx_ref, st_ref):
        @pl.when(pl.program_id(0) == 0)
        def _():
            st_ref[...] = jnp.zeros_like(st_ref)

        xv = x_ref[...]
        rstd = lax.rsqrt(jnp.mean(xv * xv, axis=-1, keepdims=True) + RMS_EPS)
        xh = xv * rstd
        err = xh * w_ref[...] - t_ref[...]
        dy = err * (1.0 / D)
        dxh = dy * w_ref[...]
        dx_ref[...] = rstd * (dxh - xh * jnp.mean(dxh * xh, axis=-1, keepdims=True))
        st_ref[pl.ds(0, 1), :] = st_ref[pl.ds(0, 1), :] + jnp.sum(dy * xh, axis=0, keepdims=True)
        st_ref[pl.ds(1, 1), :] = st_ref[pl.ds(1, 1), :] + jnp.sum(err * err, axis=0, keepdims=True)

    row = pl.BlockSpec((tm, D), lambda i: (i, 0))
    return pl.pallas_call(
        body, name="loss_head", grid=(S // tm,),
        out_shape=(jax.ShapeDtypeStruct((S, D), F32), jax.ShapeDtypeStruct((8, D), F32)),
        in_specs=[row, row, _const_spec((1, D))], out_specs=(row, _const_spec((8, D))),
        compiler_params=_params(("arbitrary",), VMEM_LIMIT),
    )(x, target, fnw)


def _gate_bwd(dx, y, g, w, gu, *, w_is_transposed, name):
    S = dx.shape[0]
    K = w.shape[1] if w_is_transposed else w.shape[0]
    ffn = gu is not None
    tm = 256 if ffn else 512
    wout = 2 * K if ffn else K

    def body(dx_ref, y_ref, g_ref, w_ref, *rest):
        if ffn:
            gu_ref, da_ref, dyb_ref, st_ref = rest
        else:
            da_ref, dyb_ref, st_ref = rest

        @pl.when(pl.program_id(0) == 0)
        def _():
            st_ref[...] = jnp.zeros_like(st_ref)

        dxv = dx_ref[...]
        st_ref[pl.ds(0, 1), :] = st_ref[pl.ds(0, 1), :] + jnp.sum(dxv * y_ref[...].astype(F32), axis=0, keepdims=True)
        dyb = (dxv * g_ref[...]).astype(BF16)
        dyb_ref[...] = dyb
        da = _nn(dyb, w_ref[...]) if w_is_transposed else _nt(dyb, w_ref[...])
        if ffn:
            gate = gu_ref[:, pl.ds(0, K)].astype(F32)
            up = gu_ref[:, pl.ds(K, K)].astype(F32)
            sig = jax.nn.sigmoid(gate)
            da_ref[:, pl.ds(0, K)] = (da * up * (sig * (1.0 + gate * (1.0 - sig)))).astype(BF16)
            da_ref[:, pl.ds(K, K)] = (da * (gate * sig)).astype(BF16)
        else:
            da_ref[...] = da.astype(BF16)

    row = lambda w_: pl.BlockSpec((tm, w_), lambda i: (i, 0))
    in_specs = [row(D), row(D), _const_spec((1, D)), _const_spec(w.shape)]
    args = [dx, y, g, w]
    if ffn:
        in_specs.append(row(wout))
        args.append(gu)
    return pl.pallas_call(
        body, name=name, grid=(S // tm,),
        out_shape=(jax.ShapeDtypeStruct((S, wout), BF16), jax.ShapeDtypeStruct((S, D), BF16), jax.ShapeDtypeStruct((8, D), F32)),
        in_specs=in_specs, out_specs=(row(wout), row(D), _const_spec((8, D))),
        compiler_params=_params(("arbitrary",), VMEM_LIMIT),
    )(*args)


def _norm_bwd(dy, wt, x, dres, nw, sc, *, name):
    S, N = dy.shape
    tm = 256 if N > 4096 else 512

    def body(dy_ref, w_ref, x_ref, dres_ref, nw_ref, sc_ref, dx_ref, st_ref):
        @pl.when(pl.program_id(0) == 0)
        def _():
            st_ref[...] = jnp.zeros_like(st_ref)

        dh = _nn(dy_ref[...], w_ref[...])
        xv = x_ref[...]
        rstd = lax.rsqrt(jnp.mean(xv * xv, axis=-1, keepdims=True) + RMS_EPS)
        xh = xv * rstd
        nwv, scale = nw_ref[...], 1.0 + sc_ref[...]
        dxh = dh * (nwv * scale)
        dx_ref[...] = dres_ref[...] + rstd * (dxh - xh * jnp.mean(dxh * xh, axis=-1, keepdims=True))
        dhx = dh * xh
        st_ref[pl.ds(0, 1), :] = st_ref[pl.ds(0, 1), :] + jnp.sum(dh, axis=0, keepdims=True)
        st_ref[pl.ds(1, 1), :] = st_ref[pl.ds(1, 1), :] + jnp.sum(dhx * nwv, axis=0, keepdims=True)
        st_ref[pl.ds(2, 1), :] = st_ref[pl.ds(2, 1), :] + jnp.sum(dhx * scale, axis=0, keepdims=True)

    row = lambda w_: pl.BlockSpec((tm, w_), lambda i: (i, 0))
    vec = _const_spec((1, D))
    return pl.pallas_call(
        body, name=name, grid=(S // tm,),
        out_shape=(jax.ShapeDtypeStruct((S, D), F32), jax.ShapeDtypeStruct((8, D), F32)),
        in_specs=[row(N), _const_spec((N, D)), row(D), row(D), vec, vec], out_specs=(row(D), _const_spec((8, D))),
        compiler_params=_params(("arbitrary",), VMEM_LIMIT),
    )(dy, wt, x, dres, nw, sc)


def _weight_grad(a, b, *, transpose_out, name):
    S, N = b.shape
    nb = N // 2 if N > 4096 else N
    tk = 512

    def body(a_ref, b_ref, out_ref, acc):
        k = pl.program_id(1)

        @pl.when(k == 0)
        def _():
            acc[...] = jnp.zeros_like(acc)

        acc[...] += _tn(a_ref[...], b_ref[...])

        @pl.when(k == pl.num_programs(1) - 1)
        def _():
            out_ref[...] = (acc[...].T if transpose_out else acc[...]).astype(BF16)

    out_block = pl.BlockSpec((nb, D), lambda n, k: (n, 0)) if transpose_out else pl.BlockSpec((D, nb), lambda n, k: (0, n))
    return pl.pallas_call(
        body, name=name, grid=(N // nb, S // tk),
        out_shape=jax.ShapeDtypeStruct((N, D) if transpose_out else (D, N), BF16),
        in_specs=[pl.BlockSpec((tk, D), lambda n, k: (k, 0)), pl.BlockSpec((tk, nb), lambda n, k: (k, n))],
        out_specs=out_block, scratch_shapes=[pltpu.VMEM((D, nb), F32)],
        compiler_params=_params(("parallel", "arbitrary"), VMEM_LIMIT),
    )(a, b)


def _adamw(w, g, m, v):
    m = ADAM_B1 * m + (1.0 - ADAM_B1) * g
    v = ADAM_B2 * v + (1.0 - ADAM_B2) * (g * g)
    m_hat = m / (1.0 - ADAM_B1 ** ADAM_STEP)
    v_hat = v / (1.0 - ADAM_B2 ** ADAM_STEP)
    delta = -ADAM_LR * (m_hat / (jnp.sqrt(v_hat) + ADAM_EPS) + ADAM_WD * w)
    return delta, m, v


def _adam_shard(parts, own, w, m, v, name):
    R = w.shape[0]
    tr = max(t for t in (16, 32, 64, 128, 192, 256) if R % t == 0)

    def body(p_ref, o_ref, w_ref, m_ref, v_ref, g_out, d_out, m_out, v_out):
        me = _my_index()
        g = jnp.zeros((tr, D), F32)
        for j in range(N_DEV):
            g = g + jnp.where(me == j, o_ref[...], p_ref[j]).astype(F32)
        delta, mn, vn = _adamw(w_ref[...], g, m_ref[...], v_ref[...])
        g_out[...] = g
        d_out[...] = delta
        m_out[...] = mn
        v_out[...] = vn

    row = pl.BlockSpec((tr, D), lambda i: (i, 0))
    shp = jax.ShapeDtypeStruct((R, D), F32)
    return pl.pallas_call(
        body, name=name, grid=(R // tr,), out_shape=(shp,) * 4,
        in_specs=[pl.BlockSpec((N_DEV, tr, D), lambda i: (0, i, 0)), row, row, row, row], out_specs=(row,) * 4,
        compiler_params=_params(("parallel",), VMEM_LIMIT),
    )(parts, own, w, m, v)


def _adam_ada_w(cond_t, dmod, w, m, v):
    ncol = w.shape[-1]
    tr = 512

    def body(c_ref, d_ref, w_ref, m_ref, v_ref, g_out, d_out, m_out, v_out):
        g = _nn(c_ref[...], d_ref[0])
        delta, mn, vn = _adamw(w_ref[0], g, m_ref[0], v_ref[0])
        g_out[0] = g
        d_out[0] = delta
        m_out[0] = mn
        v_out[0] = vn

    blk = pl.BlockSpec((1, tr, ncol), lambda l, i: (l, i, 0))
    shp = jax.ShapeDtypeStruct(w.shape, F32)
    return pl.pallas_call(
        body, name="adam_ada_w", grid=(DEPTH, D // tr), out_shape=(shp,) * 4,
        in_specs=[pl.BlockSpec((tr, LANES), lambda l, i: (i, 0)), pl.BlockSpec((1, LANES, ncol), lambda l, i: (l, 0, 0)), blk, blk, blk],
        out_specs=(blk,) * 4, compiler_params=_params(("parallel", "parallel"), VMEM_LIMIT),
    )(cond_t, dmod, w, m, v)


TILE_ROWS = 168


def _stat_sources():
    pairs = []
    for i in range(DEPTH):
        b = 32 * i
        for q, src in enumerate((b, b + 1, b + 8, b + 16, b + 17, b + 24)):
            pairs.append((6 * i + q, src))
        pairs.append((24 + i, b + 2))
        pairs.append((32 + i, b + 18))
    pairs += [(40, 128), (41, 129)]
    return pairs


def _small_exchange(tiles, w, m, v):
    loss_row, sink_row, sink_src = 41, 48, 136

    def body(s_ref, w_ref, m_ref, v_ref, dmod_out, g_out, d_out, m_out, v_out, loss_out, all_ref, tot_ref, send_sems, recv_sems):
        me = _my_index()
        all_ref[me] = s_ref[...]
        copies = []
        for k in range(1, N_DEV):
            dev, _ = _peer(k)
            cp = pltpu.make_async_remote_copy(src_ref=s_ref, dst_ref=all_ref.at[me], send_sem=send_sems.at[k - 1],
                                              recv_sem=recv_sems.at[k - 1], device_id=dev, device_id_type=MESH)
            cp.start()
            copies.append(cp)
        for k in range(1, N_DEV):
            dev, pidx = _peer(k)
            pltpu.make_async_remote_copy(src_ref=s_ref, dst_ref=all_ref.at[pidx], send_sem=send_sems.at[k - 1],
                                         recv_sem=recv_sems.at[k - 1], device_id=dev, device_id_type=MESH).wait_recv()
        for cp in copies:
            cp.wait_send()
        tot = all_ref[0]
        for j in range(1, N_DEV):
            tot = tot + all_ref[j]
        tot_ref[...] = tot
        g_out[...] = jnp.zeros_like(g_out)
        for dst, src in _stat_sources():
            g_out[pl.ds(dst, 1), :] = tot_ref[pl.ds(src, 1), :]
            if dst < 24:
                for j in range(N_DEV):
                    dmod_out[j, pl.ds(dst, 1), :] = all_ref[j, pl.ds(src, 1), :]
        lane = lax.broadcasted_iota(jnp.int32, (1, D), 1)
        sink = jnp.zeros((1, D), F32)
        for h in range(32):
            sink = jnp.where(lane == h, tot_ref[pl.ds(sink_src + h, 1), :], sink)
        g_out[pl.ds(sink_row, 1), :] = sink
        g = g_out[...]
        delta, mn, vn = _adamw(w_ref[...], g, m_ref[...], v_ref[...])
        d_out[...] = delta
        m_out[...] = mn
        v_out[...] = vn
        loss = jnp.sum(g[loss_row:loss_row + 1, :], axis=-1, keepdims=True) * (0.5 / D)
        loss_out[...] = jnp.broadcast_to(loss, loss_out.shape)

    vm = pl.BlockSpec(memory_space=pltpu.VMEM)
    shp = jax.ShapeDtypeStruct((STAT_ROWS, D), F32)
    return pl.pallas_call(
        body, name="small_exchange",
        out_shape=(jax.ShapeDtypeStruct((N_DEV, 24, D), F32), shp, shp, shp, shp, jax.ShapeDtypeStruct((8, LANES), F32)),
        in_specs=[vm] * 4, out_specs=(vm,) * 6,
        scratch_shapes=[pltpu.VMEM((N_DEV, TILE_ROWS, D), F32), pltpu.VMEM((TILE_ROWS, D), F32),
                        pltpu.SemaphoreType.DMA((N_DEV - 1,)), pltpu.SemaphoreType.DMA((N_DEV - 1,))],
        compiler_params=_params(vmem=VMEM_LIMIT),
    )(tiles, w, m, v)


def _to_rows(name, a):
    if name in ("ffn_in", "a_in", "b_in"):
        return a.T
    if name == "b_out":
        return a.T.reshape(-1, D)
    return a


def _from_rows(name, a):
    if name in ("ffn_in", "a_in", "b_in"):
        return a.T
    if name == "b_out":
        return a.reshape(-1, 512).T
    return a


def _rows8(a):
    return jnp.pad(a, ((0, 8 - a.shape[0]), (0, 0)))


def _pack_small(ada_b, norm_mix, norm_ffn, final_norm, sink):
    sink_row = jnp.pad(sink.reshape(1, -1), ((0, 0), (0, D - sink.size)))
    return jnp.concatenate([ada_b.reshape(24, D), _rows8(norm_mix), _rows8(norm_ffn), _rows8(final_norm.reshape(1, D)),
                            _rows8(sink_row)], axis=0)


def _unpack_small(a):
    return a[0:24].reshape(4, 6 * D), a[24:28], a[32:36], a[40], a[48, :32].reshape(2, 16)


def kernel(x, c, ada_w, ada_b, norm_mix, norm_ffn, ffn_w_in, ffn_w_out, a_w_in, a_w_out, a_sink, b_w_in, b_w_out, final_norm, loss_target, m_ada_w, m_ada_b, m_norm_mix, m_norm_ffn, m_ffn_w_in, m_ffn_w_out, m_a_w_in, m_a_w_out, m_a_sink, m_b_w_in, m_b_w_out, m_final_norm, v_ada_w, v_ada_b, v_norm_mix, v_norm_ffn, v_ffn_w_in, v_ffn_w_out, v_a_w_in, v_a_w_out, v_a_sink, v_b_w_in, v_b_w_out, v_final_norm):
    S = x.shape[1]
    x0 = x.reshape(S, D)
    target = loss_target.reshape(S, D)
    me = _my_index()
    ncol = ada_w.shape[-1]

    ada_b_mine = lax.dynamic_slice_in_dim(ada_b, me * ncol, ncol, axis=1)
    cond_all, parts = _cond_exchange(jnp.broadcast_to(c.reshape(1, D), (8, D)), ada_w, ada_b_mine)
    mod = lax.dynamic_index_in_dim(parts, me, axis=2, keepdims=False)
    mod = jnp.transpose(mod, (1, 0, 2)).reshape(DEPTH, 6, 1, D)

    weights = {"ffn_in": ffn_w_in, "ffn_out": ffn_w_out, "a_in": a_w_in, "a_out": a_w_out, "b_in": b_w_in, "b_out": b_w_out}
    shard = {(n, l): _to_rows(n, weights[n][l]).astype(BF16) for n, l, _ in SEGMENTS}
    first = [sg for sg in _layer_segments(0) if not sg[0].startswith("ffn")]
    gathered0 = _all_gather_weights([shard[(n, l)] for n, l, _ in first])
    W = {(n, l): g for (n, l, _), g in zip(first, gathered0)}
    groups = [[sg for sg in _layer_segments(0) if sg[0].startswith("ffn")], [sg for i in range(1, DEPTH) for sg in _layer_segments(i)]]
    gathers, order = [], gathered0[0]
    for q, segs in enumerate(groups):
        mine = jnp.concatenate([shard[(n, l)] for n, l, _ in segs], axis=0)
        zone = lax.empty((N_DEV, mine.shape[0], D), BF16)
        gathers.append(_exchange_start([mine], zone, [mine.shape[0]], [0], False, order, "weight_gather_start_%d" % q))
        order = gathers[-1][-1]
    gather_token = order[0:1, 0:1]

    def finish_gather(q, after):
        zone = _exchange_wait(gathers[q], after, "weight_gather_wait_%d" % q)
        offs, _ = _offsets(groups[q])
        for (n, l, rows), off in zip(groups[q], offs):
            full = lax.dynamic_update_slice(zone[:, off:off + rows], shard[(n, l)][None], (me, 0, 0))
            W[(n, l)] = full.reshape(D, 512) if n == "b_out" else full.reshape(N_DEV * rows, D)

    a_slopes, b_slopes = _slopes(16), _slopes(24)
    bias_a = _alibi_bias(a_slopes, A_HALF, 1)
    bias_b = [_alibi_bias(b_slopes[8 * g:8 * g + 8], B_HALF, dil) for g, dil in enumerate(B_DILS)]
    bias_b_fwd = [_alibi_bias(b_slopes[8 * g:8 * g + 8], B_HALF, dil, max(CHUNK, TQ * dil), both=True) for g, dil in enumerate(B_DILS)]
    a_geom = dict(C=A_QKV, r=1, half=A_HALF, qoff=0, koff=1024, voff=1280, n_units=2)
    b_geom = [dict(C=B_QKV, r=dil, half=B_HALF, qoff=512 * g, koff=1536 + 128 * g, voff=1920 + 128 * g, n_units=1)
              for g, dil in enumerate(B_DILS)]

    saved = []
    xcur = x0
    for i in range(DEPTH):
        j = i // 2
        sh1, sc1, g1, sh2, sc2, g2 = [mod[i, q] for q in range(6)]
        nm, nf = norm_mix[i].reshape(1, D), norm_ffn[i].reshape(1, D)
        if i == 0:
            nm = nm + gather_token
        if i == 1:
            finish_gather(1, xcur)
        if i % 2 == 0:
            sink_rep = jnp.repeat(jnp.repeat(a_sink[j], TQ).reshape(2, 1, 8 * TQ), 8, axis=1).reshape(16, 8 * TQ)
            h1, qkv = _proj(xcur, nm, sc1, sh1, W[("a_in", j)], ffn=False, name="proj_a")
            o, lse = _attn_fwd(qkv, bias_a, sink_rep, out_dtype=BF16, name="attn_a_fwd", **a_geom)
            x1, y1 = _gated_residual(o, W[("a_out", j)], xcur, g1, w_is_transposed=False, name="out_a")
        else:
            sink_rep = None
            h1, qkv = _proj(xcur, nm, sc1, sh1, W[("b_in", j)], ffn=False, name="proj_b")
            outs = [_attn_fwd(qkv, bias_b_fwd[g], None, out_dtype=F32, name="attn_b%d_fwd" % g, **b_geom[g]) for g in range(3)]
            o, lse = _merge_groups([t[0] for t in outs], [t[1] for t in outs])
            x1, y1 = _gated_residual(o, W[("b_out", j)], xcur, g1, w_is_transposed=True, name="out_b")
        if i == 0:
            finish_gather(0, x1)
        h2, gu, act = _proj(x1, nf, sc2, sh2, W[("ffn_in", i)], ffn=True, name="ffn_in")
        x2, y2 = _gated_residual(act, W[("ffn_out", i)], x1, g2, w_is_transposed=False, name="ffn_out")
        saved.append(dict(x0=xcur, h1=h1, qkv=qkv, o=o, lse=lse, y1=y1, x1=x1, h2=h2, gu=gu, act=act, y2=y2, sink=sink_rep))
        xcur = x2

    dx, head_stats = _loss_head(xcur, target, final_norm.reshape(1, D))

    dW = {}
    stat_tiles, dsink = [None] * DEPTH, [None] * 2
    exchanges = []
    start_token = None

    def start_exchange(segs):
        offs, total = _offsets(segs)
        own = jnp.concatenate([lax.dynamic_slice_in_dim(dW[(n, l)], me * rows, rows, axis=0) for n, l, rows in segs], axis=0)
        started = _exchange_start([dW[(n, l)] for n, l, _ in segs], lax.empty((N_DEV, total, D), BF16), [sg[2] for sg in segs],
                                  offs, True, own, "grad_exchange_start_%d" % len(exchanges))
        exchanges.append((segs, started, own))
        return started[-1][0:1, 0:1]

    for i in reversed(range(DEPTH)):
        j = i // 2
        sv = saved[i]
        sh1, sc1, g1, sh2, sc2, g2 = [mod[i, q] for q in range(6)]
        if start_token is not None:
            g2 = g2 + start_token
            start_token = None
        nm, nf = norm_mix[i].reshape(1, D), norm_ffn[i].reshape(1, D)
        dgu, dy2, st_g2 = _gate_bwd(dx, sv["y2"], g2, W[("ffn_out", i)], sv["gu"], w_is_transposed=False, name="ffn_out_bwd")
        dW[("ffn_out", i)] = _weight_grad(dy2, sv["act"], transpose_out=True, name="dw_ffn_out")
        dW[("ffn_in", i)] = _weight_grad(sv["h2"], dgu, transpose_out=True, name="dw_ffn_in")
        if i == 0:
            g1 = g1 + start_exchange([sg for sg in _layer_segments(0) if sg[0].startswith("ffn")])
        dx1, st_f = _norm_bwd(dgu, W[("ffn_in", i)], sv["x1"], dx, nf, sc2, name="ffn_in_bwd")
        if i % 2 == 0:
            do, dy1, st_g1 = _gate_bwd(dx1, sv["y1"], g1, W[("a_out", j)], None, w_is_transposed=False, name="out_a_bwd")
            dW[("a_out", j)] = _weight_grad(dy1, sv["o"], transpose_out=True, name="dw_a_out")
            dq, dk, dv, ds = _attn_bwd(sv["qkv"], bias_a, sv["sink"], sv["o"], do, sv["lse"], name="attn_a_bwd", **a_geom)
            dsink[j] = ds
            dqkv = jnp.concatenate([dq, dk[0].astype(BF16), dk[1].astype(BF16), dv[0].astype(BF16), dv[1].astype(BF16)], axis=1)
            dW[("a_in", j)] = _weight_grad(sv["h1"], dqkv, transpose_out=True, name="dw_a_in")
            dx0, st_m = _norm_bwd(dqkv, W[("a_in", j)], sv["x0"], dx1, nm, sc1, name="proj_a_bwd")
        else:
            do, dy1, st_g1 = _gate_bwd(dx1, sv["y1"], g1, W[("b_out", j)], None, w_is_transposed=True, name="out_b_bwd")
            dW[("b_out", j)] = _weight_grad(dy1, sv["o"], transpose_out=False, name="dw_b_out").reshape(N_DEV * 64, D)
            gr = [_attn_bwd(sv["qkv"], bias_b[g], None, sv["o"], do, sv["lse"], name="attn_b%d_bwd" % g, **b_geom[g]) for g in range(3)]
            dqkv = jnp.concatenate([t[0] for t in gr] + [t[1][0].astype(BF16) for t in gr] + [t[2][0].astype(BF16) for t in gr], axis=1)
            dW[("b_in", j)] = _weight_grad(sv["h1"], dqkv, transpose_out=True, name="dw_b_in")
            dx0, st_m = _norm_bwd(dqkv, W[("b_in", j)], sv["x0"], dx1, nm, sc1, name="proj_b_bwd")
        stat_tiles[i] = [st_m, st_g1, st_f, st_g2]
        if i > 0:
            start_token = start_exchange(_layer_segments(i))
        else:
            start_exchange([sg for sg in _layer_segments(0) if not sg[0].startswith("ffn")])
        dx = dx0
    grad_x = dx.reshape(1, S, D)

    masters = {"ffn_in": (ffn_w_in, m_ffn_w_in, v_ffn_w_in), "ffn_out": (ffn_w_out, m_ffn_w_out, v_ffn_w_out),
               "a_in": (a_w_in, m_a_w_in, v_a_w_in), "a_out": (a_w_out, m_a_w_out, v_a_w_out),
               "b_in": (b_w_in, m_b_w_in, v_b_w_in), "b_out": (b_w_out, m_b_w_out, v_b_w_out)}
    pieces = {}
    after = dx
    for segs, started, own in exchanges:
        offs, total = _offsets(segs)
        parts_g = _exchange_wait(started, after, "grad_exchange_wait_%d" % len(pieces))
        rows_wmv = [jnp.concatenate([_to_rows(n, masters[n][q][l]) for n, l, _ in segs], axis=0) for q in range(3)]
        res_rows = _adam_shard(parts_g, own, *rows_wmv, name="adam_%d" % total)
        after = res_rows[0]
        for q, kind in enumerate(("grad", "delta", "m", "v")):
            for (n, l, rows), off in zip(segs, offs):
                pieces[(kind, n, l)] = _from_rows(n, res_rows[q][off:off + rows])
    big = {(kind, n): jnp.stack([pieces[(kind, n, l)] for l in range(4 if n.startswith("ffn") else 2)])
           for kind in ("grad", "delta", "m", "v") for n in masters}

    tiles = jnp.concatenate([t for i in range(DEPTH) for t in stat_tiles[i]] + [head_stats]
                            + [jnp.pad(ds, ((0, 0), (0, D - LANES))) for ds in dsink], axis=0)
    small = [_pack_small(*t) for t in ((ada_b, norm_mix, norm_ffn, final_norm, a_sink),
                                       (m_ada_b, m_norm_mix, m_norm_ffn, m_final_norm, m_a_sink),
                                       (v_ada_b, v_norm_mix, v_norm_ffn, v_final_norm, v_a_sink))]
    dmod_all, sg, sd, sm, sv_, loss_tile = _small_exchange(tiles, *small)
    loss = loss_tile[0, 0]
    dmod_all = dmod_all.reshape(N_DEV, DEPTH, 6 * D)
    dmod_mine = lax.dynamic_slice_in_dim(dmod_all, me * ncol, ncol, axis=2)
    dmod_pad = jnp.pad(jnp.transpose(dmod_mine, (1, 0, 2)), ((0, 0), (0, LANES - N_DEV), (0, 0))).astype(BF16)
    cond_t = jnp.pad(cond_all.T, ((0, 0), (0, LANES - N_DEV))).astype(BF16)
    ada = _adam_ada_w(cond_t, dmod_pad, ada_w, m_ada_w, v_ada_w)

    outs = [loss, grad_x]
    small_res = [_unpack_small(t) for t in (sg, sd, sm, sv_)]
    for q, kind in enumerate(("grad", "delta", "m", "v")):
        ab, nm_, nf_, fn, sk = small_res[q]
        outs += [ada[q], ab, nm_, nf_, big[(kind, "ffn_in")], big[(kind, "ffn_out")], big[(kind, "a_in")], big[(kind, "a_out")],
                 sk, big[(kind, "b_in")], big[(kind, "b_out")], fn]
    return tuple(outs)
```

```python
import functools
import math

import numpy as np
import jax
import jax.numpy as jnp
from jax import lax
from jax.experimental import pallas as pl
from jax.experimental.pallas import tpu as pltpu

D = 1024
HEAD_DIM = 64
D_FF = 2816
DEPTH = 4
N_DEV = 8
A_QKV = 1536
B_QKV = 2304
A_HALF = 128
B_HALF = 64
B_DILS = (1, 4, 16)
RMS_EPS = 1e-6
NEG = -1e30
ADAM_LR = 0.001
ADAM_B1 = 0.9
ADAM_B2 = 0.999
ADAM_EPS = 1e-08
ADAM_WD = 0.01
ADAM_STEP = 10

LANES = 128
TQ = 128
VMEM_LIMIT = 56 * 1024 * 1024
MESH = pl.DeviceIdType.MESH
F32 = jnp.float32
BF16 = jnp.bfloat16

SEGMENTS = ([("ffn_in", l, 704) for l in range(4)] + [("ffn_out", l, 352) for l in range(4)]
            + [("a_in", j, 192) for j in range(2)] + [("a_out", j, 128) for j in range(2)]
            + [("b_in", j, 288) for j in range(2)] + [("b_out", j, 64) for j in range(2)])
def _layer_segments(i):
    mixer = "a" if i % 2 == 0 else "b"
    return [s for s in SEGMENTS if (s[0].startswith("ffn") and s[1] == i) or (s[0].startswith(mixer + "_") and s[1] == i // 2)]


def _offsets(segs):
    rows = [s[2] for s in segs]
    return [sum(rows[:k]) for k in range(len(rows))], sum(rows)
STAT_ROWS = 56


def _nn(a, b):
    return jnp.dot(a, b, preferred_element_type=F32)


def _nt(a, b):
    return lax.dot_general(a, b, (((1,), (1,)), ((), ())), preferred_element_type=F32)


def _tn(a, b):
    return lax.dot_general(a, b, (((0,), (0,)), ((), ())), preferred_element_type=F32)


def _params(dims=None, vmem=None):
    kw = {}
    if dims is not None:
        kw["dimension_semantics"] = dims
    if vmem is not None:
        kw["vmem_limit_bytes"] = vmem
    return pltpu.CompilerParams(**kw)


def _my_index():
    return 4 * lax.axis_index("x") + 2 * lax.axis_index("y") + lax.axis_index("c")


def _peer(k):
    x, y, c = lax.axis_index("x"), lax.axis_index("y"), lax.axis_index("c")
    px, py, pc = x ^ ((k >> 2) & 1), y ^ ((k >> 1) & 1), c ^ (k & 1)
    return (px, py, pc), 4 * px + 2 * py + pc


def _const_spec(shape):
    nd = len(shape)
    return pl.BlockSpec(shape, lambda *_: (0,) * nd)


def _cond_exchange(c_tile, ada_w, ada_b_mine):
    ncol = ada_w.shape[-1]

    def body(c_ref, w_ref, b_ref, cond_ref, parts_ref, call_ref, mine_ref, send_sems, recv_sems):
        me = _my_index()
        call_ref[me] = c_ref[...]
        copies = []
        for k in range(1, N_DEV):
            dev, _ = _peer(k)
            cp = pltpu.make_async_remote_copy(src_ref=c_ref, dst_ref=call_ref.at[me], send_sem=send_sems.at[0, k - 1],
                                              recv_sem=recv_sems.at[0, k - 1], device_id=dev, device_id_type=MESH)
            cp.start()
            copies.append(cp)
        for k in range(1, N_DEV):
            _, pidx = _peer(k)
            pltpu.make_async_remote_copy(src_ref=c_ref, dst_ref=call_ref.at[pidx], send_sem=send_sems.at[0, k - 1],
                                         recv_sem=recv_sems.at[0, k - 1], device_id=_peer(k)[0], device_id_type=MESH).wait_recv()
        for cp in copies:
            cp.wait_send()
        row = lax.broadcasted_iota(jnp.int32, (N_DEV, D), 0)
        cmat = jnp.zeros((N_DEV, D), F32)
        for j in range(N_DEV):
            cmat = jnp.where(row == j, call_ref[j], cmat)
        cond = cmat * jax.nn.sigmoid(cmat)
        cond_ref[...] = cond
        cb = cond.astype(BF16)
        for l in range(DEPTH):
            mine_ref[l] = _nn(cb, w_ref[l].astype(BF16)) + b_ref[pl.ds(l, 1), :]
        parts_ref[me] = mine_ref[...]
        copies = []
        for k in range(1, N_DEV):
            dev, _ = _peer(k)
            cp = pltpu.make_async_remote_copy(src_ref=mine_ref, dst_ref=parts_ref.at[me], send_sem=send_sems.at[1, k - 1],
                                              recv_sem=recv_sems.at[1, k - 1], device_id=dev, device_id_type=MESH)
            cp.start()
            copies.append(cp)
        for k in range(1, N_DEV):
            dev, pidx = _peer(k)
            pltpu.make_async_remote_copy(src_ref=mine_ref, dst_ref=parts_ref.at[pidx], send_sem=send_sems.at[1, k - 1],
                                         recv_sem=recv_sems.at[1, k - 1], device_id=dev, device_id_type=MESH).wait_recv()
        for cp in copies:
            cp.wait_send()

    vm = pl.BlockSpec(memory_space=pltpu.VMEM)
    return pl.pallas_call(
        body, name="cond_exchange",
        out_shape=(jax.ShapeDtypeStruct((N_DEV, D), F32), jax.ShapeDtypeStruct((N_DEV, DEPTH, N_DEV, ncol), F32)),
        in_specs=[vm, vm, vm], out_specs=(vm, vm),
        scratch_shapes=[pltpu.VMEM((N_DEV, N_DEV, D), F32), pltpu.VMEM((DEPTH, N_DEV, ncol), F32),
                        pltpu.SemaphoreType.DMA((2, N_DEV - 1)), pltpu.SemaphoreType.DMA((2, N_DEV - 1))],
        compiler_params=_params(vmem=VMEM_LIMIT),
    )(c_tile, ada_w, ada_b_mine)[:2]


def _all_gather_weights(shards):
    n = len(shards)
    big = max(range(n), key=lambda s: shards[s].shape[0])
    total = sum(sh.shape[0] for sh in shards)
    assert N_DEV * shards[big].shape[0] >= total

    def body(*refs):
        ins, outs = refs[:n], refs[n:2 * n]
        local_sems, send_sems, recv_sems = refs[2 * n:]
        me = _my_index()
        local = []
        for s in range(n):
            rows = ins[s].shape[0]
            cp = pltpu.make_async_copy(ins[s], outs[s].at[pl.ds(me * rows, rows)], local_sems.at[s])
            cp.start()
            local.append(cp)
        for k in range(1, N_DEV):
            dev, _ = _peer(k)
            for s in range(n):
                rows = ins[s].shape[0]
                pltpu.make_async_remote_copy(src_ref=ins[s], dst_ref=outs[s].at[pl.ds(me * rows, rows)],
                                             send_sem=send_sems.at[k - 1], recv_sem=recv_sems.at[k - 1],
                                             device_id=dev, device_id_type=MESH).start()
        whole = outs[big].at[pl.ds(0, total)]
        for k in range(1, N_DEV):
            dev, _ = _peer(k)
            w = pltpu.make_async_remote_copy(src_ref=whole, dst_ref=whole, send_sem=send_sems.at[k - 1],
                                             recv_sem=recv_sems.at[k - 1], device_id=dev, device_id_type=MESH)
            w.wait_send()
            w.wait_recv()
        for cp in local:
            cp.wait()

    hbm = pl.BlockSpec(memory_space=pl.ANY)
    return pl.pallas_call(
        body, name="weight_all_gather",
        out_shape=tuple(jax.ShapeDtypeStruct((N_DEV * s.shape[0], D), s.dtype) for s in shards),
        in_specs=[hbm] * n, out_specs=tuple([hbm] * n),
        scratch_shapes=[pltpu.SemaphoreType.DMA((n,)), pltpu.SemaphoreType.DMA((N_DEV - 1,)),
                        pltpu.SemaphoreType.DMA((N_DEV - 1,))],
    )(*shards)


HBM = pl.BlockSpec(memory_space=pltpu.HBM)
SEM = pl.BlockSpec(memory_space=pltpu.SEMAPHORE)
EFFECT = pltpu.SideEffectType.DATAFLOW_SIDE_EFFECTING


def _exchange_start(srcs, landing, rows, offs, to_peer_rows, after, name):
    n = len(srcs)

    def body(*refs):
        src_refs, land_ref = refs[:n], refs[n]
        send_sems, recv_sems = refs[n + 2], refs[n + 3]
        token = refs[-1]
        me = _my_index()
        for k in range(1, N_DEV):
            dev, pidx = _peer(k)
            for q in range(n):
                src = src_refs[q].at[pl.ds(pidx * rows[q], rows[q])] if to_peer_rows else src_refs[q]
                pltpu.make_async_remote_copy(src_ref=src, dst_ref=land_ref.at[me, pl.ds(offs[q], rows[q])],
                                             send_sem=send_sems.at[k - 1], recv_sem=recv_sems.at[k - 1],
                                             device_id=dev, device_id_type=MESH).start()
        token[...] = jnp.zeros_like(token)

    arrays = list(srcs) + [landing]
    return pl.pallas_call(
        body, name=name,
        out_shape=(pltpu.SemaphoreType.DMA((N_DEV - 1,)), pltpu.SemaphoreType.DMA((N_DEV - 1,)),
                   *[pltpu.HBM(a.shape, a.dtype) for a in arrays], jax.ShapeDtypeStruct((8, LANES), F32)),
        in_specs=[HBM] * (n + 1) + [pl.BlockSpec(memory_space=pl.ANY)],
        out_specs=(SEM, SEM, *[HBM] * (n + 1), pl.BlockSpec(memory_space=pltpu.VMEM)),
        input_output_aliases={q: 2 + q for q in range(n + 1)},
        compiler_params=pltpu.CompilerParams(has_side_effects=EFFECT),
    )(*[pltpu.with_memory_space_constraint(a, pltpu.HBM) for a in arrays], after)


def _exchange_wait(started, after, name):
    send_sems, recv_sems = started[0], started[1]
    arrays = list(started[2:-1])
    n1 = len(arrays)

    def body(*refs):
        land_ref = refs[n1 - 1]
        sends, recvs = refs[n1], refs[n1 + 1]
        for k in range(1, N_DEV):
            dev, _ = _peer(k)
            w = pltpu.make_async_remote_copy(src_ref=land_ref.at[0], dst_ref=land_ref.at[0], send_sem=sends.at[k - 1],
                                             recv_sem=recvs.at[k - 1], device_id=dev, device_id_type=MESH)
            w.wait_send()
            w.wait_recv()

    return pl.pallas_call(
        body, name=name, out_shape=tuple(pltpu.HBM(a.shape, a.dtype) for a in arrays),
        in_specs=[HBM] * n1 + [SEM, SEM, pl.BlockSpec(memory_space=pl.ANY)], out_specs=tuple([HBM] * n1),
        input_output_aliases={q: q for q in range(n1)},
        compiler_params=pltpu.CompilerParams(has_side_effects=EFFECT),
    )(*arrays, send_sems, recv_sems, after)[n1 - 1]


def _norm_mod(x, nw, sc, sh):
    ms = jnp.mean(x * x, axis=-1, keepdims=True)
    xh = x * lax.rsqrt(ms + RMS_EPS)
    return xh, (xh * nw) * (1.0 + sc) + sh


def _proj(x, nw, sc, sh, wt, *, ffn, name):
    S, N = x.shape[0], wt.shape[0]
    tm = 256 if ffn else 512

    def body(x_ref, nw_ref, sc_ref, sh_ref, w_ref, h_ref, out_ref, *act_ref):
        _, h = _norm_mod(x_ref[...], nw_ref[...], sc_ref[...], sh_ref[...])
        hb = h.astype(BF16)
        h_ref[...] = hb
        if ffn:
            gate = _nt(hb, w_ref[pl.ds(0, D_FF), :])
            up = _nt(hb, w_ref[pl.ds(D_FF, D_FF), :])
            sig = jax.nn.sigmoid(gate)
            silu = gate * sig
            out_ref[:, pl.ds(0, D_FF)] = (up * (sig * (1.0 + gate * (1.0 - sig)))).astype(BF16)
            out_ref[:, pl.ds(D_FF, D_FF)] = silu.astype(BF16)
            act_ref[0][...] = (silu * up).astype(BF16)
        else:
            out_ref[...] = _nt(hb, w_ref[...]).astype(BF16)

    row = lambda w: pl.BlockSpec((tm, w), lambda i: (i, 0))
    out_shape = [jax.ShapeDtypeStruct((S, D), BF16), jax.ShapeDtypeStruct((S, N), BF16)]
    out_specs = [row(D), row(N)]
    if ffn:
        out_shape.append(jax.ShapeDtypeStruct((S, D_FF), BF16))
        out_specs.append(row(D_FF))
    vec = _const_spec((1, D))
    return pl.pallas_call(
        body, name=name, grid=(S // tm,), out_shape=tuple(out_shape),
        in_specs=[row(D), vec, vec, vec, _const_spec((N, D))], out_specs=tuple(out_specs),
        compiler_params=_params(("parallel",), VMEM_LIMIT),
    )(x, nw, sc, sh, wt)


def _gated_residual(a, w, x, g, *, w_is_transposed, name):
    S, K = a.shape
    tm = 512

    def body(a_ref, w_ref, x_ref, g_ref, xo_ref, y_ref):
        y = _nt(a_ref[...], w_ref[...]) if w_is_transposed else _nn(a_ref[...], w_ref[...])
        y_ref[...] = y.astype(BF16)
        xo_ref[...] = x_ref[...] + g_ref[...] * y

    row = lambda w_: pl.BlockSpec((tm, w_), lambda i: (i, 0))
    return pl.pallas_call(
        body, name=name, grid=(S // tm,),
        out_shape=(jax.ShapeDtypeStruct((S, D), F32), jax.ShapeDtypeStruct((S, D), BF16)),
        in_specs=[row(K), _const_spec(w.shape), row(D), _const_spec((1, D))], out_specs=(row(D), row(D)),
        compiler_params=_params(("parallel",), VMEM_LIMIT),
    )(a, w, x, g)


CHUNK = 1024


def _tile_rows(r, chunk=CHUNK):
    return min(TQ, chunk // r)


def _alibi_bias(slopes, half, dil, chunk=CHUNK, both=False):
    tq = _tile_rows(dil, chunk)
    tk = tq + 2 * half
    rel = np.arange(tk)[:, None] - half - np.arange(tq)[None, :]
    band = np.abs(rel) <= half
    dist = (dil * np.abs(rel)).astype(np.float32)
    tabs = [np.where(band, -np.float32(s) * dist, np.float32(NEG)).astype(np.float32) for s in slopes]
    out = []
    for u in range(0, len(tabs), 8):
        tab = np.concatenate(tabs[u:u + 8], axis=1)
        first, last = tab.copy(), tab.copy()
        first[:half] = NEG
        last[tk - half:] = NEG
        out += [tab, first, last]
        if both:
            last = last.copy()
            last[:half] = NEG
            out.append(last)
    return jnp.asarray(np.concatenate(out, axis=0))


def _slopes(n):
    return (2.0 ** (-8.0 * np.arange(1, n + 1) / n)).astype(np.float32)


def _head_masks(tq):
    lane = lax.broadcasted_iota(jnp.int32, (tq, LANES), 1)
    lo = lane < HEAD_DIM
    return lo, jnp.logical_not(lo)


def _stack_heads(tiles, lo, hi, scale):
    blocks = []
    for t in range(4):
        xf = tiles[t] if scale == 1.0 else tiles[t] * scale
        for a in range(2):
            xm = jnp.where(lo if a == 0 else hi, xf, 0.0)
            if a != t // 2:
                xm = pltpu.roll(xm, HEAD_DIM, 1)
            blocks.append(xm.astype(BF16))
    return jnp.concatenate(blocks, axis=0)


def _tile_from_columns(x8t, t, tq):
    r0 = HEAD_DIM * (t // 2)
    top = x8t[r0:r0 + HEAD_DIM, 2 * t * tq:(2 * t + 1) * tq]
    bot = x8t[r0:r0 + HEAD_DIM, (2 * t + 1) * tq:(2 * t + 2) * tq]
    return jnp.concatenate([top, bot], axis=0).T


def _attn_layout(S, C, r, half, qoff, koff, voff, chunk):
    hb = half * r
    per = chunk // hb
    nhb = S // hb
    main = lambda off: pl.BlockSpec((chunk, LANES), lambda u, i: (i, off // LANES + u))
    prev = lambda off: pl.BlockSpec((hb, LANES), lambda u, i: (jnp.maximum(i * per - 1, 0), off // LANES + u))
    nxt = lambda off: pl.BlockSpec((hb, LANES), lambda u, i: (jnp.minimum((i + 1) * per, nhb - 1), off // LANES + u))
    specs = [pl.BlockSpec((chunk, 4 * LANES), lambda u, i: (i, qoff // (4 * LANES) + u))]
    specs += [prev(koff), main(koff), nxt(koff), prev(voff), main(voff), nxt(voff)]
    return specs, hb


def _stage(dst, srcs):
    row = 0
    for src in srcs:
        n = src.shape[0]
        dst[pl.ds(row, n), :] = src[...].astype(F32)
        row += n


def _rows(start, n, r):
    return pl.ds(start, n, stride=r) if r > 1 else pl.ds(start, n)


def _attn_fwd(qkv, bias, sink, *, C, r, half, qoff, koff, voff, n_units, out_dtype, name):
    S = qkv.shape[0]
    chunk = max(CHUNK, TQ * r)
    tq = _tile_rows(r, chunk)
    tk = tq + 2 * half
    tiles = chunk // (r * tq)
    nsteps = S // chunk
    specs, hb = _attn_layout(S, C, r, half, qoff, koff, voff, chunk)
    use_sink = sink is not None

    def body(*refs):
        q_ref, kp, km, kn, vp, vm, vn, bias_ref = refs[:8]
        rest = list(refs[8:])
        sink_ref = rest.pop(0) if use_sink else None
        o_ref, lse_ref, qs, ks, vs, os_, ls = rest
        i = pl.program_id(1)
        for t in range(4):
            qs[t] = q_ref[:, pl.ds(t * LANES, LANES)].astype(F32)
        _stage(ks, [kp, km, kn])
        _stage(vs, [vp, vm, vn])
        lo, hi = _head_masks(tq)
        ones = jnp.ones((16, tk), BF16)
        if use_sink:
            sk = sink_ref[pl.ds(0, 1), :]

        def chain(n, carry):
            rho, c = n // tiles, n % tiles
            start = c * (tq * r) + rho
            if r == 1:
                start = pl.multiple_of(start, tq)
            variant = jnp.where(jnp.logical_and(i == 0, c == 0), 1, 0) + jnp.where(
                jnp.logical_and(i == nsteps - 1, c == tiles - 1), 2, 0)
            k2 = ks[_rows(start, tk, r), :].astype(BF16)
            v2t = jnp.concatenate([vs[_rows(start, tk, r), :].T.astype(BF16), ones], axis=0)
            q8 = _stack_heads([qs[t, _rows(start, tq, r), :] for t in range(4)], lo, hi, HEAD_DIM ** -0.5)
            s = _nt(k2, q8) + bias_ref[pl.ds(pl.multiple_of(variant * tk, 8), tk), :]
            m = jnp.max(s, axis=0, keepdims=True)
            if use_sink:
                m = jnp.maximum(m, sk)
            pv = _nn(v2t, jnp.exp(s - m).astype(BF16))
            l = pv[LANES:LANES + 1]
            if use_sink:
                l = l + jnp.exp(sk - m)
            o8t = pv[:LANES] / l
            lse8 = jnp.broadcast_to(m + jnp.log(l), (LANES, 8 * tq))
            for t in range(4):
                os_[t, _rows(start, tq, r), :] = _tile_from_columns(o8t, t, tq)
                ls[t, _rows(start, tq, r), :] = _tile_from_columns(lse8, t, tq)
            return carry

        lax.fori_loop(0, r * tiles, chain, 0, unroll=4)
        for t in range(4):
            o_ref[:, pl.ds(t * LANES, LANES)] = os_[t].astype(out_dtype)
            lse_ref[:, pl.ds(t * LANES, LANES)] = ls[t]

    in_specs = specs + [pl.BlockSpec((bias.shape[0] // n_units, 8 * tq), lambda u, i: (u, 0))]
    args = [qkv] * 7 + [bias]
    if use_sink:
        in_specs.append(pl.BlockSpec((8, 8 * tq), lambda u, i: (u, 0)))
        args.append(sink)
    wide = pl.BlockSpec((chunk, 4 * LANES), lambda u, i: (i, u))
    win = hb + chunk + hb
    return pl.pallas_call(
        body, name=name, grid=(n_units, nsteps),
        out_shape=(jax.ShapeDtypeStruct((S, n_units * 512), out_dtype), jax.ShapeDtypeStruct((S, n_units * 512), F32)),
        in_specs=in_specs, out_specs=(wide, wide),
        scratch_shapes=[pltpu.VMEM((4, chunk, LANES), F32), pltpu.VMEM((win, LANES), F32), pltpu.VMEM((win, LANES), F32),
                        pltpu.VMEM((4, chunk, LANES), F32), pltpu.VMEM((4, chunk, LANES), F32)],
        compiler_params=_params(("parallel", "parallel"), VMEM_LIMIT),
    )(*args)


def _attn_bwd(qkv, bias, sink, o, do, lse, *, C, r, half, qoff, koff, voff, n_units, name):
    S = qkv.shape[0]
    tq = _tile_rows(r)
    tk = tq + 2 * half
    tiles = CHUNK // (r * tq)
    nsteps = S // CHUNK
    specs, hb = _attn_layout(S, C, r, half, qoff, koff, voff, CHUNK)
    use_sink = sink is not None

    def body(*refs):
        q_ref, kp, km, kn, vp, vm, vn, bias_ref = refs[:8]
        rest = list(refs[8:])
        sink_ref = rest.pop(0) if use_sink else None
        o_ref, do_ref, lse_ref, dq_ref, dk_hbm, dv_hbm = rest[:6]
        rest = rest[6:]
        dsink_ref = rest.pop(0) if use_sink else None
        qs, ks, vs, os_, dos, ls, dqs, acck, accv, sem = rest
        u, i = pl.program_id(0), pl.program_id(1)

        @pl.when(i == 0)
        def _():
            acck[...] = jnp.zeros_like(acck)
            accv[...] = jnp.zeros_like(accv)
            if use_sink:
                dsink_ref[...] = jnp.zeros_like(dsink_ref)

        for t in range(4):
            cols = pl.ds(t * LANES, LANES)
            qs[t] = q_ref[:, cols].astype(F32)
            os_[t] = o_ref[:, cols].astype(F32)
            dos[t] = do_ref[:, cols].astype(F32)
            ls[t] = lse_ref[:, cols]
        _stage(ks, [kp, km, kn])
        _stage(vs, [vp, vm, vn])
        lo, hi = _head_masks(tq)
        base = pl.multiple_of(i * CHUNK, CHUNK)
        if use_sink:
            sk = sink_ref[pl.ds(0, 1), :]

        def chain(n, carry):
            rho, c = n // tiles, n % tiles
            start = c * (tq * r) + rho
            if r == 1:
                start = pl.multiple_of(start, tq)
            variant = jnp.where(jnp.logical_and(i == 0, c == 0), 1, 0) + jnp.where(
                jnp.logical_and(i == nsteps - 1, c == tiles - 1), 2, 0)
            k2 = ks[_rows(start, tk, r), :].astype(BF16)
            v2 = vs[_rows(start, tk, r), :].astype(BF16)
            k2t = ks[_rows(start, tk, r), :].T.astype(BF16)
            q8 = _stack_heads([qs[t, _rows(start, tq, r), :] for t in range(4)], lo, hi, HEAD_DIM ** -0.5)
            do_tiles = [dos[t, _rows(start, tq, r), :] for t in range(4)]
            do8 = _stack_heads(do_tiles, lo, hi, 1.0)
            deltas, lses = [], []
            for t in range(4):
                prod_t = (do_tiles[t] * os_[t, _rows(start, tq, r), :]).T
                lse_t = ls[t, _rows(start, tq, r), :].T
                for a in range(2):
                    deltas.append(jnp.sum(prod_t[a * HEAD_DIM:(a + 1) * HEAD_DIM], axis=0, keepdims=True))
                    lses.append(lse_t[a * HEAD_DIM:a * HEAD_DIM + 1])
            delta8 = jnp.concatenate(deltas, axis=1)
            lse8 = jnp.concatenate(lses, axis=1)
            s = _nt(k2, q8) + bias_ref[pl.ds(pl.multiple_of(variant * tk, 8), tk), :]
            p = jnp.exp(s - lse8)
            dp = _nt(v2, do8)
            dsb = (p * (dp - delta8)).astype(BF16)
            dq8t = _nn(k2t, dsb)
            for t in range(4):
                dqs[t, _rows(start, tq, r), :] = _tile_from_columns(dq8t, t, tq) * (HEAD_DIM ** -0.5)
            arow = base + start
            if r == 1:
                arow = pl.multiple_of(arow, tq)
            acck[_rows(arow, tk, r), :] = acck[_rows(arow, tk, r), :] + _nn(dsb, q8)
            accv[_rows(arow, tk, r), :] = accv[_rows(arow, tk, r), :] + _nn(p.astype(BF16), do8)
            if use_sink:
                e = jnp.exp(sk - lse8) * delta8
                for h in range(8):
                    part = -jnp.sum(e[:, h * tq:(h + 1) * tq], axis=1, keepdims=True)
                    dsink_ref[pl.ds(h, 1), :] = dsink_ref[pl.ds(h, 1), :] + part
            return carry

        lax.fori_loop(0, r * tiles, chain, 0, unroll=2)
        for t in range(4):
            dq_ref[:, pl.ds(t * LANES, LANES)] = dqs[t].astype(BF16)

        @pl.when(i == nsteps - 1)
        def _():
            ck = pltpu.make_async_copy(acck.at[pl.ds(hb, S)], dk_hbm.at[u], sem.at[0])
            cv = pltpu.make_async_copy(accv.at[pl.ds(hb, S)], dv_hbm.at[u], sem.at[1])
            ck.start()
            cv.start()
            ck.wait()
            cv.wait()

    wide = pl.BlockSpec((CHUNK, 4 * LANES), lambda u, i: (i, u))
    hbm = pl.BlockSpec(memory_space=pl.ANY)
    in_specs = specs + [pl.BlockSpec((3 * tk, 8 * tq), lambda u, i: (u, 0))]
    args = [qkv] * 7 + [bias]
    if use_sink:
        in_specs.append(pl.BlockSpec((8, 8 * tq), lambda u, i: (u, 0)))
        args.append(sink)
    in_specs += [wide, wide, wide]
    args += [o, do, lse]
    out_shape = [jax.ShapeDtypeStruct((S, n_units * 512), BF16), jax.ShapeDtypeStruct((n_units, S, LANES), F32),
                 jax.ShapeDtypeStruct((n_units, S, LANES), F32)]
    out_specs = [wide, hbm, hbm]
    if use_sink:
        out_shape.append(jax.ShapeDtypeStruct((n_units * 8, LANES), F32))
        out_specs.append(pl.BlockSpec((8, LANES), lambda u, i: (u, 0)))
    win = hb + CHUNK + hb
    big = lambda: pltpu.VMEM((4, CHUNK, LANES), F32)
    res = pl.pallas_call(
        body, name=name, grid=(n_units, nsteps), out_shape=tuple(out_shape), in_specs=in_specs, out_specs=tuple(out_specs),
        scratch_shapes=[big(), pltpu.VMEM((win, LANES), F32), pltpu.VMEM((win, LANES), F32), big(), big(), big(), big(),
                        pltpu.VMEM((S + 2 * hb, LANES), F32), pltpu.VMEM((S + 2 * hb, LANES), F32), pltpu.SemaphoreType.DMA((2,))],
        compiler_params=_params(("arbitrary", "arbitrary"), VMEM_LIMIT),
    )(*args)
    return res[0], res[1], res[2], (res[3] if use_sink else None)


def _merge_groups(os_, lses):
    S, W = os_[0].shape
    tm = 512

    def body(o0, o1, o2, l0, l1, l2, o_ref, lse_ref):
        ls = [l0[...], l1[...], l2[...]]
        mx = jnp.maximum(jnp.maximum(ls[0], ls[1]), ls[2])
        es = [jnp.exp(l - mx) for l in ls]
        den = es[0] + es[1] + es[2]
        o = (es[0] / den) * o0[...] + (es[1] / den) * o1[...] + (es[2] / den) * o2[...]
        o_ref[...] = o.astype(BF16)
        lse_ref[...] = mx + jnp.log(den)

    row = pl.BlockSpec((tm, W), lambda i: (i, 0))
    return pl.pallas_call(
        body, name="merge_groups", grid=(S // tm,),
        out_shape=(jax.ShapeDtypeStruct((S, W), BF16), jax.ShapeDtypeStruct((S, W), F32)),
        in_specs=[row] * 6, out_specs=(row, row), compiler_params=_params(("parallel",), VMEM_LIMIT),
    )(*os_, *lses)


def _loss_head(x, target, fnw):
    S = x.shape[0]
    tm = 512

    def body(x_ref, t_ref, w_ref, dx_ref, st_ref):
        @pl.when(pl.program_id(0) == 0)
        def _():
            st_ref[...] = jnp.zeros_like(st_ref)

        xv = x_ref[...]
        rstd = lax.rsqrt(jnp.mean(xv * xv, axis=-1, keepdims=True) + RMS_EPS)
        xh = xv * rstd
        err = xh * w_ref[...] - t_ref[...]
        dy = err * (1.0 / D)
        dxh = dy * w_ref[...]
        dx_ref[...] = rstd * (dxh - xh * jnp.mean(dxh * xh, axis=-1, keepdims=True))
        st_ref[pl.ds(0, 1), :] = st_ref[pl.ds(0, 1), :] + jnp.sum(dy * xh, axis=0, keepdims=True)
        st_ref[pl.ds(1, 1), :] = st_ref[pl.ds(1, 1), :] + jnp.sum(err * err, axis=0, keepdims=True)

    row = pl.BlockSpec((tm, D), lambda i: (i, 0))
    return pl.pallas_call(
        body, name="loss_head", grid=(S // tm,),
        out_shape=(jax.ShapeDtypeStruct((S, D), F32), jax.ShapeDtypeStruct((8, D), F32)),
        in_specs=[row, row, _const_spec((1, D))], out_specs=(row, _const_spec((8, D))),
        compiler_params=_params(("arbitrary",), VMEM_LIMIT),
    )(x, target, fnw)


def _gate_bwd(dx, y, g, w, gu, *, w_is_transposed, name):
    S = dx.shape[0]
    K = w.shape[1] if w_is_transposed else w.shape[0]
    ffn = gu is not None
    tm = 256 if ffn else 512
    wout = 2 * K if ffn else K

    def body(dx_ref, y_ref, g_ref, w_ref, *rest):
        if ffn:
            gu_ref, da_ref, dyb_ref, st_ref = rest
        else:
            da_ref, dyb_ref, st_ref = rest

        @pl.when(pl.program_id(0) == 0)
        def _():
            st_ref[...] = jnp.zeros_like(st_ref)

        dxv = dx_ref[...]
        st_ref[pl.ds(0, 1), :] = st_ref[pl.ds(0, 1), :] + jnp.sum(dxv * y_ref[...].astype(F32), axis=0, keepdims=True)
        dyb = (dxv * g_ref[...]).astype(BF16)
        dyb_ref[...] = dyb
        da = _nn(dyb, w_ref[...]) if w_is_transposed else _nt(dyb, w_ref[...])
        if ffn:
            da_ref[:, pl.ds(0, K)] = (da * gu_ref[:, pl.ds(0, K)].astype(F32)).astype(BF16)
            da_ref[:, pl.ds(K, K)] = (da * gu_ref[:, pl.ds(K, K)].astype(F32)).astype(BF16)
        else:
            da_ref[...] = da.astype(BF16)

    row = lambda w_: pl.BlockSpec((tm, w_), lambda i: (i, 0))
    in_specs = [row(D), row(D), _const_spec((1, D)), _const_spec(w.shape)]
    args = [dx, y, g, w]
    if ffn:
        in_specs.append(row(wout))
        args.append(gu)
    return pl.pallas_call(
        body, name=name, grid=(S // tm,),
        out_shape=(jax.ShapeDtypeStruct((S, wout), BF16), jax.ShapeDtypeStruct((S, D), BF16), jax.ShapeDtypeStruct((8, D), F32)),
        in_specs=in_specs, out_specs=(row(wout), row(D), _const_spec((8, D))),
        compiler_params=_params(("arbitrary",), VMEM_LIMIT),
    )(*args)


def _norm_bwd(dy, wt, x, dres, nw, sc, *, name):
    S, N = dy.shape
    tm = 256 if N > 4096 else 512

    def body(dy_ref, w_ref, x_ref, dres_ref, nw_ref, sc_ref, dx_ref, st_ref):
        @pl.when(pl.program_id(0) == 0)
        def _():
            st_ref[...] = jnp.zeros_like(st_ref)

        dh = _nn(dy_ref[...], w_ref[...])
        xv = x_ref[...]
        rstd = lax.rsqrt(jnp.mean(xv * xv, axis=-1, keepdims=True) + RMS_EPS)
        xh = xv * rstd
        nwv, scale = nw_ref[...], 1.0 + sc_ref[...]
        dxh = dh * (nwv * scale)
        dx_ref[...] = dres_ref[...] + rstd * (dxh - xh * jnp.mean(dxh * xh, axis=-1, keepdims=True))
        dhx = dh * xh
        st_ref[pl.ds(0, 1), :] = st_ref[pl.ds(0, 1), :] + jnp.sum(dh, axis=0, keepdims=True)
        st_ref[pl.ds(1, 1), :] = st_ref[pl.ds(1, 1), :] + jnp.sum(dhx * nwv, axis=0, keepdims=True)
        st_ref[pl.ds(2, 1), :] = st_ref[pl.ds(2, 1), :] + jnp.sum(dhx * scale, axis=0, keepdims=True)

    row = lambda w_: pl.BlockSpec((tm, w_), lambda i: (i, 0))
    vec = _const_spec((1, D))
    return pl.pallas_call(
        body, name=name, grid=(S // tm,),
        out_shape=(jax.ShapeDtypeStruct((S, D), F32), jax.ShapeDtypeStruct((8, D), F32)),
        in_specs=[row(N), _const_spec((N, D)), row(D), row(D), vec, vec], out_specs=(row(D), _const_spec((8, D))),
        compiler_params=_params(("arbitrary",), VMEM_LIMIT),
    )(dy, wt, x, dres, nw, sc)


def _weight_grad(a, b, *, transpose_out, name):
    S, N = b.shape
    nb = N // 2 if N > 4096 else N
    tk = 512

    def body(a_ref, b_ref, out_ref, acc):
        k = pl.program_id(1)

        @pl.when(k == 0)
        def _():
            acc[...] = jnp.zeros_like(acc)

        acc[...] += _tn(a_ref[...], b_ref[...])

        @pl.when(k == pl.num_programs(1) - 1)
        def _():
            out_ref[...] = (acc[...].T if transpose_out else acc[...]).astype(BF16)

    out_block = pl.BlockSpec((nb, D), lambda n, k: (n, 0)) if transpose_out else pl.BlockSpec((D, nb), lambda n, k: (0, n))
    return pl.pallas_call(
        body, name=name, grid=(N // nb, S // tk),
        out_shape=jax.ShapeDtypeStruct((N, D) if transpose_out else (D, N), BF16),
        in_specs=[pl.BlockSpec((tk, D), lambda n, k: (k, 0)), pl.BlockSpec((tk, nb), lambda n, k: (k, n))],
        out_specs=out_block, scratch_shapes=[pltpu.VMEM((D, nb), F32)],
        compiler_params=_params(("parallel", "arbitrary"), VMEM_LIMIT),
    )(a, b)


def _adamw(w, g, m, v):
    m = ADAM_B1 * m + (1.0 - ADAM_B1) * g
    v = ADAM_B2 * v + (1.0 - ADAM_B2) * (g * g)
    m_hat = m / (1.0 - ADAM_B1 ** ADAM_STEP)
    v_hat = v / (1.0 - ADAM_B2 ** ADAM_STEP)
    delta = -ADAM_LR * (m_hat / (jnp.sqrt(v_hat) + ADAM_EPS) + ADAM_WD * w)
    return delta, m, v


def _adam_shard(parts, own, w, m, v, name):
    R = w.shape[0]
    tr = max(t for t in (16, 32, 64, 128, 192, 256) if R % t == 0)

    def body(p_ref, o_ref, w_ref, m_ref, v_ref, g_out, d_out, m_out, v_out):
        me = _my_index()
        g = jnp.zeros((tr, D), F32)
        for j in range(N_DEV):
            g = g + jnp.where(me == j, o_ref[...], p_ref[j]).astype(F32)
        delta, mn, vn = _adamw(w_ref[...], g, m_ref[...], v_ref[...])
        g_out[...] = g
        d_out[...] = delta
        m_out[...] = mn
        v_out[...] = vn

    row = pl.BlockSpec((tr, D), lambda i: (i, 0))
    shp = jax.ShapeDtypeStruct((R, D), F32)
    return pl.pallas_call(
        body, name=name, grid=(R // tr,), out_shape=(shp,) * 4,
        in_specs=[pl.BlockSpec((N_DEV, tr, D), lambda i: (0, i, 0)), row, row, row, row], out_specs=(row,) * 4,
        compiler_params=_params(("parallel",), VMEM_LIMIT),
    )(parts, own, w, m, v)


def _adam_ada_w(cond_t, dmod, w, m, v):
    ncol = w.shape[-1]
    tr = 512

    def body(c_ref, d_ref, w_ref, m_ref, v_ref, g_out, d_out, m_out, v_out):
        g = _nn(c_ref[...], d_ref[0])
        delta, mn, vn = _adamw(w_ref[0], g, m_ref[0], v_ref[0])
        g_out[0] = g
        d_out[0] = delta
        m_out[0] = mn
        v_out[0] = vn

    blk = pl.BlockSpec((1, tr, ncol), lambda l, i: (l, i, 0))
    shp = jax.ShapeDtypeStruct(w.shape, F32)
    return pl.pallas_call(
        body, name="adam_ada_w", grid=(DEPTH, D // tr), out_shape=(shp,) * 4,
        in_specs=[pl.BlockSpec((tr, LANES), lambda l, i: (i, 0)), pl.BlockSpec((1, LANES, ncol), lambda l, i: (l, 0, 0)), blk, blk, blk],
        out_specs=(blk,) * 4, compiler_params=_params(("parallel", "parallel"), VMEM_LIMIT),
    )(cond_t, dmod, w, m, v)


TILE_ROWS = 168


def _stat_sources():
    pairs = []
    for i in range(DEPTH):
        b = 32 * i
        for q, src in enumerate((b, b + 1, b + 8, b + 16, b + 17, b + 24)):
            pairs.append((6 * i + q, src))
        pairs.append((24 + i, b + 2))
        pairs.append((32 + i, b + 18))
    pairs += [(40, 128), (41, 129)]
    return pairs


def _small_exchange(tiles, w, m, v):
    loss_row, sink_row, sink_src = 41, 48, 136

    def body(s_ref, w_ref, m_ref, v_ref, dmod_out, g_out, d_out, m_out, v_out, loss_out, all_ref, tot_ref, send_sems, recv_sems):
        me = _my_index()
        all_ref[me] = s_ref[...]
        copies = []
        for k in range(1, N_DEV):
            dev, _ = _peer(k)
            cp = pltpu.make_async_remote_copy(src_ref=s_ref, dst_ref=all_ref.at[me], send_sem=send_sems.at[k - 1],
                                              recv_sem=recv_sems.at[k - 1], device_id=dev, device_id_type=MESH)
            cp.start()
            copies.append(cp)
        for k in range(1, N_DEV):
            dev, pidx = _peer(k)
            pltpu.make_async_remote_copy(src_ref=s_ref, dst_ref=all_ref.at[pidx], send_sem=send_sems.at[k - 1],
                                         recv_sem=recv_sems.at[k - 1], device_id=dev, device_id_type=MESH).wait_recv()
        for cp in copies:
            cp.wait_send()
        tot = all_ref[0]
        for j in range(1, N_DEV):
            tot = tot + all_ref[j]
        tot_ref[...] = tot
        g_out[...] = jnp.zeros_like(g_out)
        for dst, src in _stat_sources():
            g_out[pl.ds(dst, 1), :] = tot_ref[pl.ds(src, 1), :]
            if dst < 24:
                for j in range(N_DEV):
                    dmod_out[j, pl.ds(dst, 1), :] = all_ref[j, pl.ds(src, 1), :]
        lane = lax.broadcasted_iota(jnp.int32, (1, D), 1)
        sink = jnp.zeros((1, D), F32)
        for h in range(32):
            sink = jnp.where(lane == h, tot_ref[pl.ds(sink_src + h, 1), :], sink)
        g_out[pl.ds(sink_row, 1), :] = sink
        g = g_out[...]
        delta, mn, vn = _adamw(w_ref[...], g, m_ref[...], v_ref[...])
        d_out[...] = delta
        m_out[...] = mn
        v_out[...] = vn
        loss = jnp.sum(g[loss_row:loss_row + 1, :], axis=-1, keepdims=True) * (0.5 / D)
        loss_out[...] = jnp.broadcast_to(loss, loss_out.shape)

    vm = pl.BlockSpec(memory_space=pltpu.VMEM)
    shp = jax.ShapeDtypeStruct((STAT_ROWS, D), F32)
    return pl.pallas_call(
        body, name="small_exchange",
        out_shape=(jax.ShapeDtypeStruct((N_DEV, 24, D), F32), shp, shp, shp, shp, jax.ShapeDtypeStruct((8, LANES), F32)),
        in_specs=[vm] * 4, out_specs=(vm,) * 6,
        scratch_shapes=[pltpu.VMEM((N_DEV, TILE_ROWS, D), F32), pltpu.VMEM((TILE_ROWS, D), F32),
                        pltpu.SemaphoreType.DMA((N_DEV - 1,)), pltpu.SemaphoreType.DMA((N_DEV - 1,))],
        compiler_params=_params(vmem=VMEM_LIMIT),
    )(tiles, w, m, v)


def _to_rows(name, a):
    if name in ("ffn_in", "a_in", "b_in"):
        return a.T
    if name == "b_out":
        return a.T.reshape(-1, D)
    return a


def _from_rows(name, a):
    if name in ("ffn_in", "a_in", "b_in"):
        return a.T
    if name == "b_out":
        return a.reshape(-1, 512).T
    return a


def _rows8(a):
    return jnp.pad(a, ((0, 8 - a.shape[0]), (0, 0)))


def _pack_small(ada_b, norm_mix, norm_ffn, final_norm, sink):
    sink_row = jnp.pad(sink.reshape(1, -1), ((0, 0), (0, D - sink.size)))
    return jnp.concatenate([ada_b.reshape(24, D), _rows8(norm_mix), _rows8(norm_ffn), _rows8(final_norm.reshape(1, D)),
                            _rows8(sink_row)], axis=0)


def _unpack_small(a):
    return a[0:24].reshape(4, 6 * D), a[24:28], a[32:36], a[40], a[48, :32].reshape(2, 16)


def kernel(x, c, ada_w, ada_b, norm_mix, norm_ffn, ffn_w_in, ffn_w_out, a_w_in, a_w_out, a_sink, b_w_in, b_w_out, final_norm, loss_target, m_ada_w, m_ada_b, m_norm_mix, m_norm_ffn, m_ffn_w_in, m_ffn_w_out, m_a_w_in, m_a_w_out, m_a_sink, m_b_w_in, m_b_w_out, m_final_norm, v_ada_w, v_ada_b, v_norm_mix, v_norm_ffn, v_ffn_w_in, v_ffn_w_out, v_a_w_in, v_a_w_out, v_a_sink, v_b_w_in, v_b_w_out, v_final_norm):
    S = x.shape[1]
    x0 = x.reshape(S, D)
    target = loss_target.reshape(S, D)
    me = _my_index()
    ncol = ada_w.shape[-1]

    ada_b_mine = lax.dynamic_slice_in_dim(ada_b, me * ncol, ncol, axis=1)
    cond_all, parts = _cond_exchange(jnp.broadcast_to(c.reshape(1, D), (8, D)), ada_w, ada_b_mine)
    mod = lax.dynamic_index_in_dim(parts, me, axis=2, keepdims=False)
    mod = jnp.transpose(mod, (1, 0, 2)).reshape(DEPTH, 6, 1, D)

    weights = {"ffn_in": ffn_w_in, "ffn_out": ffn_w_out, "a_in": a_w_in, "a_out": a_w_out, "b_in": b_w_in, "b_out": b_w_out}
    shard = {(n, l): _to_rows(n, weights[n][l]).astype(BF16) for n, l, _ in SEGMENTS}
    first = [sg for sg in _layer_segments(0) if not sg[0].startswith("ffn")]
    gathered0 = _all_gather_weights([shard[(n, l)] for n, l, _ in first])
    W = {(n, l): g for (n, l, _), g in zip(first, gathered0)}
    groups = [[sg for sg in _layer_segments(0) if sg[0].startswith("ffn")], [sg for i in range(1, DEPTH) for sg in _layer_segments(i)]]
    gathers, order = [], gathered0[0]
    for q, segs in enumerate(groups):
        mine = jnp.concatenate([shard[(n, l)] for n, l, _ in segs], axis=0)
        zone = lax.empty((N_DEV, mine.shape[0], D), BF16)
        gathers.append(_exchange_start([mine], zone, [mine.shape[0]], [0], False, order, "weight_gather_start_%d" % q))
        order = gathers[-1][-1]
    gather_token = order[0:1, 0:1]

    def finish_gather(q, after):
        zone = _exchange_wait(gathers[q], after, "weight_gather_wait_%d" % q)
        offs, _ = _offsets(groups[q])
        for (n, l, rows), off in zip(groups[q], offs):
            full = lax.dynamic_update_slice(zone[:, off:off + rows], shard[(n, l)][None], (me, 0, 0))
            W[(n, l)] = full.reshape(D, 512) if n == "b_out" else full.reshape(N_DEV * rows, D)

    a_slopes, b_slopes = _slopes(16), _slopes(24)
    bias_a = _alibi_bias(a_slopes, A_HALF, 1)
    bias_b = [_alibi_bias(b_slopes[8 * g:8 * g + 8], B_HALF, dil) for g, dil in enumerate(B_DILS)]
    bias_b_fwd = [_alibi_bias(b_slopes[8 * g:8 * g + 8], B_HALF, dil, max(CHUNK, TQ * dil), both=True) for g, dil in enumerate(B_DILS)]
    a_geom = dict(C=A_QKV, r=1, half=A_HALF, qoff=0, koff=1024, voff=1280, n_units=2)
    b_geom = [dict(C=B_QKV, r=dil, half=B_HALF, qoff=512 * g, koff=1536 + 128 * g, voff=1920 + 128 * g, n_units=1)
              for g, dil in enumerate(B_DILS)]

    saved = []
    xcur = x0
    for i in range(DEPTH):
        j = i // 2
        sh1, sc1, g1, sh2, sc2, g2 = [mod[i, q] for q in range(6)]
        nm, nf = norm_mix[i].reshape(1, D), norm_ffn[i].reshape(1, D)
        if i == 0:
            nm = nm + gather_token
        if i == 1:
            finish_gather(1, xcur)
        if i % 2 == 0:
            sink_rep = jnp.repeat(jnp.repeat(a_sink[j], TQ).reshape(2, 1, 8 * TQ), 8, axis=1).reshape(16, 8 * TQ)
            h1, qkv = _proj(xcur, nm, sc1, sh1, W[("a_in", j)], ffn=False, name="proj_a")
            o, lse = _attn_fwd(qkv, bias_a, sink_rep, out_dtype=BF16, name="attn_a_fwd", **a_geom)
            x1, y1 = _gated_residual(o, W[("a_out", j)], xcur, g1, w_is_transposed=False, name="out_a")
        else:
            sink_rep = None
            h1, qkv = _proj(xcur, nm, sc1, sh1, W[("b_in", j)], ffn=False, name="proj_b")
            outs = [_attn_fwd(qkv, bias_b_fwd[g], None, out_dtype=F32, name="attn_b%d_fwd" % g, **b_geom[g]) for g in range(3)]
            o, lse = _merge_groups([t[0] for t in outs], [t[1] for t in outs])
            x1, y1 = _gated_residual(o, W[("b_out", j)], xcur, g1, w_is_transposed=True, name="out_b")
        if i == 0:
            finish_gather(0, x1)
        h2, gu, act = _proj(x1, nf, sc2, sh2, W[("ffn_in", i)], ffn=True, name="ffn_in")
        x2, y2 = _gated_residual(act, W[("ffn_out", i)], x1, g2, w_is_transposed=False, name="ffn_out")
        saved.append(dict(x0=xcur, h1=h1, qkv=qkv, o=o, lse=lse, y1=y1, x1=x1, h2=h2, gu=gu, act=act, y2=y2, sink=sink_rep))
        xcur = x2

    dx, head_stats = _loss_head(xcur, target, final_norm.reshape(1, D))

    dW = {}
    stat_tiles, dsink = [None] * DEPTH, [None] * 2
    exchanges = []
    start_token = None

    def start_exchange(segs):
        offs, total = _offsets(segs)
        own = jnp.concatenate([lax.dynamic_slice_in_dim(dW[(n, l)], me * rows, rows, axis=0) for n, l, rows in segs], axis=0)
        started = _exchange_start([dW[(n, l)] for n, l, _ in segs], lax.empty((N_DEV, total, D), BF16), [sg[2] for sg in segs],
                                  offs, True, own, "grad_exchange_start_%d" % len(exchanges))
        exchanges.append((segs, started, own))
        return started[-1][0:1, 0:1]

    for i in reversed(range(DEPTH)):
        j = i // 2
        sv = saved[i]
        sh1, sc1, g1, sh2, sc2, g2 = [mod[i, q] for q in range(6)]
        if start_token is not None:
            g2 = g2 + start_token
            start_token = None
        nm, nf = norm_mix[i].reshape(1, D), norm_ffn[i].reshape(1, D)
        dgu, dy2, st_g2 = _gate_bwd(dx, sv["y2"], g2, W[("ffn_out", i)], sv["gu"], w_is_transposed=False, name="ffn_out_bwd")
        dW[("ffn_out", i)] = _weight_grad(dy2, sv["act"], transpose_out=True, name="dw_ffn_out")
        dW[("ffn_in", i)] = _weight_grad(sv["h2"], dgu, transpose_out=True, name="dw_ffn_in")
        if i == 0:
            g1 = g1 + start_exchange([sg for sg in _layer_segments(0) if sg[0].startswith("ffn")])
        dx1, st_f = _norm_bwd(dgu, W[("ffn_in", i)], sv["x1"], dx, nf, sc2, name="ffn_in_bwd")
        if i % 2 == 0:
            do, dy1, st_g1 = _gate_bwd(dx1, sv["y1"], g1, W[("a_out", j)], None, w_is_transposed=False, name="out_a_bwd")
            dW[("a_out", j)] = _weight_grad(dy1, sv["o"], transpose_out=True, name="dw_a_out")
            dq, dk, dv, ds = _attn_bwd(sv["qkv"], bias_a, sv["sink"], sv["o"], do, sv["lse"], name="attn_a_bwd", **a_geom)
            dsink[j] = ds
            dqkv = jnp.concatenate([dq, dk[0].astype(BF16), dk[1].astype(BF16), dv[0].astype(BF16), dv[1].astype(BF16)], axis=1)
            dW[("a_in", j)] = _weight_grad(sv["h1"], dqkv, transpose_out=True, name="dw_a_in")
            dx0, st_m = _norm_bwd(dqkv, W[("a_in", j)], sv["x0"], dx1, nm, sc1, name="proj_a_bwd")
        else:
            do, dy1, st_g1 = _gate_bwd(dx1, sv["y1"], g1, W[("b_out", j)], None, w_is_transposed=True, name="out_b_bwd")
            dW[("b_out", j)] = _weight_grad(dy1, sv["o"], transpose_out=False, name="dw_b_out").reshape(N_DEV * 64, D)
            gr = [_attn_bwd(sv["qkv"], bias_b[g], None, sv["o"], do, sv["lse"], name="attn_b%d_bwd" % g, **b_geom[g]) for g in range(3)]
            dqkv = jnp.concatenate([t[0] for t in gr] + [t[1][0].astype(BF16) for t in gr] + [t[2][0].astype(BF16) for t in gr], axis=1)
            dW[("b_in", j)] = _weight_grad(sv["h1"], dqkv, transpose_out=True, name="dw_b_in")
            dx0, st_m = _norm_bwd(dqkv, W[("b_in", j)], sv["x0"], dx1, nm, sc1, name="proj_b_bwd")
        stat_tiles[i] = [st_m, st_g1, st_f, st_g2]
        if i > 0:
            start_token = start_exchange(_layer_segments(i))
        else:
            start_exchange([sg for sg in _layer_segments(0) if not sg[0].startswith("ffn")])
        dx = dx0
    grad_x = dx.reshape(1, S, D)

    masters = {"ffn_in": (ffn_w_in, m_ffn_w_in, v_ffn_w_in), "ffn_out": (ffn_w_out, m_ffn_w_out, v_ffn_w_out),
               "a_in": (a_w_in, m_a_w_in, v_a_w_in), "a_out": (a_w_out, m_a_w_out, v_a_w_out),
               "b_in": (b_w_in, m_b_w_in, v_b_w_in), "b_out": (b_w_out, m_b_w_out, v_b_w_out)}
    pieces = {}
    after = dx
    for segs, started, own in exchanges:
        offs, total = _offsets(segs)
        parts_g = _exchange_wait(started, after, "grad_exchange_wait_%d" % len(pieces))
        rows_wmv = [jnp.concatenate([_to_rows(n, masters[n][q][l]) for n, l, _ in segs], axis=0) for q in range(3)]
        res_rows = _adam_shard(parts_g, own, *rows_wmv, name="adam_%d" % total)
        after = res_rows[0]
        for q, kind in enumerate(("grad", "delta", "m", "v")):
            for (n, l, rows), off in zip(segs, offs):
                pieces[(kind, n, l)] = _from_rows(n, res_rows[q][off:off + rows])
    big = {(kind, n): jnp.stack([pieces[(kind, n, l)] for l in range(4 if n.startswith("ffn") else 2)])
           for kind in ("grad", "delta", "m", "v") for n in masters}

    tiles = jnp.concatenate([t for i in range(DEPTH) for t in stat_tiles[i]] + [head_stats]
                            + [jnp.pad(ds, ((0, 0), (0, D - LANES))) for ds in dsink], axis=0)
    small = [_pack_small(*t) for t in ((ada_b, norm_mix, norm_ffn, final_norm, a_sink),
                                       (m_ada_b, m_norm_mix, m_norm_ffn, m_final_norm, m_a_sink),
                                       (v_ada_b, v_norm_mix, v_norm_ffn, v_final_norm, v_a_sink))]
    dmod_all, sg, sd, sm, sv_, loss_tile = _small_exchange(tiles, *small)
    loss = loss_tile[0, 0]
    dmod_all = dmod_all.reshape(N_DEV, DEPTH, 6 * D)
    dmod_mine = lax.dynamic_slice_in_dim(dmod_all, me * ncol, ncol, axis=2)
    dmod_pad = jnp.pad(jnp.transpose(dmod_mine, (1, 0, 2)), ((0, 0), (0, LANES - N_DEV), (0, 0))).astype(BF16)
    cond_t = jnp.pad(cond_all.T, ((0, 0), (0, LANES - N_DEV))).astype(BF16)
    ada = _adam_ada_w(cond_t, dmod_pad, ada_w, m_ada_w, v_ada_w)

    outs = [loss, grad_x]
    small_res = [_unpack_small(t) for t in (sg, sd, sm, sv_)]
    for q, kind in enumerate(("grad", "delta", "m", "v")):
        ab, nm_, nf_, fn, sk = small_res[q]
        outs += [ada[q], ab, nm_, nf_, big[(kind, "ffn_in")], big[(kind, "ffn_out")], big[(kind, "a_in")], big[(kind, "a_out")],
                 sk, big[(kind, "b_in")], big[(kind, "b_out")], fn]
    return tuple(outs)
```

```python
import functools
import math

import numpy as np
import jax
import jax.numpy as jnp
from jax import lax
from jax.experimental import pallas as pl
from jax.experimental.pallas import tpu as pltpu

D = 1024
HEAD_DIM = 64
D_FF = 2816
DEPTH = 4
N_DEV = 8
A_QKV = 1536
B_QKV = 2304
A_HALF = 128
B_HALF = 64
B_DILS = (1, 4, 16)
RMS_EPS = 1e-6
NEG = -1e30
ADAM_LR = 0.001
ADAM_B1 = 0.9
ADAM_B2 = 0.999
ADAM_EPS = 1e-08
ADAM_WD = 0.01
ADAM_STEP = 10

LANES = 128
SPLIT = 2
TQ = 128
VMEM_LIMIT = 56 * 1024 * 1024
MESH = pl.DeviceIdType.MESH
F32 = jnp.float32
BF16 = jnp.bfloat16

SEGMENTS = ([("ffn_in", l, 704) for l in range(4)] + [("ffn_out", l, 352) for l in range(4)]
            + [("a_in", j, 192) for j in range(2)] + [("a_out", j, 128) for j in range(2)]
            + [("b_in", j, 288) for j in range(2)] + [("b_out", j, 64) for j in range(2)])
def _layer_segments(i):
    mixer = "a" if i % 2 == 0 else "b"
    return [s for s in SEGMENTS if (s[0].startswith("ffn") and s[1] == i) or (s[0].startswith(mixer + "_") and s[1] == i // 2)]


def _offsets(segs):
    rows = [s[2] for s in segs]
    return [sum(rows[:k]) for k in range(len(rows))], sum(rows)
STAT_ROWS = 56


def _nn(a, b):
    return jnp.dot(a, b, preferred_element_type=F32)


def _nt(a, b):
    return lax.dot_general(a, b, (((1,), (1,)), ((), ())), preferred_element_type=F32)


def _tn(a, b):
    return lax.dot_general(a, b, (((0,), (0,)), ((), ())), preferred_element_type=F32)


def _params(dims=None, vmem=None):
    kw = {}
    if dims is not None:
        kw["dimension_semantics"] = dims
    if vmem is not None:
        kw["vmem_limit_bytes"] = vmem
    return pltpu.CompilerParams(**kw)


def _my_index():
    return 4 * lax.axis_index("x") + 2 * lax.axis_index("y") + lax.axis_index("c")


def _peer(k):
    x, y, c = lax.axis_index("x"), lax.axis_index("y"), lax.axis_index("c")
    px, py, pc = x ^ ((k >> 2) & 1), y ^ ((k >> 1) & 1), c ^ (k & 1)
    return (px, py, pc), 4 * px + 2 * py + pc


def _const_spec(shape):
    nd = len(shape)
    return pl.BlockSpec(shape, lambda *_: (0,) * nd)


def _cond_exchange(c_tile, ada_w, ada_b_mine):
    ncol = ada_w.shape[-1]

    def body(c_ref, w_ref, b_ref, cond_ref, parts_ref, call_ref, mine_ref, send_sems, recv_sems):
        me = _my_index()
        call_ref[me] = c_ref[...]
        copies = []
        for k in range(1, N_DEV):
            dev, _ = _peer(k)
            cp = pltpu.make_async_remote_copy(src_ref=c_ref, dst_ref=call_ref.at[me], send_sem=send_sems.at[0, k - 1],
                                              recv_sem=recv_sems.at[0, k - 1], device_id=dev, device_id_type=MESH)
            cp.start()
            copies.append(cp)
        for k in range(1, N_DEV):
            _, pidx = _peer(k)
            pltpu.make_async_remote_copy(src_ref=c_ref, dst_ref=call_ref.at[pidx], send_sem=send_sems.at[0, k - 1],
                                         recv_sem=recv_sems.at[0, k - 1], device_id=_peer(k)[0], device_id_type=MESH).wait_recv()
        for cp in copies:
            cp.wait_send()
        row = lax.broadcasted_iota(jnp.int32, (N_DEV, D), 0)
        cmat = jnp.zeros((N_DEV, D), F32)
        for j in range(N_DEV):
            cmat = jnp.where(row == j, call_ref[j], cmat)
        cond = cmat * jax.nn.sigmoid(cmat)
        cond_ref[...] = cond
        cb = cond.astype(BF16)
        for l in range(DEPTH):
            mine_ref[l] = _nn(cb, w_ref[l].astype(BF16)) + b_ref[pl.ds(l, 1), :]
        parts_ref[me] = mine_ref[...]
        copies = []
        for k in range(1, N_DEV):
            dev, _ = _peer(k)
            cp = pltpu.make_async_remote_copy(src_ref=mine_ref, dst_ref=parts_ref.at[me], send_sem=send_sems.at[1, k - 1],
                                              recv_sem=recv_sems.at[1, k - 1], device_id=dev, device_id_type=MESH)
            cp.start()
            copies.append(cp)
        for k in range(1, N_DEV):
            dev, pidx = _peer(k)
            pltpu.make_async_remote_copy(src_ref=mine_ref, dst_ref=parts_ref.at[pidx], send_sem=send_sems.at[1, k - 1],
                                         recv_sem=recv_sems.at[1, k - 1], device_id=dev, device_id_type=MESH).wait_recv()
        for cp in copies:
            cp.wait_send()

    vm = pl.BlockSpec(memory_space=pltpu.VMEM)
    return pl.pallas_call(
        body, name="cond_exchange",
        out_shape=(jax.ShapeDtypeStruct((N_DEV, D), F32), jax.ShapeDtypeStruct((N_DEV, DEPTH, N_DEV, ncol), F32)),
        in_specs=[vm, vm, vm], out_specs=(vm, vm),
        scratch_shapes=[pltpu.VMEM((N_DEV, N_DEV, D), F32), pltpu.VMEM((DEPTH, N_DEV, ncol), F32),
                        pltpu.SemaphoreType.DMA((2, N_DEV - 1)), pltpu.SemaphoreType.DMA((2, N_DEV - 1))],
        compiler_params=_params(vmem=VMEM_LIMIT),
    )(c_tile, ada_w, ada_b_mine)[:2]


def _all_gather_weights(shards):
    n = len(shards)
    big = max(range(n), key=lambda s: shards[s].shape[0])
    total = sum(sh.shape[0] for sh in shards)
    assert N_DEV * shards[big].shape[0] >= total

    def body(*refs):
        ins, outs = refs[:n], refs[n:2 * n]
        local_sems, send_sems, recv_sems = refs[2 * n:]
        me = _my_index()
        local = []
        for s in range(n):
            rows = ins[s].shape[0]
            cp = pltpu.make_async_copy(ins[s], outs[s].at[pl.ds(me * rows, rows)], local_sems.at[s])
            cp.start()
            local.append(cp)
        for k in range(1, N_DEV):
            dev, _ = _peer(k)
            for s in range(n):
                rows = ins[s].shape[0]
                pltpu.make_async_remote_copy(src_ref=ins[s], dst_ref=outs[s].at[pl.ds(me * rows, rows)],
                                             send_sem=send_sems.at[k - 1], recv_sem=recv_sems.at[k - 1],
                                             device_id=dev, device_id_type=MESH).start()
        whole = outs[big].at[pl.ds(0, total)]
        for k in range(1, N_DEV):
            dev, _ = _peer(k)
            w = pltpu.make_async_remote_copy(src_ref=whole, dst_ref=whole, send_sem=send_sems.at[k - 1],
                                             recv_sem=recv_sems.at[k - 1], device_id=dev, device_id_type=MESH)
            w.wait_send()
            w.wait_recv()
        for cp in local:
            cp.wait()

    hbm = pl.BlockSpec(memory_space=pl.ANY)
    return pl.pallas_call(
        body, name="weight_all_gather",
        out_shape=tuple(jax.ShapeDtypeStruct((N_DEV * s.shape[0], D), s.dtype) for s in shards),
        in_specs=[hbm] * n, out_specs=tuple([hbm] * n),
        scratch_shapes=[pltpu.SemaphoreType.DMA((n,)), pltpu.SemaphoreType.DMA((N_DEV - 1,)),
                        pltpu.SemaphoreType.DMA((N_DEV - 1,))],
    )(*shards)


HBM = pl.BlockSpec(memory_space=pltpu.HBM)
SEM = pl.BlockSpec(memory_space=pltpu.SEMAPHORE)
EFFECT = pltpu.SideEffectType.DATAFLOW_SIDE_EFFECTING


def _exchange_start(srcs, landing, rows, offs, to_peer_rows, after, name):
    n = len(srcs)

    def body(*refs):
        src_refs, land_ref = refs[:n], refs[n]
        send_sems, recv_sems = refs[n + 2], refs[n + 3]
        token = refs[-1]
        me = _my_index()
        for k in range(1, N_DEV):
            dev, pidx = _peer(k)
            for q in range(n):
                src = src_refs[q].at[pl.ds(pidx * rows[q], rows[q])] if to_peer_rows else src_refs[q]
                pltpu.make_async_remote_copy(src_ref=src, dst_ref=land_ref.at[me, pl.ds(offs[q], rows[q])],
                                             send_sem=send_sems.at[k - 1], recv_sem=recv_sems.at[k - 1],
                                             device_id=dev, device_id_type=MESH).start()
        token[...] = jnp.zeros_like(token)

    arrays = list(srcs) + [landing]
    return pl.pallas_call(
        body, name=name,
        out_shape=(pltpu.SemaphoreType.DMA((N_DEV - 1,)), pltpu.SemaphoreType.DMA((N_DEV - 1,)),
                   *[pltpu.HBM(a.shape, a.dtype) for a in arrays], jax.ShapeDtypeStruct((8, LANES), F32)),
        in_specs=[HBM] * (n + 1) + [pl.BlockSpec(memory_space=pl.ANY)],
        out_specs=(SEM, SEM, *[HBM] * (n + 1), pl.BlockSpec(memory_space=pltpu.VMEM)),
        input_output_aliases={q: 2 + q for q in range(n + 1)},
        compiler_params=pltpu.CompilerParams(has_side_effects=EFFECT),
    )(*[pltpu.with_memory_space_constraint(a, pltpu.HBM) for a in arrays], after)


def _exchange_wait(started, after, name):
    send_sems, recv_sems = started[0], started[1]
    arrays = list(started[2:-1])
    n1 = len(arrays)

    def body(*refs):
        land_ref = refs[n1 - 1]
        sends, recvs = refs[n1], refs[n1 + 1]
        for k in range(1, N_DEV):
            dev, _ = _peer(k)
            w = pltpu.make_async_remote_copy(src_ref=land_ref.at[0], dst_ref=land_ref.at[0], send_sem=sends.at[k - 1],
                                             recv_sem=recvs.at[k - 1], device_id=dev, device_id_type=MESH)
            w.wait_send()
            w.wait_recv()

    return pl.pallas_call(
        body, name=name, out_shape=tuple(pltpu.HBM(a.shape, a.dtype) for a in arrays),
        in_specs=[HBM] * n1 + [SEM, SEM, pl.BlockSpec(memory_space=pl.ANY)], out_specs=tuple([HBM] * n1),
        input_output_aliases={q: q for q in range(n1)},
        compiler_params=pltpu.CompilerParams(has_side_effects=EFFECT),
    )(*arrays, send_sems, recv_sems, after)[n1 - 1]


def _norm_mod(x, nw, sc, sh):
    ms = jnp.mean(x * x, axis=-1, keepdims=True)
    xh = x * lax.rsqrt(ms + RMS_EPS)
    return xh, (xh * nw) * (1.0 + sc) + sh


def _proj(x, nw, sc, sh, wt, *, ffn, name):
    S, N = x.shape[0], wt.shape[0]
    tm = 256 if ffn else 512

    def body(x_ref, nw_ref, sc_ref, sh_ref, w_ref, h_ref, out_ref, *act_ref):
        for half in range(SPLIT):
            rows = pl.ds(half * (tm // SPLIT), tm // SPLIT)
            _, h = _norm_mod(x_ref[rows, :], nw_ref[...], sc_ref[...], sh_ref[...])
            hb = h.astype(BF16)
            h_ref[rows, :] = hb
            if ffn:
                gate = _nt(hb, w_ref[pl.ds(0, D_FF), :])
                up = _nt(hb, w_ref[pl.ds(D_FF, D_FF), :])
                sig = jax.nn.sigmoid(gate)
                silu = gate * sig
                out_ref[rows, pl.ds(0, D_FF)] = (up * (sig * (1.0 + gate * (1.0 - sig)))).astype(BF16)
                out_ref[rows, pl.ds(D_FF, D_FF)] = silu.astype(BF16)
                act_ref[0][rows, :] = (silu * up).astype(BF16)
            else:
                out_ref[rows, :] = _nt(hb, w_ref[...]).astype(BF16)

    row = lambda w: pl.BlockSpec((tm, w), lambda i: (i, 0))
    out_shape = [jax.ShapeDtypeStruct((S, D), BF16), jax.ShapeDtypeStruct((S, N), BF16)]
    out_specs = [row(D), row(N)]
    if ffn:
        out_shape.append(jax.ShapeDtypeStruct((S, D_FF), BF16))
        out_specs.append(row(D_FF))
    vec = _const_spec((1, D))
    return pl.pallas_call(
        body, name=name, grid=(S // tm,), out_shape=tuple(out_shape),
        in_specs=[row(D), vec, vec, vec, _const_spec((N, D))], out_specs=tuple(out_specs),
        compiler_params=_params(("parallel",), VMEM_LIMIT),
    )(x, nw, sc, sh, wt)


def _gated_residual(a, w, x, g, *, w_is_transposed, name):
    S, K = a.shape
    tm = 512

    def body(a_ref, w_ref, x_ref, g_ref, xo_ref, y_ref):
        y = _nt(a_ref[...], w_ref[...]) if w_is_transposed else _nn(a_ref[...], w_ref[...])
        y_ref[...] = y.astype(BF16)
        xo_ref[...] = x_ref[...] + g_ref[...] * y

    row = lambda w_: pl.BlockSpec((tm, w_), lambda i: (i, 0))
    return pl.pallas_call(
        body, name=name, grid=(S // tm,),
        out_shape=(jax.ShapeDtypeStruct((S, D), F32), jax.ShapeDtypeStruct((S, D), BF16)),
        in_specs=[row(K), _const_spec(w.shape), row(D), _const_spec((1, D))], out_specs=(row(D), row(D)),
        compiler_params=_params(("parallel",), VMEM_LIMIT),
    )(a, w, x, g)


CHUNK = 1024


def _tile_rows(r, chunk=CHUNK):
    return min(TQ, chunk // r)


def _alibi_bias(slopes, half, dil, chunk=CHUNK, both=False):
    tq = _tile_rows(dil, chunk)
    tk = tq + 2 * half
    rel = np.arange(tk)[:, None] - half - np.arange(tq)[None, :]
    band = np.abs(rel) <= half
    dist = (dil * np.abs(rel)).astype(np.float32)
    tabs = [np.where(band, -np.float32(s) * dist, np.float32(NEG)).astype(np.float32) for s in slopes]
    out = []
    for u in range(0, len(tabs), 8):
        tab = np.concatenate(tabs[u:u + 8], axis=1)
        first, last = tab.copy(), tab.copy()
        first[:half] = NEG
        last[tk - half:] = NEG
        out += [tab, first, last]
        if both:
            last = last.copy()
            last[:half] = NEG
            out.append(last)
    return jnp.asarray(np.concatenate(out, axis=0))


def _slopes(n):
    return (2.0 ** (-8.0 * np.arange(1, n + 1) / n)).astype(np.float32)


def _head_masks(tq):
    lane = lax.broadcasted_iota(jnp.int32, (tq, LANES), 1)
    lo = lane < HEAD_DIM
    return lo, jnp.logical_not(lo)


def _stack_heads(tiles, lo, hi, scale):
    blocks = []
    for t in range(4):
        xf = tiles[t] if scale == 1.0 else tiles[t] * scale
        for a in range(2):
            xm = jnp.where(lo if a == 0 else hi, xf, 0.0)
            if a != t // 2:
                xm = pltpu.roll(xm, HEAD_DIM, 1)
            blocks.append(xm.astype(BF16))
    return jnp.concatenate(blocks, axis=0)


def _tile_from_columns(x8t, t, tq):
    r0 = HEAD_DIM * (t // 2)
    top = x8t[r0:r0 + HEAD_DIM, 2 * t * tq:(2 * t + 1) * tq]
    bot = x8t[r0:r0 + HEAD_DIM, (2 * t + 1) * tq:(2 * t + 2) * tq]
    return jnp.concatenate([top, bot], axis=0).T


def _attn_layout(S, C, r, half, qoff, koff, voff, chunk):
    hb = half * r
    per = chunk // hb
    nhb = S // hb
    main = lambda off: pl.BlockSpec((chunk, LANES), lambda u, i: (i, off // LANES + u))
    prev = lambda off: pl.BlockSpec((hb, LANES), lambda u, i: (jnp.maximum(i * per - 1, 0), off // LANES + u))
    nxt = lambda off: pl.BlockSpec((hb, LANES), lambda u, i: (jnp.minimum((i + 1) * per, nhb - 1), off // LANES + u))
    specs = [pl.BlockSpec((chunk, 4 * LANES), lambda u, i: (i, qoff // (4 * LANES) + u))]
    specs += [prev(koff), main(koff), nxt(koff), prev(voff), main(voff), nxt(voff)]
    return specs, hb


def _stage(dst, srcs):
    row = 0
    for src in srcs:
        n = src.shape[0]
        dst[pl.ds(row, n), :] = src[...].astype(F32)
        row += n


def _rows(start, n, r):
    return pl.ds(start, n, stride=r) if r > 1 else pl.ds(start, n)


def _attn_fwd(qkv, bias, sink, *, C, r, half, qoff, koff, voff, n_units, out_dtype, name):
    S = qkv.shape[0]
    chunk = max(CHUNK, TQ * r)
    tq = _tile_rows(r, chunk)
    tk = tq + 2 * half
    tiles = chunk // (r * tq)
    nsteps = S // chunk
    specs, hb = _attn_layout(S, C, r, half, qoff, koff, voff, chunk)
    use_sink = sink is not None

    def body(*refs):
        q_ref, kp, km, kn, vp, vm, vn, bias_ref = refs[:8]
        rest = list(refs[8:])
        sink_ref = rest.pop(0) if use_sink else None
        o_ref, lse_ref, qs, ks, vs, os_, ls = rest
        i = pl.program_id(1)
        for t in range(4):
            qs[t] = q_ref[:, pl.ds(t * LANES, LANES)].astype(F32)
        _stage(ks, [kp, km, kn])
        _stage(vs, [vp, vm, vn])
        lo, hi = _head_masks(tq)
        ones = jnp.ones((16, tk), BF16)
        if use_sink:
            sk = sink_ref[pl.ds(0, 1), :]

        def chain(n, carry):
            rho, c = n // tiles, n % tiles
            start = c * (tq * r) + rho
            if r == 1:
                start = pl.multiple_of(start, tq)
            variant = jnp.where(jnp.logical_and(i == 0, c == 0), 1, 0) + jnp.where(
                jnp.logical_and(i == nsteps - 1, c == tiles - 1), 2, 0)
            k2 = ks[_rows(start, tk, r), :].astype(BF16)
            v2t = jnp.concatenate([vs[_rows(start, tk, r), :].T.astype(BF16), ones], axis=0)
            q8 = _stack_heads([qs[t, _rows(start, tq, r), :] for t in range(4)], lo, hi, HEAD_DIM ** -0.5)
            s = _nt(k2, q8) + bias_ref[pl.ds(pl.multiple_of(variant * tk, 8), tk), :]
            m = jnp.max(s, axis=0, keepdims=True)
            if use_sink:
                m = jnp.maximum(m, sk)
            pv = _nn(v2t, jnp.exp(s - m).astype(BF16))
            l = pv[LANES:LANES + 1]
            if use_sink:
                l = l + jnp.exp(sk - m)
            o8t = pv[:LANES] / l
            lse8 = jnp.broadcast_to(m + jnp.log(l), (LANES, 8 * tq))
            for t in range(4):
                os_[t, _rows(start, tq, r), :] = _tile_from_columns(o8t, t, tq)
                ls[t, _rows(start, tq, r), :] = _tile_from_columns(lse8, t, tq)
            return carry

        lax.fori_loop(0, r * tiles, chain, 0, unroll=4)
        for t in range(4):
            o_ref[:, pl.ds(t * LANES, LANES)] = os_[t].astype(out_dtype)
            lse_ref[:, pl.ds(t * LANES, LANES)] = ls[t]

    in_specs = specs + [pl.BlockSpec((bias.shape[0] // n_units, 8 * tq), lambda u, i: (u, 0))]
    args = [qkv] * 7 + [bias]
    if use_sink:
        in_specs.append(pl.BlockSpec((8, 8 * tq), lambda u, i: (u, 0)))
        args.append(sink)
    wide = pl.BlockSpec((chunk, 4 * LANES), lambda u, i: (i, u))
    win = hb + chunk + hb
    return pl.pallas_call(
        body, name=name, grid=(n_units, nsteps),
        out_shape=(jax.ShapeDtypeStruct((S, n_units * 512), out_dtype), jax.ShapeDtypeStruct((S, n_units * 512), F32)),
        in_specs=in_specs, out_specs=(wide, wide),
        scratch_shapes=[pltpu.VMEM((4, chunk, LANES), F32), pltpu.VMEM((win, LANES), F32), pltpu.VMEM((win, LANES), F32),
                        pltpu.VMEM((4, chunk, LANES), F32), pltpu.VMEM((4, chunk, LANES), F32)],
        compiler_params=_params(("parallel", "parallel"), VMEM_LIMIT),
    )(*args)


def _attn_bwd(qkv, bias, sink, o, do, lse, *, C, r, half, qoff, koff, voff, n_units, name):
    S = qkv.shape[0]
    tq = _tile_rows(r)
    tk = tq + 2 * half
    tiles = CHUNK // (r * tq)
    nsteps = S // CHUNK
    specs, hb = _attn_layout(S, C, r, half, qoff, koff, voff, CHUNK)
    use_sink = sink is not None

    def body(*refs):
        q_ref, kp, km, kn, vp, vm, vn, bias_ref = refs[:8]
        rest = list(refs[8:])
        sink_ref = rest.pop(0) if use_sink else None
        o_ref, do_ref, lse_ref, dq_ref, dk_hbm, dv_hbm = rest[:6]
        rest = rest[6:]
        dsink_ref = rest.pop(0) if use_sink else None
        qs, ks, vs, os_, dos, ls, dqs, acck, accv, sem = rest
        u, i = pl.program_id(0), pl.program_id(1)

        @pl.when(i == 0)
        def _():
            acck[...] = jnp.zeros_like(acck)
            accv[...] = jnp.zeros_like(accv)
            if use_sink:
                dsink_ref[...] = jnp.zeros_like(dsink_ref)

        for t in range(4):
            cols = pl.ds(t * LANES, LANES)
            qs[t] = q_ref[:, cols].astype(F32)
            os_[t] = o_ref[:, cols].astype(F32)
            dos[t] = do_ref[:, cols].astype(F32)
            ls[t] = lse_ref[:, cols]
        _stage(ks, [kp, km, kn])
        _stage(vs, [vp, vm, vn])
        lo, hi = _head_masks(tq)
        base = pl.multiple_of(i * CHUNK, CHUNK)
        if use_sink:
            sk = sink_ref[pl.ds(0, 1), :]

        def chain(n, carry):
            rho, c = n // tiles, n % tiles
            start = c * (tq * r) + rho
            if r == 1:
                start = pl.multiple_of(start, tq)
            variant = jnp.where(jnp.logical_and(i == 0, c == 0), 1, 0) + jnp.where(
                jnp.logical_and(i == nsteps - 1, c == tiles - 1), 2, 0)
            k2 = ks[_rows(start, tk, r), :].astype(BF16)
            v2 = vs[_rows(start, tk, r), :].astype(BF16)
            k2t = ks[_rows(start, tk, r), :].T.astype(BF16)
            q8 = _stack_heads([qs[t, _rows(start, tq, r), :] for t in range(4)], lo, hi, HEAD_DIM ** -0.5)
            do_tiles = [dos[t, _rows(start, tq, r), :] for t in range(4)]
            do8 = _stack_heads(do_tiles, lo, hi, 1.0)
            deltas, lses = [], []
            for t in range(4):
                prod_t = (do_tiles[t] * os_[t, _rows(start, tq, r), :]).T
                lse_t = ls[t, _rows(start, tq, r), :].T
                for a in range(2):
                    deltas.append(jnp.sum(prod_t[a * HEAD_DIM:(a + 1) * HEAD_DIM], axis=0, keepdims=True))
                    lses.append(lse_t[a * HEAD_DIM:a * HEAD_DIM + 1])
            delta8 = jnp.concatenate(deltas, axis=1)
            lse8 = jnp.concatenate(lses, axis=1)
            s = _nt(k2, q8) + bias_ref[pl.ds(pl.multiple_of(variant * tk, 8), tk), :]
            p = jnp.exp(s - lse8)
            dp = _nt(v2, do8)
            dsb = (p * (dp - delta8)).astype(BF16)
            dq8t = _nn(k2t, dsb)
            for t in range(4):
                dqs[t, _rows(start, tq, r), :] = _tile_from_columns(dq8t, t, tq) * (HEAD_DIM ** -0.5)
            arow = base + start
            if r == 1:
                arow = pl.multiple_of(arow, tq)
            acck[_rows(arow, tk, r), :] = acck[_rows(arow, tk, r), :] + _nn(dsb, q8)
            accv[_rows(arow, tk, r), :] = accv[_rows(arow, tk, r), :] + _nn(p.astype(BF16), do8)
            if use_sink:
                e = jnp.exp(sk - lse8) * delta8
                for h in range(8):
                    part = -jnp.sum(e[:, h * tq:(h + 1) * tq], axis=1, keepdims=True)
                    dsink_ref[pl.ds(h, 1), :] = dsink_ref[pl.ds(h, 1), :] + part
            return carry

        lax.fori_loop(0, r * tiles, chain, 0, unroll=2)
        for t in range(4):
            dq_ref[:, pl.ds(t * LANES, LANES)] = dqs[t].astype(BF16)

        @pl.when(i == nsteps - 1)
        def _():
            ck = pltpu.make_async_copy(acck.at[pl.ds(hb, S)], dk_hbm.at[u], sem.at[0])
            cv = pltpu.make_async_copy(accv.at[pl.ds(hb, S)], dv_hbm.at[u], sem.at[1])
            ck.start()
            cv.start()
            ck.wait()
            cv.wait()

    wide = pl.BlockSpec((CHUNK, 4 * LANES), lambda u, i: (i, u))
    hbm = pl.BlockSpec(memory_space=pl.ANY)
    in_specs = specs + [pl.BlockSpec((3 * tk, 8 * tq), lambda u, i: (u, 0))]
    args = [qkv] * 7 + [bias]
    if use_sink:
        in_specs.append(pl.BlockSpec((8, 8 * tq), lambda u, i: (u, 0)))
        args.append(sink)
    in_specs += [wide, wide, wide]
    args += [o, do, lse]
    out_shape = [jax.ShapeDtypeStruct((S, n_units * 512), BF16), jax.ShapeDtypeStruct((n_units, S, LANES), F32),
                 jax.ShapeDtypeStruct((n_units, S, LANES), F32)]
    out_specs = [wide, hbm, hbm]
    if use_sink:
        out_shape.append(jax.ShapeDtypeStruct((n_units * 8, LANES), F32))
        out_specs.append(pl.BlockSpec((8, LANES), lambda u, i: (u, 0)))
    win = hb + CHUNK + hb
    big = lambda: pltpu.VMEM((4, CHUNK, LANES), F32)
    res = pl.pallas_call(
        body, name=name, grid=(n_units, nsteps), out_shape=tuple(out_shape), in_specs=in_specs, out_specs=tuple(out_specs),
        scratch_shapes=[big(), pltpu.VMEM((win, LANES), F32), pltpu.VMEM((win, LANES), F32), big(), big(), big(), big(),
                        pltpu.VMEM((S + 2 * hb, LANES), F32), pltpu.VMEM((S + 2 * hb, LANES), F32), pltpu.SemaphoreType.DMA((2,))],
        compiler_params=_params(("arbitrary", "arbitrary"), VMEM_LIMIT),
    )(*args)
    return res[0], res[1], res[2], (res[3] if use_sink else None)


def _merge_groups(os_, lses):
    S, W = os_[0].shape
    tm = 512

    def body(o0, o1, o2, l0, l1, l2, o_ref, lse_ref):
        ls = [l0[...], l1[...], l2[...]]
        mx = jnp.maximum(jnp.maximum(ls[0], ls[1]), ls[2])
        es = [jnp.exp(l - mx) for l in ls]
        den = es[0] + es[1] + es[2]
        o = (es[0] / den) * o0[...] + (es[1] / den) * o1[...] + (es[2] / den) * o2[...]
        o_ref[...] = o.astype(BF16)
        lse_ref[...] = mx + jnp.log(den)

    row = pl.BlockSpec((tm, W), lambda i: (i, 0))
    return pl.pallas_call(
        body, name="merge_groups", grid=(S // tm,),
        out_shape=(jax.ShapeDtypeStruct((S, W), BF16), jax.ShapeDtypeStruct((S, W), F32)),
        in_specs=[row] * 6, out_specs=(row, row), compiler_params=_params(("parallel",), VMEM_LIMIT),
    )(*os_, *lses)


def _loss_head(x, target, fnw):
    S = x.shape[0]
    tm = 512

    def body(x_ref, t_ref, w_ref, dx_ref, st_ref):
        @pl.when(pl.program_id(0) == 0)
        def _():
            st_ref[...] = jnp.zeros_like(st_ref)

        xv = x_ref[...]
        rstd = lax.rsqrt(jnp.mean(xv * xv, axis=-1, keepdims=True) + RMS_EPS)
        xh = xv * rstd
        err = xh * w_ref[...] - t_ref[...]
        dy = err * (1.0 / D)
        dxh = dy * w_ref[...]
        dx_ref[...] = rstd * (dxh - xh * jnp.mean(dxh * xh, axis=-1, keepdims=True))
        st_ref[pl.ds(0, 1), :] = st_ref[pl.ds(0, 1), :] + jnp.sum(dy * xh, axis=0, keepdims=True)
        st_ref[pl.ds(1, 1), :] = st_ref[pl.ds(1, 1), :] + jnp.sum(err * err, axis=0, keepdims=True)

    row = pl.BlockSpec((tm, D), lambda i: (i, 0))
    return pl.pallas_call(
        body, name="loss_head", grid=(S // tm,),
        out_shape=(jax.ShapeDtypeStruct((S, D), F32), jax.ShapeDtypeStruct((8, D), F32)),
        in_specs=[row, row, _const_spec((1, D))], out_specs=(row, _const_spec((8, D))),
        compiler_params=_params(("arbitrary",), VMEM_LIMIT),
    )(x, target, fnw)


def _gate_bwd(dx, y, g, w, gu, *, w_is_transposed, name):
    S = dx.shape[0]
    K = w.shape[1] if w_is_transposed else w.shape[0]
    ffn = gu is not None
    tm = 512
    wout = 2 * K if ffn else K

    def body(dx_ref, y_ref, g_ref, w_ref, *rest):
        if ffn:
            gu_ref, da_ref, dyb_ref, st_ref = rest
        else:
            da_ref, dyb_ref, st_ref = rest

        @pl.when(pl.program_id(0) == 0)
        def _():
            st_ref[...] = jnp.zeros_like(st_ref)

        total = jnp.zeros((1, D), F32)
        split = 1 if ffn else SPLIT
        for half in range(split):
            rows = pl.ds(half * (tm // split), tm // split)
            dxv = dx_ref[rows, :]
            total = total + jnp.sum(dxv * y_ref[rows, :].astype(F32), axis=0, keepdims=True)
            dyb = (dxv * g_ref[...]).astype(BF16)
            dyb_ref[rows, :] = dyb
            da = _nn(dyb, w_ref[...]) if w_is_transposed else _nt(dyb, w_ref[...])
            if ffn:
                da_ref[rows, pl.ds(0, K)] = (da * gu_ref[rows, pl.ds(0, K)].astype(F32)).astype(BF16)
                da_ref[rows, pl.ds(K, K)] = (da * gu_ref[rows, pl.ds(K, K)].astype(F32)).astype(BF16)
            else:
                da_ref[rows, :] = da.astype(BF16)
        st_ref[pl.ds(0, 1), :] = st_ref[pl.ds(0, 1), :] + total

    row = lambda w_: pl.BlockSpec((tm, w_), lambda i: (i, 0))
    in_specs = [row(D), row(D), _const_spec((1, D)), _const_spec(w.shape)]
    args = [dx, y, g, w]
    if ffn:
        in_specs.append(row(wout))
        args.append(gu)
    return pl.pallas_call(
        body, name=name, grid=(S // tm,),
        out_shape=(jax.ShapeDtypeStruct((S, wout), BF16), jax.ShapeDtypeStruct((S, D), BF16), jax.ShapeDtypeStruct((8, D), F32)),
        in_specs=in_specs, out_specs=(row(wout), row(D), _const_spec((8, D))),
        compiler_params=_params(("arbitrary",), VMEM_LIMIT),
    )(*args)


def _norm_bwd(dy, wt, x, dres, nw, sc, *, name):
    S, N = dy.shape
    tm = 512

    def body(dy_ref, w_ref, x_ref, dres_ref, nw_ref, sc_ref, dx_ref, st_ref):
        @pl.when(pl.program_id(0) == 0)
        def _():
            st_ref[...] = jnp.zeros_like(st_ref)

        nwv, scale = nw_ref[...], 1.0 + sc_ref[...]
        sums = [jnp.zeros((1, D), F32)] * 3
        for half in range(SPLIT):
            rows = pl.ds(half * (tm // SPLIT), tm // SPLIT)
            dh = _nn(dy_ref[rows, :], w_ref[...])
            xv = x_ref[rows, :]
            rstd = lax.rsqrt(jnp.mean(xv * xv, axis=-1, keepdims=True) + RMS_EPS)
            xh = xv * rstd
            dxh = dh * (nwv * scale)
            dx_ref[rows, :] = dres_ref[rows, :] + rstd * (dxh - xh * jnp.mean(dxh * xh, axis=-1, keepdims=True))
            dhx = dh * xh
            sums = [sums[0] + jnp.sum(dh, axis=0, keepdims=True), sums[1] + jnp.sum(dhx * nwv, axis=0, keepdims=True),
                    sums[2] + jnp.sum(dhx * scale, axis=0, keepdims=True)]
        for q in range(3):
            st_ref[pl.ds(q, 1), :] = st_ref[pl.ds(q, 1), :] + sums[q]

    row = lambda w_: pl.BlockSpec((tm, w_), lambda i: (i, 0))
    vec = _const_spec((1, D))
    return pl.pallas_call(
        body, name=name, grid=(S // tm,),
        out_shape=(jax.ShapeDtypeStruct((S, D), F32), jax.ShapeDtypeStruct((8, D), F32)),
        in_specs=[row(N), _const_spec((N, D)), row(D), row(D), vec, vec], out_specs=(row(D), _const_spec((8, D))),
        compiler_params=_params(("arbitrary",), VMEM_LIMIT),
    )(dy, wt, x, dres, nw, sc)


def _weight_grad(a, b, *, transpose_out, name):
    S, N = b.shape
    nb = N // 2 if N > 4096 else N
    tk = 512

    def body(a_ref, b_ref, out_ref, acc):
        k = pl.program_id(1)

        @pl.when(k == 0)
        def _():
            acc[...] = jnp.zeros_like(acc)

        acc[...] += _tn(a_ref[...], b_ref[...])

        @pl.when(k == pl.num_programs(1) - 1)
        def _():
            out_ref[...] = (acc[...].T if transpose_out else acc[...]).astype(BF16)

    out_block = pl.BlockSpec((nb, D), lambda n, k: (n, 0)) if transpose_out else pl.BlockSpec((D, nb), lambda n, k: (0, n))
    return pl.pallas_call(
        body, name=name, grid=(N // nb, S // tk),
        out_shape=jax.ShapeDtypeStruct((N, D) if transpose_out else (D, N), BF16),
        in_specs=[pl.BlockSpec((tk, D), lambda n, k: (k, 0)), pl.BlockSpec((tk, nb), lambda n, k: (k, n))],
        out_specs=out_block, scratch_shapes=[pltpu.VMEM((D, nb), F32)],
        compiler_params=_params(("parallel", "arbitrary"), VMEM_LIMIT),
    )(a, b)


def _adamw(w, g, m, v):
    m = ADAM_B1 * m + (1.0 - ADAM_B1) * g
    v = ADAM_B2 * v + (1.0 - ADAM_B2) * (g * g)
    m_hat = m / (1.0 - ADAM_B1 ** ADAM_STEP)
    v_hat = v / (1.0 - ADAM_B2 ** ADAM_STEP)
    delta = -ADAM_LR * (m_hat / (jnp.sqrt(v_hat) + ADAM_EPS) + ADAM_WD * w)
    return delta, m, v


def _adam_shard(parts, own, w, m, v, name):
    R = w.shape[0]
    tr = max(t for t in (16, 32, 64, 128, 192, 256) if R % t == 0)

    def body(p_ref, o_ref, w_ref, m_ref, v_ref, g_out, d_out, m_out, v_out):
        me = _my_index()
        g = jnp.zeros((tr, D), F32)
        for j in range(N_DEV):
            g = g + jnp.where(me == j, o_ref[...], p_ref[j]).astype(F32)
        delta, mn, vn = _adamw(w_ref[...], g, m_ref[...], v_ref[...])
        g_out[...] = g
        d_out[...] = delta
        m_out[...] = mn
        v_out[...] = vn

    row = pl.BlockSpec((tr, D), lambda i: (i, 0))
    shp = jax.ShapeDtypeStruct((R, D), F32)
    return pl.pallas_call(
        body, name=name, grid=(R // tr,), out_shape=(shp,) * 4,
        in_specs=[pl.BlockSpec((N_DEV, tr, D), lambda i: (0, i, 0)), row, row, row, row], out_specs=(row,) * 4,
        compiler_params=_params(("parallel",), VMEM_LIMIT),
    )(parts, own, w, m, v)


def _adam_ada_w(cond_t, dmod, w, m, v):
    ncol = w.shape[-1]
    tr = 512

    def body(c_ref, d_ref, w_ref, m_ref, v_ref, g_out, d_out, m_out, v_out):
        g = _nn(c_ref[...], d_ref[0])
        delta, mn, vn = _adamw(w_ref[0], g, m_ref[0], v_ref[0])
        g_out[0] = g
        d_out[0] = delta
        m_out[0] = mn
        v_out[0] = vn

    blk = pl.BlockSpec((1, tr, ncol), lambda l, i: (l, i, 0))
    shp = jax.ShapeDtypeStruct(w.shape, F32)
    return pl.pallas_call(
        body, name="adam_ada_w", grid=(DEPTH, D // tr), out_shape=(shp,) * 4,
        in_specs=[pl.BlockSpec((tr, LANES), lambda l, i: (i, 0)), pl.BlockSpec((1, LANES, ncol), lambda l, i: (l, 0, 0)), blk, blk, blk],
        out_specs=(blk,) * 4, compiler_params=_params(("parallel", "parallel"), VMEM_LIMIT),
    )(cond_t, dmod, w, m, v)


TILE_ROWS = 168


def _stat_sources():
    pairs = []
    for i in range(DEPTH):
        b = 32 * i
        for q, src in enumerate((b, b + 1, b + 8, b + 16, b + 17, b + 24)):
            pairs.append((6 * i + q, src))
        pairs.append((24 + i, b + 2))
        pairs.append((32 + i, b + 18))
    pairs += [(40, 128), (41, 129)]
    return pairs


def _small_exchange(tiles, w, m, v):
    loss_row, sink_row, sink_src = 41, 48, 136

    def body(s_ref, w_ref, m_ref, v_ref, dmod_out, g_out, d_out, m_out, v_out, loss_out, all_ref, tot_ref, send_sems, recv_sems):
        me = _my_index()
        all_ref[me] = s_ref[...]
        copies = []
        for k in range(1, N_DEV):
            dev, _ = _peer(k)
            cp = pltpu.make_async_remote_copy(src_ref=s_ref, dst_ref=all_ref.at[me], send_sem=send_sems.at[k - 1],
                                              recv_sem=recv_sems.at[k - 1], device_id=dev, device_id_type=MESH)
            cp.start()
            copies.append(cp)
        for k in range(1, N_DEV):
            dev, pidx = _peer(k)
            pltpu.make_async_remote_copy(src_ref=s_ref, dst_ref=all_ref.at[pidx], send_sem=send_sems.at[k - 1],
                                         recv_sem=recv_sems.at[k - 1], device_id=dev, device_id_type=MESH).wait_recv()
        for cp in copies:
            cp.wait_send()
        tot = all_ref[0]
        for j in range(1, N_DEV):
            tot = tot + all_ref[j]
        tot_ref[...] = tot
        g_out[...] = jnp.zeros_like(g_out)
        for dst, src in _stat_sources():
            g_out[pl.ds(dst, 1), :] = tot_ref[pl.ds(src, 1), :]
            if dst < 24:
                for j in range(N_DEV):
                    dmod_out[j, pl.ds(dst, 1), :] = all_ref[j, pl.ds(src, 1), :]
        lane = lax.broadcasted_iota(jnp.int32, (1, D), 1)
        sink = jnp.zeros((1, D), F32)
        for h in range(32):
            sink = jnp.where(lane == h, tot_ref[pl.ds(sink_src + h, 1), :], sink)
        g_out[pl.ds(sink_row, 1), :] = sink
        g = g_out[...]
        delta, mn, vn = _adamw(w_ref[...], g, m_ref[...], v_ref[...])
        d_out[...] = delta
        m_out[...] = mn
        v_out[...] = vn
        loss = jnp.sum(g[loss_row:loss_row + 1, :], axis=-1, keepdims=True) * (0.5 / D)
        loss_out[...] = jnp.broadcast_to(loss, loss_out.shape)

    vm = pl.BlockSpec(memory_space=pltpu.VMEM)
    shp = jax.ShapeDtypeStruct((STAT_ROWS, D), F32)
    return pl.pallas_call(
        body, name="small_exchange",
        out_shape=(jax.ShapeDtypeStruct((N_DEV, 24, D), F32), shp, shp, shp, shp, jax.ShapeDtypeStruct((8, LANES), F32)),
        in_specs=[vm] * 4, out_specs=(vm,) * 6,
        scratch_shapes=[pltpu.VMEM((N_DEV, TILE_ROWS, D), F32), pltpu.VMEM((TILE_ROWS, D), F32),
                        pltpu.SemaphoreType.DMA((N_DEV - 1,)), pltpu.SemaphoreType.DMA((N_DEV - 1,))],
        compiler_params=_params(vmem=VMEM_LIMIT),
    )(tiles, w, m, v)


def _to_rows(name, a):
    if name in ("ffn_in", "a_in", "b_in"):
        return a.T
    if name == "b_out":
        return a.T.reshape(-1, D)
    return a


def _from_rows(name, a):
    if name in ("ffn_in", "a_in", "b_in"):
        return a.T
    if name == "b_out":
        return a.reshape(-1, 512).T
    return a


def _rows8(a):
    return jnp.pad(a, ((0, 8 - a.shape[0]), (0, 0)))


def _pack_small(ada_b, norm_mix, norm_ffn, final_norm, sink):
    sink_row = jnp.pad(sink.reshape(1, -1), ((0, 0), (0, D - sink.size)))
    return jnp.concatenate([ada_b.reshape(24, D), _rows8(norm_mix), _rows8(norm_ffn), _rows8(final_norm.reshape(1, D)),
                            _rows8(sink_row)], axis=0)


def _unpack_small(a):
    return a[0:24].reshape(4, 6 * D), a[24:28], a[32:36], a[40], a[48, :32].reshape(2, 16)


def kernel(x, c, ada_w, ada_b, norm_mix, norm_ffn, ffn_w_in, ffn_w_out, a_w_in, a_w_out, a_sink, b_w_in, b_w_out, final_norm, loss_target, m_ada_w, m_ada_b, m_norm_mix, m_norm_ffn, m_ffn_w_in, m_ffn_w_out, m_a_w_in, m_a_w_out, m_a_sink, m_b_w_in, m_b_w_out, m_final_norm, v_ada_w, v_ada_b, v_norm_mix, v_norm_ffn, v_ffn_w_in, v_ffn_w_out, v_a_w_in, v_a_w_out, v_a_sink, v_b_w_in, v_b_w_out, v_final_norm):
    S = x.shape[1]
    x0 = x.reshape(S, D)
    target = loss_target.reshape(S, D)
    me = _my_index()
    ncol = ada_w.shape[-1]

    ada_b_mine = lax.dynamic_slice_in_dim(ada_b, me * ncol, ncol, axis=1)
    cond_all, parts = _cond_exchange(jnp.broadcast_to(c.reshape(1, D), (8, D)), ada_w, ada_b_mine)
    mod = lax.dynamic_index_in_dim(parts, me, axis=2, keepdims=False)
    mod = jnp.transpose(mod, (1, 0, 2)).reshape(DEPTH, 6, 1, D)

    weights = {"ffn_in": ffn_w_in, "ffn_out": ffn_w_out, "a_in": a_w_in, "a_out": a_w_out, "b_in": b_w_in, "b_out": b_w_out}
    shard = {(n, l): _to_rows(n, weights[n][l]).astype(BF16) for n, l, _ in SEGMENTS}
    first = [sg for sg in _layer_segments(0) if not sg[0].startswith("ffn")]
    gathered0 = _all_gather_weights([shard[(n, l)] for n, l, _ in first])
    W = {(n, l): g for (n, l, _), g in zip(first, gathered0)}
    groups = [[sg for sg in _layer_segments(0) if sg[0].startswith("ffn")], [sg for i in range(1, DEPTH) for sg in _layer_segments(i)]]
    gathers, order = [], gathered0[0]
    for q, segs in enumerate(groups):
        mine = jnp.concatenate([shard[(n, l)] for n, l, _ in segs], axis=0)
        zone = lax.empty((N_DEV, mine.shape[0], D), BF16)
        gathers.append(_exchange_start([mine], zone, [mine.shape[0]], [0], False, order, "weight_gather_start_%d" % q))
        order = gathers[-1][-1]
    gather_token = order[0:1, 0:1]

    def finish_gather(q, after):
        zone = _exchange_wait(gathers[q], after, "weight_gather_wait_%d" % q)
        offs, _ = _offsets(groups[q])
        for (n, l, rows), off in zip(groups[q], offs):
            full = lax.dynamic_update_slice(zone[:, off:off + rows], shard[(n, l)][None], (me, 0, 0))
            W[(n, l)] = full.reshape(D, 512) if n == "b_out" else full.reshape(N_DEV * rows, D)

    a_slopes, b_slopes = _slopes(16), _slopes(24)
    bias_a = _alibi_bias(a_slopes, A_HALF, 1)
    bias_b = [_alibi_bias(b_slopes[8 * g:8 * g + 8], B_HALF, dil) for g, dil in enumerate(B_DILS)]
    bias_b_fwd = [_alibi_bias(b_slopes[8 * g:8 * g + 8], B_HALF, dil, max(CHUNK, TQ * dil), both=True) for g, dil in enumerate(B_DILS)]
    a_geom = dict(C=A_QKV, r=1, half=A_HALF, qoff=0, koff=1024, voff=1280, n_units=2)
    b_geom = [dict(C=B_QKV, r=dil, half=B_HALF, qoff=512 * g, koff=1536 + 128 * g, voff=1920 + 128 * g, n_units=1)
              for g, dil in enumerate(B_DILS)]

    saved = []
    xcur = x0
    for i in range(DEPTH):
        j = i // 2
        sh1, sc1, g1, sh2, sc2, g2 = [mod[i, q] for q in range(6)]
        nm, nf = norm_mix[i].reshape(1, D), norm_ffn[i].reshape(1, D)
        if i == 0:
            nm = nm + gather_token
        if i == 1:
            finish_gather(1, xcur)
        if i % 2 == 0:
            sink_rep = jnp.repeat(jnp.repeat(a_sink[j], TQ).reshape(2, 1, 8 * TQ), 8, axis=1).reshape(16, 8 * TQ)
            h1, qkv = _proj(xcur, nm, sc1, sh1, W[("a_in", j)], ffn=False, name="proj_a")
            o, lse = _attn_fwd(qkv, bias_a, sink_rep, out_dtype=BF16, name="attn_a_fwd", **a_geom)
            x1, y1 = _gated_residual(o, W[("a_out", j)], xcur, g1, w_is_transposed=False, name="out_a")
        else:
            sink_rep = None
            h1, qkv = _proj(xcur, nm, sc1, sh1, W[("b_in", j)], ffn=False, name="proj_b")
            outs = [_attn_fwd(qkv, bias_b_fwd[g], None, out_dtype=F32, name="attn_b%d_fwd" % g, **b_geom[g]) for g in range(3)]
            o, lse = _merge_groups([t[0] for t in outs], [t[1] for t in outs])
            x1, y1 = _gated_residual(o, W[("b_out", j)], xcur, g1, w_is_transposed=True, name="out_b")
        if i == 0:
            finish_gather(0, x1)
        h2, gu, act = _proj(x1, nf, sc2, sh2, W[("ffn_in", i)], ffn=True, name="ffn_in")
        x2, y2 = _gated_residual(act, W[("ffn_out", i)], x1, g2, w_is_transposed=False, name="ffn_out")
        saved.append(dict(x0=xcur, h1=h1, qkv=qkv, o=o, lse=lse, y1=y1, x1=x1, h2=h2, gu=gu, act=act, y2=y2, sink=sink_rep))
        xcur = x2

    dx, head_stats = _loss_head(xcur, target, final_norm.reshape(1, D))

    dW = {}
    stat_tiles, dsink = [None] * DEPTH, [None] * 2
    exchanges = []
    start_token = None

    def start_exchange(segs):
        offs, total = _offsets(segs)
        own = jnp.concatenate([lax.dynamic_slice_in_dim(dW[(n, l)], me * rows, rows, axis=0) for n, l, rows in segs], axis=0)
        started = _exchange_start([dW[(n, l)] for n, l, _ in segs], lax.empty((N_DEV, total, D), BF16), [sg[2] for sg in segs],
                                  offs, True, own, "grad_exchange_start_%d" % len(exchanges))
        exchanges.append((segs, started, own))
        return started[-1][0:1, 0:1]

    for i in reversed(range(DEPTH)):
        j = i // 2
        sv = saved[i]
        sh1, sc1, g1, sh2, sc2, g2 = [mod[i, q] for q in range(6)]
        if start_token is not None:
            g2 = g2 + start_token
            start_token = None
        nm, nf = norm_mix[i].reshape(1, D), norm_ffn[i].reshape(1, D)
        dgu, dy2, st_g2 = _gate_bwd(dx, sv["y2"], g2, W[("ffn_out", i)], sv["gu"], w_is_transposed=False, name="ffn_out_bwd")
        dW[("ffn_out", i)] = _weight_grad(dy2, sv["act"], transpose_out=True, name="dw_ffn_out")
        dW[("ffn_in", i)] = _weight_grad(sv["h2"], dgu, transpose_out=True, name="dw_ffn_in")
        if i == 0:
            g1 = g1 + start_exchange([sg for sg in _layer_segments(0) if sg[0].startswith("ffn")])
        dx1, st_f = _norm_bwd(dgu, W[("ffn_in", i)], sv["x1"], dx, nf, sc2, name="ffn_in_bwd")
        if i % 2 == 0:
            do, dy1, st_g1 = _gate_bwd(dx1, sv["y1"], g1, W[("a_out", j)], None, w_is_transposed=False, name="out_a_bwd")
            dW[("a_out", j)] = _weight_grad(dy1, sv["o"], transpose_out=True, name="dw_a_out")
            dq, dk, dv, ds = _attn_bwd(sv["qkv"], bias_a, sv["sink"], sv["o"], do, sv["lse"], name="attn_a_bwd", **a_geom)
            dsink[j] = ds
            dqkv = jnp.concatenate([dq, dk[0].astype(BF16), dk[1].astype(BF16), dv[0].astype(BF16), dv[1].astype(BF16)], axis=1)
            dW[("a_in", j)] = _weight_grad(sv["h1"], dqkv, transpose_out=True, name="dw_a_in")
            dx0, st_m = _norm_bwd(dqkv, W[("a_in", j)], sv["x0"], dx1, nm, sc1, name="proj_a_bwd")
        else:
            do, dy1, st_g1 = _gate_bwd(dx1, sv["y1"], g1, W[("b_out", j)], None, w_is_transposed=True, name="out_b_bwd")
            dW[("b_out", j)] = _weight_grad(dy1, sv["o"], transpose_out=False, name="dw_b_out").reshape(N_DEV * 64, D)
            gr = [_attn_bwd(sv["qkv"], bias_b[g], None, sv["o"], do, sv["lse"], name="attn_b%d_bwd" % g, **b_geom[g]) for g in range(3)]
            dqkv = jnp.concatenate([t[0] for t in gr] + [t[1][0].astype(BF16) for t in gr] + [t[2][0].astype(BF16) for t in gr], axis=1)
            dW[("b_in", j)] = _weight_grad(sv["h1"], dqkv, transpose_out=True, name="dw_b_in")
            dx0, st_m = _norm_bwd(dqkv, W[("b_in", j)], sv["x0"], dx1, nm, sc1, name="proj_b_bwd")
        stat_tiles[i] = [st_m, st_g1, st_f, st_g2]
        if i > 0:
            start_token = start_exchange(_layer_segments(i))
        else:
            start_exchange([sg for sg in _layer_segments(0) if not sg[0].startswith("ffn")])
        dx = dx0
    grad_x = dx.reshape(1, S, D)

    masters = {"ffn_in": (ffn_w_in, m_ffn_w_in, v_ffn_w_in), "ffn_out": (ffn_w_out, m_ffn_w_out, v_ffn_w_out),
               "a_in": (a_w_in, m_a_w_in, v_a_w_in), "a_out": (a_w_out, m_a_w_out, v_a_w_out),
               "b_in": (b_w_in, m_b_w_in, v_b_w_in), "b_out": (b_w_out, m_b_w_out, v_b_w_out)}
    pieces = {}
    after = dx
    for segs, started, own in exchanges:
        offs, total = _offsets(segs)
        parts_g = _exchange_wait(started, after, "grad_exchange_wait_%d" % len(pieces))
        rows_wmv = [jnp.concatenate([_to_rows(n, masters[n][q][l]) for n, l, _ in segs], axis=0) for q in range(3)]
        res_rows = _adam_shard(parts_g, own, *rows_wmv, name="adam_%d" % total)
        after = res_rows[0]
        for q, kind in enumerate(("grad", "delta", "m", "v")):
            for (n, l, rows), off in zip(segs, offs):
                pieces[(kind, n, l)] = _from_rows(n, res_rows[q][off:off + rows])
    big = {(kind, n): jnp.stack([pieces[(kind, n, l)] for l in range(4 if n.startswith("ffn") else 2)])
           for kind in ("grad", "delta", "m", "v") for n in masters}

    tiles = jnp.concatenate([t for i in range(DEPTH) for t in stat_tiles[i]] + [head_stats]
                            + [jnp.pad(ds, ((0, 0), (0, D - LANES))) for ds in dsink], axis=0)
    small = [_pack_small(*t) for t in ((ada_b, norm_mix, norm_ffn, final_norm, a_sink),
                                       (m_ada_b, m_norm_mix, m_norm_ffn, m_final_norm, m_a_sink),
                                       (v_ada_b, v_norm_mix, v_norm_ffn, v_final_norm, v_a_sink))]
    dmod_all, sg, sd, sm, sv_, loss_tile = _small_exchange(tiles, *small)
    loss = loss_tile[0, 0]
    dmod_all = dmod_all.reshape(N_DEV, DEPTH, 6 * D)
    dmod_mine = lax.dynamic_slice_in_dim(dmod_all, me * ncol, ncol, axis=2)
    dmod_pad = jnp.pad(jnp.transpose(dmod_mine, (1, 0, 2)), ((0, 0), (0, LANES - N_DEV), (0, 0))).astype(BF16)
    cond_t = jnp.pad(cond_all.T, ((0, 0), (0, LANES - N_DEV))).astype(BF16)
    ada = _adam_ada_w(cond_t, dmod_pad, ada_w, m_ada_w, v_ada_w)

    outs = [loss, grad_x]
    small_res = [_unpack_small(t) for t in (sg, sd, sm, sv_)]
    for q, kind in enumerate(("grad", "delta", "m", "v")):
        ab, nm_, nf_, fn, sk = small_res[q]
        outs += [ada[q], ab, nm_, nf_, big[(kind, "ffn_in")], big[(kind, "ffn_out")], big[(kind, "a_in")], big[(kind, "a_out")],
                 sk, big[(kind, "b_in")], big[(kind, "b_out")], fn]
    return tuple(outs)
```

```python
import functools
import math

import numpy as np
import jax
import jax.numpy as jnp
from jax import lax
from jax.experimental import pallas as pl
from jax.experimental.pallas import tpu as pltpu

D = 1024
HEAD_DIM = 64
D_FF = 2816
DEPTH = 4
N_DEV = 8
A_QKV = 1536
B_QKV = 2304
A_HALF = 128
B_HALF = 64
B_DILS = (1, 4, 16)
RMS_EPS = 1e-6
NEG = -1e30
ADAM_LR = 0.001
ADAM_B1 = 0.9
ADAM_B2 = 0.999
ADAM_EPS = 1e-08
ADAM_WD = 0.01
ADAM_STEP = 10

LANES = 128
SPLIT = 2
TQ = 128
VMEM_LIMIT = 56 * 1024 * 1024
MESH = pl.DeviceIdType.MESH
F32 = jnp.float32
BF16 = jnp.bfloat16

SEGMENTS = ([("ffn_in", l, 704) for l in range(4)] + [("ffn_out", l, 352) for l in range(4)]
            + [("a_in", j, 192) for j in range(2)] + [("a_out", j, 128) for j in range(2)]
            + [("b_in", j, 288) for j in range(2)] + [("b_out", j, 64) for j in range(2)])
def _layer_segments(i):
    mixer = "a" if i % 2 == 0 else "b"
    return [s for s in SEGMENTS if (s[0].startswith("ffn") and s[1] == i) or (s[0].startswith(mixer + "_") and s[1] == i // 2)]


def _offsets(segs):
    rows = [s[2] for s in segs]
    return [sum(rows[:k]) for k in range(len(rows))], sum(rows)
STAT_ROWS = 56


def _nn(a, b):
    return jnp.dot(a, b, preferred_element_type=F32)


def _nt(a, b):
    return lax.dot_general(a, b, (((1,), (1,)), ((), ())), preferred_element_type=F32)


def _tn(a, b):
    return lax.dot_general(a, b, (((0,), (0,)), ((), ())), preferred_element_type=F32)


def _params(dims=None, vmem=None):
    kw = {}
    if dims is not None:
        kw["dimension_semantics"] = dims
    if vmem is not None:
        kw["vmem_limit_bytes"] = vmem
    return pltpu.CompilerParams(**kw)


def _my_index():
    return 4 * lax.axis_index("x") + 2 * lax.axis_index("y") + lax.axis_index("c")


def _peer(k):
    x, y, c = lax.axis_index("x"), lax.axis_index("y"), lax.axis_index("c")
    px, py, pc = x ^ ((k >> 2) & 1), y ^ ((k >> 1) & 1), c ^ (k & 1)
    return (px, py, pc), 4 * px + 2 * py + pc


def _const_spec(shape):
    nd = len(shape)
    return pl.BlockSpec(shape, lambda *_: (0,) * nd)


def _cond_exchange(c_tile, ada_w, ada_b_mine):
    ncol = ada_w.shape[-1]

    def body(c_ref, w_ref, b_ref, cond_ref, parts_ref, call_ref, mine_ref, send_sems, recv_sems):
        me = _my_index()
        call_ref[me] = c_ref[...]
        copies = []
        for k in range(1, N_DEV):
            dev, _ = _peer(k)
            cp = pltpu.make_async_remote_copy(src_ref=c_ref, dst_ref=call_ref.at[me], send_sem=send_sems.at[0, k - 1],
                                              recv_sem=recv_sems.at[0, k - 1], device_id=dev, device_id_type=MESH)
            cp.start()
            copies.append(cp)
        for k in range(1, N_DEV):
            _, pidx = _peer(k)
            pltpu.make_async_remote_copy(src_ref=c_ref, dst_ref=call_ref.at[pidx], send_sem=send_sems.at[0, k - 1],
                                         recv_sem=recv_sems.at[0, k - 1], device_id=_peer(k)[0], device_id_type=MESH).wait_recv()
        for cp in copies:
            cp.wait_send()
        row = lax.broadcasted_iota(jnp.int32, (N_DEV, D), 0)
        cmat = jnp.zeros((N_DEV, D), F32)
        for j in range(N_DEV):
            cmat = jnp.where(row == j, call_ref[j], cmat)
        cond = cmat * jax.nn.sigmoid(cmat)
        cond_ref[...] = cond
        cb = cond.astype(BF16)
        for l in range(DEPTH):
            mine_ref[l] = _nn(cb, w_ref[l].astype(BF16)) + b_ref[pl.ds(l, 1), :]
        parts_ref[me] = mine_ref[...]
        copies = []
        for k in range(1, N_DEV):
            dev, _ = _peer(k)
            cp = pltpu.make_async_remote_copy(src_ref=mine_ref, dst_ref=parts_ref.at[me], send_sem=send_sems.at[1, k - 1],
                                              recv_sem=recv_sems.at[1, k - 1], device_id=dev, device_id_type=MESH)
            cp.start()
            copies.append(cp)
        for k in range(1, N_DEV):
            dev, pidx = _peer(k)
            pltpu.make_async_remote_copy(src_ref=mine_ref, dst_ref=parts_ref.at[pidx], send_sem=send_sems.at[1, k - 1],
                                         recv_sem=recv_sems.at[1, k - 1], device_id=dev, device_id_type=MESH).wait_recv()
        for cp in copies:
            cp.wait_send()

    vm = pl.BlockSpec(memory_space=pltpu.VMEM)
    return pl.pallas_call(
        body, name="cond_exchange",
        out_shape=(jax.ShapeDtypeStruct((N_DEV, D), F32), jax.ShapeDtypeStruct((N_DEV, DEPTH, N_DEV, ncol), F32)),
        in_specs=[vm, vm, vm], out_specs=(vm, vm),
        scratch_shapes=[pltpu.VMEM((N_DEV, N_DEV, D), F32), pltpu.VMEM((DEPTH, N_DEV, ncol), F32),
                        pltpu.SemaphoreType.DMA((2, N_DEV - 1)), pltpu.SemaphoreType.DMA((2, N_DEV - 1))],
        compiler_params=_params(vmem=VMEM_LIMIT),
    )(c_tile, ada_w, ada_b_mine)[:2]


def _all_gather_weights(shards):
    n = len(shards)
    big = max(range(n), key=lambda s: shards[s].shape[0])
    total = sum(sh.shape[0] for sh in shards)
    assert N_DEV * shards[big].shape[0] >= total

    def body(*refs):
        ins, outs = refs[:n], refs[n:2 * n]
        local_sems, send_sems, recv_sems = refs[2 * n:]
        me = _my_index()
        local = []
        for s in range(n):
            rows = ins[s].shape[0]
            cp = pltpu.make_async_copy(ins[s], outs[s].at[pl.ds(me * rows, rows)], local_sems.at[s])
            cp.start()
            local.append(cp)
        for k in range(1, N_DEV):
            dev, _ = _peer(k)
            for s in range(n):
                rows = ins[s].shape[0]
                pltpu.make_async_remote_copy(src_ref=ins[s], dst_ref=outs[s].at[pl.ds(me * rows, rows)],
                                             send_sem=send_sems.at[k - 1], recv_sem=recv_sems.at[k - 1],
                                             device_id=dev, device_id_type=MESH).start()
        whole = outs[big].at[pl.ds(0, total)]
        for k in range(1, N_DEV):
            dev, _ = _peer(k)
            w = pltpu.make_async_remote_copy(src_ref=whole, dst_ref=whole, send_sem=send_sems.at[k - 1],
                                             recv_sem=recv_sems.at[k - 1], device_id=dev, device_id_type=MESH)
            w.wait_send()
            w.wait_recv()
        for cp in local:
            cp.wait()

    hbm = pl.BlockSpec(memory_space=pl.ANY)
    return pl.pallas_call(
        body, name="weight_all_gather",
        out_shape=tuple(jax.ShapeDtypeStruct((N_DEV * s.shape[0], D), s.dtype) for s in shards),
        in_specs=[hbm] * n, out_specs=tuple([hbm] * n),
        scratch_shapes=[pltpu.SemaphoreType.DMA((n,)), pltpu.SemaphoreType.DMA((N_DEV - 1,)),
                        pltpu.SemaphoreType.DMA((N_DEV - 1,))],
    )(*shards)


HBM = pl.BlockSpec(memory_space=pltpu.HBM)
SEM = pl.BlockSpec(memory_space=pltpu.SEMAPHORE)
EFFECT = pltpu.SideEffectType.DATAFLOW_SIDE_EFFECTING


def _exchange_start(srcs, landing, rows, offs, to_peer_rows, after, name):
    n = len(srcs)

    def body(*refs):
        src_refs, land_ref = refs[:n], refs[n]
        send_sems, recv_sems = refs[n + 2], refs[n + 3]
        token = refs[-1]
        me = _my_index()
        for k in range(1, N_DEV):
            dev, pidx = _peer(k)
            for q in range(n):
                src = src_refs[q].at[pl.ds(pidx * rows[q], rows[q])] if to_peer_rows else src_refs[q]
                pltpu.make_async_remote_copy(src_ref=src, dst_ref=land_ref.at[me, pl.ds(offs[q], rows[q])],
                                             send_sem=send_sems.at[k - 1], recv_sem=recv_sems.at[k - 1],
                                             device_id=dev, device_id_type=MESH).start()
        token[...] = jnp.zeros_like(token)

    arrays = list(srcs) + [landing]
    return pl.pallas_call(
        body, name=name,
        out_shape=(pltpu.SemaphoreType.DMA((N_DEV - 1,)), pltpu.SemaphoreType.DMA((N_DEV - 1,)),
                   *[pltpu.HBM(a.shape, a.dtype) for a in arrays], jax.ShapeDtypeStruct((8, LANES), F32)),
        in_specs=[HBM] * (n + 1) + [pl.BlockSpec(memory_space=pl.ANY)],
        out_specs=(SEM, SEM, *[HBM] * (n + 1), pl.BlockSpec(memory_space=pltpu.VMEM)),
        input_output_aliases={q: 2 + q for q in range(n + 1)},
        compiler_params=pltpu.CompilerParams(has_side_effects=EFFECT),
    )(*[pltpu.with_memory_space_constraint(a, pltpu.HBM) for a in arrays], after)


def _exchange_wait(started, after, name):
    send_sems, recv_sems = started[0], started[1]
    arrays = list(started[2:-1])
    n1 = len(arrays)

    def body(*refs):
        land_ref = refs[n1 - 1]
        sends, recvs = refs[n1], refs[n1 + 1]
        for k in range(1, N_DEV):
            dev, _ = _peer(k)
            w = pltpu.make_async_remote_copy(src_ref=land_ref.at[0], dst_ref=land_ref.at[0], send_sem=sends.at[k - 1],
                                             recv_sem=recvs.at[k - 1], device_id=dev, device_id_type=MESH)
            w.wait_send()
            w.wait_recv()

    return pl.pallas_call(
        body, name=name, out_shape=tuple(pltpu.HBM(a.shape, a.dtype) for a in arrays),
        in_specs=[HBM] * n1 + [SEM, SEM, pl.BlockSpec(memory_space=pl.ANY)], out_specs=tuple([HBM] * n1),
        input_output_aliases={q: q for q in range(n1)},
        compiler_params=pltpu.CompilerParams(has_side_effects=EFFECT),
    )(*arrays, send_sems, recv_sems, after)[n1 - 1]


def _norm_mod(x, nw, sc, sh):
    ms = jnp.mean(x * x, axis=-1, keepdims=True)
    xh = x * lax.rsqrt(ms + RMS_EPS)
    return xh, (xh * nw) * (1.0 + sc) + sh


def _proj(x, nw, sc, sh, wt, *, ffn, name):
    S, N = x.shape[0], wt.shape[0]
    tm = 256 if ffn else 512

    def body(x_ref, nw_ref, sc_ref, sh_ref, w_ref, h_ref, out_ref, *act_ref):
        split = 1 if ffn else SPLIT
        for half in range(split):
            rows = pl.ds(half * (tm // split), tm // split)
            _, h = _norm_mod(x_ref[rows, :], nw_ref[...], sc_ref[...], sh_ref[...])
            hb = h.astype(BF16)
            h_ref[rows, :] = hb
            if ffn:
                gate = _nt(hb, w_ref[pl.ds(0, D_FF), :])
                up = _nt(hb, w_ref[pl.ds(D_FF, D_FF), :])
                sig = jax.nn.sigmoid(gate)
                silu = gate * sig
                out_ref[rows, pl.ds(0, D_FF)] = (up * (sig * (1.0 + gate * (1.0 - sig)))).astype(BF16)
                out_ref[rows, pl.ds(D_FF, D_FF)] = silu.astype(BF16)
                act_ref[0][rows, :] = (silu * up).astype(BF16)
            else:
                out_ref[rows, :] = _nt(hb, w_ref[...]).astype(BF16)

    row = lambda w: pl.BlockSpec((tm, w), lambda i: (i, 0))
    out_shape = [jax.ShapeDtypeStruct((S, D), BF16), jax.ShapeDtypeStruct((S, N), BF16)]
    out_specs = [row(D), row(N)]
    if ffn:
        out_shape.append(jax.ShapeDtypeStruct((S, D_FF), BF16))
        out_specs.append(row(D_FF))
    vec = _const_spec((1, D))
    return pl.pallas_call(
        body, name=name, grid=(S // tm,), out_shape=tuple(out_shape),
        in_specs=[row(D), vec, vec, vec, _const_spec((N, D))], out_specs=tuple(out_specs),
        compiler_params=_params(("parallel",), VMEM_LIMIT),
    )(x, nw, sc, sh, wt)


def _gated_residual(a, w, x, g, *, w_is_transposed, name):
    S, K = a.shape
    tm = 512

    def body(a_ref, w_ref, x_ref, g_ref, xo_ref, y_ref):
        y = _nt(a_ref[...], w_ref[...]) if w_is_transposed else _nn(a_ref[...], w_ref[...])
        y_ref[...] = y.astype(BF16)
        xo_ref[...] = x_ref[...] + g_ref[...] * y

    row = lambda w_: pl.BlockSpec((tm, w_), lambda i: (i, 0))
    return pl.pallas_call(
        body, name=name, grid=(S // tm,),
        out_shape=(jax.ShapeDtypeStruct((S, D), F32), jax.ShapeDtypeStruct((S, D), BF16)),
        in_specs=[row(K), _const_spec(w.shape), row(D), _const_spec((1, D))], out_specs=(row(D), row(D)),
        compiler_params=_params(("parallel",), VMEM_LIMIT),
    )(a, w, x, g)


CHUNK = 1024


def _tile_rows(r, chunk=CHUNK):
    return min(TQ, chunk // r)


def _alibi_bias(slopes, half, dil, chunk=CHUNK, both=False):
    tq = _tile_rows(dil, chunk)
    tk = tq + 2 * half
    rel = np.arange(tk)[:, None] - half - np.arange(tq)[None, :]
    band = np.abs(rel) <= half
    dist = (dil * np.abs(rel)).astype(np.float32)
    tabs = [np.where(band, -np.float32(s) * dist, np.float32(NEG)).astype(np.float32) for s in slopes]
    out = []
    for u in range(0, len(tabs), 8):
        tab = np.concatenate(tabs[u:u + 8], axis=1)
        first, last = tab.copy(), tab.copy()
        first[:half] = NEG
        last[tk - half:] = NEG
        out += [tab, first, last]
        if both:
            last = last.copy()
            last[:half] = NEG
            out.append(last)
    return jnp.asarray(np.concatenate(out, axis=0))


def _slopes(n):
    return (2.0 ** (-8.0 * np.arange(1, n + 1) / n)).astype(np.float32)


def _head_masks(tq):
    lane = lax.broadcasted_iota(jnp.int32, (tq, LANES), 1)
    lo = lane < HEAD_DIM
    return lo, jnp.logical_not(lo)


def _stack_heads(tiles, lo, hi, scale):
    blocks = []
    for t in range(4):
        xf = tiles[t] if scale == 1.0 else tiles[t] * scale
        for a in range(2):
            xm = jnp.where(lo if a == 0 else hi, xf, 0.0)
            if a != t // 2:
                xm = pltpu.roll(xm, HEAD_DIM, 1)
            blocks.append(xm.astype(BF16))
    return jnp.concatenate(blocks, axis=0)


def _tile_from_columns(x8t, t, tq):
    r0 = HEAD_DIM * (t // 2)
    top = x8t[r0:r0 + HEAD_DIM, 2 * t * tq:(2 * t + 1) * tq]
    bot = x8t[r0:r0 + HEAD_DIM, (2 * t + 1) * tq:(2 * t + 2) * tq]
    return jnp.concatenate([top, bot], axis=0).T


def _attn_layout(S, C, r, half, qoff, koff, voff, chunk):
    hb = half * r
    per = chunk // hb
    nhb = S // hb
    main = lambda off: pl.BlockSpec((chunk, LANES), lambda u, i: (i, off // LANES + u))
    prev = lambda off: pl.BlockSpec((hb, LANES), lambda u, i: (jnp.maximum(i * per - 1, 0), off // LANES + u))
    nxt = lambda off: pl.BlockSpec((hb, LANES), lambda u, i: (jnp.minimum((i + 1) * per, nhb - 1), off // LANES + u))
    specs = [pl.BlockSpec((chunk, 4 * LANES), lambda u, i: (i, qoff // (4 * LANES) + u))]
    specs += [prev(koff), main(koff), nxt(koff), prev(voff), main(voff), nxt(voff)]
    return specs, hb


def _stage(dst, srcs):
    row = 0
    for src in srcs:
        n = src.shape[0]
        dst[pl.ds(row, n), :] = src[...].astype(F32)
        row += n


def _rows(start, n, r):
    return pl.ds(start, n, stride=r) if r > 1 else pl.ds(start, n)


def _attn_fwd(qkv, bias, sink, *, C, r, half, qoff, koff, voff, n_units, out_dtype, name):
    S = qkv.shape[0]
    chunk = max(CHUNK, TQ * r)
    tq = _tile_rows(r, chunk)
    tk = tq + 2 * half
    tiles = chunk // (r * tq)
    nsteps = S // chunk
    specs, hb = _attn_layout(S, C, r, half, qoff, koff, voff, chunk)
    use_sink = sink is not None

    def body(*refs):
        q_ref, kp, km, kn, vp, vm, vn, bias_ref = refs[:8]
        rest = list(refs[8:])
        sink_ref = rest.pop(0) if use_sink else None
        o_ref, lse_ref, qs, ks, vs, os_, ls = rest
        i = pl.program_id(1)
        for t in range(4):
            qs[t] = q_ref[:, pl.ds(t * LANES, LANES)].astype(F32)
        _stage(ks, [kp, km, kn])
        _stage(vs, [vp, vm, vn])
        lo, hi = _head_masks(tq)
        ones = jnp.ones((16, tk), BF16)
        if use_sink:
            sk = sink_ref[pl.ds(0, 1), :]

        def chain(n, carry):
            rho, c = n // tiles, n % tiles
            start = c * (tq * r) + rho
            if r == 1:
                start = pl.multiple_of(start, tq)
            variant = jnp.where(jnp.logical_and(i == 0, c == 0), 1, 0) + jnp.where(
                jnp.logical_and(i == nsteps - 1, c == tiles - 1), 2, 0)
            k2 = ks[_rows(start, tk, r), :].astype(BF16)
            v2t = jnp.concatenate([vs[_rows(start, tk, r), :].T.astype(BF16), ones], axis=0)
            q8 = _stack_heads([qs[t, _rows(start, tq, r), :] for t in range(4)], lo, hi, HEAD_DIM ** -0.5)
            s = _nt(k2, q8) + bias_ref[pl.ds(pl.multiple_of(variant * tk, 8), tk), :]
            m = jnp.max(s, axis=0, keepdims=True)
            if use_sink:
                m = jnp.maximum(m, sk)
            pv = _nn(v2t, jnp.exp(s - m).astype(BF16))
            l = pv[LANES:LANES + 1]
            if use_sink:
                l = l + jnp.exp(sk - m)
            o8t = pv[:LANES] / l
            lse8 = jnp.broadcast_to(m + jnp.log(l), (LANES, 8 * tq))
            for t in range(4):
                os_[t, _rows(start, tq, r), :] = _tile_from_columns(o8t, t, tq)
                ls[t, _rows(start, tq, r), :] = _tile_from_columns(lse8, t, tq)
            return carry

        lax.fori_loop(0, r * tiles, chain, 0, unroll=4)
        for t in range(4):
            o_ref[:, pl.ds(t * LANES, LANES)] = os_[t].astype(out_dtype)
            lse_ref[:, pl.ds(t * LANES, LANES)] = ls[t]

    in_specs = specs + [pl.BlockSpec((bias.shape[0] // n_units, 8 * tq), lambda u, i: (u, 0))]
    args = [qkv] * 7 + [bias]
    if use_sink:
        in_specs.append(pl.BlockSpec((8, 8 * tq), lambda u, i: (u, 0)))
        args.append(sink)
    wide = pl.BlockSpec((chunk, 4 * LANES), lambda u, i: (i, u))
    win = hb + chunk + hb
    return pl.pallas_call(
        body, name=name, grid=(n_units, nsteps),
        out_shape=(jax.ShapeDtypeStruct((S, n_units * 512), out_dtype), jax.ShapeDtypeStruct((S, n_units * 512), F32)),
        in_specs=in_specs, out_specs=(wide, wide),
        scratch_shapes=[pltpu.VMEM((4, chunk, LANES), F32), pltpu.VMEM((win, LANES), F32), pltpu.VMEM((win, LANES), F32),
                        pltpu.VMEM((4, chunk, LANES), F32), pltpu.VMEM((4, chunk, LANES), F32)],
        compiler_params=_params(("parallel", "parallel"), VMEM_LIMIT),
    )(*args)


def _attn_bwd(qkv, bias, sink, o, do, lse, *, C, r, half, qoff, koff, voff, n_units, name):
    S = qkv.shape[0]
    tq = _tile_rows(r)
    tk = tq + 2 * half
    tiles = CHUNK // (r * tq)
    nsteps = S // CHUNK
    specs, hb = _attn_layout(S, C, r, half, qoff, koff, voff, CHUNK)
    use_sink = sink is not None

    def body(*refs):
        q_ref, kp, km, kn, vp, vm, vn, bias_ref = refs[:8]
        rest = list(refs[8:])
        sink_ref = rest.pop(0) if use_sink else None
        o_ref, do_ref, lse_ref, dq_ref, dk_hbm, dv_hbm = rest[:6]
        rest = rest[6:]
        dsink_ref = rest.pop(0) if use_sink else None
        qs, ks, vs, os_, dos, ls, dqs, acck, accv, sem = rest
        u, i = pl.program_id(0), pl.program_id(1)

        @pl.when(i == 0)
        def _():
            acck[...] = jnp.zeros_like(acck)
            accv[...] = jnp.zeros_like(accv)
            if use_sink:
                dsink_ref[...] = jnp.zeros_like(dsink_ref)

        for t in range(4):
            cols = pl.ds(t * LANES, LANES)
            qs[t] = q_ref[:, cols].astype(F32)
            os_[t] = o_ref[:, cols].astype(F32)
            dos[t] = do_ref[:, cols].astype(F32)
            ls[t] = lse_ref[:, cols]
        _stage(ks, [kp, km, kn])
        _stage(vs, [vp, vm, vn])
        lo, hi = _head_masks(tq)
        base = pl.multiple_of(i * CHUNK, CHUNK)
        if use_sink:
            sk = sink_ref[pl.ds(0, 1), :]

        def chain(n, carry):
            rho, c = n // tiles, n % tiles
            start = c * (tq * r) + rho
            if r == 1:
                start = pl.multiple_of(start, tq)
            variant = jnp.where(jnp.logical_and(i == 0, c == 0), 1, 0) + jnp.where(
                jnp.logical_and(i == nsteps - 1, c == tiles - 1), 2, 0)
            k2 = ks[_rows(start, tk, r), :].astype(BF16)
            v2 = vs[_rows(start, tk, r), :].astype(BF16)
            k2t = ks[_rows(start, tk, r), :].T.astype(BF16)
            q8 = _stack_heads([qs[t, _rows(start, tq, r), :] for t in range(4)], lo, hi, HEAD_DIM ** -0.5)
            do_tiles = [dos[t, _rows(start, tq, r), :] for t in range(4)]
            do8 = _stack_heads(do_tiles, lo, hi, 1.0)
            deltas, lses = [], []
            for t in range(4):
                prod_t = (do_tiles[t] * os_[t, _rows(start, tq, r), :]).T
                lse_t = ls[t, _rows(start, tq, r), :].T
                for a in range(2):
                    deltas.append(jnp.sum(prod_t[a * HEAD_DIM:(a + 1) * HEAD_DIM], axis=0, keepdims=True))
                    lses.append(lse_t[a * HEAD_DIM:a * HEAD_DIM + 1])
            delta8 = jnp.concatenate(deltas, axis=1)
            lse8 = jnp.concatenate(lses, axis=1)
            s = _nt(k2, q8) + bias_ref[pl.ds(pl.multiple_of(variant * tk, 8), tk), :]
            p = jnp.exp(s - lse8)
            dp = _nt(v2, do8)
            dsb = (p * (dp - delta8)).astype(BF16)
            dq8t = _nn(k2t, dsb)
            for t in range(4):
                dqs[t, _rows(start, tq, r), :] = _tile_from_columns(dq8t, t, tq) * (HEAD_DIM ** -0.5)
            arow = base + start
            if r == 1:
                arow = pl.multiple_of(arow, tq)
            acck[_rows(arow, tk, r), :] = acck[_rows(arow, tk, r), :] + _nn(dsb, q8)
            accv[_rows(arow, tk, r), :] = accv[_rows(arow, tk, r), :] + _nn(p.astype(BF16), do8)
            if use_sink:
                e = jnp.exp(sk - lse8) * delta8
                for h in range(8):
                    part = -jnp.sum(e[:, h * tq:(h + 1) * tq], axis=1, keepdims=True)
                    dsink_ref[pl.ds(h, 1), :] = dsink_ref[pl.ds(h, 1), :] + part
            return carry

        lax.fori_loop(0, r * tiles, chain, 0, unroll=2)
        for t in range(4):
            dq_ref[:, pl.ds(t * LANES, LANES)] = dqs[t].astype(BF16)

        @pl.when(i == nsteps - 1)
        def _():
            ck = pltpu.make_async_copy(acck.at[pl.ds(hb, S)], dk_hbm.at[u], sem.at[0])
            cv = pltpu.make_async_copy(accv.at[pl.ds(hb, S)], dv_hbm.at[u], sem.at[1])
            ck.start()
            cv.start()
            ck.wait()
            cv.wait()

    wide = pl.BlockSpec((CHUNK, 4 * LANES), lambda u, i: (i, u))
    hbm = pl.BlockSpec(memory_space=pl.ANY)
    in_specs = specs + [pl.BlockSpec((3 * tk, 8 * tq), lambda u, i: (u, 0))]
    args = [qkv] * 7 + [bias]
    if use_sink:
        in_specs.append(pl.BlockSpec((8, 8 * tq), lambda u, i: (u, 0)))
        args.append(sink)
    in_specs += [wide, wide, wide]
    args += [o, do, lse]
    out_shape = [jax.ShapeDtypeStruct((S, n_units * 512), BF16), jax.ShapeDtypeStruct((n_units, S, LANES), F32),
                 jax.ShapeDtypeStruct((n_units, S, LANES), F32)]
    out_specs = [wide, hbm, hbm]
    if use_sink:
        out_shape.append(jax.ShapeDtypeStruct((n_units * 8, LANES), F32))
        out_specs.append(pl.BlockSpec((8, LANES), lambda u, i: (u, 0)))
    win = hb + CHUNK + hb
    big = lambda: pltpu.VMEM((4, CHUNK, LANES), F32)
    res = pl.pallas_call(
        body, name=name, grid=(n_units, nsteps), out_shape=tuple(out_shape), in_specs=in_specs, out_specs=tuple(out_specs),
        scratch_shapes=[big(), pltpu.VMEM((win, LANES), F32), pltpu.VMEM((win, LANES), F32), big(), big(), big(), big(),
                        pltpu.VMEM((S + 2 * hb, LANES), F32), pltpu.VMEM((S + 2 * hb, LANES), F32), pltpu.SemaphoreType.DMA((2,))],
        compiler_params=_params(("arbitrary", "arbitrary"), VMEM_LIMIT),
    )(*args)
    return res[0], res[1], res[2], (res[3] if use_sink else None)


def _merge_groups(os_, lses):
    S, W = os_[0].shape
    tm = 512

    def body(o0, o1, o2, l0, l1, l2, o_ref, lse_ref):
        ls = [l0[...], l1[...], l2[...]]
        mx = jnp.maximum(jnp.maximum(ls[0], ls[1]), ls[2])
        es = [jnp.exp(l - mx) for l in ls]
        den = es[0] + es[1] + es[2]
        o = (es[0] / den) * o0[...] + (es[1] / den) * o1[...] + (es[2] / den) * o2[...]
        o_ref[...] = o.astype(BF16)
        lse_ref[...] = mx + jnp.log(den)

    row = pl.BlockSpec((tm, W), lambda i: (i, 0))
    return pl.pallas_call(
        body, name="merge_groups", grid=(S // tm,),
        out_shape=(jax.ShapeDtypeStruct((S, W), BF16), jax.ShapeDtypeStruct((S, W), F32)),
        in_specs=[row] * 6, out_specs=(row, row), compiler_params=_params(("parallel",), VMEM_LIMIT),
    )(*os_, *lses)


def _loss_head(x, target, fnw):
    S = x.shape[0]
    tm = 512

    def body(x_ref, t_ref, w_ref, dx_ref, st_ref):
        @pl.when(pl.program_id(0) == 0)
        def _():
            st_ref[...] = jnp.zeros_like(st_ref)

        xv = x_ref[...]
        rstd = lax.rsqrt(jnp.mean(xv * xv, axis=-1, keepdims=True) + RMS_EPS)
        xh = xv * rstd
        err = xh * w_ref[...] - t_ref[...]
        dy = err * (1.0 / D)
        dxh = dy * w_ref[...]
        dx_ref[...] = rstd * (dxh - xh * jnp.mean(dxh * xh, axis=-1, keepdims=True))
        st_ref[pl.ds(0, 1), :] = st_ref[pl.ds(0, 1), :] + jnp.sum(dy * xh, axis=0, keepdims=True)
        st_ref[pl.ds(1, 1), :] = st_ref[pl.ds(1, 1), :] + jnp.sum(err * err, axis=0, keepdims=True)

    row = pl.BlockSpec((tm, D), lambda i: (i, 0))
    return pl.pallas_call(
        body, name="loss_head", grid=(S // tm,),
        out_shape=(jax.ShapeDtypeStruct((S, D), F32), jax.ShapeDtypeStruct((8, D), F32)),
        in_specs=[row, row, _const_spec((1, D))], out_specs=(row, _const_spec((8, D))),
        compiler_params=_params(("arbitrary",), VMEM_LIMIT),
    )(x, target, fnw)


def _gate_bwd(dx, y, g, w, gu, *, w_is_transposed, name):
    S = dx.shape[0]
    K = w.shape[1] if w_is_transposed else w.shape[0]
    ffn = gu is not None
    tm = 512
    wout = 2 * K if ffn else K

    def body(dx_ref, y_ref, g_ref, w_ref, *rest):
        if ffn:
            gu_ref, da_ref, dyb_ref, st_ref = rest
        else:
            da_ref, dyb_ref, st_ref = rest

        @pl.when(pl.program_id(0) == 0)
        def _():
            st_ref[...] = jnp.zeros_like(st_ref)

        total = jnp.zeros((1, D), F32)
        split = 1 if ffn else SPLIT
        for half in range(split):
            rows = pl.ds(half * (tm // split), tm // split)
            dxv = dx_ref[rows, :]
            total = total + jnp.sum(dxv * y_ref[rows, :].astype(F32), axis=0, keepdims=True)
            dyb = (dxv * g_ref[...]).astype(BF16)
            dyb_ref[rows, :] = dyb
            da = _nn(dyb, w_ref[...]) if w_is_transposed else _nt(dyb, w_ref[...])
            if ffn:
                da_ref[rows, pl.ds(0, K)] = (da * gu_ref[rows, pl.ds(0, K)].astype(F32)).astype(BF16)
                da_ref[rows, pl.ds(K, K)] = (da * gu_ref[rows, pl.ds(K, K)].astype(F32)).astype(BF16)
            else:
                da_ref[rows, :] = da.astype(BF16)
        st_ref[pl.ds(0, 1), :] = st_ref[pl.ds(0, 1), :] + total

    row = lambda w_: pl.BlockSpec((tm, w_), lambda i: (i, 0))
    in_specs = [row(D), row(D), _const_spec((1, D)), _const_spec(w.shape)]
    args = [dx, y, g, w]
    if ffn:
        in_specs.append(row(wout))
        args.append(gu)
    return pl.pallas_call(
        body, name=name, grid=(S // tm,),
        out_shape=(jax.ShapeDtypeStruct((S, wout), BF16), jax.ShapeDtypeStruct((S, D), BF16), jax.ShapeDtypeStruct((8, D), F32)),
        in_specs=in_specs, out_specs=(row(wout), row(D), _const_spec((8, D))),
        compiler_params=_params(("arbitrary",), VMEM_LIMIT),
    )(*args)


def _norm_bwd(dy, wt, x, dres, nw, sc, *, name):
    S, N = dy.shape
    tm = 512

    def body(dy_ref, w_ref, x_ref, dres_ref, nw_ref, sc_ref, dx_ref, st_ref):
        @pl.when(pl.program_id(0) == 0)
        def _():
            st_ref[...] = jnp.zeros_like(st_ref)

        nwv, scale = nw_ref[...], 1.0 + sc_ref[...]
        sums = [jnp.zeros((1, D), F32)] * 3
        for half in range(SPLIT):
            rows = pl.ds(half * (tm // SPLIT), tm // SPLIT)
            dh = _nn(dy_ref[rows, :], w_ref[...])
            xv = x_ref[rows, :]
            rstd = lax.rsqrt(jnp.mean(xv * xv, axis=-1, keepdims=True) + RMS_EPS)
            xh = xv * rstd
            dxh = dh * (nwv * scale)
            dx_ref[rows, :] = dres_ref[rows, :] + rstd * (dxh - xh * jnp.mean(dxh * xh, axis=-1, keepdims=True))
            dhx = dh * xh
            sums = [sums[0] + jnp.sum(dh, axis=0, keepdims=True), sums[1] + jnp.sum(dhx * nwv, axis=0, keepdims=True),
                    sums[2] + jnp.sum(dhx * scale, axis=0, keepdims=True)]
        for q in range(3):
            st_ref[pl.ds(q, 1), :] = st_ref[pl.ds(q, 1), :] + sums[q]

    row = lambda w_: pl.BlockSpec((tm, w_), lambda i: (i, 0))
    vec = _const_spec((1, D))
    return pl.pallas_call(
        body, name=name, grid=(S // tm,),
        out_shape=(jax.ShapeDtypeStruct((S, D), F32), jax.ShapeDtypeStruct((8, D), F32)),
        in_specs=[row(N), _const_spec((N, D)), row(D), row(D), vec, vec], out_specs=(row(D), _const_spec((8, D))),
        compiler_params=_params(("arbitrary",), VMEM_LIMIT),
    )(dy, wt, x, dres, nw, sc)


def _weight_grad(a, b, *, transpose_out, name):
    S, N = b.shape
    nb = N // 2 if N > 4096 else N
    tk = 512

    def body(a_ref, b_ref, out_ref, acc):
        k = pl.program_id(1)

        @pl.when(k == 0)
        def _():
            acc[...] = jnp.zeros_like(acc)

        acc[...] += _tn(a_ref[...], b_ref[...])

        @pl.when(k == pl.num_programs(1) - 1)
        def _():
            out_ref[...] = (acc[...].T if transpose_out else acc[...]).astype(BF16)

    out_block = pl.BlockSpec((nb, D), lambda n, k: (n, 0)) if transpose_out else pl.BlockSpec((D, nb), lambda n, k: (0, n))
    return pl.pallas_call(
        body, name=name, grid=(N // nb, S // tk),
        out_shape=jax.ShapeDtypeStruct((N, D) if transpose_out else (D, N), BF16),
        in_specs=[pl.BlockSpec((tk, D), lambda n, k: (k, 0)), pl.BlockSpec((tk, nb), lambda n, k: (k, n))],
        out_specs=out_block, scratch_shapes=[pltpu.VMEM((D, nb), F32)],
        compiler_params=_params(("parallel", "arbitrary"), VMEM_LIMIT),
    )(a, b)


def _adamw(w, g, m, v):
    m = ADAM_B1 * m + (1.0 - ADAM_B1) * g
    v = ADAM_B2 * v + (1.0 - ADAM_B2) * (g * g)
    m_hat = m / (1.0 - ADAM_B1 ** ADAM_STEP)
    v_hat = v / (1.0 - ADAM_B2 ** ADAM_STEP)
    delta = -ADAM_LR * (m_hat / (jnp.sqrt(v_hat) + ADAM_EPS) + ADAM_WD * w)
    return delta, m, v


def _adam_shard(parts, own, w, m, v, name):
    R = w.shape[0]
    tr = max(t for t in (16, 32, 64, 128, 192, 256) if R % t == 0)

    def body(p_ref, o_ref, w_ref, m_ref, v_ref, g_out, d_out, m_out, v_out):
        me = _my_index()
        g = jnp.zeros((tr, D), F32)
        for j in range(N_DEV):
            g = g + jnp.where(me == j, o_ref[...], p_ref[j]).astype(F32)
        delta, mn, vn = _adamw(w_ref[...], g, m_ref[...], v_ref[...])
        g_out[...] = g
        d_out[...] = delta
        m_out[...] = mn
        v_out[...] = vn

    row = pl.BlockSpec((tr, D), lambda i: (i, 0))
    shp = jax.ShapeDtypeStruct((R, D), F32)
    return pl.pallas_call(
        body, name=name, grid=(R // tr,), out_shape=(shp,) * 4,
        in_specs=[pl.BlockSpec((N_DEV, tr, D), lambda i: (0, i, 0)), row, row, row, row], out_specs=(row,) * 4,
        compiler_params=_params(("parallel",), VMEM_LIMIT),
    )(parts, own, w, m, v)


def _adam_ada_w(cond_t, dmod, w, m, v):
    ncol = w.shape[-1]
    tr = 512

    def body(c_ref, d_ref, w_ref, m_ref, v_ref, g_out, d_out, m_out, v_out):
        g = _nn(c_ref[...], d_ref[0])
        delta, mn, vn = _adamw(w_ref[0], g, m_ref[0], v_ref[0])
        g_out[0] = g
        d_out[0] = delta
        m_out[0] = mn
        v_out[0] = vn

    blk = pl.BlockSpec((1, tr, ncol), lambda l, i: (l, i, 0))
    shp = jax.ShapeDtypeStruct(w.shape, F32)
    return pl.pallas_call(
        body, name="adam_ada_w", grid=(DEPTH, D // tr), out_shape=(shp,) * 4,
        in_specs=[pl.BlockSpec((tr, LANES), lambda l, i: (i, 0)), pl.BlockSpec((1, LANES, ncol), lambda l, i: (l, 0, 0)), blk, blk, blk],
        out_specs=(blk,) * 4, compiler_params=_params(("parallel", "parallel"), VMEM_LIMIT),
    )(cond_t, dmod, w, m, v)


TILE_ROWS = 168


def _stat_sources():
    pairs = []
    for i in range(DEPTH):
        b = 32 * i
        for q, src in enumerate((b, b + 1, b + 8, b + 16, b + 17, b + 24)):
            pairs.append((6 * i + q, src))
        pairs.append((24 + i, b + 2))
        pairs.append((32 + i, b + 18))
    pairs += [(40, 128), (41, 129)]
    return pairs


def _small_exchange(tiles, w, m, v):
    loss_row, sink_row, sink_src = 41, 48, 136

    def body(s_ref, w_ref, m_ref, v_ref, dmod_out, g_out, d_out, m_out, v_out, loss_out, all_ref, tot_ref, send_sems, recv_sems):
        me = _my_index()
        all_ref[me] = s_ref[...]
        copies = []
        for k in range(1, N_DEV):
            dev, _ = _peer(k)
            cp = pltpu.make_async_remote_copy(src_ref=s_ref, dst_ref=all_ref.at[me], send_sem=send_sems.at[k - 1],
                                              recv_sem=recv_sems.at[k - 1], device_id=dev, device_id_type=MESH)
            cp.start()
            copies.append(cp)
        for k in range(1, N_DEV):
            dev, pidx = _peer(k)
            pltpu.make_async_remote_copy(src_ref=s_ref, dst_ref=all_ref.at[pidx], send_sem=send_sems.at[k - 1],
                                         recv_sem=recv_sems.at[k - 1], device_id=dev, device_id_type=MESH).wait_recv()
        for cp in copies:
            cp.wait_send()
        tot = all_ref[0]
        for j in range(1, N_DEV):
            tot = tot + all_ref[j]
        tot_ref[...] = tot
        g_out[...] = jnp.zeros_like(g_out)
        for dst, src in _stat_sources():
            g_out[pl.ds(dst, 1), :] = tot_ref[pl.ds(src, 1), :]
            if dst < 24:
                for j in range(N_DEV):
                    dmod_out[j, pl.ds(dst, 1), :] = all_ref[j, pl.ds(src, 1), :]
        lane = lax.broadcasted_iota(jnp.int32, (1, D), 1)
        sink = jnp.zeros((1, D), F32)
        for h in range(32):
            sink = jnp.where(lane == h, tot_ref[pl.ds(sink_src + h, 1), :], sink)
        g_out[pl.ds(sink_row, 1), :] = sink
        g = g_out[...]
        delta, mn, vn = _adamw(w_ref[...], g, m_ref[...], v_ref[...])
        d_out[...] = delta
        m_out[...] = mn
        v_out[...] = vn
        loss = jnp.sum(g[loss_row:loss_row + 1, :], axis=-1, keepdims=True) * (0.5 / D)
        loss_out[...] = jnp.broadcast_to(loss, loss_out.shape)

    vm = pl.BlockSpec(memory_space=pltpu.VMEM)
    shp = jax.ShapeDtypeStruct((STAT_ROWS, D), F32)
    return pl.pallas_call(
        body, name="small_exchange",
        out_shape=(jax.ShapeDtypeStruct((N_DEV, 24, D), F32), shp, shp, shp, shp, jax.ShapeDtypeStruct((8, LANES), F32)),
        in_specs=[vm] * 4, out_specs=(vm,) * 6,
        scratch_shapes=[pltpu.VMEM((N_DEV, TILE_ROWS, D), F32), pltpu.VMEM((TILE_ROWS, D), F32),
                        pltpu.SemaphoreType.DMA((N_DEV - 1,)), pltpu.SemaphoreType.DMA((N_DEV - 1,))],
        compiler_params=_params(vmem=VMEM_LIMIT),
    )(tiles, w, m, v)


def _to_rows(name, a):
    if name in ("ffn_in", "a_in", "b_in"):
        return a.T
    if name == "b_out":
        return a.T.reshape(-1, D)
    return a


def _from_rows(name, a):
    if name in ("ffn_in", "a_in", "b_in"):
        return a.T
    if name == "b_out":
        return a.reshape(-1, 512).T
    return a


def _rows8(a):
    return jnp.pad(a, ((0, 8 - a.shape[0]), (0, 0)))


def _pack_small(ada_b, norm_mix, norm_ffn, final_norm, sink):
    sink_row = jnp.pad(sink.reshape(1, -1), ((0, 0), (0, D - sink.size)))
    return jnp.concatenate([ada_b.reshape(24, D), _rows8(norm_mix), _rows8(norm_ffn), _rows8(final_norm.reshape(1, D)),
                            _rows8(sink_row)], axis=0)


def _unpack_small(a):
    return a[0:24].reshape(4, 6 * D), a[24:28], a[32:36], a[40], a[48, :32].reshape(2, 16)


def kernel(x, c, ada_w, ada_b, norm_mix, norm_ffn, ffn_w_in, ffn_w_out, a_w_in, a_w_out, a_sink, b_w_in, b_w_out, final_norm, loss_target, m_ada_w, m_ada_b, m_norm_mix, m_norm_ffn, m_ffn_w_in, m_ffn_w_out, m_a_w_in, m_a_w_out, m_a_sink, m_b_w_in, m_b_w_out, m_final_norm, v_ada_w, v_ada_b, v_norm_mix, v_norm_ffn, v_ffn_w_in, v_ffn_w_out, v_a_w_in, v_a_w_out, v_a_sink, v_b_w_in, v_b_w_out, v_final_norm):
    S = x.shape[1]
    x0 = x.reshape(S, D)
    target = loss_target.reshape(S, D)
    me = _my_index()
    ncol = ada_w.shape[-1]

    ada_b_mine = lax.dynamic_slice_in_dim(ada_b, me * ncol, ncol, axis=1)
    cond_all, parts = _cond_exchange(jnp.broadcast_to(c.reshape(1, D), (8, D)), ada_w, ada_b_mine)
    mod = lax.dynamic_index_in_dim(parts, me, axis=2, keepdims=False)
    mod = jnp.transpose(mod, (1, 0, 2)).reshape(DEPTH, 6, 1, D)

    weights = {"ffn_in": ffn_w_in, "ffn_out": ffn_w_out, "a_in": a_w_in, "a_out": a_w_out, "b_in": b_w_in, "b_out": b_w_out}
    shard = {(n, l): _to_rows(n, weights[n][l]).astype(BF16) for n, l, _ in SEGMENTS}
    first = [sg for sg in _layer_segments(0) if not sg[0].startswith("ffn")]
    gathered0 = _all_gather_weights([shard[(n, l)] for n, l, _ in first])
    W = {(n, l): g for (n, l, _), g in zip(first, gathered0)}
    groups = [[sg for sg in _layer_segments(0) if sg[0].startswith("ffn")], [sg for i in range(1, DEPTH) for sg in _layer_segments(i)]]
    gathers, order = [], gathered0[0]
    for q, segs in enumerate(groups):
        mine = jnp.concatenate([shard[(n, l)] for n, l, _ in segs], axis=0)
        zone = lax.empty((N_DEV, mine.shape[0], D), BF16)
        gathers.append(_exchange_start([mine], zone, [mine.shape[0]], [0], False, order, "weight_gather_start_%d" % q))
        order = gathers[-1][-1]
    gather_token = order[0:1, 0:1]

    def finish_gather(q, after):
        zone = _exchange_wait(gathers[q], after, "weight_gather_wait_%d" % q)
        offs, _ = _offsets(groups[q])
        for (n, l, rows), off in zip(groups[q], offs):
            full = lax.dynamic_update_slice(zone[:, off:off + rows], shard[(n, l)][None], (me, 0, 0))
            W[(n, l)] = full.reshape(D, 512) if n == "b_out" else full.reshape(N_DEV * rows, D)

    a_slopes, b_slopes = _slopes(16), _slopes(24)
    bias_a = _alibi_bias(a_slopes, A_HALF, 1)
    bias_b = [_alibi_bias(b_slopes[8 * g:8 * g + 8], B_HALF, dil) for g, dil in enumerate(B_DILS)]
    bias_b_fwd = [_alibi_bias(b_slopes[8 * g:8 * g + 8], B_HALF, dil, max(CHUNK, TQ * dil), both=True) for g, dil in enumerate(B_DILS)]
    a_geom = dict(C=A_QKV, r=1, half=A_HALF, qoff=0, koff=1024, voff=1280, n_units=2)
    b_geom = [dict(C=B_QKV, r=dil, half=B_HALF, qoff=512 * g, koff=1536 + 128 * g, voff=1920 + 128 * g, n_units=1)
              for g, dil in enumerate(B_DILS)]

    saved = []
    xcur = x0
    for i in range(DEPTH):
        j = i // 2
        sh1, sc1, g1, sh2, sc2, g2 = [mod[i, q] for q in range(6)]
        nm, nf = norm_mix[i].reshape(1, D), norm_ffn[i].reshape(1, D)
        if i == 0:
            nm = nm + gather_token
        if i == 1:
            finish_gather(1, xcur)
        if i % 2 == 0:
            sink_rep = jnp.repeat(jnp.repeat(a_sink[j], TQ).reshape(2, 1, 8 * TQ), 8, axis=1).reshape(16, 8 * TQ)
            h1, qkv = _proj(xcur, nm, sc1, sh1, W[("a_in", j)], ffn=False, name="proj_a")
            o, lse = _attn_fwd(qkv, bias_a, sink_rep, out_dtype=BF16, name="attn_a_fwd", **a_geom)
            x1, y1 = _gated_residual(o, W[("a_out", j)], xcur, g1, w_is_transposed=False, name="out_a")
        else:
            sink_rep = None
            h1, qkv = _proj(xcur, nm, sc1, sh1, W[("b_in", j)], ffn=False, name="proj_b")
            outs = [_attn_fwd(qkv, bias_b_fwd[g], None, out_dtype=F32, name="attn_b%d_fwd" % g, **b_geom[g]) for g in range(3)]
            o, lse = _merge_groups([t[0] for t in outs], [t[1] for t in outs])
            x1, y1 = _gated_residual(o, W[("b_out", j)], xcur, g1, w_is_transposed=True, name="out_b")
        if i == 0:
            finish_gather(0, x1)
        h2, gu, act = _proj(x1, nf, sc2, sh2, W[("ffn_in", i)], ffn=True, name="ffn_in")
        x2, y2 = _gated_residual(act, W[("ffn_out", i)], x1, g2, w_is_transposed=False, name="ffn_out")
        saved.append(dict(x0=xcur, h1=h1, qkv=qkv, o=o, lse=lse, y1=y1, x1=x1, h2=h2, gu=gu, act=act, y2=y2, sink=sink_rep))
        xcur = x2

    dx, head_stats = _loss_head(xcur, target, final_norm.reshape(1, D))

    dW = {}
    stat_tiles, dsink = [None] * DEPTH, [None] * 2
    exchanges = []
    start_token = None

    def start_exchange(segs):
        offs, total = _offsets(segs)
        own = jnp.concatenate([lax.dynamic_slice_in_dim(dW[(n, l)], me * rows, rows, axis=0) for n, l, rows in segs], axis=0)
        started = _exchange_start([dW[(n, l)] for n, l, _ in segs], lax.empty((N_DEV, total, D), BF16), [sg[2] for sg in segs],
                                  offs, True, own, "grad_exchange_start_%d" % len(exchanges))
        exchanges.append((segs, started, own))
        return started[-1][0:1, 0:1]

    for i in reversed(range(DEPTH)):
        j = i // 2
        sv = saved[i]
        sh1, sc1, g1, sh2, sc2, g2 = [mod[i, q] for q in range(6)]
        if start_token is not None:
            g2 = g2 + start_token
            start_token = None
        nm, nf = norm_mix[i].reshape(1, D), norm_ffn[i].reshape(1, D)
        dgu, dy2, st_g2 = _gate_bwd(dx, sv["y2"], g2, W[("ffn_out", i)], sv["gu"], w_is_transposed=False, name="ffn_out_bwd")
        dW[("ffn_out", i)] = _weight_grad(dy2, sv["act"], transpose_out=True, name="dw_ffn_out")
        dW[("ffn_in", i)] = _weight_grad(sv["h2"], dgu, transpose_out=True, name="dw_ffn_in")
        if i == 0:
            g1 = g1 + start_exchange([sg for sg in _layer_segments(0) if sg[0].startswith("ffn")])
        dx1, st_f = _norm_bwd(dgu, W[("ffn_in", i)], sv["x1"], dx, nf, sc2, name="ffn_in_bwd")
        if i % 2 == 0:
            do, dy1, st_g1 = _gate_bwd(dx1, sv["y1"], g1, W[("a_out", j)], None, w_is_transposed=False, name="out_a_bwd")
            dW[("a_out", j)] = _weight_grad(dy1, sv["o"], transpose_out=True, name="dw_a_out")
            dq, dk, dv, ds = _attn_bwd(sv["qkv"], bias_a, sv["sink"], sv["o"], do, sv["lse"], name="attn_a_bwd", **a_geom)
            dsink[j] = ds
            dqkv = jnp.concatenate([dq, dk[0].astype(BF16), dk[1].astype(BF16), dv[0].astype(BF16), dv[1].astype(BF16)], axis=1)
            dW[("a_in", j)] = _weight_grad(sv["h1"], dqkv, transpose_out=True, name="dw_a_in")
            dx0, st_m = _norm_bwd(dqkv, W[("a_in", j)], sv["x0"], dx1, nm, sc1, name="proj_a_bwd")
        else:
            do, dy1, st_g1 = _gate_bwd(dx1, sv["y1"], g1, W[("b_out", j)], None, w_is_transposed=True, name="out_b_bwd")
            dW[("b_out", j)] = _weight_grad(dy1, sv["o"], transpose_out=False, name="dw_b_out").reshape(N_DEV * 64, D)
            gr = [_attn_bwd(sv["qkv"], bias_b[g], None, sv["o"], do, sv["lse"], name="attn_b%d_bwd" % g, **b_geom[g]) for g in range(3)]
            dqkv = jnp.concatenate([t[0] for t in gr] + [t[1][0].astype(BF16) for t in gr] + [t[2][0].astype(BF16) for t in gr], axis=1)
            dW[("b_in", j)] = _weight_grad(sv["h1"], dqkv, transpose_out=True, name="dw_b_in")
            dx0, st_m = _norm_bwd(dqkv, W[("b_in", j)], sv["x0"], dx1, nm, sc1, name="proj_b_bwd")
        stat_tiles[i] = [st_m, st_g1, st_f, st_g2]
        if i > 0:
            start_token = start_exchange(_layer_segments(i))
        else:
            start_exchange([sg for sg in _layer_segments(0) if not sg[0].startswith("ffn")])
        dx = dx0
    grad_x = dx.reshape(1, S, D)

    masters = {"ffn_in": (ffn_w_in, m_ffn_w_in, v_ffn_w_in), "ffn_out": (ffn_w_out, m_ffn_w_out, v_ffn_w_out),
               "a_in": (a_w_in, m_a_w_in, v_a_w_in), "a_out": (a_w_out, m_a_w_out, v_a_w_out),
               "b_in": (b_w_in, m_b_w_in, v_b_w_in), "b_out": (b_w_out, m_b_w_out, v_b_w_out)}
    pieces = {}
    after = dx
    for segs, started, own in exchanges:
        offs, total = _offsets(segs)
        parts_g = _exchange_wait(started, after, "grad_exchange_wait_%d" % len(pieces))
        rows_wmv = [jnp.concatenate([_to_rows(n, masters[n][q][l]) for n, l, _ in segs], axis=0) for q in range(3)]
        res_rows = _adam_shard(parts_g, own, *rows_wmv, name="adam_%d" % total)
        after = res_rows[0]
        for q, kind in enumerate(("grad", "delta", "m", "v")):
            for (n, l, rows), off in zip(segs, offs):
                pieces[(kind, n, l)] = _from_rows(n, res_rows[q][off:off + rows])
    big = {(kind, n): jnp.stack([pieces[(kind, n, l)] for l in range(4 if n.startswith("ffn") else 2)])
           for kind in ("grad", "delta", "m", "v") for n in masters}

    tiles = jnp.concatenate([t for i in range(DEPTH) for t in stat_tiles[i]] + [head_stats]
                            + [jnp.pad(ds, ((0, 0), (0, D - LANES))) for ds in dsink], axis=0)
    small = [_pack_small(*t) for t in ((ada_b, norm_mix, norm_ffn, final_norm, a_sink),
                                       (m_ada_b, m_norm_mix, m_norm_ffn, m_final_norm, m_a_sink),
                                       (v_ada_b, v_norm_mix, v_norm_ffn, v_final_norm, v_a_sink))]
    dmod_all, sg, sd, sm, sv_, loss_tile = _small_exchange(tiles, *small)
    loss = loss_tile[0, 0]
    dmod_all = dmod_all.reshape(N_DEV, DEPTH, 6 * D)
    dmod_mine = lax.dynamic_slice_in_dim(dmod_all, me * ncol, ncol, axis=2)
    dmod_pad = jnp.pad(jnp.transpose(dmod_mine, (1, 0, 2)), ((0, 0), (0, LANES - N_DEV), (0, 0))).astype(BF16)
    cond_t = jnp.pad(cond_all.T, ((0, 0), (0, LANES - N_DEV))).astype(BF16)
    ada = _adam_ada_w(cond_t, dmod_pad, ada_w, m_ada_w, v_ada_w)

    outs = [loss, grad_x]
    small_res = [_unpack_small(t) for t in (sg, sd, sm, sv_)]
    for q, kind in enumerate(("grad", "delta", "m", "v")):
        ab, nm_, nf_, fn, sk = small_res[q]
        outs += [ada[q], ab, nm_, nf_, big[(kind, "ffn_in")], big[(kind, "ffn_out")], big[(kind, "a_in")], big[(kind, "a_out")],
                 sk, big[(kind, "b_in")], big[(kind, "b_out")], fn]
    return tuple(outs)
```

```python
import functools
import math

import numpy as np
import jax
import jax.numpy as jnp
from jax import lax
from jax.experimental import pallas as pl
from jax.experimental.pallas import tpu as pltpu

D = 1024
HEAD_DIM = 64
D_FF = 2816
DEPTH = 4
N_DEV = 8
A_QKV = 1536
B_QKV = 2304
A_HALF = 128
B_HALF = 64
B_DILS = (1, 4, 16)
RMS_EPS = 1e-6
NEG = -1e30
ADAM_LR = 0.001
ADAM_B1 = 0.9
ADAM_B2 = 0.999
ADAM_EPS = 1e-08
ADAM_WD = 0.01
ADAM_STEP = 10

LANES = 128
SPLIT = 2
TQ = 128
VMEM_LIMIT = 56 * 1024 * 1024
MESH = pl.DeviceIdType.MESH
F32 = jnp.float32
BF16 = jnp.bfloat16

SEGMENTS = ([("ffn_in", l, 704) for l in range(4)] + [("ffn_out", l, 352) for l in range(4)]
            + [("a_in", j, 192) for j in range(2)] + [("a_out", j, 128) for j in range(2)]
            + [("b_in", j, 288) for j in range(2)] + [("b_out", j, 64) for j in range(2)])
def _layer_segments(i):
    mixer = "a" if i % 2 == 0 else "b"
    return [s for s in SEGMENTS if (s[0].startswith("ffn") and s[1] == i) or (s[0].startswith(mixer + "_") and s[1] == i // 2)]


def _offsets(segs):
    rows = [s[2] for s in segs]
    return [sum(rows[:k]) for k in range(len(rows))], sum(rows)
STAT_ROWS = 56


def _nn(a, b):
    return jnp.dot(a, b, preferred_element_type=F32)


def _nt(a, b):
    return lax.dot_general(a, b, (((1,), (1,)), ((), ())), preferred_element_type=F32)


def _tn(a, b):
    return lax.dot_general(a, b, (((0,), (0,)), ((), ())), preferred_element_type=F32)


def _params(dims=None, vmem=None):
    kw = {}
    if dims is not None:
        kw["dimension_semantics"] = dims
    if vmem is not None:
        kw["vmem_limit_bytes"] = vmem
    return pltpu.CompilerParams(**kw)


def _my_index():
    return 4 * lax.axis_index("x") + 2 * lax.axis_index("y") + lax.axis_index("c")


def _peer(k):
    x, y, c = lax.axis_index("x"), lax.axis_index("y"), lax.axis_index("c")
    px, py, pc = x ^ ((k >> 2) & 1), y ^ ((k >> 1) & 1), c ^ (k & 1)
    return (px, py, pc), 4 * px + 2 * py + pc


def _const_spec(shape):
    nd = len(shape)
    return pl.BlockSpec(shape, lambda *_: (0,) * nd)


def _cond_exchange(c_tile, ada_w, ada_b_mine):
    ncol = ada_w.shape[-1]

    def body(c_ref, w_ref, b_ref, cond_ref, parts_ref, call_ref, mine_ref, send_sems, recv_sems):
        me = _my_index()
        call_ref[me] = c_ref[...]
        copies = []
        for k in range(1, N_DEV):
            dev, _ = _peer(k)
            cp = pltpu.make_async_remote_copy(src_ref=c_ref, dst_ref=call_ref.at[me], send_sem=send_sems.at[0, k - 1],
                                              recv_sem=recv_sems.at[0, k - 1], device_id=dev, device_id_type=MESH)
            cp.start()
            copies.append(cp)
        for k in range(1, N_DEV):
            _, pidx = _peer(k)
            pltpu.make_async_remote_copy(src_ref=c_ref, dst_ref=call_ref.at[pidx], send_sem=send_sems.at[0, k - 1],
                                         recv_sem=recv_sems.at[0, k - 1], device_id=_peer(k)[0], device_id_type=MESH).wait_recv()
        for cp in copies:
            cp.wait_send()
        row = lax.broadcasted_iota(jnp.int32, (N_DEV, D), 0)
        cmat = jnp.zeros((N_DEV, D), F32)
        for j in range(N_DEV):
            cmat = jnp.where(row == j, call_ref[j], cmat)
        cond = cmat * jax.nn.sigmoid(cmat)
        cond_ref[...] = cond
        cb = cond.astype(BF16)
        for l in range(DEPTH):
            mine_ref[l] = _nn(cb, w_ref[l].astype(BF16)) + b_ref[pl.ds(l, 1), :]
        parts_ref[me] = mine_ref[...]
        copies = []
        for k in range(1, N_DEV):
            dev, _ = _peer(k)
            cp = pltpu.make_async_remote_copy(src_ref=mine_ref, dst_ref=parts_ref.at[me], send_sem=send_sems.at[1, k - 1],
                                              recv_sem=recv_sems.at[1, k - 1], device_id=dev, device_id_type=MESH)
            cp.start()
            copies.append(cp)
        for k in range(1, N_DEV):
            dev, pidx = _peer(k)
            pltpu.make_async_remote_copy(src_ref=mine_ref, dst_ref=parts_ref.at[pidx], send_sem=send_sems.at[1, k - 1],
                                         recv_sem=recv_sems.at[1, k - 1], device_id=dev, device_id_type=MESH).wait_recv()
        for cp in copies:
            cp.wait_send()

    vm = pl.BlockSpec(memory_space=pltpu.VMEM)
    return pl.pallas_call(
        body, name="cond_exchange",
        out_shape=(jax.ShapeDtypeStruct((N_DEV, D), F32), jax.ShapeDtypeStruct((N_DEV, DEPTH, N_DEV, ncol), F32)),
        in_specs=[vm, vm, vm], out_specs=(vm, vm),
        scratch_shapes=[pltpu.VMEM((N_DEV, N_DEV, D), F32), pltpu.VMEM((DEPTH, N_DEV, ncol), F32),
                        pltpu.SemaphoreType.DMA((2, N_DEV - 1)), pltpu.SemaphoreType.DMA((2, N_DEV - 1))],
        compiler_params=_params(vmem=VMEM_LIMIT),
    )(c_tile, ada_w, ada_b_mine)[:2]


def _all_gather_weights(shards):
    n = len(shards)
    big = max(range(n), key=lambda s: shards[s].shape[0])
    total = sum(sh.shape[0] for sh in shards)
    assert N_DEV * shards[big].shape[0] >= total

    def body(*refs):
        ins, outs = refs[:n], refs[n:2 * n]
        local_sems, send_sems, recv_sems = refs[2 * n:]
        me = _my_index()
        local = []
        for s in range(n):
            rows = ins[s].shape[0]
            cp = pltpu.make_async_copy(ins[s], outs[s].at[pl.ds(me * rows, rows)], local_sems.at[s])
            cp.start()
            local.append(cp)
        for k in range(1, N_DEV):
            dev, _ = _peer(k)
            for s in range(n):
                rows = ins[s].shape[0]
                pltpu.make_async_remote_copy(src_ref=ins[s], dst_ref=outs[s].at[pl.ds(me * rows, rows)],
                                             send_sem=send_sems.at[k - 1], recv_sem=recv_sems.at[k - 1],
                                             device_id=dev, device_id_type=MESH).start()
        whole = outs[big].at[pl.ds(0, total)]
        for k in range(1, N_DEV):
            dev, _ = _peer(k)
            w = pltpu.make_async_remote_copy(src_ref=whole, dst_ref=whole, send_sem=send_sems.at[k - 1],
                                             recv_sem=recv_sems.at[k - 1], device_id=dev, device_id_type=MESH)
            w.wait_send()
            w.wait_recv()
        for cp in local:
            cp.wait()

    hbm = pl.BlockSpec(memory_space=pl.ANY)
    return pl.pallas_call(
        body, name="weight_all_gather",
        out_shape=tuple(jax.ShapeDtypeStruct((N_DEV * s.shape[0], D), s.dtype) for s in shards),
        in_specs=[hbm] * n, out_specs=tuple([hbm] * n),
        scratch_shapes=[pltpu.SemaphoreType.DMA((n,)), pltpu.SemaphoreType.DMA((N_DEV - 1,)),
                        pltpu.SemaphoreType.DMA((N_DEV - 1,))],
    )(*shards)


HBM = pl.BlockSpec(memory_space=pltpu.HBM)
SEM = pl.BlockSpec(memory_space=pltpu.SEMAPHORE)
EFFECT = pltpu.SideEffectType.DATAFLOW_SIDE_EFFECTING


def _exchange_start(srcs, landing, rows, offs, to_peer_rows, after, name):
    n = len(srcs)

    def body(*refs):
        src_refs, land_ref = refs[:n], refs[n]
        send_sems, recv_sems = refs[n + 2], refs[n + 3]
        token = refs[-1]
        me = _my_index()
        for k in range(1, N_DEV):
            dev, pidx = _peer(k)
            for q in range(n):
                src = src_refs[q].at[pl.ds(pidx * rows[q], rows[q])] if to_peer_rows else src_refs[q]
                pltpu.make_async_remote_copy(src_ref=src, dst_ref=land_ref.at[me, pl.ds(offs[q], rows[q])],
                                             send_sem=send_sems.at[k - 1], recv_sem=recv_sems.at[k - 1],
                                             device_id=dev, device_id_type=MESH).start()
        token[...] = jnp.zeros_like(token)

    arrays = list(srcs) + [landing]
    return pl.pallas_call(
        body, name=name,
        out_shape=(pltpu.SemaphoreType.DMA((N_DEV - 1,)), pltpu.SemaphoreType.DMA((N_DEV - 1,)),
                   *[pltpu.HBM(a.shape, a.dtype) for a in arrays], jax.ShapeDtypeStruct((8, LANES), F32)),
        in_specs=[HBM] * (n + 1) + [pl.BlockSpec(memory_space=pl.ANY)],
        out_specs=(SEM, SEM, *[HBM] * (n + 1), pl.BlockSpec(memory_space=pltpu.VMEM)),
        input_output_aliases={q: 2 + q for q in range(n + 1)},
        compiler_params=pltpu.CompilerParams(has_side_effects=EFFECT),
    )(*[pltpu.with_memory_space_constraint(a, pltpu.HBM) for a in arrays], after)


def _exchange_wait(started, after, name):
    send_sems, recv_sems = started[0], started[1]
    arrays = list(started[2:-1])
    n1 = len(arrays)

    def body(*refs):
        land_ref = refs[n1 - 1]
        sends, recvs = refs[n1], refs[n1 + 1]
        for k in range(1, N_DEV):
            dev, _ = _peer(k)
            w = pltpu.make_async_remote_copy(src_ref=land_ref.at[0], dst_ref=land_ref.at[0], send_sem=sends.at[k - 1],
                                             recv_sem=recvs.at[k - 1], device_id=dev, device_id_type=MESH)
            w.wait_send()
            w.wait_recv()

    return pl.pallas_call(
        body, name=name, out_shape=tuple(pltpu.HBM(a.shape, a.dtype) for a in arrays),
        in_specs=[HBM] * n1 + [SEM, SEM, pl.BlockSpec(memory_space=pl.ANY)], out_specs=tuple([HBM] * n1),
        input_output_aliases={q: q for q in range(n1)},
        compiler_params=pltpu.CompilerParams(has_side_effects=EFFECT),
    )(*arrays, send_sems, recv_sems, after)[n1 - 1]


def _norm_mod(x, nw, sc, sh):
    ms = jnp.mean(x * x, axis=-1, keepdims=True)
    xh = x * lax.rsqrt(ms + RMS_EPS)
    return xh, (xh * nw) * (1.0 + sc) + sh


def _proj(x, nw, sc, sh, wt, *, ffn, name):
    S, N = x.shape[0], wt.shape[0]
    tm = 256 if ffn else 512

    def body(x_ref, nw_ref, sc_ref, sh_ref, w_ref, h_ref, out_ref, *act_ref):
        split = 1 if ffn else SPLIT
        for half in range(split):
            rows = pl.ds(half * (tm // split), tm // split)
            _, h = _norm_mod(x_ref[rows, :], nw_ref[...], sc_ref[...], sh_ref[...])
            hb = h.astype(BF16)
            h_ref[rows, :] = hb
            if ffn:
                gate = _nt(hb, w_ref[pl.ds(0, D_FF), :])
                up = _nt(hb, w_ref[pl.ds(D_FF, D_FF), :])
                sig = jax.nn.sigmoid(gate)
                silu = gate * sig
                out_ref[rows, pl.ds(0, D_FF)] = (up * (sig * (1.0 + gate * (1.0 - sig)))).astype(BF16)
                out_ref[rows, pl.ds(D_FF, D_FF)] = silu.astype(BF16)
                act_ref[0][rows, :] = (silu * up).astype(BF16)
            else:
                out_ref[rows, :] = _nt(hb, w_ref[...]).astype(BF16)

    row = lambda w: pl.BlockSpec((tm, w), lambda i: (i, 0))
    out_shape = [jax.ShapeDtypeStruct((S, D), BF16), jax.ShapeDtypeStruct((S, N), BF16)]
    out_specs = [row(D), row(N)]
    if ffn:
        out_shape.append(jax.ShapeDtypeStruct((S, D_FF), BF16))
        out_specs.append(row(D_FF))
    vec = _const_spec((1, D))
    return pl.pallas_call(
        body, name=name, grid=(S // tm,), out_shape=tuple(out_shape),
        in_specs=[row(D), vec, vec, vec, _const_spec((N, D))], out_specs=tuple(out_specs),
        compiler_params=_params(("parallel",), VMEM_LIMIT),
    )(x, nw, sc, sh, wt)


def _gated_residual(a, w, x, g, *, w_is_transposed, name):
    S, K = a.shape
    tm = 512

    def body(a_ref, w_ref, x_ref, g_ref, xo_ref, y_ref):
        y = _nt(a_ref[...], w_ref[...]) if w_is_transposed else _nn(a_ref[...], w_ref[...])
        y_ref[...] = y.astype(BF16)
        xo_ref[...] = x_ref[...] + g_ref[...] * y

    row = lambda w_: pl.BlockSpec((tm, w_), lambda i: (i, 0))
    return pl.pallas_call(
        body, name=name, grid=(S // tm,),
        out_shape=(jax.ShapeDtypeStruct((S, D), F32), jax.ShapeDtypeStruct((S, D), BF16)),
        in_specs=[row(K), _const_spec(w.shape), row(D), _const_spec((1, D))], out_specs=(row(D), row(D)),
        compiler_params=_params(("parallel",), VMEM_LIMIT),
    )(a, w, x, g)


CHUNK = 1024


def _tile_rows(r, chunk=CHUNK):
    return min(TQ, chunk // r)


def _alibi_bias(slopes, half, dil, chunk=CHUNK, both=False):
    tq = _tile_rows(dil, chunk)
    tk = tq + 2 * half
    rel = np.arange(tk)[:, None] - half - np.arange(tq)[None, :]
    band = np.abs(rel) <= half
    dist = (dil * np.abs(rel)).astype(np.float32)
    tabs = [np.where(band, -np.float32(s) * dist, np.float32(NEG)).astype(np.float32) for s in slopes]
    out = []
    for u in range(0, len(tabs), 8):
        tab = np.concatenate(tabs[u:u + 8], axis=1)
        first, last = tab.copy(), tab.copy()
        first[:half] = NEG
        last[tk - half:] = NEG
        out += [tab, first, last]
        if both:
            last = last.copy()
            last[:half] = NEG
            out.append(last)
    return jnp.asarray(np.concatenate(out, axis=0))


def _slopes(n):
    return (2.0 ** (-8.0 * np.arange(1, n + 1) / n)).astype(np.float32)


def _head_masks(tq):
    lane = lax.broadcasted_iota(jnp.int32, (tq, LANES), 1)
    lo = lane < HEAD_DIM
    return lo, jnp.logical_not(lo)


def _stack_heads(tiles, lo, hi, scale):
    blocks = []
    for t in range(4):
        xf = tiles[t] if scale == 1.0 else tiles[t] * scale
        for a in range(2):
            xm = jnp.where(lo if a == 0 else hi, xf, 0.0)
            if a != t // 2:
                xm = pltpu.roll(xm, HEAD_DIM, 1)
            blocks.append(xm.astype(BF16))
    return jnp.concatenate(blocks, axis=0)


def _tile_from_columns(x8t, t, tq):
    r0 = HEAD_DIM * (t // 2)
    top = x8t[r0:r0 + HEAD_DIM, 2 * t * tq:(2 * t + 1) * tq]
    bot = x8t[r0:r0 + HEAD_DIM, (2 * t + 1) * tq:(2 * t + 2) * tq]
    return jnp.concatenate([top, bot], axis=0).T


def _attn_layout(S, C, r, half, qoff, koff, voff, chunk):
    hb = half * r
    per = chunk // hb
    nhb = S // hb
    main = lambda off: pl.BlockSpec((chunk, LANES), lambda u, i: (i, off // LANES + u))
    prev = lambda off: pl.BlockSpec((hb, LANES), lambda u, i: (jnp.maximum(i * per - 1, 0), off // LANES + u))
    nxt = lambda off: pl.BlockSpec((hb, LANES), lambda u, i: (jnp.minimum((i + 1) * per, nhb - 1), off // LANES + u))
    specs = [pl.BlockSpec((chunk, 4 * LANES), lambda u, i: (i, qoff // (4 * LANES) + u))]
    specs += [prev(koff), main(koff), nxt(koff), prev(voff), main(voff), nxt(voff)]
    return specs, hb


def _stage(dst, srcs):
    row = 0
    for src in srcs:
        n = src.shape[0]
        dst[pl.ds(row, n), :] = src[...].astype(F32)
        row += n


def _rows(start, n, r):
    return pl.ds(start, n, stride=r) if r > 1 else pl.ds(start, n)


def _attn_fwd(qkv, bias, sink, *, C, r, half, qoff, koff, voff, n_units, out_dtype, name):
    S = qkv.shape[0]
    chunk = max(CHUNK, TQ * r)
    tq = _tile_rows(r, chunk)
    tk = tq + 2 * half
    tiles = chunk // (r * tq)
    nsteps = S // chunk
    specs, hb = _attn_layout(S, C, r, half, qoff, koff, voff, chunk)
    use_sink = sink is not None

    def body(*refs):
        q_ref, kp, km, kn, vp, vm, vn, bias_ref = refs[:8]
        rest = list(refs[8:])
        sink_ref = rest.pop(0) if use_sink else None
        o_ref, lse_ref, qs, ks, vs, os_, ls = rest
        i = pl.program_id(1)
        if r > 1:
            for t in range(4):
                qs[t] = q_ref[:, pl.ds(t * LANES, LANES)].astype(F32)
        _stage(ks, [kp, km, kn])
        _stage(vs, [vp, vm, vn])
        lo, hi = _head_masks(tq)

        def tile_in(staged, ref, t, start):
            if r > 1:
                return staged[t, _rows(start, tq, r), :]
            return ref[pl.ds(start, tq), pl.ds(t * LANES, LANES)].astype(F32)

        ones = jnp.ones((16, tk), BF16)
        if use_sink:
            sk = sink_ref[pl.ds(0, 1), :]

        def chain(n, carry):
            rho, c = n // tiles, n % tiles
            start = c * (tq * r) + rho
            if r == 1:
                start = pl.multiple_of(start, tq)
            variant = jnp.where(jnp.logical_and(i == 0, c == 0), 1, 0) + jnp.where(
                jnp.logical_and(i == nsteps - 1, c == tiles - 1), 2, 0)
            k2 = ks[_rows(start, tk, r), :].astype(BF16)
            v2t = jnp.concatenate([vs[_rows(start, tk, r), :].T.astype(BF16), ones], axis=0)
            q8 = _stack_heads([tile_in(qs, q_ref, t, start) for t in range(4)], lo, hi, HEAD_DIM ** -0.5)
            s = _nt(k2, q8) + bias_ref[pl.ds(pl.multiple_of(variant * tk, 8), tk), :]
            m = jnp.max(s, axis=0, keepdims=True)
            if use_sink:
                m = jnp.maximum(m, sk)
            pv = _nn(v2t, jnp.exp(s - m).astype(BF16))
            l = pv[LANES:LANES + 1]
            if use_sink:
                l = l + jnp.exp(sk - m)
            o8t = pv[:LANES] / l
            lse8 = jnp.broadcast_to(m + jnp.log(l), (LANES, 8 * tq))
            for t in range(4):
                if r > 1:
                    os_[t, _rows(start, tq, r), :] = _tile_from_columns(o8t, t, tq)
                    ls[t, _rows(start, tq, r), :] = _tile_from_columns(lse8, t, tq)
                else:
                    o_ref[pl.ds(start, tq), pl.ds(t * LANES, LANES)] = _tile_from_columns(o8t, t, tq).astype(out_dtype)
                    lse_ref[pl.ds(start, tq), pl.ds(t * LANES, LANES)] = _tile_from_columns(lse8, t, tq)
            return carry

        lax.fori_loop(0, r * tiles, chain, 0, unroll=4)
        if r > 1:
            for t in range(4):
                o_ref[:, pl.ds(t * LANES, LANES)] = os_[t].astype(out_dtype)
                lse_ref[:, pl.ds(t * LANES, LANES)] = ls[t]

    in_specs = specs + [pl.BlockSpec((bias.shape[0] // n_units, 8 * tq), lambda u, i: (u, 0))]
    args = [qkv] * 7 + [bias]
    if use_sink:
        in_specs.append(pl.BlockSpec((8, 8 * tq), lambda u, i: (u, 0)))
        args.append(sink)
    wide = pl.BlockSpec((chunk, 4 * LANES), lambda u, i: (i, u))
    win = hb + chunk + hb
    big = lambda: pltpu.VMEM((4, chunk if r > 1 else 8, LANES), F32)
    return pl.pallas_call(
        body, name=name, grid=(n_units, nsteps),
        out_shape=(jax.ShapeDtypeStruct((S, n_units * 512), out_dtype), jax.ShapeDtypeStruct((S, n_units * 512), F32)),
        in_specs=in_specs, out_specs=(wide, wide),
        scratch_shapes=[big(), pltpu.VMEM((win, LANES), F32), pltpu.VMEM((win, LANES), F32), big(), big()],
        compiler_params=_params(("parallel", "parallel"), VMEM_LIMIT),
    )(*args)


def _attn_bwd(qkv, bias, sink, o, do, lse, *, C, r, half, qoff, koff, voff, n_units, name):
    S = qkv.shape[0]
    tq = _tile_rows(r)
    tk = tq + 2 * half
    tiles = CHUNK // (r * tq)
    nsteps = S // CHUNK
    specs, hb = _attn_layout(S, C, r, half, qoff, koff, voff, CHUNK)
    use_sink = sink is not None

    def body(*refs):
        q_ref, kp, km, kn, vp, vm, vn, bias_ref = refs[:8]
        rest = list(refs[8:])
        sink_ref = rest.pop(0) if use_sink else None
        o_ref, do_ref, lse_ref, dq_ref, dk_hbm, dv_hbm = rest[:6]
        rest = rest[6:]
        dsink_ref = rest.pop(0) if use_sink else None
        qs, ks, vs, os_, dos, ls, dqs, acck, accv, sem = rest
        u, i = pl.program_id(0), pl.program_id(1)

        @pl.when(i == 0)
        def _():
            acck[...] = jnp.zeros_like(acck)
            accv[...] = jnp.zeros_like(accv)
            if use_sink:
                dsink_ref[...] = jnp.zeros_like(dsink_ref)

        if r > 1:
            for t in range(4):
                cols = pl.ds(t * LANES, LANES)
                qs[t] = q_ref[:, cols].astype(F32)
                os_[t] = o_ref[:, cols].astype(F32)
                dos[t] = do_ref[:, cols].astype(F32)
                ls[t] = lse_ref[:, cols]
        _stage(ks, [kp, km, kn])
        _stage(vs, [vp, vm, vn])
        lo, hi = _head_masks(tq)

        def tile_in(staged, ref, t, start):
            if r > 1:
                return staged[t, _rows(start, tq, r), :]
            return ref[pl.ds(start, tq), pl.ds(t * LANES, LANES)].astype(F32)

        base = pl.multiple_of(i * CHUNK, CHUNK)
        if use_sink:
            sk = sink_ref[pl.ds(0, 1), :]

        def chain(n, carry):
            rho, c = n // tiles, n % tiles
            start = c * (tq * r) + rho
            if r == 1:
                start = pl.multiple_of(start, tq)
            variant = jnp.where(jnp.logical_and(i == 0, c == 0), 1, 0) + jnp.where(
                jnp.logical_and(i == nsteps - 1, c == tiles - 1), 2, 0)
            k2 = ks[_rows(start, tk, r), :].astype(BF16)
            v2 = vs[_rows(start, tk, r), :].astype(BF16)
            k2t = ks[_rows(start, tk, r), :].T.astype(BF16)
            q8 = _stack_heads([tile_in(qs, q_ref, t, start) for t in range(4)], lo, hi, HEAD_DIM ** -0.5)
            do_tiles = [tile_in(dos, do_ref, t, start) for t in range(4)]
            do8 = _stack_heads(do_tiles, lo, hi, 1.0)
            deltas, lses = [], []
            for t in range(4):
                prod_t = (do_tiles[t] * tile_in(os_, o_ref, t, start)).T
                lse_t = tile_in(ls, lse_ref, t, start).T
                for a in range(2):
                    deltas.append(jnp.sum(prod_t[a * HEAD_DIM:(a + 1) * HEAD_DIM], axis=0, keepdims=True))
                    lses.append(lse_t[a * HEAD_DIM:a * HEAD_DIM + 1])
            delta8 = jnp.concatenate(deltas, axis=1)
            lse8 = jnp.concatenate(lses, axis=1)
            s = _nt(k2, q8) + bias_ref[pl.ds(pl.multiple_of(variant * tk, 8), tk), :]
            p = jnp.exp(s - lse8)
            dp = _nt(v2, do8)
            dsb = (p * (dp - delta8)).astype(BF16)
            dq8t = _nn(k2t, dsb)
            for t in range(4):
                dq_t = _tile_from_columns(dq8t, t, tq) * (HEAD_DIM ** -0.5)
                if r > 1:
                    dqs[t, _rows(start, tq, r), :] = dq_t
                else:
                    dq_ref[pl.ds(start, tq), pl.ds(t * LANES, LANES)] = dq_t.astype(BF16)
            arow = base + start
            if r == 1:
                arow = pl.multiple_of(arow, tq)
            acck[_rows(arow, tk, r), :] = acck[_rows(arow, tk, r), :] + _nn(dsb, q8)
            accv[_rows(arow, tk, r), :] = accv[_rows(arow, tk, r), :] + _nn(p.astype(BF16), do8)
            if use_sink:
                e = jnp.exp(sk - lse8) * delta8
                for h in range(8):
                    part = -jnp.sum(e[:, h * tq:(h + 1) * tq], axis=1, keepdims=True)
                    dsink_ref[pl.ds(h, 1), :] = dsink_ref[pl.ds(h, 1), :] + part
            return carry

        lax.fori_loop(0, r * tiles, chain, 0, unroll=2)
        if r > 1:
            for t in range(4):
                dq_ref[:, pl.ds(t * LANES, LANES)] = dqs[t].astype(BF16)

        @pl.when(i == nsteps - 1)
        def _():
            ck = pltpu.make_async_copy(acck.at[pl.ds(hb, S)], dk_hbm.at[u], sem.at[0])
            cv = pltpu.make_async_copy(accv.at[pl.ds(hb, S)], dv_hbm.at[u], sem.at[1])
            ck.start()
            cv.start()
            ck.wait()
            cv.wait()

    wide = pl.BlockSpec((CHUNK, 4 * LANES), lambda u, i: (i, u))
    hbm = pl.BlockSpec(memory_space=pl.ANY)
    in_specs = specs + [pl.BlockSpec((3 * tk, 8 * tq), lambda u, i: (u, 0))]
    args = [qkv] * 7 + [bias]
    if use_sink:
        in_specs.append(pl.BlockSpec((8, 8 * tq), lambda u, i: (u, 0)))
        args.append(sink)
    in_specs += [wide, wide, wide]
    args += [o, do, lse]
    out_shape = [jax.ShapeDtypeStruct((S, n_units * 512), BF16), jax.ShapeDtypeStruct((n_units, S, LANES), F32),
                 jax.ShapeDtypeStruct((n_units, S, LANES), F32)]
    out_specs = [wide, hbm, hbm]
    if use_sink:
        out_shape.append(jax.ShapeDtypeStruct((n_units * 8, LANES), F32))
        out_specs.append(pl.BlockSpec((8, LANES), lambda u, i: (u, 0)))
    win = hb + CHUNK + hb
    big = lambda: pltpu.VMEM((4, CHUNK if r > 1 else 8, LANES), F32)
    res = pl.pallas_call(
        body, name=name, grid=(n_units, nsteps), out_shape=tuple(out_shape), in_specs=in_specs, out_specs=tuple(out_specs),
        scratch_shapes=[big(), pltpu.VMEM((win, LANES), F32), pltpu.VMEM((win, LANES), F32), big(), big(), big(), big(),
                        pltpu.VMEM((S + 2 * hb, LANES), F32), pltpu.VMEM((S + 2 * hb, LANES), F32), pltpu.SemaphoreType.DMA((2,))],
        compiler_params=_params(("arbitrary", "arbitrary"), VMEM_LIMIT),
    )(*args)
    return res[0], res[1], res[2], (res[3] if use_sink else None)


def _merge_groups(os_, lses):
    S, W = os_[0].shape
    tm = 512

    def body(o0, o1, o2, l0, l1, l2, o_ref, lse_ref):
        ls = [l0[...], l1[...], l2[...]]
        mx = jnp.maximum(jnp.maximum(ls[0], ls[1]), ls[2])
        es = [jnp.exp(l - mx) for l in ls]
        den = es[0] + es[1] + es[2]
        o = (es[0] / den) * o0[...] + (es[1] / den) * o1[...] + (es[2] / den) * o2[...]
        o_ref[...] = o.astype(BF16)
        lse_ref[...] = mx + jnp.log(den)

    row = pl.BlockSpec((tm, W), lambda i: (i, 0))
    return pl.pallas_call(
        body, name="merge_groups", grid=(S // tm,),
        out_shape=(jax.ShapeDtypeStruct((S, W), BF16), jax.ShapeDtypeStruct((S, W), F32)),
        in_specs=[row] * 6, out_specs=(row, row), compiler_params=_params(("parallel",), VMEM_LIMIT),
    )(*os_, *lses)


def _loss_head(x, target, fnw):
    S = x.shape[0]
    tm = 512

    def body(x_ref, t_ref, w_ref, dx_ref, st_ref):
        @pl.when(pl.program_id(0) == 0)
        def _():
            st_ref[...] = jnp.zeros_like(st_ref)

        xv = x_ref[...]
        rstd = lax.rsqrt(jnp.mean(xv * xv, axis=-1, keepdims=True) + RMS_EPS)
        xh = xv * rstd
        err = xh * w_ref[...] - t_ref[...]
        dy = err * (1.0 / D)
        dxh = dy * w_ref[...]
        dx_ref[...] = rstd * (dxh - xh * jnp.mean(dxh * xh, axis=-1, keepdims=True))
        st_ref[pl.ds(0, 1), :] = st_ref[pl.ds(0, 1), :] + jnp.sum(dy * xh, axis=0, keepdims=True)
        st_ref[pl.ds(1, 1), :] = st_ref[pl.ds(1, 1), :] + jnp.sum(err * err, axis=0, keepdims=True)

    row = pl.BlockSpec((tm, D), lambda i: (i, 0))
    return pl.pallas_call(
        body, name="loss_head", grid=(S // tm,),
        out_shape=(jax.ShapeDtypeStruct((S, D), F32), jax.ShapeDtypeStruct((8, D), F32)),
        in_specs=[row, row, _const_spec((1, D))], out_specs=(row, _const_spec((8, D))),
        compiler_params=_params(("arbitrary",), VMEM_LIMIT),
    )(x, target, fnw)


def _gate_bwd(dx, y, g, w, gu, *, w_is_transposed, name):
    S = dx.shape[0]
    K = w.shape[1] if w_is_transposed else w.shape[0]
    ffn = gu is not None
    tm = 512
    wout = 2 * K if ffn else K

    def body(dx_ref, y_ref, g_ref, w_ref, *rest):
        if ffn:
            gu_ref, da_ref, dyb_ref, st_ref = rest
        else:
            da_ref, dyb_ref, st_ref = rest

        @pl.when(pl.program_id(0) == 0)
        def _():
            st_ref[...] = jnp.zeros_like(st_ref)

        total = jnp.zeros((1, D), F32)
        split = 1 if ffn else SPLIT
        for half in range(split):
            rows = pl.ds(half * (tm // split), tm // split)
            dxv = dx_ref[rows, :]
            total = total + jnp.sum(dxv * y_ref[rows, :].astype(F32), axis=0, keepdims=True)
            dyb = (dxv * g_ref[...]).astype(BF16)
            dyb_ref[rows, :] = dyb
            da = _nn(dyb, w_ref[...]) if w_is_transposed else _nt(dyb, w_ref[...])
            if ffn:
                da_ref[rows, pl.ds(0, K)] = (da * gu_ref[rows, pl.ds(0, K)].astype(F32)).astype(BF16)
                da_ref[rows, pl.ds(K, K)] = (da * gu_ref[rows, pl.ds(K, K)].astype(F32)).astype(BF16)
            else:
                da_ref[rows, :] = da.astype(BF16)
        st_ref[pl.ds(0, 1), :] = st_ref[pl.ds(0, 1), :] + total

    row = lambda w_: pl.BlockSpec((tm, w_), lambda i: (i, 0))
    in_specs = [row(D), row(D), _const_spec((1, D)), _const_spec(w.shape)]
    args = [dx, y, g, w]
    if ffn:
        in_specs.append(row(wout))
        args.append(gu)
    return pl.pallas_call(
        body, name=name, grid=(S // tm,),
        out_shape=(jax.ShapeDtypeStruct((S, wout), BF16), jax.ShapeDtypeStruct((S, D), BF16), jax.ShapeDtypeStruct((8, D), F32)),
        in_specs=in_specs, out_specs=(row(wout), row(D), _const_spec((8, D))),
        compiler_params=_params(("arbitrary",), VMEM_LIMIT),
    )(*args)


def _norm_bwd(dy, wt, x, dres, nw, sc, *, name):
    S, N = dy.shape
    tm = 512

    def body(dy_ref, w_ref, x_ref, dres_ref, nw_ref, sc_ref, dx_ref, st_ref):
        @pl.when(pl.program_id(0) == 0)
        def _():
            st_ref[...] = jnp.zeros_like(st_ref)

        nwv, scale = nw_ref[...], 1.0 + sc_ref[...]
        sums = [jnp.zeros((1, D), F32)] * 3
        for half in range(SPLIT):
            rows = pl.ds(half * (tm // SPLIT), tm // SPLIT)
            dh = _nn(dy_ref[rows, :], w_ref[...])
            xv = x_ref[rows, :]
            rstd = lax.rsqrt(jnp.mean(xv * xv, axis=-1, keepdims=True) + RMS_EPS)
            xh = xv * rstd
            dxh = dh * (nwv * scale)
            dx_ref[rows, :] = dres_ref[rows, :] + rstd * (dxh - xh * jnp.mean(dxh * xh, axis=-1, keepdims=True))
            dhx = dh * xh
            sums = [sums[0] + jnp.sum(dh, axis=0, keepdims=True), sums[1] + jnp.sum(dhx * nwv, axis=0, keepdims=True),
                    sums[2] + jnp.sum(dhx * scale, axis=0, keepdims=True)]
        for q in range(3):
            st_ref[pl.ds(q, 1), :] = st_ref[pl.ds(q, 1), :] + sums[q]

    row = lambda w_: pl.BlockSpec((tm, w_), lambda i: (i, 0))
    vec = _const_spec((1, D))
    return pl.pallas_call(
        body, name=name, grid=(S // tm,),
        out_shape=(jax.ShapeDtypeStruct((S, D), F32), jax.ShapeDtypeStruct((8, D), F32)),
        in_specs=[row(N), _const_spec((N, D)), row(D), row(D), vec, vec], out_specs=(row(D), _const_spec((8, D))),
        compiler_params=_params(("arbitrary",), VMEM_LIMIT),
    )(dy, wt, x, dres, nw, sc)


def _weight_grad(a, b, *, transpose_out, name):
    S, N = b.shape
    nb = N // 2 if N > 4096 else N
    tk = 512

    def body(a_ref, b_ref, out_ref, acc):
        k = pl.program_id(1)

        @pl.when(k == 0)
        def _():
            acc[...] = jnp.zeros_like(acc)

        acc[...] += _tn(a_ref[...], b_ref[...])

        @pl.when(k == pl.num_programs(1) - 1)
        def _():
            out_ref[...] = (acc[...].T if transpose_out else acc[...]).astype(BF16)

    out_block = pl.BlockSpec((nb, D), lambda n, k: (n, 0)) if transpose_out else pl.BlockSpec((D, nb), lambda n, k: (0, n))
    return pl.pallas_call(
        body, name=name, grid=(N // nb, S // tk),
        out_shape=jax.ShapeDtypeStruct((N, D) if transpose_out else (D, N), BF16),
        in_specs=[pl.BlockSpec((tk, D), lambda n, k: (k, 0)), pl.BlockSpec((tk, nb), lambda n, k: (k, n))],
        out_specs=out_block, scratch_shapes=[pltpu.VMEM((D, nb), F32)],
        compiler_params=_params(("parallel", "arbitrary"), VMEM_LIMIT),
    )(a, b)


def _adamw(w, g, m, v):
    m = ADAM_B1 * m + (1.0 - ADAM_B1) * g
    v = ADAM_B2 * v + (1.0 - ADAM_B2) * (g * g)
    m_hat = m / (1.0 - ADAM_B1 ** ADAM_STEP)
    v_hat = v / (1.0 - ADAM_B2 ** ADAM_STEP)
    delta = -ADAM_LR * (m_hat / (jnp.sqrt(v_hat) + ADAM_EPS) + ADAM_WD * w)
    return delta, m, v


def _adam_shard(parts, own, w, m, v, name):
    R = w.shape[0]
    tr = max(t for t in (16, 32, 64, 128, 192, 256) if R % t == 0)

    def body(p_ref, o_ref, w_ref, m_ref, v_ref, g_out, d_out, m_out, v_out):
        me = _my_index()
        g = jnp.zeros((tr, D), F32)
        for j in range(N_DEV):
            g = g + jnp.where(me == j, o_ref[...], p_ref[j]).astype(F32)
        delta, mn, vn = _adamw(w_ref[...], g, m_ref[...], v_ref[...])
        g_out[...] = g
        d_out[...] = delta
        m_out[...] = mn
        v_out[...] = vn

    row = pl.BlockSpec((tr, D), lambda i: (i, 0))
    shp = jax.ShapeDtypeStruct((R, D), F32)
    return pl.pallas_call(
        body, name=name, grid=(R // tr,), out_shape=(shp,) * 4,
        in_specs=[pl.BlockSpec((N_DEV, tr, D), lambda i: (0, i, 0)), row, row, row, row], out_specs=(row,) * 4,
        compiler_params=_params(("parallel",), VMEM_LIMIT),
    )(parts, own, w, m, v)


def _adam_ada_w(cond_t, dmod, w, m, v):
    ncol = w.shape[-1]
    tr = 512

    def body(c_ref, d_ref, w_ref, m_ref, v_ref, g_out, d_out, m_out, v_out):
        g = _nn(c_ref[...], d_ref[0])
        delta, mn, vn = _adamw(w_ref[0], g, m_ref[0], v_ref[0])
        g_out[0] = g
        d_out[0] = delta
        m_out[0] = mn
        v_out[0] = vn

    blk = pl.BlockSpec((1, tr, ncol), lambda l, i: (l, i, 0))
    shp = jax.ShapeDtypeStruct(w.shape, F32)
    return pl.pallas_call(
        body, name="adam_ada_w", grid=(DEPTH, D // tr), out_shape=(shp,) * 4,
        in_specs=[pl.BlockSpec((tr, LANES), lambda l, i: (i, 0)), pl.BlockSpec((1, LANES, ncol), lambda l, i: (l, 0, 0)), blk, blk, blk],
        out_specs=(blk,) * 4, compiler_params=_params(("parallel", "parallel"), VMEM_LIMIT),
    )(cond_t, dmod, w, m, v)


TILE_ROWS = 168


def _stat_sources():
    pairs = []
    for i in range(DEPTH):
        b = 32 * i
        for q, src in enumerate((b, b + 1, b + 8, b + 16, b + 17, b + 24)):
            pairs.append((6 * i + q, src))
        pairs.append((24 + i, b + 2))
        pairs.append((32 + i, b + 18))
    pairs += [(40, 128), (41, 129)]
    return pairs


def _small_exchange(tiles, w, m, v):
    loss_row, sink_row, sink_src = 41, 48, 136

    def body(s_ref, w_ref, m_ref, v_ref, dmod_out, g_out, d_out, m_out, v_out, loss_out, all_ref, tot_ref, send_sems, recv_sems):
        me = _my_index()
        all_ref[me] = s_ref[...]
        copies = []
        for k in range(1, N_DEV):
            dev, _ = _peer(k)
            cp = pltpu.make_async_remote_copy(src_ref=s_ref, dst_ref=all_ref.at[me], send_sem=send_sems.at[k - 1],
                                              recv_sem=recv_sems.at[k - 1], device_id=dev, device_id_type=MESH)
            cp.start()
            copies.append(cp)
        for k in range(1, N_DEV):
            dev, pidx = _peer(k)
            pltpu.make_async_remote_copy(src_ref=s_ref, dst_ref=all_ref.at[pidx], send_sem=send_sems.at[k - 1],
                                         recv_sem=recv_sems.at[k - 1], device_id=dev, device_id_type=MESH).wait_recv()
        for cp in copies:
            cp.wait_send()
        tot = all_ref[0]
        for j in range(1, N_DEV):
            tot = tot + all_ref[j]
        tot_ref[...] = tot
        g_out[...] = jnp.zeros_like(g_out)
        for dst, src in _stat_sources():
            g_out[pl.ds(dst, 1), :] = tot_ref[pl.ds(src, 1), :]
            if dst < 24:
                for j in range(N_DEV):
                    dmod_out[j, pl.ds(dst, 1), :] = all_ref[j, pl.ds(src, 1), :]
        lane = lax.broadcasted_iota(jnp.int32, (1, D), 1)
        sink = jnp.zeros((1, D), F32)
        for h in range(32):
            sink = jnp.where(lane == h, tot_ref[pl.ds(sink_src + h, 1), :], sink)
        g_out[pl.ds(sink_row, 1), :] = sink
        g = g_out[...]
        delta, mn, vn = _adamw(w_ref[...], g, m_ref[...], v_ref[...])
        d_out[...] = delta
        m_out[...] = mn
        v_out[...] = vn
        loss = jnp.sum(g[loss_row:loss_row + 1, :], axis=-1, keepdims=True) * (0.5 / D)
        loss_out[...] = jnp.broadcast_to(loss, loss_out.shape)

    vm = pl.BlockSpec(memory_space=pltpu.VMEM)
    shp = jax.ShapeDtypeStruct((STAT_ROWS, D), F32)
    return pl.pallas_call(
        body, name="small_exchange",
        out_shape=(jax.ShapeDtypeStruct((N_DEV, 24, D), F32), shp, shp, shp, shp, jax.ShapeDtypeStruct((8, LANES), F32)),
        in_specs=[vm] * 4, out_specs=(vm,) * 6,
        scratch_shapes=[pltpu.VMEM((N_DEV, TILE_ROWS, D), F32), pltpu.VMEM((TILE_ROWS, D), F32),
                        pltpu.SemaphoreType.DMA((N_DEV - 1,)), pltpu.SemaphoreType.DMA((N_DEV - 1,))],
        compiler_params=_params(vmem=VMEM_LIMIT),
    )(tiles, w, m, v)


def _to_rows(name, a):
    if name in ("ffn_in", "a_in", "b_in"):
        return a.T
    if name == "b_out":
        return a.T.reshape(-1, D)
    return a


def _from_rows(name, a):
    if name in ("ffn_in", "a_in", "b_in"):
        return a.T
    if name == "b_out":
        return a.reshape(-1, 512).T
    return a


def _rows8(a):
    return jnp.pad(a, ((0, 8 - a.shape[0]), (0, 0)))


def _pack_small(ada_b, norm_mix, norm_ffn, final_norm, sink):
    sink_row = jnp.pad(sink.reshape(1, -1), ((0, 0), (0, D - sink.size)))
    return jnp.concatenate([ada_b.reshape(24, D), _rows8(norm_mix), _rows8(norm_ffn), _rows8(final_norm.reshape(1, D)),
                            _rows8(sink_row)], axis=0)


def _unpack_small(a):
    return a[0:24].reshape(4, 6 * D), a[24:28], a[32:36], a[40], a[48, :32].reshape(2, 16)


def kernel(x, c, ada_w, ada_b, norm_mix, norm_ffn, ffn_w_in, ffn_w_out, a_w_in, a_w_out, a_sink, b_w_in, b_w_out, final_norm, loss_target, m_ada_w, m_ada_b, m_norm_mix, m_norm_ffn, m_ffn_w_in, m_ffn_w_out, m_a_w_in, m_a_w_out, m_a_sink, m_b_w_in, m_b_w_out, m_final_norm, v_ada_w, v_ada_b, v_norm_mix, v_norm_ffn, v_ffn_w_in, v_ffn_w_out, v_a_w_in, v_a_w_out, v_a_sink, v_b_w_in, v_b_w_out, v_final_norm):
    S = x.shape[1]
    x0 = x.reshape(S, D)
    target = loss_target.reshape(S, D)
    me = _my_index()
    ncol = ada_w.shape[-1]

    ada_b_mine = lax.dynamic_slice_in_dim(ada_b, me * ncol, ncol, axis=1)
    cond_all, parts = _cond_exchange(jnp.broadcast_to(c.reshape(1, D), (8, D)), ada_w, ada_b_mine)
    mod = lax.dynamic_index_in_dim(parts, me, axis=2, keepdims=False)
    mod = jnp.transpose(mod, (1, 0, 2)).reshape(DEPTH, 6, 1, D)

    weights = {"ffn_in": ffn_w_in, "ffn_out": ffn_w_out, "a_in": a_w_in, "a_out": a_w_out, "b_in": b_w_in, "b_out": b_w_out}
    shard = {(n, l): _to_rows(n, weights[n][l]).astype(BF16) for n, l, _ in SEGMENTS}
    first = [sg for sg in _layer_segments(0) if not sg[0].startswith("ffn")]
    gathered0 = _all_gather_weights([shard[(n, l)] for n, l, _ in first])
    W = {(n, l): g for (n, l, _), g in zip(first, gathered0)}
    groups = [[sg for sg in _layer_segments(0) if sg[0].startswith("ffn")], [sg for i in range(1, DEPTH) for sg in _layer_segments(i)]]
    gathers, order = [], gathered0[0]
    for q, segs in enumerate(groups):
        mine = jnp.concatenate([shard[(n, l)] for n, l, _ in segs], axis=0)
        zone = lax.empty((N_DEV, mine.shape[0], D), BF16)
        gathers.append(_exchange_start([mine], zone, [mine.shape[0]], [0], False, order, "weight_gather_start_%d" % q))
        order = gathers[-1][-1]
    gather_token = order[0:1, 0:1]

    def finish_gather(q, after):
        zone = _exchange_wait(gathers[q], after, "weight_gather_wait_%d" % q)
        offs, _ = _offsets(groups[q])
        for (n, l, rows), off in zip(groups[q], offs):
            full = lax.dynamic_update_slice(zone[:, off:off + rows], shard[(n, l)][None], (me, 0, 0))
            W[(n, l)] = full.reshape(D, 512) if n == "b_out" else full.reshape(N_DEV * rows, D)

    a_slopes, b_slopes = _slopes(16), _slopes(24)
    bias_a = _alibi_bias(a_slopes, A_HALF, 1)
    bias_b = [_alibi_bias(b_slopes[8 * g:8 * g + 8], B_HALF, dil) for g, dil in enumerate(B_DILS)]
    bias_b_fwd = [_alibi_bias(b_slopes[8 * g:8 * g + 8], B_HALF, dil, max(CHUNK, TQ * dil), both=True) for g, dil in enumerate(B_DILS)]
    a_geom = dict(C=A_QKV, r=1, half=A_HALF, qoff=0, koff=1024, voff=1280, n_units=2)
    b_geom = [dict(C=B_QKV, r=dil, half=B_HALF, qoff=512 * g, koff=1536 + 128 * g, voff=1920 + 128 * g, n_units=1)
              for g, dil in enumerate(B_DILS)]

    saved = []
    xcur = x0
    for i in range(DEPTH):
        j = i // 2
        sh1, sc1, g1, sh2, sc2, g2 = [mod[i, q] for q in range(6)]
        nm, nf = norm_mix[i].reshape(1, D), norm_ffn[i].reshape(1, D)
        if i == 0:
            nm = nm + gather_token
        if i == 1:
            finish_gather(1, xcur)
        if i % 2 == 0:
            sink_rep = jnp.repeat(jnp.repeat(a_sink[j], TQ).reshape(2, 1, 8 * TQ), 8, axis=1).reshape(16, 8 * TQ)
            h1, qkv = _proj(xcur, nm, sc1, sh1, W[("a_in", j)], ffn=False, name="proj_a")
            o, lse = _attn_fwd(qkv, bias_a, sink_rep, out_dtype=BF16, name="attn_a_fwd", **a_geom)
            x1, y1 = _gated_residual(o, W[("a_out", j)], xcur, g1, w_is_transposed=False, name="out_a")
        else:
            sink_rep = None
            h1, qkv = _proj(xcur, nm, sc1, sh1, W[("b_in", j)], ffn=False, name="proj_b")
            outs = [_attn_fwd(qkv, bias_b_fwd[g], None, out_dtype=F32, name="attn_b%d_fwd" % g, **b_geom[g]) for g in range(3)]
            o, lse = _merge_groups([t[0] for t in outs], [t[1] for t in outs])
            x1, y1 = _gated_residual(o, W[("b_out", j)], xcur, g1, w_is_transposed=True, name="out_b")
        if i == 0:
            finish_gather(0, x1)
        h2, gu, act = _proj(x1, nf, sc2, sh2, W[("ffn_in", i)], ffn=True, name="ffn_in")
        x2, y2 = _gated_residual(act, W[("ffn_out", i)], x1, g2, w_is_transposed=False, name="ffn_out")
        saved.append(dict(x0=xcur, h1=h1, qkv=qkv, o=o, lse=lse, y1=y1, x1=x1, h2=h2, gu=gu, act=act, y2=y2, sink=sink_rep))
        xcur = x2

    dx, head_stats = _loss_head(xcur, target, final_norm.reshape(1, D))

    dW = {}
    stat_tiles, dsink = [None] * DEPTH, [None] * 2
    exchanges = []
    start_token = None

    def start_exchange(segs):
        offs, total = _offsets(segs)
        own = jnp.concatenate([lax.dynamic_slice_in_dim(dW[(n, l)], me * rows, rows, axis=0) for n, l, rows in segs], axis=0)
        started = _exchange_start([dW[(n, l)] for n, l, _ in segs], lax.empty((N_DEV, total, D), BF16), [sg[2] for sg in segs],
                                  offs, True, own, "grad_exchange_start_%d" % len(exchanges))
        exchanges.append((segs, started, own))
        return started[-1][0:1, 0:1]

    for i in reversed(range(DEPTH)):
        j = i // 2
        sv = saved[i]
        sh1, sc1, g1, sh2, sc2, g2 = [mod[i, q] for q in range(6)]
        if start_token is not None:
            g2 = g2 + start_token
            start_token = None
        nm, nf = norm_mix[i].reshape(1, D), norm_ffn[i].reshape(1, D)
        dgu, dy2, st_g2 = _gate_bwd(dx, sv["y2"], g2, W[("ffn_out", i)], sv["gu"], w_is_transposed=False, name="ffn_out_bwd")
        dW[("ffn_out", i)] = _weight_grad(dy2, sv["act"], transpose_out=True, name="dw_ffn_out")
        dW[("ffn_in", i)] = _weight_grad(sv["h2"], dgu, transpose_out=True, name="dw_ffn_in")
        if i == 0:
            g1 = g1 + start_exchange([sg for sg in _layer_segments(0) if sg[0].startswith("ffn")])
        dx1, st_f = _norm_bwd(dgu, W[("ffn_in", i)], sv["x1"], dx, nf, sc2, name="ffn_in_bwd")
        if i % 2 == 0:
            do, dy1, st_g1 = _gate_bwd(dx1, sv["y1"], g1, W[("a_out", j)], None, w_is_transposed=False, name="out_a_bwd")
            dW[("a_out", j)] = _weight_grad(dy1, sv["o"], transpose_out=True, name="dw_a_out")
            dq, dk, dv, ds = _attn_bwd(sv["qkv"], bias_a, sv["sink"], sv["o"], do, sv["lse"], name="attn_a_bwd", **a_geom)
            dsink[j] = ds
            dqkv = jnp.concatenate([dq, dk[0].astype(BF16), dk[1].astype(BF16), dv[0].astype(BF16), dv[1].astype(BF16)], axis=1)
            dW[("a_in", j)] = _weight_grad(sv["h1"], dqkv, transpose_out=True, name="dw_a_in")
            dx0, st_m = _norm_bwd(dqkv, W[("a_in", j)], sv["x0"], dx1, nm, sc1, name="proj_a_bwd")
        else:
            do, dy1, st_g1 = _gate_bwd(dx1, sv["y1"], g1, W[("b_out", j)], None, w_is_transposed=True, name="out_b_bwd")
            dW[("b_out", j)] = _weight_grad(dy1, sv["o"], transpose_out=False, name="dw_b_out").reshape(N_DEV * 64, D)
            gr = [_attn_bwd(sv["qkv"], bias_b[g], None, sv["o"], do, sv["lse"], name="attn_b%d_bwd" % g, **b_geom[g]) for g in range(3)]
            dqkv = jnp.concatenate([t[0] for t in gr] + [t[1][0].astype(BF16) for t in gr] + [t[2][0].astype(BF16) for t in gr], axis=1)
            dW[("b_in", j)] = _weight_grad(sv["h1"], dqkv, transpose_out=True, name="dw_b_in")
            dx0, st_m = _norm_bwd(dqkv, W[("b_in", j)], sv["x0"], dx1, nm, sc1, name="proj_b_bwd")
        stat_tiles[i] = [st_m, st_g1, st_f, st_g2]
        if i > 0:
            start_token = start_exchange(_layer_segments(i))
        else:
            start_exchange([sg for sg in _layer_segments(0) if not sg[0].startswith("ffn")])
        dx = dx0
    grad_x = dx.reshape(1, S, D)

    masters = {"ffn_in": (ffn_w_in, m_ffn_w_in, v_ffn_w_in), "ffn_out": (ffn_w_out, m_ffn_w_out, v_ffn_w_out),
               "a_in": (a_w_in, m_a_w_in, v_a_w_in), "a_out": (a_w_out, m_a_w_out, v_a_w_out),
               "b_in": (b_w_in, m_b_w_in, v_b_w_in), "b_out": (b_w_out, m_b_w_out, v_b_w_out)}
    pieces = {}
    after = dx
    for segs, started, own in exchanges:
        offs, total = _offsets(segs)
        parts_g = _exchange_wait(started, after, "grad_exchange_wait_%d" % len(pieces))
        rows_wmv = [jnp.concatenate([_to_rows(n, masters[n][q][l]) for n, l, _ in segs], axis=0) for q in range(3)]
        res_rows = _adam_shard(parts_g, own, *rows_wmv, name="adam_%d" % total)
        after = res_rows[0]
        for q, kind in enumerate(("grad", "delta", "m", "v")):
            for (n, l, rows), off in zip(segs, offs):
                pieces[(kind, n, l)] = _from_rows(n, res_rows[q][off:off + rows])
    big = {(kind, n): jnp.stack([pieces[(kind, n, l)] for l in range(4 if n.startswith("ffn") else 2)])
           for kind in ("grad", "delta", "m", "v") for n in masters}

    tiles = jnp.concatenate([t for i in range(DEPTH) for t in stat_tiles[i]] + [head_stats]
                            + [jnp.pad(ds, ((0, 0), (0, D - LANES))) for ds in dsink], axis=0)
    small = [_pack_small(*t) for t in ((ada_b, norm_mix, norm_ffn, final_norm, a_sink),
                                       (m_ada_b, m_norm_mix, m_norm_ffn, m_final_norm, m_a_sink),
                                       (v_ada_b, v_norm_mix, v_norm_ffn, v_final_norm, v_a_sink))]
    dmod_all, sg, sd, sm, sv_, loss_tile = _small_exchange(tiles, *small)
    loss = loss_tile[0, 0]
    dmod_all = dmod_all.reshape(N_DEV, DEPTH, 6 * D)
    dmod_mine = lax.dynamic_slice_in_dim(dmod_all, me * ncol, ncol, axis=2)
    dmod_pad = jnp.pad(jnp.transpose(dmod_mine, (1, 0, 2)), ((0, 0), (0, LANES - N_DEV), (0, 0))).astype(BF16)
    cond_t = jnp.pad(cond_all.T, ((0, 0), (0, LANES - N_DEV))).astype(BF16)
    ada = _adam_ada_w(cond_t, dmod_pad, ada_w, m_ada_w, v_ada_w)

    outs = [loss, grad_x]
    small_res = [_unpack_small(t) for t in (sg, sd, sm, sv_)]
    for q, kind in enumerate(("grad", "delta", "m", "v")):
        ab, nm_, nf_, fn, sk = small_res[q]
        outs += [ada[q], ab, nm_, nf_, big[(kind, "ffn_in")], big[(kind, "ffn_out")], big[(kind, "a_in")], big[(kind, "a_out")],
                 sk, big[(kind, "b_in")], big[(kind, "b_out")], fn]
    return tuple(outs)
```

```python
import functools
import math

import numpy as np
import jax
import jax.numpy as jnp
from jax import lax
from jax.experimental import pallas as pl
from jax.experimental.pallas import tpu as pltpu

D = 1024
HEAD_DIM = 64
D_FF = 2816
DEPTH = 4
N_DEV = 8
A_QKV = 1536
B_QKV = 2304
A_HALF = 128
B_HALF = 64
B_DILS = (1, 4, 16)
RMS_EPS = 1e-6
NEG = -1e30
ADAM_LR = 0.001
ADAM_B1 = 0.9
ADAM_B2 = 0.999
ADAM_EPS = 1e-08
ADAM_WD = 0.01
ADAM_STEP = 10

LANES = 128
SPLIT = 2
TQ = 128
VMEM_LIMIT = 56 * 1024 * 1024
MESH = pl.DeviceIdType.MESH
F32 = jnp.float32
BF16 = jnp.bfloat16

SEGMENTS = ([("ffn_in", l, 704) for l in range(4)] + [("ffn_out", l, 352) for l in range(4)]
            + [("a_in", j, 192) for j in range(2)] + [("a_out", j, 128) for j in range(2)]
            + [("b_in", j, 288) for j in range(2)] + [("b_out", j, 64) for j in range(2)])
def _layer_segments(i):
    mixer = "a" if i % 2 == 0 else "b"
    return [s for s in SEGMENTS if (s[0].startswith("ffn") and s[1] == i) or (s[0].startswith(mixer + "_") and s[1] == i // 2)]


def _offsets(segs):
    rows = [s[2] for s in segs]
    return [sum(rows[:k]) for k in range(len(rows))], sum(rows)
STAT_ROWS = 56


def _nn(a, b):
    return jnp.dot(a, b, preferred_element_type=F32)


def _nt(a, b):
    return lax.dot_general(a, b, (((1,), (1,)), ((), ())), preferred_element_type=F32)


def _tn(a, b):
    return lax.dot_general(a, b, (((0,), (0,)), ((), ())), preferred_element_type=F32)


def _params(dims=None, vmem=None):
    kw = {}
    if dims is not None:
        kw["dimension_semantics"] = dims
    if vmem is not None:
        kw["vmem_limit_bytes"] = vmem
    return pltpu.CompilerParams(**kw)


def _my_index():
    return 4 * lax.axis_index("x") + 2 * lax.axis_index("y") + lax.axis_index("c")


def _peer(k):
    x, y, c = lax.axis_index("x"), lax.axis_index("y"), lax.axis_index("c")
    px, py, pc = x ^ ((k >> 2) & 1), y ^ ((k >> 1) & 1), c ^ (k & 1)
    return (px, py, pc), 4 * px + 2 * py + pc


def _const_spec(shape):
    nd = len(shape)
    return pl.BlockSpec(shape, lambda *_: (0,) * nd)


def _cond_exchange(c_tile, ada_w, ada_b_mine):
    ncol = ada_w.shape[-1]

    def body(c_ref, w_ref, b_ref, cond_ref, parts_ref, call_ref, mine_ref, send_sems, recv_sems):
        me = _my_index()
        call_ref[me] = c_ref[...]
        copies = []
        for k in range(1, N_DEV):
            dev, _ = _peer(k)
            cp = pltpu.make_async_remote_copy(src_ref=c_ref, dst_ref=call_ref.at[me], send_sem=send_sems.at[0, k - 1],
                                              recv_sem=recv_sems.at[0, k - 1], device_id=dev, device_id_type=MESH)
            cp.start()
            copies.append(cp)
        for k in range(1, N_DEV):
            _, pidx = _peer(k)
            pltpu.make_async_remote_copy(src_ref=c_ref, dst_ref=call_ref.at[pidx], send_sem=send_sems.at[0, k - 1],
                                         recv_sem=recv_sems.at[0, k - 1], device_id=_peer(k)[0], device_id_type=MESH).wait_recv()
        for cp in copies:
            cp.wait_send()
        row = lax.broadcasted_iota(jnp.int32, (N_DEV, D), 0)
        cmat = jnp.zeros((N_DEV, D), F32)
        for j in range(N_DEV):
            cmat = jnp.where(row == j, call_ref[j], cmat)
        cond = cmat * jax.nn.sigmoid(cmat)
        cond_ref[...] = cond
        cb = cond.astype(BF16)
        for l in range(DEPTH):
            mine_ref[l] = _nn(cb, w_ref[l].astype(BF16)) + b_ref[pl.ds(l, 1), :]
        parts_ref[me] = mine_ref[...]
        copies = []
        for k in range(1, N_DEV):
            dev, _ = _peer(k)
            cp = pltpu.make_async_remote_copy(src_ref=mine_ref, dst_ref=parts_ref.at[me], send_sem=send_sems.at[1, k - 1],
                                              recv_sem=recv_sems.at[1, k - 1], device_id=dev, device_id_type=MESH)
            cp.start()
            copies.append(cp)
        for k in range(1, N_DEV):
            dev, pidx = _peer(k)
            pltpu.make_async_remote_copy(src_ref=mine_ref, dst_ref=parts_ref.at[pidx], send_sem=send_sems.at[1, k - 1],
                                         recv_sem=recv_sems.at[1, k - 1], device_id=dev, device_id_type=MESH).wait_recv()
        for cp in copies:
            cp.wait_send()

    vm = pl.BlockSpec(memory_space=pltpu.VMEM)
    return pl.pallas_call(
        body, name="cond_exchange",
        out_shape=(jax.ShapeDtypeStruct((N_DEV, D), F32), jax.ShapeDtypeStruct((N_DEV, DEPTH, N_DEV, ncol), F32)),
        in_specs=[vm, vm, vm], out_specs=(vm, vm),
        scratch_shapes=[pltpu.VMEM((N_DEV, N_DEV, D), F32), pltpu.VMEM((DEPTH, N_DEV, ncol), F32),
                        pltpu.SemaphoreType.DMA((2, N_DEV - 1)), pltpu.SemaphoreType.DMA((2, N_DEV - 1))],
        compiler_params=_params(vmem=VMEM_LIMIT),
    )(c_tile, ada_w, ada_b_mine)[:2]


def _all_gather_weights(shards):
    n = len(shards)
    big = max(range(n), key=lambda s: shards[s].shape[0])
    total = sum(sh.shape[0] for sh in shards)
    assert N_DEV * shards[big].shape[0] >= total

    def body(*refs):
        ins, outs = refs[:n], refs[n:2 * n]
        local_sems, send_sems, recv_sems = refs[2 * n:]
        me = _my_index()
        local = []
        for s in range(n):
            rows = ins[s].shape[0]
            cp = pltpu.make_async_copy(ins[s], outs[s].at[pl.ds(me * rows, rows)], local_sems.at[s])
            cp.start()
            local.append(cp)
        for k in range(1, N_DEV):
            dev, _ = _peer(k)
            for s in range(n):
                rows = ins[s].shape[0]
                pltpu.make_async_remote_copy(src_ref=ins[s], dst_ref=outs[s].at[pl.ds(me * rows, rows)],
                                             send_sem=send_sems.at[k - 1], recv_sem=recv_sems.at[k - 1],
                                             device_id=dev, device_id_type=MESH).start()
        whole = outs[big].at[pl.ds(0, total)]
        for k in range(1, N_DEV):
            dev, _ = _peer(k)
            w = pltpu.make_async_remote_copy(src_ref=whole, dst_ref=whole, send_sem=send_sems.at[k - 1],
                                             recv_sem=recv_sems.at[k - 1], device_id=dev, device_id_type=MESH)
            w.wait_send()
            w.wait_recv()
        for cp in local:
            cp.wait()

    hbm = pl.BlockSpec(memory_space=pl.ANY)
    return pl.pallas_call(
        body, name="weight_all_gather",
        out_shape=tuple(jax.ShapeDtypeStruct((N_DEV * s.shape[0], D), s.dtype) for s in shards),
        in_specs=[hbm] * n, out_specs=tuple([hbm] * n),
        scratch_shapes=[pltpu.SemaphoreType.DMA((n,)), pltpu.SemaphoreType.DMA((N_DEV - 1,)),
                        pltpu.SemaphoreType.DMA((N_DEV - 1,))],
    )(*shards)


HBM = pl.BlockSpec(memory_space=pltpu.HBM)
SEM = pl.BlockSpec(memory_space=pltpu.SEMAPHORE)
EFFECT = pltpu.SideEffectType.DATAFLOW_SIDE_EFFECTING


def _exchange_start(srcs, landing, rows, offs, to_peer_rows, after, name):
    n = len(srcs)

    def body(*refs):
        src_refs, land_ref = refs[:n], refs[n]
        send_sems, recv_sems = refs[n + 2], refs[n + 3]
        token = refs[-1]
        me = _my_index()
        for k in range(1, N_DEV):
            dev, pidx = _peer(k)
            for q in range(n):
                src = src_refs[q].at[pl.ds(pidx * rows[q], rows[q])] if to_peer_rows else src_refs[q]
                pltpu.make_async_remote_copy(src_ref=src, dst_ref=land_ref.at[me, pl.ds(offs[q], rows[q])],
                                             send_sem=send_sems.at[k - 1], recv_sem=recv_sems.at[k - 1],
                                             device_id=dev, device_id_type=MESH).start()
        token[...] = jnp.zeros_like(token)

    arrays = list(srcs) + [landing]
    return pl.pallas_call(
        body, name=name,
        out_shape=(pltpu.SemaphoreType.DMA((N_DEV - 1,)), pltpu.SemaphoreType.DMA((N_DEV - 1,)),
                   *[pltpu.HBM(a.shape, a.dtype) for a in arrays], jax.ShapeDtypeStruct((8, LANES), F32)),
        in_specs=[HBM] * (n + 1) + [pl.BlockSpec(memory_space=pl.ANY)],
        out_specs=(SEM, SEM, *[HBM] * (n + 1), pl.BlockSpec(memory_space=pltpu.VMEM)),
        input_output_aliases={q: 2 + q for q in range(n + 1)},
        compiler_params=pltpu.CompilerParams(has_side_effects=EFFECT),
    )(*[pltpu.with_memory_space_constraint(a, pltpu.HBM) for a in arrays], after)


def _exchange_wait(started, after, name):
    send_sems, recv_sems = started[0], started[1]
    arrays = list(started[2:-1])
    n1 = len(arrays)

    def body(*refs):
        land_ref = refs[n1 - 1]
        sends, recvs = refs[n1], refs[n1 + 1]
        for k in range(1, N_DEV):
            dev, _ = _peer(k)
            w = pltpu.make_async_remote_copy(src_ref=land_ref.at[0], dst_ref=land_ref.at[0], send_sem=sends.at[k - 1],
                                             recv_sem=recvs.at[k - 1], device_id=dev, device_id_type=MESH)
            w.wait_send()
            w.wait_recv()

    return pl.pallas_call(
        body, name=name, out_shape=tuple(pltpu.HBM(a.shape, a.dtype) for a in arrays),
        in_specs=[HBM] * n1 + [SEM, SEM, pl.BlockSpec(memory_space=pl.ANY)], out_specs=tuple([HBM] * n1),
        input_output_aliases={q: q for q in range(n1)},
        compiler_params=pltpu.CompilerParams(has_side_effects=EFFECT),
    )(*arrays, send_sems, recv_sems, after)[n1 - 1]


def _norm_mod(x, nw, sc, sh):
    ms = jnp.mean(x * x, axis=-1, keepdims=True)
    xh = x * lax.rsqrt(ms + RMS_EPS)
    return xh, (xh * nw) * (1.0 + sc) + sh


def _proj(x, nw, sc, sh, wt, *, ffn, name):
    S, N = x.shape[0], wt.shape[0]
    tm = 256 if ffn else 512

    def body(x_ref, nw_ref, sc_ref, sh_ref, w_ref, h_ref, out_ref, *act_ref):
        split = 1 if ffn else SPLIT
        for half in range(split):
            rows = pl.ds(half * (tm // split), tm // split)
            _, h = _norm_mod(x_ref[rows, :], nw_ref[...], sc_ref[...], sh_ref[...])
            hb = h.astype(BF16)
            h_ref[rows, :] = hb
            if ffn:
                gate = _nt(hb, w_ref[pl.ds(0, D_FF), :])
                up = _nt(hb, w_ref[pl.ds(D_FF, D_FF), :])
                sig = jax.nn.sigmoid(gate)
                silu = gate * sig
                out_ref[rows, pl.ds(0, D_FF)] = (up * (sig * (1.0 + gate * (1.0 - sig)))).astype(BF16)
                out_ref[rows, pl.ds(D_FF, D_FF)] = silu.astype(BF16)
                act_ref[0][rows, :] = (silu * up).astype(BF16)
            else:
                out_ref[rows, :] = _nt(hb, w_ref[...]).astype(BF16)

    row = lambda w: pl.BlockSpec((tm, w), lambda i: (i, 0))
    out_shape = [jax.ShapeDtypeStruct((S, D), BF16), jax.ShapeDtypeStruct((S, N), BF16)]
    out_specs = [row(D), row(N)]
    if ffn:
        out_shape.append(jax.ShapeDtypeStruct((S, D_FF), BF16))
        out_specs.append(row(D_FF))
    vec = _const_spec((1, D))
    return pl.pallas_call(
        body, name=name, grid=(S // tm,), out_shape=tuple(out_shape),
        in_specs=[row(D), vec, vec, vec, _const_spec((N, D))], out_specs=tuple(out_specs),
        compiler_params=_params(("parallel",), VMEM_LIMIT),
    )(x, nw, sc, sh, wt)


def _gated_residual(a, w, x, g, *, w_is_transposed, name):
    S, K = a.shape
    tm = 512

    def body(a_ref, w_ref, x_ref, g_ref, xo_ref, y_ref):
        y = _nt(a_ref[...], w_ref[...]) if w_is_transposed else _nn(a_ref[...], w_ref[...])
        y_ref[...] = y.astype(BF16)
        xo_ref[...] = x_ref[...] + g_ref[...] * y

    row = lambda w_: pl.BlockSpec((tm, w_), lambda i: (i, 0))
    return pl.pallas_call(
        body, name=name, grid=(S // tm,),
        out_shape=(jax.ShapeDtypeStruct((S, D), F32), jax.ShapeDtypeStruct((S, D), BF16)),
        in_specs=[row(K), _const_spec(w.shape), row(D), _const_spec((1, D))], out_specs=(row(D), row(D)),
        compiler_params=_params(("parallel",), VMEM_LIMIT),
    )(a, w, x, g)


CHUNK = 1024


def _tile_rows(r, chunk=CHUNK):
    return min(TQ, chunk // r)


def _alibi_bias(slopes, half, dil, chunk=CHUNK, both=False):
    tq = _tile_rows(dil, chunk)
    tk = tq + 2 * half
    rel = np.arange(tk)[:, None] - half - np.arange(tq)[None, :]
    band = np.abs(rel) <= half
    dist = (dil * np.abs(rel)).astype(np.float32)
    tabs = [np.where(band, -np.float32(s) * dist, np.float32(NEG)).astype(np.float32) for s in slopes]
    out = []
    for u in range(0, len(tabs), 8):
        tab = np.concatenate(tabs[u:u + 8], axis=1)
        first, last = tab.copy(), tab.copy()
        first[:half] = NEG
        last[tk - half:] = NEG
        out += [tab, first, last]
        if both:
            last = last.copy()
            last[:half] = NEG
            out.append(last)
    return jnp.asarray(np.concatenate(out, axis=0))


def _slopes(n):
    return (2.0 ** (-8.0 * np.arange(1, n + 1) / n)).astype(np.float32)


def _head_masks(tq):
    lane = lax.broadcasted_iota(jnp.int32, (tq, LANES), 1)
    lo = lane < HEAD_DIM
    return lo, jnp.logical_not(lo)


def _stack_heads(tiles, lo, hi, scale):
    blocks = []
    for t in range(4):
        xf = tiles[t] if scale == 1.0 else tiles[t] * scale
        for a in range(2):
            xm = jnp.where(lo if a == 0 else hi, xf, 0.0)
            if a != t // 2:
                xm = pltpu.roll(xm, HEAD_DIM, 1)
            blocks.append(xm.astype(BF16))
    return jnp.concatenate(blocks, axis=0)


def _tile_from_columns(x8t, t, tq):
    r0 = HEAD_DIM * (t // 2)
    top = x8t[r0:r0 + HEAD_DIM, 2 * t * tq:(2 * t + 1) * tq]
    bot = x8t[r0:r0 + HEAD_DIM, (2 * t + 1) * tq:(2 * t + 2) * tq]
    return jnp.concatenate([top, bot], axis=0).T


def _attn_layout(S, C, r, half, qoff, koff, voff, chunk):
    hb = half * r
    per = chunk // hb
    nhb = S // hb
    main = lambda off: pl.BlockSpec((chunk, LANES), lambda u, i: (i, off // LANES + u))
    prev = lambda off: pl.BlockSpec((hb, LANES), lambda u, i: (jnp.maximum(i * per - 1, 0), off // LANES + u))
    nxt = lambda off: pl.BlockSpec((hb, LANES), lambda u, i: (jnp.minimum((i + 1) * per, nhb - 1), off // LANES + u))
    specs = [pl.BlockSpec((chunk, 4 * LANES), lambda u, i: (i, qoff // (4 * LANES) + u))]
    specs += [prev(koff), main(koff), nxt(koff), prev(voff), main(voff), nxt(voff)]
    return specs, hb


def _stage(dst, srcs):
    row = 0
    for src in srcs:
        n = src.shape[0]
        dst[pl.ds(row, n), :] = src[...].astype(F32)
        row += n


def _rows(start, n, r):
    return pl.ds(start, n, stride=r) if r > 1 else pl.ds(start, n)


def _attn_fwd(qkv, bias, sink, *, C, r, half, qoff, koff, voff, n_units, out_dtype, name):
    S = qkv.shape[0]
    chunk = max(CHUNK, TQ * r)
    tq = _tile_rows(r, chunk)
    tk = tq + 2 * half
    tiles = chunk // (r * tq)
    nsteps = S // chunk
    specs, hb = _attn_layout(S, C, r, half, qoff, koff, voff, chunk)
    use_sink = sink is not None

    def body(*refs):
        q_ref, kp, km, kn, vp, vm, vn, bias_ref = refs[:8]
        rest = list(refs[8:])
        sink_ref = rest.pop(0) if use_sink else None
        o_ref, lse_ref, qs, ks, vs, os_, ls = rest
        i = pl.program_id(1)
        if r > 1:
            for t in range(4):
                qs[t] = q_ref[:, pl.ds(t * LANES, LANES)].astype(F32)
        _stage(ks, [kp, km, kn])
        _stage(vs, [vp, vm, vn])
        lo, hi = _head_masks(tq)

        def tile_in(staged, ref, t, start):
            if r > 1:
                return staged[t, _rows(start, tq, r), :]
            return ref[pl.ds(start, tq), pl.ds(t * LANES, LANES)].astype(F32)

        ones = jnp.ones((16, tk), BF16)
        if use_sink:
            sk = sink_ref[pl.ds(0, 1), :]

        def chain(n, carry):
            rho, c = n // tiles, n % tiles
            start = c * (tq * r) + rho
            if r == 1:
                start = pl.multiple_of(start, tq)
            variant = jnp.where(jnp.logical_and(i == 0, c == 0), 1, 0) + jnp.where(
                jnp.logical_and(i == nsteps - 1, c == tiles - 1), 2, 0)
            k2 = ks[_rows(start, tk, r), :].astype(BF16)
            v2t = jnp.concatenate([vs[_rows(start, tk, r), :].T.astype(BF16), ones], axis=0)
            q8 = _stack_heads([tile_in(qs, q_ref, t, start) for t in range(4)], lo, hi, HEAD_DIM ** -0.5)
            s = _nt(k2, q8) + bias_ref[pl.ds(pl.multiple_of(variant * tk, 8), tk), :]
            m = jnp.max(s, axis=0, keepdims=True)
            if use_sink:
                m = jnp.maximum(m, sk)
            pv = _nn(v2t, jnp.exp(s - m).astype(BF16))
            l = pv[LANES:LANES + 1]
            if use_sink:
                l = l + jnp.exp(sk - m)
            o8t = pv[:LANES] / l
            lse8 = jnp.broadcast_to(m + jnp.log(l), (LANES, 8 * tq))
            for t in range(4):
                if r > 1:
                    os_[t, _rows(start, tq, r), :] = _tile_from_columns(o8t, t, tq)
                    ls[t, _rows(start, tq, r), :] = _tile_from_columns(lse8, t, tq)
                else:
                    o_ref[pl.ds(start, tq), pl.ds(t * LANES, LANES)] = _tile_from_columns(o8t, t, tq).astype(out_dtype)
                    lse_ref[pl.ds(start, tq), pl.ds(t * LANES, LANES)] = _tile_from_columns(lse8, t, tq)
            return carry

        lax.fori_loop(0, r * tiles, chain, 0, unroll=4)
        if r > 1:
            for t in range(4):
                o_ref[:, pl.ds(t * LANES, LANES)] = os_[t].astype(out_dtype)
                lse_ref[:, pl.ds(t * LANES, LANES)] = ls[t]

    in_specs = specs + [pl.BlockSpec((bias.shape[0] // n_units, 8 * tq), lambda u, i: (u, 0))]
    args = [qkv] * 7 + [bias]
    if use_sink:
        in_specs.append(pl.BlockSpec((8, 8 * tq), lambda u, i: (u, 0)))
        args.append(sink)
    wide = pl.BlockSpec((chunk, 4 * LANES), lambda u, i: (i, u))
    win = hb + chunk + hb
    big = lambda: pltpu.VMEM((4, chunk if r > 1 else 8, LANES), F32)
    return pl.pallas_call(
        body, name=name, grid=(n_units, nsteps),
        out_shape=(jax.ShapeDtypeStruct((S, n_units * 512), out_dtype), jax.ShapeDtypeStruct((S, n_units * 512), F32)),
        in_specs=in_specs, out_specs=(wide, wide),
        scratch_shapes=[big(), pltpu.VMEM((win, LANES), F32), pltpu.VMEM((win, LANES), F32), big(), big()],
        compiler_params=_params(("parallel", "parallel"), VMEM_LIMIT),
    )(*args)


def _attn_bwd(qkv, bias, sink, o, do, lse, *, C, r, half, qoff, koff, voff, n_units, name):
    S = qkv.shape[0]
    tq = _tile_rows(r)
    tk = tq + 2 * half
    tiles = CHUNK // (r * tq)
    nsteps = S // CHUNK
    specs, hb = _attn_layout(S, C, r, half, qoff, koff, voff, CHUNK)
    use_sink = sink is not None

    def body(*refs):
        q_ref, kp, km, kn, vp, vm, vn, bias_ref = refs[:8]
        rest = list(refs[8:])
        sink_ref = rest.pop(0) if use_sink else None
        o_ref, do_ref, lse_ref, dq_ref, dk_hbm, dv_hbm = rest[:6]
        rest = rest[6:]
        dsink_ref = rest.pop(0) if use_sink else None
        qs, ks, vs, os_, dos, ls, dqs, acck, accv, sem = rest
        u, i = pl.program_id(0), pl.program_id(1)

        @pl.when(i == 0)
        def _():
            acck[...] = jnp.zeros_like(acck)
            accv[...] = jnp.zeros_like(accv)
            if use_sink:
                dsink_ref[...] = jnp.zeros_like(dsink_ref)

        if r > 1:
            for t in range(4):
                cols = pl.ds(t * LANES, LANES)
                qs[t] = q_ref[:, cols].astype(F32)
                os_[t] = o_ref[:, cols].astype(F32)
                dos[t] = do_ref[:, cols].astype(F32)
                ls[t] = lse_ref[:, cols]
        _stage(ks, [kp, km, kn])
        _stage(vs, [vp, vm, vn])
        lo, hi = _head_masks(tq)

        def tile_in(staged, ref, t, start):
            if r > 1:
                return staged[t, _rows(start, tq, r), :]
            return ref[pl.ds(start, tq), pl.ds(t * LANES, LANES)].astype(F32)

        base = pl.multiple_of(i * CHUNK, CHUNK)
        if use_sink:
            sk = sink_ref[pl.ds(0, 1), :]

        def chain(n, carry):
            rho, c = n // tiles, n % tiles
            start = c * (tq * r) + rho
            if r == 1:
                start = pl.multiple_of(start, tq)
            variant = jnp.where(jnp.logical_and(i == 0, c == 0), 1, 0) + jnp.where(
                jnp.logical_and(i == nsteps - 1, c == tiles - 1), 2, 0)
            k2 = ks[_rows(start, tk, r), :].astype(BF16)
            v2 = vs[_rows(start, tk, r), :].astype(BF16)
            k2t = ks[_rows(start, tk, r), :].T.astype(BF16)
            q8 = _stack_heads([tile_in(qs, q_ref, t, start) for t in range(4)], lo, hi, HEAD_DIM ** -0.5)
            do_tiles = [tile_in(dos, do_ref, t, start) for t in range(4)]
            do8 = _stack_heads(do_tiles, lo, hi, 1.0)
            deltas, lses = [], []
            for t in range(4):
                prod_t = (do_tiles[t] * tile_in(os_, o_ref, t, start)).T
                lse_t = tile_in(ls, lse_ref, t, start).T
                for a in range(2):
                    deltas.append(jnp.sum(prod_t[a * HEAD_DIM:(a + 1) * HEAD_DIM], axis=0, keepdims=True))
                    lses.append(lse_t[a * HEAD_DIM:a * HEAD_DIM + 1])
            delta8 = jnp.concatenate(deltas, axis=1)
            lse8 = jnp.concatenate(lses, axis=1)
            s = _nt(k2, q8) + bias_ref[pl.ds(pl.multiple_of(variant * tk, 8), tk), :]
            p = jnp.exp(s - lse8)
            dp = _nt(v2, do8)
            dsb = (p * (dp - delta8)).astype(BF16)
            dq8t = _nn(k2t, dsb)
            for t in range(4):
                dq_t = _tile_from_columns(dq8t, t, tq) * (HEAD_DIM ** -0.5)
                if r > 1:
                    dqs[t, _rows(start, tq, r), :] = dq_t
                else:
                    dq_ref[pl.ds(start, tq), pl.ds(t * LANES, LANES)] = dq_t.astype(BF16)
            arow = base + start
            if r == 1:
                arow = pl.multiple_of(arow, tq)
            acck[_rows(arow, tk, r), :] = acck[_rows(arow, tk, r), :] + _nn(dsb, q8)
            accv[_rows(arow, tk, r), :] = accv[_rows(arow, tk, r), :] + _nn(p.astype(BF16), do8)
            if use_sink:
                e = jnp.exp(sk - lse8) * delta8
                for h in range(8):
                    part = -jnp.sum(e[:, h * tq:(h + 1) * tq], axis=1, keepdims=True)
                    dsink_ref[pl.ds(h, 1), :] = dsink_ref[pl.ds(h, 1), :] + part
            return carry

        lax.fori_loop(0, r * tiles, chain, 0, unroll=2)
        if r > 1:
            for t in range(4):
                dq_ref[:, pl.ds(t * LANES, LANES)] = dqs[t].astype(BF16)

        @pl.when(i == nsteps - 1)
        def _():
            ck = pltpu.make_async_copy(acck.at[pl.ds(hb, S)], dk_hbm.at[u], sem.at[0])
            cv = pltpu.make_async_copy(accv.at[pl.ds(hb, S)], dv_hbm.at[u], sem.at[1])
            ck.start()
            cv.start()
            ck.wait()
            cv.wait()

    wide = pl.BlockSpec((CHUNK, 4 * LANES), lambda u, i: (i, u))
    hbm = pl.BlockSpec(memory_space=pl.ANY)
    in_specs = specs + [pl.BlockSpec((3 * tk, 8 * tq), lambda u, i: (u, 0))]
    args = [qkv] * 7 + [bias]
    if use_sink:
        in_specs.append(pl.BlockSpec((8, 8 * tq), lambda u, i: (u, 0)))
        args.append(sink)
    in_specs += [wide, wide, wide]
    args += [o, do, lse]
    out_shape = [jax.ShapeDtypeStruct((S, n_units * 512), BF16), jax.ShapeDtypeStruct((n_units, S, LANES), F32),
                 jax.ShapeDtypeStruct((n_units, S, LANES), F32)]
    out_specs = [wide, hbm, hbm]
    if use_sink:
        out_shape.append(jax.ShapeDtypeStruct((n_units * 8, LANES), F32))
        out_specs.append(pl.BlockSpec((8, LANES), lambda u, i: (u, 0)))
    win = hb + CHUNK + hb
    big = lambda: pltpu.VMEM((4, CHUNK if r > 1 else 8, LANES), F32)
    res = pl.pallas_call(
        body, name=name, grid=(n_units, nsteps), out_shape=tuple(out_shape), in_specs=in_specs, out_specs=tuple(out_specs),
        scratch_shapes=[big(), pltpu.VMEM((win, LANES), F32), pltpu.VMEM((win, LANES), F32), big(), big(), big(), big(),
                        pltpu.VMEM((S + 2 * hb, LANES), F32), pltpu.VMEM((S + 2 * hb, LANES), F32), pltpu.SemaphoreType.DMA((2,))],
        compiler_params=_params(("arbitrary", "arbitrary"), VMEM_LIMIT),
    )(*args)
    return res[0], res[1], res[2], (res[3] if use_sink else None)


def _merge_groups(os_, lses):
    S, W = os_[0].shape
    tm = 512

    def body(o0, o1, o2, l0, l1, l2, o_ref, lse_ref):
        ls = [l0[...], l1[...], l2[...]]
        mx = jnp.maximum(jnp.maximum(ls[0], ls[1]), ls[2])
        es = [jnp.exp(l - mx) for l in ls]
        den = es[0] + es[1] + es[2]
        o = (es[0] / den) * o0[...] + (es[1] / den) * o1[...] + (es[2] / den) * o2[...]
        o_ref[...] = o.astype(BF16)
        lse_ref[...] = mx + jnp.log(den)

    row = pl.BlockSpec((tm, W), lambda i: (i, 0))
    return pl.pallas_call(
        body, name="merge_groups", grid=(S // tm,),
        out_shape=(jax.ShapeDtypeStruct((S, W), BF16), jax.ShapeDtypeStruct((S, W), F32)),
        in_specs=[row] * 6, out_specs=(row, row), compiler_params=_params(("parallel",), VMEM_LIMIT),
    )(*os_, *lses)


def _loss_head(x, target, fnw):
    S = x.shape[0]
    tm = 512

    def body(x_ref, t_ref, w_ref, dx_ref, st_ref):
        @pl.when(pl.program_id(0) == 0)
        def _():
            st_ref[...] = jnp.zeros_like(st_ref)

        xv = x_ref[...]
        rstd = lax.rsqrt(jnp.mean(xv * xv, axis=-1, keepdims=True) + RMS_EPS)
        xh = xv * rstd
        err = xh * w_ref[...] - t_ref[...]
        dy = err * (1.0 / D)
        dxh = dy * w_ref[...]
        dx_ref[...] = rstd * (dxh - xh * jnp.mean(dxh * xh, axis=-1, keepdims=True))
        st_ref[pl.ds(0, 1), :] = st_ref[pl.ds(0, 1), :] + jnp.sum(dy * xh, axis=0, keepdims=True)
        st_ref[pl.ds(1, 1), :] = st_ref[pl.ds(1, 1), :] + jnp.sum(err * err, axis=0, keepdims=True)

    row = pl.BlockSpec((tm, D), lambda i: (i, 0))
    return pl.pallas_call(
        body, name="loss_head", grid=(S // tm,),
        out_shape=(jax.ShapeDtypeStruct((S, D), F32), jax.ShapeDtypeStruct((8, D), F32)),
        in_specs=[row, row, _const_spec((1, D))], out_specs=(row, _const_spec((8, D))),
        compiler_params=_params(("arbitrary",), VMEM_LIMIT),
    )(x, target, fnw)


def _gate_bwd(dx, y, g, w, gu, *, w_is_transposed, name):
    S = dx.shape[0]
    K = w.shape[1] if w_is_transposed else w.shape[0]
    ffn = gu is not None
    tm = 512
    wout = 2 * K if ffn else K

    def body(dx_ref, y_ref, g_ref, w_ref, *rest):
        if ffn:
            gu_ref, da_ref, dyb_ref, st_ref = rest
        else:
            da_ref, dyb_ref, st_ref = rest

        @pl.when(pl.program_id(0) == 0)
        def _():
            st_ref[...] = jnp.zeros_like(st_ref)

        total = jnp.zeros((1, D), F32)
        split = 1 if ffn else SPLIT
        for half in range(split):
            rows = pl.ds(half * (tm // split), tm // split)
            dxv = dx_ref[rows, :]
            total = total + jnp.sum(dxv * y_ref[rows, :].astype(F32), axis=0, keepdims=True)
            dyb = (dxv * g_ref[...]).astype(BF16)
            dyb_ref[rows, :] = dyb
            da = _nn(dyb, w_ref[...]) if w_is_transposed else _nt(dyb, w_ref[...])
            if ffn:
                da_ref[rows, pl.ds(0, K)] = (da * gu_ref[rows, pl.ds(0, K)].astype(F32)).astype(BF16)
                da_ref[rows, pl.ds(K, K)] = (da * gu_ref[rows, pl.ds(K, K)].astype(F32)).astype(BF16)
            else:
                da_ref[rows, :] = da.astype(BF16)
        st_ref[pl.ds(0, 1), :] = st_ref[pl.ds(0, 1), :] + total

    row = lambda w_: pl.BlockSpec((tm, w_), lambda i: (i, 0))
    in_specs = [row(D), row(D), _const_spec((1, D)), _const_spec(w.shape)]
    args = [dx, y, g, w]
    if ffn:
        in_specs.append(row(wout))
        args.append(gu)
    return pl.pallas_call(
        body, name=name, grid=(S // tm,),
        out_shape=(jax.ShapeDtypeStruct((S, wout), BF16), jax.ShapeDtypeStruct((S, D), BF16), jax.ShapeDtypeStruct((8, D), F32)),
        in_specs=in_specs, out_specs=(row(wout), row(D), _const_spec((8, D))),
        compiler_params=_params(("arbitrary",), VMEM_LIMIT),
    )(*args)


def _norm_bwd(dy, wt, x, dres, nw, sc, *, name):
    S, N = dy.shape
    tm = 512

    def body(dy_ref, w_ref, x_ref, dres_ref, nw_ref, sc_ref, dx_ref, st_ref):
        @pl.when(pl.program_id(0) == 0)
        def _():
            st_ref[...] = jnp.zeros_like(st_ref)

        nwv, scale = nw_ref[...], 1.0 + sc_ref[...]
        sums = [jnp.zeros((1, D), F32)] * 3
        for half in range(SPLIT):
            rows = pl.ds(half * (tm // SPLIT), tm // SPLIT)
            dh = _nn(dy_ref[rows, :], w_ref[...])
            xv = x_ref[rows, :]
            rstd = lax.rsqrt(jnp.mean(xv * xv, axis=-1, keepdims=True) + RMS_EPS)
            xh = xv * rstd
            dxh = dh * (nwv * scale)
            dx_ref[rows, :] = dres_ref[rows, :] + rstd * (dxh - xh * jnp.mean(dxh * xh, axis=-1, keepdims=True))
            dhx = dh * xh
            sums = [sums[0] + jnp.sum(dh, axis=0, keepdims=True), sums[1] + jnp.sum(dhx * nwv, axis=0, keepdims=True),
                    sums[2] + jnp.sum(dhx * scale, axis=0, keepdims=True)]
        for q in range(3):
            st_ref[pl.ds(q, 1), :] = st_ref[pl.ds(q, 1), :] + sums[q]

    row = lambda w_: pl.BlockSpec((tm, w_), lambda i: (i, 0))
    vec = _const_spec((1, D))
    return pl.pallas_call(
        body, name=name, grid=(S // tm,),
        out_shape=(jax.ShapeDtypeStruct((S, D), F32), jax.ShapeDtypeStruct((8, D), F32)),
        in_specs=[row(N), _const_spec((N, D)), row(D), row(D), vec, vec], out_specs=(row(D), _const_spec((8, D))),
        compiler_params=_params(("arbitrary",), VMEM_LIMIT),
    )(dy, wt, x, dres, nw, sc)


def _ffn_bwd(dx, y, g, w_out, gu, wt_in, x, nw, sc, *, name):
    S = dx.shape[0]
    K = w_out.shape[0]
    tm = 256

    def body(dx_ref, y_ref, g_ref, wo_ref, gu_ref, wi_ref, x_ref, nw_ref, sc_ref, dgu_ref, dyb_ref, dxo_ref, stg_ref, stf_ref):
        @pl.when(pl.program_id(0) == 0)
        def _():
            stg_ref[...] = jnp.zeros_like(stg_ref)
            stf_ref[...] = jnp.zeros_like(stf_ref)

        dxv = dx_ref[...]
        stg_ref[pl.ds(0, 1), :] = stg_ref[pl.ds(0, 1), :] + jnp.sum(dxv * y_ref[...].astype(F32), axis=0, keepdims=True)
        dyb = (dxv * g_ref[...]).astype(BF16)
        dyb_ref[...] = dyb
        da = _nt(dyb, wo_ref[...])
        dgate = (da * gu_ref[:, pl.ds(0, K)].astype(F32)).astype(BF16)
        dup = (da * gu_ref[:, pl.ds(K, K)].astype(F32)).astype(BF16)
        dgu_ref[:, pl.ds(0, K)] = dgate
        dgu_ref[:, pl.ds(K, K)] = dup
        dh = _nn(dgate, wi_ref[pl.ds(0, K), :]) + _nn(dup, wi_ref[pl.ds(K, K), :])
        xv = x_ref[...]
        rstd = lax.rsqrt(jnp.mean(xv * xv, axis=-1, keepdims=True) + RMS_EPS)
        xh = xv * rstd
        nwv, scale = nw_ref[...], 1.0 + sc_ref[...]
        dxh = dh * (nwv * scale)
        dxo_ref[...] = dxv + rstd * (dxh - xh * jnp.mean(dxh * xh, axis=-1, keepdims=True))
        dhx = dh * xh
        stf_ref[pl.ds(0, 1), :] = stf_ref[pl.ds(0, 1), :] + jnp.sum(dh, axis=0, keepdims=True)
        stf_ref[pl.ds(1, 1), :] = stf_ref[pl.ds(1, 1), :] + jnp.sum(dhx * nwv, axis=0, keepdims=True)
        stf_ref[pl.ds(2, 1), :] = stf_ref[pl.ds(2, 1), :] + jnp.sum(dhx * scale, axis=0, keepdims=True)

    row = lambda w_: pl.BlockSpec((tm, w_), lambda i: (i, 0))
    vec = _const_spec((1, D))
    st = jax.ShapeDtypeStruct((8, D), F32)
    return pl.pallas_call(
        body, name=name, grid=(S // tm,),
        out_shape=(jax.ShapeDtypeStruct((S, 2 * K), BF16), jax.ShapeDtypeStruct((S, D), BF16), jax.ShapeDtypeStruct((S, D), F32), st, st),
        in_specs=[row(D), row(D), vec, _const_spec(w_out.shape), row(2 * K), _const_spec(wt_in.shape), row(D), vec, vec],
        out_specs=(row(2 * K), row(D), row(D), _const_spec((8, D)), _const_spec((8, D))),
        compiler_params=_params(("arbitrary",), VMEM_LIMIT),
    )(dx, y, g, w_out, gu, wt_in, x, nw, sc)


def _weight_grad(a, b, *, transpose_out, name):
    S, N = b.shape
    nb = N // 2 if N > 4096 else N
    tk = 512

    def body(a_ref, b_ref, out_ref, acc):
        k = pl.program_id(1)

        @pl.when(k == 0)
        def _():
            acc[...] = jnp.zeros_like(acc)

        acc[...] += _tn(a_ref[...], b_ref[...])

        @pl.when(k == pl.num_programs(1) - 1)
        def _():
            out_ref[...] = (acc[...].T if transpose_out else acc[...]).astype(BF16)

    out_block = pl.BlockSpec((nb, D), lambda n, k: (n, 0)) if transpose_out else pl.BlockSpec((D, nb), lambda n, k: (0, n))
    return pl.pallas_call(
        body, name=name, grid=(N // nb, S // tk),
        out_shape=jax.ShapeDtypeStruct((N, D) if transpose_out else (D, N), BF16),
        in_specs=[pl.BlockSpec((tk, D), lambda n, k: (k, 0)), pl.BlockSpec((tk, nb), lambda n, k: (k, n))],
        out_specs=out_block, scratch_shapes=[pltpu.VMEM((D, nb), F32)],
        compiler_params=_params(("parallel", "arbitrary"), VMEM_LIMIT),
    )(a, b)


def _adamw(w, g, m, v):
    m = ADAM_B1 * m + (1.0 - ADAM_B1) * g
    v = ADAM_B2 * v + (1.0 - ADAM_B2) * (g * g)
    m_hat = m / (1.0 - ADAM_B1 ** ADAM_STEP)
    v_hat = v / (1.0 - ADAM_B2 ** ADAM_STEP)
    delta = -ADAM_LR * (m_hat / (jnp.sqrt(v_hat) + ADAM_EPS) + ADAM_WD * w)
    return delta, m, v


def _adam_shard(parts, own, w, m, v, name):
    R = w.shape[0]
    tr = max(t for t in (16, 32, 64, 128, 192, 256) if R % t == 0)

    def body(p_ref, o_ref, w_ref, m_ref, v_ref, g_out, d_out, m_out, v_out):
        me = _my_index()
        g = jnp.zeros((tr, D), F32)
        for j in range(N_DEV):
            g = g + jnp.where(me == j, o_ref[...], p_ref[j]).astype(F32)
        delta, mn, vn = _adamw(w_ref[...], g, m_ref[...], v_ref[...])
        g_out[...] = g
        d_out[...] = delta
        m_out[...] = mn
        v_out[...] = vn

    row = pl.BlockSpec((tr, D), lambda i: (i, 0))
    shp = jax.ShapeDtypeStruct((R, D), F32)
    return pl.pallas_call(
        body, name=name, grid=(R // tr,), out_shape=(shp,) * 4,
        in_specs=[pl.BlockSpec((N_DEV, tr, D), lambda i: (0, i, 0)), row, row, row, row], out_specs=(row,) * 4,
        compiler_params=_params(("parallel",), VMEM_LIMIT),
    )(parts, own, w, m, v)


def _adam_ada_w(cond_t, dmod, w, m, v):
    ncol = w.shape[-1]
    tr = 512

    def body(c_ref, d_ref, w_ref, m_ref, v_ref, g_out, d_out, m_out, v_out):
        g = _nn(c_ref[...], d_ref[0])
        delta, mn, vn = _adamw(w_ref[0], g, m_ref[0], v_ref[0])
        g_out[0] = g
        d_out[0] = delta
        m_out[0] = mn
        v_out[0] = vn

    blk = pl.BlockSpec((1, tr, ncol), lambda l, i: (l, i, 0))
    shp = jax.ShapeDtypeStruct(w.shape, F32)
    return pl.pallas_call(
        body, name="adam_ada_w", grid=(DEPTH, D // tr), out_shape=(shp,) * 4,
        in_specs=[pl.BlockSpec((tr, LANES), lambda l, i: (i, 0)), pl.BlockSpec((1, LANES, ncol), lambda l, i: (l, 0, 0)), blk, blk, blk],
        out_specs=(blk,) * 4, compiler_params=_params(("parallel", "parallel"), VMEM_LIMIT),
    )(cond_t, dmod, w, m, v)


TILE_ROWS = 168


def _stat_sources():
    pairs = []
    for i in range(DEPTH):
        b = 32 * i
        for q, src in enumerate((b, b + 1, b + 8, b + 16, b + 17, b + 24)):
            pairs.append((6 * i + q, src))
        pairs.append((24 + i, b + 2))
        pairs.append((32 + i, b + 18))
    pairs += [(40, 128), (41, 129)]
    return pairs


def _small_exchange(tiles, w, m, v):
    loss_row, sink_row, sink_src = 41, 48, 136

    def body(s_ref, w_ref, m_ref, v_ref, dmod_out, g_out, d_out, m_out, v_out, loss_out, all_ref, tot_ref, send_sems, recv_sems):
        me = _my_index()
        all_ref[me] = s_ref[...]
        copies = []
        for k in range(1, N_DEV):
            dev, _ = _peer(k)
            cp = pltpu.make_async_remote_copy(src_ref=s_ref, dst_ref=all_ref.at[me], send_sem=send_sems.at[k - 1],
                                              recv_sem=recv_sems.at[k - 1], device_id=dev, device_id_type=MESH)
            cp.start()
            copies.append(cp)
        for k in range(1, N_DEV):
            dev, pidx = _peer(k)
            pltpu.make_async_remote_copy(src_ref=s_ref, dst_ref=all_ref.at[pidx], send_sem=send_sems.at[k - 1],
                                         recv_sem=recv_sems.at[k - 1], device_id=dev, device_id_type=MESH).wait_recv()
        for cp in copies:
            cp.wait_send()
        tot = all_ref[0]
        for j in range(1, N_DEV):
            tot = tot + all_ref[j]
        tot_ref[...] = tot
        g_out[...] = jnp.zeros_like(g_out)
        for dst, src in _stat_sources():
            g_out[pl.ds(dst, 1), :] = tot_ref[pl.ds(src, 1), :]
            if dst < 24:
                for j in range(N_DEV):
                    dmod_out[j, pl.ds(dst, 1), :] = all_ref[j, pl.ds(src, 1), :]
        lane = lax.broadcasted_iota(jnp.int32, (1, D), 1)
        sink = jnp.zeros((1, D), F32)
        for h in range(32):
            sink = jnp.where(lane == h, tot_ref[pl.ds(sink_src + h, 1), :], sink)
        g_out[pl.ds(sink_row, 1), :] = sink
        g = g_out[...]
        delta, mn, vn = _adamw(w_ref[...], g, m_ref[...], v_ref[...])
        d_out[...] = delta
        m_out[...] = mn
        v_out[...] = vn
        loss = jnp.sum(g[loss_row:loss_row + 1, :], axis=-1, keepdims=True) * (0.5 / D)
        loss_out[...] = jnp.broadcast_to(loss, loss_out.shape)

    vm = pl.BlockSpec(memory_space=pltpu.VMEM)
    shp = jax.ShapeDtypeStruct((STAT_ROWS, D), F32)
    return pl.pallas_call(
        body, name="small_exchange",
        out_shape=(jax.ShapeDtypeStruct((N_DEV, 24, D), F32), shp, shp, shp, shp, jax.ShapeDtypeStruct((8, LANES), F32)),
        in_specs=[vm] * 4, out_specs=(vm,) * 6,
        scratch_shapes=[pltpu.VMEM((N_DEV, TILE_ROWS, D), F32), pltpu.VMEM((TILE_ROWS, D), F32),
                        pltpu.SemaphoreType.DMA((N_DEV - 1,)), pltpu.SemaphoreType.DMA((N_DEV - 1,))],
        compiler_params=_params(vmem=VMEM_LIMIT),
    )(tiles, w, m, v)


def _to_rows(name, a):
    if name in ("ffn_in", "a_in", "b_in"):
        return a.T
    if name == "b_out":
        return a.T.reshape(-1, D)
    return a


def _from_rows(name, a):
    if name in ("ffn_in", "a_in", "b_in"):
        return a.T
    if name == "b_out":
        return a.reshape(-1, 512).T
    return a


def _rows8(a):
    return jnp.pad(a, ((0, 8 - a.shape[0]), (0, 0)))


def _pack_small(ada_b, norm_mix, norm_ffn, final_norm, sink):
    sink_row = jnp.pad(sink.reshape(1, -1), ((0, 0), (0, D - sink.size)))
    return jnp.concatenate([ada_b.reshape(24, D), _rows8(norm_mix), _rows8(norm_ffn), _rows8(final_norm.reshape(1, D)),
                            _rows8(sink_row)], axis=0)


def _unpack_small(a):
    return a[0:24].reshape(4, 6 * D), a[24:28], a[32:36], a[40], a[48, :32].reshape(2, 16)


def kernel(x, c, ada_w, ada_b, norm_mix, norm_ffn, ffn_w_in, ffn_w_out, a_w_in, a_w_out, a_sink, b_w_in, b_w_out, final_norm, loss_target, m_ada_w, m_ada_b, m_norm_mix, m_norm_ffn, m_ffn_w_in, m_ffn_w_out, m_a_w_in, m_a_w_out, m_a_sink, m_b_w_in, m_b_w_out, m_final_norm, v_ada_w, v_ada_b, v_norm_mix, v_norm_ffn, v_ffn_w_in, v_ffn_w_out, v_a_w_in, v_a_w_out, v_a_sink, v_b_w_in, v_b_w_out, v_final_norm):
    S = x.shape[1]
    x0 = x.reshape(S, D)
    target = loss_target.reshape(S, D)
    me = _my_index()
    ncol = ada_w.shape[-1]

    ada_b_mine = lax.dynamic_slice_in_dim(ada_b, me * ncol, ncol, axis=1)
    cond_all, parts = _cond_exchange(jnp.broadcast_to(c.reshape(1, D), (8, D)), ada_w, ada_b_mine)
    mod = lax.dynamic_index_in_dim(parts, me, axis=2, keepdims=False)
    mod = jnp.transpose(mod, (1, 0, 2)).reshape(DEPTH, 6, 1, D)

    weights = {"ffn_in": ffn_w_in, "ffn_out": ffn_w_out, "a_in": a_w_in, "a_out": a_w_out, "b_in": b_w_in, "b_out": b_w_out}
    shard = {(n, l): _to_rows(n, weights[n][l]).astype(BF16) for n, l, _ in SEGMENTS}
    first = [sg for sg in _layer_segments(0) if not sg[0].startswith("ffn")]
    gathered0 = _all_gather_weights([shard[(n, l)] for n, l, _ in first])
    W = {(n, l): g for (n, l, _), g in zip(first, gathered0)}
    groups = [[sg for sg in _layer_segments(0) if sg[0].startswith("ffn")], [sg for i in range(1, DEPTH) for sg in _layer_segments(i)]]
    gathers, order = [], gathered0[0]
    for q, segs in enumerate(groups):
        mine = jnp.concatenate([shard[(n, l)] for n, l, _ in segs], axis=0)
        zone = lax.empty((N_DEV, mine.shape[0], D), BF16)
        gathers.append(_exchange_start([mine], zone, [mine.shape[0]], [0], False, order, "weight_gather_start_%d" % q))
        order = gathers[-1][-1]
    gather_token = order[0:1, 0:1]

    def finish_gather(q, after):
        zone = _exchange_wait(gathers[q], after, "weight_gather_wait_%d" % q)
        offs, _ = _offsets(groups[q])
        for (n, l, rows), off in zip(groups[q], offs):
            full = lax.dynamic_update_slice(zone[:, off:off + rows], shard[(n, l)][None], (me, 0, 0))
            W[(n, l)] = full.reshape(D, 512) if n == "b_out" else full.reshape(N_DEV * rows, D)

    a_slopes, b_slopes = _slopes(16), _slopes(24)
    bias_a = _alibi_bias(a_slopes, A_HALF, 1)
    bias_b = [_alibi_bias(b_slopes[8 * g:8 * g + 8], B_HALF, dil) for g, dil in enumerate(B_DILS)]
    bias_b_fwd = [_alibi_bias(b_slopes[8 * g:8 * g + 8], B_HALF, dil, max(CHUNK, TQ * dil), both=True) for g, dil in enumerate(B_DILS)]
    a_geom = dict(C=A_QKV, r=1, half=A_HALF, qoff=0, koff=1024, voff=1280, n_units=2)
    b_geom = [dict(C=B_QKV, r=dil, half=B_HALF, qoff=512 * g, koff=1536 + 128 * g, voff=1920 + 128 * g, n_units=1)
              for g, dil in enumerate(B_DILS)]

    saved = []
    xcur = x0
    for i in range(DEPTH):
        j = i // 2
        sh1, sc1, g1, sh2, sc2, g2 = [mod[i, q] for q in range(6)]
        nm, nf = norm_mix[i].reshape(1, D), norm_ffn[i].reshape(1, D)
        if i == 0:
            nm = nm + gather_token
        if i == 1:
            finish_gather(1, xcur)
        if i % 2 == 0:
            sink_rep = jnp.repeat(jnp.repeat(a_sink[j], TQ).reshape(2, 1, 8 * TQ), 8, axis=1).reshape(16, 8 * TQ)
            h1, qkv = _proj(xcur, nm, sc1, sh1, W[("a_in", j)], ffn=False, name="proj_a")
            o, lse = _attn_fwd(qkv, bias_a, sink_rep, out_dtype=BF16, name="attn_a_fwd", **a_geom)
            x1, y1 = _gated_residual(o, W[("a_out", j)], xcur, g1, w_is_transposed=False, name="out_a")
        else:
            sink_rep = None
            h1, qkv = _proj(xcur, nm, sc1, sh1, W[("b_in", j)], ffn=False, name="proj_b")
            outs = [_attn_fwd(qkv, bias_b_fwd[g], None, out_dtype=F32, name="attn_b%d_fwd" % g, **b_geom[g]) for g in range(3)]
            o, lse = _merge_groups([t[0] for t in outs], [t[1] for t in outs])
            x1, y1 = _gated_residual(o, W[("b_out", j)], xcur, g1, w_is_transposed=True, name="out_b")
        if i == 0:
            finish_gather(0, x1)
        h2, gu, act = _proj(x1, nf, sc2, sh2, W[("ffn_in", i)], ffn=True, name="ffn_in")
        x2, y2 = _gated_residual(act, W[("ffn_out", i)], x1, g2, w_is_transposed=False, name="ffn_out")
        saved.append(dict(x0=xcur, h1=h1, qkv=qkv, o=o, lse=lse, y1=y1, x1=x1, h2=h2, gu=gu, act=act, y2=y2, sink=sink_rep))
        xcur = x2

    dx, head_stats = _loss_head(xcur, target, final_norm.reshape(1, D))

    dW = {}
    stat_tiles, dsink = [None] * DEPTH, [None] * 2
    exchanges = []
    start_token = None

    def start_exchange(segs):
        offs, total = _offsets(segs)
        own = jnp.concatenate([lax.dynamic_slice_in_dim(dW[(n, l)], me * rows, rows, axis=0) for n, l, rows in segs], axis=0)
        started = _exchange_start([dW[(n, l)] for n, l, _ in segs], lax.empty((N_DEV, total, D), BF16), [sg[2] for sg in segs],
                                  offs, True, own, "grad_exchange_start_%d" % len(exchanges))
        exchanges.append((segs, started, own))
        return started[-1][0:1, 0:1]

    for i in reversed(range(DEPTH)):
        j = i // 2
        sv = saved[i]
        sh1, sc1, g1, sh2, sc2, g2 = [mod[i, q] for q in range(6)]
        if start_token is not None:
            g2 = g2 + start_token
            start_token = None
        nm, nf = norm_mix[i].reshape(1, D), norm_ffn[i].reshape(1, D)
        dgu, dy2, dx1, st_g2, st_f = _ffn_bwd(dx, sv["y2"], g2, W[("ffn_out", i)], sv["gu"], W[("ffn_in", i)], sv["x1"], nf, sc2,
                                              name="ffn_bwd")
        dW[("ffn_out", i)] = _weight_grad(dy2, sv["act"], transpose_out=True, name="dw_ffn_out")
        dW[("ffn_in", i)] = _weight_grad(sv["h2"], dgu, transpose_out=True, name="dw_ffn_in")
        if i == 0:
            g1 = g1 + start_exchange([sg for sg in _layer_segments(0) if sg[0].startswith("ffn")])
        if i % 2 == 0:
            do, dy1, st_g1 = _gate_bwd(dx1, sv["y1"], g1, W[("a_out", j)], None, w_is_transposed=False, name="out_a_bwd")
            dW[("a_out", j)] = _weight_grad(dy1, sv["o"], transpose_out=True, name="dw_a_out")
            dq, dk, dv, ds = _attn_bwd(sv["qkv"], bias_a, sv["sink"], sv["o"], do, sv["lse"], name="attn_a_bwd", **a_geom)
            dsink[j] = ds
            dqkv = jnp.concatenate([dq, dk[0].astype(BF16), dk[1].astype(BF16), dv[0].astype(BF16), dv[1].astype(BF16)], axis=1)
            dW[("a_in", j)] = _weight_grad(sv["h1"], dqkv, transpose_out=True, name="dw_a_in")
            dx0, st_m = _norm_bwd(dqkv, W[("a_in", j)], sv["x0"], dx1, nm, sc1, name="proj_a_bwd")
        else:
            do, dy1, st_g1 = _gate_bwd(dx1, sv["y1"], g1, W[("b_out", j)], None, w_is_transposed=True, name="out_b_bwd")
            dW[("b_out", j)] = _weight_grad(dy1, sv["o"], transpose_out=False, name="dw_b_out").reshape(N_DEV * 64, D)
            gr = [_attn_bwd(sv["qkv"], bias_b[g], None, sv["o"], do, sv["lse"], name="attn_b%d_bwd" % g, **b_geom[g]) for g in range(3)]
            dqkv = jnp.concatenate([t[0] for t in gr] + [t[1][0].astype(BF16) for t in gr] + [t[2][0].astype(BF16) for t in gr], axis=1)
            dW[("b_in", j)] = _weight_grad(sv["h1"], dqkv, transpose_out=True, name="dw_b_in")
            dx0, st_m = _norm_bwd(dqkv, W[("b_in", j)], sv["x0"], dx1, nm, sc1, name="proj_b_bwd")
        stat_tiles[i] = [st_m, st_g1, st_f, st_g2]
        if i > 0:
            start_token = start_exchange(_layer_segments(i))
        else:
            start_exchange([sg for sg in _layer_segments(0) if not sg[0].startswith("ffn")])
        dx = dx0
    grad_x = dx.reshape(1, S, D)

    masters = {"ffn_in": (ffn_w_in, m_ffn_w_in, v_ffn_w_in), "ffn_out": (ffn_w_out, m_ffn_w_out, v_ffn_w_out),
               "a_in": (a_w_in, m_a_w_in, v_a_w_in), "a_out": (a_w_out, m_a_w_out, v_a_w_out),
               "b_in": (b_w_in, m_b_w_in, v_b_w_in), "b_out": (b_w_out, m_b_w_out, v_b_w_out)}
    pieces = {}
    after = dx
    for segs, started, own in exchanges:
        offs, total = _offsets(segs)
        parts_g = _exchange_wait(started, after, "grad_exchange_wait_%d" % len(pieces))
        rows_wmv = [jnp.concatenate([_to_rows(n, masters[n][q][l]) for n, l, _ in segs], axis=0) for q in range(3)]
        res_rows = _adam_shard(parts_g, own, *rows_wmv, name="adam_%d" % total)
        after = res_rows[0]
        for q, kind in enumerate(("grad", "delta", "m", "v")):
            for (n, l, rows), off in zip(segs, offs):
                pieces[(kind, n, l)] = _from_rows(n, res_rows[q][off:off + rows])
    big = {(kind, n): jnp.stack([pieces[(kind, n, l)] for l in range(4 if n.startswith("ffn") else 2)])
           for kind in ("grad", "delta", "m", "v") for n in masters}

    tiles = jnp.concatenate([t for i in range(DEPTH) for t in stat_tiles[i]] + [head_stats]
                            + [jnp.pad(ds, ((0, 0), (0, D - LANES))) for ds in dsink], axis=0)
    small = [_pack_small(*t) for t in ((ada_b, norm_mix, norm_ffn, final_norm, a_sink),
                                       (m_ada_b, m_norm_mix, m_norm_ffn, m_final_norm, m_a_sink),
                                       (v_ada_b, v_norm_mix, v_norm_ffn, v_final_norm, v_a_sink))]
    dmod_all, sg, sd, sm, sv_, loss_tile = _small_exchange(tiles, *small)
    loss = loss_tile[0, 0]
    dmod_all = dmod_all.reshape(N_DEV, DEPTH, 6 * D)
    dmod_mine = lax.dynamic_slice_in_dim(dmod_all, me * ncol, ncol, axis=2)
    dmod_pad = jnp.pad(jnp.transpose(dmod_mine, (1, 0, 2)), ((0, 0), (0, LANES - N_DEV), (0, 0))).astype(BF16)
    cond_t = jnp.pad(cond_all.T, ((0, 0), (0, LANES - N_DEV))).astype(BF16)
    ada = _adam_ada_w(cond_t, dmod_pad, ada_w, m_ada_w, v_ada_w)

    outs = [loss, grad_x]
    small_res = [_unpack_small(t) for t in (sg, sd, sm, sv_)]
    for q, kind in enumerate(("grad", "delta", "m", "v")):
        ab, nm_, nf_, fn, sk = small_res[q]
        outs += [ada[q], ab, nm_, nf_, big[(kind, "ffn_in")], big[(kind, "ffn_out")], big[(kind, "a_in")], big[(kind, "a_out")],
                 sk, big[(kind, "b_in")], big[(kind, "b_out")], fn]
    return tuple(outs)
```

```python
import functools
import math

import numpy as np
import jax
import jax.numpy as jnp
from jax import lax
from jax.experimental import pallas as pl
from jax.experimental.pallas import tpu as pltpu

D = 1024
HEAD_DIM = 64
D_FF = 2816
DEPTH = 4
N_DEV = 8
A_QKV = 1536
B_QKV = 2304
A_HALF = 128
B_HALF = 64
B_DILS = (1, 4, 16)
RMS_EPS = 1e-6
NEG = -1e30
ADAM_LR = 0.001
ADAM_B1 = 0.9
ADAM_B2 = 0.999
ADAM_EPS = 1e-08
ADAM_WD = 0.01
ADAM_STEP = 10

LANES = 128
SPLIT = 2
TQ = 128
VMEM_LIMIT = 56 * 1024 * 1024
MESH = pl.DeviceIdType.MESH
F32 = jnp.float32
BF16 = jnp.bfloat16

SEGMENTS = ([("ffn_in", l, 704) for l in range(4)] + [("ffn_out", l, 352) for l in range(4)]
            + [("a_in", j, 192) for j in range(2)] + [("a_out", j, 128) for j in range(2)]
            + [("b_in", j, 288) for j in range(2)] + [("b_out", j, 64) for j in range(2)])
def _layer_segments(i):
    mixer = "a" if i % 2 == 0 else "b"
    return [s for s in SEGMENTS if (s[0].startswith("ffn") and s[1] == i) or (s[0].startswith(mixer + "_") and s[1] == i // 2)]


def _offsets(segs):
    rows = [s[2] for s in segs]
    return [sum(rows[:k]) for k in range(len(rows))], sum(rows)
STAT_ROWS = 56


def _nn(a, b):
    return jnp.dot(a, b, preferred_element_type=F32)


def _nt(a, b):
    return lax.dot_general(a, b, (((1,), (1,)), ((), ())), preferred_element_type=F32)


def _tn(a, b):
    return lax.dot_general(a, b, (((0,), (0,)), ((), ())), preferred_element_type=F32)


def _params(dims=None, vmem=None):
    kw = {}
    if dims is not None:
        kw["dimension_semantics"] = dims
    if vmem is not None:
        kw["vmem_limit_bytes"] = vmem
    return pltpu.CompilerParams(**kw)


def _my_index():
    return 4 * lax.axis_index("x") + 2 * lax.axis_index("y") + lax.axis_index("c")


def _peer(k):
    x, y, c = lax.axis_index("x"), lax.axis_index("y"), lax.axis_index("c")
    px, py, pc = x ^ ((k >> 2) & 1), y ^ ((k >> 1) & 1), c ^ (k & 1)
    return (px, py, pc), 4 * px + 2 * py + pc


def _const_spec(shape):
    nd = len(shape)
    return pl.BlockSpec(shape, lambda *_: (0,) * nd)


def _cond_exchange(c_tile, ada_w, ada_b_mine):
    ncol = ada_w.shape[-1]

    def body(c_ref, w_ref, b_ref, cond_ref, parts_ref, call_ref, mine_ref, send_sems, recv_sems):
        me = _my_index()
        call_ref[me] = c_ref[...]
        copies = []
        for k in range(1, N_DEV):
            dev, _ = _peer(k)
            cp = pltpu.make_async_remote_copy(src_ref=c_ref, dst_ref=call_ref.at[me], send_sem=send_sems.at[0, k - 1],
                                              recv_sem=recv_sems.at[0, k - 1], device_id=dev, device_id_type=MESH)
            cp.start()
            copies.append(cp)
        for k in range(1, N_DEV):
            _, pidx = _peer(k)
            pltpu.make_async_remote_copy(src_ref=c_ref, dst_ref=call_ref.at[pidx], send_sem=send_sems.at[0, k - 1],
                                         recv_sem=recv_sems.at[0, k - 1], device_id=_peer(k)[0], device_id_type=MESH).wait_recv()
        for cp in copies:
            cp.wait_send()
        row = lax.broadcasted_iota(jnp.int32, (N_DEV, D), 0)
        cmat = jnp.zeros((N_DEV, D), F32)
        for j in range(N_DEV):
            cmat = jnp.where(row == j, call_ref[j], cmat)
        cond = cmat * jax.nn.sigmoid(cmat)
        cond_ref[...] = cond
        cb = cond.astype(BF16)
        for l in range(DEPTH):
            mine_ref[l] = _nn(cb, w_ref[l].astype(BF16)) + b_ref[pl.ds(l, 1), :]
        parts_ref[me] = mine_ref[...]
        copies = []
        for k in range(1, N_DEV):
            dev, _ = _peer(k)
            cp = pltpu.make_async_remote_copy(src_ref=mine_ref, dst_ref=parts_ref.at[me], send_sem=send_sems.at[1, k - 1],
                                              recv_sem=recv_sems.at[1, k - 1], device_id=dev, device_id_type=MESH)
            cp.start()
            copies.append(cp)
        for k in range(1, N_DEV):
            dev, pidx = _peer(k)
            pltpu.make_async_remote_copy(src_ref=mine_ref, dst_ref=parts_ref.at[pidx], send_sem=send_sems.at[1, k - 1],
                                         recv_sem=recv_sems.at[1, k - 1], device_id=dev, device_id_type=MESH).wait_recv()
        for cp in copies:
            cp.wait_send()

    vm = pl.BlockSpec(memory_space=pltpu.VMEM)
    return pl.pallas_call(
        body, name="cond_exchange",
        out_shape=(jax.ShapeDtypeStruct((N_DEV, D), F32), jax.ShapeDtypeStruct((N_DEV, DEPTH, N_DEV, ncol), F32)),
        in_specs=[vm, vm, vm], out_specs=(vm, vm),
        scratch_shapes=[pltpu.VMEM((N_DEV, N_DEV, D), F32), pltpu.VMEM((DEPTH, N_DEV, ncol), F32),
                        pltpu.SemaphoreType.DMA((2, N_DEV - 1)), pltpu.SemaphoreType.DMA((2, N_DEV - 1))],
        compiler_params=_params(vmem=VMEM_LIMIT),
    )(c_tile, ada_w, ada_b_mine)[:2]


def _all_gather_weights(shards):
    n = len(shards)
    big = max(range(n), key=lambda s: shards[s].shape[0])
    total = sum(sh.shape[0] for sh in shards)
    assert N_DEV * shards[big].shape[0] >= total

    def body(*refs):
        ins, outs = refs[:n], refs[n:2 * n]
        local_sems, send_sems, recv_sems = refs[2 * n:]
        me = _my_index()
        local = []
        for s in range(n):
            rows = ins[s].shape[0]
            cp = pltpu.make_async_copy(ins[s], outs[s].at[pl.ds(me * rows, rows)], local_sems.at[s])
            cp.start()
            local.append(cp)
        for k in range(1, N_DEV):
            dev, _ = _peer(k)
            for s in range(n):
                rows = ins[s].shape[0]
                pltpu.make_async_remote_copy(src_ref=ins[s], dst_ref=outs[s].at[pl.ds(me * rows, rows)],
                                             send_sem=send_sems.at[k - 1], recv_sem=recv_sems.at[k - 1],
                                             device_id=dev, device_id_type=MESH).start()
        whole = outs[big].at[pl.ds(0, total)]
        for k in range(1, N_DEV):
            dev, _ = _peer(k)
            w = pltpu.make_async_remote_copy(src_ref=whole, dst_ref=whole, send_sem=send_sems.at[k - 1],
                                             recv_sem=recv_sems.at[k - 1], device_id=dev, device_id_type=MESH)
            w.wait_send()
            w.wait_recv()
        for cp in local:
            cp.wait()

    hbm = pl.BlockSpec(memory_space=pl.ANY)
    return pl.pallas_call(
        body, name="weight_all_gather",
        out_shape=tuple(jax.ShapeDtypeStruct((N_DEV * s.shape[0], D), s.dtype) for s in shards),
        in_specs=[hbm] * n, out_specs=tuple([hbm] * n),
        scratch_shapes=[pltpu.SemaphoreType.DMA((n,)), pltpu.SemaphoreType.DMA((N_DEV - 1,)),
                        pltpu.SemaphoreType.DMA((N_DEV - 1,))],
    )(*shards)


HBM = pl.BlockSpec(memory_space=pltpu.HBM)
SEM = pl.BlockSpec(memory_space=pltpu.SEMAPHORE)
EFFECT = pltpu.SideEffectType.DATAFLOW_SIDE_EFFECTING


def _exchange_start(srcs, landing, rows, offs, to_peer_rows, after, name):
    n = len(srcs)

    def body(*refs):
        src_refs, land_ref = refs[:n], refs[n]
        send_sems, recv_sems = refs[n + 2], refs[n + 3]
        token = refs[-1]
        me = _my_index()
        for k in range(1, N_DEV):
            dev, pidx = _peer(k)
            for q in range(n):
                src = src_refs[q].at[pl.ds(pidx * rows[q], rows[q])] if to_peer_rows else src_refs[q]
                pltpu.make_async_remote_copy(src_ref=src, dst_ref=land_ref.at[me, pl.ds(offs[q], rows[q])],
                                             send_sem=send_sems.at[k - 1], recv_sem=recv_sems.at[k - 1],
                                             device_id=dev, device_id_type=MESH).start()
        token[...] = jnp.zeros_like(token)

    arrays = list(srcs) + [landing]
    return pl.pallas_call(
        body, name=name,
        out_shape=(pltpu.SemaphoreType.DMA((N_DEV - 1,)), pltpu.SemaphoreType.DMA((N_DEV - 1,)),
                   *[pltpu.HBM(a.shape, a.dtype) for a in arrays], jax.ShapeDtypeStruct((8, LANES), F32)),
        in_specs=[HBM] * (n + 1) + [pl.BlockSpec(memory_space=pl.ANY)],
        out_specs=(SEM, SEM, *[HBM] * (n + 1), pl.BlockSpec(memory_space=pltpu.VMEM)),
        input_output_aliases={q: 2 + q for q in range(n + 1)},
        compiler_params=pltpu.CompilerParams(has_side_effects=EFFECT),
    )(*[pltpu.with_memory_space_constraint(a, pltpu.HBM) for a in arrays], after)


def _exchange_wait(started, after, name):
    send_sems, recv_sems = started[0], started[1]
    arrays = list(started[2:-1])
    n1 = len(arrays)

    def body(*refs):
        land_ref = refs[n1 - 1]
        sends, recvs = refs[n1], refs[n1 + 1]
        for k in range(1, N_DEV):
            dev, _ = _peer(k)
            w = pltpu.make_async_remote_copy(src_ref=land_ref.at[0], dst_ref=land_ref.at[0], send_sem=sends.at[k - 1],
                                             recv_sem=recvs.at[k - 1], device_id=dev, device_id_type=MESH)
            w.wait_send()
            w.wait_recv()

    return pl.pallas_call(
        body, name=name, out_shape=tuple(pltpu.HBM(a.shape, a.dtype) for a in arrays),
        in_specs=[HBM] * n1 + [SEM, SEM, pl.BlockSpec(memory_space=pl.ANY)], out_specs=tuple([HBM] * n1),
        input_output_aliases={q: q for q in range(n1)},
        compiler_params=pltpu.CompilerParams(has_side_effects=EFFECT),
    )(*arrays, send_sems, recv_sems, after)[n1 - 1]


def _norm_mod(x, nw, sc, sh):
    ms = jnp.mean(x * x, axis=-1, keepdims=True)
    xh = x * lax.rsqrt(ms + RMS_EPS)
    return xh, (xh * nw) * (1.0 + sc) + sh


def _proj(x, nw, sc, sh, wt, *, ffn, name):
    S, N = x.shape[0], wt.shape[0]
    tm = 256 if ffn else 512

    def body(x_ref, nw_ref, sc_ref, sh_ref, w_ref, h_ref, out_ref, *act_ref):
        split = 1 if ffn else SPLIT
        for half in range(split):
            rows = pl.ds(half * (tm // split), tm // split)
            _, h = _norm_mod(x_ref[rows, :], nw_ref[...], sc_ref[...], sh_ref[...])
            hb = h.astype(BF16)
            h_ref[rows, :] = hb
            if ffn:
                gate = _nt(hb, w_ref[pl.ds(0, D_FF), :])
                up = _nt(hb, w_ref[pl.ds(D_FF, D_FF), :])
                sig = jax.nn.sigmoid(gate)
                silu = gate * sig
                out_ref[rows, pl.ds(0, D_FF)] = (up * (sig * (1.0 + gate * (1.0 - sig)))).astype(BF16)
                out_ref[rows, pl.ds(D_FF, D_FF)] = silu.astype(BF16)
                act_ref[0][rows, :] = (silu * up).astype(BF16)
            else:
                out_ref[rows, :] = _nt(hb, w_ref[...]).astype(BF16)

    row = lambda w: pl.BlockSpec((tm, w), lambda i: (i, 0))
    out_shape = [jax.ShapeDtypeStruct((S, D), BF16), jax.ShapeDtypeStruct((S, N), BF16)]
    out_specs = [row(D), row(N)]
    if ffn:
        out_shape.append(jax.ShapeDtypeStruct((S, D_FF), BF16))
        out_specs.append(row(D_FF))
    vec = _const_spec((1, D))
    return pl.pallas_call(
        body, name=name, grid=(S // tm,), out_shape=tuple(out_shape),
        in_specs=[row(D), vec, vec, vec, _const_spec((N, D))], out_specs=tuple(out_specs),
        compiler_params=_params(("parallel",), VMEM_LIMIT),
    )(x, nw, sc, sh, wt)


def _gated_residual(a, w, x, g, *, w_is_transposed, name):
    S, K = a.shape
    tm = 512

    def body(a_ref, w_ref, x_ref, g_ref, xo_ref, y_ref):
        y = _nt(a_ref[...], w_ref[...]) if w_is_transposed else _nn(a_ref[...], w_ref[...])
        y_ref[...] = y.astype(BF16)
        xo_ref[...] = x_ref[...] + g_ref[...] * y

    row = lambda w_: pl.BlockSpec((tm, w_), lambda i: (i, 0))
    return pl.pallas_call(
        body, name=name, grid=(S // tm,),
        out_shape=(jax.ShapeDtypeStruct((S, D), F32), jax.ShapeDtypeStruct((S, D), BF16)),
        in_specs=[row(K), _const_spec(w.shape), row(D), _const_spec((1, D))], out_specs=(row(D), row(D)),
        compiler_params=_params(("parallel",), VMEM_LIMIT),
    )(a, w, x, g)


CHUNK = 1024


def _tile_rows(r, chunk=CHUNK):
    return min(TQ, chunk // r)


def _out_ffn_in(a, w_mix, x, g, nw, sc, sh, wt, *, w_is_transposed, name):
    S, K = a.shape
    tm = 256

    def body(a_ref, wm_ref, x_ref, g_ref, nw_ref, sc_ref, sh_ref, w_ref, x1_ref, y_ref, h_ref, gu_ref, act_ref):
        y = _nt(a_ref[...], wm_ref[...]) if w_is_transposed else _nn(a_ref[...], wm_ref[...])
        y_ref[...] = y.astype(BF16)
        x1 = x_ref[...] + g_ref[...] * y
        x1_ref[...] = x1
        _, h = _norm_mod(x1, nw_ref[...], sc_ref[...], sh_ref[...])
        hb = h.astype(BF16)
        h_ref[...] = hb
        gate = _nt(hb, w_ref[pl.ds(0, D_FF), :])
        up = _nt(hb, w_ref[pl.ds(D_FF, D_FF), :])
        sig = jax.nn.sigmoid(gate)
        silu = gate * sig
        gu_ref[:, pl.ds(0, D_FF)] = (up * (sig * (1.0 + gate * (1.0 - sig)))).astype(BF16)
        gu_ref[:, pl.ds(D_FF, D_FF)] = silu.astype(BF16)
        act_ref[...] = (silu * up).astype(BF16)

    row = lambda w_: pl.BlockSpec((tm, w_), lambda i: (i, 0))
    vec = _const_spec((1, D))
    return pl.pallas_call(
        body, name=name, grid=(S // tm,),
        out_shape=(jax.ShapeDtypeStruct((S, D), F32), jax.ShapeDtypeStruct((S, D), BF16), jax.ShapeDtypeStruct((S, D), BF16),
                   jax.ShapeDtypeStruct((S, 2 * D_FF), BF16), jax.ShapeDtypeStruct((S, D_FF), BF16)),
        in_specs=[row(K), _const_spec(w_mix.shape), row(D), vec, vec, vec, vec, _const_spec(wt.shape)],
        out_specs=(row(D), row(D), row(D), row(2 * D_FF), row(D_FF)),
        compiler_params=_params(("parallel",), VMEM_LIMIT),
    )(a, w_mix, x, g, nw, sc, sh, wt)


def _alibi_bias(slopes, half, dil, chunk=CHUNK, both=False):
    tq = _tile_rows(dil, chunk)
    tk = tq + 2 * half
    rel = np.arange(tk)[:, None] - half - np.arange(tq)[None, :]
    band = np.abs(rel) <= half
    dist = (dil * np.abs(rel)).astype(np.float32)
    tabs = [np.where(band, -np.float32(s) * dist, np.float32(NEG)).astype(np.float32) for s in slopes]
    out = []
    for u in range(0, len(tabs), 8):
        tab = np.concatenate(tabs[u:u + 8], axis=1)
        first, last = tab.copy(), tab.copy()
        first[:half] = NEG
        last[tk - half:] = NEG
        out += [tab, first, last]
        if both:
            last = last.copy()
            last[:half] = NEG
            out.append(last)
    return jnp.asarray(np.concatenate(out, axis=0))


def _slopes(n):
    return (2.0 ** (-8.0 * np.arange(1, n + 1) / n)).astype(np.float32)


def _head_masks(tq):
    lane = lax.broadcasted_iota(jnp.int32, (tq, LANES), 1)
    lo = lane < HEAD_DIM
    return lo, jnp.logical_not(lo)


def _stack_heads(tiles, lo, hi, scale):
    blocks = []
    for t in range(4):
        xf = tiles[t] if scale == 1.0 else tiles[t] * scale
        for a in range(2):
            xm = jnp.where(lo if a == 0 else hi, xf, 0.0)
            if a != t // 2:
                xm = pltpu.roll(xm, HEAD_DIM, 1)
            blocks.append(xm.astype(BF16))
    return jnp.concatenate(blocks, axis=0)


def _tile_from_columns(x8t, t, tq):
    r0 = HEAD_DIM * (t // 2)
    top = x8t[r0:r0 + HEAD_DIM, 2 * t * tq:(2 * t + 1) * tq]
    bot = x8t[r0:r0 + HEAD_DIM, (2 * t + 1) * tq:(2 * t + 2) * tq]
    return jnp.concatenate([top, bot], axis=0).T


def _attn_layout(S, C, r, half, qoff, koff, voff, chunk):
    hb = half * r
    per = chunk // hb
    nhb = S // hb
    main = lambda off: pl.BlockSpec((chunk, LANES), lambda u, i: (i, off // LANES + u))
    prev = lambda off: pl.BlockSpec((hb, LANES), lambda u, i: (jnp.maximum(i * per - 1, 0), off // LANES + u))
    nxt = lambda off: pl.BlockSpec((hb, LANES), lambda u, i: (jnp.minimum((i + 1) * per, nhb - 1), off // LANES + u))
    specs = [pl.BlockSpec((chunk, 4 * LANES), lambda u, i: (i, qoff // (4 * LANES) + u))]
    specs += [prev(koff), main(koff), nxt(koff), prev(voff), main(voff), nxt(voff)]
    return specs, hb


def _stage(dst, srcs):
    row = 0
    for src in srcs:
        n = src.shape[0]
        dst[pl.ds(row, n), :] = src[...].astype(F32)
        row += n


def _rows(start, n, r):
    return pl.ds(start, n, stride=r) if r > 1 else pl.ds(start, n)


def _attn_fwd(qkv, bias, sink, *, C, r, half, qoff, koff, voff, n_units, out_dtype, name):
    S = qkv.shape[0]
    chunk = max(CHUNK, TQ * r)
    tq = _tile_rows(r, chunk)
    tk = tq + 2 * half
    tiles = chunk // (r * tq)
    nsteps = S // chunk
    specs, hb = _attn_layout(S, C, r, half, qoff, koff, voff, chunk)
    use_sink = sink is not None

    def body(*refs):
        q_ref, kp, km, kn, vp, vm, vn, bias_ref = refs[:8]
        rest = list(refs[8:])
        sink_ref = rest.pop(0) if use_sink else None
        o_ref, lse_ref, qs, ks, vs, os_, ls = rest
        i = pl.program_id(1)
        if r > 1:
            for t in range(4):
                qs[t] = q_ref[:, pl.ds(t * LANES, LANES)].astype(F32)
        _stage(ks, [kp, km, kn])
        _stage(vs, [vp, vm, vn])
        lo, hi = _head_masks(tq)

        def tile_in(staged, ref, t, start):
            if r > 1:
                return staged[t, _rows(start, tq, r), :]
            return ref[pl.ds(start, tq), pl.ds(t * LANES, LANES)].astype(F32)

        ones = jnp.ones((16, tk), BF16)
        if use_sink:
            sk = sink_ref[pl.ds(0, 1), :]

        def chain(n, carry):
            rho, c = n // tiles, n % tiles
            start = c * (tq * r) + rho
            if r == 1:
                start = pl.multiple_of(start, tq)
            variant = jnp.where(jnp.logical_and(i == 0, c == 0), 1, 0) + jnp.where(
                jnp.logical_and(i == nsteps - 1, c == tiles - 1), 2, 0)
            k2 = ks[_rows(start, tk, r), :].astype(BF16)
            v2t = jnp.concatenate([vs[_rows(start, tk, r), :].T.astype(BF16), ones], axis=0)
            q8 = _stack_heads([tile_in(qs, q_ref, t, start) for t in range(4)], lo, hi, HEAD_DIM ** -0.5)
            s = _nt(k2, q8) + bias_ref[pl.ds(pl.multiple_of(variant * tk, 8), tk), :]
            m = jnp.max(s, axis=0, keepdims=True)
            if use_sink:
                m = jnp.maximum(m, sk)
            pv = _nn(v2t, jnp.exp(s - m).astype(BF16))
            l = pv[LANES:LANES + 1]
            if use_sink:
                l = l + jnp.exp(sk - m)
            o8t = pv[:LANES] / l
            lse8 = jnp.broadcast_to(m + jnp.log(l), (LANES, 8 * tq))
            for t in range(4):
                if r > 1:
                    os_[t, _rows(start, tq, r), :] = _tile_from_columns(o8t, t, tq)
                    ls[t, _rows(start, tq, r), :] = _tile_from_columns(lse8, t, tq)
                else:
                    o_ref[pl.ds(start, tq), pl.ds(t * LANES, LANES)] = _tile_from_columns(o8t, t, tq).astype(out_dtype)
                    lse_ref[pl.ds(start, tq), pl.ds(t * LANES, LANES)] = _tile_from_columns(lse8, t, tq)
            return carry

        lax.fori_loop(0, r * tiles, chain, 0, unroll=4)
        if r > 1:
            for t in range(4):
                o_ref[:, pl.ds(t * LANES, LANES)] = os_[t].astype(out_dtype)
                lse_ref[:, pl.ds(t * LANES, LANES)] = ls[t]

    in_specs = specs + [pl.BlockSpec((bias.shape[0] // n_units, 8 * tq), lambda u, i: (u, 0))]
    args = [qkv] * 7 + [bias]
    if use_sink:
        in_specs.append(pl.BlockSpec((8, 8 * tq), lambda u, i: (u, 0)))
        args.append(sink)
    wide = pl.BlockSpec((chunk, 4 * LANES), lambda u, i: (i, u))
    win = hb + chunk + hb
    big = lambda: pltpu.VMEM((4, chunk if r > 1 else 8, LANES), F32)
    return pl.pallas_call(
        body, name=name, grid=(n_units, nsteps),
        out_shape=(jax.ShapeDtypeStruct((S, n_units * 512), out_dtype), jax.ShapeDtypeStruct((S, n_units * 512), F32)),
        in_specs=in_specs, out_specs=(wide, wide),
        scratch_shapes=[big(), pltpu.VMEM((win, LANES), F32), pltpu.VMEM((win, LANES), F32), big(), big()],
        compiler_params=_params(("parallel", "parallel"), VMEM_LIMIT),
    )(*args)


def _attn_bwd(qkv, bias, sink, o, do, lse, *, C, r, half, qoff, koff, voff, n_units, name):
    S = qkv.shape[0]
    tq = _tile_rows(r)
    tk = tq + 2 * half
    tiles = CHUNK // (r * tq)
    nsteps = S // CHUNK
    specs, hb = _attn_layout(S, C, r, half, qoff, koff, voff, CHUNK)
    use_sink = sink is not None

    def body(*refs):
        q_ref, kp, km, kn, vp, vm, vn, bias_ref = refs[:8]
        rest = list(refs[8:])
        sink_ref = rest.pop(0) if use_sink else None
        o_ref, do_ref, lse_ref, dq_ref, dk_hbm, dv_hbm = rest[:6]
        rest = rest[6:]
        dsink_ref = rest.pop(0) if use_sink else None
        qs, ks, vs, os_, dos, ls, dqs, acck, accv, sem = rest
        u, i = pl.program_id(0), pl.program_id(1)

        @pl.when(i == 0)
        def _():
            acck[...] = jnp.zeros_like(acck)
            accv[...] = jnp.zeros_like(accv)
            if use_sink:
                dsink_ref[...] = jnp.zeros_like(dsink_ref)

        if r > 1:
            for t in range(4):
                cols = pl.ds(t * LANES, LANES)
                qs[t] = q_ref[:, cols].astype(F32)
                os_[t] = o_ref[:, cols].astype(F32)
                dos[t] = do_ref[:, cols].astype(F32)
                ls[t] = lse_ref[:, cols]
        _stage(ks, [kp, km, kn])
        _stage(vs, [vp, vm, vn])
        lo, hi = _head_masks(tq)

        def tile_in(staged, ref, t, start):
            if r > 1:
                return staged[t, _rows(start, tq, r), :]
            return ref[pl.ds(start, tq), pl.ds(t * LANES, LANES)].astype(F32)

        base = pl.multiple_of(i * CHUNK, CHUNK)
        if use_sink:
            sk = sink_ref[pl.ds(0, 1), :]

        def chain(n, carry):
            rho, c = n // tiles, n % tiles
            start = c * (tq * r) + rho
            if r == 1:
                start = pl.multiple_of(start, tq)
            variant = jnp.where(jnp.logical_and(i == 0, c == 0), 1, 0) + jnp.where(
                jnp.logical_and(i == nsteps - 1, c == tiles - 1), 2, 0)
            k2 = ks[_rows(start, tk, r), :].astype(BF16)
            v2 = vs[_rows(start, tk, r), :].astype(BF16)
            k2t = ks[_rows(start, tk, r), :].T.astype(BF16)
            q8 = _stack_heads([tile_in(qs, q_ref, t, start) for t in range(4)], lo, hi, HEAD_DIM ** -0.5)
            do_tiles = [tile_in(dos, do_ref, t, start) for t in range(4)]
            do8 = _stack_heads(do_tiles, lo, hi, 1.0)
            deltas, lses = [], []
            for t in range(4):
                prod_t = (do_tiles[t] * tile_in(os_, o_ref, t, start)).T
                lse_t = tile_in(ls, lse_ref, t, start).T
                for a in range(2):
                    deltas.append(jnp.sum(prod_t[a * HEAD_DIM:(a + 1) * HEAD_DIM], axis=0, keepdims=True))
                    lses.append(lse_t[a * HEAD_DIM:a * HEAD_DIM + 1])
            delta8 = jnp.concatenate(deltas, axis=1)
            lse8 = jnp.concatenate(lses, axis=1)
            s = _nt(k2, q8) + bias_ref[pl.ds(pl.multiple_of(variant * tk, 8), tk), :]
            p = jnp.exp(s - lse8)
            dp = _nt(v2, do8)
            dsb = (p * (dp - delta8)).astype(BF16)
            dq8t = _nn(k2t, dsb)
            for t in range(4):
                dq_t = _tile_from_columns(dq8t, t, tq) * (HEAD_DIM ** -0.5)
                if r > 1:
                    dqs[t, _rows(start, tq, r), :] = dq_t
                else:
                    dq_ref[pl.ds(start, tq), pl.ds(t * LANES, LANES)] = dq_t.astype(BF16)
            arow = base + start
            if r == 1:
                arow = pl.multiple_of(arow, tq)
            acck[_rows(arow, tk, r), :] = acck[_rows(arow, tk, r), :] + _nn(dsb, q8)
            accv[_rows(arow, tk, r), :] = accv[_rows(arow, tk, r), :] + _nn(p.astype(BF16), do8)
            if use_sink:
                e = jnp.exp(sk - lse8) * delta8
                for h in range(8):
                    part = -jnp.sum(e[:, h * tq:(h + 1) * tq], axis=1, keepdims=True)
                    dsink_ref[pl.ds(h, 1), :] = dsink_ref[pl.ds(h, 1), :] + part
            return carry

        lax.fori_loop(0, r * tiles, chain, 0, unroll=2)
        if r > 1:
            for t in range(4):
                dq_ref[:, pl.ds(t * LANES, LANES)] = dqs[t].astype(BF16)

        @pl.when(i == nsteps - 1)
        def _():
            ck = pltpu.make_async_copy(acck.at[pl.ds(hb, S)], dk_hbm.at[u], sem.at[0])
            cv = pltpu.make_async_copy(accv.at[pl.ds(hb, S)], dv_hbm.at[u], sem.at[1])
            ck.start()
            cv.start()
            ck.wait()
            cv.wait()

    wide = pl.BlockSpec((CHUNK, 4 * LANES), lambda u, i: (i, u))
    hbm = pl.BlockSpec(memory_space=pl.ANY)
    in_specs = specs + [pl.BlockSpec((3 * tk, 8 * tq), lambda u, i: (u, 0))]
    args = [qkv] * 7 + [bias]
    if use_sink:
        in_specs.append(pl.BlockSpec((8, 8 * tq), lambda u, i: (u, 0)))
        args.append(sink)
    in_specs += [wide, wide, wide]
    args += [o, do, lse]
    out_shape = [jax.ShapeDtypeStruct((S, n_units * 512), BF16), jax.ShapeDtypeStruct((n_units, S, LANES), F32),
                 jax.ShapeDtypeStruct((n_units, S, LANES), F32)]
    out_specs = [wide, hbm, hbm]
    if use_sink:
        out_shape.append(jax.ShapeDtypeStruct((n_units * 8, LANES), F32))
        out_specs.append(pl.BlockSpec((8, LANES), lambda u, i: (u, 0)))
    win = hb + CHUNK + hb
    big = lambda: pltpu.VMEM((4, CHUNK if r > 1 else 8, LANES), F32)
    res = pl.pallas_call(
        body, name=name, grid=(n_units, nsteps), out_shape=tuple(out_shape), in_specs=in_specs, out_specs=tuple(out_specs),
        scratch_shapes=[big(), pltpu.VMEM((win, LANES), F32), pltpu.VMEM((win, LANES), F32), big(), big(), big(), big(),
                        pltpu.VMEM((S + 2 * hb, LANES), F32), pltpu.VMEM((S + 2 * hb, LANES), F32), pltpu.SemaphoreType.DMA((2,))],
        compiler_params=_params(("arbitrary", "arbitrary"), VMEM_LIMIT),
    )(*args)
    return res[0], res[1], res[2], (res[3] if use_sink else None)


def _merge_groups(os_, lses):
    S, W = os_[0].shape
    tm = 512

    def body(o0, o1, o2, l0, l1, l2, o_ref, lse_ref):
        ls = [l0[...], l1[...], l2[...]]
        mx = jnp.maximum(jnp.maximum(ls[0], ls[1]), ls[2])
        es = [jnp.exp(l - mx) for l in ls]
        den = es[0] + es[1] + es[2]
        o = (es[0] / den) * o0[...] + (es[1] / den) * o1[...] + (es[2] / den) * o2[...]
        o_ref[...] = o.astype(BF16)
        lse_ref[...] = mx + jnp.log(den)

    row = pl.BlockSpec((tm, W), lambda i: (i, 0))
    return pl.pallas_call(
        body, name="merge_groups", grid=(S // tm,),
        out_shape=(jax.ShapeDtypeStruct((S, W), BF16), jax.ShapeDtypeStruct((S, W), F32)),
        in_specs=[row] * 6, out_specs=(row, row), compiler_params=_params(("parallel",), VMEM_LIMIT),
    )(*os_, *lses)


def _loss_head(x, target, fnw):
    S = x.shape[0]
    tm = 512

    def body(x_ref, t_ref, w_ref, dx_ref, st_ref):
        @pl.when(pl.program_id(0) == 0)
        def _():
            st_ref[...] = jnp.zeros_like(st_ref)

        xv = x_ref[...]
        rstd = lax.rsqrt(jnp.mean(xv * xv, axis=-1, keepdims=True) + RMS_EPS)
        xh = xv * rstd
        err = xh * w_ref[...] - t_ref[...]
        dy = err * (1.0 / D)
        dxh = dy * w_ref[...]
        dx_ref[...] = rstd * (dxh - xh * jnp.mean(dxh * xh, axis=-1, keepdims=True))
        st_ref[pl.ds(0, 1), :] = st_ref[pl.ds(0, 1), :] + jnp.sum(dy * xh, axis=0, keepdims=True)
        st_ref[pl.ds(1, 1), :] = st_ref[pl.ds(1, 1), :] + jnp.sum(err * err, axis=0, keepdims=True)

    row = pl.BlockSpec((tm, D), lambda i: (i, 0))
    return pl.pallas_call(
        body, name="loss_head", grid=(S // tm,),
        out_shape=(jax.ShapeDtypeStruct((S, D), F32), jax.ShapeDtypeStruct((8, D), F32)),
        in_specs=[row, row, _const_spec((1, D))], out_specs=(row, _const_spec((8, D))),
        compiler_params=_params(("arbitrary",), VMEM_LIMIT),
    )(x, target, fnw)


def _gate_bwd(dx, y, g, w, gu, *, w_is_transposed, name):
    S = dx.shape[0]
    K = w.shape[1] if w_is_transposed else w.shape[0]
    ffn = gu is not None
    tm = 512
    wout = 2 * K if ffn else K

    def body(dx_ref, y_ref, g_ref, w_ref, *rest):
        if ffn:
            gu_ref, da_ref, dyb_ref, st_ref = rest
        else:
            da_ref, dyb_ref, st_ref = rest

        @pl.when(pl.program_id(0) == 0)
        def _():
            st_ref[...] = jnp.zeros_like(st_ref)

        total = jnp.zeros((1, D), F32)
        split = 1 if ffn else SPLIT
        for half in range(split):
            rows = pl.ds(half * (tm // split), tm // split)
            dxv = dx_ref[rows, :]
            total = total + jnp.sum(dxv * y_ref[rows, :].astype(F32), axis=0, keepdims=True)
            dyb = (dxv * g_ref[...]).astype(BF16)
            dyb_ref[rows, :] = dyb
            da = _nn(dyb, w_ref[...]) if w_is_transposed else _nt(dyb, w_ref[...])
            if ffn:
                da_ref[rows, pl.ds(0, K)] = (da * gu_ref[rows, pl.ds(0, K)].astype(F32)).astype(BF16)
                da_ref[rows, pl.ds(K, K)] = (da * gu_ref[rows, pl.ds(K, K)].astype(F32)).astype(BF16)
            else:
                da_ref[rows, :] = da.astype(BF16)
        st_ref[pl.ds(0, 1), :] = st_ref[pl.ds(0, 1), :] + total

    row = lambda w_: pl.BlockSpec((tm, w_), lambda i: (i, 0))
    in_specs = [row(D), row(D), _const_spec((1, D)), _const_spec(w.shape)]
    args = [dx, y, g, w]
    if ffn:
        in_specs.append(row(wout))
        args.append(gu)
    return pl.pallas_call(
        body, name=name, grid=(S // tm,),
        out_shape=(jax.ShapeDtypeStruct((S, wout), BF16), jax.ShapeDtypeStruct((S, D), BF16), jax.ShapeDtypeStruct((8, D), F32)),
        in_specs=in_specs, out_specs=(row(wout), row(D), _const_spec((8, D))),
        compiler_params=_params(("arbitrary",), VMEM_LIMIT),
    )(*args)


def _norm_bwd(dy, wt, x, dres, nw, sc, *, name):
    S, N = dy.shape
    tm = 512

    def body(dy_ref, w_ref, x_ref, dres_ref, nw_ref, sc_ref, dx_ref, st_ref):
        @pl.when(pl.program_id(0) == 0)
        def _():
            st_ref[...] = jnp.zeros_like(st_ref)

        nwv, scale = nw_ref[...], 1.0 + sc_ref[...]
        sums = [jnp.zeros((1, D), F32)] * 3
        for half in range(SPLIT):
            rows = pl.ds(half * (tm // SPLIT), tm // SPLIT)
            dh = _nn(dy_ref[rows, :], w_ref[...])
            xv = x_ref[rows, :]
            rstd = lax.rsqrt(jnp.mean(xv * xv, axis=-1, keepdims=True) + RMS_EPS)
            xh = xv * rstd
            dxh = dh * (nwv * scale)
            dx_ref[rows, :] = dres_ref[rows, :] + rstd * (dxh - xh * jnp.mean(dxh * xh, axis=-1, keepdims=True))
            dhx = dh * xh
            sums = [sums[0] + jnp.sum(dh, axis=0, keepdims=True), sums[1] + jnp.sum(dhx * nwv, axis=0, keepdims=True),
                    sums[2] + jnp.sum(dhx * scale, axis=0, keepdims=True)]
        for q in range(3):
            st_ref[pl.ds(q, 1), :] = st_ref[pl.ds(q, 1), :] + sums[q]

    row = lambda w_: pl.BlockSpec((tm, w_), lambda i: (i, 0))
    vec = _const_spec((1, D))
    return pl.pallas_call(
        body, name=name, grid=(S // tm,),
        out_shape=(jax.ShapeDtypeStruct((S, D), F32), jax.ShapeDtypeStruct((8, D), F32)),
        in_specs=[row(N), _const_spec((N, D)), row(D), row(D), vec, vec], out_specs=(row(D), _const_spec((8, D))),
        compiler_params=_params(("arbitrary",), VMEM_LIMIT),
    )(dy, wt, x, dres, nw, sc)


def _ffn_bwd(dx, y, g, w_out, gu, wt_in, x, nw, sc, y1, g1, w_mix, *, mix_is_transposed, name):
    S = dx.shape[0]
    K = w_out.shape[0]
    Km = w_mix.shape[1] if mix_is_transposed else w_mix.shape[0]
    tm = 256

    def body(dx_ref, y_ref, g_ref, wo_ref, gu_ref, wi_ref, x_ref, nw_ref, sc_ref, y1_ref, g1_ref, wm_ref,
             dgu_ref, dyb_ref, dxo_ref, da_ref, dy1_ref, stg_ref, stf_ref, stm_ref):
        @pl.when(pl.program_id(0) == 0)
        def _():
            stg_ref[...] = jnp.zeros_like(stg_ref)
            stf_ref[...] = jnp.zeros_like(stf_ref)
            stm_ref[...] = jnp.zeros_like(stm_ref)

        dxv = dx_ref[...]
        stg_ref[pl.ds(0, 1), :] = stg_ref[pl.ds(0, 1), :] + jnp.sum(dxv * y_ref[...].astype(F32), axis=0, keepdims=True)
        dyb = (dxv * g_ref[...]).astype(BF16)
        dyb_ref[...] = dyb
        da = _nt(dyb, wo_ref[...])
        dgate = (da * gu_ref[:, pl.ds(0, K)].astype(F32)).astype(BF16)
        dup = (da * gu_ref[:, pl.ds(K, K)].astype(F32)).astype(BF16)
        dgu_ref[:, pl.ds(0, K)] = dgate
        dgu_ref[:, pl.ds(K, K)] = dup
        dh = _nn(dgate, wi_ref[pl.ds(0, K), :]) + _nn(dup, wi_ref[pl.ds(K, K), :])
        xv = x_ref[...]
        rstd = lax.rsqrt(jnp.mean(xv * xv, axis=-1, keepdims=True) + RMS_EPS)
        xh = xv * rstd
        nwv, scale = nw_ref[...], 1.0 + sc_ref[...]
        dxh = dh * (nwv * scale)
        dx1 = dxv + rstd * (dxh - xh * jnp.mean(dxh * xh, axis=-1, keepdims=True))
        dxo_ref[...] = dx1
        dhx = dh * xh
        stf_ref[pl.ds(0, 1), :] = stf_ref[pl.ds(0, 1), :] + jnp.sum(dh, axis=0, keepdims=True)
        stf_ref[pl.ds(1, 1), :] = stf_ref[pl.ds(1, 1), :] + jnp.sum(dhx * nwv, axis=0, keepdims=True)
        stf_ref[pl.ds(2, 1), :] = stf_ref[pl.ds(2, 1), :] + jnp.sum(dhx * scale, axis=0, keepdims=True)
        stm_ref[pl.ds(0, 1), :] = stm_ref[pl.ds(0, 1), :] + jnp.sum(dx1 * y1_ref[...].astype(F32), axis=0, keepdims=True)
        dy1 = (dx1 * g1_ref[...]).astype(BF16)
        dy1_ref[...] = dy1
        da_ref[...] = (_nn(dy1, wm_ref[...]) if mix_is_transposed else _nt(dy1, wm_ref[...])).astype(BF16)

    row = lambda w_: pl.BlockSpec((tm, w_), lambda i: (i, 0))
    vec = _const_spec((1, D))
    st = jax.ShapeDtypeStruct((8, D), F32)
    act = lambda w_: jax.ShapeDtypeStruct((S, w_), BF16)
    return pl.pallas_call(
        body, name=name, grid=(S // tm,),
        out_shape=(act(2 * K), act(D), jax.ShapeDtypeStruct((S, D), F32), act(Km), act(D), st, st, st),
        in_specs=[row(D), row(D), vec, _const_spec(w_out.shape), row(2 * K), _const_spec(wt_in.shape), row(D), vec, vec,
                  row(D), vec, _const_spec(w_mix.shape)],
        out_specs=(row(2 * K), row(D), row(D), row(Km), row(D), _const_spec((8, D)), _const_spec((8, D)), _const_spec((8, D))),
        compiler_params=_params(("arbitrary",), VMEM_LIMIT),
    )(dx, y, g, w_out, gu, wt_in, x, nw, sc, y1, g1, w_mix)


def _weight_grad(a, b, *, transpose_out, name):
    S, N = b.shape
    nb = N // 2 if N > 4096 else N
    tk = 512

    def body(a_ref, b_ref, out_ref, acc):
        k = pl.program_id(1)

        @pl.when(k == 0)
        def _():
            acc[...] = jnp.zeros_like(acc)

        acc[...] += _tn(a_ref[...], b_ref[...])

        @pl.when(k == pl.num_programs(1) - 1)
        def _():
            out_ref[...] = (acc[...].T if transpose_out else acc[...]).astype(BF16)

    out_block = pl.BlockSpec((nb, D), lambda n, k: (n, 0)) if transpose_out else pl.BlockSpec((D, nb), lambda n, k: (0, n))
    return pl.pallas_call(
        body, name=name, grid=(N // nb, S // tk),
        out_shape=jax.ShapeDtypeStruct((N, D) if transpose_out else (D, N), BF16),
        in_specs=[pl.BlockSpec((tk, D), lambda n, k: (k, 0)), pl.BlockSpec((tk, nb), lambda n, k: (k, n))],
        out_specs=out_block, scratch_shapes=[pltpu.VMEM((D, nb), F32)],
        compiler_params=_params(("parallel", "arbitrary"), VMEM_LIMIT),
    )(a, b)


def _adamw(w, g, m, v):
    m = ADAM_B1 * m + (1.0 - ADAM_B1) * g
    v = ADAM_B2 * v + (1.0 - ADAM_B2) * (g * g)
    m_hat = m / (1.0 - ADAM_B1 ** ADAM_STEP)
    v_hat = v / (1.0 - ADAM_B2 ** ADAM_STEP)
    delta = -ADAM_LR * (m_hat / (jnp.sqrt(v_hat) + ADAM_EPS) + ADAM_WD * w)
    return delta, m, v


def _adam_shard(parts, own, w, m, v, name):
    R = w.shape[0]
    tr = max(t for t in (16, 32, 64, 128, 192, 256) if R % t == 0)

    def body(p_ref, o_ref, w_ref, m_ref, v_ref, g_out, d_out, m_out, v_out):
        me = _my_index()
        g = jnp.zeros((tr, D), F32)
        for j in range(N_DEV):
            g = g + jnp.where(me == j, o_ref[...], p_ref[j]).astype(F32)
        delta, mn, vn = _adamw(w_ref[...], g, m_ref[...], v_ref[...])
        g_out[...] = g
        d_out[...] = delta
        m_out[...] = mn
        v_out[...] = vn

    row = pl.BlockSpec((tr, D), lambda i: (i, 0))
    shp = jax.ShapeDtypeStruct((R, D), F32)
    return pl.pallas_call(
        body, name=name, grid=(R // tr,), out_shape=(shp,) * 4,
        in_specs=[pl.BlockSpec((N_DEV, tr, D), lambda i: (0, i, 0)), row, row, row, row], out_specs=(row,) * 4,
        compiler_params=_params(("parallel",), VMEM_LIMIT),
    )(parts, own, w, m, v)


def _adam_ada_w(cond_t, dmod, w, m, v):
    ncol = w.shape[-1]
    tr = 512

    def body(c_ref, d_ref, w_ref, m_ref, v_ref, g_out, d_out, m_out, v_out):
        g = _nn(c_ref[...], d_ref[0])
        delta, mn, vn = _adamw(w_ref[0], g, m_ref[0], v_ref[0])
        g_out[0] = g
        d_out[0] = delta
        m_out[0] = mn
        v_out[0] = vn

    blk = pl.BlockSpec((1, tr, ncol), lambda l, i: (l, i, 0))
    shp = jax.ShapeDtypeStruct(w.shape, F32)
    return pl.pallas_call(
        body, name="adam_ada_w", grid=(DEPTH, D // tr), out_shape=(shp,) * 4,
        in_specs=[pl.BlockSpec((tr, LANES), lambda l, i: (i, 0)), pl.BlockSpec((1, LANES, ncol), lambda l, i: (l, 0, 0)), blk, blk, blk],
        out_specs=(blk,) * 4, compiler_params=_params(("parallel", "parallel"), VMEM_LIMIT),
    )(cond_t, dmod, w, m, v)


TILE_ROWS = 168


def _stat_sources():
    pairs = []
    for i in range(DEPTH):
        b = 32 * i
        for q, src in enumerate((b, b + 1, b + 8, b + 16, b + 17, b + 24)):
            pairs.append((6 * i + q, src))
        pairs.append((24 + i, b + 2))
        pairs.append((32 + i, b + 18))
    pairs += [(40, 128), (41, 129)]
    return pairs


def _small_exchange(tiles, w, m, v):
    loss_row, sink_row, sink_src = 41, 48, 136

    def body(s_ref, w_ref, m_ref, v_ref, dmod_out, g_out, d_out, m_out, v_out, loss_out, all_ref, tot_ref, send_sems, recv_sems):
        me = _my_index()
        all_ref[me] = s_ref[...]
        copies = []
        for k in range(1, N_DEV):
            dev, _ = _peer(k)
            cp = pltpu.make_async_remote_copy(src_ref=s_ref, dst_ref=all_ref.at[me], send_sem=send_sems.at[k - 1],
                                              recv_sem=recv_sems.at[k - 1], device_id=dev, device_id_type=MESH)
            cp.start()
            copies.append(cp)
        for k in range(1, N_DEV):
            dev, pidx = _peer(k)
            pltpu.make_async_remote_copy(src_ref=s_ref, dst_ref=all_ref.at[pidx], send_sem=send_sems.at[k - 1],
                                         recv_sem=recv_sems.at[k - 1], device_id=dev, device_id_type=MESH).wait_recv()
        for cp in copies:
            cp.wait_send()
        tot = all_ref[0]
        for j in range(1, N_DEV):
            tot = tot + all_ref[j]
        tot_ref[...] = tot
        g_out[...] = jnp.zeros_like(g_out)
        for dst, src in _stat_sources():
            g_out[pl.ds(dst, 1), :] = tot_ref[pl.ds(src, 1), :]
            if dst < 24:
                for j in range(N_DEV):
                    dmod_out[j, pl.ds(dst, 1), :] = all_ref[j, pl.ds(src, 1), :]
        lane = lax.broadcasted_iota(jnp.int32, (1, D), 1)
        sink = jnp.zeros((1, D), F32)
        for h in range(32):
            sink = jnp.where(lane == h, tot_ref[pl.ds(sink_src + h, 1), :], sink)
        g_out[pl.ds(sink_row, 1), :] = sink
        g = g_out[...]
        delta, mn, vn = _adamw(w_ref[...], g, m_ref[...], v_ref[...])
        d_out[...] = delta
        m_out[...] = mn
        v_out[...] = vn
        loss = jnp.sum(g[loss_row:loss_row + 1, :], axis=-1, keepdims=True) * (0.5 / D)
        loss_out[...] = jnp.broadcast_to(loss, loss_out.shape)

    vm = pl.BlockSpec(memory_space=pltpu.VMEM)
    shp = jax.ShapeDtypeStruct((STAT_ROWS, D), F32)
    return pl.pallas_call(
        body, name="small_exchange",
        out_shape=(jax.ShapeDtypeStruct((N_DEV, 24, D), F32), shp, shp, shp, shp, jax.ShapeDtypeStruct((8, LANES), F32)),
        in_specs=[vm] * 4, out_specs=(vm,) * 6,
        scratch_shapes=[pltpu.VMEM((N_DEV, TILE_ROWS, D), F32), pltpu.VMEM((TILE_ROWS, D), F32),
                        pltpu.SemaphoreType.DMA((N_DEV - 1,)), pltpu.SemaphoreType.DMA((N_DEV - 1,))],
        compiler_params=_params(vmem=VMEM_LIMIT),
    )(tiles, w, m, v)


def _to_rows(name, a):
    if name in ("ffn_in", "a_in", "b_in"):
        return a.T
    if name == "b_out":
        return a.T.reshape(-1, D)
    return a


def _from_rows(name, a):
    if name in ("ffn_in", "a_in", "b_in"):
        return a.T
    if name == "b_out":
        return a.reshape(-1, 512).T
    return a


def _rows8(a):
    return jnp.pad(a, ((0, 8 - a.shape[0]), (0, 0)))


def _pack_small(ada_b, norm_mix, norm_ffn, final_norm, sink):
    sink_row = jnp.pad(sink.reshape(1, -1), ((0, 0), (0, D - sink.size)))
    return jnp.concatenate([ada_b.reshape(24, D), _rows8(norm_mix), _rows8(norm_ffn), _rows8(final_norm.reshape(1, D)),
                            _rows8(sink_row)], axis=0)


def _unpack_small(a):
    return a[0:24].reshape(4, 6 * D), a[24:28], a[32:36], a[40], a[48, :32].reshape(2, 16)


def kernel(x, c, ada_w, ada_b, norm_mix, norm_ffn, ffn_w_in, ffn_w_out, a_w_in, a_w_out, a_sink, b_w_in, b_w_out, final_norm, loss_target, m_ada_w, m_ada_b, m_norm_mix, m_norm_ffn, m_ffn_w_in, m_ffn_w_out, m_a_w_in, m_a_w_out, m_a_sink, m_b_w_in, m_b_w_out, m_final_norm, v_ada_w, v_ada_b, v_norm_mix, v_norm_ffn, v_ffn_w_in, v_ffn_w_out, v_a_w_in, v_a_w_out, v_a_sink, v_b_w_in, v_b_w_out, v_final_norm):
    S = x.shape[1]
    x0 = x.reshape(S, D)
    target = loss_target.reshape(S, D)
    me = _my_index()
    ncol = ada_w.shape[-1]

    ada_b_mine = lax.dynamic_slice_in_dim(ada_b, me * ncol, ncol, axis=1)
    cond_all, parts = _cond_exchange(jnp.broadcast_to(c.reshape(1, D), (8, D)), ada_w, ada_b_mine)
    mod = lax.dynamic_index_in_dim(parts, me, axis=2, keepdims=False)
    mod = jnp.transpose(mod, (1, 0, 2)).reshape(DEPTH, 6, 1, D)

    weights = {"ffn_in": ffn_w_in, "ffn_out": ffn_w_out, "a_in": a_w_in, "a_out": a_w_out, "b_in": b_w_in, "b_out": b_w_out}
    shard = {(n, l): _to_rows(n, weights[n][l]).astype(BF16) for n, l, _ in SEGMENTS}
    first = [sg for sg in _layer_segments(0) if not sg[0].startswith("ffn")]
    gathered0 = _all_gather_weights([shard[(n, l)] for n, l, _ in first])
    W = {(n, l): g for (n, l, _), g in zip(first, gathered0)}
    groups = [[sg for sg in _layer_segments(0) if sg[0].startswith("ffn")], [sg for i in range(1, DEPTH) for sg in _layer_segments(i)]]
    gathers, order = [], gathered0[0]
    for q, segs in enumerate(groups):
        mine = jnp.concatenate([shard[(n, l)] for n, l, _ in segs], axis=0)
        zone = lax.empty((N_DEV, mine.shape[0], D), BF16)
        gathers.append(_exchange_start([mine], zone, [mine.shape[0]], [0], False, order, "weight_gather_start_%d" % q))
        order = gathers[-1][-1]
    gather_token = order[0:1, 0:1]

    def finish_gather(q, after):
        zone = _exchange_wait(gathers[q], after, "weight_gather_wait_%d" % q)
        offs, _ = _offsets(groups[q])
        for (n, l, rows), off in zip(groups[q], offs):
            full = lax.dynamic_update_slice(zone[:, off:off + rows], shard[(n, l)][None], (me, 0, 0))
            W[(n, l)] = full.reshape(D, 512) if n == "b_out" else full.reshape(N_DEV * rows, D)

    a_slopes, b_slopes = _slopes(16), _slopes(24)
    bias_a = _alibi_bias(a_slopes, A_HALF, 1)
    bias_b = [_alibi_bias(b_slopes[8 * g:8 * g + 8], B_HALF, dil) for g, dil in enumerate(B_DILS)]
    bias_b_fwd = [_alibi_bias(b_slopes[8 * g:8 * g + 8], B_HALF, dil, max(CHUNK, TQ * dil), both=True) for g, dil in enumerate(B_DILS)]
    a_geom = dict(C=A_QKV, r=1, half=A_HALF, qoff=0, koff=1024, voff=1280, n_units=2)
    b_geom = [dict(C=B_QKV, r=dil, half=B_HALF, qoff=512 * g, koff=1536 + 128 * g, voff=1920 + 128 * g, n_units=1)
              for g, dil in enumerate(B_DILS)]

    saved = []
    xcur = x0
    for i in range(DEPTH):
        j = i // 2
        sh1, sc1, g1, sh2, sc2, g2 = [mod[i, q] for q in range(6)]
        nm, nf = norm_mix[i].reshape(1, D), norm_ffn[i].reshape(1, D)
        if i == 0:
            nm = nm + gather_token
        if i == 1:
            finish_gather(1, xcur)
        if i % 2 == 0:
            sink_rep = jnp.repeat(jnp.repeat(a_sink[j], TQ).reshape(2, 1, 8 * TQ), 8, axis=1).reshape(16, 8 * TQ)
            h1, qkv = _proj(xcur, nm, sc1, sh1, W[("a_in", j)], ffn=False, name="proj_a")
            o, lse = _attn_fwd(qkv, bias_a, sink_rep, out_dtype=BF16, name="attn_a_fwd", **a_geom)
            if i == 0:
                finish_gather(0, o)
            x1, y1, h2, gu, act = _out_ffn_in(o, W[("a_out", j)], xcur, g1, nf, sc2, sh2, W[("ffn_in", i)],
                                              w_is_transposed=False, name="out_a_ffn_in")
        else:
            sink_rep = None
            h1, qkv = _proj(xcur, nm, sc1, sh1, W[("b_in", j)], ffn=False, name="proj_b")
            outs = [_attn_fwd(qkv, bias_b_fwd[g], None, out_dtype=F32, name="attn_b%d_fwd" % g, **b_geom[g]) for g in range(3)]
            o, lse = _merge_groups([t[0] for t in outs], [t[1] for t in outs])
            x1, y1, h2, gu, act = _out_ffn_in(o, W[("b_out", j)], xcur, g1, nf, sc2, sh2, W[("ffn_in", i)],
                                              w_is_transposed=True, name="out_b_ffn_in")
        x2, y2 = _gated_residual(act, W[("ffn_out", i)], x1, g2, w_is_transposed=False, name="ffn_out")
        saved.append(dict(x0=xcur, h1=h1, qkv=qkv, o=o, lse=lse, y1=y1, x1=x1, h2=h2, gu=gu, act=act, y2=y2, sink=sink_rep))
        xcur = x2

    dx, head_stats = _loss_head(xcur, target, final_norm.reshape(1, D))

    dW = {}
    stat_tiles, dsink = [None] * DEPTH, [None] * 2
    exchanges = []
    start_token = None

    def start_exchange(segs):
        offs, total = _offsets(segs)
        own = jnp.concatenate([lax.dynamic_slice_in_dim(dW[(n, l)], me * rows, rows, axis=0) for n, l, rows in segs], axis=0)
        started = _exchange_start([dW[(n, l)] for n, l, _ in segs], lax.empty((N_DEV, total, D), BF16), [sg[2] for sg in segs],
                                  offs, True, own, "grad_exchange_start_%d" % len(exchanges))
        exchanges.append((segs, started, own))
        return started[-1][0:1, 0:1]

    for i in reversed(range(DEPTH)):
        j = i // 2
        sv = saved[i]
        sh1, sc1, g1, sh2, sc2, g2 = [mod[i, q] for q in range(6)]
        if start_token is not None:
            g2 = g2 + start_token
            start_token = None
        nm, nf = norm_mix[i].reshape(1, D), norm_ffn[i].reshape(1, D)
        mix = "a_out" if i % 2 == 0 else "b_out"
        dgu, dy2, dx1, do, dy1, st_g2, st_f, st_g1 = _ffn_bwd(
            dx, sv["y2"], g2, W[("ffn_out", i)], sv["gu"], W[("ffn_in", i)], sv["x1"], nf, sc2, sv["y1"], g1, W[(mix, j)],
            mix_is_transposed=(i % 2 == 1), name="ffn_bwd_" + mix)
        dW[("ffn_out", i)] = _weight_grad(dy2, sv["act"], transpose_out=True, name="dw_ffn_out")
        dW[("ffn_in", i)] = _weight_grad(sv["h2"], dgu, transpose_out=True, name="dw_ffn_in")
        sink_bwd = sv["sink"]
        if i == 0:
            sink_bwd = sink_bwd + start_exchange([sg for sg in _layer_segments(0) if sg[0].startswith("ffn")])
        if i % 2 == 0:
            dW[("a_out", j)] = _weight_grad(dy1, sv["o"], transpose_out=True, name="dw_a_out")
            dq, dk, dv, ds = _attn_bwd(sv["qkv"], bias_a, sink_bwd, sv["o"], do, sv["lse"], name="attn_a_bwd", **a_geom)
            dsink[j] = ds
            dqkv = jnp.concatenate([dq, dk[0].astype(BF16), dk[1].astype(BF16), dv[0].astype(BF16), dv[1].astype(BF16)], axis=1)
            dW[("a_in", j)] = _weight_grad(sv["h1"], dqkv, transpose_out=True, name="dw_a_in")
            dx0, st_m = _norm_bwd(dqkv, W[("a_in", j)], sv["x0"], dx1, nm, sc1, name="proj_a_bwd")
        else:
            dW[("b_out", j)] = _weight_grad(dy1, sv["o"], transpose_out=False, name="dw_b_out").reshape(N_DEV * 64, D)
            gr = [_attn_bwd(sv["qkv"], bias_b[g], None, sv["o"], do, sv["lse"], name="attn_b%d_bwd" % g, **b_geom[g]) for g in range(3)]
            dqkv = jnp.concatenate([t[0] for t in gr] + [t[1][0].astype(BF16) for t in gr] + [t[2][0].astype(BF16) for t in gr], axis=1)
            dW[("b_in", j)] = _weight_grad(sv["h1"], dqkv, transpose_out=True, name="dw_b_in")
            dx0, st_m = _norm_bwd(dqkv, W[("b_in", j)], sv["x0"], dx1, nm, sc1, name="proj_b_bwd")
        stat_tiles[i] = [st_m, st_g1, st_f, st_g2]
        if i > 0:
            start_token = start_exchange(_layer_segments(i))
        else:
            start_exchange([sg for sg in _layer_segments(0) if not sg[0].startswith("ffn")])
        dx = dx0
    grad_x = dx.reshape(1, S, D)

    masters = {"ffn_in": (ffn_w_in, m_ffn_w_in, v_ffn_w_in), "ffn_out": (ffn_w_out, m_ffn_w_out, v_ffn_w_out),
               "a_in": (a_w_in, m_a_w_in, v_a_w_in), "a_out": (a_w_out, m_a_w_out, v_a_w_out),
               "b_in": (b_w_in, m_b_w_in, v_b_w_in), "b_out": (b_w_out, m_b_w_out, v_b_w_out)}
    pieces = {}
    after = dx
    for segs, started, own in exchanges:
        offs, total = _offsets(segs)
        parts_g = _exchange_wait(started, after, "grad_exchange_wait_%d" % len(pieces))
        rows_wmv = [jnp.concatenate([_to_rows(n, masters[n][q][l]) for n, l, _ in segs], axis=0) for q in range(3)]
        res_rows = _adam_shard(parts_g, own, *rows_wmv, name="adam_%d" % total)
        after = res_rows[0]
        for q, kind in enumerate(("grad", "delta", "m", "v")):
            for (n, l, rows), off in zip(segs, offs):
                pieces[(kind, n, l)] = _from_rows(n, res_rows[q][off:off + rows])
    big = {(kind, n): jnp.stack([pieces[(kind, n, l)] for l in range(4 if n.startswith("ffn") else 2)])
           for kind in ("grad", "delta", "m", "v") for n in masters}

    tiles = jnp.concatenate([t for i in range(DEPTH) for t in stat_tiles[i]] + [head_stats]
                            + [jnp.pad(ds, ((0, 0), (0, D - LANES))) for ds in dsink], axis=0)
    small = [_pack_small(*t) for t in ((ada_b, norm_mix, norm_ffn, final_norm, a_sink),
                                       (m_ada_b, m_norm_mix, m_norm_ffn, m_final_norm, m_a_sink),
                                       (v_ada_b, v_norm_mix, v_norm_ffn, v_final_norm, v_a_sink))]
    dmod_all, sg, sd, sm, sv_, loss_tile = _small_exchange(tiles, *small)
    loss = loss_tile[0, 0]
    dmod_all = dmod_all.reshape(N_DEV, DEPTH, 6 * D)
    dmod_mine = lax.dynamic_slice_in_dim(dmod_all, me * ncol, ncol, axis=2)
    dmod_pad = jnp.pad(jnp.transpose(dmod_mine, (1, 0, 2)), ((0, 0), (0, LANES - N_DEV), (0, 0))).astype(BF16)
    cond_t = jnp.pad(cond_all.T, ((0, 0), (0, LANES - N_DEV))).astype(BF16)
    ada = _adam_ada_w(cond_t, dmod_pad, ada_w, m_ada_w, v_ada_w)

    outs = [loss, grad_x]
    small_res = [_unpack_small(t) for t in (sg, sd, sm, sv_)]
    for q, kind in enumerate(("grad", "delta", "m", "v")):
        ab, nm_, nf_, fn, sk = small_res[q]
        outs += [ada[q], ab, nm_, nf_, big[(kind, "ffn_in")], big[(kind, "ffn_out")], big[(kind, "a_in")], big[(kind, "a_out")],
                 sk, big[(kind, "b_in")], big[(kind, "b_out")], fn]
    return tuple(outs)
```

```python
import functools
import math

import numpy as np
import jax
import jax.numpy as jnp
from jax import lax
from jax.experimental import pallas as pl
from jax.experimental.pallas import tpu as pltpu

D = 1024
HEAD_DIM = 64
D_FF = 2816
DEPTH = 4
N_DEV = 8
A_QKV = 1536
B_QKV = 2304
A_HALF = 128
B_HALF = 64
B_DILS = (1, 4, 16)
RMS_EPS = 1e-6
NEG = -1e30
ADAM_LR = 0.001
ADAM_B1 = 0.9
ADAM_B2 = 0.999
ADAM_EPS = 1e-08
ADAM_WD = 0.01
ADAM_STEP = 10

LANES = 128
SPLIT = 2
TQ = 128
VMEM_LIMIT = 56 * 1024 * 1024
MESH = pl.DeviceIdType.MESH
F32 = jnp.float32
BF16 = jnp.bfloat16

SEGMENTS = ([("ffn_in", l, 704) for l in range(4)] + [("ffn_out", l, 352) for l in range(4)]
            + [("a_in", j, 192) for j in range(2)] + [("a_out", j, 128) for j in range(2)]
            + [("b_in", j, 288) for j in range(2)] + [("b_out", j, 64) for j in range(2)])
def _layer_segments(i):
    mixer = "a" if i % 2 == 0 else "b"
    return [s for s in SEGMENTS if (s[0].startswith("ffn") and s[1] == i) or (s[0].startswith(mixer + "_") and s[1] == i // 2)]


def _offsets(segs):
    rows = [s[2] for s in segs]
    return [sum(rows[:k]) for k in range(len(rows))], sum(rows)
STAT_ROWS = 56


def _nn(a, b):
    return jnp.dot(a, b, preferred_element_type=F32)


def _nt(a, b):
    return lax.dot_general(a, b, (((1,), (1,)), ((), ())), preferred_element_type=F32)


def _tn(a, b):
    return lax.dot_general(a, b, (((0,), (0,)), ((), ())), preferred_element_type=F32)


def _params(dims=None, vmem=None):
    kw = {}
    if dims is not None:
        kw["dimension_semantics"] = dims
    if vmem is not None:
        kw["vmem_limit_bytes"] = vmem
    return pltpu.CompilerParams(**kw)


def _my_index():
    return 4 * lax.axis_index("x") + 2 * lax.axis_index("y") + lax.axis_index("c")


def _peer(k):
    x, y, c = lax.axis_index("x"), lax.axis_index("y"), lax.axis_index("c")
    px, py, pc = x ^ ((k >> 2) & 1), y ^ ((k >> 1) & 1), c ^ (k & 1)
    return (px, py, pc), 4 * px + 2 * py + pc


def _const_spec(shape):
    nd = len(shape)
    return pl.BlockSpec(shape, lambda *_: (0,) * nd)


def _cond_exchange(c_tile, ada_w, ada_b_mine):
    ncol = ada_w.shape[-1]

    def body(c_ref, w_ref, b_ref, cond_ref, parts_ref, call_ref, mine_ref, send_sems, recv_sems):
        me = _my_index()
        call_ref[me] = c_ref[...]
        copies = []
        for k in range(1, N_DEV):
            dev, _ = _peer(k)
            cp = pltpu.make_async_remote_copy(src_ref=c_ref, dst_ref=call_ref.at[me], send_sem=send_sems.at[0, k - 1],
                                              recv_sem=recv_sems.at[0, k - 1], device_id=dev, device_id_type=MESH)
            cp.start()
            copies.append(cp)
        for k in range(1, N_DEV):
            _, pidx = _peer(k)
            pltpu.make_async_remote_copy(src_ref=c_ref, dst_ref=call_ref.at[pidx], send_sem=send_sems.at[0, k - 1],
                                         recv_sem=recv_sems.at[0, k - 1], device_id=_peer(k)[0], device_id_type=MESH).wait_recv()
        for cp in copies:
            cp.wait_send()
        row = lax.broadcasted_iota(jnp.int32, (N_DEV, D), 0)
        cmat = jnp.zeros((N_DEV, D), F32)
        for j in range(N_DEV):
            cmat = jnp.where(row == j, call_ref[j], cmat)
        cond = cmat * jax.nn.sigmoid(cmat)
        cond_ref[...] = cond
        cb = cond.astype(BF16)
        for l in range(DEPTH):
            mine_ref[l] = _nn(cb, w_ref[l].astype(BF16)) + b_ref[pl.ds(l, 1), :]
        parts_ref[me] = mine_ref[...]
        copies = []
        for k in range(1, N_DEV):
            dev, _ = _peer(k)
            cp = pltpu.make_async_remote_copy(src_ref=mine_ref, dst_ref=parts_ref.at[me], send_sem=send_sems.at[1, k - 1],
                                              recv_sem=recv_sems.at[1, k - 1], device_id=dev, device_id_type=MESH)
            cp.start()
            copies.append(cp)
        for k in range(1, N_DEV):
            dev, pidx = _peer(k)
            pltpu.make_async_remote_copy(src_ref=mine_ref, dst_ref=parts_ref.at[pidx], send_sem=send_sems.at[1, k - 1],
                                         recv_sem=recv_sems.at[1, k - 1], device_id=dev, device_id_type=MESH).wait_recv()
        for cp in copies:
            cp.wait_send()

    vm = pl.BlockSpec(memory_space=pltpu.VMEM)
    return pl.pallas_call(
        body, name="cond_exchange",
        out_shape=(jax.ShapeDtypeStruct((N_DEV, D), F32), jax.ShapeDtypeStruct((N_DEV, DEPTH, N_DEV, ncol), F32)),
        in_specs=[vm, vm, vm], out_specs=(vm, vm),
        scratch_shapes=[pltpu.VMEM((N_DEV, N_DEV, D), F32), pltpu.VMEM((DEPTH, N_DEV, ncol), F32),
                        pltpu.SemaphoreType.DMA((2, N_DEV - 1)), pltpu.SemaphoreType.DMA((2, N_DEV - 1))],
        compiler_params=_params(vmem=VMEM_LIMIT),
    )(c_tile, ada_w, ada_b_mine)[:2]


def _all_gather_weights(shards):
    n = len(shards)
    big = max(range(n), key=lambda s: shards[s].shape[0])
    total = sum(sh.shape[0] for sh in shards)
    assert N_DEV * shards[big].shape[0] >= total

    def body(*refs):
        ins, outs = refs[:n], refs[n:2 * n]
        local_sems, send_sems, recv_sems = refs[2 * n:]
        me = _my_index()
        local = []
        for s in range(n):
            rows = ins[s].shape[0]
            cp = pltpu.make_async_copy(ins[s], outs[s].at[pl.ds(me * rows, rows)], local_sems.at[s])
            cp.start()
            local.append(cp)
        for k in range(1, N_DEV):
            dev, _ = _peer(k)
            for s in range(n):
                rows = ins[s].shape[0]
                pltpu.make_async_remote_copy(src_ref=ins[s], dst_ref=outs[s].at[pl.ds(me * rows, rows)],
                                             send_sem=send_sems.at[k - 1], recv_sem=recv_sems.at[k - 1],
                                             device_id=dev, device_id_type=MESH).start()
        whole = outs[big].at[pl.ds(0, total)]
        for k in range(1, N_DEV):
            dev, _ = _peer(k)
            w = pltpu.make_async_remote_copy(src_ref=whole, dst_ref=whole, send_sem=send_sems.at[k - 1],
                                             recv_sem=recv_sems.at[k - 1], device_id=dev, device_id_type=MESH)
            w.wait_send()
            w.wait_recv()
        for cp in local:
            cp.wait()

    hbm = pl.BlockSpec(memory_space=pl.ANY)
    return pl.pallas_call(
        body, name="weight_all_gather",
        out_shape=tuple(jax.ShapeDtypeStruct((N_DEV * s.shape[0], D), s.dtype) for s in shards),
        in_specs=[hbm] * n, out_specs=tuple([hbm] * n),
        scratch_shapes=[pltpu.SemaphoreType.DMA((n,)), pltpu.SemaphoreType.DMA((N_DEV - 1,)),
                        pltpu.SemaphoreType.DMA((N_DEV - 1,))],
    )(*shards)


HBM = pl.BlockSpec(memory_space=pltpu.HBM)
SEM = pl.BlockSpec(memory_space=pltpu.SEMAPHORE)
EFFECT = pltpu.SideEffectType.DATAFLOW_SIDE_EFFECTING


def _exchange_start(srcs, landing, rows, offs, to_peer_rows, after, name):
    n = len(srcs)

    def body(*refs):
        src_refs, land_ref = refs[:n], refs[n]
        send_sems, recv_sems = refs[n + 2], refs[n + 3]
        token = refs[-1]
        me = _my_index()
        for k in range(1, N_DEV):
            dev, pidx = _peer(k)
            for q in range(n):
                src = src_refs[q].at[pl.ds(pidx * rows[q], rows[q])] if to_peer_rows else src_refs[q]
                pltpu.make_async_remote_copy(src_ref=src, dst_ref=land_ref.at[me, pl.ds(offs[q], rows[q])],
                                             send_sem=send_sems.at[k - 1], recv_sem=recv_sems.at[k - 1],
                                             device_id=dev, device_id_type=MESH).start()
        token[...] = jnp.zeros_like(token)

    arrays = list(srcs) + [landing]
    return pl.pallas_call(
        body, name=name,
        out_shape=(pltpu.SemaphoreType.DMA((N_DEV - 1,)), pltpu.SemaphoreType.DMA((N_DEV - 1,)),
                   *[pltpu.HBM(a.shape, a.dtype) for a in arrays], jax.ShapeDtypeStruct((8, LANES), F32)),
        in_specs=[HBM] * (n + 1) + [pl.BlockSpec(memory_space=pl.ANY)],
        out_specs=(SEM, SEM, *[HBM] * (n + 1), pl.BlockSpec(memory_space=pltpu.VMEM)),
        input_output_aliases={q: 2 + q for q in range(n + 1)},
        compiler_params=pltpu.CompilerParams(has_side_effects=EFFECT),
    )(*[pltpu.with_memory_space_constraint(a, pltpu.HBM) for a in arrays], after)


def _exchange_wait(started, after, name):
    send_sems, recv_sems = started[0], started[1]
    arrays = list(started[2:-1])
    n1 = len(arrays)

    def body(*refs):
        land_ref = refs[n1 - 1]
        sends, recvs = refs[n1], refs[n1 + 1]
        for k in range(1, N_DEV):
            dev, _ = _peer(k)
            w = pltpu.make_async_remote_copy(src_ref=land_ref.at[0], dst_ref=land_ref.at[0], send_sem=sends.at[k - 1],
                                             recv_sem=recvs.at[k - 1], device_id=dev, device_id_type=MESH)
            w.wait_send()
            w.wait_recv()

    return pl.pallas_call(
        body, name=name, out_shape=tuple(pltpu.HBM(a.shape, a.dtype) for a in arrays),
        in_specs=[HBM] * n1 + [SEM, SEM, pl.BlockSpec(memory_space=pl.ANY)], out_specs=tuple([HBM] * n1),
        input_output_aliases={q: q for q in range(n1)},
        compiler_params=pltpu.CompilerParams(has_side_effects=EFFECT),
    )(*arrays, send_sems, recv_sems, after)[n1 - 1]


def _norm_mod(x, nw, sc, sh):
    ms = jnp.mean(x * x, axis=-1, keepdims=True)
    xh = x * lax.rsqrt(ms + RMS_EPS)
    return xh, (xh * nw) * (1.0 + sc) + sh


def _proj(x, nw, sc, sh, wt, *, name):
    S, N = x.shape[0], wt.shape[0]
    tm = 512

    def body(x_ref, nw_ref, sc_ref, sh_ref, w_ref, h_ref, out_ref):
        for half in range(SPLIT):
            rows = pl.ds(half * (tm // SPLIT), tm // SPLIT)
            _, h = _norm_mod(x_ref[rows, :], nw_ref[...], sc_ref[...], sh_ref[...])
            hb = h.astype(BF16)
            h_ref[rows, :] = hb
            out_ref[rows, :] = _nt(hb, w_ref[...]).astype(BF16)

    row = lambda w: pl.BlockSpec((tm, w), lambda i: (i, 0))
    vec = _const_spec((1, D))
    return pl.pallas_call(
        body, name=name, grid=(S // tm,), out_shape=(jax.ShapeDtypeStruct((S, D), BF16), jax.ShapeDtypeStruct((S, N), BF16)),
        in_specs=[row(D), vec, vec, vec, _const_spec((N, D))], out_specs=(row(D), row(N)),
        compiler_params=_params(("parallel",), VMEM_LIMIT),
    )(x, nw, sc, sh, wt)


def _ffn_out(a, w, x, g, target, fnw, *, name):
    S, K = a.shape
    tm = 512
    last = target is not None

    def body(a_ref, w_ref, x_ref, g_ref, *rest):
        y = _nn(a_ref[...], w_ref[...])
        xv = x_ref[...] + g_ref[...] * y
        if not last:
            xo_ref, y_ref = rest
            y_ref[...] = y.astype(BF16)
            xo_ref[...] = xv
            return
        t_ref, fw_ref, dx_ref, y_ref, st_ref = rest
        y_ref[...] = y.astype(BF16)

        @pl.when(pl.program_id(0) == 0)
        def _():
            st_ref[...] = jnp.zeros_like(st_ref)

        rstd = lax.rsqrt(jnp.mean(xv * xv, axis=-1, keepdims=True) + RMS_EPS)
        xh = xv * rstd
        err = xh * fw_ref[...] - t_ref[...]
        dy = err * (1.0 / D)
        dxh = dy * fw_ref[...]
        dx_ref[...] = rstd * (dxh - xh * jnp.mean(dxh * xh, axis=-1, keepdims=True))
        st_ref[pl.ds(0, 1), :] = st_ref[pl.ds(0, 1), :] + jnp.sum(dy * xh, axis=0, keepdims=True)
        st_ref[pl.ds(1, 1), :] = st_ref[pl.ds(1, 1), :] + jnp.sum(err * err, axis=0, keepdims=True)

    row = lambda w_: pl.BlockSpec((tm, w_), lambda i: (i, 0))
    in_specs = [row(K), _const_spec(w.shape), row(D), _const_spec((1, D))]
    args = [a, w, x, g]
    out_shape = [jax.ShapeDtypeStruct((S, D), F32), jax.ShapeDtypeStruct((S, D), BF16)]
    out_specs = [row(D), row(D)]
    if last:
        in_specs += [row(D), _const_spec((1, D))]
        args += [target, fnw]
        out_shape.append(jax.ShapeDtypeStruct((8, D), F32))
        out_specs.append(_const_spec((8, D)))
    return pl.pallas_call(
        body, name=name, grid=(S // tm,), out_shape=tuple(out_shape), in_specs=in_specs, out_specs=tuple(out_specs),
        compiler_params=_params(("arbitrary",) if last else ("parallel",), VMEM_LIMIT),
    )(*args)


CHUNK = 1024


def _tile_rows(r, chunk=CHUNK):
    return min(TQ, chunk // r)


def _out_ffn_in(a, w_mix, x, g, nw, sc, sh, wt, *, w_is_transposed, name):
    S, K = a.shape
    tm = 256

    def body(a_ref, wm_ref, x_ref, g_ref, nw_ref, sc_ref, sh_ref, w_ref, x1_ref, y_ref, h_ref, gu_ref, act_ref):
        y = _nt(a_ref[...], wm_ref[...]) if w_is_transposed else _nn(a_ref[...], wm_ref[...])
        y_ref[...] = y.astype(BF16)
        x1 = x_ref[...] + g_ref[...] * y
        x1_ref[...] = x1
        _, h = _norm_mod(x1, nw_ref[...], sc_ref[...], sh_ref[...])
        hb = h.astype(BF16)
        h_ref[...] = hb
        gate = _nt(hb, w_ref[pl.ds(0, D_FF), :])
        up = _nt(hb, w_ref[pl.ds(D_FF, D_FF), :])
        sig = jax.nn.sigmoid(gate)
        silu = gate * sig
        gu_ref[:, pl.ds(0, D_FF)] = (up * (sig * (1.0 + gate * (1.0 - sig)))).astype(BF16)
        gu_ref[:, pl.ds(D_FF, D_FF)] = silu.astype(BF16)
        act_ref[...] = (silu * up).astype(BF16)

    row = lambda w_: pl.BlockSpec((tm, w_), lambda i: (i, 0))
    vec = _const_spec((1, D))
    return pl.pallas_call(
        body, name=name, grid=(S // tm,),
        out_shape=(jax.ShapeDtypeStruct((S, D), F32), jax.ShapeDtypeStruct((S, D), BF16), jax.ShapeDtypeStruct((S, D), BF16),
                   jax.ShapeDtypeStruct((S, 2 * D_FF), BF16), jax.ShapeDtypeStruct((S, D_FF), BF16)),
        in_specs=[row(K), _const_spec(w_mix.shape), row(D), vec, vec, vec, vec, _const_spec(wt.shape)],
        out_specs=(row(D), row(D), row(D), row(2 * D_FF), row(D_FF)),
        compiler_params=_params(("parallel",), VMEM_LIMIT),
    )(a, w_mix, x, g, nw, sc, sh, wt)


def _alibi_bias(slopes, half, dil, chunk=CHUNK, both=False):
    tq = _tile_rows(dil, chunk)
    tk = tq + 2 * half
    rel = np.arange(tk)[:, None] - half - np.arange(tq)[None, :]
    band = np.abs(rel) <= half
    dist = (dil * np.abs(rel)).astype(np.float32)
    tabs = [np.where(band, -np.float32(s) * dist, np.float32(NEG)).astype(np.float32) for s in slopes]
    out = []
    for u in range(0, len(tabs), 8):
        tab = np.concatenate(tabs[u:u + 8], axis=1)
        first, last = tab.copy(), tab.copy()
        first[:half] = NEG
        last[tk - half:] = NEG
        out += [tab, first, last]
        if both:
            last = last.copy()
            last[:half] = NEG
            out.append(last)
    return jnp.asarray(np.concatenate(out, axis=0))


def _slopes(n):
    return (2.0 ** (-8.0 * np.arange(1, n + 1) / n)).astype(np.float32)


def _head_masks(tq):
    lane = lax.broadcasted_iota(jnp.int32, (tq, LANES), 1)
    lo = lane < HEAD_DIM
    return lo, jnp.logical_not(lo)


def _stack_heads(tiles, lo, hi, scale):
    blocks = []
    for t in range(4):
        xf = tiles[t] if scale == 1.0 else tiles[t] * scale
        for a in range(2):
            xm = jnp.where(lo if a == 0 else hi, xf, 0.0)
            if a != t // 2:
                xm = pltpu.roll(xm, HEAD_DIM, 1)
            blocks.append(xm.astype(BF16))
    return jnp.concatenate(blocks, axis=0)


def _tile_from_columns(x8t, t, tq):
    r0 = HEAD_DIM * (t // 2)
    top = x8t[r0:r0 + HEAD_DIM, 2 * t * tq:(2 * t + 1) * tq]
    bot = x8t[r0:r0 + HEAD_DIM, (2 * t + 1) * tq:(2 * t + 2) * tq]
    return jnp.concatenate([top, bot], axis=0).T


def _attn_layout(S, C, r, half, qoff, koff, voff, chunk):
    hb = half * r
    per = chunk // hb
    nhb = S // hb
    main = lambda off: pl.BlockSpec((chunk, LANES), lambda u, i: (i, off // LANES + u))
    prev = lambda off: pl.BlockSpec((hb, LANES), lambda u, i: (jnp.maximum(i * per - 1, 0), off // LANES + u))
    nxt = lambda off: pl.BlockSpec((hb, LANES), lambda u, i: (jnp.minimum((i + 1) * per, nhb - 1), off // LANES + u))
    specs = [pl.BlockSpec((chunk, 4 * LANES), lambda u, i: (i, qoff // (4 * LANES) + u))]
    specs += [prev(koff), main(koff), nxt(koff), prev(voff), main(voff), nxt(voff)]
    return specs, hb


def _stage(dst, srcs):
    row = 0
    for src in srcs:
        n = src.shape[0]
        dst[pl.ds(row, n), :] = src[...].astype(F32)
        row += n


def _rows(start, n, r):
    return pl.ds(start, n, stride=r) if r > 1 else pl.ds(start, n)


def _attn_fwd(qkv, bias, sink, *, C, r, half, qoff, koff, voff, n_units, out_dtype, name):
    S = qkv.shape[0]
    chunk = max(CHUNK, TQ * r)
    tq = _tile_rows(r, chunk)
    tk = tq + 2 * half
    tiles = chunk // (r * tq)
    nsteps = S // chunk
    specs, hb = _attn_layout(S, C, r, half, qoff, koff, voff, chunk)
    use_sink = sink is not None

    def body(*refs):
        q_ref, kp, km, kn, vp, vm, vn, bias_ref = refs[:8]
        rest = list(refs[8:])
        sink_ref = rest.pop(0) if use_sink else None
        o_ref, lse_ref, qs, ks, vs, os_, ls = rest
        i = pl.program_id(1)
        if r > 1:
            for t in range(4):
                qs[t] = q_ref[:, pl.ds(t * LANES, LANES)].astype(F32)
        _stage(ks, [kp, km, kn])
        _stage(vs, [vp, vm, vn])
        lo, hi = _head_masks(tq)

        def tile_in(staged, ref, t, start):
            if r > 1:
                return staged[t, _rows(start, tq, r), :]
            return ref[pl.ds(start, tq), pl.ds(t * LANES, LANES)].astype(F32)

        ones = jnp.ones((16, tk), BF16)
        if use_sink:
            sk = sink_ref[pl.ds(0, 1), :]

        def chain(n, carry):
            rho, c = n // tiles, n % tiles
            start = c * (tq * r) + rho
            if r == 1:
                start = pl.multiple_of(start, tq)
            variant = jnp.where(jnp.logical_and(i == 0, c == 0), 1, 0) + jnp.where(
                jnp.logical_and(i == nsteps - 1, c == tiles - 1), 2, 0)
            k2 = ks[_rows(start, tk, r), :].astype(BF16)
            v2t = jnp.concatenate([vs[_rows(start, tk, r), :].T.astype(BF16), ones], axis=0)
            q8 = _stack_heads([tile_in(qs, q_ref, t, start) for t in range(4)], lo, hi, HEAD_DIM ** -0.5)
            s = _nt(k2, q8) + bias_ref[pl.ds(pl.multiple_of(variant * tk, 8), tk), :]
            m = jnp.max(s, axis=0, keepdims=True)
            if use_sink:
                m = jnp.maximum(m, sk)
            pv = _nn(v2t, jnp.exp(s - m).astype(BF16))
            l = pv[LANES:LANES + 1]
            if use_sink:
                l = l + jnp.exp(sk - m)
            o8t = pv[:LANES] / l
            lse8 = jnp.broadcast_to(m + jnp.log(l), (LANES, 8 * tq))
            for t in range(4):
                if r > 1:
                    os_[t, _rows(start, tq, r), :] = _tile_from_columns(o8t, t, tq)
                    ls[t, _rows(start, tq, r), :] = _tile_from_columns(lse8, t, tq)
                else:
                    o_ref[pl.ds(start, tq), pl.ds(t * LANES, LANES)] = _tile_from_columns(o8t, t, tq).astype(out_dtype)
                    lse_ref[pl.ds(start, tq), pl.ds(t * LANES, LANES)] = _tile_from_columns(lse8, t, tq)
            return carry

        lax.fori_loop(0, r * tiles, chain, 0, unroll=4)
        if r > 1:
            for t in range(4):
                o_ref[:, pl.ds(t * LANES, LANES)] = os_[t].astype(out_dtype)
                lse_ref[:, pl.ds(t * LANES, LANES)] = ls[t]

    in_specs = specs + [pl.BlockSpec((bias.shape[0] // n_units, 8 * tq), lambda u, i: (u, 0))]
    args = [qkv] * 7 + [bias]
    if use_sink:
        in_specs.append(pl.BlockSpec((8, 8 * tq), lambda u, i: (u, 0)))
        args.append(sink)
    wide = pl.BlockSpec((chunk, 4 * LANES), lambda u, i: (i, u))
    win = hb + chunk + hb
    big = lambda: pltpu.VMEM((4, chunk if r > 1 else 8, LANES), F32)
    return pl.pallas_call(
        body, name=name, grid=(n_units, nsteps),
        out_shape=(jax.ShapeDtypeStruct((S, n_units * 512), out_dtype), jax.ShapeDtypeStruct((S, n_units * 512), F32)),
        in_specs=in_specs, out_specs=(wide, wide),
        scratch_shapes=[big(), pltpu.VMEM((win, LANES), F32), pltpu.VMEM((win, LANES), F32), big(), big()],
        compiler_params=_params(("parallel", "parallel"), VMEM_LIMIT),
    )(*args)


def _attn_bwd(qkv, bias, sink, o, do, lse, *, C, r, half, qoff, koff, voff, n_units, name):
    S = qkv.shape[0]
    tq = _tile_rows(r)
    tk = tq + 2 * half
    tiles = CHUNK // (r * tq)
    nsteps = S // CHUNK
    specs, hb = _attn_layout(S, C, r, half, qoff, koff, voff, CHUNK)
    use_sink = sink is not None

    def body(*refs):
        q_ref, kp, km, kn, vp, vm, vn, bias_ref = refs[:8]
        rest = list(refs[8:])
        sink_ref = rest.pop(0) if use_sink else None
        o_ref, do_ref, lse_ref, dq_ref, dk_hbm, dv_hbm = rest[:6]
        rest = rest[6:]
        dsink_ref = rest.pop(0) if use_sink else None
        qs, ks, vs, os_, dos, ls, dqs, acck, accv, sem = rest
        u, i = pl.program_id(0), pl.program_id(1)

        @pl.when(i == 0)
        def _():
            acck[...] = jnp.zeros_like(acck)
            accv[...] = jnp.zeros_like(accv)
            if use_sink:
                dsink_ref[...] = jnp.zeros_like(dsink_ref)

        if r > 1:
            for t in range(4):
                cols = pl.ds(t * LANES, LANES)
                qs[t] = q_ref[:, cols].astype(F32)
                os_[t] = o_ref[:, cols].astype(F32)
                dos[t] = do_ref[:, cols].astype(F32)
                ls[t] = lse_ref[:, cols]
        _stage(ks, [kp, km, kn])
        _stage(vs, [vp, vm, vn])
        lo, hi = _head_masks(tq)

        def tile_in(staged, ref, t, start):
            if r > 1:
                return staged[t, _rows(start, tq, r), :]
            return ref[pl.ds(start, tq), pl.ds(t * LANES, LANES)].astype(F32)

        base = pl.multiple_of(i * CHUNK, CHUNK)
        if use_sink:
            sk = sink_ref[pl.ds(0, 1), :]

        def chain(n, carry):
            rho, c = n // tiles, n % tiles
            start = c * (tq * r) + rho
            if r == 1:
                start = pl.multiple_of(start, tq)
            variant = jnp.where(jnp.logical_and(i == 0, c == 0), 1, 0) + jnp.where(
                jnp.logical_and(i == nsteps - 1, c == tiles - 1), 2, 0)
            k2 = ks[_rows(start, tk, r), :].astype(BF16)
            v2 = vs[_rows(start, tk, r), :].astype(BF16)
            k2t = ks[_rows(start, tk, r), :].T.astype(BF16)
            q8 = _stack_heads([tile_in(qs, q_ref, t, start) for t in range(4)], lo, hi, HEAD_DIM ** -0.5)
            do_tiles = [tile_in(dos, do_ref, t, start) for t in range(4)]
            do8 = _stack_heads(do_tiles, lo, hi, 1.0)
            deltas, lses = [], []
            for t in range(4):
                prod_t = (do_tiles[t] * tile_in(os_, o_ref, t, start)).T
                lse_t = tile_in(ls, lse_ref, t, start).T
                for a in range(2):
                    deltas.append(jnp.sum(prod_t[a * HEAD_DIM:(a + 1) * HEAD_DIM], axis=0, keepdims=True))
                    lses.append(lse_t[a * HEAD_DIM:a * HEAD_DIM + 1])
            delta8 = jnp.concatenate(deltas, axis=1)
            lse8 = jnp.concatenate(lses, axis=1)
            s = _nt(k2, q8) + bias_ref[pl.ds(pl.multiple_of(variant * tk, 8), tk), :]
            p = jnp.exp(s - lse8)
            dp = _nt(v2, do8)
            dsb = (p * (dp - delta8)).astype(BF16)
            dq8t = _nn(k2t, dsb)
            for t in range(4):
                dq_t = _tile_from_columns(dq8t, t, tq) * (HEAD_DIM ** -0.5)
                if r > 1:
                    dqs[t, _rows(start, tq, r), :] = dq_t
                else:
                    dq_ref[pl.ds(start, tq), pl.ds(t * LANES, LANES)] = dq_t.astype(BF16)
            arow = base + start
            if r == 1:
                arow = pl.multiple_of(arow, tq)
            acck[_rows(arow, tk, r), :] = acck[_rows(arow, tk, r), :] + _nn(dsb, q8)
            accv[_rows(arow, tk, r), :] = accv[_rows(arow, tk, r), :] + _nn(p.astype(BF16), do8)
            if use_sink:
                e = jnp.exp(sk - lse8) * delta8
                for h in range(8):
                    part = -jnp.sum(e[:, h * tq:(h + 1) * tq], axis=1, keepdims=True)
                    dsink_ref[pl.ds(h, 1), :] = dsink_ref[pl.ds(h, 1), :] + part
            return carry

        lax.fori_loop(0, r * tiles, chain, 0, unroll=2)
        if r > 1:
            for t in range(4):
                dq_ref[:, pl.ds(t * LANES, LANES)] = dqs[t].astype(BF16)

        @pl.when(i == nsteps - 1)
        def _():
            ck = pltpu.make_async_copy(acck.at[pl.ds(hb, S)], dk_hbm.at[u], sem.at[0])
            cv = pltpu.make_async_copy(accv.at[pl.ds(hb, S)], dv_hbm.at[u], sem.at[1])
            ck.start()
            cv.start()
            ck.wait()
            cv.wait()

    wide = pl.BlockSpec((CHUNK, 4 * LANES), lambda u, i: (i, u))
    hbm = pl.BlockSpec(memory_space=pl.ANY)
    in_specs = specs + [pl.BlockSpec((3 * tk, 8 * tq), lambda u, i: (u, 0))]
    args = [qkv] * 7 + [bias]
    if use_sink:
        in_specs.append(pl.BlockSpec((8, 8 * tq), lambda u, i: (u, 0)))
        args.append(sink)
    in_specs += [wide, wide, wide]
    args += [o, do, lse]
    out_shape = [jax.ShapeDtypeStruct((S, n_units * 512), BF16), jax.ShapeDtypeStruct((n_units, S, LANES), F32),
                 jax.ShapeDtypeStruct((n_units, S, LANES), F32)]
    out_specs = [wide, hbm, hbm]
    if use_sink:
        out_shape.append(jax.ShapeDtypeStruct((n_units * 8, LANES), F32))
        out_specs.append(pl.BlockSpec((8, LANES), lambda u, i: (u, 0)))
    win = hb + CHUNK + hb
    big = lambda: pltpu.VMEM((4, CHUNK if r > 1 else 8, LANES), F32)
    res = pl.pallas_call(
        body, name=name, grid=(n_units, nsteps), out_shape=tuple(out_shape), in_specs=in_specs, out_specs=tuple(out_specs),
        scratch_shapes=[big(), pltpu.VMEM((win, LANES), F32), pltpu.VMEM((win, LANES), F32), big(), big(), big(), big(),
                        pltpu.VMEM((S + 2 * hb, LANES), F32), pltpu.VMEM((S + 2 * hb, LANES), F32), pltpu.SemaphoreType.DMA((2,))],
        compiler_params=_params(("arbitrary", "arbitrary"), VMEM_LIMIT),
    )(*args)
    return res[0], res[1], res[2], (res[3] if use_sink else None)


def _merge_groups(os_, lses):
    S, W = os_[0].shape
    tm = 512

    def body(o0, o1, o2, l0, l1, l2, o_ref, lse_ref):
        ls = [l0[...], l1[...], l2[...]]
        mx = jnp.maximum(jnp.maximum(ls[0], ls[1]), ls[2])
        es = [jnp.exp(l - mx) for l in ls]
        den = es[0] + es[1] + es[2]
        o = (es[0] / den) * o0[...] + (es[1] / den) * o1[...] + (es[2] / den) * o2[...]
        o_ref[...] = o.astype(BF16)
        lse_ref[...] = mx + jnp.log(den)

    row = pl.BlockSpec((tm, W), lambda i: (i, 0))
    return pl.pallas_call(
        body, name="merge_groups", grid=(S // tm,),
        out_shape=(jax.ShapeDtypeStruct((S, W), BF16), jax.ShapeDtypeStruct((S, W), F32)),
        in_specs=[row] * 6, out_specs=(row, row), compiler_params=_params(("parallel",), VMEM_LIMIT),
    )(*os_, *lses)


def _norm_bwd(dy, wt, x, dres, nw, sc, *, name):
    S, N = dy.shape
    tm = 512

    def body(dy_ref, w_ref, x_ref, dres_ref, nw_ref, sc_ref, dx_ref, st_ref):
        @pl.when(pl.program_id(0) == 0)
        def _():
            st_ref[...] = jnp.zeros_like(st_ref)

        nwv, scale = nw_ref[...], 1.0 + sc_ref[...]
        sums = [jnp.zeros((1, D), F32)] * 3
        for half in range(SPLIT):
            rows = pl.ds(half * (tm // SPLIT), tm // SPLIT)
            dh = _nn(dy_ref[rows, :], w_ref[...])
            xv = x_ref[rows, :]
            rstd = lax.rsqrt(jnp.mean(xv * xv, axis=-1, keepdims=True) + RMS_EPS)
            xh = xv * rstd
            dxh = dh * (nwv * scale)
            dx_ref[rows, :] = dres_ref[rows, :] + rstd * (dxh - xh * jnp.mean(dxh * xh, axis=-1, keepdims=True))
            dhx = dh * xh
            sums = [sums[0] + jnp.sum(dh, axis=0, keepdims=True), sums[1] + jnp.sum(dhx * nwv, axis=0, keepdims=True),
                    sums[2] + jnp.sum(dhx * scale, axis=0, keepdims=True)]
        for q in range(3):
            st_ref[pl.ds(q, 1), :] = st_ref[pl.ds(q, 1), :] + sums[q]

    row = lambda w_: pl.BlockSpec((tm, w_), lambda i: (i, 0))
    vec = _const_spec((1, D))
    return pl.pallas_call(
        body, name=name, grid=(S // tm,),
        out_shape=(jax.ShapeDtypeStruct((S, D), F32), jax.ShapeDtypeStruct((8, D), F32)),
        in_specs=[row(N), _const_spec((N, D)), row(D), row(D), vec, vec], out_specs=(row(D), _const_spec((8, D))),
        compiler_params=_params(("arbitrary",), VMEM_LIMIT),
    )(dy, wt, x, dres, nw, sc)


def _ffn_bwd(dx, y, g, w_out, gu, wt_in, x, nw, sc, y1, g1, w_mix, *, mix_is_transposed, name):
    S = dx.shape[0]
    K = w_out.shape[0]
    Km = w_mix.shape[1] if mix_is_transposed else w_mix.shape[0]
    tm = 256

    def body(dx_ref, y_ref, g_ref, wo_ref, gu_ref, wi_ref, x_ref, nw_ref, sc_ref, y1_ref, g1_ref, wm_ref,
             dgu_ref, dyb_ref, dxo_ref, da_ref, dy1_ref, stg_ref, stf_ref, stm_ref):
        @pl.when(pl.program_id(0) == 0)
        def _():
            stg_ref[...] = jnp.zeros_like(stg_ref)
            stf_ref[...] = jnp.zeros_like(stf_ref)
            stm_ref[...] = jnp.zeros_like(stm_ref)

        dxv = dx_ref[...]
        stg_ref[pl.ds(0, 1), :] = stg_ref[pl.ds(0, 1), :] + jnp.sum(dxv * y_ref[...].astype(F32), axis=0, keepdims=True)
        dyb = (dxv * g_ref[...]).astype(BF16)
        dyb_ref[...] = dyb
        da = _nt(dyb, wo_ref[...])
        dgate = (da * gu_ref[:, pl.ds(0, K)].astype(F32)).astype(BF16)
        dup = (da * gu_ref[:, pl.ds(K, K)].astype(F32)).astype(BF16)
        dgu_ref[:, pl.ds(0, K)] = dgate
        dgu_ref[:, pl.ds(K, K)] = dup
        dh = _nn(dgate, wi_ref[pl.ds(0, K), :]) + _nn(dup, wi_ref[pl.ds(K, K), :])
        xv = x_ref[...]
        rstd = lax.rsqrt(jnp.mean(xv * xv, axis=-1, keepdims=True) + RMS_EPS)
        xh = xv * rstd
        nwv, scale = nw_ref[...], 1.0 + sc_ref[...]
        dxh = dh * (nwv * scale)
        dx1 = dxv + rstd * (dxh - xh * jnp.mean(dxh * xh, axis=-1, keepdims=True))
        dxo_ref[...] = dx1
        dhx = dh * xh
        stf_ref[pl.ds(0, 1), :] = stf_ref[pl.ds(0, 1), :] + jnp.sum(dh, axis=0, keepdims=True)
        stf_ref[pl.ds(1, 1), :] = stf_ref[pl.ds(1, 1), :] + jnp.sum(dhx * nwv, axis=0, keepdims=True)
        stf_ref[pl.ds(2, 1), :] = stf_ref[pl.ds(2, 1), :] + jnp.sum(dhx * scale, axis=0, keepdims=True)
        stm_ref[pl.ds(0, 1), :] = stm_ref[pl.ds(0, 1), :] + jnp.sum(dx1 * y1_ref[...].astype(F32), axis=0, keepdims=True)
        dy1 = (dx1 * g1_ref[...]).astype(BF16)
        dy1_ref[...] = dy1
        da_ref[...] = (_nn(dy1, wm_ref[...]) if mix_is_transposed else _nt(dy1, wm_ref[...])).astype(BF16)

    row = lambda w_: pl.BlockSpec((tm, w_), lambda i: (i, 0))
    vec = _const_spec((1, D))
    st = jax.ShapeDtypeStruct((8, D), F32)
    act = lambda w_: jax.ShapeDtypeStruct((S, w_), BF16)
    return pl.pallas_call(
        body, name=name, grid=(S // tm,),
        out_shape=(act(2 * K), act(D), jax.ShapeDtypeStruct((S, D), F32), act(Km), act(D), st, st, st),
        in_specs=[row(D), row(D), vec, _const_spec(w_out.shape), row(2 * K), _const_spec(wt_in.shape), row(D), vec, vec,
                  row(D), vec, _const_spec(w_mix.shape)],
        out_specs=(row(2 * K), row(D), row(D), row(Km), row(D), _const_spec((8, D)), _const_spec((8, D)), _const_spec((8, D))),
        compiler_params=_params(("arbitrary",), VMEM_LIMIT),
    )(dx, y, g, w_out, gu, wt_in, x, nw, sc, y1, g1, w_mix)


def _weight_grad(a, b, *, transpose_out, name):
    S, N = b.shape
    nb = N // 2 if N > 4096 else N
    tk = 512

    def body(a_ref, b_ref, out_ref, acc):
        k = pl.program_id(1)

        @pl.when(k == 0)
        def _():
            acc[...] = jnp.zeros_like(acc)

        acc[...] += _tn(a_ref[...], b_ref[...])

        @pl.when(k == pl.num_programs(1) - 1)
        def _():
            out_ref[...] = (acc[...].T if transpose_out else acc[...]).astype(BF16)

    out_block = pl.BlockSpec((nb, D), lambda n, k: (n, 0)) if transpose_out else pl.BlockSpec((D, nb), lambda n, k: (0, n))
    return pl.pallas_call(
        body, name=name, grid=(N // nb, S // tk),
        out_shape=jax.ShapeDtypeStruct((N, D) if transpose_out else (D, N), BF16),
        in_specs=[pl.BlockSpec((tk, D), lambda n, k: (k, 0)), pl.BlockSpec((tk, nb), lambda n, k: (k, n))],
        out_specs=out_block, scratch_shapes=[pltpu.VMEM((D, nb), F32)],
        compiler_params=_params(("parallel", "arbitrary"), VMEM_LIMIT),
    )(a, b)


def _adamw(w, g, m, v):
    m = ADAM_B1 * m + (1.0 - ADAM_B1) * g
    v = ADAM_B2 * v + (1.0 - ADAM_B2) * (g * g)
    m_hat = m / (1.0 - ADAM_B1 ** ADAM_STEP)
    v_hat = v / (1.0 - ADAM_B2 ** ADAM_STEP)
    delta = -ADAM_LR * (m_hat / (jnp.sqrt(v_hat) + ADAM_EPS) + ADAM_WD * w)
    return delta, m, v


def _adam_shard(parts, own, w, m, v, name):
    R = w.shape[0]
    tr = max(t for t in (16, 32, 64, 128, 192, 256) if R % t == 0)

    def body(p_ref, o_ref, w_ref, m_ref, v_ref, g_out, d_out, m_out, v_out):
        me = _my_index()
        g = jnp.zeros((tr, D), F32)
        for j in range(N_DEV):
            g = g + jnp.where(me == j, o_ref[...], p_ref[j]).astype(F32)
        delta, mn, vn = _adamw(w_ref[...], g, m_ref[...], v_ref[...])
        g_out[...] = g
        d_out[...] = delta
        m_out[...] = mn
        v_out[...] = vn

    row = pl.BlockSpec((tr, D), lambda i: (i, 0))
    shp = jax.ShapeDtypeStruct((R, D), F32)
    return pl.pallas_call(
        body, name=name, grid=(R // tr,), out_shape=(shp,) * 4,
        in_specs=[pl.BlockSpec((N_DEV, tr, D), lambda i: (0, i, 0)), row, row, row, row], out_specs=(row,) * 4,
        compiler_params=_params(("parallel",), VMEM_LIMIT),
    )(parts, own, w, m, v)


def _adam_ada_w(cond_t, dmod, w, m, v):
    ncol = w.shape[-1]
    tr = 512

    def body(c_ref, d_ref, w_ref, m_ref, v_ref, g_out, d_out, m_out, v_out):
        g = _nn(c_ref[...], d_ref[0])
        delta, mn, vn = _adamw(w_ref[0], g, m_ref[0], v_ref[0])
        g_out[0] = g
        d_out[0] = delta
        m_out[0] = mn
        v_out[0] = vn

    blk = pl.BlockSpec((1, tr, ncol), lambda l, i: (l, i, 0))
    shp = jax.ShapeDtypeStruct(w.shape, F32)
    return pl.pallas_call(
        body, name="adam_ada_w", grid=(DEPTH, D // tr), out_shape=(shp,) * 4,
        in_specs=[pl.BlockSpec((tr, LANES), lambda l, i: (i, 0)), pl.BlockSpec((1, LANES, ncol), lambda l, i: (l, 0, 0)), blk, blk, blk],
        out_specs=(blk,) * 4, compiler_params=_params(("parallel", "parallel"), VMEM_LIMIT),
    )(cond_t, dmod, w, m, v)


TILE_ROWS = 168


def _stat_sources():
    pairs = []
    for i in range(DEPTH):
        b = 32 * i
        for q, src in enumerate((b, b + 1, b + 8, b + 16, b + 17, b + 24)):
            pairs.append((6 * i + q, src))
        pairs.append((24 + i, b + 2))
        pairs.append((32 + i, b + 18))
    pairs += [(40, 128), (41, 129)]
    return pairs


def _small_exchange(tiles, w, m, v):
    loss_row, sink_row, sink_src = 41, 48, 136

    def body(s_ref, w_ref, m_ref, v_ref, dmod_out, g_out, d_out, m_out, v_out, loss_out, all_ref, tot_ref, send_sems, recv_sems):
        me = _my_index()
        all_ref[me] = s_ref[...]
        copies = []
        for k in range(1, N_DEV):
            dev, _ = _peer(k)
            cp = pltpu.make_async_remote_copy(src_ref=s_ref, dst_ref=all_ref.at[me], send_sem=send_sems.at[k - 1],
                                              recv_sem=recv_sems.at[k - 1], device_id=dev, device_id_type=MESH)
            cp.start()
            copies.append(cp)
        for k in range(1, N_DEV):
            dev, pidx = _peer(k)
            pltpu.make_async_remote_copy(src_ref=s_ref, dst_ref=all_ref.at[pidx], send_sem=send_sems.at[k - 1],
                                         recv_sem=recv_sems.at[k - 1], device_id=dev, device_id_type=MESH).wait_recv()
        for cp in copies:
            cp.wait_send()
        tot = all_ref[0]
        for j in range(1, N_DEV):
            tot = tot + all_ref[j]
        tot_ref[...] = tot
        g_out[...] = jnp.zeros_like(g_out)
        for dst, src in _stat_sources():
            g_out[pl.ds(dst, 1), :] = tot_ref[pl.ds(src, 1), :]
            if dst < 24:
                for j in range(N_DEV):
                    dmod_out[j, pl.ds(dst, 1), :] = all_ref[j, pl.ds(src, 1), :]
        lane = lax.broadcasted_iota(jnp.int32, (1, D), 1)
        sink = jnp.zeros((1, D), F32)
        for h in range(32):
            sink = jnp.where(lane == h, tot_ref[pl.ds(sink_src + h, 1), :], sink)
        g_out[pl.ds(sink_row, 1), :] = sink
        g = g_out[...]
        delta, mn, vn = _adamw(w_ref[...], g, m_ref[...], v_ref[...])
        d_out[...] = delta
        m_out[...] = mn
        v_out[...] = vn
        loss = jnp.sum(g[loss_row:loss_row + 1, :], axis=-1, keepdims=True) * (0.5 / D)
        loss_out[...] = jnp.broadcast_to(loss, loss_out.shape)

    vm = pl.BlockSpec(memory_space=pltpu.VMEM)
    shp = jax.ShapeDtypeStruct((STAT_ROWS, D), F32)
    return pl.pallas_call(
        body, name="small_exchange",
        out_shape=(jax.ShapeDtypeStruct((N_DEV, 24, D), F32), shp, shp, shp, shp, jax.ShapeDtypeStruct((8, LANES), F32)),
        in_specs=[vm] * 4, out_specs=(vm,) * 6,
        scratch_shapes=[pltpu.VMEM((N_DEV, TILE_ROWS, D), F32), pltpu.VMEM((TILE_ROWS, D), F32),
                        pltpu.SemaphoreType.DMA((N_DEV - 1,)), pltpu.SemaphoreType.DMA((N_DEV - 1,))],
        compiler_params=_params(vmem=VMEM_LIMIT),
    )(tiles, w, m, v)


def _to_rows(name, a):
    if name in ("ffn_in", "a_in", "b_in"):
        return a.T
    if name == "b_out":
        return a.T.reshape(-1, D)
    return a


def _from_rows(name, a):
    if name in ("ffn_in", "a_in", "b_in"):
        return a.T
    if name == "b_out":
        return a.reshape(-1, 512).T
    return a


def _rows8(a):
    return jnp.pad(a, ((0, 8 - a.shape[0]), (0, 0)))


def _pack_small(ada_b, norm_mix, norm_ffn, final_norm, sink):
    sink_row = jnp.pad(sink.reshape(1, -1), ((0, 0), (0, D - sink.size)))
    return jnp.concatenate([ada_b.reshape(24, D), _rows8(norm_mix), _rows8(norm_ffn), _rows8(final_norm.reshape(1, D)),
                            _rows8(sink_row)], axis=0)


def _unpack_small(a):
    return a[0:24].reshape(4, 6 * D), a[24:28], a[32:36], a[40], a[48, :32].reshape(2, 16)


def kernel(x, c, ada_w, ada_b, norm_mix, norm_ffn, ffn_w_in, ffn_w_out, a_w_in, a_w_out, a_sink, b_w_in, b_w_out, final_norm, loss_target, m_ada_w, m_ada_b, m_norm_mix, m_norm_ffn, m_ffn_w_in, m_ffn_w_out, m_a_w_in, m_a_w_out, m_a_sink, m_b_w_in, m_b_w_out, m_final_norm, v_ada_w, v_ada_b, v_norm_mix, v_norm_ffn, v_ffn_w_in, v_ffn_w_out, v_a_w_in, v_a_w_out, v_a_sink, v_b_w_in, v_b_w_out, v_final_norm):
    S = x.shape[1]
    x0 = x.reshape(S, D)
    target = loss_target.reshape(S, D)
    me = _my_index()
    ncol = ada_w.shape[-1]

    ada_b_mine = lax.dynamic_slice_in_dim(ada_b, me * ncol, ncol, axis=1)
    cond_all, parts = _cond_exchange(jnp.broadcast_to(c.reshape(1, D), (8, D)), ada_w, ada_b_mine)
    mod = lax.dynamic_index_in_dim(parts, me, axis=2, keepdims=False)
    mod = jnp.transpose(mod, (1, 0, 2)).reshape(DEPTH, 6, 1, D)

    weights = {"ffn_in": ffn_w_in, "ffn_out": ffn_w_out, "a_in": a_w_in, "a_out": a_w_out, "b_in": b_w_in, "b_out": b_w_out}
    shard = {(n, l): _to_rows(n, weights[n][l]).astype(BF16) for n, l, _ in SEGMENTS}
    first = [sg for sg in _layer_segments(0) if not sg[0].startswith("ffn")]
    gathered0 = _all_gather_weights([shard[(n, l)] for n, l, _ in first])
    W = {(n, l): g for (n, l, _), g in zip(first, gathered0)}
    groups = [[sg for sg in _layer_segments(0) if sg[0].startswith("ffn")]] + [_layer_segments(i) for i in range(1, DEPTH)]
    gathers, order = [], gathered0[0]
    for q, segs in enumerate(groups):
        mine = jnp.concatenate([shard[(n, l)] for n, l, _ in segs], axis=0)
        zone = lax.empty((N_DEV, mine.shape[0], D), BF16)
        gathers.append(_exchange_start([mine], zone, [mine.shape[0]], [0], False, order, "weight_gather_start_%d" % q))
        order = gathers[-1][-1]
    gather_token = order[0:1, 0:1]

    def finish_gather(q, after):
        zone = _exchange_wait(gathers[q], after, "weight_gather_wait_%d" % q)
        offs, _ = _offsets(groups[q])
        for (n, l, rows), off in zip(groups[q], offs):
            full = lax.dynamic_update_slice(zone[:, off:off + rows], shard[(n, l)][None], (me, 0, 0))
            W[(n, l)] = full.reshape(D, 512) if n == "b_out" else full.reshape(N_DEV * rows, D)

    a_slopes, b_slopes = _slopes(16), _slopes(24)
    bias_a = _alibi_bias(a_slopes, A_HALF, 1)
    bias_b = [_alibi_bias(b_slopes[8 * g:8 * g + 8], B_HALF, dil) for g, dil in enumerate(B_DILS)]
    bias_b_fwd = [_alibi_bias(b_slopes[8 * g:8 * g + 8], B_HALF, dil, max(CHUNK, TQ * dil), both=True) for g, dil in enumerate(B_DILS)]
    a_geom = dict(C=A_QKV, r=1, half=A_HALF, qoff=0, koff=1024, voff=1280, n_units=2)
    b_geom = [dict(C=B_QKV, r=dil, half=B_HALF, qoff=512 * g, koff=1536 + 128 * g, voff=1920 + 128 * g, n_units=1)
              for g, dil in enumerate(B_DILS)]

    saved = []
    xcur = x0
    for i in range(DEPTH):
        j = i // 2
        sh1, sc1, g1, sh2, sc2, g2 = [mod[i, q] for q in range(6)]
        nm, nf = norm_mix[i].reshape(1, D), norm_ffn[i].reshape(1, D)
        if i == 0:
            nm = nm + gather_token
        if i > 0:
            finish_gather(i, xcur)
        if i % 2 == 0:
            sink_rep = jnp.repeat(jnp.repeat(a_sink[j], TQ).reshape(2, 1, 8 * TQ), 8, axis=1).reshape(16, 8 * TQ)
            h1, qkv = _proj(xcur, nm, sc1, sh1, W[("a_in", j)], name="proj_a")
            o, lse = _attn_fwd(qkv, bias_a, sink_rep, out_dtype=BF16, name="attn_a_fwd", **a_geom)
            if i == 0:
                finish_gather(0, o)
            x1, y1, h2, gu, act = _out_ffn_in(o, W[("a_out", j)], xcur, g1, nf, sc2, sh2, W[("ffn_in", i)],
                                              w_is_transposed=False, name="out_a_ffn_in")
        else:
            sink_rep = None
            h1, qkv = _proj(xcur, nm, sc1, sh1, W[("b_in", j)], name="proj_b")
            outs = [_attn_fwd(qkv, bias_b_fwd[g], None, out_dtype=F32, name="attn_b%d_fwd" % g, **b_geom[g]) for g in range(3)]
            o, lse = _merge_groups([t[0] for t in outs], [t[1] for t in outs])
            x1, y1, h2, gu, act = _out_ffn_in(o, W[("b_out", j)], xcur, g1, nf, sc2, sh2, W[("ffn_in", i)],
                                              w_is_transposed=True, name="out_b_ffn_in")
        if i < DEPTH - 1:
            x2, y2 = _ffn_out(act, W[("ffn_out", i)], x1, g2, None, None, name="ffn_out")
        else:
            x2, y2, head_stats = _ffn_out(act, W[("ffn_out", i)], x1, g2, target, final_norm.reshape(1, D), name="ffn_out_loss")
        saved.append(dict(x0=xcur, h1=h1, qkv=qkv, o=o, lse=lse, y1=y1, x1=x1, h2=h2, gu=gu, act=act, y2=y2, sink=sink_rep))
        xcur = x2

    dx = xcur

    dW = {}
    stat_tiles, dsink = [None] * DEPTH, [None] * 2
    exchanges = []
    start_token = None

    def start_exchange(segs):
        offs, total = _offsets(segs)
        own = jnp.concatenate([lax.dynamic_slice_in_dim(dW[(n, l)], me * rows, rows, axis=0) for n, l, rows in segs], axis=0)
        started = _exchange_start([dW[(n, l)] for n, l, _ in segs], lax.empty((N_DEV, total, D), BF16), [sg[2] for sg in segs],
                                  offs, True, own, "grad_exchange_start_%d" % len(exchanges))
        exchanges.append((segs, started, own))
        return started[-1][0:1, 0:1]

    for i in reversed(range(DEPTH)):
        j = i // 2
        sv = saved[i]
        sh1, sc1, g1, sh2, sc2, g2 = [mod[i, q] for q in range(6)]
        if start_token is not None:
            g2 = g2 + start_token
            start_token = None
        nm, nf = norm_mix[i].reshape(1, D), norm_ffn[i].reshape(1, D)
        mix = "a_out" if i % 2 == 0 else "b_out"
        dgu, dy2, dx1, do, dy1, st_g2, st_f, st_g1 = _ffn_bwd(
            dx, sv["y2"], g2, W[("ffn_out", i)], sv["gu"], W[("ffn_in", i)], sv["x1"], nf, sc2, sv["y1"], g1, W[(mix, j)],
            mix_is_transposed=(i % 2 == 1), name="ffn_bwd_" + mix)
        dW[("ffn_out", i)] = _weight_grad(dy2, sv["act"], transpose_out=True, name="dw_ffn_out")
        dW[("ffn_in", i)] = _weight_grad(sv["h2"], dgu, transpose_out=True, name="dw_ffn_in")
        sink_bwd = sv["sink"]
        if i == 0:
            sink_bwd = sink_bwd + start_exchange([sg for sg in _layer_segments(0) if sg[0].startswith("ffn")])
        if i % 2 == 0:
            dW[("a_out", j)] = _weight_grad(dy1, sv["o"], transpose_out=True, name="dw_a_out")
            dq, dk, dv, ds = _attn_bwd(sv["qkv"], bias_a, sink_bwd, sv["o"], do, sv["lse"], name="attn_a_bwd", **a_geom)
            dsink[j] = ds
            dqkv = jnp.concatenate([dq, dk[0].astype(BF16), dk[1].astype(BF16), dv[0].astype(BF16), dv[1].astype(BF16)], axis=1)
            dW[("a_in", j)] = _weight_grad(sv["h1"], dqkv, transpose_out=True, name="dw_a_in")
            dx0, st_m = _norm_bwd(dqkv, W[("a_in", j)], sv["x0"], dx1, nm, sc1, name="proj_a_bwd")
        else:
            dW[("b_out", j)] = _weight_grad(dy1, sv["o"], transpose_out=False, name="dw_b_out").reshape(N_DEV * 64, D)
            gr = [_attn_bwd(sv["qkv"], bias_b[g], None, sv["o"], do, sv["lse"], name="attn_b%d_bwd" % g, **b_geom[g]) for g in range(3)]
            dqkv = jnp.concatenate([t[0] for t in gr] + [t[1][0].astype(BF16) for t in gr] + [t[2][0].astype(BF16) for t in gr], axis=1)
            dW[("b_in", j)] = _weight_grad(sv["h1"], dqkv, transpose_out=True, name="dw_b_in")
            dx0, st_m = _norm_bwd(dqkv, W[("b_in", j)], sv["x0"], dx1, nm, sc1, name="proj_b_bwd")
        stat_tiles[i] = [st_m, st_g1, st_f, st_g2]
        if i > 0:
            start_token = start_exchange(_layer_segments(i))
        else:
            start_exchange([sg for sg in _layer_segments(0) if not sg[0].startswith("ffn")])
        dx = dx0
    grad_x = dx.reshape(1, S, D)

    masters = {"ffn_in": (ffn_w_in, m_ffn_w_in, v_ffn_w_in), "ffn_out": (ffn_w_out, m_ffn_w_out, v_ffn_w_out),
               "a_in": (a_w_in, m_a_w_in, v_a_w_in), "a_out": (a_w_out, m_a_w_out, v_a_w_out),
               "b_in": (b_w_in, m_b_w_in, v_b_w_in), "b_out": (b_w_out, m_b_w_out, v_b_w_out)}
    pieces = {}
    after = dx
    for segs, started, own in exchanges:
        offs, total = _offsets(segs)
        parts_g = _exchange_wait(started, after, "grad_exchange_wait_%d" % len(pieces))
        rows_wmv = [jnp.concatenate([_to_rows(n, masters[n][q][l]) for n, l, _ in segs], axis=0) for q in range(3)]
        res_rows = _adam_shard(parts_g, own, *rows_wmv, name="adam_%d" % total)
        after = res_rows[0]
        for q, kind in enumerate(("grad", "delta", "m", "v")):
            for (n, l, rows), off in zip(segs, offs):
                pieces[(kind, n, l)] = _from_rows(n, res_rows[q][off:off + rows])
    big = {(kind, n): jnp.stack([pieces[(kind, n, l)] for l in range(4 if n.startswith("ffn") else 2)])
           for kind in ("grad", "delta", "m", "v") for n in masters}

    tiles = jnp.concatenate([t for i in range(DEPTH) for t in stat_tiles[i]] + [head_stats]
                            + [jnp.pad(ds, ((0, 0), (0, D - LANES))) for ds in dsink], axis=0)
    small = [_pack_small(*t) for t in ((ada_b, norm_mix, norm_ffn, final_norm, a_sink),
                                       (m_ada_b, m_norm_mix, m_norm_ffn, m_final_norm, m_a_sink),
                                       (v_ada_b, v_norm_mix, v_norm_ffn, v_final_norm, v_a_sink))]
    dmod_all, sg, sd, sm, sv_, loss_tile = _small_exchange(tiles, *small)
    loss = loss_tile[0, 0]
    dmod_all = dmod_all.reshape(N_DEV, DEPTH, 6 * D)
    dmod_mine = lax.dynamic_slice_in_dim(dmod_all, me * ncol, ncol, axis=2)
    dmod_pad = jnp.pad(jnp.transpose(dmod_mine, (1, 0, 2)), ((0, 0), (0, LANES - N_DEV), (0, 0))).astype(BF16)
    cond_t = jnp.pad(cond_all.T, ((0, 0), (0, LANES - N_DEV))).astype(BF16)
    ada = _adam_ada_w(cond_t, dmod_pad, ada_w, m_ada_w, v_ada_w)

    outs = [loss, grad_x]
    small_res = [_unpack_small(t) for t in (sg, sd, sm, sv_)]
    for q, kind in enumerate(("grad", "delta", "m", "v")):
        ab, nm_, nf_, fn, sk = small_res[q]
        outs += [ada[q], ab, nm_, nf_, big[(kind, "ffn_in")], big[(kind, "ffn_out")], big[(kind, "a_in")], big[(kind, "a_out")],
                 sk, big[(kind, "b_in")], big[(kind, "b_out")], fn]
    return tuple(outs)
```

```python
import functools
import math

import numpy as np
import jax
import jax.numpy as jnp
from jax import lax
from jax.experimental import pallas as pl
from jax.experimental.pallas import tpu as pltpu

D = 1024
HEAD_DIM = 64
D_FF = 2816
DEPTH = 4
N_DEV = 8
A_QKV = 1536
B_QKV = 2304
A_HALF = 128
B_HALF = 64
B_DILS = (1, 4, 16)
RMS_EPS = 1e-6
NEG = -1e30
ADAM_LR = 0.001
ADAM_B1 = 0.9
ADAM_B2 = 0.999
ADAM_EPS = 1e-08
ADAM_WD = 0.01
ADAM_STEP = 10

LANES = 128
SPLIT = 2
TQ = 128
VMEM_LIMIT = 56 * 1024 * 1024
MESH = pl.DeviceIdType.MESH
F32 = jnp.float32
BF16 = jnp.bfloat16

SEGMENTS = ([("ffn_in", l, 704) for l in range(4)] + [("ffn_out", l, 352) for l in range(4)]
            + [("a_in", j, 192) for j in range(2)] + [("a_out", j, 128) for j in range(2)]
            + [("b_in", j, 288) for j in range(2)] + [("b_out", j, 64) for j in range(2)])
def _layer_segments(i):
    mixer = "a" if i % 2 == 0 else "b"
    return [s for s in SEGMENTS if (s[0].startswith("ffn") and s[1] == i) or (s[0].startswith(mixer + "_") and s[1] == i // 2)]


def _offsets(segs):
    rows = [s[2] for s in segs]
    return [sum(rows[:k]) for k in range(len(rows))], sum(rows)
STAT_ROWS = 56


def _nn(a, b):
    return jnp.dot(a, b, preferred_element_type=F32)


def _nt(a, b):
    return lax.dot_general(a, b, (((1,), (1,)), ((), ())), preferred_element_type=F32)


def _tn(a, b):
    return lax.dot_general(a, b, (((0,), (0,)), ((), ())), preferred_element_type=F32)


def _params(dims=None, vmem=None):
    kw = {}
    if dims is not None:
        kw["dimension_semantics"] = dims
    if vmem is not None:
        kw["vmem_limit_bytes"] = vmem
    return pltpu.CompilerParams(**kw)


def _my_index():
    return 4 * lax.axis_index("x") + 2 * lax.axis_index("y") + lax.axis_index("c")


def _peer(k):
    x, y, c = lax.axis_index("x"), lax.axis_index("y"), lax.axis_index("c")
    px, py, pc = x ^ ((k >> 2) & 1), y ^ ((k >> 1) & 1), c ^ (k & 1)
    return (px, py, pc), 4 * px + 2 * py + pc


def _const_spec(shape):
    nd = len(shape)
    return pl.BlockSpec(shape, lambda *_: (0,) * nd)


def _cond_exchange(c_tile, ada_w, ada_b_mine):
    ncol = ada_w.shape[-1]

    def body(c_ref, w_ref, b_ref, cond_ref, parts_ref, call_ref, mine_ref, send_sems, recv_sems):
        me = _my_index()
        call_ref[me] = c_ref[...]
        copies = []
        for k in range(1, N_DEV):
            dev, _ = _peer(k)
            cp = pltpu.make_async_remote_copy(src_ref=c_ref, dst_ref=call_ref.at[me], send_sem=send_sems.at[0, k - 1],
                                              recv_sem=recv_sems.at[0, k - 1], device_id=dev, device_id_type=MESH)
            cp.start()
            copies.append(cp)
        for k in range(1, N_DEV):
            _, pidx = _peer(k)
            pltpu.make_async_remote_copy(src_ref=c_ref, dst_ref=call_ref.at[pidx], send_sem=send_sems.at[0, k - 1],
                                         recv_sem=recv_sems.at[0, k - 1], device_id=_peer(k)[0], device_id_type=MESH).wait_recv()
        for cp in copies:
            cp.wait_send()
        row = lax.broadcasted_iota(jnp.int32, (N_DEV, D), 0)
        cmat = jnp.zeros((N_DEV, D), F32)
        for j in range(N_DEV):
            cmat = jnp.where(row == j, call_ref[j], cmat)
        cond = cmat * jax.nn.sigmoid(cmat)
        cond_ref[...] = cond
        cb = cond.astype(BF16)
        for l in range(DEPTH):
            mine_ref[l] = _nn(cb, w_ref[l].astype(BF16)) + b_ref[pl.ds(l, 1), :]
        parts_ref[me] = mine_ref[...]
        copies = []
        for k in range(1, N_DEV):
            dev, _ = _peer(k)
            cp = pltpu.make_async_remote_copy(src_ref=mine_ref, dst_ref=parts_ref.at[me], send_sem=send_sems.at[1, k - 1],
                                              recv_sem=recv_sems.at[1, k - 1], device_id=dev, device_id_type=MESH)
            cp.start()
            copies.append(cp)
        for k in range(1, N_DEV):
            dev, pidx = _peer(k)
            pltpu.make_async_remote_copy(src_ref=mine_ref, dst_ref=parts_ref.at[pidx], send_sem=send_sems.at[1, k - 1],
                                         recv_sem=recv_sems.at[1, k - 1], device_id=dev, device_id_type=MESH).wait_recv()
        for cp in copies:
            cp.wait_send()

    vm = pl.BlockSpec(memory_space=pltpu.VMEM)
    return pl.pallas_call(
        body, name="cond_exchange",
        out_shape=(jax.ShapeDtypeStruct((N_DEV, D), F32), jax.ShapeDtypeStruct((N_DEV, DEPTH, N_DEV, ncol), F32)),
        in_specs=[vm, vm, vm], out_specs=(vm, vm),
        scratch_shapes=[pltpu.VMEM((N_DEV, N_DEV, D), F32), pltpu.VMEM((DEPTH, N_DEV, ncol), F32),
                        pltpu.SemaphoreType.DMA((2, N_DEV - 1)), pltpu.SemaphoreType.DMA((2, N_DEV - 1))],
        compiler_params=_params(vmem=VMEM_LIMIT),
    )(c_tile, ada_w, ada_b_mine)[:2]


def _all_gather_weights(shards):
    n = len(shards)
    big = max(range(n), key=lambda s: shards[s].shape[0])
    total = sum(sh.shape[0] for sh in shards)
    assert N_DEV * shards[big].shape[0] >= total

    def body(*refs):
        ins, outs = refs[:n], refs[n:2 * n]
        local_sems, send_sems, recv_sems = refs[2 * n:]
        me = _my_index()
        local = []
        for s in range(n):
            rows = ins[s].shape[0]
            cp = pltpu.make_async_copy(ins[s], outs[s].at[pl.ds(me * rows, rows)], local_sems.at[s])
            cp.start()
            local.append(cp)
        for k in range(1, N_DEV):
            dev, _ = _peer(k)
            for s in range(n):
                rows = ins[s].shape[0]
                pltpu.make_async_remote_copy(src_ref=ins[s], dst_ref=outs[s].at[pl.ds(me * rows, rows)],
                                             send_sem=send_sems.at[k - 1], recv_sem=recv_sems.at[k - 1],
                                             device_id=dev, device_id_type=MESH).start()
        whole = outs[big].at[pl.ds(0, total)]
        for k in range(1, N_DEV):
            dev, _ = _peer(k)
            w = pltpu.make_async_remote_copy(src_ref=whole, dst_ref=whole, send_sem=send_sems.at[k - 1],
                                             recv_sem=recv_sems.at[k - 1], device_id=dev, device_id_type=MESH)
            w.wait_send()
            w.wait_recv()
        for cp in local:
            cp.wait()

    hbm = pl.BlockSpec(memory_space=pl.ANY)
    return pl.pallas_call(
        body, name="weight_all_gather",
        out_shape=tuple(jax.ShapeDtypeStruct((N_DEV * s.shape[0], D), s.dtype) for s in shards),
        in_specs=[hbm] * n, out_specs=tuple([hbm] * n),
        scratch_shapes=[pltpu.SemaphoreType.DMA((n,)), pltpu.SemaphoreType.DMA((N_DEV - 1,)),
                        pltpu.SemaphoreType.DMA((N_DEV - 1,))],
    )(*shards)


HBM = pl.BlockSpec(memory_space=pltpu.HBM)
SEM = pl.BlockSpec(memory_space=pltpu.SEMAPHORE)
EFFECT = pltpu.SideEffectType.DATAFLOW_SIDE_EFFECTING


def _exchange_start(srcs, landings, dst, rows, to_peer_rows, after, name):
    n, nl = len(srcs), len(landings)

    def body(*refs):
        src_refs, land_refs = refs[:n], refs[n:n + nl]
        send_sems, recv_sems = refs[n + nl + 1], refs[n + nl + 2]
        token = refs[-1]
        me = _my_index()
        for k in range(1, N_DEV):
            dev, pidx = _peer(k)
            for q in range(n):
                src = src_refs[q].at[pl.ds(pidx * rows[q], rows[q])] if to_peer_rows else src_refs[q]
                pltpu.make_async_remote_copy(src_ref=src, dst_ref=land_refs[dst[q][0]].at[me, pl.ds(dst[q][1], rows[q])],
                                             send_sem=send_sems.at[(k - 1) * n + q], recv_sem=recv_sems.at[(k - 1) * n + q],
                                             device_id=dev, device_id_type=MESH).start()
        token[...] = jnp.zeros_like(token)

    arrays = list(srcs) + list(landings)
    sems = pltpu.SemaphoreType.DMA(((N_DEV - 1) * n,))
    return pl.pallas_call(
        body, name=name,
        out_shape=(sems, sems, *[pltpu.HBM(a.shape, a.dtype) for a in arrays], jax.ShapeDtypeStruct((8, LANES), F32)),
        in_specs=[HBM] * (n + nl) + [pl.BlockSpec(memory_space=pl.ANY)],
        out_specs=(SEM, SEM, *[HBM] * (n + nl), pl.BlockSpec(memory_space=pltpu.VMEM)),
        input_output_aliases={q: 2 + q for q in range(n + nl)},
        compiler_params=pltpu.CompilerParams(has_side_effects=EFFECT),
    )(*[pltpu.with_memory_space_constraint(a, pltpu.HBM) for a in arrays], after)


def _exchange_wait(started, n, dst, rows, after, name):
    send_sems, recv_sems = started[0], started[1]
    arrays = list(started[2:-1])
    n1 = len(arrays)

    def body(*refs):
        land_refs = refs[n:n1]
        sends, recvs = refs[n1], refs[n1 + 1]
        for k in range(1, N_DEV):
            dev, _ = _peer(k)
            for q in range(n):
                slot = land_refs[dst[q][0]].at[0, pl.ds(dst[q][1], rows[q])]
                w = pltpu.make_async_remote_copy(src_ref=slot, dst_ref=slot, send_sem=sends.at[(k - 1) * n + q],
                                                 recv_sem=recvs.at[(k - 1) * n + q], device_id=dev, device_id_type=MESH)
                w.wait_send()
                w.wait_recv()

    return pl.pallas_call(
        body, name=name, out_shape=tuple(pltpu.HBM(a.shape, a.dtype) for a in arrays),
        in_specs=[HBM] * n1 + [SEM, SEM, pl.BlockSpec(memory_space=pl.ANY)], out_specs=tuple([HBM] * n1),
        input_output_aliases={q: q for q in range(n1)},
        compiler_params=pltpu.CompilerParams(has_side_effects=EFFECT),
    )(*arrays, send_sems, recv_sems, after)[n:]


def _norm_mod(x, nw, sc, sh):
    ms = jnp.mean(x * x, axis=-1, keepdims=True)
    xh = x * lax.rsqrt(ms + RMS_EPS)
    return xh, (xh * nw) * (1.0 + sc) + sh


def _proj(x, nw, sc, sh, wt, *, name):
    S, N = x.shape[0], wt.shape[0]
    tm = 512

    def body(x_ref, nw_ref, sc_ref, sh_ref, w_ref, h_ref, out_ref):
        for half in range(SPLIT):
            rows = pl.ds(half * (tm // SPLIT), tm // SPLIT)
            _, h = _norm_mod(x_ref[rows, :], nw_ref[...], sc_ref[...], sh_ref[...])
            hb = h.astype(BF16)
            h_ref[rows, :] = hb
            out_ref[rows, :] = _nt(hb, w_ref[...]).astype(BF16)

    row = lambda w: pl.BlockSpec((tm, w), lambda i: (i, 0))
    vec = _const_spec((1, D))
    return pl.pallas_call(
        body, name=name, grid=(S // tm,), out_shape=(jax.ShapeDtypeStruct((S, D), BF16), jax.ShapeDtypeStruct((S, N), BF16)),
        in_specs=[row(D), vec, vec, vec, _const_spec((N, D))], out_specs=(row(D), row(N)),
        compiler_params=_params(("parallel",), VMEM_LIMIT),
    )(x, nw, sc, sh, wt)


def _ffn_out(a, w, x, g, target, fnw, *, name):
    S, K = a.shape
    tm = 512
    last = target is not None

    def body(a_ref, w_ref, x_ref, g_ref, *rest):
        y = _nn(a_ref[...], w_ref[...])
        xv = x_ref[...] + g_ref[...] * y
        if not last:
            xo_ref, y_ref = rest
            y_ref[...] = y.astype(BF16)
            xo_ref[...] = xv
            return
        t_ref, fw_ref, dx_ref, y_ref, st_ref = rest
        y_ref[...] = y.astype(BF16)

        @pl.when(pl.program_id(0) == 0)
        def _():
            st_ref[...] = jnp.zeros_like(st_ref)

        rstd = lax.rsqrt(jnp.mean(xv * xv, axis=-1, keepdims=True) + RMS_EPS)
        xh = xv * rstd
        err = xh * fw_ref[...] - t_ref[...]
        dy = err * (1.0 / D)
        dxh = dy * fw_ref[...]
        dx_ref[...] = rstd * (dxh - xh * jnp.mean(dxh * xh, axis=-1, keepdims=True))
        st_ref[pl.ds(0, 1), :] = st_ref[pl.ds(0, 1), :] + jnp.sum(dy * xh, axis=0, keepdims=True)
        st_ref[pl.ds(1, 1), :] = st_ref[pl.ds(1, 1), :] + jnp.sum(err * err, axis=0, keepdims=True)

    row = lambda w_: pl.BlockSpec((tm, w_), lambda i: (i, 0))
    in_specs = [row(K), _const_spec(w.shape), row(D), _const_spec((1, D))]
    args = [a, w, x, g]
    out_shape = [jax.ShapeDtypeStruct((S, D), F32), jax.ShapeDtypeStruct((S, D), BF16)]
    out_specs = [row(D), row(D)]
    if last:
        in_specs += [row(D), _const_spec((1, D))]
        args += [target, fnw]
        out_shape.append(jax.ShapeDtypeStruct((8, D), F32))
        out_specs.append(_const_spec((8, D)))
    return pl.pallas_call(
        body, name=name, grid=(S // tm,), out_shape=tuple(out_shape), in_specs=in_specs, out_specs=tuple(out_specs),
        compiler_params=_params(("arbitrary",) if last else ("parallel",), VMEM_LIMIT),
    )(*args)


CHUNK = 1024


def _tile_rows(r, chunk=CHUNK):
    return min(TQ, chunk // r)


def _out_ffn_in(a, w_mix, x, g, nw, sc, sh, wt, *, w_is_transposed, name):
    S, K = a.shape
    tm = 256

    def body(a_ref, wm_ref, x_ref, g_ref, nw_ref, sc_ref, sh_ref, w_ref, x1_ref, y_ref, h_ref, gu_ref, act_ref):
        y = _nt(a_ref[...], wm_ref[...]) if w_is_transposed else _nn(a_ref[...], wm_ref[...])
        y_ref[...] = y.astype(BF16)
        x1 = x_ref[...] + g_ref[...] * y
        x1_ref[...] = x1
        _, h = _norm_mod(x1, nw_ref[...], sc_ref[...], sh_ref[...])
        hb = h.astype(BF16)
        h_ref[...] = hb
        gate = _nt(hb, w_ref[pl.ds(0, D_FF), :])
        up = _nt(hb, w_ref[pl.ds(D_FF, D_FF), :])
        sig = jax.nn.sigmoid(gate)
        silu = gate * sig
        gu_ref[:, pl.ds(0, D_FF)] = (up * (sig * (1.0 + gate * (1.0 - sig)))).astype(BF16)
        gu_ref[:, pl.ds(D_FF, D_FF)] = silu.astype(BF16)
        act_ref[...] = (silu * up).astype(BF16)

    row = lambda w_: pl.BlockSpec((tm, w_), lambda i: (i, 0))
    vec = _const_spec((1, D))
    return pl.pallas_call(
        body, name=name, grid=(S // tm,),
        out_shape=(jax.ShapeDtypeStruct((S, D), F32), jax.ShapeDtypeStruct((S, D), BF16), jax.ShapeDtypeStruct((S, D), BF16),
                   jax.ShapeDtypeStruct((S, 2 * D_FF), BF16), jax.ShapeDtypeStruct((S, D_FF), BF16)),
        in_specs=[row(K), _const_spec(w_mix.shape), row(D), vec, vec, vec, vec, _const_spec(wt.shape)],
        out_specs=(row(D), row(D), row(D), row(2 * D_FF), row(D_FF)),
        compiler_params=_params(("parallel",), VMEM_LIMIT),
    )(a, w_mix, x, g, nw, sc, sh, wt)


def _alibi_bias(slopes, half, dil, chunk=CHUNK, both=False):
    tq = _tile_rows(dil, chunk)
    tk = tq + 2 * half
    rel = np.arange(tk)[:, None] - half - np.arange(tq)[None, :]
    band = np.abs(rel) <= half
    dist = (dil * np.abs(rel)).astype(np.float32)
    tabs = [np.where(band, -np.float32(s) * dist, np.float32(NEG)).astype(np.float32) for s in slopes]
    out = []
    for u in range(0, len(tabs), 8):
        tab = np.concatenate(tabs[u:u + 8], axis=1)
        first, last = tab.copy(), tab.copy()
        first[:half] = NEG
        last[tk - half:] = NEG
        out += [tab, first, last]
        if both:
            last = last.copy()
            last[:half] = NEG
            out.append(last)
    return jnp.asarray(np.concatenate(out, axis=0))


def _slopes(n):
    return (2.0 ** (-8.0 * np.arange(1, n + 1) / n)).astype(np.float32)


def _head_masks(tq):
    lane = lax.broadcasted_iota(jnp.int32, (tq, LANES), 1)
    lo = lane < HEAD_DIM
    return lo, jnp.logical_not(lo)


def _stack_heads(tiles, lo, hi, scale):
    blocks = []
    for t in range(4):
        xf = tiles[t] if scale == 1.0 else tiles[t] * scale
        for a in range(2):
            xm = jnp.where(lo if a == 0 else hi, xf, 0.0)
            if a != t // 2:
                xm = pltpu.roll(xm, HEAD_DIM, 1)
            blocks.append(xm.astype(BF16))
    return jnp.concatenate(blocks, axis=0)


def _tile_from_columns(x8t, t, tq):
    r0 = HEAD_DIM * (t // 2)
    top = x8t[r0:r0 + HEAD_DIM, 2 * t * tq:(2 * t + 1) * tq]
    bot = x8t[r0:r0 + HEAD_DIM, (2 * t + 1) * tq:(2 * t + 2) * tq]
    return jnp.concatenate([top, bot], axis=0).T


def _attn_layout(S, C, r, half, qoff, koff, voff, chunk):
    hb = half * r
    per = chunk // hb
    nhb = S // hb
    main = lambda off: pl.BlockSpec((chunk, LANES), lambda u, i: (i, off // LANES + u))
    prev = lambda off: pl.BlockSpec((hb, LANES), lambda u, i: (jnp.maximum(i * per - 1, 0), off // LANES + u))
    nxt = lambda off: pl.BlockSpec((hb, LANES), lambda u, i: (jnp.minimum((i + 1) * per, nhb - 1), off // LANES + u))
    specs = [pl.BlockSpec((chunk, 4 * LANES), lambda u, i: (i, qoff // (4 * LANES) + u))]
    specs += [prev(koff), main(koff), nxt(koff), prev(voff), main(voff), nxt(voff)]
    return specs, hb


def _stage(dst, srcs):
    row = 0
    for src in srcs:
        n = src.shape[0]
        dst[pl.ds(row, n), :] = src[...].astype(F32)
        row += n


def _rows(start, n, r):
    return pl.ds(start, n, stride=r) if r > 1 else pl.ds(start, n)


def _attn_fwd(qkv, bias, sink, *, C, r, half, qoff, koff, voff, n_units, out_dtype, name):
    S = qkv.shape[0]
    chunk = max(CHUNK, TQ * r)
    tq = _tile_rows(r, chunk)
    tk = tq + 2 * half
    tiles = chunk // (r * tq)
    nsteps = S // chunk
    specs, hb = _attn_layout(S, C, r, half, qoff, koff, voff, chunk)
    use_sink = sink is not None

    def body(*refs):
        q_ref, kp, km, kn, vp, vm, vn, bias_ref = refs[:8]
        rest = list(refs[8:])
        sink_ref = rest.pop(0) if use_sink else None
        o_ref, lse_ref, qs, ks, vs, os_, ls = rest
        i = pl.program_id(1)
        if r > 1:
            for t in range(4):
                qs[t] = q_ref[:, pl.ds(t * LANES, LANES)].astype(F32)
        _stage(ks, [kp, km, kn])
        _stage(vs, [vp, vm, vn])
        lo, hi = _head_masks(tq)

        def tile_in(staged, ref, t, start):
            if r > 1:
                return staged[t, _rows(start, tq, r), :]
            return ref[pl.ds(start, tq), pl.ds(t * LANES, LANES)].astype(F32)

        ones = jnp.ones((16, tk), BF16)
        if use_sink:
            sk = sink_ref[pl.ds(0, 1), :]

        def chain(n, carry):
            rho, c = n // tiles, n % tiles
            start = c * (tq * r) + rho
            if r == 1:
                start = pl.multiple_of(start, tq)
            variant = jnp.where(jnp.logical_and(i == 0, c == 0), 1, 0) + jnp.where(
                jnp.logical_and(i == nsteps - 1, c == tiles - 1), 2, 0)
            k2 = ks[_rows(start, tk, r), :].astype(BF16)
            v2t = jnp.concatenate([vs[_rows(start, tk, r), :].T.astype(BF16), ones], axis=0)
            q8 = _stack_heads([tile_in(qs, q_ref, t, start) for t in range(4)], lo, hi, HEAD_DIM ** -0.5)
            s = _nt(k2, q8) + bias_ref[pl.ds(pl.multiple_of(variant * tk, 8), tk), :]
            m = jnp.max(s, axis=0, keepdims=True)
            if use_sink:
                m = jnp.maximum(m, sk)
            pv = _nn(v2t, jnp.exp(s - m).astype(BF16))
            l = pv[LANES:LANES + 1]
            if use_sink:
                l = l + jnp.exp(sk - m)
            o8t = pv[:LANES] / l
            lse8 = jnp.broadcast_to(m + jnp.log(l), (LANES, 8 * tq))
            for t in range(4):
                if r > 1:
                    os_[t, _rows(start, tq, r), :] = _tile_from_columns(o8t, t, tq)
                    ls[t, _rows(start, tq, r), :] = _tile_from_columns(lse8, t, tq)
                else:
                    o_ref[pl.ds(start, tq), pl.ds(t * LANES, LANES)] = _tile_from_columns(o8t, t, tq).astype(out_dtype)
                    lse_ref[pl.ds(start, tq), pl.ds(t * LANES, LANES)] = _tile_from_columns(lse8, t, tq)
            return carry

        lax.fori_loop(0, r * tiles, chain, 0, unroll=4)
        if r > 1:
            for t in range(4):
                o_ref[:, pl.ds(t * LANES, LANES)] = os_[t].astype(out_dtype)
                lse_ref[:, pl.ds(t * LANES, LANES)] = ls[t]

    in_specs = specs + [pl.BlockSpec((bias.shape[0] // n_units, 8 * tq), lambda u, i: (u, 0))]
    args = [qkv] * 7 + [bias]
    if use_sink:
        in_specs.append(pl.BlockSpec((8, 8 * tq), lambda u, i: (u, 0)))
        args.append(sink)
    wide = pl.BlockSpec((chunk, 4 * LANES), lambda u, i: (i, u))
    win = hb + chunk + hb
    big = lambda: pltpu.VMEM((4, chunk if r > 1 else 8, LANES), F32)
    return pl.pallas_call(
        body, name=name, grid=(n_units, nsteps),
        out_shape=(jax.ShapeDtypeStruct((S, n_units * 512), out_dtype), jax.ShapeDtypeStruct((S, n_units * 512), F32)),
        in_specs=in_specs, out_specs=(wide, wide),
        scratch_shapes=[big(), pltpu.VMEM((win, LANES), F32), pltpu.VMEM((win, LANES), F32), big(), big()],
        compiler_params=_params(("parallel", "parallel"), VMEM_LIMIT),
    )(*args)


def _attn_bwd(qkv, bias, sink, o, do, lse, *, C, r, half, qoff, koff, voff, n_units, name):
    S = qkv.shape[0]
    tq = _tile_rows(r)
    tk = tq + 2 * half
    tiles = CHUNK // (r * tq)
    nsteps = S // CHUNK
    specs, hb = _attn_layout(S, C, r, half, qoff, koff, voff, CHUNK)
    use_sink = sink is not None

    def body(*refs):
        q_ref, kp, km, kn, vp, vm, vn, bias_ref = refs[:8]
        rest = list(refs[8:])
        sink_ref = rest.pop(0) if use_sink else None
        o_ref, do_ref, lse_ref, dq_ref, dk_hbm, dv_hbm = rest[:6]
        rest = rest[6:]
        dsink_ref = rest.pop(0) if use_sink else None
        qs, ks, vs, os_, dos, ls, dqs, acck, accv, sem = rest
        u, i = pl.program_id(0), pl.program_id(1)

        @pl.when(i == 0)
        def _():
            acck[...] = jnp.zeros_like(acck)
            accv[...] = jnp.zeros_like(accv)
            if use_sink:
                dsink_ref[...] = jnp.zeros_like(dsink_ref)

        if r > 1:
            for t in range(4):
                cols = pl.ds(t * LANES, LANES)
                qs[t] = q_ref[:, cols].astype(F32)
                os_[t] = o_ref[:, cols].astype(F32)
                dos[t] = do_ref[:, cols].astype(F32)
                ls[t] = lse_ref[:, cols]
        _stage(ks, [kp, km, kn])
        _stage(vs, [vp, vm, vn])
        lo, hi = _head_masks(tq)

        def tile_in(staged, ref, t, start):
            if r > 1:
                return staged[t, _rows(start, tq, r), :]
            return ref[pl.ds(start, tq), pl.ds(t * LANES, LANES)].astype(F32)

        base = pl.multiple_of(i * CHUNK, CHUNK)
        if use_sink:
            sk = sink_ref[pl.ds(0, 1), :]

        def chain(n, carry):
            rho, c = n // tiles, n % tiles
            start = c * (tq * r) + rho
            if r == 1:
                start = pl.multiple_of(start, tq)
            variant = jnp.where(jnp.logical_and(i == 0, c == 0), 1, 0) + jnp.where(
                jnp.logical_and(i == nsteps - 1, c == tiles - 1), 2, 0)
            k2 = ks[_rows(start, tk, r), :].astype(BF16)
            v2 = vs[_rows(start, tk, r), :].astype(BF16)
            k2t = ks[_rows(start, tk, r), :].T.astype(BF16)
            q8 = _stack_heads([tile_in(qs, q_ref, t, start) for t in range(4)], lo, hi, HEAD_DIM ** -0.5)
            do_tiles = [tile_in(dos, do_ref, t, start) for t in range(4)]
            do8 = _stack_heads(do_tiles, lo, hi, 1.0)
            deltas, lses = [], []
            for t in range(4):
                prod_t = (do_tiles[t] * tile_in(os_, o_ref, t, start)).T
                lse_t = tile_in(ls, lse_ref, t, start).T
                for a in range(2):
                    deltas.append(jnp.sum(prod_t[a * HEAD_DIM:(a + 1) * HEAD_DIM], axis=0, keepdims=True))
                    lses.append(lse_t[a * HEAD_DIM:a * HEAD_DIM + 1])
            delta8 = jnp.concatenate(deltas, axis=1)
            lse8 = jnp.concatenate(lses, axis=1)
            s = _nt(k2, q8) + bias_ref[pl.ds(pl.multiple_of(variant * tk, 8), tk), :]
            p = jnp.exp(s - lse8)
            dp = _nt(v2, do8)
            dsb = (p * (dp - delta8)).astype(BF16)
            dq8t = _nn(k2t, dsb)
            for t in range(4):
                dq_t = _tile_from_columns(dq8t, t, tq) * (HEAD_DIM ** -0.5)
                if r > 1:
                    dqs[t, _rows(start, tq, r), :] = dq_t
                else:
                    dq_ref[pl.ds(start, tq), pl.ds(t * LANES, LANES)] = dq_t.astype(BF16)
            arow = base + start
            if r == 1:
                arow = pl.multiple_of(arow, tq)
            acck[_rows(arow, tk, r), :] = acck[_rows(arow, tk, r), :] + _nn(dsb, q8)
            accv[_rows(arow, tk, r), :] = accv[_rows(arow, tk, r), :] + _nn(p.astype(BF16), do8)
            if use_sink:
                e = jnp.exp(sk - lse8) * delta8
                for h in range(8):
                    part = -jnp.sum(e[:, h * tq:(h + 1) * tq], axis=1, keepdims=True)
                    dsink_ref[pl.ds(h, 1), :] = dsink_ref[pl.ds(h, 1), :] + part
            return carry

        lax.fori_loop(0, r * tiles, chain, 0, unroll=2)
        if r > 1:
            for t in range(4):
                dq_ref[:, pl.ds(t * LANES, LANES)] = dqs[t].astype(BF16)

        @pl.when(i == nsteps - 1)
        def _():
            ck = pltpu.make_async_copy(acck.at[pl.ds(hb, S)], dk_hbm.at[u], sem.at[0])
            cv = pltpu.make_async_copy(accv.at[pl.ds(hb, S)], dv_hbm.at[u], sem.at[1])
            ck.start()
            cv.start()
            ck.wait()
            cv.wait()

    wide = pl.BlockSpec((CHUNK, 4 * LANES), lambda u, i: (i, u))
    hbm = pl.BlockSpec(memory_space=pl.ANY)
    in_specs = specs + [pl.BlockSpec((3 * tk, 8 * tq), lambda u, i: (u, 0))]
    args = [qkv] * 7 + [bias]
    if use_sink:
        in_specs.append(pl.BlockSpec((8, 8 * tq), lambda u, i: (u, 0)))
        args.append(sink)
    in_specs += [wide, wide, wide]
    args += [o, do, lse]
    out_shape = [jax.ShapeDtypeStruct((S, n_units * 512), BF16), jax.ShapeDtypeStruct((n_units, S, LANES), F32),
                 jax.ShapeDtypeStruct((n_units, S, LANES), F32)]
    out_specs = [wide, hbm, hbm]
    if use_sink:
        out_shape.append(jax.ShapeDtypeStruct((n_units * 8, LANES), F32))
        out_specs.append(pl.BlockSpec((8, LANES), lambda u, i: (u, 0)))
    win = hb + CHUNK + hb
    big = lambda: pltpu.VMEM((4, CHUNK if r > 1 else 8, LANES), F32)
    res = pl.pallas_call(
        body, name=name, grid=(n_units, nsteps), out_shape=tuple(out_shape), in_specs=in_specs, out_specs=tuple(out_specs),
        scratch_shapes=[big(), pltpu.VMEM((win, LANES), F32), pltpu.VMEM((win, LANES), F32), big(), big(), big(), big(),
                        pltpu.VMEM((S + 2 * hb, LANES), F32), pltpu.VMEM((S + 2 * hb, LANES), F32), pltpu.SemaphoreType.DMA((2,))],
        compiler_params=_params(("arbitrary", "arbitrary"), VMEM_LIMIT),
    )(*args)
    return res[0], res[1], res[2], (res[3] if use_sink else None)


def _merge_groups(os_, lses):
    S, W = os_[0].shape
    tm = 512

    def body(o0, o1, o2, l0, l1, l2, o_ref, lse_ref):
        ls = [l0[...], l1[...], l2[...]]
        mx = jnp.maximum(jnp.maximum(ls[0], ls[1]), ls[2])
        es = [jnp.exp(l - mx) for l in ls]
        den = es[0] + es[1] + es[2]
        o = (es[0] / den) * o0[...] + (es[1] / den) * o1[...] + (es[2] / den) * o2[...]
        o_ref[...] = o.astype(BF16)
        lse_ref[...] = mx + jnp.log(den)

    row = pl.BlockSpec((tm, W), lambda i: (i, 0))
    return pl.pallas_call(
        body, name="merge_groups", grid=(S // tm,),
        out_shape=(jax.ShapeDtypeStruct((S, W), BF16), jax.ShapeDtypeStruct((S, W), F32)),
        in_specs=[row] * 6, out_specs=(row, row), compiler_params=_params(("parallel",), VMEM_LIMIT),
    )(*os_, *lses)


def _norm_bwd(dy, wt, x, dres, nw, sc, *, name):
    S, N = dy.shape
    tm = 512

    def body(dy_ref, w_ref, x_ref, dres_ref, nw_ref, sc_ref, dx_ref, st_ref):
        @pl.when(pl.program_id(0) == 0)
        def _():
            st_ref[...] = jnp.zeros_like(st_ref)

        nwv, scale = nw_ref[...], 1.0 + sc_ref[...]
        sums = [jnp.zeros((1, D), F32)] * 3
        for half in range(SPLIT):
            rows = pl.ds(half * (tm // SPLIT), tm // SPLIT)
            dh = _nn(dy_ref[rows, :], w_ref[...])
            xv = x_ref[rows, :]
            rstd = lax.rsqrt(jnp.mean(xv * xv, axis=-1, keepdims=True) + RMS_EPS)
            xh = xv * rstd
            dxh = dh * (nwv * scale)
            dx_ref[rows, :] = dres_ref[rows, :] + rstd * (dxh - xh * jnp.mean(dxh * xh, axis=-1, keepdims=True))
            dhx = dh * xh
            sums = [sums[0] + jnp.sum(dh, axis=0, keepdims=True), sums[1] + jnp.sum(dhx * nwv, axis=0, keepdims=True),
                    sums[2] + jnp.sum(dhx * scale, axis=0, keepdims=True)]
        for q in range(3):
            st_ref[pl.ds(q, 1), :] = st_ref[pl.ds(q, 1), :] + sums[q]

    row = lambda w_: pl.BlockSpec((tm, w_), lambda i: (i, 0))
    vec = _const_spec((1, D))
    return pl.pallas_call(
        body, name=name, grid=(S // tm,),
        out_shape=(jax.ShapeDtypeStruct((S, D), F32), jax.ShapeDtypeStruct((8, D), F32)),
        in_specs=[row(N), _const_spec((N, D)), row(D), row(D), vec, vec], out_specs=(row(D), _const_spec((8, D))),
        compiler_params=_params(("arbitrary",), VMEM_LIMIT),
    )(dy, wt, x, dres, nw, sc)


def _ffn_bwd(dx, y, g, w_out, gu, wt_in, x, nw, sc, y1, g1, w_mix, *, mix_is_transposed, name):
    S = dx.shape[0]
    K = w_out.shape[0]
    Km = w_mix.shape[1] if mix_is_transposed else w_mix.shape[0]
    tm = 256

    def body(dx_ref, y_ref, g_ref, wo_ref, gu_ref, wi_ref, x_ref, nw_ref, sc_ref, y1_ref, g1_ref, wm_ref,
             dgu_ref, dyb_ref, dxo_ref, da_ref, dy1_ref, stg_ref, stf_ref, stm_ref):
        @pl.when(pl.program_id(0) == 0)
        def _():
            stg_ref[...] = jnp.zeros_like(stg_ref)
            stf_ref[...] = jnp.zeros_like(stf_ref)
            stm_ref[...] = jnp.zeros_like(stm_ref)

        dxv = dx_ref[...]
        stg_ref[pl.ds(0, 1), :] = stg_ref[pl.ds(0, 1), :] + jnp.sum(dxv * y_ref[...].astype(F32), axis=0, keepdims=True)
        dyb = (dxv * g_ref[...]).astype(BF16)
        dyb_ref[...] = dyb
        da = _nt(dyb, wo_ref[...])
        dgate = (da * gu_ref[:, pl.ds(0, K)].astype(F32)).astype(BF16)
        dup = (da * gu_ref[:, pl.ds(K, K)].astype(F32)).astype(BF16)
        dgu_ref[:, pl.ds(0, K)] = dgate
        dgu_ref[:, pl.ds(K, K)] = dup
        dh = _nn(dgate, wi_ref[pl.ds(0, K), :]) + _nn(dup, wi_ref[pl.ds(K, K), :])
        xv = x_ref[...]
        rstd = lax.rsqrt(jnp.mean(xv * xv, axis=-1, keepdims=True) + RMS_EPS)
        xh = xv * rstd
        nwv, scale = nw_ref[...], 1.0 + sc_ref[...]
        dxh = dh * (nwv * scale)
        dx1 = dxv + rstd * (dxh - xh * jnp.mean(dxh * xh, axis=-1, keepdims=True))
        dxo_ref[...] = dx1
        dhx = dh * xh
        stf_ref[pl.ds(0, 1), :] = stf_ref[pl.ds(0, 1), :] + jnp.sum(dh, axis=0, keepdims=True)
        stf_ref[pl.ds(1, 1), :] = stf_ref[pl.ds(1, 1), :] + jnp.sum(dhx * nwv, axis=0, keepdims=True)
        stf_ref[pl.ds(2, 1), :] = stf_ref[pl.ds(2, 1), :] + jnp.sum(dhx * scale, axis=0, keepdims=True)
        stm_ref[pl.ds(0, 1), :] = stm_ref[pl.ds(0, 1), :] + jnp.sum(dx1 * y1_ref[...].astype(F32), axis=0, keepdims=True)
        dy1 = (dx1 * g1_ref[...]).astype(BF16)
        dy1_ref[...] = dy1
        da_ref[...] = (_nn(dy1, wm_ref[...]) if mix_is_transposed else _nt(dy1, wm_ref[...])).astype(BF16)

    row = lambda w_: pl.BlockSpec((tm, w_), lambda i: (i, 0))
    vec = _const_spec((1, D))
    st = jax.ShapeDtypeStruct((8, D), F32)
    act = lambda w_: jax.ShapeDtypeStruct((S, w_), BF16)
    return pl.pallas_call(
        body, name=name, grid=(S // tm,),
        out_shape=(act(2 * K), act(D), jax.ShapeDtypeStruct((S, D), F32), act(Km), act(D), st, st, st),
        in_specs=[row(D), row(D), vec, _const_spec(w_out.shape), row(2 * K), _const_spec(wt_in.shape), row(D), vec, vec,
                  row(D), vec, _const_spec(w_mix.shape)],
        out_specs=(row(2 * K), row(D), row(D), row(Km), row(D), _const_spec((8, D)), _const_spec((8, D)), _const_spec((8, D))),
        compiler_params=_params(("arbitrary",), VMEM_LIMIT),
    )(dx, y, g, w_out, gu, wt_in, x, nw, sc, y1, g1, w_mix)


def _weight_grad(a, b, *, transpose_out, name):
    S, N = b.shape
    nb = N // 2 if N > 4096 else N
    tk = 512

    def body(a_ref, b_ref, out_ref, acc):
        k = pl.program_id(1)

        @pl.when(k == 0)
        def _():
            acc[...] = jnp.zeros_like(acc)

        acc[...] += _tn(a_ref[...], b_ref[...])

        @pl.when(k == pl.num_programs(1) - 1)
        def _():
            out_ref[...] = (acc[...].T if transpose_out else acc[...]).astype(BF16)

    out_block = pl.BlockSpec((nb, D), lambda n, k: (n, 0)) if transpose_out else pl.BlockSpec((D, nb), lambda n, k: (0, n))
    return pl.pallas_call(
        body, name=name, grid=(N // nb, S // tk),
        out_shape=jax.ShapeDtypeStruct((N, D) if transpose_out else (D, N), BF16),
        in_specs=[pl.BlockSpec((tk, D), lambda n, k: (k, 0)), pl.BlockSpec((tk, nb), lambda n, k: (k, n))],
        out_specs=out_block, scratch_shapes=[pltpu.VMEM((D, nb), F32)],
        compiler_params=_params(("parallel", "arbitrary"), VMEM_LIMIT),
    )(a, b)


def _adamw(w, g, m, v):
    m = ADAM_B1 * m + (1.0 - ADAM_B1) * g
    v = ADAM_B2 * v + (1.0 - ADAM_B2) * (g * g)
    m_hat = m / (1.0 - ADAM_B1 ** ADAM_STEP)
    v_hat = v / (1.0 - ADAM_B2 ** ADAM_STEP)
    delta = -ADAM_LR * (m_hat / (jnp.sqrt(v_hat) + ADAM_EPS) + ADAM_WD * w)
    return delta, m, v


def _adam_shard(parts, own, w, m, v, name):
    R = w.shape[0]
    tr = max(t for t in (16, 32, 64, 128, 192, 256) if R % t == 0)

    def body(p_ref, o_ref, w_ref, m_ref, v_ref, g_out, d_out, m_out, v_out):
        me = _my_index()
        g = jnp.zeros((tr, D), F32)
        for j in range(N_DEV):
            g = g + jnp.where(me == j, o_ref[...], p_ref[j]).astype(F32)
        delta, mn, vn = _adamw(w_ref[...], g, m_ref[...], v_ref[...])
        g_out[...] = g
        d_out[...] = delta
        m_out[...] = mn
        v_out[...] = vn

    row = pl.BlockSpec((tr, D), lambda i: (i, 0))
    shp = jax.ShapeDtypeStruct((R, D), F32)
    return pl.pallas_call(
        body, name=name, grid=(R // tr,), out_shape=(shp,) * 4,
        in_specs=[pl.BlockSpec((N_DEV, tr, D), lambda i: (0, i, 0)), row, row, row, row], out_specs=(row,) * 4,
        compiler_params=_params(("parallel",), VMEM_LIMIT),
    )(parts, own, w, m, v)


def _adam_ada_w(cond_t, dmod, w, m, v):
    ncol = w.shape[-1]
    tr = 512

    def body(c_ref, d_ref, w_ref, m_ref, v_ref, g_out, d_out, m_out, v_out):
        g = _nn(c_ref[...], d_ref[0])
        delta, mn, vn = _adamw(w_ref[0], g, m_ref[0], v_ref[0])
        g_out[0] = g
        d_out[0] = delta
        m_out[0] = mn
        v_out[0] = vn

    blk = pl.BlockSpec((1, tr, ncol), lambda l, i: (l, i, 0))
    shp = jax.ShapeDtypeStruct(w.shape, F32)
    return pl.pallas_call(
        body, name="adam_ada_w", grid=(DEPTH, D // tr), out_shape=(shp,) * 4,
        in_specs=[pl.BlockSpec((tr, LANES), lambda l, i: (i, 0)), pl.BlockSpec((1, LANES, ncol), lambda l, i: (l, 0, 0)), blk, blk, blk],
        out_specs=(blk,) * 4, compiler_params=_params(("parallel", "parallel"), VMEM_LIMIT),
    )(cond_t, dmod, w, m, v)


TILE_ROWS = 168


def _stat_sources():
    pairs = []
    for i in range(DEPTH):
        b = 32 * i
        for q, src in enumerate((b, b + 1, b + 8, b + 16, b + 17, b + 24)):
            pairs.append((6 * i + q, src))
        pairs.append((24 + i, b + 2))
        pairs.append((32 + i, b + 18))
    pairs += [(40, 128), (41, 129)]
    return pairs


def _small_exchange(tiles, w, m, v):
    loss_row, sink_row, sink_src = 41, 48, 136

    def body(s_ref, w_ref, m_ref, v_ref, dmod_out, g_out, d_out, m_out, v_out, loss_out, all_ref, tot_ref, send_sems, recv_sems):
        me = _my_index()
        all_ref[me] = s_ref[...]
        copies = []
        for k in range(1, N_DEV):
            dev, _ = _peer(k)
            cp = pltpu.make_async_remote_copy(src_ref=s_ref, dst_ref=all_ref.at[me], send_sem=send_sems.at[k - 1],
                                              recv_sem=recv_sems.at[k - 1], device_id=dev, device_id_type=MESH)
            cp.start()
            copies.append(cp)
        for k in range(1, N_DEV):
            dev, pidx = _peer(k)
            pltpu.make_async_remote_copy(src_ref=s_ref, dst_ref=all_ref.at[pidx], send_sem=send_sems.at[k - 1],
                                         recv_sem=recv_sems.at[k - 1], device_id=dev, device_id_type=MESH).wait_recv()
        for cp in copies:
            cp.wait_send()
        tot = all_ref[0]
        for j in range(1, N_DEV):
            tot = tot + all_ref[j]
        tot_ref[...] = tot
        g_out[...] = jnp.zeros_like(g_out)
        for dst, src in _stat_sources():
            g_out[pl.ds(dst, 1), :] = tot_ref[pl.ds(src, 1), :]
            if dst < 24:
                for j in range(N_DEV):
                    dmod_out[j, pl.ds(dst, 1), :] = all_ref[j, pl.ds(src, 1), :]
        lane = lax.broadcasted_iota(jnp.int32, (1, D), 1)
        sink = jnp.zeros((1, D), F32)
        for h in range(32):
            sink = jnp.where(lane == h, tot_ref[pl.ds(sink_src + h, 1), :], sink)
        g_out[pl.ds(sink_row, 1), :] = sink
        g = g_out[...]
        delta, mn, vn = _adamw(w_ref[...], g, m_ref[...], v_ref[...])
        d_out[...] = delta
        m_out[...] = mn
        v_out[...] = vn
        loss = jnp.sum(g[loss_row:loss_row + 1, :], axis=-1, keepdims=True) * (0.5 / D)
        loss_out[...] = jnp.broadcast_to(loss, loss_out.shape)

    vm = pl.BlockSpec(memory_space=pltpu.VMEM)
    shp = jax.ShapeDtypeStruct((STAT_ROWS, D), F32)
    return pl.pallas_call(
        body, name="small_exchange",
        out_shape=(jax.ShapeDtypeStruct((N_DEV, 24, D), F32), shp, shp, shp, shp, jax.ShapeDtypeStruct((8, LANES), F32)),
        in_specs=[vm] * 4, out_specs=(vm,) * 6,
        scratch_shapes=[pltpu.VMEM((N_DEV, TILE_ROWS, D), F32), pltpu.VMEM((TILE_ROWS, D), F32),
                        pltpu.SemaphoreType.DMA((N_DEV - 1,)), pltpu.SemaphoreType.DMA((N_DEV - 1,))],
        compiler_params=_params(vmem=VMEM_LIMIT),
    )(tiles, w, m, v)


def _to_rows(name, a):
    if name in ("ffn_in", "a_in", "b_in"):
        return a.T
    if name == "b_out":
        return a.T.reshape(-1, D)
    return a


def _from_rows(name, a):
    if name in ("ffn_in", "a_in", "b_in"):
        return a.T
    if name == "b_out":
        return a.reshape(-1, 512).T
    return a


def _rows8(a):
    return jnp.pad(a, ((0, 8 - a.shape[0]), (0, 0)))


def _pack_small(ada_b, norm_mix, norm_ffn, final_norm, sink):
    sink_row = jnp.pad(sink.reshape(1, -1), ((0, 0), (0, D - sink.size)))
    return jnp.concatenate([ada_b.reshape(24, D), _rows8(norm_mix), _rows8(norm_ffn), _rows8(final_norm.reshape(1, D)),
                            _rows8(sink_row)], axis=0)


def _unpack_small(a):
    return a[0:24].reshape(4, 6 * D), a[24:28], a[32:36], a[40], a[48, :32].reshape(2, 16)


def kernel(x, c, ada_w, ada_b, norm_mix, norm_ffn, ffn_w_in, ffn_w_out, a_w_in, a_w_out, a_sink, b_w_in, b_w_out, final_norm, loss_target, m_ada_w, m_ada_b, m_norm_mix, m_norm_ffn, m_ffn_w_in, m_ffn_w_out, m_a_w_in, m_a_w_out, m_a_sink, m_b_w_in, m_b_w_out, m_final_norm, v_ada_w, v_ada_b, v_norm_mix, v_norm_ffn, v_ffn_w_in, v_ffn_w_out, v_a_w_in, v_a_w_out, v_a_sink, v_b_w_in, v_b_w_out, v_final_norm):
    S = x.shape[1]
    x0 = x.reshape(S, D)
    target = loss_target.reshape(S, D)
    me = _my_index()
    ncol = ada_w.shape[-1]

    ada_b_mine = lax.dynamic_slice_in_dim(ada_b, me * ncol, ncol, axis=1)
    cond_all, parts = _cond_exchange(jnp.broadcast_to(c.reshape(1, D), (8, D)), ada_w, ada_b_mine)
    mod = lax.dynamic_index_in_dim(parts, me, axis=2, keepdims=False)
    mod = jnp.transpose(mod, (1, 0, 2)).reshape(DEPTH, 6, 1, D)

    weights = {"ffn_in": ffn_w_in, "ffn_out": ffn_w_out, "a_in": a_w_in, "a_out": a_w_out, "b_in": b_w_in, "b_out": b_w_out}
    shard = {(n, l): _to_rows(n, weights[n][l]).astype(BF16) for n, l, _ in SEGMENTS}
    first = [sg for sg in _layer_segments(0) if not sg[0].startswith("ffn")]
    gathered0 = _all_gather_weights([shard[(n, l)] for n, l, _ in first])
    W = {(n, l): g for (n, l, _), g in zip(first, gathered0)}
    groups = [[sg for sg in _layer_segments(0) if sg[0].startswith("ffn")]] + [_layer_segments(i) for i in range(1, DEPTH)]
    gathers, order = [], gathered0[0]
    for q, segs in enumerate(groups):
        zones = [lax.empty((N_DEV, rows, D), BF16) for _, _, rows in segs]
        gathers.append(_exchange_start([shard[(n, l)] for n, l, _ in segs], zones, [(s, 0) for s in range(len(segs))],
                                       [sg[2] for sg in segs], False, order, "weight_gather_start_%d" % q))
        order = gathers[-1][-1]
    gather_token = order[0:1, 0:1]

    def finish_gather(q, after):
        segs = groups[q]
        zones = _exchange_wait(gathers[q], len(segs), [(s, 0) for s in range(len(segs))], [sg[2] for sg in segs], after,
                               "weight_gather_wait_%d" % q)
        for (n, l, rows), zone in zip(segs, zones):
            full = lax.dynamic_update_slice(zone, shard[(n, l)][None], (me, 0, 0))
            W[(n, l)] = full.reshape(D, 512) if n == "b_out" else full.reshape(N_DEV * rows, D)

    a_slopes, b_slopes = _slopes(16), _slopes(24)
    bias_a = _alibi_bias(a_slopes, A_HALF, 1)
    bias_b = [_alibi_bias(b_slopes[8 * g:8 * g + 8], B_HALF, dil) for g, dil in enumerate(B_DILS)]
    bias_b_fwd = [_alibi_bias(b_slopes[8 * g:8 * g + 8], B_HALF, dil, max(CHUNK, TQ * dil), both=True) for g, dil in enumerate(B_DILS)]
    a_geom = dict(C=A_QKV, r=1, half=A_HALF, qoff=0, koff=1024, voff=1280, n_units=2)
    b_geom = [dict(C=B_QKV, r=dil, half=B_HALF, qoff=512 * g, koff=1536 + 128 * g, voff=1920 + 128 * g, n_units=1)
              for g, dil in enumerate(B_DILS)]

    saved = []
    xcur = x0
    for i in range(DEPTH):
        j = i // 2
        sh1, sc1, g1, sh2, sc2, g2 = [mod[i, q] for q in range(6)]
        nm, nf = norm_mix[i].reshape(1, D), norm_ffn[i].reshape(1, D)
        if i == 0:
            nm = nm + gather_token
        if i > 0:
            finish_gather(i, xcur)
        if i % 2 == 0:
            sink_rep = jnp.repeat(jnp.repeat(a_sink[j], TQ).reshape(2, 1, 8 * TQ), 8, axis=1).reshape(16, 8 * TQ)
            h1, qkv = _proj(xcur, nm, sc1, sh1, W[("a_in", j)], name="proj_a")
            o, lse = _attn_fwd(qkv, bias_a, sink_rep, out_dtype=BF16, name="attn_a_fwd", **a_geom)
            if i == 0:
                finish_gather(0, o)
            x1, y1, h2, gu, act = _out_ffn_in(o, W[("a_out", j)], xcur, g1, nf, sc2, sh2, W[("ffn_in", i)],
                                              w_is_transposed=False, name="out_a_ffn_in")
        else:
            sink_rep = None
            h1, qkv = _proj(xcur, nm, sc1, sh1, W[("b_in", j)], name="proj_b")
            outs = [_attn_fwd(qkv, bias_b_fwd[g], None, out_dtype=F32, name="attn_b%d_fwd" % g, **b_geom[g]) for g in range(3)]
            o, lse = _merge_groups([t[0] for t in outs], [t[1] for t in outs])
            x1, y1, h2, gu, act = _out_ffn_in(o, W[("b_out", j)], xcur, g1, nf, sc2, sh2, W[("ffn_in", i)],
                                              w_is_transposed=True, name="out_b_ffn_in")
        if i < DEPTH - 1:
            x2, y2 = _ffn_out(act, W[("ffn_out", i)], x1, g2, None, None, name="ffn_out")
        else:
            x2, y2, head_stats = _ffn_out(act, W[("ffn_out", i)], x1, g2, target, final_norm.reshape(1, D), name="ffn_out_loss")
        saved.append(dict(x0=xcur, h1=h1, qkv=qkv, o=o, lse=lse, y1=y1, x1=x1, h2=h2, gu=gu, act=act, y2=y2, sink=sink_rep))
        xcur = x2

    dx = xcur

    dW = {}
    stat_tiles, dsink = [None] * DEPTH, [None] * 2
    exchanges = []
    start_token = None

    def start_exchange(segs):
        offs, total = _offsets(segs)
        own = jnp.concatenate([lax.dynamic_slice_in_dim(dW[(n, l)], me * rows, rows, axis=0) for n, l, rows in segs], axis=0)
        started = _exchange_start([dW[(n, l)] for n, l, _ in segs], [lax.empty((N_DEV, total, D), BF16)], [(0, off) for off in offs],
                                  [sg[2] for sg in segs], True, own, "grad_exchange_start_%d" % len(exchanges))
        exchanges.append((segs, started, own))
        return started[-1][0:1, 0:1]

    for i in reversed(range(DEPTH)):
        j = i // 2
        sv = saved[i]
        sh1, sc1, g1, sh2, sc2, g2 = [mod[i, q] for q in range(6)]
        if start_token is not None:
            g2 = g2 + start_token
            start_token = None
        nm, nf = norm_mix[i].reshape(1, D), norm_ffn[i].reshape(1, D)
        mix = "a_out" if i % 2 == 0 else "b_out"
        dgu, dy2, dx1, do, dy1, st_g2, st_f, st_g1 = _ffn_bwd(
            dx, sv["y2"], g2, W[("ffn_out", i)], sv["gu"], W[("ffn_in", i)], sv["x1"], nf, sc2, sv["y1"], g1, W[(mix, j)],
            mix_is_transposed=(i % 2 == 1), name="ffn_bwd_" + mix)
        dW[("ffn_out", i)] = _weight_grad(dy2, sv["act"], transpose_out=True, name="dw_ffn_out")
        dW[("ffn_in", i)] = _weight_grad(sv["h2"], dgu, transpose_out=True, name="dw_ffn_in")
        sink_bwd = sv["sink"]
        if i == 0:
            sink_bwd = sink_bwd + start_exchange([sg for sg in _layer_segments(0) if sg[0].startswith("ffn")])
        if i % 2 == 0:
            dW[("a_out", j)] = _weight_grad(dy1, sv["o"], transpose_out=True, name="dw_a_out")
            dq, dk, dv, ds = _attn_bwd(sv["qkv"], bias_a, sink_bwd, sv["o"], do, sv["lse"], name="attn_a_bwd", **a_geom)
            dsink[j] = ds
            dqkv = jnp.concatenate([dq, dk[0].astype(BF16), dk[1].astype(BF16), dv[0].astype(BF16), dv[1].astype(BF16)], axis=1)
            dW[("a_in", j)] = _weight_grad(sv["h1"], dqkv, transpose_out=True, name="dw_a_in")
            dx0, st_m = _norm_bwd(dqkv, W[("a_in", j)], sv["x0"], dx1, nm, sc1, name="proj_a_bwd")
        else:
            dW[("b_out", j)] = _weight_grad(dy1, sv["o"], transpose_out=False, name="dw_b_out").reshape(N_DEV * 64, D)
            gr = [_attn_bwd(sv["qkv"], bias_b[g], None, sv["o"], do, sv["lse"], name="attn_b%d_bwd" % g, **b_geom[g]) for g in range(3)]
            dqkv = jnp.concatenate([t[0] for t in gr] + [t[1][0].astype(BF16) for t in gr] + [t[2][0].astype(BF16) for t in gr], axis=1)
            dW[("b_in", j)] = _weight_grad(sv["h1"], dqkv, transpose_out=True, name="dw_b_in")
            dx0, st_m = _norm_bwd(dqkv, W[("b_in", j)], sv["x0"], dx1, nm, sc1, name="proj_b_bwd")
        stat_tiles[i] = [st_m, st_g1, st_f, st_g2]
        if i > 0:
            start_token = start_exchange(_layer_segments(i))
        else:
            start_exchange([sg for sg in _layer_segments(0) if not sg[0].startswith("ffn")])
        dx = dx0
    grad_x = dx.reshape(1, S, D)

    masters = {"ffn_in": (ffn_w_in, m_ffn_w_in, v_ffn_w_in), "ffn_out": (ffn_w_out, m_ffn_w_out, v_ffn_w_out),
               "a_in": (a_w_in, m_a_w_in, v_a_w_in), "a_out": (a_w_out, m_a_w_out, v_a_w_out),
               "b_in": (b_w_in, m_b_w_in, v_b_w_in), "b_out": (b_w_out, m_b_w_out, v_b_w_out)}
    pieces = {}
    after = dx
    for segs, started, own in exchanges:
        offs, total = _offsets(segs)
        parts_g = _exchange_wait(started, len(segs), [(0, off) for off in offs], [sg[2] for sg in segs], after,
                                 "grad_exchange_wait_%d" % len(pieces))[0]
        rows_wmv = [jnp.concatenate([_to_rows(n, masters[n][q][l]) for n, l, _ in segs], axis=0) for q in range(3)]
        res_rows = _adam_shard(parts_g, own, *rows_wmv, name="adam_%d" % total)
        after = res_rows[0]
        for q, kind in enumerate(("grad", "delta", "m", "v")):
            for (n, l, rows), off in zip(segs, offs):
                pieces[(kind, n, l)] = _from_rows(n, res_rows[q][off:off + rows])
    big = {(kind, n): jnp.stack([pieces[(kind, n, l)] for l in range(4 if n.startswith("ffn") else 2)])
           for kind in ("grad", "delta", "m", "v") for n in masters}

    tiles = jnp.concatenate([t for i in range(DEPTH) for t in stat_tiles[i]] + [head_stats]
                            + [jnp.pad(ds, ((0, 0), (0, D - LANES))) for ds in dsink], axis=0)
    small = [_pack_small(*t) for t in ((ada_b, norm_mix, norm_ffn, final_norm, a_sink),
                                       (m_ada_b, m_norm_mix, m_norm_ffn, m_final_norm, m_a_sink),
                                       (v_ada_b, v_norm_mix, v_norm_ffn, v_final_norm, v_a_sink))]
    dmod_all, sg, sd, sm, sv_, loss_tile = _small_exchange(tiles, *small)
    loss = loss_tile[0, 0]
    dmod_all = dmod_all.reshape(N_DEV, DEPTH, 6 * D)
    dmod_mine = lax.dynamic_slice_in_dim(dmod_all, me * ncol, ncol, axis=2)
    dmod_pad = jnp.pad(jnp.transpose(dmod_mine, (1, 0, 2)), ((0, 0), (0, LANES - N_DEV), (0, 0))).astype(BF16)
    cond_t = jnp.pad(cond_all.T, ((0, 0), (0, LANES - N_DEV))).astype(BF16)
    ada = _adam_ada_w(cond_t, dmod_pad, ada_w, m_ada_w, v_ada_w)

    outs = [loss, grad_x]
    small_res = [_unpack_small(t) for t in (sg, sd, sm, sv_)]
    for q, kind in enumerate(("grad", "delta", "m", "v")):
        ab, nm_, nf_, fn, sk = small_res[q]
        outs += [ada[q], ab, nm_, nf_, big[(kind, "ffn_in")], big[(kind, "ffn_out")], big[(kind, "a_in")], big[(kind, "a_out")],
                 sk, big[(kind, "b_in")], big[(kind, "b_out")], fn]
    return tuple(outs)
```

```python
import functools
import math

import numpy as np
import jax
import jax.numpy as jnp
from jax import lax
from jax.experimental import pallas as pl
from jax.experimental.pallas import tpu as pltpu

D = 1024
HEAD_DIM = 64
D_FF = 2816
DEPTH = 4
N_DEV = 8
A_QKV = 1536
B_QKV = 2304
A_HALF = 128
B_HALF = 64
B_DILS = (1, 4, 16)
RMS_EPS = 1e-6
NEG = -1e30
ADAM_LR = 0.001
ADAM_B1 = 0.9
ADAM_B2 = 0.999
ADAM_EPS = 1e-08
ADAM_WD = 0.01
ADAM_STEP = 10

LANES = 128
SPLIT = 2
TQ = 128
VMEM_LIMIT = 56 * 1024 * 1024
MESH = pl.DeviceIdType.MESH
F32 = jnp.float32
BF16 = jnp.bfloat16

SEGMENTS = ([("ffn_in", l, 704) for l in range(4)] + [("ffn_out", l, 352) for l in range(4)]
            + [("a_in", j, 192) for j in range(2)] + [("a_out", j, 128) for j in range(2)]
            + [("b_in", j, 288) for j in range(2)] + [("b_out", j, 64) for j in range(2)])
def _layer_segments(i):
    mixer = "a" if i % 2 == 0 else "b"
    return [s for s in SEGMENTS if (s[0].startswith("ffn") and s[1] == i) or (s[0].startswith(mixer + "_") and s[1] == i // 2)]


def _offsets(segs):
    rows = [s[2] for s in segs]
    return [sum(rows[:k]) for k in range(len(rows))], sum(rows)
STAT_ROWS = 56


def _nn(a, b):
    return jnp.dot(a, b, preferred_element_type=F32)


def _nt(a, b):
    return lax.dot_general(a, b, (((1,), (1,)), ((), ())), preferred_element_type=F32)


def _tn(a, b):
    return lax.dot_general(a, b, (((0,), (0,)), ((), ())), preferred_element_type=F32)


def _params(dims=None, vmem=None):
    kw = {}
    if dims is not None:
        kw["dimension_semantics"] = dims
    if vmem is not None:
        kw["vmem_limit_bytes"] = vmem
    return pltpu.CompilerParams(**kw)


def _my_index():
    return 4 * lax.axis_index("x") + 2 * lax.axis_index("y") + lax.axis_index("c")


def _peer(k):
    x, y, c = lax.axis_index("x"), lax.axis_index("y"), lax.axis_index("c")
    px, py, pc = x ^ ((k >> 2) & 1), y ^ ((k >> 1) & 1), c ^ (k & 1)
    return (px, py, pc), 4 * px + 2 * py + pc


def _const_spec(shape):
    nd = len(shape)
    return pl.BlockSpec(shape, lambda *_: (0,) * nd)


def _cond_exchange(c_tile, ada_w, ada_b_mine):
    ncol = ada_w.shape[-1]

    def body(c_ref, w_ref, b_ref, cond_ref, parts_ref, call_ref, mine_ref, send_sems, recv_sems):
        me = _my_index()
        call_ref[me] = c_ref[...]
        copies = []
        for k in range(1, N_DEV):
            dev, _ = _peer(k)
            cp = pltpu.make_async_remote_copy(src_ref=c_ref, dst_ref=call_ref.at[me], send_sem=send_sems.at[0, k - 1],
                                              recv_sem=recv_sems.at[0, k - 1], device_id=dev, device_id_type=MESH)
            cp.start()
            copies.append(cp)
        for k in range(1, N_DEV):
            _, pidx = _peer(k)
            pltpu.make_async_remote_copy(src_ref=c_ref, dst_ref=call_ref.at[pidx], send_sem=send_sems.at[0, k - 1],
                                         recv_sem=recv_sems.at[0, k - 1], device_id=_peer(k)[0], device_id_type=MESH).wait_recv()
        for cp in copies:
            cp.wait_send()
        row = lax.broadcasted_iota(jnp.int32, (N_DEV, D), 0)
        cmat = jnp.zeros((N_DEV, D), F32)
        for j in range(N_DEV):
            cmat = jnp.where(row == j, call_ref[j], cmat)
        cond = cmat * jax.nn.sigmoid(cmat)
        cond_ref[...] = cond
        cb = cond.astype(BF16)
        for l in range(DEPTH):
            mine_ref[l] = _nn(cb, w_ref[l].astype(BF16)) + b_ref[pl.ds(l, 1), :]
        parts_ref[me] = mine_ref[...]
        copies = []
        for k in range(1, N_DEV):
            dev, _ = _peer(k)
            cp = pltpu.make_async_remote_copy(src_ref=mine_ref, dst_ref=parts_ref.at[me], send_sem=send_sems.at[1, k - 1],
                                              recv_sem=recv_sems.at[1, k - 1], device_id=dev, device_id_type=MESH)
            cp.start()
            copies.append(cp)
        for k in range(1, N_DEV):
            dev, pidx = _peer(k)
            pltpu.make_async_remote_copy(src_ref=mine_ref, dst_ref=parts_ref.at[pidx], send_sem=send_sems.at[1, k - 1],
                                         recv_sem=recv_sems.at[1, k - 1], device_id=dev, device_id_type=MESH).wait_recv()
        for cp in copies:
            cp.wait_send()

    vm = pl.BlockSpec(memory_space=pltpu.VMEM)
    return pl.pallas_call(
        body, name="cond_exchange",
        out_shape=(jax.ShapeDtypeStruct((N_DEV, D), F32), jax.ShapeDtypeStruct((N_DEV, DEPTH, N_DEV, ncol), F32)),
        in_specs=[vm, vm, vm], out_specs=(vm, vm),
        scratch_shapes=[pltpu.VMEM((N_DEV, N_DEV, D), F32), pltpu.VMEM((DEPTH, N_DEV, ncol), F32),
                        pltpu.SemaphoreType.DMA((2, N_DEV - 1)), pltpu.SemaphoreType.DMA((2, N_DEV - 1))],
        compiler_params=_params(vmem=VMEM_LIMIT),
    )(c_tile, ada_w, ada_b_mine)[:2]


def _all_gather_weights(shards):
    n = len(shards)
    big = max(range(n), key=lambda s: shards[s].shape[0])
    total = sum(sh.shape[0] for sh in shards)
    assert N_DEV * shards[big].shape[0] >= total

    def body(*refs):
        ins, outs = refs[:n], refs[n:2 * n]
        local_sems, send_sems, recv_sems = refs[2 * n:]
        me = _my_index()
        local = []
        for s in range(n):
            rows = ins[s].shape[0]
            cp = pltpu.make_async_copy(ins[s], outs[s].at[pl.ds(me * rows, rows)], local_sems.at[s])
            cp.start()
            local.append(cp)
        for k in range(1, N_DEV):
            dev, _ = _peer(k)
            for s in range(n):
                rows = ins[s].shape[0]
                pltpu.make_async_remote_copy(src_ref=ins[s], dst_ref=outs[s].at[pl.ds(me * rows, rows)],
                                             send_sem=send_sems.at[k - 1], recv_sem=recv_sems.at[k - 1],
                                             device_id=dev, device_id_type=MESH).start()
        whole = outs[big].at[pl.ds(0, total)]
        for k in range(1, N_DEV):
            dev, _ = _peer(k)
            w = pltpu.make_async_remote_copy(src_ref=whole, dst_ref=whole, send_sem=send_sems.at[k - 1],
                                             recv_sem=recv_sems.at[k - 1], device_id=dev, device_id_type=MESH)
            w.wait_send()
            w.wait_recv()
        for cp in local:
            cp.wait()

    hbm = pl.BlockSpec(memory_space=pl.ANY)
    return pl.pallas_call(
        body, name="weight_all_gather",
        out_shape=tuple(jax.ShapeDtypeStruct((N_DEV * s.shape[0], D), s.dtype) for s in shards),
        in_specs=[hbm] * n, out_specs=tuple([hbm] * n),
        scratch_shapes=[pltpu.SemaphoreType.DMA((n,)), pltpu.SemaphoreType.DMA((N_DEV - 1,)),
                        pltpu.SemaphoreType.DMA((N_DEV - 1,))],
    )(*shards)


HBM = pl.BlockSpec(memory_space=pltpu.HBM)
SEM = pl.BlockSpec(memory_space=pltpu.SEMAPHORE)
EFFECT = pltpu.SideEffectType.DATAFLOW_SIDE_EFFECTING


def _exchange_start(srcs, landings, dst, rows, to_peer_rows, after, name):
    n, nl = len(srcs), len(landings)

    def body(*refs):
        src_refs, land_refs = refs[:n], refs[n:n + nl]
        send_sems, recv_sems = refs[n + nl + 1], refs[n + nl + 2]
        token = refs[-1]
        me = _my_index()
        for k in range(1, N_DEV):
            dev, pidx = _peer(k)
            for q in range(n):
                src = src_refs[q].at[pl.ds(pidx * rows[q], rows[q])] if to_peer_rows else src_refs[q]
                pltpu.make_async_remote_copy(src_ref=src, dst_ref=land_refs[dst[q][0]].at[me, pl.ds(dst[q][1], rows[q])],
                                             send_sem=send_sems.at[(k - 1) * n + q], recv_sem=recv_sems.at[(k - 1) * n + q],
                                             device_id=dev, device_id_type=MESH).start()
        token[...] = jnp.zeros_like(token)

    arrays = list(srcs) + list(landings)
    sems = pltpu.SemaphoreType.DMA(((N_DEV - 1) * n,))
    return pl.pallas_call(
        body, name=name,
        out_shape=(sems, sems, *[pltpu.HBM(a.shape, a.dtype) for a in arrays], jax.ShapeDtypeStruct((8, LANES), F32)),
        in_specs=[HBM] * (n + nl) + [pl.BlockSpec(memory_space=pl.ANY)],
        out_specs=(SEM, SEM, *[HBM] * (n + nl), pl.BlockSpec(memory_space=pltpu.VMEM)),
        input_output_aliases={q: 2 + q for q in range(n + nl)},
        compiler_params=pltpu.CompilerParams(has_side_effects=EFFECT),
    )(*[pltpu.with_memory_space_constraint(a, pltpu.HBM) for a in arrays], after)


def _exchange_wait(started, n, dst, rows, after, name):
    send_sems, recv_sems = started[0], started[1]
    arrays = list(started[2:-1])
    n1 = len(arrays)

    def body(*refs):
        land_refs = refs[n:n1]
        sends, recvs = refs[n1], refs[n1 + 1]
        for k in range(1, N_DEV):
            dev, _ = _peer(k)
            for q in range(n):
                slot = land_refs[dst[q][0]].at[0, pl.ds(dst[q][1], rows[q])]
                w = pltpu.make_async_remote_copy(src_ref=slot, dst_ref=slot, send_sem=sends.at[(k - 1) * n + q],
                                                 recv_sem=recvs.at[(k - 1) * n + q], device_id=dev, device_id_type=MESH)
                w.wait_send()
                w.wait_recv()

    return pl.pallas_call(
        body, name=name, out_shape=tuple(pltpu.HBM(a.shape, a.dtype) for a in arrays),
        in_specs=[HBM] * n1 + [SEM, SEM, pl.BlockSpec(memory_space=pl.ANY)], out_specs=tuple([HBM] * n1),
        input_output_aliases={q: q for q in range(n1)},
        compiler_params=pltpu.CompilerParams(has_side_effects=EFFECT),
    )(*arrays, send_sems, recv_sems, after)[n:]


def _norm_mod(x, nw, sc, sh):
    ms = jnp.mean(x * x, axis=-1, keepdims=True)
    xh = x * lax.rsqrt(ms + RMS_EPS)
    return xh, (xh * nw) * (1.0 + sc) + sh


def _proj(x, nw, sc, sh, wt, *, name):
    S, N = x.shape[0], wt.shape[0]
    tm = 512

    def body(x_ref, nw_ref, sc_ref, sh_ref, w_ref, h_ref, out_ref):
        for half in range(SPLIT):
            rows = pl.ds(half * (tm // SPLIT), tm // SPLIT)
            _, h = _norm_mod(x_ref[rows, :], nw_ref[...], sc_ref[...], sh_ref[...])
            hb = h.astype(BF16)
            h_ref[rows, :] = hb
            out_ref[rows, :] = _nt(hb, w_ref[...]).astype(BF16)

    row = lambda w: pl.BlockSpec((tm, w), lambda i: (i, 0))
    vec = _const_spec((1, D))
    return pl.pallas_call(
        body, name=name, grid=(S // tm,), out_shape=(jax.ShapeDtypeStruct((S, D), BF16), jax.ShapeDtypeStruct((S, N), BF16)),
        in_specs=[row(D), vec, vec, vec, _const_spec((N, D))], out_specs=(row(D), row(N)),
        compiler_params=_params(("parallel",), VMEM_LIMIT),
    )(x, nw, sc, sh, wt)


def _ffn_out(a, w, x, g, target, fnw, *, name):
    S, K = a.shape
    tm = 512
    last = target is not None

    def body(a_ref, w_ref, x_ref, g_ref, *rest):
        y = _nn(a_ref[...], w_ref[...])
        xv = x_ref[...] + g_ref[...] * y
        if not last:
            xo_ref, y_ref = rest
            y_ref[...] = y.astype(BF16)
            xo_ref[...] = xv
            return
        t_ref, fw_ref, dx_ref, y_ref, st_ref = rest
        y_ref[...] = y.astype(BF16)

        @pl.when(pl.program_id(0) == 0)
        def _():
            st_ref[...] = jnp.zeros_like(st_ref)

        rstd = lax.rsqrt(jnp.mean(xv * xv, axis=-1, keepdims=True) + RMS_EPS)
        xh = xv * rstd
        err = xh * fw_ref[...] - t_ref[...]
        dy = err * (1.0 / D)
        dxh = dy * fw_ref[...]
        dx_ref[...] = rstd * (dxh - xh * jnp.mean(dxh * xh, axis=-1, keepdims=True))
        st_ref[pl.ds(0, 1), :] = st_ref[pl.ds(0, 1), :] + jnp.sum(dy * xh, axis=0, keepdims=True)
        st_ref[pl.ds(1, 1), :] = st_ref[pl.ds(1, 1), :] + jnp.sum(err * err, axis=0, keepdims=True)

    row = lambda w_: pl.BlockSpec((tm, w_), lambda i: (i, 0))
    in_specs = [row(K), _const_spec(w.shape), row(D), _const_spec((1, D))]
    args = [a, w, x, g]
    out_shape = [jax.ShapeDtypeStruct((S, D), F32), jax.ShapeDtypeStruct((S, D), BF16)]
    out_specs = [row(D), row(D)]
    if last:
        in_specs += [row(D), _const_spec((1, D))]
        args += [target, fnw]
        out_shape.append(jax.ShapeDtypeStruct((8, D), F32))
        out_specs.append(_const_spec((8, D)))
    return pl.pallas_call(
        body, name=name, grid=(S // tm,), out_shape=tuple(out_shape), in_specs=in_specs, out_specs=tuple(out_specs),
        compiler_params=_params(("arbitrary",) if last else ("parallel",), VMEM_LIMIT),
    )(*args)


CHUNK = 1024


def _tile_rows(r, chunk=CHUNK):
    return min(TQ, chunk // r)


def _out_ffn_in(a, w_mix, x, g, nw, sc, sh, wt, *, w_is_transposed, name):
    S, K = a.shape
    tm = 256

    def body(a_ref, wm_ref, x_ref, g_ref, nw_ref, sc_ref, sh_ref, w_ref, x1_ref, y_ref, h_ref, gu_ref, act_ref):
        y = _nt(a_ref[...], wm_ref[...]) if w_is_transposed else _nn(a_ref[...], wm_ref[...])
        y_ref[...] = y.astype(BF16)
        x1 = x_ref[...] + g_ref[...] * y
        x1_ref[...] = x1
        _, h = _norm_mod(x1, nw_ref[...], sc_ref[...], sh_ref[...])
        hb = h.astype(BF16)
        h_ref[...] = hb
        gate = _nt(hb, w_ref[pl.ds(0, D_FF), :])
        up = _nt(hb, w_ref[pl.ds(D_FF, D_FF), :])
        sig = jax.nn.sigmoid(gate)
        silu = gate * sig
        gu_ref[:, pl.ds(0, D_FF)] = (up * (sig * (1.0 + gate * (1.0 - sig)))).astype(BF16)
        gu_ref[:, pl.ds(D_FF, D_FF)] = silu.astype(BF16)
        act_ref[...] = (silu * up).astype(BF16)

    row = lambda w_: pl.BlockSpec((tm, w_), lambda i: (i, 0))
    vec = _const_spec((1, D))
    return pl.pallas_call(
        body, name=name, grid=(S // tm,),
        out_shape=(jax.ShapeDtypeStruct((S, D), F32), jax.ShapeDtypeStruct((S, D), BF16), jax.ShapeDtypeStruct((S, D), BF16),
                   jax.ShapeDtypeStruct((S, 2 * D_FF), BF16), jax.ShapeDtypeStruct((S, D_FF), BF16)),
        in_specs=[row(K), _const_spec(w_mix.shape), row(D), vec, vec, vec, vec, _const_spec(wt.shape)],
        out_specs=(row(D), row(D), row(D), row(2 * D_FF), row(D_FF)),
        compiler_params=_params(("parallel",), VMEM_LIMIT),
    )(a, w_mix, x, g, nw, sc, sh, wt)


def _alibi_bias(slopes, half, dil, chunk=CHUNK, both=False):
    tq = _tile_rows(dil, chunk)
    tk = tq + 2 * half
    rel = np.arange(tk)[:, None] - half - np.arange(tq)[None, :]
    band = np.abs(rel) <= half
    dist = (dil * np.abs(rel)).astype(np.float32)
    tabs = [np.where(band, -np.float32(s) * dist, np.float32(NEG)).astype(np.float32) for s in slopes]
    out = []
    for u in range(0, len(tabs), 8):
        tab = np.concatenate(tabs[u:u + 8], axis=1)
        first, last = tab.copy(), tab.copy()
        first[:half] = NEG
        last[tk - half:] = NEG
        out += [tab, first, last]
        if both:
            last = last.copy()
            last[:half] = NEG
            out.append(last)
    return jnp.asarray(np.concatenate(out, axis=0))


def _slopes(n):
    return (2.0 ** (-8.0 * np.arange(1, n + 1) / n)).astype(np.float32)


def _head_masks(tq):
    lane = lax.broadcasted_iota(jnp.int32, (tq, LANES), 1)
    lo = lane < HEAD_DIM
    return lo, jnp.logical_not(lo)


def _stack_heads(tiles, lo, hi, scale):
    blocks = []
    for t in range(4):
        xf = tiles[t] if scale == 1.0 else tiles[t] * scale
        for a in range(2):
            xm = jnp.where(lo if a == 0 else hi, xf, 0.0)
            if a != t // 2:
                xm = pltpu.roll(xm, HEAD_DIM, 1)
            blocks.append(xm.astype(BF16))
    return jnp.concatenate(blocks, axis=0)


def _tile_from_columns(x8t, t, tq):
    r0 = HEAD_DIM * (t // 2)
    top = x8t[r0:r0 + HEAD_DIM, 2 * t * tq:(2 * t + 1) * tq]
    bot = x8t[r0:r0 + HEAD_DIM, (2 * t + 1) * tq:(2 * t + 2) * tq]
    return jnp.concatenate([top, bot], axis=0).T


def _attn_layout(S, C, r, half, qoff, koff, voff, chunk):
    hb = half * r
    per = chunk // hb
    nhb = S // hb
    main = lambda off: pl.BlockSpec((chunk, LANES), lambda u, i: (i, off // LANES + u))
    prev = lambda off: pl.BlockSpec((hb, LANES), lambda u, i: (jnp.maximum(i * per - 1, 0), off // LANES + u))
    nxt = lambda off: pl.BlockSpec((hb, LANES), lambda u, i: (jnp.minimum((i + 1) * per, nhb - 1), off // LANES + u))
    specs = [pl.BlockSpec((chunk, 4 * LANES), lambda u, i: (i, qoff // (4 * LANES) + u))]
    specs += [prev(koff), main(koff), nxt(koff), prev(voff), main(voff), nxt(voff)]
    return specs, hb


def _stage(dst, srcs):
    row = 0
    for src in srcs:
        n = src.shape[0]
        dst[pl.ds(row, n), :] = src[...].astype(F32)
        row += n


def _rows(start, n, r):
    return pl.ds(start, n, stride=r) if r > 1 else pl.ds(start, n)


def _attn_fwd(qkv, bias, sink, *, C, r, half, qoff, koff, voff, n_units, out_dtype, name):
    S = qkv.shape[0]
    chunk = max(CHUNK, TQ * r)
    tq = _tile_rows(r, chunk)
    tk = tq + 2 * half
    tiles = chunk // (r * tq)
    nsteps = S // chunk
    specs, hb = _attn_layout(S, C, r, half, qoff, koff, voff, chunk)
    use_sink = sink is not None

    def body(*refs):
        q_ref, kp, km, kn, vp, vm, vn, bias_ref = refs[:8]
        rest = list(refs[8:])
        sink_ref = rest.pop(0) if use_sink else None
        o_ref, lse_ref, qs, ks, vs, os_, ls = rest
        i = pl.program_id(1)
        if r > 1:
            for t in range(4):
                qs[t] = q_ref[:, pl.ds(t * LANES, LANES)].astype(F32)
        _stage(ks, [kp, km, kn])
        _stage(vs, [vp, vm, vn])
        lo, hi = _head_masks(tq)

        def tile_in(staged, ref, t, start):
            if r > 1:
                return staged[t, _rows(start, tq, r), :]
            return ref[pl.ds(start, tq), pl.ds(t * LANES, LANES)].astype(F32)

        ones = jnp.ones((16, tk), BF16)
        if use_sink:
            sk = sink_ref[pl.ds(0, 1), :]

        def chain(n, carry):
            rho, c = n // tiles, n % tiles
            start = c * (tq * r) + rho
            if r == 1:
                start = pl.multiple_of(start, tq)
            variant = jnp.where(jnp.logical_and(i == 0, c == 0), 1, 0) + jnp.where(
                jnp.logical_and(i == nsteps - 1, c == tiles - 1), 2, 0)
            k2 = ks[_rows(start, tk, r), :].astype(BF16)
            v2t = jnp.concatenate([vs[_rows(start, tk, r), :].T.astype(BF16), ones], axis=0)
            q8 = _stack_heads([tile_in(qs, q_ref, t, start) for t in range(4)], lo, hi, HEAD_DIM ** -0.5)
            s = _nt(k2, q8) + bias_ref[pl.ds(pl.multiple_of(variant * tk, 8), tk), :]
            m = jnp.max(s, axis=0, keepdims=True)
            if use_sink:
                m = jnp.maximum(m, sk)
            pv = _nn(v2t, jnp.exp(s - m).astype(BF16))
            l = pv[LANES:LANES + 1]
            if use_sink:
                l = l + jnp.exp(sk - m)
            o8t = pv[:LANES] / l
            lse8 = jnp.broadcast_to(m + jnp.log(l), (LANES, 8 * tq))
            for t in range(4):
                if r > 1:
                    os_[t, _rows(start, tq, r), :] = _tile_from_columns(o8t, t, tq)
                    ls[t, _rows(start, tq, r), :] = _tile_from_columns(lse8, t, tq)
                else:
                    o_ref[pl.ds(start, tq), pl.ds(t * LANES, LANES)] = _tile_from_columns(o8t, t, tq).astype(out_dtype)
                    lse_ref[pl.ds(start, tq), pl.ds(t * LANES, LANES)] = _tile_from_columns(lse8, t, tq)
            return carry

        lax.fori_loop(0, r * tiles, chain, 0, unroll=4)
        if r > 1:
            for t in range(4):
                o_ref[:, pl.ds(t * LANES, LANES)] = os_[t].astype(out_dtype)
                lse_ref[:, pl.ds(t * LANES, LANES)] = ls[t]

    in_specs = specs + [pl.BlockSpec((bias.shape[0] // n_units, 8 * tq), lambda u, i: (u, 0))]
    args = [qkv] * 7 + [bias]
    if use_sink:
        in_specs.append(pl.BlockSpec((8, 8 * tq), lambda u, i: (u, 0)))
        args.append(sink)
    wide = pl.BlockSpec((chunk, 4 * LANES), lambda u, i: (i, u))
    win = hb + chunk + hb
    big = lambda: pltpu.VMEM((4, chunk if r > 1 else 8, LANES), F32)
    return pl.pallas_call(
        body, name=name, grid=(n_units, nsteps),
        out_shape=(jax.ShapeDtypeStruct((S, n_units * 512), out_dtype), jax.ShapeDtypeStruct((S, n_units * 512), F32)),
        in_specs=in_specs, out_specs=(wide, wide),
        scratch_shapes=[big(), pltpu.VMEM((win, LANES), F32), pltpu.VMEM((win, LANES), F32), big(), big()],
        compiler_params=_params(("parallel", "parallel"), VMEM_LIMIT),
    )(*args)


def _attn_bwd(qkv, bias, sink, o, do, lse, *, C, r, half, qoff, koff, voff, n_units, name):
    S = qkv.shape[0]
    tq = _tile_rows(r)
    tk = tq + 2 * half
    tiles = CHUNK // (r * tq)
    nsteps = S // CHUNK
    specs, hb = _attn_layout(S, C, r, half, qoff, koff, voff, CHUNK)
    use_sink = sink is not None

    def body(*refs):
        q_ref, kp, km, kn, vp, vm, vn, bias_ref = refs[:8]
        rest = list(refs[8:])
        sink_ref = rest.pop(0) if use_sink else None
        o_ref, do_ref, lse_ref, dq_ref, dk_hbm, dv_hbm = rest[:6]
        rest = rest[6:]
        dsink_ref = rest.pop(0) if use_sink else None
        qs, ks, vs, os_, dos, ls, dqs, acck, accv, sem = rest
        u, i = pl.program_id(0), pl.program_id(1)

        @pl.when(i == 0)
        def _():
            acck[...] = jnp.zeros_like(acck)
            accv[...] = jnp.zeros_like(accv)
            if use_sink:
                dsink_ref[...] = jnp.zeros_like(dsink_ref)

        if r > 1:
            for t in range(4):
                cols = pl.ds(t * LANES, LANES)
                qs[t] = q_ref[:, cols].astype(F32)
                os_[t] = o_ref[:, cols].astype(F32)
                dos[t] = do_ref[:, cols].astype(F32)
                ls[t] = lse_ref[:, cols]
        _stage(ks, [kp, km, kn])
        _stage(vs, [vp, vm, vn])
        lo, hi = _head_masks(tq)

        def tile_in(staged, ref, t, start):
            if r > 1:
                return staged[t, _rows(start, tq, r), :]
            return ref[pl.ds(start, tq), pl.ds(t * LANES, LANES)].astype(F32)

        base = pl.multiple_of(i * CHUNK, CHUNK)
        if use_sink:
            sk = sink_ref[pl.ds(0, 1), :]

        def chain(n, carry):
            rho, c = n // tiles, n % tiles
            start = c * (tq * r) + rho
            if r == 1:
                start = pl.multiple_of(start, tq)
            variant = jnp.where(jnp.logical_and(i == 0, c == 0), 1, 0) + jnp.where(
                jnp.logical_and(i == nsteps - 1, c == tiles - 1), 2, 0)
            k2 = ks[_rows(start, tk, r), :].astype(BF16)
            v2 = vs[_rows(start, tk, r), :].astype(BF16)
            k2t = ks[_rows(start, tk, r), :].T.astype(BF16)
            q8 = _stack_heads([tile_in(qs, q_ref, t, start) for t in range(4)], lo, hi, HEAD_DIM ** -0.5)
            do_tiles = [tile_in(dos, do_ref, t, start) for t in range(4)]
            do8 = _stack_heads(do_tiles, lo, hi, 1.0)
            deltas, lses = [], []
            for t in range(4):
                prod_t = (do_tiles[t] * tile_in(os_, o_ref, t, start)).T
                lse_t = tile_in(ls, lse_ref, t, start).T
                for a in range(2):
                    deltas.append(jnp.sum(prod_t[a * HEAD_DIM:(a + 1) * HEAD_DIM], axis=0, keepdims=True))
                    lses.append(lse_t[a * HEAD_DIM:a * HEAD_DIM + 1])
            delta8 = jnp.concatenate(deltas, axis=1)
            lse8 = jnp.concatenate(lses, axis=1)
            s = _nt(k2, q8) + bias_ref[pl.ds(pl.multiple_of(variant * tk, 8), tk), :]
            p = jnp.exp(s - lse8)
            dp = _nt(v2, do8)
            dsb = (p * (dp - delta8)).astype(BF16)
            dq8t = _nn(k2t, dsb)
            for t in range(4):
                dq_t = _tile_from_columns(dq8t, t, tq) * (HEAD_DIM ** -0.5)
                if r > 1:
                    dqs[t, _rows(start, tq, r), :] = dq_t
                else:
                    dq_ref[pl.ds(start, tq), pl.ds(t * LANES, LANES)] = dq_t.astype(BF16)
            arow = base + start
            if r == 1:
                arow = pl.multiple_of(arow, tq)
            acck[_rows(arow, tk, r), :] = acck[_rows(arow, tk, r), :] + _nn(dsb, q8)
            accv[_rows(arow, tk, r), :] = accv[_rows(arow, tk, r), :] + _nn(p.astype(BF16), do8)
            if use_sink:
                e = jnp.exp(sk - lse8) * delta8
                for h in range(8):
                    part = -jnp.sum(e[:, h * tq:(h + 1) * tq], axis=1, keepdims=True)
                    dsink_ref[pl.ds(h, 1), :] = dsink_ref[pl.ds(h, 1), :] + part
            return carry

        lax.fori_loop(0, r * tiles, chain, 0, unroll=2)
        if r > 1:
            for t in range(4):
                dq_ref[:, pl.ds(t * LANES, LANES)] = dqs[t].astype(BF16)

        @pl.when(i == nsteps - 1)
        def _():
            ck = pltpu.make_async_copy(acck.at[pl.ds(hb, S)], dk_hbm.at[u], sem.at[0])
            cv = pltpu.make_async_copy(accv.at[pl.ds(hb, S)], dv_hbm.at[u], sem.at[1])
            ck.start()
            cv.start()
            ck.wait()
            cv.wait()

    wide = pl.BlockSpec((CHUNK, 4 * LANES), lambda u, i: (i, u))
    hbm = pl.BlockSpec(memory_space=pl.ANY)
    in_specs = specs + [pl.BlockSpec((3 * tk, 8 * tq), lambda u, i: (u, 0))]
    args = [qkv] * 7 + [bias]
    if use_sink:
        in_specs.append(pl.BlockSpec((8, 8 * tq), lambda u, i: (u, 0)))
        args.append(sink)
    in_specs += [wide, wide, wide]
    args += [o, do, lse]
    out_shape = [jax.ShapeDtypeStruct((S, n_units * 512), BF16), jax.ShapeDtypeStruct((n_units, S, LANES), F32),
                 jax.ShapeDtypeStruct((n_units, S, LANES), F32)]
    out_specs = [wide, hbm, hbm]
    if use_sink:
        out_shape.append(jax.ShapeDtypeStruct((n_units * 8, LANES), F32))
        out_specs.append(pl.BlockSpec((8, LANES), lambda u, i: (u, 0)))
    win = hb + CHUNK + hb
    big = lambda: pltpu.VMEM((4, CHUNK if r > 1 else 8, LANES), F32)
    res = pl.pallas_call(
        body, name=name, grid=(n_units, nsteps), out_shape=tuple(out_shape), in_specs=in_specs, out_specs=tuple(out_specs),
        scratch_shapes=[big(), pltpu.VMEM((win, LANES), F32), pltpu.VMEM((win, LANES), F32), big(), big(), big(), big(),
                        pltpu.VMEM((S + 2 * hb, LANES), F32), pltpu.VMEM((S + 2 * hb, LANES), F32), pltpu.SemaphoreType.DMA((2,))],
        compiler_params=_params(("arbitrary", "arbitrary"), VMEM_LIMIT),
    )(*args)
    return res[0], res[1], res[2], (res[3] if use_sink else None)


def _merge_groups(os_, lses):
    S, W = os_[0].shape
    tm = 512

    def body(o0, o1, o2, l0, l1, l2, o_ref, lse_ref):
        ls = [l0[...], l1[...], l2[...]]
        mx = jnp.maximum(jnp.maximum(ls[0], ls[1]), ls[2])
        es = [jnp.exp(l - mx) for l in ls]
        den = es[0] + es[1] + es[2]
        o = (es[0] / den) * o0[...] + (es[1] / den) * o1[...] + (es[2] / den) * o2[...]
        o_ref[...] = o.astype(BF16)
        lse_ref[...] = mx + jnp.log(den)

    row = pl.BlockSpec((tm, W), lambda i: (i, 0))
    return pl.pallas_call(
        body, name="merge_groups", grid=(S // tm,),
        out_shape=(jax.ShapeDtypeStruct((S, W), BF16), jax.ShapeDtypeStruct((S, W), F32)),
        in_specs=[row] * 6, out_specs=(row, row), compiler_params=_params(("parallel",), VMEM_LIMIT),
    )(*os_, *lses)


def _norm_bwd(dy, wt, x, dres, nw, sc, *, name):
    S, N = dy.shape
    tm = 512

    def body(dy_ref, w_ref, x_ref, dres_ref, nw_ref, sc_ref, dx_ref, st_ref):
        @pl.when(pl.program_id(0) == 0)
        def _():
            st_ref[...] = jnp.zeros_like(st_ref)

        nwv, scale = nw_ref[...], 1.0 + sc_ref[...]
        sums = [jnp.zeros((1, D), F32)] * 3
        for half in range(SPLIT):
            rows = pl.ds(half * (tm // SPLIT), tm // SPLIT)
            dh = _nn(dy_ref[rows, :], w_ref[...])
            xv = x_ref[rows, :]
            rstd = lax.rsqrt(jnp.mean(xv * xv, axis=-1, keepdims=True) + RMS_EPS)
            xh = xv * rstd
            dxh = dh * (nwv * scale)
            dx_ref[rows, :] = dres_ref[rows, :] + rstd * (dxh - xh * jnp.mean(dxh * xh, axis=-1, keepdims=True))
            dhx = dh * xh
            sums = [sums[0] + jnp.sum(dh, axis=0, keepdims=True), sums[1] + jnp.sum(dhx * nwv, axis=0, keepdims=True),
                    sums[2] + jnp.sum(dhx * scale, axis=0, keepdims=True)]
        for q in range(3):
            st_ref[pl.ds(q, 1), :] = st_ref[pl.ds(q, 1), :] + sums[q]

    row = lambda w_: pl.BlockSpec((tm, w_), lambda i: (i, 0))
    vec = _const_spec((1, D))
    return pl.pallas_call(
        body, name=name, grid=(S // tm,),
        out_shape=(jax.ShapeDtypeStruct((S, D), F32), jax.ShapeDtypeStruct((8, D), F32)),
        in_specs=[row(N), _const_spec((N, D)), row(D), row(D), vec, vec], out_specs=(row(D), _const_spec((8, D))),
        compiler_params=_params(("arbitrary",), VMEM_LIMIT),
    )(dy, wt, x, dres, nw, sc)


def _ffn_bwd(dx, y, g, w_out, gu, wt_in, x, nw, sc, y1, g1, w_mix, *, mix_is_transposed, name):
    S = dx.shape[0]
    K = w_out.shape[0]
    Km = w_mix.shape[1] if mix_is_transposed else w_mix.shape[0]
    tm = 256

    def body(dx_ref, y_ref, g_ref, wo_ref, gu_ref, wi_ref, x_ref, nw_ref, sc_ref, y1_ref, g1_ref, wm_ref,
             dgu_ref, dyb_ref, dxo_ref, da_ref, dy1_ref, stg_ref, stf_ref, stm_ref):
        @pl.when(pl.program_id(0) == 0)
        def _():
            stg_ref[...] = jnp.zeros_like(stg_ref)
            stf_ref[...] = jnp.zeros_like(stf_ref)
            stm_ref[...] = jnp.zeros_like(stm_ref)

        dxv = dx_ref[...]
        stg_ref[pl.ds(0, 1), :] = stg_ref[pl.ds(0, 1), :] + jnp.sum(dxv * y_ref[...].astype(F32), axis=0, keepdims=True)
        dyb = (dxv * g_ref[...]).astype(BF16)
        dyb_ref[...] = dyb
        da = _nt(dyb, wo_ref[...])
        dgate = (da * gu_ref[:, pl.ds(0, K)].astype(F32)).astype(BF16)
        dup = (da * gu_ref[:, pl.ds(K, K)].astype(F32)).astype(BF16)
        dgu_ref[:, pl.ds(0, K)] = dgate
        dgu_ref[:, pl.ds(K, K)] = dup
        dh = _nn(dgate, wi_ref[pl.ds(0, K), :]) + _nn(dup, wi_ref[pl.ds(K, K), :])
        xv = x_ref[...]
        rstd = lax.rsqrt(jnp.mean(xv * xv, axis=-1, keepdims=True) + RMS_EPS)
        xh = xv * rstd
        nwv, scale = nw_ref[...], 1.0 + sc_ref[...]
        dxh = dh * (nwv * scale)
        dx1 = dxv + rstd * (dxh - xh * jnp.mean(dxh * xh, axis=-1, keepdims=True))
        dxo_ref[...] = dx1
        dhx = dh * xh
        stf_ref[pl.ds(0, 1), :] = stf_ref[pl.ds(0, 1), :] + jnp.sum(dh, axis=0, keepdims=True)
        stf_ref[pl.ds(1, 1), :] = stf_ref[pl.ds(1, 1), :] + jnp.sum(dhx * nwv, axis=0, keepdims=True)
        stf_ref[pl.ds(2, 1), :] = stf_ref[pl.ds(2, 1), :] + jnp.sum(dhx * scale, axis=0, keepdims=True)
        stm_ref[pl.ds(0, 1), :] = stm_ref[pl.ds(0, 1), :] + jnp.sum(dx1 * y1_ref[...].astype(F32), axis=0, keepdims=True)
        dy1 = (dx1 * g1_ref[...]).astype(BF16)
        dy1_ref[...] = dy1
        da_ref[...] = (_nn(dy1, wm_ref[...]) if mix_is_transposed else _nt(dy1, wm_ref[...])).astype(BF16)

    row = lambda w_: pl.BlockSpec((tm, w_), lambda i: (i, 0))
    vec = _const_spec((1, D))
    st = jax.ShapeDtypeStruct((8, D), F32)
    act = lambda w_: jax.ShapeDtypeStruct((S, w_), BF16)
    return pl.pallas_call(
        body, name=name, grid=(S // tm,),
        out_shape=(act(2 * K), act(D), jax.ShapeDtypeStruct((S, D), F32), act(Km), act(D), st, st, st),
        in_specs=[row(D), row(D), vec, _const_spec(w_out.shape), row(2 * K), _const_spec(wt_in.shape), row(D), vec, vec,
                  row(D), vec, _const_spec(w_mix.shape)],
        out_specs=(row(2 * K), row(D), row(D), row(Km), row(D), _const_spec((8, D)), _const_spec((8, D)), _const_spec((8, D))),
        compiler_params=_params(("arbitrary",), VMEM_LIMIT),
    )(dx, y, g, w_out, gu, wt_in, x, nw, sc, y1, g1, w_mix)


def _weight_grad(a, b, *, transpose_out, name):
    S, N = b.shape
    nb = N // 2 if N > 4096 else N
    tk = 512

    def body(a_ref, b_ref, out_ref, acc):
        k = pl.program_id(1)

        @pl.when(k == 0)
        def _():
            acc[...] = jnp.zeros_like(acc)

        acc[...] += _tn(a_ref[...], b_ref[...])

        @pl.when(k == pl.num_programs(1) - 1)
        def _():
            out_ref[...] = (acc[...].T if transpose_out else acc[...]).astype(BF16)

    out_block = pl.BlockSpec((nb, D), lambda n, k: (n, 0)) if transpose_out else pl.BlockSpec((D, nb), lambda n, k: (0, n))
    return pl.pallas_call(
        body, name=name, grid=(N // nb, S // tk),
        out_shape=jax.ShapeDtypeStruct((N, D) if transpose_out else (D, N), BF16),
        in_specs=[pl.BlockSpec((tk, D), lambda n, k: (k, 0)), pl.BlockSpec((tk, nb), lambda n, k: (k, n))],
        out_specs=out_block, scratch_shapes=[pltpu.VMEM((D, nb), F32)],
        compiler_params=_params(("parallel", "arbitrary"), VMEM_LIMIT),
    )(a, b)


def _adamw(w, g, m, v):
    m = ADAM_B1 * m + (1.0 - ADAM_B1) * g
    v = ADAM_B2 * v + (1.0 - ADAM_B2) * (g * g)
    m_hat = m / (1.0 - ADAM_B1 ** ADAM_STEP)
    v_hat = v / (1.0 - ADAM_B2 ** ADAM_STEP)
    delta = -ADAM_LR * (m_hat / (jnp.sqrt(v_hat) + ADAM_EPS) + ADAM_WD * w)
    return delta, m, v


def _adam_segment(parts, own, w, m, v, outs, layer, *, transposed, name):
    R, C = own.shape
    if transposed:
        tr, tc, steps = R, min(C, LANES), C // min(C, LANES)
        mine = lambda i: (0, i)
        theirs = lambda i: (0, 0, i)
        block = pl.BlockSpec((1, tc, R), lambda i: (layer, i, 0))
    else:
        tr, tc, steps = 32, C, R // 32
        mine = lambda i: (i, 0)
        theirs = lambda i: (0, i, 0)
        block = pl.BlockSpec((1, tr, C), lambda i: (layer, i, 0))

    def body(p_ref, o_ref, w_ref, m_ref, v_ref, *rest):
        g_out, d_out, m_out, v_out = rest[4:]
        me = _my_index()
        g = jnp.zeros((tr, tc), F32)
        for j in range(N_DEV):
            g = g + jnp.where(me == j, o_ref[...], p_ref[j]).astype(F32)
        if transposed:
            g = g.T
        delta, mn, vn = _adamw(w_ref[0], g, m_ref[0], v_ref[0])
        g_out[0] = g
        d_out[0] = delta
        m_out[0] = mn
        v_out[0] = vn

    passed = pl.BlockSpec(memory_space=pl.ANY)
    shp = jax.ShapeDtypeStruct(w.shape, F32)
    return pl.pallas_call(
        body, name=name, grid=(steps,), out_shape=(shp,) * 4,
        in_specs=[pl.BlockSpec((N_DEV, tr, tc), theirs), pl.BlockSpec((tr, tc), mine), block, block, block] + [passed] * 4,
        out_specs=(block,) * 4, input_output_aliases={5 + q: q for q in range(4)},
        compiler_params=_params(("parallel",), VMEM_LIMIT),
    )(parts, own, w, m, v, *outs)


def _adam_ada_w(cond_t, dmod, w, m, v):
    ncol = w.shape[-1]
    tr = 512

    def body(c_ref, d_ref, w_ref, m_ref, v_ref, g_out, d_out, m_out, v_out):
        g = _nn(c_ref[...], d_ref[0])
        delta, mn, vn = _adamw(w_ref[0], g, m_ref[0], v_ref[0])
        g_out[0] = g
        d_out[0] = delta
        m_out[0] = mn
        v_out[0] = vn

    blk = pl.BlockSpec((1, tr, ncol), lambda l, i: (l, i, 0))
    shp = jax.ShapeDtypeStruct(w.shape, F32)
    return pl.pallas_call(
        body, name="adam_ada_w", grid=(DEPTH, D // tr), out_shape=(shp,) * 4,
        in_specs=[pl.BlockSpec((tr, LANES), lambda l, i: (i, 0)), pl.BlockSpec((1, LANES, ncol), lambda l, i: (l, 0, 0)), blk, blk, blk],
        out_specs=(blk,) * 4, compiler_params=_params(("parallel", "parallel"), VMEM_LIMIT),
    )(cond_t, dmod, w, m, v)


TILE_ROWS = 168


def _stat_sources():
    pairs = []
    for i in range(DEPTH):
        b = 32 * i
        for q, src in enumerate((b, b + 1, b + 8, b + 16, b + 17, b + 24)):
            pairs.append((6 * i + q, src))
        pairs.append((24 + i, b + 2))
        pairs.append((32 + i, b + 18))
    pairs += [(40, 128), (41, 129)]
    return pairs


def _small_exchange(tiles, w, m, v):
    loss_row, sink_row, sink_src = 41, 48, 136

    def body(s_ref, w_ref, m_ref, v_ref, dmod_out, g_out, d_out, m_out, v_out, loss_out, all_ref, tot_ref, send_sems, recv_sems):
        me = _my_index()
        all_ref[me] = s_ref[...]
        copies = []
        for k in range(1, N_DEV):
            dev, _ = _peer(k)
            cp = pltpu.make_async_remote_copy(src_ref=s_ref, dst_ref=all_ref.at[me], send_sem=send_sems.at[k - 1],
                                              recv_sem=recv_sems.at[k - 1], device_id=dev, device_id_type=MESH)
            cp.start()
            copies.append(cp)
        for k in range(1, N_DEV):
            dev, pidx = _peer(k)
            pltpu.make_async_remote_copy(src_ref=s_ref, dst_ref=all_ref.at[pidx], send_sem=send_sems.at[k - 1],
                                         recv_sem=recv_sems.at[k - 1], device_id=dev, device_id_type=MESH).wait_recv()
        for cp in copies:
            cp.wait_send()
        tot = all_ref[0]
        for j in range(1, N_DEV):
            tot = tot + all_ref[j]
        tot_ref[...] = tot
        g_out[...] = jnp.zeros_like(g_out)
        for dst, src in _stat_sources():
            g_out[pl.ds(dst, 1), :] = tot_ref[pl.ds(src, 1), :]
            if dst < 24:
                for j in range(N_DEV):
                    dmod_out[j, pl.ds(dst, 1), :] = all_ref[j, pl.ds(src, 1), :]
        lane = lax.broadcasted_iota(jnp.int32, (1, D), 1)
        sink = jnp.zeros((1, D), F32)
        for h in range(32):
            sink = jnp.where(lane == h, tot_ref[pl.ds(sink_src + h, 1), :], sink)
        g_out[pl.ds(sink_row, 1), :] = sink
        g = g_out[...]
        delta, mn, vn = _adamw(w_ref[...], g, m_ref[...], v_ref[...])
        d_out[...] = delta
        m_out[...] = mn
        v_out[...] = vn
        loss = jnp.sum(g[loss_row:loss_row + 1, :], axis=-1, keepdims=True) * (0.5 / D)
        loss_out[...] = jnp.broadcast_to(loss, loss_out.shape)

    vm = pl.BlockSpec(memory_space=pltpu.VMEM)
    shp = jax.ShapeDtypeStruct((STAT_ROWS, D), F32)
    return pl.pallas_call(
        body, name="small_exchange",
        out_shape=(jax.ShapeDtypeStruct((N_DEV, 24, D), F32), shp, shp, shp, shp, jax.ShapeDtypeStruct((8, LANES), F32)),
        in_specs=[vm] * 4, out_specs=(vm,) * 6,
        scratch_shapes=[pltpu.VMEM((N_DEV, TILE_ROWS, D), F32), pltpu.VMEM((TILE_ROWS, D), F32),
                        pltpu.SemaphoreType.DMA((N_DEV - 1,)), pltpu.SemaphoreType.DMA((N_DEV - 1,))],
        compiler_params=_params(vmem=VMEM_LIMIT),
    )(tiles, w, m, v)


def _to_rows(name, a):
    if name in ("ffn_in", "a_in", "b_in"):
        return a.T
    if name == "b_out":
        return a.T.reshape(-1, D)
    return a


def _rows8(a):
    return jnp.pad(a, ((0, 8 - a.shape[0]), (0, 0)))


def _pack_small(ada_b, norm_mix, norm_ffn, final_norm, sink):
    sink_row = jnp.pad(sink.reshape(1, -1), ((0, 0), (0, D - sink.size)))
    return jnp.concatenate([ada_b.reshape(24, D), _rows8(norm_mix), _rows8(norm_ffn), _rows8(final_norm.reshape(1, D)),
                            _rows8(sink_row)], axis=0)


def _unpack_small(a):
    return a[0:24].reshape(4, 6 * D), a[24:28], a[32:36], a[40], a[48, :32].reshape(2, 16)


def kernel(x, c, ada_w, ada_b, norm_mix, norm_ffn, ffn_w_in, ffn_w_out, a_w_in, a_w_out, a_sink, b_w_in, b_w_out, final_norm, loss_target, m_ada_w, m_ada_b, m_norm_mix, m_norm_ffn, m_ffn_w_in, m_ffn_w_out, m_a_w_in, m_a_w_out, m_a_sink, m_b_w_in, m_b_w_out, m_final_norm, v_ada_w, v_ada_b, v_norm_mix, v_norm_ffn, v_ffn_w_in, v_ffn_w_out, v_a_w_in, v_a_w_out, v_a_sink, v_b_w_in, v_b_w_out, v_final_norm):
    S = x.shape[1]
    x0 = x.reshape(S, D)
    target = loss_target.reshape(S, D)
    me = _my_index()
    ncol = ada_w.shape[-1]

    ada_b_mine = lax.dynamic_slice_in_dim(ada_b, me * ncol, ncol, axis=1)
    cond_all, parts = _cond_exchange(jnp.broadcast_to(c.reshape(1, D), (8, D)), ada_w, ada_b_mine)
    mod = lax.dynamic_index_in_dim(parts, me, axis=2, keepdims=False)
    mod = jnp.transpose(mod, (1, 0, 2)).reshape(DEPTH, 6, 1, D)

    weights = {"ffn_in": ffn_w_in, "ffn_out": ffn_w_out, "a_in": a_w_in, "a_out": a_w_out, "b_in": b_w_in, "b_out": b_w_out}
    shard = {(n, l): _to_rows(n, weights[n][l]).astype(BF16) for n, l, _ in SEGMENTS}
    first = [sg for sg in _layer_segments(0) if not sg[0].startswith("ffn")]
    gathered0 = _all_gather_weights([shard[(n, l)] for n, l, _ in first])
    W = {(n, l): g for (n, l, _), g in zip(first, gathered0)}
    groups = [[sg for sg in _layer_segments(0) if sg[0].startswith("ffn")]] + [_layer_segments(i) for i in range(1, DEPTH)]
    gathers, order = [], gathered0[0]
    for q, segs in enumerate(groups):
        zones = [lax.empty((N_DEV, rows, D), BF16) for _, _, rows in segs]
        gathers.append(_exchange_start([shard[(n, l)] for n, l, _ in segs], zones, [(s, 0) for s in range(len(segs))],
                                       [sg[2] for sg in segs], False, order, "weight_gather_start_%d" % q))
        order = gathers[-1][-1]
    gather_token = order[0:1, 0:1]

    def finish_gather(q, after):
        segs = groups[q]
        zones = _exchange_wait(gathers[q], len(segs), [(s, 0) for s in range(len(segs))], [sg[2] for sg in segs], after,
                               "weight_gather_wait_%d" % q)
        for (n, l, rows), zone in zip(segs, zones):
            full = lax.dynamic_update_slice(zone, shard[(n, l)][None], (me, 0, 0))
            W[(n, l)] = full.reshape(D, 512) if n == "b_out" else full.reshape(N_DEV * rows, D)

    a_slopes, b_slopes = _slopes(16), _slopes(24)
    bias_a = _alibi_bias(a_slopes, A_HALF, 1)
    bias_b = [_alibi_bias(b_slopes[8 * g:8 * g + 8], B_HALF, dil) for g, dil in enumerate(B_DILS)]
    bias_b_fwd = [_alibi_bias(b_slopes[8 * g:8 * g + 8], B_HALF, dil, max(CHUNK, TQ * dil), both=True) for g, dil in enumerate(B_DILS)]
    a_geom = dict(C=A_QKV, r=1, half=A_HALF, qoff=0, koff=1024, voff=1280, n_units=2)
    b_geom = [dict(C=B_QKV, r=dil, half=B_HALF, qoff=512 * g, koff=1536 + 128 * g, voff=1920 + 128 * g, n_units=1)
              for g, dil in enumerate(B_DILS)]

    saved = []
    xcur = x0
    for i in range(DEPTH):
        j = i // 2
        sh1, sc1, g1, sh2, sc2, g2 = [mod[i, q] for q in range(6)]
        nm, nf = norm_mix[i].reshape(1, D), norm_ffn[i].reshape(1, D)
        if i == 0:
            nm = nm + gather_token
        if i > 0:
            finish_gather(i, xcur)
        if i % 2 == 0:
            sink_rep = jnp.repeat(jnp.repeat(a_sink[j], TQ).reshape(2, 1, 8 * TQ), 8, axis=1).reshape(16, 8 * TQ)
            h1, qkv = _proj(xcur, nm, sc1, sh1, W[("a_in", j)], name="proj_a")
            o, lse = _attn_fwd(qkv, bias_a, sink_rep, out_dtype=BF16, name="attn_a_fwd", **a_geom)
            if i == 0:
                finish_gather(0, o)
            x1, y1, h2, gu, act = _out_ffn_in(o, W[("a_out", j)], xcur, g1, nf, sc2, sh2, W[("ffn_in", i)],
                                              w_is_transposed=False, name="out_a_ffn_in")
        else:
            sink_rep = None
            h1, qkv = _proj(xcur, nm, sc1, sh1, W[("b_in", j)], name="proj_b")
            outs = [_attn_fwd(qkv, bias_b_fwd[g], None, out_dtype=F32, name="attn_b%d_fwd" % g, **b_geom[g]) for g in range(3)]
            o, lse = _merge_groups([t[0] for t in outs], [t[1] for t in outs])
            x1, y1, h2, gu, act = _out_ffn_in(o, W[("b_out", j)], xcur, g1, nf, sc2, sh2, W[("ffn_in", i)],
                                              w_is_transposed=True, name="out_b_ffn_in")
        if i < DEPTH - 1:
            x2, y2 = _ffn_out(act, W[("ffn_out", i)], x1, g2, None, None, name="ffn_out")
        else:
            x2, y2, head_stats = _ffn_out(act, W[("ffn_out", i)], x1, g2, target, final_norm.reshape(1, D), name="ffn_out_loss")
        saved.append(dict(x0=xcur, h1=h1, qkv=qkv, o=o, lse=lse, y1=y1, x1=x1, h2=h2, gu=gu, act=act, y2=y2, sink=sink_rep))
        xcur = x2

    dx = xcur

    dW = {}
    stat_tiles, dsink = [None] * DEPTH, [None] * 2
    exchanges = []
    start_token = None

    def start_exchange(segs):
        own = [lax.dynamic_slice_in_dim(dW[(n, l)], me * rows, rows, axis=0) for n, l, rows in segs]
        zones = [lax.empty((N_DEV, rows, D), BF16) for _, _, rows in segs]
        started = _exchange_start([dW[(n, l)] for n, l, _ in segs], zones, [(s, 0) for s in range(len(segs))],
                                  [sg[2] for sg in segs], True, own[0], "grad_exchange_start_%d" % len(exchanges))
        exchanges.append((segs, started, own))
        return started[-1][0:1, 0:1]

    for i in reversed(range(DEPTH)):
        j = i // 2
        sv = saved[i]
        sh1, sc1, g1, sh2, sc2, g2 = [mod[i, q] for q in range(6)]
        if start_token is not None:
            g2 = g2 + start_token
            start_token = None
        nm, nf = norm_mix[i].reshape(1, D), norm_ffn[i].reshape(1, D)
        mix = "a_out" if i % 2 == 0 else "b_out"
        dgu, dy2, dx1, do, dy1, st_g2, st_f, st_g1 = _ffn_bwd(
            dx, sv["y2"], g2, W[("ffn_out", i)], sv["gu"], W[("ffn_in", i)], sv["x1"], nf, sc2, sv["y1"], g1, W[(mix, j)],
            mix_is_transposed=(i % 2 == 1), name="ffn_bwd_" + mix)
        dW[("ffn_out", i)] = _weight_grad(dy2, sv["act"], transpose_out=True, name="dw_ffn_out")
        dW[("ffn_in", i)] = _weight_grad(sv["h2"], dgu, transpose_out=True, name="dw_ffn_in")
        sink_bwd = sv["sink"]
        if i == 0:
            sink_bwd = sink_bwd + start_exchange([sg for sg in _layer_segments(0) if sg[0].startswith("ffn")])
        if i % 2 == 0:
            dW[("a_out", j)] = _weight_grad(dy1, sv["o"], transpose_out=True, name="dw_a_out")
            dq, dk, dv, ds = _attn_bwd(sv["qkv"], bias_a, sink_bwd, sv["o"], do, sv["lse"], name="attn_a_bwd", **a_geom)
            dsink[j] = ds
            dqkv = jnp.concatenate([dq, dk[0].astype(BF16), dk[1].astype(BF16), dv[0].astype(BF16), dv[1].astype(BF16)], axis=1)
            dW[("a_in", j)] = _weight_grad(sv["h1"], dqkv, transpose_out=True, name="dw_a_in")
            dx0, st_m = _norm_bwd(dqkv, W[("a_in", j)], sv["x0"], dx1, nm, sc1, name="proj_a_bwd")
        else:
            dW[("b_out", j)] = _weight_grad(dy1, sv["o"], transpose_out=False, name="dw_b_out").reshape(N_DEV * 64, D)
            gr = [_attn_bwd(sv["qkv"], bias_b[g], None, sv["o"], do, sv["lse"], name="attn_b%d_bwd" % g, **b_geom[g]) for g in range(3)]
            dqkv = jnp.concatenate([t[0] for t in gr] + [t[1][0].astype(BF16) for t in gr] + [t[2][0].astype(BF16) for t in gr], axis=1)
            dW[("b_in", j)] = _weight_grad(sv["h1"], dqkv, transpose_out=True, name="dw_b_in")
            dx0, st_m = _norm_bwd(dqkv, W[("b_in", j)], sv["x0"], dx1, nm, sc1, name="proj_b_bwd")
        stat_tiles[i] = [st_m, st_g1, st_f, st_g2]
        if i > 0:
            start_token = start_exchange(_layer_segments(i))
        else:
            start_exchange([sg for sg in _layer_segments(0) if not sg[0].startswith("ffn")])
        dx = dx0
    grad_x = dx.reshape(1, S, D)

    masters = {"ffn_in": (ffn_w_in, m_ffn_w_in, v_ffn_w_in), "ffn_out": (ffn_w_out, m_ffn_w_out, v_ffn_w_out),
               "a_in": (a_w_in, m_a_w_in, v_a_w_in), "a_out": (a_w_out, m_a_w_out, v_a_w_out),
               "b_in": (b_w_in, m_b_w_in, v_b_w_in), "b_out": (b_w_out, m_b_w_out, v_b_w_out)}
    results = {n: tuple(lax.empty(wmv[0].shape, F32) for _ in range(4)) for n, wmv in masters.items()}
    after = dx
    for e, (segs, started, own) in enumerate(exchanges):
        zones = _exchange_wait(started, len(segs), [(s, 0) for s in range(len(segs))], [sg[2] for sg in segs], after,
                               "grad_exchange_wait_%d" % e)
        for (n, l, rows), zone, mine in zip(segs, zones, own):
            if n == "b_out":
                zone, mine = zone.reshape(N_DEV, LANES, 512), mine.reshape(LANES, 512)
            results[n] = _adam_segment(zone, mine, *masters[n], results[n], l, transposed=n in ("ffn_in", "a_in", "b_in", "b_out"),
                                       name="adam_" + n)
            after = results[n][0]
    big = {(kind, n): results[n][q] for q, kind in enumerate(("grad", "delta", "m", "v")) for n in masters}

    tiles = jnp.concatenate([t for i in range(DEPTH) for t in stat_tiles[i]] + [head_stats]
                            + [jnp.pad(ds, ((0, 0), (0, D - LANES))) for ds in dsink], axis=0)
    small = [_pack_small(*t) for t in ((ada_b, norm_mix, norm_ffn, final_norm, a_sink),
                                       (m_ada_b, m_norm_mix, m_norm_ffn, m_final_norm, m_a_sink),
                                       (v_ada_b, v_norm_mix, v_norm_ffn, v_final_norm, v_a_sink))]
    dmod_all, sg, sd, sm, sv_, loss_tile = _small_exchange(tiles, *small)
    loss = loss_tile[0, 0]
    dmod_all = dmod_all.reshape(N_DEV, DEPTH, 6 * D)
    dmod_mine = lax.dynamic_slice_in_dim(dmod_all, me * ncol, ncol, axis=2)
    dmod_pad = jnp.pad(jnp.transpose(dmod_mine, (1, 0, 2)), ((0, 0), (0, LANES - N_DEV), (0, 0))).astype(BF16)
    cond_t = jnp.pad(cond_all.T, ((0, 0), (0, LANES - N_DEV))).astype(BF16)
    ada = _adam_ada_w(cond_t, dmod_pad, ada_w, m_ada_w, v_ada_w)

    outs = [loss, grad_x]
    small_res = [_unpack_small(t) for t in (sg, sd, sm, sv_)]
    for q, kind in enumerate(("grad", "delta", "m", "v")):
        ab, nm_, nf_, fn, sk = small_res[q]
        outs += [ada[q], ab, nm_, nf_, big[(kind, "ffn_in")], big[(kind, "ffn_out")], big[(kind, "a_in")], big[(kind, "a_out")],
                 sk, big[(kind, "b_in")], big[(kind, "b_out")], fn]
    return tuple(outs)
```

```python
import functools
import math

import numpy as np
import jax
import jax.numpy as jnp
from jax import lax
from jax.experimental import pallas as pl
from jax.experimental.pallas import tpu as pltpu

D = 1024
HEAD_DIM = 64
D_FF = 2816
DEPTH = 4
N_DEV = 8
A_QKV = 1536
B_QKV = 2304
A_HALF = 128
B_HALF = 64
B_DILS = (1, 4, 16)
RMS_EPS = 1e-6
NEG = -1e30
ADAM_LR = 0.001
ADAM_B1 = 0.9
ADAM_B2 = 0.999
ADAM_EPS = 1e-08
ADAM_WD = 0.01
ADAM_STEP = 10

LANES = 128
SPLIT = 2
TQ = 128
VMEM_LIMIT = 56 * 1024 * 1024
MESH = pl.DeviceIdType.MESH
F32 = jnp.float32
BF16 = jnp.bfloat16

SEGMENTS = ([("ffn_in", l, 704) for l in range(4)] + [("ffn_out", l, 352) for l in range(4)]
            + [("a_in", j, 192) for j in range(2)] + [("a_out", j, 128) for j in range(2)]
            + [("b_in", j, 288) for j in range(2)] + [("b_out", j, 64) for j in range(2)])
def _layer_segments(i):
    mixer = "a" if i % 2 == 0 else "b"
    return [s for s in SEGMENTS if (s[0].startswith("ffn") and s[1] == i) or (s[0].startswith(mixer + "_") and s[1] == i // 2)]


def _offsets(segs):
    rows = [s[2] for s in segs]
    return [sum(rows[:k]) for k in range(len(rows))], sum(rows)
STAT_ROWS = 56


def _nn(a, b):
    return jnp.dot(a, b, preferred_element_type=F32)


def _nt(a, b):
    return lax.dot_general(a, b, (((1,), (1,)), ((), ())), preferred_element_type=F32)


def _tn(a, b):
    return lax.dot_general(a, b, (((0,), (0,)), ((), ())), preferred_element_type=F32)


def _params(dims=None, vmem=None):
    kw = {}
    if dims is not None:
        kw["dimension_semantics"] = dims
    if vmem is not None:
        kw["vmem_limit_bytes"] = vmem
    return pltpu.CompilerParams(**kw)


def _my_index():
    return 4 * lax.axis_index("x") + 2 * lax.axis_index("y") + lax.axis_index("c")


def _peer(k):
    x, y, c = lax.axis_index("x"), lax.axis_index("y"), lax.axis_index("c")
    px, py, pc = x ^ ((k >> 2) & 1), y ^ ((k >> 1) & 1), c ^ (k & 1)
    return (px, py, pc), 4 * px + 2 * py + pc


def _const_spec(shape):
    nd = len(shape)
    return pl.BlockSpec(shape, lambda *_: (0,) * nd)


def _cond_exchange(c_tile, ada_w, ada_b_mine):
    ncol = ada_w.shape[-1]

    def body(c_ref, w_ref, b_ref, cond_ref, parts_ref, call_ref, mine_ref, send_sems, recv_sems):
        me = _my_index()
        call_ref[me] = c_ref[...]
        copies = []
        for k in range(1, N_DEV):
            dev, _ = _peer(k)
            cp = pltpu.make_async_remote_copy(src_ref=c_ref, dst_ref=call_ref.at[me], send_sem=send_sems.at[0, k - 1],
                                              recv_sem=recv_sems.at[0, k - 1], device_id=dev, device_id_type=MESH)
            cp.start()
            copies.append(cp)
        for k in range(1, N_DEV):
            _, pidx = _peer(k)
            pltpu.make_async_remote_copy(src_ref=c_ref, dst_ref=call_ref.at[pidx], send_sem=send_sems.at[0, k - 1],
                                         recv_sem=recv_sems.at[0, k - 1], device_id=_peer(k)[0], device_id_type=MESH).wait_recv()
        for cp in copies:
            cp.wait_send()
        row = lax.broadcasted_iota(jnp.int32, (N_DEV, D), 0)
        cmat = jnp.zeros((N_DEV, D), F32)
        for j in range(N_DEV):
            cmat = jnp.where(row == j, call_ref[j], cmat)
        cond = cmat * jax.nn.sigmoid(cmat)
        cond_ref[...] = cond
        cb = cond.astype(BF16)
        for l in range(DEPTH):
            mine_ref[l] = _nn(cb, w_ref[l].astype(BF16)) + b_ref[pl.ds(l, 1), :]
        parts_ref[me] = mine_ref[...]
        copies = []
        for k in range(1, N_DEV):
            dev, _ = _peer(k)
            cp = pltpu.make_async_remote_copy(src_ref=mine_ref, dst_ref=parts_ref.at[me], send_sem=send_sems.at[1, k - 1],
                                              recv_sem=recv_sems.at[1, k - 1], device_id=dev, device_id_type=MESH)
            cp.start()
            copies.append(cp)
        for k in range(1, N_DEV):
            dev, pidx = _peer(k)
            pltpu.make_async_remote_copy(src_ref=mine_ref, dst_ref=parts_ref.at[pidx], send_sem=send_sems.at[1, k - 1],
                                         recv_sem=recv_sems.at[1, k - 1], device_id=dev, device_id_type=MESH).wait_recv()
        for cp in copies:
            cp.wait_send()

    vm = pl.BlockSpec(memory_space=pltpu.VMEM)
    return pl.pallas_call(
        body, name="cond_exchange",
        out_shape=(jax.ShapeDtypeStruct((N_DEV, D), F32), jax.ShapeDtypeStruct((N_DEV, DEPTH, N_DEV, ncol), F32)),
        in_specs=[vm, vm, vm], out_specs=(vm, vm),
        scratch_shapes=[pltpu.VMEM((N_DEV, N_DEV, D), F32), pltpu.VMEM((DEPTH, N_DEV, ncol), F32),
                        pltpu.SemaphoreType.DMA((2, N_DEV - 1)), pltpu.SemaphoreType.DMA((2, N_DEV - 1))],
        compiler_params=_params(vmem=VMEM_LIMIT),
    )(c_tile, ada_w, ada_b_mine)[:2]


def _all_gather_weights(shards):
    n = len(shards)
    big = max(range(n), key=lambda s: shards[s].shape[0])
    total = sum(sh.shape[0] for sh in shards)
    assert N_DEV * shards[big].shape[0] >= total

    def body(*refs):
        ins, outs = refs[:n], refs[n:2 * n]
        local_sems, send_sems, recv_sems = refs[2 * n:]
        me = _my_index()
        local = []
        for s in range(n):
            rows = ins[s].shape[0]
            cp = pltpu.make_async_copy(ins[s], outs[s].at[pl.ds(me * rows, rows)], local_sems.at[s])
            cp.start()
            local.append(cp)
        for k in range(1, N_DEV):
            dev, _ = _peer(k)
            for s in range(n):
                rows = ins[s].shape[0]
                pltpu.make_async_remote_copy(src_ref=ins[s], dst_ref=outs[s].at[pl.ds(me * rows, rows)],
                                             send_sem=send_sems.at[k - 1], recv_sem=recv_sems.at[k - 1],
                                             device_id=dev, device_id_type=MESH).start()
        whole = outs[big].at[pl.ds(0, total)]
        for k in range(1, N_DEV):
            dev, _ = _peer(k)
            w = pltpu.make_async_remote_copy(src_ref=whole, dst_ref=whole, send_sem=send_sems.at[k - 1],
                                             recv_sem=recv_sems.at[k - 1], device_id=dev, device_id_type=MESH)
            w.wait_send()
            w.wait_recv()
        for cp in local:
            cp.wait()

    hbm = pl.BlockSpec(memory_space=pl.ANY)
    return pl.pallas_call(
        body, name="weight_all_gather",
        out_shape=tuple(jax.ShapeDtypeStruct((N_DEV * s.shape[0], D), s.dtype) for s in shards),
        in_specs=[hbm] * n, out_specs=tuple([hbm] * n),
        scratch_shapes=[pltpu.SemaphoreType.DMA((n,)), pltpu.SemaphoreType.DMA((N_DEV - 1,)),
                        pltpu.SemaphoreType.DMA((N_DEV - 1,))],
    )(*shards)


HBM = pl.BlockSpec(memory_space=pltpu.HBM)
SEM = pl.BlockSpec(memory_space=pltpu.SEMAPHORE)
EFFECT = pltpu.SideEffectType.DATAFLOW_SIDE_EFFECTING


def _exchange_start(srcs, landings, dst, rows, to_peer_rows, after, name):
    n, nl = len(srcs), len(landings)

    def body(*refs):
        src_refs = refs[:n]
        send_sems, recv_sems = refs[n + 1], refs[n + 2]
        land_refs = refs[2 * n + 3:2 * n + 3 + nl]
        token = refs[-1]
        me = _my_index()
        for k in range(1, N_DEV):
            dev, pidx = _peer(k)
            for q in range(n):
                src = src_refs[q].at[pl.ds(pidx * rows[q], rows[q])] if to_peer_rows else src_refs[q]
                pltpu.make_async_remote_copy(src_ref=src, dst_ref=land_refs[dst[q][0]].at[me, pl.ds(dst[q][1], rows[q])],
                                             send_sem=send_sems.at[(k - 1) * n + q], recv_sem=recv_sems.at[(k - 1) * n + q],
                                             device_id=dev, device_id_type=MESH).start()
        token[...] = jnp.zeros_like(token)

    sems = pltpu.SemaphoreType.DMA(((N_DEV - 1) * n,))
    return pl.pallas_call(
        body, name=name,
        out_shape=(sems, sems, *[pltpu.HBM(a.shape, a.dtype) for a in srcs], *[pltpu.HBM(shape, BF16) for shape in landings],
                   jax.ShapeDtypeStruct((8, LANES), F32)),
        in_specs=[HBM] * n + [pl.BlockSpec(memory_space=pl.ANY)],
        out_specs=(SEM, SEM, *[HBM] * (n + nl), pl.BlockSpec(memory_space=pltpu.VMEM)),
        input_output_aliases={q: 2 + q for q in range(n)},
        compiler_params=pltpu.CompilerParams(has_side_effects=EFFECT),
    )(*[pltpu.with_memory_space_constraint(a, pltpu.HBM) for a in srcs], after)


def _exchange_wait(started, n, dst, rows, after, name):
    send_sems, recv_sems = started[0], started[1]
    arrays = list(started[2:-1])
    n1 = len(arrays)

    def body(*refs):
        land_refs = refs[n:n1]
        sends, recvs = refs[n1], refs[n1 + 1]
        for k in range(1, N_DEV):
            dev, _ = _peer(k)
            for q in range(n):
                slot = land_refs[dst[q][0]].at[0, pl.ds(dst[q][1], rows[q])]
                w = pltpu.make_async_remote_copy(src_ref=slot, dst_ref=slot, send_sem=sends.at[(k - 1) * n + q],
                                                 recv_sem=recvs.at[(k - 1) * n + q], device_id=dev, device_id_type=MESH)
                w.wait_send()
                w.wait_recv()

    return pl.pallas_call(
        body, name=name, out_shape=tuple(pltpu.HBM(a.shape, a.dtype) for a in arrays),
        in_specs=[HBM] * n1 + [SEM, SEM, pl.BlockSpec(memory_space=pl.ANY)], out_specs=tuple([HBM] * n1),
        input_output_aliases={q: q for q in range(n1)},
        compiler_params=pltpu.CompilerParams(has_side_effects=EFFECT),
    )(*arrays, send_sems, recv_sems, after)[n:]


def _norm_mod(x, nw, sc, sh):
    ms = jnp.mean(x * x, axis=-1, keepdims=True)
    xh = x * lax.rsqrt(ms + RMS_EPS)
    return xh, (xh * nw) * (1.0 + sc) + sh


def _proj(x, nw, sc, sh, wt, *, name):
    S, N = x.shape[0], wt.shape[0]
    tm = 512

    def body(x_ref, nw_ref, sc_ref, sh_ref, w_ref, h_ref, out_ref):
        for half in range(SPLIT):
            rows = pl.ds(half * (tm // SPLIT), tm // SPLIT)
            _, h = _norm_mod(x_ref[rows, :], nw_ref[...], sc_ref[...], sh_ref[...])
            hb = h.astype(BF16)
            h_ref[rows, :] = hb
            out_ref[rows, :] = _nt(hb, w_ref[...]).astype(BF16)

    row = lambda w: pl.BlockSpec((tm, w), lambda i: (i, 0))
    vec = _const_spec((1, D))
    return pl.pallas_call(
        body, name=name, grid=(S // tm,), out_shape=(jax.ShapeDtypeStruct((S, D), BF16), jax.ShapeDtypeStruct((S, N), BF16)),
        in_specs=[row(D), vec, vec, vec, _const_spec((N, D))], out_specs=(row(D), row(N)),
        compiler_params=_params(("parallel",), VMEM_LIMIT),
    )(x, nw, sc, sh, wt)


def _ffn_out(a, w, x, g, target, fnw, *, name):
    S, K = a.shape
    tm = 512
    last = target is not None

    def body(a_ref, w_ref, x_ref, g_ref, *rest):
        y = _nn(a_ref[...], w_ref[...])
        xv = x_ref[...] + g_ref[...] * y
        if not last:
            xo_ref, y_ref = rest
            y_ref[...] = y.astype(BF16)
            xo_ref[...] = xv
            return
        t_ref, fw_ref, dx_ref, y_ref, st_ref = rest
        y_ref[...] = y.astype(BF16)

        @pl.when(pl.program_id(0) == 0)
        def _():
            st_ref[...] = jnp.zeros_like(st_ref)

        rstd = lax.rsqrt(jnp.mean(xv * xv, axis=-1, keepdims=True) + RMS_EPS)
        xh = xv * rstd
        err = xh * fw_ref[...] - t_ref[...]
        dy = err * (1.0 / D)
        dxh = dy * fw_ref[...]
        dx_ref[...] = rstd * (dxh - xh * jnp.mean(dxh * xh, axis=-1, keepdims=True))
        st_ref[pl.ds(0, 1), :] = st_ref[pl.ds(0, 1), :] + jnp.sum(dy * xh, axis=0, keepdims=True)
        st_ref[pl.ds(1, 1), :] = st_ref[pl.ds(1, 1), :] + jnp.sum(err * err, axis=0, keepdims=True)

    row = lambda w_: pl.BlockSpec((tm, w_), lambda i: (i, 0))
    in_specs = [row(K), _const_spec(w.shape), row(D), _const_spec((1, D))]
    args = [a, w, x, g]
    out_shape = [jax.ShapeDtypeStruct((S, D), F32), jax.ShapeDtypeStruct((S, D), BF16)]
    out_specs = [row(D), row(D)]
    if last:
        in_specs += [row(D), _const_spec((1, D))]
        args += [target, fnw]
        out_shape.append(jax.ShapeDtypeStruct((8, D), F32))
        out_specs.append(_const_spec((8, D)))
    return pl.pallas_call(
        body, name=name, grid=(S // tm,), out_shape=tuple(out_shape), in_specs=in_specs, out_specs=tuple(out_specs),
        compiler_params=_params(("arbitrary",) if last else ("parallel",), VMEM_LIMIT),
    )(*args)


CHUNK = 1024


def _tile_rows(r, chunk=CHUNK):
    return min(TQ, chunk // r)


def _out_ffn_in(a, w_mix, x, g, nw, sc, sh, wt, *, w_is_transposed, name):
    S, K = a.shape
    tm = 256

    def body(a_ref, wm_ref, x_ref, g_ref, nw_ref, sc_ref, sh_ref, w_ref, x1_ref, y_ref, h_ref, gu_ref, act_ref):
        y = _nt(a_ref[...], wm_ref[...]) if w_is_transposed else _nn(a_ref[...], wm_ref[...])
        y_ref[...] = y.astype(BF16)
        x1 = x_ref[...] + g_ref[...] * y
        x1_ref[...] = x1
        _, h = _norm_mod(x1, nw_ref[...], sc_ref[...], sh_ref[...])
        hb = h.astype(BF16)
        h_ref[...] = hb
        gate = _nt(hb, w_ref[pl.ds(0, D_FF), :])
        up = _nt(hb, w_ref[pl.ds(D_FF, D_FF), :])
        sig = jax.nn.sigmoid(gate)
        silu = gate * sig
        gu_ref[:, pl.ds(0, D_FF)] = (up * (sig * (1.0 + gate * (1.0 - sig)))).astype(BF16)
        gu_ref[:, pl.ds(D_FF, D_FF)] = silu.astype(BF16)
        act_ref[...] = (silu * up).astype(BF16)

    row = lambda w_: pl.BlockSpec((tm, w_), lambda i: (i, 0))
    vec = _const_spec((1, D))
    return pl.pallas_call(
        body, name=name, grid=(S // tm,),
        out_shape=(jax.ShapeDtypeStruct((S, D), F32), jax.ShapeDtypeStruct((S, D), BF16), jax.ShapeDtypeStruct((S, D), BF16),
                   jax.ShapeDtypeStruct((S, 2 * D_FF), BF16), jax.ShapeDtypeStruct((S, D_FF), BF16)),
        in_specs=[row(K), _const_spec(w_mix.shape), row(D), vec, vec, vec, vec, _const_spec(wt.shape)],
        out_specs=(row(D), row(D), row(D), row(2 * D_FF), row(D_FF)),
        compiler_params=_params(("parallel",), VMEM_LIMIT),
    )(a, w_mix, x, g, nw, sc, sh, wt)


def _alibi_bias(slopes, half, dil, chunk=CHUNK, both=False):
    tq = _tile_rows(dil, chunk)
    tk = tq + 2 * half
    rel = np.arange(tk)[:, None] - half - np.arange(tq)[None, :]
    band = np.abs(rel) <= half
    dist = (dil * np.abs(rel)).astype(np.float32)
    tabs = [np.where(band, -np.float32(s) * dist, np.float32(NEG)).astype(np.float32) for s in slopes]
    out = []
    for u in range(0, len(tabs), 8):
        tab = np.concatenate(tabs[u:u + 8], axis=1)
        first, last = tab.copy(), tab.copy()
        first[:half] = NEG
        last[tk - half:] = NEG
        out += [tab, first, last]
        if both:
            last = last.copy()
            last[:half] = NEG
            out.append(last)
    return jnp.asarray(np.concatenate(out, axis=0))


def _slopes(n):
    return (2.0 ** (-8.0 * np.arange(1, n + 1) / n)).astype(np.float32)


def _head_masks(tq):
    lane = lax.broadcasted_iota(jnp.int32, (tq, LANES), 1)
    lo = lane < HEAD_DIM
    return lo, jnp.logical_not(lo)


def _stack_heads(tiles, lo, hi, scale):
    blocks = []
    for t in range(4):
        xf = tiles[t] if scale == 1.0 else tiles[t] * scale
        for a in range(2):
            xm = jnp.where(lo if a == 0 else hi, xf, 0.0)
            if a != t // 2:
                xm = pltpu.roll(xm, HEAD_DIM, 1)
            blocks.append(xm.astype(BF16))
    return jnp.concatenate(blocks, axis=0)


def _tile_from_columns(x8t, t, tq):
    r0 = HEAD_DIM * (t // 2)
    top = x8t[r0:r0 + HEAD_DIM, 2 * t * tq:(2 * t + 1) * tq]
    bot = x8t[r0:r0 + HEAD_DIM, (2 * t + 1) * tq:(2 * t + 2) * tq]
    return jnp.concatenate([top, bot], axis=0).T


def _attn_layout(S, C, r, half, qoff, koff, voff, chunk):
    hb = half * r
    per = chunk // hb
    nhb = S // hb
    main = lambda off: pl.BlockSpec((chunk, LANES), lambda u, i: (i, off // LANES + u))
    prev = lambda off: pl.BlockSpec((hb, LANES), lambda u, i: (jnp.maximum(i * per - 1, 0), off // LANES + u))
    nxt = lambda off: pl.BlockSpec((hb, LANES), lambda u, i: (jnp.minimum((i + 1) * per, nhb - 1), off // LANES + u))
    specs = [pl.BlockSpec((chunk, 4 * LANES), lambda u, i: (i, qoff // (4 * LANES) + u))]
    specs += [prev(koff), main(koff), nxt(koff), prev(voff), main(voff), nxt(voff)]
    return specs, hb


def _stage(dst, srcs):
    row = 0
    for src in srcs:
        n = src.shape[0]
        dst[pl.ds(row, n), :] = src[...].astype(F32)
        row += n


def _rows(start, n, r):
    return pl.ds(start, n, stride=r) if r > 1 else pl.ds(start, n)


def _attn_fwd(qkv, bias, sink, *, C, r, half, qoff, koff, voff, n_units, out_dtype, name):
    S = qkv.shape[0]
    chunk = max(CHUNK, TQ * r)
    tq = _tile_rows(r, chunk)
    tk = tq + 2 * half
    tiles = chunk // (r * tq)
    nsteps = S // chunk
    specs, hb = _attn_layout(S, C, r, half, qoff, koff, voff, chunk)
    use_sink = sink is not None

    def body(*refs):
        q_ref, kp, km, kn, vp, vm, vn, bias_ref = refs[:8]
        rest = list(refs[8:])
        sink_ref = rest.pop(0) if use_sink else None
        o_ref, lse_ref, qs, ks, vs, os_, ls = rest
        i = pl.program_id(1)
        if r > 1:
            for t in range(4):
                qs[t] = q_ref[:, pl.ds(t * LANES, LANES)].astype(F32)
        _stage(ks, [kp, km, kn])
        _stage(vs, [vp, vm, vn])
        lo, hi = _head_masks(tq)

        def tile_in(staged, ref, t, start):
            if r > 1:
                return staged[t, _rows(start, tq, r), :]
            return ref[pl.ds(start, tq), pl.ds(t * LANES, LANES)].astype(F32)

        ones = jnp.ones((16, tk), BF16)
        if use_sink:
            sk = sink_ref[pl.ds(0, 1), :]

        def chain(n, carry):
            rho, c = n // tiles, n % tiles
            start = c * (tq * r) + rho
            if r == 1:
                start = pl.multiple_of(start, tq)
            variant = jnp.where(jnp.logical_and(i == 0, c == 0), 1, 0) + jnp.where(
                jnp.logical_and(i == nsteps - 1, c == tiles - 1), 2, 0)
            k2 = ks[_rows(start, tk, r), :].astype(BF16)
            v2t = jnp.concatenate([vs[_rows(start, tk, r), :].T.astype(BF16), ones], axis=0)
            q8 = _stack_heads([tile_in(qs, q_ref, t, start) for t in range(4)], lo, hi, HEAD_DIM ** -0.5)
            s = _nt(k2, q8) + bias_ref[pl.ds(pl.multiple_of(variant * tk, 8), tk), :]
            m = jnp.max(s, axis=0, keepdims=True)
            if use_sink:
                m = jnp.maximum(m, sk)
            pv = _nn(v2t, jnp.exp(s - m).astype(BF16))
            l = pv[LANES:LANES + 1]
            if use_sink:
                l = l + jnp.exp(sk - m)
            o8t = pv[:LANES] / l
            lse8 = jnp.broadcast_to(m + jnp.log(l), (LANES, 8 * tq))
            for t in range(4):
                if r > 1:
                    os_[t, _rows(start, tq, r), :] = _tile_from_columns(o8t, t, tq)
                    ls[t, _rows(start, tq, r), :] = _tile_from_columns(lse8, t, tq)
                else:
                    o_ref[pl.ds(start, tq), pl.ds(t * LANES, LANES)] = _tile_from_columns(o8t, t, tq).astype(out_dtype)
                    lse_ref[pl.ds(start, tq), pl.ds(t * LANES, LANES)] = _tile_from_columns(lse8, t, tq)
            return carry

        lax.fori_loop(0, r * tiles, chain, 0, unroll=4)
        if r > 1:
            for t in range(4):
                o_ref[:, pl.ds(t * LANES, LANES)] = os_[t].astype(out_dtype)
                lse_ref[:, pl.ds(t * LANES, LANES)] = ls[t]

    in_specs = specs + [pl.BlockSpec((bias.shape[0] // n_units, 8 * tq), lambda u, i: (u, 0))]
    args = [qkv] * 7 + [bias]
    if use_sink:
        in_specs.append(pl.BlockSpec((8, 8 * tq), lambda u, i: (u, 0)))
        args.append(sink)
    wide = pl.BlockSpec((chunk, 4 * LANES), lambda u, i: (i, u))
    win = hb + chunk + hb
    big = lambda: pltpu.VMEM((4, chunk if r > 1 else 8, LANES), F32)
    return pl.pallas_call(
        body, name=name, grid=(n_units, nsteps),
        out_shape=(jax.ShapeDtypeStruct((S, n_units * 512), out_dtype), jax.ShapeDtypeStruct((S, n_units * 512), F32)),
        in_specs=in_specs, out_specs=(wide, wide),
        scratch_shapes=[big(), pltpu.VMEM((win, LANES), F32), pltpu.VMEM((win, LANES), F32), big(), big()],
        compiler_params=_params(("parallel", "parallel"), VMEM_LIMIT),
    )(*args)


def _attn_bwd(qkv, bias, sink, o, do, lse, *, C, r, half, qoff, koff, voff, n_units, name):
    S = qkv.shape[0]
    tq = _tile_rows(r)
    tk = tq + 2 * half
    tiles = CHUNK // (r * tq)
    nsteps = S // CHUNK
    specs, hb = _attn_layout(S, C, r, half, qoff, koff, voff, CHUNK)
    use_sink = sink is not None

    def body(*refs):
        q_ref, kp, km, kn, vp, vm, vn, bias_ref = refs[:8]
        rest = list(refs[8:])
        sink_ref = rest.pop(0) if use_sink else None
        o_ref, do_ref, lse_ref, dq_ref, dk_hbm, dv_hbm = rest[:6]
        rest = rest[6:]
        dsink_ref = rest.pop(0) if use_sink else None
        qs, ks, vs, os_, dos, ls, dqs, acck, accv, sem = rest
        u, i = pl.program_id(0), pl.program_id(1)

        @pl.when(i == 0)
        def _():
            acck[...] = jnp.zeros_like(acck)
            accv[...] = jnp.zeros_like(accv)
            if use_sink:
                dsink_ref[...] = jnp.zeros_like(dsink_ref)

        if r > 1:
            for t in range(4):
                cols = pl.ds(t * LANES, LANES)
                qs[t] = q_ref[:, cols].astype(F32)
                os_[t] = o_ref[:, cols].astype(F32)
                dos[t] = do_ref[:, cols].astype(F32)
                ls[t] = lse_ref[:, cols]
        _stage(ks, [kp, km, kn])
        _stage(vs, [vp, vm, vn])
        lo, hi = _head_masks(tq)

        def tile_in(staged, ref, t, start):
            if r > 1:
                return staged[t, _rows(start, tq, r), :]
            return ref[pl.ds(start, tq), pl.ds(t * LANES, LANES)].astype(F32)

        base = pl.multiple_of(i * CHUNK, CHUNK)
        if use_sink:
            sk = sink_ref[pl.ds(0, 1), :]

        def chain(n, carry):
            rho, c = n // tiles, n % tiles
            start = c * (tq * r) + rho
            if r == 1:
                start = pl.multiple_of(start, tq)
            variant = jnp.where(jnp.logical_and(i == 0, c == 0), 1, 0) + jnp.where(
                jnp.logical_and(i == nsteps - 1, c == tiles - 1), 2, 0)
            k2 = ks[_rows(start, tk, r), :].astype(BF16)
            v2 = vs[_rows(start, tk, r), :].astype(BF16)
            k2t = ks[_rows(start, tk, r), :].T.astype(BF16)
            q8 = _stack_heads([tile_in(qs, q_ref, t, start) for t in range(4)], lo, hi, HEAD_DIM ** -0.5)
            do_tiles = [tile_in(dos, do_ref, t, start) for t in range(4)]
            do8 = _stack_heads(do_tiles, lo, hi, 1.0)
            deltas, lses = [], []
            for t in range(4):
                prod_t = (do_tiles[t] * tile_in(os_, o_ref, t, start)).T
                lse_t = tile_in(ls, lse_ref, t, start).T
                for a in range(2):
                    deltas.append(jnp.sum(prod_t[a * HEAD_DIM:(a + 1) * HEAD_DIM], axis=0, keepdims=True))
                    lses.append(lse_t[a * HEAD_DIM:a * HEAD_DIM + 1])
            delta8 = jnp.concatenate(deltas, axis=1)
            lse8 = jnp.concatenate(lses, axis=1)
            s = _nt(k2, q8) + bias_ref[pl.ds(pl.multiple_of(variant * tk, 8), tk), :]
            p = jnp.exp(s - lse8)
            dp = _nt(v2, do8)
            dsb = (p * (dp - delta8)).astype(BF16)
            dq8t = _nn(k2t, dsb)
            for t in range(4):
                dq_t = _tile_from_columns(dq8t, t, tq) * (HEAD_DIM ** -0.5)
                if r > 1:
                    dqs[t, _rows(start, tq, r), :] = dq_t
                else:
                    dq_ref[pl.ds(start, tq), pl.ds(t * LANES, LANES)] = dq_t.astype(BF16)
            arow = base + start
            if r == 1:
                arow = pl.multiple_of(arow, tq)
            acck[_rows(arow, tk, r), :] = acck[_rows(arow, tk, r), :] + _nn(dsb, q8)
            accv[_rows(arow, tk, r), :] = accv[_rows(arow, tk, r), :] + _nn(p.astype(BF16), do8)
            if use_sink:
                e = jnp.exp(sk - lse8) * delta8
                for h in range(8):
                    part = -jnp.sum(e[:, h * tq:(h + 1) * tq], axis=1, keepdims=True)
                    dsink_ref[pl.ds(h, 1), :] = dsink_ref[pl.ds(h, 1), :] + part
            return carry

        lax.fori_loop(0, r * tiles, chain, 0, unroll=2)
        if r > 1:
            for t in range(4):
                dq_ref[:, pl.ds(t * LANES, LANES)] = dqs[t].astype(BF16)

        @pl.when(i == nsteps - 1)
        def _():
            ck = pltpu.make_async_copy(acck.at[pl.ds(hb, S)], dk_hbm.at[u], sem.at[0])
            cv = pltpu.make_async_copy(accv.at[pl.ds(hb, S)], dv_hbm.at[u], sem.at[1])
            ck.start()
            cv.start()
            ck.wait()
            cv.wait()

    wide = pl.BlockSpec((CHUNK, 4 * LANES), lambda u, i: (i, u))
    hbm = pl.BlockSpec(memory_space=pl.ANY)
    in_specs = specs + [pl.BlockSpec((3 * tk, 8 * tq), lambda u, i: (u, 0))]
    args = [qkv] * 7 + [bias]
    if use_sink:
        in_specs.append(pl.BlockSpec((8, 8 * tq), lambda u, i: (u, 0)))
        args.append(sink)
    in_specs += [wide, wide, wide]
    args += [o, do, lse]
    out_shape = [jax.ShapeDtypeStruct((S, n_units * 512), BF16), jax.ShapeDtypeStruct((n_units, S, LANES), F32),
                 jax.ShapeDtypeStruct((n_units, S, LANES), F32)]
    out_specs = [wide, hbm, hbm]
    if use_sink:
        out_shape.append(jax.ShapeDtypeStruct((n_units * 8, LANES), F32))
        out_specs.append(pl.BlockSpec((8, LANES), lambda u, i: (u, 0)))
    win = hb + CHUNK + hb
    big = lambda: pltpu.VMEM((4, CHUNK if r > 1 else 8, LANES), F32)
    res = pl.pallas_call(
        body, name=name, grid=(n_units, nsteps), out_shape=tuple(out_shape), in_specs=in_specs, out_specs=tuple(out_specs),
        scratch_shapes=[big(), pltpu.VMEM((win, LANES), F32), pltpu.VMEM((win, LANES), F32), big(), big(), big(), big(),
                        pltpu.VMEM((S + 2 * hb, LANES), F32), pltpu.VMEM((S + 2 * hb, LANES), F32), pltpu.SemaphoreType.DMA((2,))],
        compiler_params=_params(("arbitrary", "arbitrary"), VMEM_LIMIT),
    )(*args)
    return res[0], res[1], res[2], (res[3] if use_sink else None)


def _merge_groups(os_, lses):
    S, W = os_[0].shape
    tm = 512

    def body(o0, o1, o2, l0, l1, l2, o_ref, lse_ref):
        ls = [l0[...], l1[...], l2[...]]
        mx = jnp.maximum(jnp.maximum(ls[0], ls[1]), ls[2])
        es = [jnp.exp(l - mx) for l in ls]
        den = es[0] + es[1] + es[2]
        o = (es[0] / den) * o0[...] + (es[1] / den) * o1[...] + (es[2] / den) * o2[...]
        o_ref[...] = o.astype(BF16)
        lse_ref[...] = mx + jnp.log(den)

    row = pl.BlockSpec((tm, W), lambda i: (i, 0))
    return pl.pallas_call(
        body, name="merge_groups", grid=(S // tm,),
        out_shape=(jax.ShapeDtypeStruct((S, W), BF16), jax.ShapeDtypeStruct((S, W), F32)),
        in_specs=[row] * 6, out_specs=(row, row), compiler_params=_params(("parallel",), VMEM_LIMIT),
    )(*os_, *lses)


def _norm_bwd(dy, wt, x, dres, nw, sc, *, name):
    S, N = dy.shape
    tm = 512

    def body(dy_ref, w_ref, x_ref, dres_ref, nw_ref, sc_ref, dx_ref, st_ref):
        @pl.when(pl.program_id(0) == 0)
        def _():
            st_ref[...] = jnp.zeros_like(st_ref)

        nwv, scale = nw_ref[...], 1.0 + sc_ref[...]
        sums = [jnp.zeros((1, D), F32)] * 3
        for half in range(SPLIT):
            rows = pl.ds(half * (tm // SPLIT), tm // SPLIT)
            dh = _nn(dy_ref[rows, :], w_ref[...])
            xv = x_ref[rows, :]
            rstd = lax.rsqrt(jnp.mean(xv * xv, axis=-1, keepdims=True) + RMS_EPS)
            xh = xv * rstd
            dxh = dh * (nwv * scale)
            dx_ref[rows, :] = dres_ref[rows, :] + rstd * (dxh - xh * jnp.mean(dxh * xh, axis=-1, keepdims=True))
            dhx = dh * xh
            sums = [sums[0] + jnp.sum(dh, axis=0, keepdims=True), sums[1] + jnp.sum(dhx * nwv, axis=0, keepdims=True),
                    sums[2] + jnp.sum(dhx * scale, axis=0, keepdims=True)]
        for q in range(3):
            st_ref[pl.ds(q, 1), :] = st_ref[pl.ds(q, 1), :] + sums[q]

    row = lambda w_: pl.BlockSpec((tm, w_), lambda i: (i, 0))
    vec = _const_spec((1, D))
    return pl.pallas_call(
        body, name=name, grid=(S // tm,),
        out_shape=(jax.ShapeDtypeStruct((S, D), F32), jax.ShapeDtypeStruct((8, D), F32)),
        in_specs=[row(N), _const_spec((N, D)), row(D), row(D), vec, vec], out_specs=(row(D), _const_spec((8, D))),
        compiler_params=_params(("arbitrary",), VMEM_LIMIT),
    )(dy, wt, x, dres, nw, sc)


def _ffn_bwd(dx, y, g, w_out, gu, wt_in, x, nw, sc, y1, g1, w_mix, *, mix_is_transposed, name):
    S = dx.shape[0]
    K = w_out.shape[0]
    Km = w_mix.shape[1] if mix_is_transposed else w_mix.shape[0]
    tm = 256

    def body(dx_ref, y_ref, g_ref, wo_ref, gu_ref, wi_ref, x_ref, nw_ref, sc_ref, y1_ref, g1_ref, wm_ref,
             dgu_ref, dyb_ref, dxo_ref, da_ref, dy1_ref, stg_ref, stf_ref, stm_ref):
        @pl.when(pl.program_id(0) == 0)
        def _():
            stg_ref[...] = jnp.zeros_like(stg_ref)
            stf_ref[...] = jnp.zeros_like(stf_ref)
            stm_ref[...] = jnp.zeros_like(stm_ref)

        dxv = dx_ref[...]
        stg_ref[pl.ds(0, 1), :] = stg_ref[pl.ds(0, 1), :] + jnp.sum(dxv * y_ref[...].astype(F32), axis=0, keepdims=True)
        dyb = (dxv * g_ref[...]).astype(BF16)
        dyb_ref[...] = dyb
        da = _nt(dyb, wo_ref[...])
        dgate = (da * gu_ref[:, pl.ds(0, K)].astype(F32)).astype(BF16)
        dup = (da * gu_ref[:, pl.ds(K, K)].astype(F32)).astype(BF16)
        dgu_ref[:, pl.ds(0, K)] = dgate
        dgu_ref[:, pl.ds(K, K)] = dup
        dh = _nn(dgate, wi_ref[pl.ds(0, K), :]) + _nn(dup, wi_ref[pl.ds(K, K), :])
        xv = x_ref[...]
        rstd = lax.rsqrt(jnp.mean(xv * xv, axis=-1, keepdims=True) + RMS_EPS)
        xh = xv * rstd
        nwv, scale = nw_ref[...], 1.0 + sc_ref[...]
        dxh = dh * (nwv * scale)
        dx1 = dxv + rstd * (dxh - xh * jnp.mean(dxh * xh, axis=-1, keepdims=True))
        dxo_ref[...] = dx1
        dhx = dh * xh
        stf_ref[pl.ds(0, 1), :] = stf_ref[pl.ds(0, 1), :] + jnp.sum(dh, axis=0, keepdims=True)
        stf_ref[pl.ds(1, 1), :] = stf_ref[pl.ds(1, 1), :] + jnp.sum(dhx * nwv, axis=0, keepdims=True)
        stf_ref[pl.ds(2, 1), :] = stf_ref[pl.ds(2, 1), :] + jnp.sum(dhx * scale, axis=0, keepdims=True)
        stm_ref[pl.ds(0, 1), :] = stm_ref[pl.ds(0, 1), :] + jnp.sum(dx1 * y1_ref[...].astype(F32), axis=0, keepdims=True)
        dy1 = (dx1 * g1_ref[...]).astype(BF16)
        dy1_ref[...] = dy1
        da_ref[...] = (_nn(dy1, wm_ref[...]) if mix_is_transposed else _nt(dy1, wm_ref[...])).astype(BF16)

    row = lambda w_: pl.BlockSpec((tm, w_), lambda i: (i, 0))
    vec = _const_spec((1, D))
    st = jax.ShapeDtypeStruct((8, D), F32)
    act = lambda w_: jax.ShapeDtypeStruct((S, w_), BF16)
    return pl.pallas_call(
        body, name=name, grid=(S // tm,),
        out_shape=(act(2 * K), act(D), jax.ShapeDtypeStruct((S, D), F32), act(Km), act(D), st, st, st),
        in_specs=[row(D), row(D), vec, _const_spec(w_out.shape), row(2 * K), _const_spec(wt_in.shape), row(D), vec, vec,
                  row(D), vec, _const_spec(w_mix.shape)],
        out_specs=(row(2 * K), row(D), row(D), row(Km), row(D), _const_spec((8, D)), _const_spec((8, D)), _const_spec((8, D))),
        compiler_params=_params(("arbitrary",), VMEM_LIMIT),
    )(dx, y, g, w_out, gu, wt_in, x, nw, sc, y1, g1, w_mix)


def _weight_grad(a, b, *, transpose_out, name):
    S, N = b.shape
    nb = N // 2 if N > 4096 else N
    tk = 512

    def body(a_ref, b_ref, out_ref, acc):
        k = pl.program_id(1)

        @pl.when(k == 0)
        def _():
            acc[...] = jnp.zeros_like(acc)

        acc[...] += _tn(a_ref[...], b_ref[...])

        @pl.when(k == pl.num_programs(1) - 1)
        def _():
            out_ref[...] = (acc[...].T if transpose_out else acc[...]).astype(BF16)

    out_block = pl.BlockSpec((nb, D), lambda n, k: (n, 0)) if transpose_out else pl.BlockSpec((D, nb), lambda n, k: (0, n))
    return pl.pallas_call(
        body, name=name, grid=(N // nb, S // tk),
        out_shape=jax.ShapeDtypeStruct((N, D) if transpose_out else (D, N), BF16),
        in_specs=[pl.BlockSpec((tk, D), lambda n, k: (k, 0)), pl.BlockSpec((tk, nb), lambda n, k: (k, n))],
        out_specs=out_block, scratch_shapes=[pltpu.VMEM((D, nb), F32)],
        compiler_params=_params(("parallel", "arbitrary"), VMEM_LIMIT),
    )(a, b)


def _adamw(w, g, m, v):
    m = ADAM_B1 * m + (1.0 - ADAM_B1) * g
    v = ADAM_B2 * v + (1.0 - ADAM_B2) * (g * g)
    m_hat = m / (1.0 - ADAM_B1 ** ADAM_STEP)
    v_hat = v / (1.0 - ADAM_B2 ** ADAM_STEP)
    delta = -ADAM_LR * (m_hat / (jnp.sqrt(v_hat) + ADAM_EPS) + ADAM_WD * w)
    return delta, m, v


def _adam_segment(parts, own, w, m, v, outs, layer, *, transposed, name):
    R, C = own.shape
    if transposed:
        tr, tc, steps = R, min(C, LANES), C // min(C, LANES)
        mine = lambda i: (0, i)
        theirs = lambda i: (0, 0, i)
        block = pl.BlockSpec((1, tc, R), lambda i: (layer, i, 0))
    else:
        tr, tc, steps = 32, C, R // 32
        mine = lambda i: (i, 0)
        theirs = lambda i: (0, i, 0)
        block = pl.BlockSpec((1, tr, C), lambda i: (layer, i, 0))

    def body(p_ref, o_ref, w_ref, m_ref, v_ref, *rest):
        g_out, d_out, m_out, v_out = rest[-4:]
        me = _my_index()
        g = jnp.zeros((tr, tc), F32)
        for j in range(N_DEV):
            g = g + jnp.where(me == j, o_ref[...], p_ref[j]).astype(F32)
        if transposed:
            g = g.T
        delta, mn, vn = _adamw(w_ref[0], g, m_ref[0], v_ref[0])
        g_out[0] = g
        d_out[0] = delta
        m_out[0] = mn
        v_out[0] = vn

    passed = [] if outs is None else [pl.BlockSpec(memory_space=pl.ANY)] * 4
    shp = jax.ShapeDtypeStruct(w.shape, F32)
    return pl.pallas_call(
        body, name=name, grid=(steps,), out_shape=(shp,) * 4,
        in_specs=[pl.BlockSpec((N_DEV, tr, tc), theirs), pl.BlockSpec((tr, tc), mine), block, block, block] + passed,
        out_specs=(block,) * 4, input_output_aliases={5 + q: q for q in range(len(passed))},
        compiler_params=_params(("parallel",), VMEM_LIMIT),
    )(parts, own, w, m, v, *(outs or ()))


def _adam_ada_w(cond_t, dmod, w, m, v):
    ncol = w.shape[-1]
    tr = 512

    def body(c_ref, d_ref, w_ref, m_ref, v_ref, g_out, d_out, m_out, v_out):
        g = _nn(c_ref[...], d_ref[0])
        delta, mn, vn = _adamw(w_ref[0], g, m_ref[0], v_ref[0])
        g_out[0] = g
        d_out[0] = delta
        m_out[0] = mn
        v_out[0] = vn

    blk = pl.BlockSpec((1, tr, ncol), lambda l, i: (l, i, 0))
    shp = jax.ShapeDtypeStruct(w.shape, F32)
    return pl.pallas_call(
        body, name="adam_ada_w", grid=(DEPTH, D // tr), out_shape=(shp,) * 4,
        in_specs=[pl.BlockSpec((tr, LANES), lambda l, i: (i, 0)), pl.BlockSpec((1, LANES, ncol), lambda l, i: (l, 0, 0)), blk, blk, blk],
        out_specs=(blk,) * 4, compiler_params=_params(("parallel", "parallel"), VMEM_LIMIT),
    )(cond_t, dmod, w, m, v)


TILE_ROWS = 168


def _stat_sources():
    pairs = []
    for i in range(DEPTH):
        b = 32 * i
        for q, src in enumerate((b, b + 1, b + 8, b + 16, b + 17, b + 24)):
            pairs.append((6 * i + q, src))
        pairs.append((24 + i, b + 2))
        pairs.append((32 + i, b + 18))
    pairs += [(40, 128), (41, 129)]
    return pairs


def _small_exchange(tiles, w, m, v):
    loss_row, sink_row, sink_src = 41, 48, 136

    def body(s_ref, w_ref, m_ref, v_ref, dmod_out, g_out, d_out, m_out, v_out, loss_out, all_ref, tot_ref, send_sems, recv_sems):
        me = _my_index()
        all_ref[me] = s_ref[...]
        copies = []
        for k in range(1, N_DEV):
            dev, _ = _peer(k)
            cp = pltpu.make_async_remote_copy(src_ref=s_ref, dst_ref=all_ref.at[me], send_sem=send_sems.at[k - 1],
                                              recv_sem=recv_sems.at[k - 1], device_id=dev, device_id_type=MESH)
            cp.start()
            copies.append(cp)
        for k in range(1, N_DEV):
            dev, pidx = _peer(k)
            pltpu.make_async_remote_copy(src_ref=s_ref, dst_ref=all_ref.at[pidx], send_sem=send_sems.at[k - 1],
                                         recv_sem=recv_sems.at[k - 1], device_id=dev, device_id_type=MESH).wait_recv()
        for cp in copies:
            cp.wait_send()
        tot = all_ref[0]
        for j in range(1, N_DEV):
            tot = tot + all_ref[j]
        tot_ref[...] = tot
        g_out[...] = jnp.zeros_like(g_out)
        for dst, src in _stat_sources():
            g_out[pl.ds(dst, 1), :] = tot_ref[pl.ds(src, 1), :]
            if dst < 24:
                for j in range(N_DEV):
                    dmod_out[j, pl.ds(dst, 1), :] = all_ref[j, pl.ds(src, 1), :]
        lane = lax.broadcasted_iota(jnp.int32, (1, D), 1)
        sink = jnp.zeros((1, D), F32)
        for h in range(32):
            sink = jnp.where(lane == h, tot_ref[pl.ds(sink_src + h, 1), :], sink)
        g_out[pl.ds(sink_row, 1), :] = sink
        g = g_out[...]
        delta, mn, vn = _adamw(w_ref[...], g, m_ref[...], v_ref[...])
        d_out[...] = delta
        m_out[...] = mn
        v_out[...] = vn
        loss = jnp.sum(g[loss_row:loss_row + 1, :], axis=-1, keepdims=True) * (0.5 / D)
        loss_out[...] = jnp.broadcast_to(loss, loss_out.shape)

    vm = pl.BlockSpec(memory_space=pltpu.VMEM)
    shp = jax.ShapeDtypeStruct((STAT_ROWS, D), F32)
    return pl.pallas_call(
        body, name="small_exchange",
        out_shape=(jax.ShapeDtypeStruct((N_DEV, 24, D), F32), shp, shp, shp, shp, jax.ShapeDtypeStruct((8, LANES), F32)),
        in_specs=[vm] * 4, out_specs=(vm,) * 6,
        scratch_shapes=[pltpu.VMEM((N_DEV, TILE_ROWS, D), F32), pltpu.VMEM((TILE_ROWS, D), F32),
                        pltpu.SemaphoreType.DMA((N_DEV - 1,)), pltpu.SemaphoreType.DMA((N_DEV - 1,))],
        compiler_params=_params(vmem=VMEM_LIMIT),
    )(tiles, w, m, v)


def _to_rows(name, a):
    if name in ("ffn_in", "a_in", "b_in"):
        return a.T
    if name == "b_out":
        return a.T.reshape(-1, D)
    return a


def _rows8(a):
    return jnp.pad(a, ((0, 8 - a.shape[0]), (0, 0)))


def _pack_small(ada_b, norm_mix, norm_ffn, final_norm, sink):
    sink_row = jnp.pad(sink.reshape(1, -1), ((0, 0), (0, D - sink.size)))
    return jnp.concatenate([ada_b.reshape(24, D), _rows8(norm_mix), _rows8(norm_ffn), _rows8(final_norm.reshape(1, D)),
                            _rows8(sink_row)], axis=0)


def _unpack_small(a):
    return a[0:24].reshape(4, 6 * D), a[24:28], a[32:36], a[40], a[48, :32].reshape(2, 16)


def kernel(x, c, ada_w, ada_b, norm_mix, norm_ffn, ffn_w_in, ffn_w_out, a_w_in, a_w_out, a_sink, b_w_in, b_w_out, final_norm, loss_target, m_ada_w, m_ada_b, m_norm_mix, m_norm_ffn, m_ffn_w_in, m_ffn_w_out, m_a_w_in, m_a_w_out, m_a_sink, m_b_w_in, m_b_w_out, m_final_norm, v_ada_w, v_ada_b, v_norm_mix, v_norm_ffn, v_ffn_w_in, v_ffn_w_out, v_a_w_in, v_a_w_out, v_a_sink, v_b_w_in, v_b_w_out, v_final_norm):
    S = x.shape[1]
    x0 = x.reshape(S, D)
    target = loss_target.reshape(S, D)
    me = _my_index()
    ncol = ada_w.shape[-1]

    ada_b_mine = lax.dynamic_slice_in_dim(ada_b, me * ncol, ncol, axis=1)
    cond_all, parts = _cond_exchange(jnp.broadcast_to(c.reshape(1, D), (8, D)), ada_w, ada_b_mine)
    mod = lax.dynamic_index_in_dim(parts, me, axis=2, keepdims=False)
    mod = jnp.transpose(mod, (1, 0, 2)).reshape(DEPTH, 6, 1, D)

    weights = {"ffn_in": ffn_w_in, "ffn_out": ffn_w_out, "a_in": a_w_in, "a_out": a_w_out, "b_in": b_w_in, "b_out": b_w_out}
    shard = {(n, l): _to_rows(n, weights[n][l]).astype(BF16) for n, l, _ in SEGMENTS}
    first = [sg for sg in _layer_segments(0) if not sg[0].startswith("ffn")]
    gathered0 = _all_gather_weights([shard[(n, l)] for n, l, _ in first])
    W = {(n, l): g for (n, l, _), g in zip(first, gathered0)}
    groups = [[sg for sg in _layer_segments(0) if sg[0].startswith("ffn")]] + [_layer_segments(i) for i in range(1, DEPTH)]
    gathers, order = [], gathered0[0]
    for q, segs in enumerate(groups):
        zones = [(N_DEV, rows, D) for _, _, rows in segs]
        gathers.append(_exchange_start([shard[(n, l)] for n, l, _ in segs], zones, [(s, 0) for s in range(len(segs))],
                                       [sg[2] for sg in segs], False, order, "weight_gather_start_%d" % q))
        order = gathers[-1][-1]
    gather_token = order[0:1, 0:1]

    def finish_gather(q, after):
        segs = groups[q]
        zones = _exchange_wait(gathers[q], len(segs), [(s, 0) for s in range(len(segs))], [sg[2] for sg in segs], after,
                               "weight_gather_wait_%d" % q)
        for (n, l, rows), zone in zip(segs, zones):
            full = lax.dynamic_update_slice(zone, shard[(n, l)][None], (me, 0, 0))
            W[(n, l)] = full.reshape(D, 512) if n == "b_out" else full.reshape(N_DEV * rows, D)

    a_slopes, b_slopes = _slopes(16), _slopes(24)
    bias_a = _alibi_bias(a_slopes, A_HALF, 1)
    bias_b = [_alibi_bias(b_slopes[8 * g:8 * g + 8], B_HALF, dil) for g, dil in enumerate(B_DILS)]
    bias_b_fwd = [_alibi_bias(b_slopes[8 * g:8 * g + 8], B_HALF, dil, max(CHUNK, TQ * dil), both=True) for g, dil in enumerate(B_DILS)]
    a_geom = dict(C=A_QKV, r=1, half=A_HALF, qoff=0, koff=1024, voff=1280, n_units=2)
    b_geom = [dict(C=B_QKV, r=dil, half=B_HALF, qoff=512 * g, koff=1536 + 128 * g, voff=1920 + 128 * g, n_units=1)
              for g, dil in enumerate(B_DILS)]

    saved = []
    xcur = x0
    for i in range(DEPTH):
        j = i // 2
        sh1, sc1, g1, sh2, sc2, g2 = [mod[i, q] for q in range(6)]
        nm, nf = norm_mix[i].reshape(1, D), norm_ffn[i].reshape(1, D)
        if i == 0:
            nm = nm + gather_token
        if i > 0:
            finish_gather(i, xcur)
        if i % 2 == 0:
            sink_rep = jnp.repeat(jnp.repeat(a_sink[j], TQ).reshape(2, 1, 8 * TQ), 8, axis=1).reshape(16, 8 * TQ)
            h1, qkv = _proj(xcur, nm, sc1, sh1, W[("a_in", j)], name="proj_a")
            o, lse = _attn_fwd(qkv, bias_a, sink_rep, out_dtype=BF16, name="attn_a_fwd", **a_geom)
            if i == 0:
                finish_gather(0, o)
            x1, y1, h2, gu, act = _out_ffn_in(o, W[("a_out", j)], xcur, g1, nf, sc2, sh2, W[("ffn_in", i)],
                                              w_is_transposed=False, name="out_a_ffn_in")
        else:
            sink_rep = None
            h1, qkv = _proj(xcur, nm, sc1, sh1, W[("b_in", j)], name="proj_b")
            outs = [_attn_fwd(qkv, bias_b_fwd[g], None, out_dtype=F32, name="attn_b%d_fwd" % g, **b_geom[g]) for g in range(3)]
            o, lse = _merge_groups([t[0] for t in outs], [t[1] for t in outs])
            x1, y1, h2, gu, act = _out_ffn_in(o, W[("b_out", j)], xcur, g1, nf, sc2, sh2, W[("ffn_in", i)],
                                              w_is_transposed=True, name="out_b_ffn_in")
        if i < DEPTH - 1:
            x2, y2 = _ffn_out(act, W[("ffn_out", i)], x1, g2, None, None, name="ffn_out")
        else:
            x2, y2, head_stats = _ffn_out(act, W[("ffn_out", i)], x1, g2, target, final_norm.reshape(1, D), name="ffn_out_loss")
        saved.append(dict(x0=xcur, h1=h1, qkv=qkv, o=o, lse=lse, y1=y1, x1=x1, h2=h2, gu=gu, act=act, y2=y2, sink=sink_rep))
        xcur = x2

    dx = xcur

    dW = {}
    stat_tiles, dsink = [None] * DEPTH, [None] * 2
    exchanges = []
    start_token = None

    def start_exchange(segs):
        own = [lax.dynamic_slice_in_dim(dW[(n, l)], me * rows, rows, axis=0) for n, l, rows in segs]
        zones = [(N_DEV, rows, D) for _, _, rows in segs]
        started = _exchange_start([dW[(n, l)] for n, l, _ in segs], zones, [(s, 0) for s in range(len(segs))],
                                  [sg[2] for sg in segs], True, own[0], "grad_exchange_start_%d" % len(exchanges))
        exchanges.append((segs, started, own))
        return started[-1][0:1, 0:1]

    for i in reversed(range(DEPTH)):
        j = i // 2
        sv = saved[i]
        sh1, sc1, g1, sh2, sc2, g2 = [mod[i, q] for q in range(6)]
        if start_token is not None:
            g2 = g2 + start_token
            start_token = None
        nm, nf = norm_mix[i].reshape(1, D), norm_ffn[i].reshape(1, D)
        mix = "a_out" if i % 2 == 0 else "b_out"
        dgu, dy2, dx1, do, dy1, st_g2, st_f, st_g1 = _ffn_bwd(
            dx, sv["y2"], g2, W[("ffn_out", i)], sv["gu"], W[("ffn_in", i)], sv["x1"], nf, sc2, sv["y1"], g1, W[(mix, j)],
            mix_is_transposed=(i % 2 == 1), name="ffn_bwd_" + mix)
        dW[("ffn_out", i)] = _weight_grad(dy2, sv["act"], transpose_out=True, name="dw_ffn_out")
        dW[("ffn_in", i)] = _weight_grad(sv["h2"], dgu, transpose_out=True, name="dw_ffn_in")
        sink_bwd = sv["sink"]
        if i == 0:
            sink_bwd = sink_bwd + start_exchange([sg for sg in _layer_segments(0) if sg[0].startswith("ffn")])
        if i % 2 == 0:
            dW[("a_out", j)] = _weight_grad(dy1, sv["o"], transpose_out=True, name="dw_a_out")
            dq, dk, dv, ds = _attn_bwd(sv["qkv"], bias_a, sink_bwd, sv["o"], do, sv["lse"], name="attn_a_bwd", **a_geom)
            dsink[j] = ds
            dqkv = jnp.concatenate([dq, dk[0].astype(BF16), dk[1].astype(BF16), dv[0].astype(BF16), dv[1].astype(BF16)], axis=1)
            dW[("a_in", j)] = _weight_grad(sv["h1"], dqkv, transpose_out=True, name="dw_a_in")
            dx0, st_m = _norm_bwd(dqkv, W[("a_in", j)], sv["x0"], dx1, nm, sc1, name="proj_a_bwd")
        else:
            dW[("b_out", j)] = _weight_grad(dy1, sv["o"], transpose_out=False, name="dw_b_out").reshape(N_DEV * 64, D)
            gr = [_attn_bwd(sv["qkv"], bias_b[g], None, sv["o"], do, sv["lse"], name="attn_b%d_bwd" % g, **b_geom[g]) for g in range(3)]
            dqkv = jnp.concatenate([t[0] for t in gr] + [t[1][0].astype(BF16) for t in gr] + [t[2][0].astype(BF16) for t in gr], axis=1)
            dW[("b_in", j)] = _weight_grad(sv["h1"], dqkv, transpose_out=True, name="dw_b_in")
            dx0, st_m = _norm_bwd(dqkv, W[("b_in", j)], sv["x0"], dx1, nm, sc1, name="proj_b_bwd")
        stat_tiles[i] = [st_m, st_g1, st_f, st_g2]
        if i > 0:
            start_token = start_exchange(_layer_segments(i))
        else:
            start_exchange([sg for sg in _layer_segments(0) if not sg[0].startswith("ffn")])
        dx = dx0
    grad_x = dx.reshape(1, S, D)

    masters = {"ffn_in": (ffn_w_in, m_ffn_w_in, v_ffn_w_in), "ffn_out": (ffn_w_out, m_ffn_w_out, v_ffn_w_out),
               "a_in": (a_w_in, m_a_w_in, v_a_w_in), "a_out": (a_w_out, m_a_w_out, v_a_w_out),
               "b_in": (b_w_in, m_b_w_in, v_b_w_in), "b_out": (b_w_out, m_b_w_out, v_b_w_out)}
    results = {n: None for n in masters}
    after = dx
    for e, (segs, started, own) in enumerate(exchanges):
        zones = _exchange_wait(started, len(segs), [(s, 0) for s in range(len(segs))], [sg[2] for sg in segs], after,
                               "grad_exchange_wait_%d" % e)
        for (n, l, rows), zone, mine in zip(segs, zones, own):
            if n == "b_out":
                zone, mine = zone.reshape(N_DEV, LANES, 512), mine.reshape(LANES, 512)
            results[n] = _adam_segment(zone, mine, *masters[n], results[n], l, transposed=n in ("ffn_in", "a_in", "b_in", "b_out"),
                                       name="adam_" + n)
            after = results[n][0]
    big = {(kind, n): results[n][q] for q, kind in enumerate(("grad", "delta", "m", "v")) for n in masters}

    tiles = jnp.concatenate([t for i in range(DEPTH) for t in stat_tiles[i]] + [head_stats]
                            + [jnp.pad(ds, ((0, 0), (0, D - LANES))) for ds in dsink], axis=0)
    small = [_pack_small(*t) for t in ((ada_b, norm_mix, norm_ffn, final_norm, a_sink),
                                       (m_ada_b, m_norm_mix, m_norm_ffn, m_final_norm, m_a_sink),
                                       (v_ada_b, v_norm_mix, v_norm_ffn, v_final_norm, v_a_sink))]
    dmod_all, sg, sd, sm, sv_, loss_tile = _small_exchange(tiles, *small)
    loss = loss_tile[0, 0]
    dmod_all = dmod_all.reshape(N_DEV, DEPTH, 6 * D)
    dmod_mine = lax.dynamic_slice_in_dim(dmod_all, me * ncol, ncol, axis=2)
    dmod_pad = jnp.pad(jnp.transpose(dmod_mine, (1, 0, 2)), ((0, 0), (0, LANES - N_DEV), (0, 0))).astype(BF16)
    cond_t = jnp.pad(cond_all.T, ((0, 0), (0, LANES - N_DEV))).astype(BF16)
    ada = _adam_ada_w(cond_t, dmod_pad, ada_w, m_ada_w, v_ada_w)

    outs = [loss, grad_x]
    small_res = [_unpack_small(t) for t in (sg, sd, sm, sv_)]
    for q, kind in enumerate(("grad", "delta", "m", "v")):
        ab, nm_, nf_, fn, sk = small_res[q]
        outs += [ada[q], ab, nm_, nf_, big[(kind, "ffn_in")], big[(kind, "ffn_out")], big[(kind, "a_in")], big[(kind, "a_out")],
                 sk, big[(kind, "b_in")], big[(kind, "b_out")], fn]
    return tuple(outs)
```

```python
import functools
import math

import numpy as np
import jax
import jax.numpy as jnp
from jax import lax
from jax.experimental import pallas as pl
from jax.experimental.pallas import tpu as pltpu

D = 1024
HEAD_DIM = 64
D_FF = 2816
DEPTH = 4
N_DEV = 8
A_QKV = 1536
B_QKV = 2304
A_HALF = 128
B_HALF = 64
B_DILS = (1, 4, 16)
RMS_EPS = 1e-6
NEG = -1e30
ADAM_LR = 0.001
ADAM_B1 = 0.9
ADAM_B2 = 0.999
ADAM_EPS = 1e-08
ADAM_WD = 0.01
ADAM_STEP = 10

LANES = 128
SPLIT = 2
TQ = 128
VMEM_LIMIT = 56 * 1024 * 1024
MESH = pl.DeviceIdType.MESH
F32 = jnp.float32
BF16 = jnp.bfloat16

SEGMENTS = ([("ffn_in", l, 704) for l in range(4)] + [("ffn_out", l, 352) for l in range(4)]
            + [("a_in", j, 192) for j in range(2)] + [("a_out", j, 128) for j in range(2)]
            + [("b_in", j, 288) for j in range(2)] + [("b_out", j, 64) for j in range(2)])
def _layer_segments(i):
    mixer = "a" if i % 2 == 0 else "b"
    return [s for s in SEGMENTS if (s[0].startswith("ffn") and s[1] == i) or (s[0].startswith(mixer + "_") and s[1] == i // 2)]


def _offsets(segs):
    rows = [s[2] for s in segs]
    return [sum(rows[:k]) for k in range(len(rows))], sum(rows)
STAT_ROWS = 56


def _nn(a, b):
    return jnp.dot(a, b, preferred_element_type=F32)


def _nt(a, b):
    return lax.dot_general(a, b, (((1,), (1,)), ((), ())), preferred_element_type=F32)


def _tn(a, b):
    return lax.dot_general(a, b, (((0,), (0,)), ((), ())), preferred_element_type=F32)


def _params(dims=None, vmem=None):
    kw = {}
    if dims is not None:
        kw["dimension_semantics"] = dims
    if vmem is not None:
        kw["vmem_limit_bytes"] = vmem
    return pltpu.CompilerParams(**kw)


def _my_index():
    return 4 * lax.axis_index("x") + 2 * lax.axis_index("y") + lax.axis_index("c")


def _peer(k):
    x, y, c = lax.axis_index("x"), lax.axis_index("y"), lax.axis_index("c")
    px, py, pc = x ^ ((k >> 2) & 1), y ^ ((k >> 1) & 1), c ^ (k & 1)
    return (px, py, pc), 4 * px + 2 * py + pc


def _const_spec(shape):
    nd = len(shape)
    return pl.BlockSpec(shape, lambda *_: (0,) * nd)


def _cond_exchange(c_tile, ada_w, ada_b_mine):
    ncol = ada_w.shape[-1]

    def body(c_ref, w_ref, b_ref, cond_ref, parts_ref, call_ref, mine_ref, send_sems, recv_sems):
        me = _my_index()
        call_ref[me] = c_ref[...]
        copies = []
        for k in range(1, N_DEV):
            dev, _ = _peer(k)
            cp = pltpu.make_async_remote_copy(src_ref=c_ref, dst_ref=call_ref.at[me], send_sem=send_sems.at[0, k - 1],
                                              recv_sem=recv_sems.at[0, k - 1], device_id=dev, device_id_type=MESH)
            cp.start()
            copies.append(cp)
        for k in range(1, N_DEV):
            _, pidx = _peer(k)
            pltpu.make_async_remote_copy(src_ref=c_ref, dst_ref=call_ref.at[pidx], send_sem=send_sems.at[0, k - 1],
                                         recv_sem=recv_sems.at[0, k - 1], device_id=_peer(k)[0], device_id_type=MESH).wait_recv()
        for cp in copies:
            cp.wait_send()
        row = lax.broadcasted_iota(jnp.int32, (N_DEV, D), 0)
        cmat = jnp.zeros((N_DEV, D), F32)
        for j in range(N_DEV):
            cmat = jnp.where(row == j, call_ref[j], cmat)
        cond = cmat * jax.nn.sigmoid(cmat)
        cond_ref[...] = cond
        cb = cond.astype(BF16)
        for l in range(DEPTH):
            mine_ref[l] = _nn(cb, w_ref[l].astype(BF16)) + b_ref[pl.ds(l, 1), :]
        parts_ref[me] = mine_ref[...]
        copies = []
        for k in range(1, N_DEV):
            dev, _ = _peer(k)
            cp = pltpu.make_async_remote_copy(src_ref=mine_ref, dst_ref=parts_ref.at[me], send_sem=send_sems.at[1, k - 1],
                                              recv_sem=recv_sems.at[1, k - 1], device_id=dev, device_id_type=MESH)
            cp.start()
            copies.append(cp)
        for k in range(1, N_DEV):
            dev, pidx = _peer(k)
            pltpu.make_async_remote_copy(src_ref=mine_ref, dst_ref=parts_ref.at[pidx], send_sem=send_sems.at[1, k - 1],
                                         recv_sem=recv_sems.at[1, k - 1], device_id=dev, device_id_type=MESH).wait_recv()
        for cp in copies:
            cp.wait_send()

    vm = pl.BlockSpec(memory_space=pltpu.VMEM)
    return pl.pallas_call(
        body, name="cond_exchange",
        out_shape=(jax.ShapeDtypeStruct((N_DEV, D), F32), jax.ShapeDtypeStruct((N_DEV, DEPTH, N_DEV, ncol), F32)),
        in_specs=[vm, vm, vm], out_specs=(vm, vm),
        scratch_shapes=[pltpu.VMEM((N_DEV, N_DEV, D), F32), pltpu.VMEM((DEPTH, N_DEV, ncol), F32),
                        pltpu.SemaphoreType.DMA((2, N_DEV - 1)), pltpu.SemaphoreType.DMA((2, N_DEV - 1))],
        compiler_params=_params(vmem=VMEM_LIMIT),
    )(c_tile, ada_w, ada_b_mine)[:2]


def _all_gather_weights(shards):
    n = len(shards)
    big = max(range(n), key=lambda s: shards[s].shape[0])
    total = sum(sh.shape[0] for sh in shards)
    assert N_DEV * shards[big].shape[0] >= total

    def body(*refs):
        ins, outs = refs[:n], refs[n:2 * n]
        local_sems, send_sems, recv_sems = refs[2 * n:]
        me = _my_index()
        local = []
        for s in range(n):
            rows = ins[s].shape[0]
            cp = pltpu.make_async_copy(ins[s], outs[s].at[pl.ds(me * rows, rows)], local_sems.at[s])
            cp.start()
            local.append(cp)
        for k in range(1, N_DEV):
            dev, _ = _peer(k)
            for s in range(n):
                rows = ins[s].shape[0]
                pltpu.make_async_remote_copy(src_ref=ins[s], dst_ref=outs[s].at[pl.ds(me * rows, rows)],
                                             send_sem=send_sems.at[k - 1], recv_sem=recv_sems.at[k - 1],
                                             device_id=dev, device_id_type=MESH).start()
        whole = outs[big].at[pl.ds(0, total)]
        for k in range(1, N_DEV):
            dev, _ = _peer(k)
            w = pltpu.make_async_remote_copy(src_ref=whole, dst_ref=whole, send_sem=send_sems.at[k - 1],
                                             recv_sem=recv_sems.at[k - 1], device_id=dev, device_id_type=MESH)
            w.wait_send()
            w.wait_recv()
        for cp in local:
            cp.wait()

    hbm = pl.BlockSpec(memory_space=pl.ANY)
    return pl.pallas_call(
        body, name="weight_all_gather",
        out_shape=tuple(jax.ShapeDtypeStruct((N_DEV * s.shape[0], D), s.dtype) for s in shards),
        in_specs=[hbm] * n, out_specs=tuple([hbm] * n),
        scratch_shapes=[pltpu.SemaphoreType.DMA((n,)), pltpu.SemaphoreType.DMA((N_DEV - 1,)),
                        pltpu.SemaphoreType.DMA((N_DEV - 1,))],
    )(*shards)


HBM = pl.BlockSpec(memory_space=pltpu.HBM)
SEM = pl.BlockSpec(memory_space=pltpu.SEMAPHORE)
EFFECT = pltpu.SideEffectType.DATAFLOW_SIDE_EFFECTING


def _exchange_start(srcs, landings, dst, rows, to_peer_rows, after, name):
    n, nl = len(srcs), len(landings)

    def body(*refs):
        src_refs = refs[:n]
        send_sems, recv_sems = refs[n + 1], refs[n + 2]
        land_refs = refs[2 * n + 3:2 * n + 3 + nl]
        token = refs[-1]
        me = _my_index()
        for k in range(1, N_DEV):
            dev, pidx = _peer(k)
            for q in range(n):
                src = src_refs[q].at[pl.ds(pidx * rows[q], rows[q])] if to_peer_rows else src_refs[q]
                pltpu.make_async_remote_copy(src_ref=src, dst_ref=land_refs[dst[q][0]].at[me, pl.ds(dst[q][1], rows[q])],
                                             send_sem=send_sems.at[k * n + q], recv_sem=recv_sems.at[k * n + q],
                                             device_id=dev, device_id_type=MESH).start()
        if not to_peer_rows:
            for q in range(n):
                pltpu.make_async_copy(src_refs[q], land_refs[dst[q][0]].at[me, pl.ds(dst[q][1], rows[q])], send_sems.at[q]).start()
        token[...] = jnp.zeros_like(token)

    sems = pltpu.SemaphoreType.DMA((N_DEV * n,))
    return pl.pallas_call(
        body, name=name,
        out_shape=(sems, sems, *[pltpu.HBM(a.shape, a.dtype) for a in srcs], *[pltpu.HBM(shape, BF16) for shape in landings],
                   jax.ShapeDtypeStruct((8, LANES), F32)),
        in_specs=[HBM] * n + [pl.BlockSpec(memory_space=pl.ANY)],
        out_specs=(SEM, SEM, *[HBM] * (n + nl), pl.BlockSpec(memory_space=pltpu.VMEM)),
        input_output_aliases={q: 2 + q for q in range(n)},
        compiler_params=pltpu.CompilerParams(has_side_effects=EFFECT),
    )(*[pltpu.with_memory_space_constraint(a, pltpu.HBM) for a in srcs], after)


def _exchange_wait(started, n, dst, rows, own_slot, after, name):
    send_sems, recv_sems = started[0], started[1]
    arrays = list(started[2:-1])
    n1 = len(arrays)

    def body(*refs):
        land_refs = refs[n:n1]
        sends, recvs = refs[n1], refs[n1 + 1]
        for k in range(1, N_DEV):
            dev, _ = _peer(k)
            for q in range(n):
                slot = land_refs[dst[q][0]].at[0, pl.ds(dst[q][1], rows[q])]
                w = pltpu.make_async_remote_copy(src_ref=slot, dst_ref=slot, send_sem=sends.at[k * n + q],
                                                 recv_sem=recvs.at[k * n + q], device_id=dev, device_id_type=MESH)
                w.wait_send()
                w.wait_recv()
        if own_slot:
            for q in range(n):
                slot = land_refs[dst[q][0]].at[0, pl.ds(dst[q][1], rows[q])]
                pltpu.make_async_copy(slot, slot, sends.at[q]).wait()

    return pl.pallas_call(
        body, name=name, out_shape=tuple(pltpu.HBM(a.shape, a.dtype) for a in arrays),
        in_specs=[HBM] * n1 + [SEM, SEM, pl.BlockSpec(memory_space=pl.ANY)], out_specs=tuple([HBM] * n1),
        input_output_aliases={q: q for q in range(n1)},
        compiler_params=pltpu.CompilerParams(has_side_effects=EFFECT),
    )(*arrays, send_sems, recv_sems, after)[n:]


def _norm_mod(x, nw, sc, sh):
    ms = jnp.mean(x * x, axis=-1, keepdims=True)
    xh = x * lax.rsqrt(ms + RMS_EPS)
    return xh, (xh * nw) * (1.0 + sc) + sh


def _proj(x, nw, sc, sh, wt, *, name):
    S, N = x.shape[0], wt.shape[0]
    tm = 512

    def body(x_ref, nw_ref, sc_ref, sh_ref, w_ref, h_ref, out_ref):
        for half in range(SPLIT):
            rows = pl.ds(half * (tm // SPLIT), tm // SPLIT)
            _, h = _norm_mod(x_ref[rows, :], nw_ref[...], sc_ref[...], sh_ref[...])
            hb = h.astype(BF16)
            h_ref[rows, :] = hb
            out_ref[rows, :] = _nt(hb, w_ref[...]).astype(BF16)

    row = lambda w: pl.BlockSpec((tm, w), lambda i: (i, 0))
    vec = _const_spec((1, D))
    return pl.pallas_call(
        body, name=name, grid=(S // tm,), out_shape=(jax.ShapeDtypeStruct((S, D), BF16), jax.ShapeDtypeStruct((S, N), BF16)),
        in_specs=[row(D), vec, vec, vec, _const_spec((N, D))], out_specs=(row(D), row(N)),
        compiler_params=_params(("parallel",), VMEM_LIMIT),
    )(x, nw, sc, sh, wt)


def _ffn_out(a, w, x, g, target, fnw, *, name):
    S, K = a.shape
    tm = 512
    last = target is not None

    def body(a_ref, w_ref, x_ref, g_ref, *rest):
        y = _nn(a_ref[...], w_ref[...])
        xv = x_ref[...] + g_ref[...] * y
        if not last:
            xo_ref, y_ref = rest
            y_ref[...] = y.astype(BF16)
            xo_ref[...] = xv
            return
        t_ref, fw_ref, dx_ref, y_ref, st_ref = rest
        y_ref[...] = y.astype(BF16)

        @pl.when(pl.program_id(0) == 0)
        def _():
            st_ref[...] = jnp.zeros_like(st_ref)

        rstd = lax.rsqrt(jnp.mean(xv * xv, axis=-1, keepdims=True) + RMS_EPS)
        xh = xv * rstd
        err = xh * fw_ref[...] - t_ref[...]
        dy = err * (1.0 / D)
        dxh = dy * fw_ref[...]
        dx_ref[...] = rstd * (dxh - xh * jnp.mean(dxh * xh, axis=-1, keepdims=True))
        st_ref[pl.ds(0, 1), :] = st_ref[pl.ds(0, 1), :] + jnp.sum(dy * xh, axis=0, keepdims=True)
        st_ref[pl.ds(1, 1), :] = st_ref[pl.ds(1, 1), :] + jnp.sum(err * err, axis=0, keepdims=True)

    row = lambda w_: pl.BlockSpec((tm, w_), lambda i: (i, 0))
    in_specs = [row(K), _const_spec(w.shape), row(D), _const_spec((1, D))]
    args = [a, w, x, g]
    out_shape = [jax.ShapeDtypeStruct((S, D), F32), jax.ShapeDtypeStruct((S, D), BF16)]
    out_specs = [row(D), row(D)]
    if last:
        in_specs += [row(D), _const_spec((1, D))]
        args += [target, fnw]
        out_shape.append(jax.ShapeDtypeStruct((8, D), F32))
        out_specs.append(_const_spec((8, D)))
    return pl.pallas_call(
        body, name=name, grid=(S // tm,), out_shape=tuple(out_shape), in_specs=in_specs, out_specs=tuple(out_specs),
        compiler_params=_params(("arbitrary",) if last else ("parallel",), VMEM_LIMIT),
    )(*args)


CHUNK = 1024


def _tile_rows(r, chunk=CHUNK):
    return min(TQ, chunk // r)


def _out_ffn_in(a, w_mix, x, g, nw, sc, sh, wt, *, w_is_transposed, name):
    S, K = a.shape
    tm = 256

    def body(a_ref, wm_ref, x_ref, g_ref, nw_ref, sc_ref, sh_ref, w_ref, x1_ref, y_ref, h_ref, gu_ref, act_ref):
        y = _nt(a_ref[...], wm_ref[...]) if w_is_transposed else _nn(a_ref[...], wm_ref[...])
        y_ref[...] = y.astype(BF16)
        x1 = x_ref[...] + g_ref[...] * y
        x1_ref[...] = x1
        _, h = _norm_mod(x1, nw_ref[...], sc_ref[...], sh_ref[...])
        hb = h.astype(BF16)
        h_ref[...] = hb
        gate = _nt(hb, w_ref[pl.ds(0, D_FF), :])
        up = _nt(hb, w_ref[pl.ds(D_FF, D_FF), :])
        sig = jax.nn.sigmoid(gate)
        silu = gate * sig
        gu_ref[:, pl.ds(0, D_FF)] = (up * (sig * (1.0 + gate * (1.0 - sig)))).astype(BF16)
        gu_ref[:, pl.ds(D_FF, D_FF)] = silu.astype(BF16)
        act_ref[...] = (silu * up).astype(BF16)

    row = lambda w_: pl.BlockSpec((tm, w_), lambda i: (i, 0))
    vec = _const_spec((1, D))
    return pl.pallas_call(
        body, name=name, grid=(S // tm,),
        out_shape=(jax.ShapeDtypeStruct((S, D), F32), jax.ShapeDtypeStruct((S, D), BF16), jax.ShapeDtypeStruct((S, D), BF16),
                   jax.ShapeDtypeStruct((S, 2 * D_FF), BF16), jax.ShapeDtypeStruct((S, D_FF), BF16)),
        in_specs=[row(K), _const_spec(w_mix.shape), row(D), vec, vec, vec, vec, _const_spec(wt.shape)],
        out_specs=(row(D), row(D), row(D), row(2 * D_FF), row(D_FF)),
        compiler_params=_params(("parallel",), VMEM_LIMIT),
    )(a, w_mix, x, g, nw, sc, sh, wt)


def _alibi_bias(slopes, half, dil, chunk=CHUNK, both=False):
    tq = _tile_rows(dil, chunk)
    tk = tq + 2 * half
    rel = np.arange(tk)[:, None] - half - np.arange(tq)[None, :]
    band = np.abs(rel) <= half
    dist = (dil * np.abs(rel)).astype(np.float32)
    tabs = [np.where(band, -np.float32(s) * dist, np.float32(NEG)).astype(np.float32) for s in slopes]
    out = []
    for u in range(0, len(tabs), 8):
        tab = np.concatenate(tabs[u:u + 8], axis=1)
        first, last = tab.copy(), tab.copy()
        first[:half] = NEG
        last[tk - half:] = NEG
        out += [tab, first, last]
        if both:
            last = last.copy()
            last[:half] = NEG
            out.append(last)
    return jnp.asarray(np.concatenate(out, axis=0))


def _slopes(n):
    return (2.0 ** (-8.0 * np.arange(1, n + 1) / n)).astype(np.float32)


def _head_masks(tq):
    lane = lax.broadcasted_iota(jnp.int32, (tq, LANES), 1)
    lo = lane < HEAD_DIM
    return lo, jnp.logical_not(lo)


def _stack_heads(tiles, lo, hi, scale):
    blocks = []
    for t in range(4):
        xf = tiles[t] if scale == 1.0 else tiles[t] * scale
        for a in range(2):
            xm = jnp.where(lo if a == 0 else hi, xf, 0.0)
            if a != t // 2:
                xm = pltpu.roll(xm, HEAD_DIM, 1)
            blocks.append(xm.astype(BF16))
    return jnp.concatenate(blocks, axis=0)


def _tile_from_columns(x8t, t, tq):
    r0 = HEAD_DIM * (t // 2)
    top = x8t[r0:r0 + HEAD_DIM, 2 * t * tq:(2 * t + 1) * tq]
    bot = x8t[r0:r0 + HEAD_DIM, (2 * t + 1) * tq:(2 * t + 2) * tq]
    return jnp.concatenate([top, bot], axis=0).T


def _attn_layout(S, C, r, half, qoff, koff, voff, chunk):
    hb = half * r
    per = chunk // hb
    nhb = S // hb
    main = lambda off: pl.BlockSpec((chunk, LANES), lambda u, i: (i, off // LANES + u))
    prev = lambda off: pl.BlockSpec((hb, LANES), lambda u, i: (jnp.maximum(i * per - 1, 0), off // LANES + u))
    nxt = lambda off: pl.BlockSpec((hb, LANES), lambda u, i: (jnp.minimum((i + 1) * per, nhb - 1), off // LANES + u))
    specs = [pl.BlockSpec((chunk, 4 * LANES), lambda u, i: (i, qoff // (4 * LANES) + u))]
    specs += [prev(koff), main(koff), nxt(koff), prev(voff), main(voff), nxt(voff)]
    return specs, hb


def _stage(dst, srcs):
    row = 0
    for src in srcs:
        n = src.shape[0]
        dst[pl.ds(row, n), :] = src[...].astype(F32)
        row += n


def _rows(start, n, r):
    return pl.ds(start, n, stride=r) if r > 1 else pl.ds(start, n)


def _attn_fwd(qkv, bias, sink, *, C, r, half, qoff, koff, voff, n_units, out_dtype, name):
    S = qkv.shape[0]
    chunk = max(CHUNK, TQ * r)
    tq = _tile_rows(r, chunk)
    tk = tq + 2 * half
    tiles = chunk // (r * tq)
    nsteps = S // chunk
    specs, hb = _attn_layout(S, C, r, half, qoff, koff, voff, chunk)
    use_sink = sink is not None

    def body(*refs):
        q_ref, kp, km, kn, vp, vm, vn, bias_ref = refs[:8]
        rest = list(refs[8:])
        sink_ref = rest.pop(0) if use_sink else None
        o_ref, lse_ref, qs, ks, vs, os_, ls = rest
        i = pl.program_id(1)
        if r > 1:
            for t in range(4):
                qs[t] = q_ref[:, pl.ds(t * LANES, LANES)].astype(F32)
        _stage(ks, [kp, km, kn])
        _stage(vs, [vp, vm, vn])
        lo, hi = _head_masks(tq)

        def tile_in(staged, ref, t, start):
            if r > 1:
                return staged[t, _rows(start, tq, r), :]
            return ref[pl.ds(start, tq), pl.ds(t * LANES, LANES)].astype(F32)

        ones = jnp.ones((16, tk), BF16)
        if use_sink:
            sk = sink_ref[pl.ds(0, 1), :]

        def chain(n, carry):
            rho, c = n // tiles, n % tiles
            start = c * (tq * r) + rho
            if r == 1:
                start = pl.multiple_of(start, tq)
            variant = jnp.where(jnp.logical_and(i == 0, c == 0), 1, 0) + jnp.where(
                jnp.logical_and(i == nsteps - 1, c == tiles - 1), 2, 0)
            k2 = ks[_rows(start, tk, r), :].astype(BF16)
            v2t = jnp.concatenate([vs[_rows(start, tk, r), :].T.astype(BF16), ones], axis=0)
            q8 = _stack_heads([tile_in(qs, q_ref, t, start) for t in range(4)], lo, hi, HEAD_DIM ** -0.5)
            s = _nt(k2, q8) + bias_ref[pl.ds(pl.multiple_of(variant * tk, 8), tk), :]
            m = jnp.max(s, axis=0, keepdims=True)
            if use_sink:
                m = jnp.maximum(m, sk)
            pv = _nn(v2t, jnp.exp(s - m).astype(BF16))
            l = pv[LANES:LANES + 1]
            if use_sink:
                l = l + jnp.exp(sk - m)
            o8t = pv[:LANES] / l
            lse8 = jnp.broadcast_to(m + jnp.log(l), (LANES, 8 * tq))
            for t in range(4):
                if r > 1:
                    os_[t, _rows(start, tq, r), :] = _tile_from_columns(o8t, t, tq)
                    ls[t, _rows(start, tq, r), :] = _tile_from_columns(lse8, t, tq)
                else:
                    o_ref[pl.ds(start, tq), pl.ds(t * LANES, LANES)] = _tile_from_columns(o8t, t, tq).astype(out_dtype)
                    lse_ref[pl.ds(start, tq), pl.ds(t * LANES, LANES)] = _tile_from_columns(lse8, t, tq)
            return carry

        lax.fori_loop(0, r * tiles, chain, 0, unroll=4)
        if r > 1:
            for t in range(4):
                o_ref[:, pl.ds(t * LANES, LANES)] = os_[t].astype(out_dtype)
                lse_ref[:, pl.ds(t * LANES, LANES)] = ls[t]

    in_specs = specs + [pl.BlockSpec((bias.shape[0] // n_units, 8 * tq), lambda u, i: (u, 0))]
    args = [qkv] * 7 + [bias]
    if use_sink:
        in_specs.append(pl.BlockSpec((8, 8 * tq), lambda u, i: (u, 0)))
        args.append(sink)
    wide = pl.BlockSpec((chunk, 4 * LANES), lambda u, i: (i, u))
    win = hb + chunk + hb
    big = lambda: pltpu.VMEM((4, chunk if r > 1 else 8, LANES), F32)
    return pl.pallas_call(
        body, name=name, grid=(n_units, nsteps),
        out_shape=(jax.ShapeDtypeStruct((S, n_units * 512), out_dtype), jax.ShapeDtypeStruct((S, n_units * 512), F32)),
        in_specs=in_specs, out_specs=(wide, wide),
        scratch_shapes=[big(), pltpu.VMEM((win, LANES), F32), pltpu.VMEM((win, LANES), F32), big(), big()],
        compiler_params=_params(("parallel", "parallel"), VMEM_LIMIT),
    )(*args)


def _attn_bwd(qkv, bias, sink, o, do, lse, *, C, r, half, qoff, koff, voff, n_units, name):
    S = qkv.shape[0]
    tq = _tile_rows(r)
    tk = tq + 2 * half
    tiles = CHUNK // (r * tq)
    nsteps = S // CHUNK
    specs, hb = _attn_layout(S, C, r, half, qoff, koff, voff, CHUNK)
    use_sink = sink is not None

    def body(*refs):
        q_ref, kp, km, kn, vp, vm, vn, bias_ref = refs[:8]
        rest = list(refs[8:])
        sink_ref = rest.pop(0) if use_sink else None
        o_ref, do_ref, lse_ref, dq_ref, dk_hbm, dv_hbm = rest[:6]
        rest = rest[6:]
        dsink_ref = rest.pop(0) if use_sink else None
        qs, ks, vs, os_, dos, ls, dqs, acck, accv, sem = rest
        u, i = pl.program_id(0), pl.program_id(1)

        @pl.when(i == 0)
        def _():
            acck[...] = jnp.zeros_like(acck)
            accv[...] = jnp.zeros_like(accv)
            if use_sink:
                dsink_ref[...] = jnp.zeros_like(dsink_ref)

        if r > 1:
            for t in range(4):
                cols = pl.ds(t * LANES, LANES)
                qs[t] = q_ref[:, cols].astype(F32)
                os_[t] = o_ref[:, cols].astype(F32)
                dos[t] = do_ref[:, cols].astype(F32)
                ls[t] = lse_ref[:, cols]
        _stage(ks, [kp, km, kn])
        _stage(vs, [vp, vm, vn])
        lo, hi = _head_masks(tq)

        def tile_in(staged, ref, t, start):
            if r > 1:
                return staged[t, _rows(start, tq, r), :]
            return ref[pl.ds(start, tq), pl.ds(t * LANES, LANES)].astype(F32)

        base = pl.multiple_of(i * CHUNK, CHUNK)
        if use_sink:
            sk = sink_ref[pl.ds(0, 1), :]

        def chain(n, carry):
            rho, c = n // tiles, n % tiles
            start = c * (tq * r) + rho
            if r == 1:
                start = pl.multiple_of(start, tq)
            variant = jnp.where(jnp.logical_and(i == 0, c == 0), 1, 0) + jnp.where(
                jnp.logical_and(i == nsteps - 1, c == tiles - 1), 2, 0)
            k2 = ks[_rows(start, tk, r), :].astype(BF16)
            v2 = vs[_rows(start, tk, r), :].astype(BF16)
            k2t = ks[_rows(start, tk, r), :].T.astype(BF16)
            q8 = _stack_heads([tile_in(qs, q_ref, t, start) for t in range(4)], lo, hi, HEAD_DIM ** -0.5)
            do_tiles = [tile_in(dos, do_ref, t, start) for t in range(4)]
            do8 = _stack_heads(do_tiles, lo, hi, 1.0)
            deltas, lses = [], []
            for t in range(4):
                prod_t = (do_tiles[t] * tile_in(os_, o_ref, t, start)).T
                lse_t = tile_in(ls, lse_ref, t, start).T
                for a in range(2):
                    deltas.append(jnp.sum(prod_t[a * HEAD_DIM:(a + 1) * HEAD_DIM], axis=0, keepdims=True))
                    lses.append(lse_t[a * HEAD_DIM:a * HEAD_DIM + 1])
            delta8 = jnp.concatenate(deltas, axis=1)
            lse8 = jnp.concatenate(lses, axis=1)
            s = _nt(k2, q8) + bias_ref[pl.ds(pl.multiple_of(variant * tk, 8), tk), :]
            p = jnp.exp(s - lse8)
            dp = _nt(v2, do8)
            dsb = (p * (dp - delta8)).astype(BF16)
            dq8t = _nn(k2t, dsb)
            for t in range(4):
                dq_t = _tile_from_columns(dq8t, t, tq) * (HEAD_DIM ** -0.5)
                if r > 1:
                    dqs[t, _rows(start, tq, r), :] = dq_t
                else:
                    dq_ref[pl.ds(start, tq), pl.ds(t * LANES, LANES)] = dq_t.astype(BF16)
            arow = base + start
            if r == 1:
                arow = pl.multiple_of(arow, tq)
            acck[_rows(arow, tk, r), :] = acck[_rows(arow, tk, r), :] + _nn(dsb, q8)
            accv[_rows(arow, tk, r), :] = accv[_rows(arow, tk, r), :] + _nn(p.astype(BF16), do8)
            if use_sink:
                e = jnp.exp(sk - lse8) * delta8
                for h in range(8):
                    part = -jnp.sum(e[:, h * tq:(h + 1) * tq], axis=1, keepdims=True)
                    dsink_ref[pl.ds(h, 1), :] = dsink_ref[pl.ds(h, 1), :] + part
            return carry

        lax.fori_loop(0, r * tiles, chain, 0, unroll=2)
        if r > 1:
            for t in range(4):
                dq_ref[:, pl.ds(t * LANES, LANES)] = dqs[t].astype(BF16)

        @pl.when(i == nsteps - 1)
        def _():
            ck = pltpu.make_async_copy(acck.at[pl.ds(hb, S)], dk_hbm.at[u], sem.at[0])
            cv = pltpu.make_async_copy(accv.at[pl.ds(hb, S)], dv_hbm.at[u], sem.at[1])
            ck.start()
            cv.start()
            ck.wait()
            cv.wait()

    wide = pl.BlockSpec((CHUNK, 4 * LANES), lambda u, i: (i, u))
    hbm = pl.BlockSpec(memory_space=pl.ANY)
    in_specs = specs + [pl.BlockSpec((3 * tk, 8 * tq), lambda u, i: (u, 0))]
    args = [qkv] * 7 + [bias]
    if use_sink:
        in_specs.append(pl.BlockSpec((8, 8 * tq), lambda u, i: (u, 0)))
        args.append(sink)
    in_specs += [wide, wide, wide]
    args += [o, do, lse]
    out_shape = [jax.ShapeDtypeStruct((S, n_units * 512), BF16), jax.ShapeDtypeStruct((n_units, S, LANES), F32),
                 jax.ShapeDtypeStruct((n_units, S, LANES), F32)]
    out_specs = [wide, hbm, hbm]
    if use_sink:
        out_shape.append(jax.ShapeDtypeStruct((n_units * 8, LANES), F32))
        out_specs.append(pl.BlockSpec((8, LANES), lambda u, i: (u, 0)))
    win = hb + CHUNK + hb
    big = lambda: pltpu.VMEM((4, CHUNK if r > 1 else 8, LANES), F32)
    res = pl.pallas_call(
        body, name=name, grid=(n_units, nsteps), out_shape=tuple(out_shape), in_specs=in_specs, out_specs=tuple(out_specs),
        scratch_shapes=[big(), pltpu.VMEM((win, LANES), F32), pltpu.VMEM((win, LANES), F32), big(), big(), big(), big(),
                        pltpu.VMEM((S + 2 * hb, LANES), F32), pltpu.VMEM((S + 2 * hb, LANES), F32), pltpu.SemaphoreType.DMA((2,))],
        compiler_params=_params(("arbitrary", "arbitrary"), VMEM_LIMIT),
    )(*args)
    return res[0], res[1], res[2], (res[3] if use_sink else None)


def _merge_groups(os_, lses):
    S, W = os_[0].shape
    tm = 512

    def body(o0, o1, o2, l0, l1, l2, o_ref, lse_ref):
        ls = [l0[...], l1[...], l2[...]]
        mx = jnp.maximum(jnp.maximum(ls[0], ls[1]), ls[2])
        es = [jnp.exp(l - mx) for l in ls]
        den = es[0] + es[1] + es[2]
        o = (es[0] / den) * o0[...] + (es[1] / den) * o1[...] + (es[2] / den) * o2[...]
        o_ref[...] = o.astype(BF16)
        lse_ref[...] = mx + jnp.log(den)

    row = pl.BlockSpec((tm, W), lambda i: (i, 0))
    return pl.pallas_call(
        body, name="merge_groups", grid=(S // tm,),
        out_shape=(jax.ShapeDtypeStruct((S, W), BF16), jax.ShapeDtypeStruct((S, W), F32)),
        in_specs=[row] * 6, out_specs=(row, row), compiler_params=_params(("parallel",), VMEM_LIMIT),
    )(*os_, *lses)


def _norm_bwd(dy, wt, x, dres, nw, sc, *, name):
    S, N = dy.shape
    tm = 512

    def body(dy_ref, w_ref, x_ref, dres_ref, nw_ref, sc_ref, dx_ref, st_ref):
        @pl.when(pl.program_id(0) == 0)
        def _():
            st_ref[...] = jnp.zeros_like(st_ref)

        nwv, scale = nw_ref[...], 1.0 + sc_ref[...]
        sums = [jnp.zeros((1, D), F32)] * 3
        for half in range(SPLIT):
            rows = pl.ds(half * (tm // SPLIT), tm // SPLIT)
            dh = _nn(dy_ref[rows, :], w_ref[...])
            xv = x_ref[rows, :]
            rstd = lax.rsqrt(jnp.mean(xv * xv, axis=-1, keepdims=True) + RMS_EPS)
            xh = xv * rstd
            dxh = dh * (nwv * scale)
            dx_ref[rows, :] = dres_ref[rows, :] + rstd * (dxh - xh * jnp.mean(dxh * xh, axis=-1, keepdims=True))
            dhx = dh * xh
            sums = [sums[0] + jnp.sum(dh, axis=0, keepdims=True), sums[1] + jnp.sum(dhx * nwv, axis=0, keepdims=True),
                    sums[2] + jnp.sum(dhx * scale, axis=0, keepdims=True)]
        for q in range(3):
            st_ref[pl.ds(q, 1), :] = st_ref[pl.ds(q, 1), :] + sums[q]

    row = lambda w_: pl.BlockSpec((tm, w_), lambda i: (i, 0))
    vec = _const_spec((1, D))
    return pl.pallas_call(
        body, name=name, grid=(S // tm,),
        out_shape=(jax.ShapeDtypeStruct((S, D), F32), jax.ShapeDtypeStruct((8, D), F32)),
        in_specs=[row(N), _const_spec((N, D)), row(D), row(D), vec, vec], out_specs=(row(D), _const_spec((8, D))),
        compiler_params=_params(("arbitrary",), VMEM_LIMIT),
    )(dy, wt, x, dres, nw, sc)


def _ffn_bwd(dx, y, g, w_out, gu, wt_in, x, nw, sc, y1, g1, w_mix, *, mix_is_transposed, name):
    S = dx.shape[0]
    K = w_out.shape[0]
    Km = w_mix.shape[1] if mix_is_transposed else w_mix.shape[0]
    tm = 256

    def body(dx_ref, y_ref, g_ref, wo_ref, gu_ref, wi_ref, x_ref, nw_ref, sc_ref, y1_ref, g1_ref, wm_ref,
             dgu_ref, dyb_ref, dxo_ref, da_ref, dy1_ref, stg_ref, stf_ref, stm_ref):
        @pl.when(pl.program_id(0) == 0)
        def _():
            stg_ref[...] = jnp.zeros_like(stg_ref)
            stf_ref[...] = jnp.zeros_like(stf_ref)
            stm_ref[...] = jnp.zeros_like(stm_ref)

        dxv = dx_ref[...]
        stg_ref[pl.ds(0, 1), :] = stg_ref[pl.ds(0, 1), :] + jnp.sum(dxv * y_ref[...].astype(F32), axis=0, keepdims=True)
        dyb = (dxv * g_ref[...]).astype(BF16)
        dyb_ref[...] = dyb
        da = _nt(dyb, wo_ref[...])
        dgate = (da * gu_ref[:, pl.ds(0, K)].astype(F32)).astype(BF16)
        dup = (da * gu_ref[:, pl.ds(K, K)].astype(F32)).astype(BF16)
        dgu_ref[:, pl.ds(0, K)] = dgate
        dgu_ref[:, pl.ds(K, K)] = dup
        dh = _nn(dgate, wi_ref[pl.ds(0, K), :]) + _nn(dup, wi_ref[pl.ds(K, K), :])
        xv = x_ref[...]
        rstd = lax.rsqrt(jnp.mean(xv * xv, axis=-1, keepdims=True) + RMS_EPS)
        xh = xv * rstd
        nwv, scale = nw_ref[...], 1.0 + sc_ref[...]
        dxh = dh * (nwv * scale)
        dx1 = dxv + rstd * (dxh - xh * jnp.mean(dxh * xh, axis=-1, keepdims=True))
        dxo_ref[...] = dx1
        dhx = dh * xh
        stf_ref[pl.ds(0, 1), :] = stf_ref[pl.ds(0, 1), :] + jnp.sum(dh, axis=0, keepdims=True)
        stf_ref[pl.ds(1, 1), :] = stf_ref[pl.ds(1, 1), :] + jnp.sum(dhx * nwv, axis=0, keepdims=True)
        stf_ref[pl.ds(2, 1), :] = stf_ref[pl.ds(2, 1), :] + jnp.sum(dhx * scale, axis=0, keepdims=True)
        stm_ref[pl.ds(0, 1), :] = stm_ref[pl.ds(0, 1), :] + jnp.sum(dx1 * y1_ref[...].astype(F32), axis=0, keepdims=True)
        dy1 = (dx1 * g1_ref[...]).astype(BF16)
        dy1_ref[...] = dy1
        da_ref[...] = (_nn(dy1, wm_ref[...]) if mix_is_transposed else _nt(dy1, wm_ref[...])).astype(BF16)

    row = lambda w_: pl.BlockSpec((tm, w_), lambda i: (i, 0))
    vec = _const_spec((1, D))
    st = jax.ShapeDtypeStruct((8, D), F32)
    act = lambda w_: jax.ShapeDtypeStruct((S, w_), BF16)
    return pl.pallas_call(
        body, name=name, grid=(S // tm,),
        out_shape=(act(2 * K), act(D), jax.ShapeDtypeStruct((S, D), F32), act(Km), act(D), st, st, st),
        in_specs=[row(D), row(D), vec, _const_spec(w_out.shape), row(2 * K), _const_spec(wt_in.shape), row(D), vec, vec,
                  row(D), vec, _const_spec(w_mix.shape)],
        out_specs=(row(2 * K), row(D), row(D), row(Km), row(D), _const_spec((8, D)), _const_spec((8, D)), _const_spec((8, D))),
        compiler_params=_params(("arbitrary",), VMEM_LIMIT),
    )(dx, y, g, w_out, gu, wt_in, x, nw, sc, y1, g1, w_mix)


def _weight_grad(a, b, *, transpose_out, name):
    S, N = b.shape
    nb = N // 2 if N > 4096 else N
    tk = 512

    def body(a_ref, b_ref, out_ref, acc):
        k = pl.program_id(1)

        @pl.when(k == 0)
        def _():
            acc[...] = jnp.zeros_like(acc)

        acc[...] += _tn(a_ref[...], b_ref[...])

        @pl.when(k == pl.num_programs(1) - 1)
        def _():
            out_ref[...] = (acc[...].T if transpose_out else acc[...]).astype(BF16)

    out_block = pl.BlockSpec((nb, D), lambda n, k: (n, 0)) if transpose_out else pl.BlockSpec((D, nb), lambda n, k: (0, n))
    return pl.pallas_call(
        body, name=name, grid=(N // nb, S // tk),
        out_shape=jax.ShapeDtypeStruct((N, D) if transpose_out else (D, N), BF16),
        in_specs=[pl.BlockSpec((tk, D), lambda n, k: (k, 0)), pl.BlockSpec((tk, nb), lambda n, k: (k, n))],
        out_specs=out_block, scratch_shapes=[pltpu.VMEM((D, nb), F32)],
        compiler_params=_params(("parallel", "arbitrary"), VMEM_LIMIT),
    )(a, b)


def _adamw(w, g, m, v):
    m = ADAM_B1 * m + (1.0 - ADAM_B1) * g
    v = ADAM_B2 * v + (1.0 - ADAM_B2) * (g * g)
    m_hat = m / (1.0 - ADAM_B1 ** ADAM_STEP)
    v_hat = v / (1.0 - ADAM_B2 ** ADAM_STEP)
    delta = -ADAM_LR * (m_hat / (jnp.sqrt(v_hat) + ADAM_EPS) + ADAM_WD * w)
    return delta, m, v


def _adam_segment(parts, own, w, m, v, outs, layer, *, transposed, name):
    R, C = own.shape
    if transposed:
        tr, tc, steps = R, min(C, LANES), C // min(C, LANES)
        mine = lambda i: (0, i)
        theirs = lambda i: (0, 0, i)
        block = pl.BlockSpec((1, tc, R), lambda i: (layer, i, 0))
    else:
        tr, tc, steps = 32, C, R // 32
        mine = lambda i: (i, 0)
        theirs = lambda i: (0, i, 0)
        block = pl.BlockSpec((1, tr, C), lambda i: (layer, i, 0))

    def body(p_ref, o_ref, w_ref, m_ref, v_ref, *rest):
        g_out, d_out, m_out, v_out = rest[-4:]
        me = _my_index()
        g = jnp.zeros((tr, tc), F32)
        for j in range(N_DEV):
            g = g + jnp.where(me == j, o_ref[...], p_ref[j]).astype(F32)
        if transposed:
            g = g.T
        delta, mn, vn = _adamw(w_ref[0], g, m_ref[0], v_ref[0])
        g_out[0] = g
        d_out[0] = delta
        m_out[0] = mn
        v_out[0] = vn

    passed = [] if outs is None else [pl.BlockSpec(memory_space=pl.ANY)] * 4
    shp = jax.ShapeDtypeStruct(w.shape, F32)
    return pl.pallas_call(
        body, name=name, grid=(steps,), out_shape=(shp,) * 4,
        in_specs=[pl.BlockSpec((N_DEV, tr, tc), theirs), pl.BlockSpec((tr, tc), mine), block, block, block] + passed,
        out_specs=(block,) * 4, input_output_aliases={5 + q: q for q in range(len(passed))},
        compiler_params=_params(("parallel",), VMEM_LIMIT),
    )(parts, own, w, m, v, *(outs or ()))


def _adam_ada_w(cond_t, dmod, w, m, v):
    ncol = w.shape[-1]
    tr = 512

    def body(c_ref, d_ref, w_ref, m_ref, v_ref, g_out, d_out, m_out, v_out):
        g = _nn(c_ref[...], d_ref[0])
        delta, mn, vn = _adamw(w_ref[0], g, m_ref[0], v_ref[0])
        g_out[0] = g
        d_out[0] = delta
        m_out[0] = mn
        v_out[0] = vn

    blk = pl.BlockSpec((1, tr, ncol), lambda l, i: (l, i, 0))
    shp = jax.ShapeDtypeStruct(w.shape, F32)
    return pl.pallas_call(
        body, name="adam_ada_w", grid=(DEPTH, D // tr), out_shape=(shp,) * 4,
        in_specs=[pl.BlockSpec((tr, LANES), lambda l, i: (i, 0)), pl.BlockSpec((1, LANES, ncol), lambda l, i: (l, 0, 0)), blk, blk, blk],
        out_specs=(blk,) * 4, compiler_params=_params(("parallel", "parallel"), VMEM_LIMIT),
    )(cond_t, dmod, w, m, v)


TILE_ROWS = 168


def _stat_sources():
    pairs = []
    for i in range(DEPTH):
        b = 32 * i
        for q, src in enumerate((b, b + 1, b + 8, b + 16, b + 17, b + 24)):
            pairs.append((6 * i + q, src))
        pairs.append((24 + i, b + 2))
        pairs.append((32 + i, b + 18))
    pairs += [(40, 128), (41, 129)]
    return pairs


def _small_exchange(tiles, w, m, v):
    loss_row, sink_row, sink_src = 41, 48, 136

    def body(s_ref, w_ref, m_ref, v_ref, dmod_out, g_out, d_out, m_out, v_out, loss_out, all_ref, tot_ref, send_sems, recv_sems):
        me = _my_index()
        all_ref[me] = s_ref[...]
        copies = []
        for k in range(1, N_DEV):
            dev, _ = _peer(k)
            cp = pltpu.make_async_remote_copy(src_ref=s_ref, dst_ref=all_ref.at[me], send_sem=send_sems.at[k - 1],
                                              recv_sem=recv_sems.at[k - 1], device_id=dev, device_id_type=MESH)
            cp.start()
            copies.append(cp)
        for k in range(1, N_DEV):
            dev, pidx = _peer(k)
            pltpu.make_async_remote_copy(src_ref=s_ref, dst_ref=all_ref.at[pidx], send_sem=send_sems.at[k - 1],
                                         recv_sem=recv_sems.at[k - 1], device_id=dev, device_id_type=MESH).wait_recv()
        for cp in copies:
            cp.wait_send()
        tot = all_ref[0]
        for j in range(1, N_DEV):
            tot = tot + all_ref[j]
        tot_ref[...] = tot
        g_out[...] = jnp.zeros_like(g_out)
        for dst, src in _stat_sources():
            g_out[pl.ds(dst, 1), :] = tot_ref[pl.ds(src, 1), :]
            if dst < 24:
                for j in range(N_DEV):
                    dmod_out[j, pl.ds(dst, 1), :] = all_ref[j, pl.ds(src, 1), :]
        lane = lax.broadcasted_iota(jnp.int32, (1, D), 1)
        sink = jnp.zeros((1, D), F32)
        for h in range(32):
            sink = jnp.where(lane == h, tot_ref[pl.ds(sink_src + h, 1), :], sink)
        g_out[pl.ds(sink_row, 1), :] = sink
        g = g_out[...]
        delta, mn, vn = _adamw(w_ref[...], g, m_ref[...], v_ref[...])
        d_out[...] = delta
        m_out[...] = mn
        v_out[...] = vn
        loss = jnp.sum(g[loss_row:loss_row + 1, :], axis=-1, keepdims=True) * (0.5 / D)
        loss_out[...] = jnp.broadcast_to(loss, loss_out.shape)

    vm = pl.BlockSpec(memory_space=pltpu.VMEM)
    shp = jax.ShapeDtypeStruct((STAT_ROWS, D), F32)
    return pl.pallas_call(
        body, name="small_exchange",
        out_shape=(jax.ShapeDtypeStruct((N_DEV, 24, D), F32), shp, shp, shp, shp, jax.ShapeDtypeStruct((8, LANES), F32)),
        in_specs=[vm] * 4, out_specs=(vm,) * 6,
        scratch_shapes=[pltpu.VMEM((N_DEV, TILE_ROWS, D), F32), pltpu.VMEM((TILE_ROWS, D), F32),
                        pltpu.SemaphoreType.DMA((N_DEV - 1,)), pltpu.SemaphoreType.DMA((N_DEV - 1,))],
        compiler_params=_params(vmem=VMEM_LIMIT),
    )(tiles, w, m, v)


def _to_rows(name, a):
    if name in ("ffn_in", "a_in", "b_in"):
        return a.T
    if name == "b_out":
        return a.T.reshape(-1, D)
    return a


def _rows8(a):
    return jnp.pad(a, ((0, 8 - a.shape[0]), (0, 0)))


def _pack_small(ada_b, norm_mix, norm_ffn, final_norm, sink):
    sink_row = jnp.pad(sink.reshape(1, -1), ((0, 0), (0, D - sink.size)))
    return jnp.concatenate([ada_b.reshape(24, D), _rows8(norm_mix), _rows8(norm_ffn), _rows8(final_norm.reshape(1, D)),
                            _rows8(sink_row)], axis=0)


def _unpack_small(a):
    return a[0:24].reshape(4, 6 * D), a[24:28], a[32:36], a[40], a[48, :32].reshape(2, 16)


def kernel(x, c, ada_w, ada_b, norm_mix, norm_ffn, ffn_w_in, ffn_w_out, a_w_in, a_w_out, a_sink, b_w_in, b_w_out, final_norm, loss_target, m_ada_w, m_ada_b, m_norm_mix, m_norm_ffn, m_ffn_w_in, m_ffn_w_out, m_a_w_in, m_a_w_out, m_a_sink, m_b_w_in, m_b_w_out, m_final_norm, v_ada_w, v_ada_b, v_norm_mix, v_norm_ffn, v_ffn_w_in, v_ffn_w_out, v_a_w_in, v_a_w_out, v_a_sink, v_b_w_in, v_b_w_out, v_final_norm):
    S = x.shape[1]
    x0 = x.reshape(S, D)
    target = loss_target.reshape(S, D)
    me = _my_index()
    ncol = ada_w.shape[-1]

    ada_b_mine = lax.dynamic_slice_in_dim(ada_b, me * ncol, ncol, axis=1)
    cond_all, parts = _cond_exchange(jnp.broadcast_to(c.reshape(1, D), (8, D)), ada_w, ada_b_mine)
    mod = lax.dynamic_index_in_dim(parts, me, axis=2, keepdims=False)
    mod = jnp.transpose(mod, (1, 0, 2)).reshape(DEPTH, 6, 1, D)

    weights = {"ffn_in": ffn_w_in, "ffn_out": ffn_w_out, "a_in": a_w_in, "a_out": a_w_out, "b_in": b_w_in, "b_out": b_w_out}
    shard = {(n, l): _to_rows(n, weights[n][l]).astype(BF16) for n, l, _ in SEGMENTS}
    first = [sg for sg in _layer_segments(0) if not sg[0].startswith("ffn")]
    gathered0 = _all_gather_weights([shard[(n, l)] for n, l, _ in first])
    W = {(n, l): g for (n, l, _), g in zip(first, gathered0)}
    groups = [[sg for sg in _layer_segments(0) if sg[0].startswith("ffn")]] + [_layer_segments(i) for i in range(1, DEPTH)]
    gathers, order = [], gathered0[0]
    for q, segs in enumerate(groups):
        zones = [(N_DEV, rows, D) for _, _, rows in segs]
        gathers.append(_exchange_start([shard[(n, l)] for n, l, _ in segs], zones, [(s, 0) for s in range(len(segs))],
                                       [sg[2] for sg in segs], False, order, "weight_gather_start_%d" % q))
        order = gathers[-1][-1]
    gather_token = order[0:1, 0:1]

    def finish_gather(q, after):
        segs = groups[q]
        zones = _exchange_wait(gathers[q], len(segs), [(s, 0) for s in range(len(segs))], [sg[2] for sg in segs], True, after,
                               "weight_gather_wait_%d" % q)
        for (n, l, rows), zone in zip(segs, zones):
            W[(n, l)] = zone.reshape(D, 512) if n == "b_out" else zone.reshape(N_DEV * rows, D)

    a_slopes, b_slopes = _slopes(16), _slopes(24)
    bias_a = _alibi_bias(a_slopes, A_HALF, 1)
    bias_b = [_alibi_bias(b_slopes[8 * g:8 * g + 8], B_HALF, dil) for g, dil in enumerate(B_DILS)]
    bias_b_fwd = [_alibi_bias(b_slopes[8 * g:8 * g + 8], B_HALF, dil, max(CHUNK, TQ * dil), both=True) for g, dil in enumerate(B_DILS)]
    a_geom = dict(C=A_QKV, r=1, half=A_HALF, qoff=0, koff=1024, voff=1280, n_units=2)
    b_geom = [dict(C=B_QKV, r=dil, half=B_HALF, qoff=512 * g, koff=1536 + 128 * g, voff=1920 + 128 * g, n_units=1)
              for g, dil in enumerate(B_DILS)]

    saved = []
    xcur = x0
    for i in range(DEPTH):
        j = i // 2
        sh1, sc1, g1, sh2, sc2, g2 = [mod[i, q] for q in range(6)]
        nm, nf = norm_mix[i].reshape(1, D), norm_ffn[i].reshape(1, D)
        if i == 0:
            nm = nm + gather_token
        if i > 0:
            finish_gather(i, xcur)
        if i % 2 == 0:
            sink_rep = jnp.repeat(jnp.repeat(a_sink[j], TQ).reshape(2, 1, 8 * TQ), 8, axis=1).reshape(16, 8 * TQ)
            h1, qkv = _proj(xcur, nm, sc1, sh1, W[("a_in", j)], name="proj_a")
            o, lse = _attn_fwd(qkv, bias_a, sink_rep, out_dtype=BF16, name="attn_a_fwd", **a_geom)
            if i == 0:
                finish_gather(0, o)
            x1, y1, h2, gu, act = _out_ffn_in(o, W[("a_out", j)], xcur, g1, nf, sc2, sh2, W[("ffn_in", i)],
                                              w_is_transposed=False, name="out_a_ffn_in")
        else:
            sink_rep = None
            h1, qkv = _proj(xcur, nm, sc1, sh1, W[("b_in", j)], name="proj_b")
            outs = [_attn_fwd(qkv, bias_b_fwd[g], None, out_dtype=F32, name="attn_b%d_fwd" % g, **b_geom[g]) for g in range(3)]
            o, lse = _merge_groups([t[0] for t in outs], [t[1] for t in outs])
            x1, y1, h2, gu, act = _out_ffn_in(o, W[("b_out", j)], xcur, g1, nf, sc2, sh2, W[("ffn_in", i)],
                                              w_is_transposed=True, name="out_b_ffn_in")
        if i < DEPTH - 1:
            x2, y2 = _ffn_out(act, W[("ffn_out", i)], x1, g2, None, None, name="ffn_out")
        else:
            x2, y2, head_stats = _ffn_out(act, W[("ffn_out", i)], x1, g2, target, final_norm.reshape(1, D), name="ffn_out_loss")
        saved.append(dict(x0=xcur, h1=h1, qkv=qkv, o=o, lse=lse, y1=y1, x1=x1, h2=h2, gu=gu, act=act, y2=y2, sink=sink_rep))
        xcur = x2

    dx = xcur

    dW = {}
    stat_tiles, dsink = [None] * DEPTH, [None] * 2
    exchanges = []
    start_token = None

    def start_exchange(segs):
        own = [lax.dynamic_slice_in_dim(dW[(n, l)], me * rows, rows, axis=0) for n, l, rows in segs]
        zones = [(N_DEV, rows, D) for _, _, rows in segs]
        started = _exchange_start([dW[(n, l)] for n, l, _ in segs], zones, [(s, 0) for s in range(len(segs))],
                                  [sg[2] for sg in segs], True, own[0], "grad_exchange_start_%d" % len(exchanges))
        exchanges.append((segs, started, own))
        return started[-1][0:1, 0:1]

    for i in reversed(range(DEPTH)):
        j = i // 2
        sv = saved[i]
        sh1, sc1, g1, sh2, sc2, g2 = [mod[i, q] for q in range(6)]
        if start_token is not None:
            g2 = g2 + start_token
            start_token = None
        nm, nf = norm_mix[i].reshape(1, D), norm_ffn[i].reshape(1, D)
        mix = "a_out" if i % 2 == 0 else "b_out"
        dgu, dy2, dx1, do, dy1, st_g2, st_f, st_g1 = _ffn_bwd(
            dx, sv["y2"], g2, W[("ffn_out", i)], sv["gu"], W[("ffn_in", i)], sv["x1"], nf, sc2, sv["y1"], g1, W[(mix, j)],
            mix_is_transposed=(i % 2 == 1), name="ffn_bwd_" + mix)
        dW[("ffn_out", i)] = _weight_grad(dy2, sv["act"], transpose_out=True, name="dw_ffn_out")
        dW[("ffn_in", i)] = _weight_grad(sv["h2"], dgu, transpose_out=True, name="dw_ffn_in")
        sink_bwd = sv["sink"]
        if i == 0:
            sink_bwd = sink_bwd + start_exchange([sg for sg in _layer_segments(0) if sg[0].startswith("ffn")])
        if i % 2 == 0:
            dW[("a_out", j)] = _weight_grad(dy1, sv["o"], transpose_out=True, name="dw_a_out")
            dq, dk, dv, ds = _attn_bwd(sv["qkv"], bias_a, sink_bwd, sv["o"], do, sv["lse"], name="attn_a_bwd", **a_geom)
            dsink[j] = ds
            dqkv = jnp.concatenate([dq, dk[0].astype(BF16), dk[1].astype(BF16), dv[0].astype(BF16), dv[1].astype(BF16)], axis=1)
            dW[("a_in", j)] = _weight_grad(sv["h1"], dqkv, transpose_out=True, name="dw_a_in")
            dx0, st_m = _norm_bwd(dqkv, W[("a_in", j)], sv["x0"], dx1, nm, sc1, name="proj_a_bwd")
        else:
            dW[("b_out", j)] = _weight_grad(dy1, sv["o"], transpose_out=False, name="dw_b_out").reshape(N_DEV * 64, D)
            gr = [_attn_bwd(sv["qkv"], bias_b[g], None, sv["o"], do, sv["lse"], name="attn_b%d_bwd" % g, **b_geom[g]) for g in range(3)]
            dqkv = jnp.concatenate([t[0] for t in gr] + [t[1][0].astype(BF16) for t in gr] + [t[2][0].astype(BF16) for t in gr], axis=1)
            dW[("b_in", j)] = _weight_grad(sv["h1"], dqkv, transpose_out=True, name="dw_b_in")
            dx0, st_m = _norm_bwd(dqkv, W[("b_in", j)], sv["x0"], dx1, nm, sc1, name="proj_b_bwd")
        stat_tiles[i] = [st_m, st_g1, st_f, st_g2]
        if i > 0:
            start_token = start_exchange(_layer_segments(i))
        else:
            start_exchange([sg for sg in _layer_segments(0) if not sg[0].startswith("ffn")])
        dx = dx0
    grad_x = dx.reshape(1, S, D)

    masters = {"ffn_in": (ffn_w_in, m_ffn_w_in, v_ffn_w_in), "ffn_out": (ffn_w_out, m_ffn_w_out, v_ffn_w_out),
               "a_in": (a_w_in, m_a_w_in, v_a_w_in), "a_out": (a_w_out, m_a_w_out, v_a_w_out),
               "b_in": (b_w_in, m_b_w_in, v_b_w_in), "b_out": (b_w_out, m_b_w_out, v_b_w_out)}
    results = {n: None for n in masters}
    after = dx
    for e, (segs, started, own) in enumerate(exchanges):
        zones = _exchange_wait(started, len(segs), [(s, 0) for s in range(len(segs))], [sg[2] for sg in segs], False, after,
                               "grad_exchange_wait_%d" % e)
        for (n, l, rows), zone, mine in zip(segs, zones, own):
            if n == "b_out":
                zone, mine = zone.reshape(N_DEV, LANES, 512), mine.reshape(LANES, 512)
            results[n] = _adam_segment(zone, mine, *masters[n], results[n], l, transposed=n in ("ffn_in", "a_in", "b_in", "b_out"),
                                       name="adam_" + n)
            after = results[n][0]
    big = {(kind, n): results[n][q] for q, kind in enumerate(("grad", "delta", "m", "v")) for n in masters}

    tiles = jnp.concatenate([t for i in range(DEPTH) for t in stat_tiles[i]] + [head_stats]
                            + [jnp.pad(ds, ((0, 0), (0, D - LANES))) for ds in dsink], axis=0)
    small = [_pack_small(*t) for t in ((ada_b, norm_mix, norm_ffn, final_norm, a_sink),
                                       (m_ada_b, m_norm_mix, m_norm_ffn, m_final_norm, m_a_sink),
                                       (v_ada_b, v_norm_mix, v_norm_ffn, v_final_norm, v_a_sink))]
    dmod_all, sg, sd, sm, sv_, loss_tile = _small_exchange(tiles, *small)
    loss = loss_tile[0, 0]
    dmod_all = dmod_all.reshape(N_DEV, DEPTH, 6 * D)
    dmod_mine = lax.dynamic_slice_in_dim(dmod_all, me * ncol, ncol, axis=2)
    dmod_pad = jnp.pad(jnp.transpose(dmod_mine, (1, 0, 2)), ((0, 0), (0, LANES - N_DEV), (0, 0))).astype(BF16)
    cond_t = jnp.pad(cond_all.T, ((0, 0), (0, LANES - N_DEV))).astype(BF16)
    ada = _adam_ada_w(cond_t, dmod_pad, ada_w, m_ada_w, v_ada_w)

    outs = [loss, grad_x]
    small_res = [_unpack_small(t) for t in (sg, sd, sm, sv_)]
    for q, kind in enumerate(("grad", "delta", "m", "v")):
        ab, nm_, nf_, fn, sk = small_res[q]
        outs += [ada[q], ab, nm_, nf_, big[(kind, "ffn_in")], big[(kind, "ffn_out")], big[(kind, "a_in")], big[(kind, "a_out")],
                 sk, big[(kind, "b_in")], big[(kind, "b_out")], fn]
    return tuple(outs)
```

```python
import functools
import math

import numpy as np
import jax
import jax.numpy as jnp
from jax import lax
from jax.experimental import pallas as pl
from jax.experimental.pallas import tpu as pltpu

D = 1024
HEAD_DIM = 64
D_FF = 2816
DEPTH = 4
N_DEV = 8
A_QKV = 1536
B_QKV = 2304
A_HALF = 128
B_HALF = 64
B_DILS = (1, 4, 16)
RMS_EPS = 1e-6
NEG = -1e30
ADAM_LR = 0.001
ADAM_B1 = 0.9
ADAM_B2 = 0.999
ADAM_EPS = 1e-08
ADAM_WD = 0.01
ADAM_STEP = 10

LANES = 128
SPLIT = 2
TQ = 128
VMEM_LIMIT = 56 * 1024 * 1024
MESH = pl.DeviceIdType.MESH
F32 = jnp.float32
BF16 = jnp.bfloat16

SEGMENTS = ([("ffn_in", l, 704) for l in range(4)] + [("ffn_out", l, 352) for l in range(4)]
            + [("a_in", j, 192) for j in range(2)] + [("a_out", j, 128) for j in range(2)]
            + [("b_in", j, 288) for j in range(2)] + [("b_out", j, 64) for j in range(2)])
def _layer_segments(i):
    mixer = "a" if i % 2 == 0 else "b"
    return [s for s in SEGMENTS if (s[0].startswith("ffn") and s[1] == i) or (s[0].startswith(mixer + "_") and s[1] == i // 2)]


def _offsets(segs):
    rows = [s[2] for s in segs]
    return [sum(rows[:k]) for k in range(len(rows))], sum(rows)
STAT_ROWS = 56


def _nn(a, b):
    return jnp.dot(a, b, preferred_element_type=F32)


def _nt(a, b):
    return lax.dot_general(a, b, (((1,), (1,)), ((), ())), preferred_element_type=F32)


def _tn(a, b):
    return lax.dot_general(a, b, (((0,), (0,)), ((), ())), preferred_element_type=F32)


def _params(dims=None, vmem=None):
    kw = {}
    if dims is not None:
        kw["dimension_semantics"] = dims
    if vmem is not None:
        kw["vmem_limit_bytes"] = vmem
    return pltpu.CompilerParams(**kw)


def _my_index():
    return 4 * lax.axis_index("x") + 2 * lax.axis_index("y") + lax.axis_index("c")


def _peer(k):
    x, y, c = lax.axis_index("x"), lax.axis_index("y"), lax.axis_index("c")
    px, py, pc = x ^ ((k >> 2) & 1), y ^ ((k >> 1) & 1), c ^ (k & 1)
    return (px, py, pc), 4 * px + 2 * py + pc


def _const_spec(shape):
    nd = len(shape)
    return pl.BlockSpec(shape, lambda *_: (0,) * nd)


def _cond_exchange(c_tile, ada_w, ada_b_mine):
    ncol = ada_w.shape[-1]

    def body(c_ref, w_ref, b_ref, cond_ref, parts_ref, call_ref, mine_ref, send_sems, recv_sems):
        me = _my_index()
        call_ref[me] = c_ref[...]
        copies = []
        for k in range(1, N_DEV):
            dev, _ = _peer(k)
            cp = pltpu.make_async_remote_copy(src_ref=c_ref, dst_ref=call_ref.at[me], send_sem=send_sems.at[0, k - 1],
                                              recv_sem=recv_sems.at[0, k - 1], device_id=dev, device_id_type=MESH)
            cp.start()
            copies.append(cp)
        for k in range(1, N_DEV):
            _, pidx = _peer(k)
            pltpu.make_async_remote_copy(src_ref=c_ref, dst_ref=call_ref.at[pidx], send_sem=send_sems.at[0, k - 1],
                                         recv_sem=recv_sems.at[0, k - 1], device_id=_peer(k)[0], device_id_type=MESH).wait_recv()
        for cp in copies:
            cp.wait_send()
        row = lax.broadcasted_iota(jnp.int32, (N_DEV, D), 0)
        cmat = jnp.zeros((N_DEV, D), F32)
        for j in range(N_DEV):
            cmat = jnp.where(row == j, call_ref[j], cmat)
        cond = cmat * jax.nn.sigmoid(cmat)
        cond_ref[...] = cond
        cb = cond.astype(BF16)
        for l in range(DEPTH):
            mine_ref[l] = _nn(cb, w_ref[l].astype(BF16)) + b_ref[pl.ds(l, 1), :]
        parts_ref[me] = mine_ref[...]
        copies = []
        for k in range(1, N_DEV):
            dev, _ = _peer(k)
            cp = pltpu.make_async_remote_copy(src_ref=mine_ref, dst_ref=parts_ref.at[me], send_sem=send_sems.at[1, k - 1],
                                              recv_sem=recv_sems.at[1, k - 1], device_id=dev, device_id_type=MESH)
            cp.start()
            copies.append(cp)
        for k in range(1, N_DEV):
            dev, pidx = _peer(k)
            pltpu.make_async_remote_copy(src_ref=mine_ref, dst_ref=parts_ref.at[pidx], send_sem=send_sems.at[1, k - 1],
                                         recv_sem=recv_sems.at[1, k - 1], device_id=dev, device_id_type=MESH).wait_recv()
        for cp in copies:
            cp.wait_send()

    vm = pl.BlockSpec(memory_space=pltpu.VMEM)
    return pl.pallas_call(
        body, name="cond_exchange",
        out_shape=(jax.ShapeDtypeStruct((N_DEV, D), F32), jax.ShapeDtypeStruct((N_DEV, DEPTH, N_DEV, ncol), F32)),
        in_specs=[vm, vm, vm], out_specs=(vm, vm),
        scratch_shapes=[pltpu.VMEM((N_DEV, N_DEV, D), F32), pltpu.VMEM((DEPTH, N_DEV, ncol), F32),
                        pltpu.SemaphoreType.DMA((2, N_DEV - 1)), pltpu.SemaphoreType.DMA((2, N_DEV - 1))],
        compiler_params=_params(vmem=VMEM_LIMIT),
    )(c_tile, ada_w, ada_b_mine)[:2]


def _all_gather_weights(shards):
    n = len(shards)
    big = max(range(n), key=lambda s: shards[s].shape[0])
    total = sum(sh.shape[0] for sh in shards)
    assert N_DEV * shards[big].shape[0] >= total

    def body(*refs):
        ins, outs = refs[:n], refs[n:2 * n]
        local_sems, send_sems, recv_sems = refs[2 * n:]
        me = _my_index()
        local = []
        for s in range(n):
            rows = ins[s].shape[0]
            cp = pltpu.make_async_copy(ins[s], outs[s].at[pl.ds(me * rows, rows)], local_sems.at[s])
            cp.start()
            local.append(cp)
        for k in range(1, N_DEV):
            dev, _ = _peer(k)
            for s in range(n):
                rows = ins[s].shape[0]
                pltpu.make_async_remote_copy(src_ref=ins[s], dst_ref=outs[s].at[pl.ds(me * rows, rows)],
                                             send_sem=send_sems.at[k - 1], recv_sem=recv_sems.at[k - 1],
                                             device_id=dev, device_id_type=MESH).start()
        whole = outs[big].at[pl.ds(0, total)]
        for k in range(1, N_DEV):
            dev, _ = _peer(k)
            w = pltpu.make_async_remote_copy(src_ref=whole, dst_ref=whole, send_sem=send_sems.at[k - 1],
                                             recv_sem=recv_sems.at[k - 1], device_id=dev, device_id_type=MESH)
            w.wait_send()
            w.wait_recv()
        for cp in local:
            cp.wait()

    hbm = pl.BlockSpec(memory_space=pl.ANY)
    return pl.pallas_call(
        body, name="weight_all_gather",
        out_shape=tuple(jax.ShapeDtypeStruct((N_DEV * s.shape[0], D), s.dtype) for s in shards),
        in_specs=[hbm] * n, out_specs=tuple([hbm] * n),
        scratch_shapes=[pltpu.SemaphoreType.DMA((n,)), pltpu.SemaphoreType.DMA((N_DEV - 1,)),
                        pltpu.SemaphoreType.DMA((N_DEV - 1,))],
    )(*shards)


HBM = pl.BlockSpec(memory_space=pltpu.HBM)
SEM = pl.BlockSpec(memory_space=pltpu.SEMAPHORE)
EFFECT = pltpu.SideEffectType.DATAFLOW_SIDE_EFFECTING


def _exchange_start(srcs, landings, dst, rows, to_peer_rows, after, name):
    n, nl = len(srcs), len(landings)

    def body(*refs):
        src_refs = refs[:n]
        send_sems, recv_sems = refs[n + 1], refs[n + 2]
        land_refs = refs[2 * n + 3:2 * n + 3 + nl]
        token = refs[-1]
        me = _my_index()
        for k in range(1, N_DEV):
            dev, pidx = _peer(k)
            for q in range(n):
                src = src_refs[q].at[pl.ds(pidx * rows[q], rows[q])] if to_peer_rows else src_refs[q]
                pltpu.make_async_remote_copy(src_ref=src, dst_ref=land_refs[dst[q][0]].at[me, pl.ds(dst[q][1], rows[q])],
                                             send_sem=send_sems.at[k * n + q], recv_sem=recv_sems.at[k * n + q],
                                             device_id=dev, device_id_type=MESH).start()
        if not to_peer_rows:
            for q in range(n):
                pltpu.make_async_copy(src_refs[q], land_refs[dst[q][0]].at[me, pl.ds(dst[q][1], rows[q])], send_sems.at[q]).start()
        token[...] = jnp.zeros_like(token)

    sems = pltpu.SemaphoreType.DMA((N_DEV * n,))
    return pl.pallas_call(
        body, name=name,
        out_shape=(sems, sems, *[pltpu.HBM(a.shape, a.dtype) for a in srcs], *[pltpu.HBM(shape, BF16) for shape in landings],
                   jax.ShapeDtypeStruct((8, LANES), F32)),
        in_specs=[HBM] * n + [pl.BlockSpec(memory_space=pl.ANY)],
        out_specs=(SEM, SEM, *[HBM] * (n + nl), pl.BlockSpec(memory_space=pltpu.VMEM)),
        input_output_aliases={q: 2 + q for q in range(n)},
        compiler_params=pltpu.CompilerParams(has_side_effects=EFFECT),
    )(*[pltpu.with_memory_space_constraint(a, pltpu.HBM) for a in srcs], after)


def _exchange_wait(started, n, dst, rows, own_slot, after, name):
    send_sems, recv_sems = started[0], started[1]
    arrays = list(started[2:-1])
    n1 = len(arrays)

    def body(*refs):
        land_refs = refs[n:n1]
        sends, recvs = refs[n1], refs[n1 + 1]
        for k in range(1, N_DEV):
            dev, _ = _peer(k)
            for q in range(n):
                slot = land_refs[dst[q][0]].at[0, pl.ds(dst[q][1], rows[q])]
                w = pltpu.make_async_remote_copy(src_ref=slot, dst_ref=slot, send_sem=sends.at[k * n + q],
                                                 recv_sem=recvs.at[k * n + q], device_id=dev, device_id_type=MESH)
                w.wait_send()
                w.wait_recv()
        if own_slot:
            for q in range(n):
                slot = land_refs[dst[q][0]].at[0, pl.ds(dst[q][1], rows[q])]
                pltpu.make_async_copy(slot, slot, sends.at[q]).wait()

    return pl.pallas_call(
        body, name=name, out_shape=tuple(pltpu.HBM(a.shape, a.dtype) for a in arrays),
        in_specs=[HBM] * n1 + [SEM, SEM, pl.BlockSpec(memory_space=pl.ANY)], out_specs=tuple([HBM] * n1),
        input_output_aliases={q: q for q in range(n1)},
        compiler_params=pltpu.CompilerParams(has_side_effects=EFFECT),
    )(*arrays, send_sems, recv_sems, after)[n:]


def _norm_mod(x, nw, sc, sh):
    ms = jnp.mean(x * x, axis=-1, keepdims=True)
    xh = x * lax.rsqrt(ms + RMS_EPS)
    return xh, (xh * nw) * (1.0 + sc) + sh


def _proj(x, nw, sc, sh, wt, *, name):
    S, N = x.shape[0], wt.shape[0]
    tm = 512

    def body(x_ref, nw_ref, sc_ref, sh_ref, w_ref, h_ref, out_ref):
        for half in range(SPLIT):
            rows = pl.ds(half * (tm // SPLIT), tm // SPLIT)
            _, h = _norm_mod(x_ref[rows, :], nw_ref[...], sc_ref[...], sh_ref[...])
            hb = h.astype(BF16)
            h_ref[rows, :] = hb
            out_ref[rows, :] = _nt(hb, w_ref[...]).astype(BF16)

    row = lambda w: pl.BlockSpec((tm, w), lambda i: (i, 0))
    vec = _const_spec((1, D))
    return pl.pallas_call(
        body, name=name, grid=(S // tm,), out_shape=(jax.ShapeDtypeStruct((S, D), BF16), jax.ShapeDtypeStruct((S, N), BF16)),
        in_specs=[row(D), vec, vec, vec, _const_spec((N, D))], out_specs=(row(D), row(N)),
        compiler_params=_params(("parallel",), VMEM_LIMIT),
    )(x, nw, sc, sh, wt)


def _ffn_out(a, w, x, g, target, fnw, *, name):
    S, K = a.shape
    tm = 512
    last = target is not None

    def body(a_ref, w_ref, x_ref, g_ref, *rest):
        y = _nn(a_ref[...], w_ref[...])
        xv = x_ref[...] + g_ref[...] * y
        if not last:
            xo_ref, y_ref = rest
            y_ref[...] = y.astype(BF16)
            xo_ref[...] = xv
            return
        t_ref, fw_ref, dx_ref, y_ref, st_ref = rest
        y_ref[...] = y.astype(BF16)

        @pl.when(pl.program_id(0) == 0)
        def _():
            st_ref[...] = jnp.zeros_like(st_ref)

        rstd = lax.rsqrt(jnp.mean(xv * xv, axis=-1, keepdims=True) + RMS_EPS)
        xh = xv * rstd
        err = xh * fw_ref[...] - t_ref[...]
        dy = err * (1.0 / D)
        dxh = dy * fw_ref[...]
        dx_ref[...] = rstd * (dxh - xh * jnp.mean(dxh * xh, axis=-1, keepdims=True))
        st_ref[pl.ds(0, 1), :] = st_ref[pl.ds(0, 1), :] + jnp.sum(dy * xh, axis=0, keepdims=True)
        st_ref[pl.ds(1, 1), :] = st_ref[pl.ds(1, 1), :] + jnp.sum(err * err, axis=0, keepdims=True)

    row = lambda w_: pl.BlockSpec((tm, w_), lambda i: (i, 0))
    in_specs = [row(K), _const_spec(w.shape), row(D), _const_spec((1, D))]
    args = [a, w, x, g]
    out_shape = [jax.ShapeDtypeStruct((S, D), F32), jax.ShapeDtypeStruct((S, D), BF16)]
    out_specs = [row(D), row(D)]
    if last:
        in_specs += [row(D), _const_spec((1, D))]
        args += [target, fnw]
        out_shape.append(jax.ShapeDtypeStruct((8, D), F32))
        out_specs.append(_const_spec((8, D)))
    return pl.pallas_call(
        body, name=name, grid=(S // tm,), out_shape=tuple(out_shape), in_specs=in_specs, out_specs=tuple(out_specs),
        compiler_params=_params(("arbitrary",) if last else ("parallel",), VMEM_LIMIT),
    )(*args)


CHUNK = 1024


def _tile_rows(r, chunk=CHUNK):
    return min(TQ, chunk // r)


def _out_ffn_in(a, w_mix, x, g, nw, sc, sh, wt, *, w_is_transposed, name):
    S, K = a.shape
    tm = 256

    def body(a_ref, wm_ref, x_ref, g_ref, nw_ref, sc_ref, sh_ref, w_ref, x1_ref, y_ref, h_ref, gu_ref, act_ref):
        y = _nt(a_ref[...], wm_ref[...]) if w_is_transposed else _nn(a_ref[...], wm_ref[...])
        y_ref[...] = y.astype(BF16)
        x1 = x_ref[...] + g_ref[...] * y
        x1_ref[...] = x1
        _, h = _norm_mod(x1, nw_ref[...], sc_ref[...], sh_ref[...])
        hb = h.astype(BF16)
        h_ref[...] = hb
        gate = _nt(hb, w_ref[pl.ds(0, D_FF), :])
        up = _nt(hb, w_ref[pl.ds(D_FF, D_FF), :])
        sig = jax.nn.sigmoid(gate)
        silu = gate * sig
        gu_ref[:, pl.ds(0, D_FF)] = (up * (sig * (1.0 + gate * (1.0 - sig)))).astype(BF16)
        gu_ref[:, pl.ds(D_FF, D_FF)] = silu.astype(BF16)
        act_ref[...] = (silu * up).astype(BF16)

    row = lambda w_: pl.BlockSpec((tm, w_), lambda i: (i, 0))
    vec = _const_spec((1, D))
    return pl.pallas_call(
        body, name=name, grid=(S // tm,),
        out_shape=(jax.ShapeDtypeStruct((S, D), F32), jax.ShapeDtypeStruct((S, D), BF16), jax.ShapeDtypeStruct((S, D), BF16),
                   jax.ShapeDtypeStruct((S, 2 * D_FF), BF16), jax.ShapeDtypeStruct((S, D_FF), BF16)),
        in_specs=[row(K), _const_spec(w_mix.shape), row(D), vec, vec, vec, vec, _const_spec(wt.shape)],
        out_specs=(row(D), row(D), row(D), row(2 * D_FF), row(D_FF)),
        compiler_params=_params(("parallel",), VMEM_LIMIT),
    )(a, w_mix, x, g, nw, sc, sh, wt)


def _alibi_bias(slopes, half, dil, chunk=CHUNK, both=False):
    tq = _tile_rows(dil, chunk)
    tk = tq + 2 * half
    rel = np.arange(tk)[:, None] - half - np.arange(tq)[None, :]
    band = np.abs(rel) <= half
    dist = (dil * np.abs(rel)).astype(np.float32)
    tabs = [np.where(band, -np.float32(s) * dist, np.float32(NEG)).astype(np.float32) for s in slopes]
    out = []
    for u in range(0, len(tabs), 8):
        tab = np.concatenate(tabs[u:u + 8], axis=1)
        first, last = tab.copy(), tab.copy()
        first[:half] = NEG
        last[tk - half:] = NEG
        out += [tab, first, last]
        if both:
            last = last.copy()
            last[:half] = NEG
            out.append(last)
    return jnp.asarray(np.concatenate(out, axis=0))


def _slopes(n):
    return (2.0 ** (-8.0 * np.arange(1, n + 1) / n)).astype(np.float32)


def _head_masks(tq):
    lane = lax.broadcasted_iota(jnp.int32, (tq, LANES), 1)
    lo = lane < HEAD_DIM
    return lo, jnp.logical_not(lo)


def _stack_heads(tiles, lo, hi, scale):
    blocks = []
    for t in range(4):
        xf = tiles[t] if scale == 1.0 else tiles[t] * scale
        for a in range(2):
            xm = jnp.where(lo if a == 0 else hi, xf, 0.0)
            if a != t // 2:
                xm = pltpu.roll(xm, HEAD_DIM, 1)
            blocks.append(xm.astype(BF16))
    return jnp.concatenate(blocks, axis=0)


def _tile_from_columns(x8t, t, tq):
    r0 = HEAD_DIM * (t // 2)
    top = x8t[r0:r0 + HEAD_DIM, 2 * t * tq:(2 * t + 1) * tq]
    bot = x8t[r0:r0 + HEAD_DIM, (2 * t + 1) * tq:(2 * t + 2) * tq]
    return jnp.concatenate([top, bot], axis=0).T


def _attn_layout(S, C, r, half, qoff, koff, voff, chunk):
    hb = half * r
    per = chunk // hb
    nhb = S // hb
    main = lambda off: pl.BlockSpec((chunk, LANES), lambda u, i: (i, off // LANES + u))
    prev = lambda off: pl.BlockSpec((hb, LANES), lambda u, i: (jnp.maximum(i * per - 1, 0), off // LANES + u))
    nxt = lambda off: pl.BlockSpec((hb, LANES), lambda u, i: (jnp.minimum((i + 1) * per, nhb - 1), off // LANES + u))
    specs = [pl.BlockSpec((chunk, 4 * LANES), lambda u, i: (i, qoff // (4 * LANES) + u))]
    specs += [prev(koff), main(koff), nxt(koff), prev(voff), main(voff), nxt(voff)]
    return specs, hb


def _stage(dst, srcs):
    row = 0
    for src in srcs:
        n = src.shape[0]
        dst[pl.ds(row, n), :] = src[...].astype(F32)
        row += n


def _rows(start, n, r):
    return pl.ds(start, n, stride=r) if r > 1 else pl.ds(start, n)


def _attn_fwd(qkv, bias, sink, *, C, r, half, qoff, koff, voff, n_units, out_dtype, name):
    S = qkv.shape[0]
    chunk = max(CHUNK, TQ * r)
    tq = _tile_rows(r, chunk)
    tk = tq + 2 * half
    tiles = chunk // (r * tq)
    nsteps = S // chunk
    specs, hb = _attn_layout(S, C, r, half, qoff, koff, voff, chunk)
    use_sink = sink is not None

    def body(*refs):
        q_ref, kp, km, kn, vp, vm, vn, bias_ref = refs[:8]
        rest = list(refs[8:])
        sink_ref = rest.pop(0) if use_sink else None
        o_ref, lse_ref, qs, ks, vs, os_, ls = rest
        i = pl.program_id(1)
        if r > 1:
            for t in range(4):
                qs[t] = q_ref[:, pl.ds(t * LANES, LANES)].astype(F32)
        _stage(ks, [kp, km, kn])
        _stage(vs, [vp, vm, vn])
        lo, hi = _head_masks(tq)

        def tile_in(staged, ref, t, start):
            if r > 1:
                return staged[t, _rows(start, tq, r), :]
            return ref[pl.ds(start, tq), pl.ds(t * LANES, LANES)].astype(F32)

        ones = jnp.ones((16, tk), BF16)
        if use_sink:
            sk = sink_ref[pl.ds(0, 1), :]

        def chain(n, carry):
            rho, c = n // tiles, n % tiles
            start = c * (tq * r) + rho
            if r == 1:
                start = pl.multiple_of(start, tq)
            variant = jnp.where(jnp.logical_and(i == 0, c == 0), 1, 0) + jnp.where(
                jnp.logical_and(i == nsteps - 1, c == tiles - 1), 2, 0)
            k2 = ks[_rows(start, tk, r), :].astype(BF16)
            v2t = jnp.concatenate([vs[_rows(start, tk, r), :].T.astype(BF16), ones], axis=0)
            q8 = _stack_heads([tile_in(qs, q_ref, t, start) for t in range(4)], lo, hi, HEAD_DIM ** -0.5)
            s = _nt(k2, q8) + bias_ref[pl.ds(pl.multiple_of(variant * tk, 8), tk), :]
            m = jnp.max(s, axis=0, keepdims=True)
            if use_sink:
                m = jnp.maximum(m, sk)
            pv = _nn(v2t, jnp.exp(s - m).astype(BF16))
            l = pv[LANES:LANES + 1]
            if use_sink:
                l = l + jnp.exp(sk - m)
            o8t = pv[:LANES] / l
            lse8 = jnp.broadcast_to(m + jnp.log(l), (LANES, 8 * tq))
            for t in range(4):
                if r > 1:
                    os_[t, _rows(start, tq, r), :] = _tile_from_columns(o8t, t, tq)
                    ls[t, _rows(start, tq, r), :] = _tile_from_columns(lse8, t, tq)
                else:
                    o_ref[pl.ds(start, tq), pl.ds(t * LANES, LANES)] = _tile_from_columns(o8t, t, tq).astype(out_dtype)
                    lse_ref[pl.ds(start, tq), pl.ds(t * LANES, LANES)] = _tile_from_columns(lse8, t, tq)
            return carry

        lax.fori_loop(0, r * tiles, chain, 0, unroll=4)
        if r > 1:
            for t in range(4):
                o_ref[:, pl.ds(t * LANES, LANES)] = os_[t].astype(out_dtype)
                lse_ref[:, pl.ds(t * LANES, LANES)] = ls[t]

    in_specs = specs + [pl.BlockSpec((bias.shape[0] // n_units, 8 * tq), lambda u, i: (u, 0))]
    args = [qkv] * 7 + [bias]
    if use_sink:
        in_specs.append(pl.BlockSpec((8, 8 * tq), lambda u, i: (u, 0)))
        args.append(sink)
    wide = pl.BlockSpec((chunk, 4 * LANES), lambda u, i: (i, u))
    win = hb + chunk + hb
    big = lambda: pltpu.VMEM((4, chunk if r > 1 else 8, LANES), F32)
    return pl.pallas_call(
        body, name=name, grid=(n_units, nsteps),
        out_shape=(jax.ShapeDtypeStruct((S, n_units * 512), out_dtype), jax.ShapeDtypeStruct((S, n_units * 512), F32)),
        in_specs=in_specs, out_specs=(wide, wide),
        scratch_shapes=[big(), pltpu.VMEM((win, LANES), F32), pltpu.VMEM((win, LANES), F32), big(), big()],
        compiler_params=_params(("parallel", "parallel"), VMEM_LIMIT),
    )(*args)


def _attn_bwd(qkv, bias, sink, o, do, lse, *, C, r, half, qoff, koff, voff, n_units, name):
    S = qkv.shape[0]
    tq = _tile_rows(r)
    tk = tq + 2 * half
    tiles = CHUNK // (r * tq)
    nsteps = S // CHUNK
    specs, hb = _attn_layout(S, C, r, half, qoff, koff, voff, CHUNK)
    use_sink = sink is not None

    def body(*refs):
        q_ref, kp, km, kn, vp, vm, vn, bias_ref = refs[:8]
        rest = list(refs[8:])
        sink_ref = rest.pop(0) if use_sink else None
        o_ref, do_ref, lse_ref, dq_ref, dk_hbm, dv_hbm = rest[:6]
        rest = rest[6:]
        dsink_ref = rest.pop(0) if use_sink else None
        qs, ks, vs, os_, dos, ls, dqs, acck, accv, sem = rest
        u, i = pl.program_id(0), pl.program_id(1)

        @pl.when(i == 0)
        def _():
            acck[...] = jnp.zeros_like(acck)
            accv[...] = jnp.zeros_like(accv)
            if use_sink:
                dsink_ref[...] = jnp.zeros_like(dsink_ref)

        if r > 1:
            for t in range(4):
                cols = pl.ds(t * LANES, LANES)
                qs[t] = q_ref[:, cols].astype(F32)
                os_[t] = o_ref[:, cols].astype(F32)
                dos[t] = do_ref[:, cols].astype(F32)
                ls[t] = lse_ref[:, cols]
        _stage(ks, [kp, km, kn])
        _stage(vs, [vp, vm, vn])
        lo, hi = _head_masks(tq)

        def tile_in(staged, ref, t, start):
            if r > 1:
                return staged[t, _rows(start, tq, r), :]
            return ref[pl.ds(start, tq), pl.ds(t * LANES, LANES)].astype(F32)

        base = pl.multiple_of(i * CHUNK, CHUNK)
        if use_sink:
            sk = sink_ref[pl.ds(0, 1), :]

        def chain(n, carry):
            rho, c = n // tiles, n % tiles
            start = c * (tq * r) + rho
            if r == 1:
                start = pl.multiple_of(start, tq)
            variant = jnp.where(jnp.logical_and(i == 0, c == 0), 1, 0) + jnp.where(
                jnp.logical_and(i == nsteps - 1, c == tiles - 1), 2, 0)
            k2 = ks[_rows(start, tk, r), :].astype(BF16)
            v2 = vs[_rows(start, tk, r), :].astype(BF16)
            k2t = ks[_rows(start, tk, r), :].T.astype(BF16)
            q8 = _stack_heads([tile_in(qs, q_ref, t, start) for t in range(4)], lo, hi, HEAD_DIM ** -0.5)
            do_tiles = [tile_in(dos, do_ref, t, start) for t in range(4)]
            do8 = _stack_heads(do_tiles, lo, hi, 1.0)
            deltas, lses = [], []
            for t in range(4):
                prod_t = (do_tiles[t] * tile_in(os_, o_ref, t, start)).T
                lse_t = tile_in(ls, lse_ref, t, start).T
                for a in range(2):
                    deltas.append(jnp.sum(prod_t[a * HEAD_DIM:(a + 1) * HEAD_DIM], axis=0, keepdims=True))
                    lses.append(lse_t[a * HEAD_DIM:a * HEAD_DIM + 1])
            delta8 = jnp.concatenate(deltas, axis=1)
            lse8 = jnp.concatenate(lses, axis=1)
            s = _nt(k2, q8) + bias_ref[pl.ds(pl.multiple_of(variant * tk, 8), tk), :]
            p = jnp.exp(s - lse8)
            dp = _nt(v2, do8)
            dsb = (p * (dp - delta8)).astype(BF16)
            dq8t = _nn(k2t, dsb)
            for t in range(4):
                dq_t = _tile_from_columns(dq8t, t, tq) * (HEAD_DIM ** -0.5)
                if r > 1:
                    dqs[t, _rows(start, tq, r), :] = dq_t
                else:
                    dq_ref[pl.ds(start, tq), pl.ds(t * LANES, LANES)] = dq_t.astype(BF16)
            arow = base + start
            if r == 1:
                arow = pl.multiple_of(arow, tq)
            acck[_rows(arow, tk, r), :] = acck[_rows(arow, tk, r), :] + _nn(dsb, q8)
            accv[_rows(arow, tk, r), :] = accv[_rows(arow, tk, r), :] + _nn(p.astype(BF16), do8)
            if use_sink:
                e = jnp.exp(sk - lse8) * delta8
                for h in range(8):
                    part = -jnp.sum(e[:, h * tq:(h + 1) * tq], axis=1, keepdims=True)
                    dsink_ref[pl.ds(h, 1), :] = dsink_ref[pl.ds(h, 1), :] + part
            return carry

        lax.fori_loop(0, r * tiles, chain, 0, unroll=2)
        if r > 1:
            for t in range(4):
                dq_ref[:, pl.ds(t * LANES, LANES)] = dqs[t].astype(BF16)

        @pl.when(i == nsteps - 1)
        def _():
            ck = pltpu.make_async_copy(acck.at[pl.ds(hb, S)], dk_hbm.at[u], sem.at[0])
            cv = pltpu.make_async_copy(accv.at[pl.ds(hb, S)], dv_hbm.at[u], sem.at[1])
            ck.start()
            cv.start()
            ck.wait()
            cv.wait()

    wide = pl.BlockSpec((CHUNK, 4 * LANES), lambda u, i: (i, u))
    hbm = pl.BlockSpec(memory_space=pl.ANY)
    in_specs = specs + [pl.BlockSpec((3 * tk, 8 * tq), lambda u, i: (u, 0))]
    args = [qkv] * 7 + [bias]
    if use_sink:
        in_specs.append(pl.BlockSpec((8, 8 * tq), lambda u, i: (u, 0)))
        args.append(sink)
    in_specs += [wide, wide, wide]
    args += [o, do, lse]
    out_shape = [jax.ShapeDtypeStruct((S, n_units * 512), BF16), jax.ShapeDtypeStruct((n_units, S, LANES), F32),
                 jax.ShapeDtypeStruct((n_units, S, LANES), F32)]
    out_specs = [wide, hbm, hbm]
    if use_sink:
        out_shape.append(jax.ShapeDtypeStruct((n_units * 8, LANES), F32))
        out_specs.append(pl.BlockSpec((8, LANES), lambda u, i: (u, 0)))
    win = hb + CHUNK + hb
    big = lambda: pltpu.VMEM((4, CHUNK if r > 1 else 8, LANES), F32)
    res = pl.pallas_call(
        body, name=name, grid=(n_units, nsteps), out_shape=tuple(out_shape), in_specs=in_specs, out_specs=tuple(out_specs),
        scratch_shapes=[big(), pltpu.VMEM((win, LANES), F32), pltpu.VMEM((win, LANES), F32), big(), big(), big(), big(),
                        pltpu.VMEM((S + 2 * hb, LANES), F32), pltpu.VMEM((S + 2 * hb, LANES), F32), pltpu.SemaphoreType.DMA((2,))],
        compiler_params=_params(("arbitrary", "arbitrary"), VMEM_LIMIT),
    )(*args)
    return res[0], res[1], res[2], (res[3] if use_sink else None)


def _merge_groups(os_, lses):
    S, W = os_[0].shape
    tm = 512

    def body(o0, o1, o2, l0, l1, l2, o_ref, lse_ref):
        ls = [l0[...], l1[...], l2[...]]
        mx = jnp.maximum(jnp.maximum(ls[0], ls[1]), ls[2])
        es = [jnp.exp(l - mx) for l in ls]
        den = es[0] + es[1] + es[2]
        o = (es[0] / den) * o0[...] + (es[1] / den) * o1[...] + (es[2] / den) * o2[...]
        o_ref[...] = o.astype(BF16)
        lse_ref[...] = mx + jnp.log(den)

    row = pl.BlockSpec((tm, W), lambda i: (i, 0))
    return pl.pallas_call(
        body, name="merge_groups", grid=(S // tm,),
        out_shape=(jax.ShapeDtypeStruct((S, W), BF16), jax.ShapeDtypeStruct((S, W), F32)),
        in_specs=[row] * 6, out_specs=(row, row), compiler_params=_params(("parallel",), VMEM_LIMIT),
    )(*os_, *lses)


def _columns(part_refs, rows):
    return jnp.concatenate([r[rows, :].astype(BF16) for r in part_refs], axis=1)


def _norm_bwd(dy_parts, wt, x, dres, nw, sc, *, name):
    S, N = x.shape[0], wt.shape[0]
    npart = len(dy_parts)
    tm = 512

    def body(*refs):
        part_refs = refs[:npart]
        w_ref, x_ref, dres_ref, nw_ref, sc_ref, dx_ref, st_ref = refs[npart:]

        @pl.when(pl.program_id(0) == 0)
        def _():
            st_ref[...] = jnp.zeros_like(st_ref)

        nwv, scale = nw_ref[...], 1.0 + sc_ref[...]
        sums = [jnp.zeros((1, D), F32)] * 3
        for half in range(SPLIT):
            rows = pl.ds(half * (tm // SPLIT), tm // SPLIT)
            dh = _nn(_columns(part_refs, rows), w_ref[...])
            xv = x_ref[rows, :]
            rstd = lax.rsqrt(jnp.mean(xv * xv, axis=-1, keepdims=True) + RMS_EPS)
            xh = xv * rstd
            dxh = dh * (nwv * scale)
            dx_ref[rows, :] = dres_ref[rows, :] + rstd * (dxh - xh * jnp.mean(dxh * xh, axis=-1, keepdims=True))
            dhx = dh * xh
            sums = [sums[0] + jnp.sum(dh, axis=0, keepdims=True), sums[1] + jnp.sum(dhx * nwv, axis=0, keepdims=True),
                    sums[2] + jnp.sum(dhx * scale, axis=0, keepdims=True)]
        for q in range(3):
            st_ref[pl.ds(q, 1), :] = st_ref[pl.ds(q, 1), :] + sums[q]

    row = lambda w_: pl.BlockSpec((tm, w_), lambda i: (i, 0))
    vec = _const_spec((1, D))
    return pl.pallas_call(
        body, name=name, grid=(S // tm,),
        out_shape=(jax.ShapeDtypeStruct((S, D), F32), jax.ShapeDtypeStruct((8, D), F32)),
        in_specs=[row(p.shape[1]) for p in dy_parts] + [_const_spec((N, D)), row(D), row(D), vec, vec],
        out_specs=(row(D), _const_spec((8, D))), compiler_params=_params(("arbitrary",), VMEM_LIMIT),
    )(*dy_parts, wt, x, dres, nw, sc)


def _ffn_bwd(dx, y, g, w_out, gu, wt_in, x, nw, sc, y1, g1, w_mix, *, mix_is_transposed, name):
    S = dx.shape[0]
    K = w_out.shape[0]
    Km = w_mix.shape[1] if mix_is_transposed else w_mix.shape[0]
    tm = 256

    def body(dx_ref, y_ref, g_ref, wo_ref, gu_ref, wi_ref, x_ref, nw_ref, sc_ref, y1_ref, g1_ref, wm_ref,
             dgu_ref, dyb_ref, dxo_ref, da_ref, dy1_ref, stg_ref, stf_ref, stm_ref):
        @pl.when(pl.program_id(0) == 0)
        def _():
            stg_ref[...] = jnp.zeros_like(stg_ref)
            stf_ref[...] = jnp.zeros_like(stf_ref)
            stm_ref[...] = jnp.zeros_like(stm_ref)

        dxv = dx_ref[...]
        stg_ref[pl.ds(0, 1), :] = stg_ref[pl.ds(0, 1), :] + jnp.sum(dxv * y_ref[...].astype(F32), axis=0, keepdims=True)
        dyb = (dxv * g_ref[...]).astype(BF16)
        dyb_ref[...] = dyb
        da = _nt(dyb, wo_ref[...])
        dgate = (da * gu_ref[:, pl.ds(0, K)].astype(F32)).astype(BF16)
        dup = (da * gu_ref[:, pl.ds(K, K)].astype(F32)).astype(BF16)
        dgu_ref[:, pl.ds(0, K)] = dgate
        dgu_ref[:, pl.ds(K, K)] = dup
        dh = _nn(dgate, wi_ref[pl.ds(0, K), :]) + _nn(dup, wi_ref[pl.ds(K, K), :])
        xv = x_ref[...]
        rstd = lax.rsqrt(jnp.mean(xv * xv, axis=-1, keepdims=True) + RMS_EPS)
        xh = xv * rstd
        nwv, scale = nw_ref[...], 1.0 + sc_ref[...]
        dxh = dh * (nwv * scale)
        dx1 = dxv + rstd * (dxh - xh * jnp.mean(dxh * xh, axis=-1, keepdims=True))
        dxo_ref[...] = dx1
        dhx = dh * xh
        stf_ref[pl.ds(0, 1), :] = stf_ref[pl.ds(0, 1), :] + jnp.sum(dh, axis=0, keepdims=True)
        stf_ref[pl.ds(1, 1), :] = stf_ref[pl.ds(1, 1), :] + jnp.sum(dhx * nwv, axis=0, keepdims=True)
        stf_ref[pl.ds(2, 1), :] = stf_ref[pl.ds(2, 1), :] + jnp.sum(dhx * scale, axis=0, keepdims=True)
        stm_ref[pl.ds(0, 1), :] = stm_ref[pl.ds(0, 1), :] + jnp.sum(dx1 * y1_ref[...].astype(F32), axis=0, keepdims=True)
        dy1 = (dx1 * g1_ref[...]).astype(BF16)
        dy1_ref[...] = dy1
        da_ref[...] = (_nn(dy1, wm_ref[...]) if mix_is_transposed else _nt(dy1, wm_ref[...])).astype(BF16)

    row = lambda w_: pl.BlockSpec((tm, w_), lambda i: (i, 0))
    vec = _const_spec((1, D))
    st = jax.ShapeDtypeStruct((8, D), F32)
    act = lambda w_: jax.ShapeDtypeStruct((S, w_), BF16)
    return pl.pallas_call(
        body, name=name, grid=(S // tm,),
        out_shape=(act(2 * K), act(D), jax.ShapeDtypeStruct((S, D), F32), act(Km), act(D), st, st, st),
        in_specs=[row(D), row(D), vec, _const_spec(w_out.shape), row(2 * K), _const_spec(wt_in.shape), row(D), vec, vec,
                  row(D), vec, _const_spec(w_mix.shape)],
        out_specs=(row(2 * K), row(D), row(D), row(Km), row(D), _const_spec((8, D)), _const_spec((8, D)), _const_spec((8, D))),
        compiler_params=_params(("arbitrary",), VMEM_LIMIT),
    )(dx, y, g, w_out, gu, wt_in, x, nw, sc, y1, g1, w_mix)


def _weight_grad(a, b_parts, *, transpose_out, name):
    S = a.shape[0]
    N = sum(p.shape[1] for p in b_parts)
    npart = len(b_parts)
    nb = N // 2 if N > 4096 else N
    assert nb == N or npart == 1
    tk = 512

    def body(*refs):
        a_ref, part_refs = refs[0], refs[1:1 + npart]
        out_ref, acc = refs[1 + npart:]
        k = pl.program_id(1)

        @pl.when(k == 0)
        def _():
            acc[...] = jnp.zeros_like(acc)

        acc[...] += _tn(a_ref[...], _columns(part_refs, slice(None)))

        @pl.when(k == pl.num_programs(1) - 1)
        def _():
            out_ref[...] = (acc[...].T if transpose_out else acc[...]).astype(BF16)

    out_block = pl.BlockSpec((nb, D), lambda n, k: (n, 0)) if transpose_out else pl.BlockSpec((D, nb), lambda n, k: (0, n))
    part_specs = [pl.BlockSpec((tk, nb if npart == 1 else p.shape[1]), lambda n, k: (k, n)) for p in b_parts]
    return pl.pallas_call(
        body, name=name, grid=(N // nb, S // tk),
        out_shape=jax.ShapeDtypeStruct((N, D) if transpose_out else (D, N), BF16),
        in_specs=[pl.BlockSpec((tk, D), lambda n, k: (k, 0))] + part_specs,
        out_specs=out_block, scratch_shapes=[pltpu.VMEM((D, nb), F32)],
        compiler_params=_params(("parallel", "arbitrary"), VMEM_LIMIT),
    )(a, *b_parts)


def _adamw(w, g, m, v):
    m = ADAM_B1 * m + (1.0 - ADAM_B1) * g
    v = ADAM_B2 * v + (1.0 - ADAM_B2) * (g * g)
    m_hat = m / (1.0 - ADAM_B1 ** ADAM_STEP)
    v_hat = v / (1.0 - ADAM_B2 ** ADAM_STEP)
    delta = -ADAM_LR * (m_hat / (jnp.sqrt(v_hat) + ADAM_EPS) + ADAM_WD * w)
    return delta, m, v


def _adam_segment(parts, own, w, m, v, outs, layer, *, transposed, name):
    R, C = own.shape
    if transposed:
        tr, tc, steps = R, min(C, LANES), C // min(C, LANES)
        mine = lambda i: (0, i)
        theirs = lambda i: (0, 0, i)
        block = pl.BlockSpec((1, tc, R), lambda i: (layer, i, 0))
    else:
        tr, tc, steps = 32, C, R // 32
        mine = lambda i: (i, 0)
        theirs = lambda i: (0, i, 0)
        block = pl.BlockSpec((1, tr, C), lambda i: (layer, i, 0))

    def body(p_ref, o_ref, w_ref, m_ref, v_ref, *rest):
        g_out, d_out, m_out, v_out = rest[-4:]
        me = _my_index()
        g = jnp.zeros((tr, tc), F32)
        for j in range(N_DEV):
            g = g + jnp.where(me == j, o_ref[...], p_ref[j]).astype(F32)
        if transposed:
            g = g.T
        delta, mn, vn = _adamw(w_ref[0], g, m_ref[0], v_ref[0])
        g_out[0] = g
        d_out[0] = delta
        m_out[0] = mn
        v_out[0] = vn

    passed = [] if outs is None else [pl.BlockSpec(memory_space=pl.ANY)] * 4
    shp = jax.ShapeDtypeStruct(w.shape, F32)
    return pl.pallas_call(
        body, name=name, grid=(steps,), out_shape=(shp,) * 4,
        in_specs=[pl.BlockSpec((N_DEV, tr, tc), theirs), pl.BlockSpec((tr, tc), mine), block, block, block] + passed,
        out_specs=(block,) * 4, input_output_aliases={5 + q: q for q in range(len(passed))},
        compiler_params=_params(("parallel",), VMEM_LIMIT),
    )(parts, own, w, m, v, *(outs or ()))


def _adam_ada_w(cond_t, dmod, w, m, v):
    ncol = w.shape[-1]
    tr = 512

    def body(c_ref, d_ref, w_ref, m_ref, v_ref, g_out, d_out, m_out, v_out):
        g = _nn(c_ref[...], d_ref[0])
        delta, mn, vn = _adamw(w_ref[0], g, m_ref[0], v_ref[0])
        g_out[0] = g
        d_out[0] = delta
        m_out[0] = mn
        v_out[0] = vn

    blk = pl.BlockSpec((1, tr, ncol), lambda l, i: (l, i, 0))
    shp = jax.ShapeDtypeStruct(w.shape, F32)
    return pl.pallas_call(
        body, name="adam_ada_w", grid=(DEPTH, D // tr), out_shape=(shp,) * 4,
        in_specs=[pl.BlockSpec((tr, LANES), lambda l, i: (i, 0)), pl.BlockSpec((1, LANES, ncol), lambda l, i: (l, 0, 0)), blk, blk, blk],
        out_specs=(blk,) * 4, compiler_params=_params(("parallel", "parallel"), VMEM_LIMIT),
    )(cond_t, dmod, w, m, v)


TILE_ROWS = 168


def _stat_sources():
    pairs = []
    for i in range(DEPTH):
        b = 32 * i
        for q, src in enumerate((b, b + 1, b + 8, b + 16, b + 17, b + 24)):
            pairs.append((6 * i + q, src))
        pairs.append((24 + i, b + 2))
        pairs.append((32 + i, b + 18))
    pairs += [(40, 128), (41, 129)]
    return pairs


def _small_exchange(tiles, w, m, v):
    loss_row, sink_row, sink_src = 41, 48, 136

    def body(s_ref, w_ref, m_ref, v_ref, dmod_out, g_out, d_out, m_out, v_out, loss_out, all_ref, tot_ref, send_sems, recv_sems):
        me = _my_index()
        all_ref[me] = s_ref[...]
        copies = []
        for k in range(1, N_DEV):
            dev, _ = _peer(k)
            cp = pltpu.make_async_remote_copy(src_ref=s_ref, dst_ref=all_ref.at[me], send_sem=send_sems.at[k - 1],
                                              recv_sem=recv_sems.at[k - 1], device_id=dev, device_id_type=MESH)
            cp.start()
            copies.append(cp)
        for k in range(1, N_DEV):
            dev, pidx = _peer(k)
            pltpu.make_async_remote_copy(src_ref=s_ref, dst_ref=all_ref.at[pidx], send_sem=send_sems.at[k - 1],
                                         recv_sem=recv_sems.at[k - 1], device_id=dev, device_id_type=MESH).wait_recv()
        for cp in copies:
            cp.wait_send()
        tot = all_ref[0]
        for j in range(1, N_DEV):
            tot = tot + all_ref[j]
        tot_ref[...] = tot
        g_out[...] = jnp.zeros_like(g_out)
        for dst, src in _stat_sources():
            g_out[pl.ds(dst, 1), :] = tot_ref[pl.ds(src, 1), :]
            if dst < 24:
                for j in range(N_DEV):
                    dmod_out[j, pl.ds(dst, 1), :] = all_ref[j, pl.ds(src, 1), :]
        lane = lax.broadcasted_iota(jnp.int32, (1, D), 1)
        sink = jnp.zeros((1, D), F32)
        for h in range(32):
            sink = jnp.where(lane == h, tot_ref[pl.ds(sink_src + h, 1), :], sink)
        g_out[pl.ds(sink_row, 1), :] = sink
        g = g_out[...]
        delta, mn, vn = _adamw(w_ref[...], g, m_ref[...], v_ref[...])
        d_out[...] = delta
        m_out[...] = mn
        v_out[...] = vn
        loss = jnp.sum(g[loss_row:loss_row + 1, :], axis=-1, keepdims=True) * (0.5 / D)
        loss_out[...] = jnp.broadcast_to(loss, loss_out.shape)

    vm = pl.BlockSpec(memory_space=pltpu.VMEM)
    shp = jax.ShapeDtypeStruct((STAT_ROWS, D), F32)
    return pl.pallas_call(
        body, name="small_exchange",
        out_shape=(jax.ShapeDtypeStruct((N_DEV, 24, D), F32), shp, shp, shp, shp, jax.ShapeDtypeStruct((8, LANES), F32)),
        in_specs=[vm] * 4, out_specs=(vm,) * 6,
        scratch_shapes=[pltpu.VMEM((N_DEV, TILE_ROWS, D), F32), pltpu.VMEM((TILE_ROWS, D), F32),
                        pltpu.SemaphoreType.DMA((N_DEV - 1,)), pltpu.SemaphoreType.DMA((N_DEV - 1,))],
        compiler_params=_params(vmem=VMEM_LIMIT),
    )(tiles, w, m, v)


def _to_rows(name, a):
    if name in ("ffn_in", "a_in", "b_in"):
        return a.T
    if name == "b_out":
        return a.T.reshape(-1, D)
    return a


def _rows8(a):
    return jnp.pad(a, ((0, 8 - a.shape[0]), (0, 0)))


def _pack_small(ada_b, norm_mix, norm_ffn, final_norm, sink):
    sink_row = jnp.pad(sink.reshape(1, -1), ((0, 0), (0, D - sink.size)))
    return jnp.concatenate([ada_b.reshape(24, D), _rows8(norm_mix), _rows8(norm_ffn), _rows8(final_norm.reshape(1, D)),
                            _rows8(sink_row)], axis=0)


def _unpack_small(a):
    return a[0:24].reshape(4, 6 * D), a[24:28], a[32:36], a[40], a[48, :32].reshape(2, 16)


def kernel(x, c, ada_w, ada_b, norm_mix, norm_ffn, ffn_w_in, ffn_w_out, a_w_in, a_w_out, a_sink, b_w_in, b_w_out, final_norm, loss_target, m_ada_w, m_ada_b, m_norm_mix, m_norm_ffn, m_ffn_w_in, m_ffn_w_out, m_a_w_in, m_a_w_out, m_a_sink, m_b_w_in, m_b_w_out, m_final_norm, v_ada_w, v_ada_b, v_norm_mix, v_norm_ffn, v_ffn_w_in, v_ffn_w_out, v_a_w_in, v_a_w_out, v_a_sink, v_b_w_in, v_b_w_out, v_final_norm):
    S = x.shape[1]
    x0 = x.reshape(S, D)
    target = loss_target.reshape(S, D)
    me = _my_index()
    ncol = ada_w.shape[-1]

    ada_b_mine = lax.dynamic_slice_in_dim(ada_b, me * ncol, ncol, axis=1)
    cond_all, parts = _cond_exchange(jnp.broadcast_to(c.reshape(1, D), (8, D)), ada_w, ada_b_mine)
    mod = lax.dynamic_index_in_dim(parts, me, axis=2, keepdims=False)
    mod = jnp.transpose(mod, (1, 0, 2)).reshape(DEPTH, 6, 1, D)

    weights = {"ffn_in": ffn_w_in, "ffn_out": ffn_w_out, "a_in": a_w_in, "a_out": a_w_out, "b_in": b_w_in, "b_out": b_w_out}
    shard = {(n, l): _to_rows(n, weights[n][l]).astype(BF16) for n, l, _ in SEGMENTS}
    first = [sg for sg in _layer_segments(0) if not sg[0].startswith("ffn")]
    gathered0 = _all_gather_weights([shard[(n, l)] for n, l, _ in first])
    W = {(n, l): g for (n, l, _), g in zip(first, gathered0)}
    groups = [[sg for sg in _layer_segments(0) if sg[0].startswith("ffn")]] + [_layer_segments(i) for i in range(1, DEPTH)]
    gathers, order = [], gathered0[0]
    for q, segs in enumerate(groups):
        zones = [(N_DEV, rows, D) for _, _, rows in segs]
        gathers.append(_exchange_start([shard[(n, l)] for n, l, _ in segs], zones, [(s, 0) for s in range(len(segs))],
                                       [sg[2] for sg in segs], False, order, "weight_gather_start_%d" % q))
        order = gathers[-1][-1]
    gather_token = order[0:1, 0:1]

    def finish_gather(q, after):
        segs = groups[q]
        zones = _exchange_wait(gathers[q], len(segs), [(s, 0) for s in range(len(segs))], [sg[2] for sg in segs], True, after,
                               "weight_gather_wait_%d" % q)
        for (n, l, rows), zone in zip(segs, zones):
            W[(n, l)] = zone.reshape(D, 512) if n == "b_out" else zone.reshape(N_DEV * rows, D)

    a_slopes, b_slopes = _slopes(16), _slopes(24)
    bias_a = _alibi_bias(a_slopes, A_HALF, 1)
    bias_b = [_alibi_bias(b_slopes[8 * g:8 * g + 8], B_HALF, dil) for g, dil in enumerate(B_DILS)]
    bias_b_fwd = [_alibi_bias(b_slopes[8 * g:8 * g + 8], B_HALF, dil, max(CHUNK, TQ * dil), both=True) for g, dil in enumerate(B_DILS)]
    a_geom = dict(C=A_QKV, r=1, half=A_HALF, qoff=0, koff=1024, voff=1280, n_units=2)
    b_geom = [dict(C=B_QKV, r=dil, half=B_HALF, qoff=512 * g, koff=1536 + 128 * g, voff=1920 + 128 * g, n_units=1)
              for g, dil in enumerate(B_DILS)]

    saved = []
    xcur = x0
    for i in range(DEPTH):
        j = i // 2
        sh1, sc1, g1, sh2, sc2, g2 = [mod[i, q] for q in range(6)]
        nm, nf = norm_mix[i].reshape(1, D), norm_ffn[i].reshape(1, D)
        if i == 0:
            nm = nm + gather_token
        if i > 0:
            finish_gather(i, xcur)
        if i % 2 == 0:
            sink_rep = jnp.repeat(jnp.repeat(a_sink[j], TQ).reshape(2, 1, 8 * TQ), 8, axis=1).reshape(16, 8 * TQ)
            h1, qkv = _proj(xcur, nm, sc1, sh1, W[("a_in", j)], name="proj_a")
            o, lse = _attn_fwd(qkv, bias_a, sink_rep, out_dtype=BF16, name="attn_a_fwd", **a_geom)
            if i == 0:
                finish_gather(0, o)
            x1, y1, h2, gu, act = _out_ffn_in(o, W[("a_out", j)], xcur, g1, nf, sc2, sh2, W[("ffn_in", i)],
                                              w_is_transposed=False, name="out_a_ffn_in")
        else:
            sink_rep = None
            h1, qkv = _proj(xcur, nm, sc1, sh1, W[("b_in", j)], name="proj_b")
            outs = [_attn_fwd(qkv, bias_b_fwd[g], None, out_dtype=F32, name="attn_b%d_fwd" % g, **b_geom[g]) for g in range(3)]
            o, lse = _merge_groups([t[0] for t in outs], [t[1] for t in outs])
            x1, y1, h2, gu, act = _out_ffn_in(o, W[("b_out", j)], xcur, g1, nf, sc2, sh2, W[("ffn_in", i)],
                                              w_is_transposed=True, name="out_b_ffn_in")
        if i < DEPTH - 1:
            x2, y2 = _ffn_out(act, W[("ffn_out", i)], x1, g2, None, None, name="ffn_out")
        else:
            x2, y2, head_stats = _ffn_out(act, W[("ffn_out", i)], x1, g2, target, final_norm.reshape(1, D), name="ffn_out_loss")
        saved.append(dict(x0=xcur, h1=h1, qkv=qkv, o=o, lse=lse, y1=y1, x1=x1, h2=h2, gu=gu, act=act, y2=y2, sink=sink_rep))
        xcur = x2

    dx = xcur

    dW = {}
    stat_tiles, dsink = [None] * DEPTH, [None] * 2
    exchanges = []
    start_token = None

    def start_exchange(segs):
        own = [lax.dynamic_slice_in_dim(dW[(n, l)], me * rows, rows, axis=0) for n, l, rows in segs]
        zones = [(N_DEV, rows, D) for _, _, rows in segs]
        started = _exchange_start([dW[(n, l)] for n, l, _ in segs], zones, [(s, 0) for s in range(len(segs))],
                                  [sg[2] for sg in segs], True, own[0], "grad_exchange_start_%d" % len(exchanges))
        exchanges.append((segs, started, own))
        return started[-1][0:1, 0:1]

    for i in reversed(range(DEPTH)):
        j = i // 2
        sv = saved[i]
        sh1, sc1, g1, sh2, sc2, g2 = [mod[i, q] for q in range(6)]
        if start_token is not None:
            g2 = g2 + start_token
            start_token = None
        nm, nf = norm_mix[i].reshape(1, D), norm_ffn[i].reshape(1, D)
        mix = "a_out" if i % 2 == 0 else "b_out"
        dgu, dy2, dx1, do, dy1, st_g2, st_f, st_g1 = _ffn_bwd(
            dx, sv["y2"], g2, W[("ffn_out", i)], sv["gu"], W[("ffn_in", i)], sv["x1"], nf, sc2, sv["y1"], g1, W[(mix, j)],
            mix_is_transposed=(i % 2 == 1), name="ffn_bwd_" + mix)
        dW[("ffn_out", i)] = _weight_grad(dy2, [sv["act"]], transpose_out=True, name="dw_ffn_out")
        dW[("ffn_in", i)] = _weight_grad(sv["h2"], [dgu], transpose_out=True, name="dw_ffn_in")
        sink_bwd = sv["sink"]
        if i == 0:
            sink_bwd = sink_bwd + start_exchange([sg for sg in _layer_segments(0) if sg[0].startswith("ffn")])
        if i % 2 == 0:
            dW[("a_out", j)] = _weight_grad(dy1, [sv["o"]], transpose_out=True, name="dw_a_out")
            dq, dk, dv, ds = _attn_bwd(sv["qkv"], bias_a, sink_bwd, sv["o"], do, sv["lse"], name="attn_a_bwd", **a_geom)
            dsink[j] = ds
            dqkv = [dq, dk[0], dk[1], dv[0], dv[1]]
            dW[("a_in", j)] = _weight_grad(sv["h1"], dqkv, transpose_out=True, name="dw_a_in")
            dx0, st_m = _norm_bwd(dqkv, W[("a_in", j)], sv["x0"], dx1, nm, sc1, name="proj_a_bwd")
        else:
            dW[("b_out", j)] = _weight_grad(dy1, [sv["o"]], transpose_out=False, name="dw_b_out").reshape(N_DEV * 64, D)
            gr = [_attn_bwd(sv["qkv"], bias_b[g], None, sv["o"], do, sv["lse"], name="attn_b%d_bwd" % g, **b_geom[g]) for g in range(3)]
            dqkv = [t[0] for t in gr] + [t[1][0] for t in gr] + [t[2][0] for t in gr]
            dW[("b_in", j)] = _weight_grad(sv["h1"], dqkv, transpose_out=True, name="dw_b_in")
            dx0, st_m = _norm_bwd(dqkv, W[("b_in", j)], sv["x0"], dx1, nm, sc1, name="proj_b_bwd")
        stat_tiles[i] = [st_m, st_g1, st_f, st_g2]
        if i > 0:
            start_token = start_exchange(_layer_segments(i))
        else:
            start_exchange([sg for sg in _layer_segments(0) if not sg[0].startswith("ffn")])
        dx = dx0
    grad_x = dx.reshape(1, S, D)

    masters = {"ffn_in": (ffn_w_in, m_ffn_w_in, v_ffn_w_in), "ffn_out": (ffn_w_out, m_ffn_w_out, v_ffn_w_out),
               "a_in": (a_w_in, m_a_w_in, v_a_w_in), "a_out": (a_w_out, m_a_w_out, v_a_w_out),
               "b_in": (b_w_in, m_b_w_in, v_b_w_in), "b_out": (b_w_out, m_b_w_out, v_b_w_out)}
    results = {n: None for n in masters}
    after = dx
    for e, (segs, started, own) in enumerate(exchanges):
        zones = _exchange_wait(started, len(segs), [(s, 0) for s in range(len(segs))], [sg[2] for sg in segs], False, after,
                               "grad_exchange_wait_%d" % e)
        for (n, l, rows), zone, mine in zip(segs, zones, own):
            if n == "b_out":
                zone, mine = zone.reshape(N_DEV, LANES, 512), mine.reshape(LANES, 512)
            results[n] = _adam_segment(zone, mine, *masters[n], results[n], l, transposed=n in ("ffn_in", "a_in", "b_in", "b_out"),
                                       name="adam_" + n)
            after = results[n][0]
    big = {(kind, n): results[n][q] for q, kind in enumerate(("grad", "delta", "m", "v")) for n in masters}

    tiles = jnp.concatenate([t for i in range(DEPTH) for t in stat_tiles[i]] + [head_stats]
                            + [jnp.pad(ds, ((0, 0), (0, D - LANES))) for ds in dsink], axis=0)
    small = [_pack_small(*t) for t in ((ada_b, norm_mix, norm_ffn, final_norm, a_sink),
                                       (m_ada_b, m_norm_mix, m_norm_ffn, m_final_norm, m_a_sink),
                                       (v_ada_b, v_norm_mix, v_norm_ffn, v_final_norm, v_a_sink))]
    dmod_all, sg, sd, sm, sv_, loss_tile = _small_exchange(tiles, *small)
    loss = loss_tile[0, 0]
    dmod_all = dmod_all.reshape(N_DEV, DEPTH, 6 * D)
    dmod_mine = lax.dynamic_slice_in_dim(dmod_all, me * ncol, ncol, axis=2)
    dmod_pad = jnp.pad(jnp.transpose(dmod_mine, (1, 0, 2)), ((0, 0), (0, LANES - N_DEV), (0, 0))).astype(BF16)
    cond_t = jnp.pad(cond_all.T, ((0, 0), (0, LANES - N_DEV))).astype(BF16)
    ada = _adam_ada_w(cond_t, dmod_pad, ada_w, m_ada_w, v_ada_w)

    outs = [loss, grad_x]
    small_res = [_unpack_small(t) for t in (sg, sd, sm, sv_)]
    for q, kind in enumerate(("grad", "delta", "m", "v")):
        ab, nm_, nf_, fn, sk = small_res[q]
        outs += [ada[q], ab, nm_, nf_, big[(kind, "ffn_in")], big[(kind, "ffn_out")], big[(kind, "a_in")], big[(kind, "a_out")],
                 sk, big[(kind, "b_in")], big[(kind, "b_out")], fn]
    return tuple(outs)
```

```python
import functools
import math

import numpy as np
import jax
import jax.numpy as jnp
from jax import lax
from jax.experimental import pallas as pl
from jax.experimental.pallas import tpu as pltpu

D = 1024
HEAD_DIM = 64
D_FF = 2816
DEPTH = 4
N_DEV = 8
A_QKV = 1536
B_QKV = 2304
A_HALF = 128
B_HALF = 64
B_DILS = (1, 4, 16)
RMS_EPS = 1e-6
NEG = -1e30
ADAM_LR = 0.001
ADAM_B1 = 0.9
ADAM_B2 = 0.999
ADAM_EPS = 1e-08
ADAM_WD = 0.01
ADAM_STEP = 10

LANES = 128
SPLIT = 2
TQ = 128
VMEM_LIMIT = 56 * 1024 * 1024
MESH = pl.DeviceIdType.MESH
F32 = jnp.float32
BF16 = jnp.bfloat16

SEGMENTS = ([("ffn_in", l, 704) for l in range(4)] + [("ffn_out", l, 352) for l in range(4)]
            + [("a_in", j, 192) for j in range(2)] + [("a_out", j, 128) for j in range(2)]
            + [("b_in", j, 288) for j in range(2)] + [("b_out", j, 64) for j in range(2)])
def _layer_segments(i):
    mixer = "a" if i % 2 == 0 else "b"
    return [s for s in SEGMENTS if (s[0].startswith("ffn") and s[1] == i) or (s[0].startswith(mixer + "_") and s[1] == i // 2)]


def _offsets(segs):
    rows = [s[2] for s in segs]
    return [sum(rows[:k]) for k in range(len(rows))], sum(rows)
STAT_ROWS = 56


def _nn(a, b):
    return jnp.dot(a, b, preferred_element_type=F32)


def _nt(a, b):
    return lax.dot_general(a, b, (((1,), (1,)), ((), ())), preferred_element_type=F32)


def _tn(a, b):
    return lax.dot_general(a, b, (((0,), (0,)), ((), ())), preferred_element_type=F32)


def _params(dims=None, vmem=None):
    kw = {}
    if dims is not None:
        kw["dimension_semantics"] = dims
    if vmem is not None:
        kw["vmem_limit_bytes"] = vmem
    return pltpu.CompilerParams(**kw)


def _my_index():
    return 4 * lax.axis_index("x") + 2 * lax.axis_index("y") + lax.axis_index("c")


def _peer(k):
    x, y, c = lax.axis_index("x"), lax.axis_index("y"), lax.axis_index("c")
    px, py, pc = x ^ ((k >> 2) & 1), y ^ ((k >> 1) & 1), c ^ (k & 1)
    return (px, py, pc), 4 * px + 2 * py + pc


def _const_spec(shape):
    nd = len(shape)
    return pl.BlockSpec(shape, lambda *_: (0,) * nd)


def _cond_exchange(c_tile, ada_w, ada_b_mine):
    ncol = ada_w.shape[-1]

    def body(c_ref, w_ref, b_ref, cond_ref, parts_ref, call_ref, mine_ref, send_sems, recv_sems):
        me = _my_index()
        call_ref[me] = c_ref[...]
        copies = []
        for k in range(1, N_DEV):
            dev, _ = _peer(k)
            cp = pltpu.make_async_remote_copy(src_ref=c_ref, dst_ref=call_ref.at[me], send_sem=send_sems.at[0, k - 1],
                                              recv_sem=recv_sems.at[0, k - 1], device_id=dev, device_id_type=MESH)
            cp.start()
            copies.append(cp)
        for k in range(1, N_DEV):
            _, pidx = _peer(k)
            pltpu.make_async_remote_copy(src_ref=c_ref, dst_ref=call_ref.at[pidx], send_sem=send_sems.at[0, k - 1],
                                         recv_sem=recv_sems.at[0, k - 1], device_id=_peer(k)[0], device_id_type=MESH).wait_recv()
        for cp in copies:
            cp.wait_send()
        row = lax.broadcasted_iota(jnp.int32, (N_DEV, D), 0)
        cmat = jnp.zeros((N_DEV, D), F32)
        for j in range(N_DEV):
            cmat = jnp.where(row == j, call_ref[j], cmat)
        cond = cmat * jax.nn.sigmoid(cmat)
        cond_ref[...] = cond
        cb = cond.astype(BF16)
        for l in range(DEPTH):
            mine_ref[l] = _nn(cb, w_ref[l].astype(BF16)) + b_ref[pl.ds(l, 1), :]
        parts_ref[me] = mine_ref[...]
        copies = []
        for k in range(1, N_DEV):
            dev, _ = _peer(k)
            cp = pltpu.make_async_remote_copy(src_ref=mine_ref, dst_ref=parts_ref.at[me], send_sem=send_sems.at[1, k - 1],
                                              recv_sem=recv_sems.at[1, k - 1], device_id=dev, device_id_type=MESH)
            cp.start()
            copies.append(cp)
        for k in range(1, N_DEV):
            dev, pidx = _peer(k)
            pltpu.make_async_remote_copy(src_ref=mine_ref, dst_ref=parts_ref.at[pidx], send_sem=send_sems.at[1, k - 1],
                                         recv_sem=recv_sems.at[1, k - 1], device_id=dev, device_id_type=MESH).wait_recv()
        for cp in copies:
            cp.wait_send()

    vm = pl.BlockSpec(memory_space=pltpu.VMEM)
    return pl.pallas_call(
        body, name="cond_exchange",
        out_shape=(jax.ShapeDtypeStruct((N_DEV, D), F32), jax.ShapeDtypeStruct((N_DEV, DEPTH, N_DEV, ncol), F32)),
        in_specs=[vm, vm, vm], out_specs=(vm, vm),
        scratch_shapes=[pltpu.VMEM((N_DEV, N_DEV, D), F32), pltpu.VMEM((DEPTH, N_DEV, ncol), F32),
                        pltpu.SemaphoreType.DMA((2, N_DEV - 1)), pltpu.SemaphoreType.DMA((2, N_DEV - 1))],
        compiler_params=_params(vmem=VMEM_LIMIT),
    )(c_tile, ada_w, ada_b_mine)[:2]


def _all_gather_weights(shards):
    n = len(shards)
    big = max(range(n), key=lambda s: shards[s].shape[0])
    total = sum(sh.shape[0] for sh in shards)
    assert N_DEV * shards[big].shape[0] >= total

    def body(*refs):
        ins, outs = refs[:n], refs[n:2 * n]
        local_sems, send_sems, recv_sems = refs[2 * n:]
        me = _my_index()
        local = []
        for s in range(n):
            rows = ins[s].shape[0]
            cp = pltpu.make_async_copy(ins[s], outs[s].at[pl.ds(me * rows, rows)], local_sems.at[s])
            cp.start()
            local.append(cp)
        for k in range(1, N_DEV):
            dev, _ = _peer(k)
            for s in range(n):
                rows = ins[s].shape[0]
                pltpu.make_async_remote_copy(src_ref=ins[s], dst_ref=outs[s].at[pl.ds(me * rows, rows)],
                                             send_sem=send_sems.at[k - 1], recv_sem=recv_sems.at[k - 1],
                                             device_id=dev, device_id_type=MESH).start()
        whole = outs[big].at[pl.ds(0, total)]
        for k in range(1, N_DEV):
            dev, _ = _peer(k)
            w = pltpu.make_async_remote_copy(src_ref=whole, dst_ref=whole, send_sem=send_sems.at[k - 1],
                                             recv_sem=recv_sems.at[k - 1], device_id=dev, device_id_type=MESH)
            w.wait_send()
            w.wait_recv()
        for cp in local:
            cp.wait()

    hbm = pl.BlockSpec(memory_space=pl.ANY)
    return pl.pallas_call(
        body, name="weight_all_gather",
        out_shape=tuple(jax.ShapeDtypeStruct((N_DEV * s.shape[0], D), s.dtype) for s in shards),
        in_specs=[hbm] * n, out_specs=tuple([hbm] * n),
        scratch_shapes=[pltpu.SemaphoreType.DMA((n,)), pltpu.SemaphoreType.DMA((N_DEV - 1,)),
                        pltpu.SemaphoreType.DMA((N_DEV - 1,))],
    )(*shards)


HBM = pl.BlockSpec(memory_space=pltpu.HBM)
SEM = pl.BlockSpec(memory_space=pltpu.SEMAPHORE)
EFFECT = pltpu.SideEffectType.DATAFLOW_SIDE_EFFECTING


def _exchange_start(srcs, landings, dst, rows, to_peer_rows, after, name):
    n, nl = len(srcs), len(landings)

    def body(*refs):
        src_refs = refs[:n]
        send_sems, recv_sems = refs[n + 1], refs[n + 2]
        land_refs = refs[2 * n + 3:2 * n + 3 + nl]
        token = refs[-1]
        me = _my_index()
        for k in range(1, N_DEV):
            dev, pidx = _peer(k)
            for q in range(n):
                src = src_refs[q].at[pl.ds(pidx * rows[q], rows[q])] if to_peer_rows else src_refs[q]
                pltpu.make_async_remote_copy(src_ref=src, dst_ref=land_refs[dst[q][0]].at[me, pl.ds(dst[q][1], rows[q])],
                                             send_sem=send_sems.at[k * n + q], recv_sem=recv_sems.at[k * n + q],
                                             device_id=dev, device_id_type=MESH).start()
        if not to_peer_rows:
            for q in range(n):
                pltpu.make_async_copy(src_refs[q], land_refs[dst[q][0]].at[me, pl.ds(dst[q][1], rows[q])], send_sems.at[q]).start()
        token[...] = jnp.zeros_like(token)

    sems = pltpu.SemaphoreType.DMA((N_DEV * n,))
    return pl.pallas_call(
        body, name=name,
        out_shape=(sems, sems, *[pltpu.HBM(a.shape, a.dtype) for a in srcs], *[pltpu.HBM(shape, BF16) for shape in landings],
                   jax.ShapeDtypeStruct((8, LANES), F32)),
        in_specs=[HBM] * n + [pl.BlockSpec(memory_space=pl.ANY)],
        out_specs=(SEM, SEM, *[HBM] * (n + nl), pl.BlockSpec(memory_space=pltpu.VMEM)),
        input_output_aliases={q: 2 + q for q in range(n)},
        compiler_params=pltpu.CompilerParams(has_side_effects=EFFECT),
    )(*[pltpu.with_memory_space_constraint(a, pltpu.HBM) for a in srcs], after)


def _exchange_wait(started, n, dst, rows, own_slot, after, name):
    send_sems, recv_sems = started[0], started[1]
    arrays = list(started[2:-1])
    n1 = len(arrays)

    def body(*refs):
        land_refs = refs[n:n1]
        sends, recvs = refs[n1], refs[n1 + 1]
        for k in range(1, N_DEV):
            dev, _ = _peer(k)
            for q in range(n):
                slot = land_refs[dst[q][0]].at[0, pl.ds(dst[q][1], rows[q])]
                w = pltpu.make_async_remote_copy(src_ref=slot, dst_ref=slot, send_sem=sends.at[k * n + q],
                                                 recv_sem=recvs.at[k * n + q], device_id=dev, device_id_type=MESH)
                w.wait_send()
                w.wait_recv()
        if own_slot:
            for q in range(n):
                slot = land_refs[dst[q][0]].at[0, pl.ds(dst[q][1], rows[q])]
                pltpu.make_async_copy(slot, slot, sends.at[q]).wait()

    return pl.pallas_call(
        body, name=name, out_shape=tuple(pltpu.HBM(a.shape, a.dtype) for a in arrays),
        in_specs=[HBM] * n1 + [SEM, SEM, pl.BlockSpec(memory_space=pl.ANY)], out_specs=tuple([HBM] * n1),
        input_output_aliases={q: q for q in range(n1)},
        compiler_params=pltpu.CompilerParams(has_side_effects=EFFECT),
    )(*arrays, send_sems, recv_sems, after)[n:]


def _norm_mod(x, nw, sc, sh):
    ms = jnp.mean(x * x, axis=-1, keepdims=True)
    xh = x * lax.rsqrt(ms + RMS_EPS)
    return xh, (xh * nw) * (1.0 + sc) + sh


def _proj(x, nw, sc, sh, wt, *, name):
    S, N = x.shape[0], wt.shape[0]
    tm = 512

    def body(x_ref, nw_ref, sc_ref, sh_ref, w_ref, h_ref, out_ref):
        for half in range(SPLIT):
            rows = pl.ds(half * (tm // SPLIT), tm // SPLIT)
            _, h = _norm_mod(x_ref[rows, :], nw_ref[...], sc_ref[...], sh_ref[...])
            hb = h.astype(BF16)
            h_ref[rows, :] = hb
            out_ref[rows, :] = _nt(hb, w_ref[...]).astype(BF16)

    row = lambda w: pl.BlockSpec((tm, w), lambda i: (i, 0))
    vec = _const_spec((1, D))
    return pl.pallas_call(
        body, name=name, grid=(S // tm,), out_shape=(jax.ShapeDtypeStruct((S, D), BF16), jax.ShapeDtypeStruct((S, N), BF16)),
        in_specs=[row(D), vec, vec, vec, _const_spec((N, D))], out_specs=(row(D), row(N)),
        compiler_params=_params(("parallel",), VMEM_LIMIT),
    )(x, nw, sc, sh, wt)


def _ffn_out(a, w, x, g, target, fnw, *, name):
    S, K = a.shape
    tm = 512
    last = target is not None

    def body(a_ref, w_ref, x_ref, g_ref, *rest):
        y = _nn(a_ref[...], w_ref[...])
        xv = x_ref[...] + g_ref[...] * y
        if not last:
            xo_ref, y_ref = rest
            y_ref[...] = y.astype(BF16)
            xo_ref[...] = xv
            return
        t_ref, fw_ref, dx_ref, y_ref, st_ref = rest
        y_ref[...] = y.astype(BF16)

        @pl.when(pl.program_id(0) == 0)
        def _():
            st_ref[...] = jnp.zeros_like(st_ref)

        rstd = lax.rsqrt(jnp.mean(xv * xv, axis=-1, keepdims=True) + RMS_EPS)
        xh = xv * rstd
        err = xh * fw_ref[...] - t_ref[...]
        dy = err * (1.0 / D)
        dxh = dy * fw_ref[...]
        dx_ref[...] = rstd * (dxh - xh * jnp.mean(dxh * xh, axis=-1, keepdims=True))
        st_ref[pl.ds(0, 1), :] = st_ref[pl.ds(0, 1), :] + jnp.sum(dy * xh, axis=0, keepdims=True)
        st_ref[pl.ds(1, 1), :] = st_ref[pl.ds(1, 1), :] + jnp.sum(err * err, axis=0, keepdims=True)

    row = lambda w_: pl.BlockSpec((tm, w_), lambda i: (i, 0))
    in_specs = [row(K), _const_spec(w.shape), row(D), _const_spec((1, D))]
    args = [a, w, x, g]
    out_shape = [jax.ShapeDtypeStruct((S, D), F32), jax.ShapeDtypeStruct((S, D), BF16)]
    out_specs = [row(D), row(D)]
    if last:
        in_specs += [row(D), _const_spec((1, D))]
        args += [target, fnw]
        out_shape.append(jax.ShapeDtypeStruct((8, D), F32))
        out_specs.append(_const_spec((8, D)))
    return pl.pallas_call(
        body, name=name, grid=(S // tm,), out_shape=tuple(out_shape), in_specs=in_specs, out_specs=tuple(out_specs),
        compiler_params=_params(("arbitrary",) if last else ("parallel",), VMEM_LIMIT),
    )(*args)


CHUNK = 1024


def _tile_rows(r, chunk=CHUNK):
    return min(TQ, chunk // r)


def _out_ffn_in(a, w_mix, x, g, nw, sc, sh, wt, *, w_is_transposed, name):
    S, K = a.shape
    tm = 256

    def body(a_ref, wm_ref, x_ref, g_ref, nw_ref, sc_ref, sh_ref, w_ref, x1_ref, y_ref, h_ref, gu_ref, act_ref):
        y = _nt(a_ref[...], wm_ref[...]) if w_is_transposed else _nn(a_ref[...], wm_ref[...])
        y_ref[...] = y.astype(BF16)
        x1 = x_ref[...] + g_ref[...] * y
        x1_ref[...] = x1
        _, h = _norm_mod(x1, nw_ref[...], sc_ref[...], sh_ref[...])
        hb = h.astype(BF16)
        h_ref[...] = hb
        gate = _nt(hb, w_ref[pl.ds(0, D_FF), :])
        up = _nt(hb, w_ref[pl.ds(D_FF, D_FF), :])
        sig = jax.nn.sigmoid(gate)
        silu = gate * sig
        gu_ref[:, pl.ds(0, D_FF)] = (up * (sig * (1.0 + gate * (1.0 - sig)))).astype(BF16)
        gu_ref[:, pl.ds(D_FF, D_FF)] = silu.astype(BF16)
        act_ref[...] = (silu * up).astype(BF16)

    row = lambda w_: pl.BlockSpec((tm, w_), lambda i: (i, 0))
    vec = _const_spec((1, D))
    return pl.pallas_call(
        body, name=name, grid=(S // tm,),
        out_shape=(jax.ShapeDtypeStruct((S, D), F32), jax.ShapeDtypeStruct((S, D), BF16), jax.ShapeDtypeStruct((S, D), BF16),
                   jax.ShapeDtypeStruct((S, 2 * D_FF), BF16), jax.ShapeDtypeStruct((S, D_FF), BF16)),
        in_specs=[row(K), _const_spec(w_mix.shape), row(D), vec, vec, vec, vec, _const_spec(wt.shape)],
        out_specs=(row(D), row(D), row(D), row(2 * D_FF), row(D_FF)),
        compiler_params=_params(("parallel",), VMEM_LIMIT),
    )(a, w_mix, x, g, nw, sc, sh, wt)


def _alibi_bias(slopes, half, dil, chunk=CHUNK, both=False):
    tq = _tile_rows(dil, chunk)
    tk = tq + 2 * half
    rel = np.arange(tk)[:, None] - half - np.arange(tq)[None, :]
    band = np.abs(rel) <= half
    dist = (dil * np.abs(rel)).astype(np.float32)
    tabs = [np.where(band, -np.float32(s) * dist, np.float32(NEG)).astype(np.float32) for s in slopes]
    out = []
    for u in range(0, len(tabs), 8):
        tab = np.concatenate(tabs[u:u + 8], axis=1)
        first, last = tab.copy(), tab.copy()
        first[:half] = NEG
        last[tk - half:] = NEG
        out += [tab, first, last]
        if both:
            last = last.copy()
            last[:half] = NEG
            out.append(last)
    return jnp.asarray(np.concatenate(out, axis=0))


def _slopes(n):
    return (2.0 ** (-8.0 * np.arange(1, n + 1) / n)).astype(np.float32)


def _head_masks(tq):
    lane = lax.broadcasted_iota(jnp.int32, (tq, LANES), 1)
    lo = lane < HEAD_DIM
    return lo, jnp.logical_not(lo)


def _stack_heads(tiles, lo, hi, scale):
    blocks = []
    for t in range(4):
        xf = tiles[t] if scale == 1.0 else tiles[t] * scale
        for a in range(2):
            xm = jnp.where(lo if a == 0 else hi, xf, 0.0)
            if a != t // 2:
                xm = pltpu.roll(xm, HEAD_DIM, 1)
            blocks.append(xm.astype(BF16))
    return jnp.concatenate(blocks, axis=0)


def _tile_from_columns(x8t, t, tq):
    r0 = HEAD_DIM * (t // 2)
    top = x8t[r0:r0 + HEAD_DIM, 2 * t * tq:(2 * t + 1) * tq]
    bot = x8t[r0:r0 + HEAD_DIM, (2 * t + 1) * tq:(2 * t + 2) * tq]
    return jnp.concatenate([top, bot], axis=0).T


def _attn_layout(S, C, r, half, qoff, koff, voff, chunk):
    hb = half * r
    per = chunk // hb
    nhb = S // hb
    main = lambda off: pl.BlockSpec((chunk, LANES), lambda u, i: (i, off // LANES + u))
    prev = lambda off: pl.BlockSpec((hb, LANES), lambda u, i: (jnp.maximum(i * per - 1, 0), off // LANES + u))
    nxt = lambda off: pl.BlockSpec((hb, LANES), lambda u, i: (jnp.minimum((i + 1) * per, nhb - 1), off // LANES + u))
    specs = [pl.BlockSpec((chunk, 4 * LANES), lambda u, i: (i, qoff // (4 * LANES) + u))]
    specs += [prev(koff), main(koff), nxt(koff), prev(voff), main(voff), nxt(voff)]
    return specs, hb


def _stage(dst, srcs):
    row = 0
    for src in srcs:
        n = src.shape[0]
        dst[pl.ds(row, n), :] = src[...].astype(F32)
        row += n


def _rows(start, n, r):
    return pl.ds(start, n, stride=r) if r > 1 else pl.ds(start, n)


def _attn_fwd(qkv, bias, sink, *, C, r, half, qoff, koff, voff, n_units, out_dtype, name):
    S = qkv.shape[0]
    chunk = max(CHUNK, TQ * r)
    tq = _tile_rows(r, chunk)
    tk = tq + 2 * half
    tiles = chunk // (r * tq)
    nsteps = S // chunk
    specs, hb = _attn_layout(S, C, r, half, qoff, koff, voff, chunk)
    use_sink = sink is not None

    def body(*refs):
        q_ref, kp, km, kn, vp, vm, vn, bias_ref = refs[:8]
        rest = list(refs[8:])
        sink_ref = rest.pop(0) if use_sink else None
        o_ref, lse_ref, qs, ks, vs, os_, ls = rest
        i = pl.program_id(1)
        if r > 1:
            for t in range(4):
                qs[t] = q_ref[:, pl.ds(t * LANES, LANES)].astype(F32)
        _stage(ks, [kp, km, kn])
        _stage(vs, [vp, vm, vn])
        lo, hi = _head_masks(tq)

        def tile_in(staged, ref, t, start):
            if r > 1:
                return staged[t, _rows(start, tq, r), :]
            return ref[pl.ds(start, tq), pl.ds(t * LANES, LANES)].astype(F32)

        ones = jnp.ones((16, tk), BF16)
        if use_sink:
            sk = sink_ref[pl.ds(0, 1), :]

        def chain(n, carry):
            rho, c = n // tiles, n % tiles
            start = c * (tq * r) + rho
            if r == 1:
                start = pl.multiple_of(start, tq)
            variant = jnp.where(jnp.logical_and(i == 0, c == 0), 1, 0) + jnp.where(
                jnp.logical_and(i == nsteps - 1, c == tiles - 1), 2, 0)
            k2 = ks[_rows(start, tk, r), :].astype(BF16)
            v2t = jnp.concatenate([vs[_rows(start, tk, r), :].T.astype(BF16), ones], axis=0)
            q8 = _stack_heads([tile_in(qs, q_ref, t, start) for t in range(4)], lo, hi, HEAD_DIM ** -0.5)
            s = _nt(k2, q8) + bias_ref[pl.ds(pl.multiple_of(variant * tk, 8), tk), :]
            m = jnp.max(s, axis=0, keepdims=True)
            if use_sink:
                m = jnp.maximum(m, sk)
            pv = _nn(v2t, jnp.exp(s - m).astype(BF16))
            l = pv[LANES:LANES + 1]
            if use_sink:
                l = l + jnp.exp(sk - m)
            o8t = pv[:LANES] / l
            lse8 = jnp.broadcast_to(m + jnp.log(l), (LANES, 8 * tq))
            for t in range(4):
                if r > 1:
                    os_[t, _rows(start, tq, r), :] = _tile_from_columns(o8t, t, tq)
                    ls[t, _rows(start, tq, r), :] = _tile_from_columns(lse8, t, tq)
                else:
                    o_ref[pl.ds(start, tq), pl.ds(t * LANES, LANES)] = _tile_from_columns(o8t, t, tq).astype(out_dtype)
                    lse_ref[pl.ds(start, tq), pl.ds(t * LANES, LANES)] = _tile_from_columns(lse8, t, tq)
            return carry

        lax.fori_loop(0, r * tiles, chain, 0, unroll=4)
        if r > 1:
            for t in range(4):
                o_ref[:, pl.ds(t * LANES, LANES)] = os_[t].astype(out_dtype)
                lse_ref[:, pl.ds(t * LANES, LANES)] = ls[t]

    in_specs = specs + [pl.BlockSpec((bias.shape[0] // n_units, 8 * tq), lambda u, i: (u, 0))]
    args = [qkv] * 7 + [bias]
    if use_sink:
        in_specs.append(pl.BlockSpec((8, 8 * tq), lambda u, i: (u, 0)))
        args.append(sink)
    wide = pl.BlockSpec((chunk, 4 * LANES), lambda u, i: (i, u))
    win = hb + chunk + hb
    big = lambda: pltpu.VMEM((4, chunk if r > 1 else 8, LANES), F32)
    return pl.pallas_call(
        body, name=name, grid=(n_units, nsteps),
        out_shape=(jax.ShapeDtypeStruct((S, n_units * 512), out_dtype), jax.ShapeDtypeStruct((S, n_units * 512), F32)),
        in_specs=in_specs, out_specs=(wide, wide),
        scratch_shapes=[big(), pltpu.VMEM((win, LANES), F32), pltpu.VMEM((win, LANES), F32), big(), big()],
        compiler_params=_params(("parallel", "parallel"), VMEM_LIMIT),
    )(*args)


def _attn_bwd(qkv, bias, sink, o, do, lse, *, C, r, half, qoff, koff, voff, n_units, name):
    S = qkv.shape[0]
    tq = _tile_rows(r)
    tk = tq + 2 * half
    tiles = CHUNK // (r * tq)
    nsteps = S // CHUNK
    specs, hb = _attn_layout(S, C, r, half, qoff, koff, voff, CHUNK)
    use_sink = sink is not None

    def body(*refs):
        q_ref, kp, km, kn, vp, vm, vn, bias_ref = refs[:8]
        rest = list(refs[8:])
        sink_ref = rest.pop(0) if use_sink else None
        o_ref, do_ref, lse_ref, dq_ref, dk_hbm, dv_hbm = rest[:6]
        rest = rest[6:]
        dsink_ref = rest.pop(0) if use_sink else None
        qs, ks, vs, os_, dos, ls, dqs, acck, accv, sem = rest
        u, i = pl.program_id(0), pl.program_id(1)

        @pl.when(i == 0)
        def _():
            acck[...] = jnp.zeros_like(acck)
            accv[...] = jnp.zeros_like(accv)
            if use_sink:
                dsink_ref[...] = jnp.zeros_like(dsink_ref)

        if r > 1:
            for t in range(4):
                cols = pl.ds(t * LANES, LANES)
                qs[t] = q_ref[:, cols].astype(F32)
                os_[t] = o_ref[:, cols].astype(F32)
                dos[t] = do_ref[:, cols].astype(F32)
                ls[t] = lse_ref[:, cols]
        _stage(ks, [kp, km, kn])
        _stage(vs, [vp, vm, vn])
        lo, hi = _head_masks(tq)

        def tile_in(staged, ref, t, start):
            if r > 1:
                return staged[t, _rows(start, tq, r), :]
            return ref[pl.ds(start, tq), pl.ds(t * LANES, LANES)].astype(F32)

        base = pl.multiple_of(i * CHUNK, CHUNK)
        if use_sink:
            sk = sink_ref[pl.ds(0, 1), :]

        def chain(n, carry):
            rho, c = n // tiles, n % tiles
            start = c * (tq * r) + rho
            if r == 1:
                start = pl.multiple_of(start, tq)
            variant = jnp.where(jnp.logical_and(i == 0, c == 0), 1, 0) + jnp.where(
                jnp.logical_and(i == nsteps - 1, c == tiles - 1), 2, 0)
            k2 = ks[_rows(start, tk, r), :].astype(BF16)
            v2 = vs[_rows(start, tk, r), :].astype(BF16)
            k2t = ks[_rows(start, tk, r), :].T.astype(BF16)
            q8 = _stack_heads([tile_in(qs, q_ref, t, start) for t in range(4)], lo, hi, HEAD_DIM ** -0.5)
            do_tiles = [tile_in(dos, do_ref, t, start) for t in range(4)]
            do8 = _stack_heads(do_tiles, lo, hi, 1.0)
            deltas, lses = [], []
            for t in range(4):
                prod_t = (do_tiles[t] * tile_in(os_, o_ref, t, start)).T
                lse_t = tile_in(ls, lse_ref, t, start).T
                for a in range(2):
                    deltas.append(jnp.sum(prod_t[a * HEAD_DIM:(a + 1) * HEAD_DIM], axis=0, keepdims=True))
                    lses.append(lse_t[a * HEAD_DIM:a * HEAD_DIM + 1])
            delta8 = jnp.concatenate(deltas, axis=1)
            lse8 = jnp.concatenate(lses, axis=1)
            s = _nt(k2, q8) + bias_ref[pl.ds(pl.multiple_of(variant * tk, 8), tk), :]
            p = jnp.exp(s - lse8)
            dp = _nt(v2, do8)
            dsb = (p * (dp - delta8)).astype(BF16)
            dq8t = _nn(k2t, dsb)
            for t in range(4):
                dq_t = _tile_from_columns(dq8t, t, tq) * (HEAD_DIM ** -0.5)
                if r > 1:
                    dqs[t, _rows(start, tq, r), :] = dq_t
                else:
                    dq_ref[pl.ds(start, tq), pl.ds(t * LANES, LANES)] = dq_t.astype(BF16)
            arow = base + start
            if r == 1:
                arow = pl.multiple_of(arow, tq)
            acck[_rows(arow, tk, r), :] = acck[_rows(arow, tk, r), :] + _nn(dsb, q8)
            accv[_rows(arow, tk, r), :] = accv[_rows(arow, tk, r), :] + _nn(p.astype(BF16), do8)
            if use_sink:
                e = jnp.exp(sk - lse8) * delta8
                for h in range(8):
                    part = -jnp.sum(e[:, h * tq:(h + 1) * tq], axis=1, keepdims=True)
                    dsink_ref[pl.ds(h, 1), :] = dsink_ref[pl.ds(h, 1), :] + part
            return carry

        lax.fori_loop(0, r * tiles, chain, 0, unroll=2)
        if r > 1:
            for t in range(4):
                dq_ref[:, pl.ds(t * LANES, LANES)] = dqs[t].astype(BF16)

        @pl.when(i == nsteps - 1)
        def _():
            ck = pltpu.make_async_copy(acck.at[pl.ds(hb, S)], dk_hbm.at[u], sem.at[0])
            cv = pltpu.make_async_copy(accv.at[pl.ds(hb, S)], dv_hbm.at[u], sem.at[1])
            ck.start()
            cv.start()
            ck.wait()
            cv.wait()

    wide = pl.BlockSpec((CHUNK, 4 * LANES), lambda u, i: (i, u))
    hbm = pl.BlockSpec(memory_space=pl.ANY)
    in_specs = specs + [pl.BlockSpec((3 * tk, 8 * tq), lambda u, i: (u, 0))]
    args = [qkv] * 7 + [bias]
    if use_sink:
        in_specs.append(pl.BlockSpec((8, 8 * tq), lambda u, i: (u, 0)))
        args.append(sink)
    in_specs += [wide, wide, wide]
    args += [o, do, lse]
    out_shape = [jax.ShapeDtypeStruct((S, n_units * 512), BF16), jax.ShapeDtypeStruct((n_units, S, LANES), F32),
                 jax.ShapeDtypeStruct((n_units, S, LANES), F32)]
    out_specs = [wide, hbm, hbm]
    if use_sink:
        out_shape.append(jax.ShapeDtypeStruct((n_units * 8, LANES), F32))
        out_specs.append(pl.BlockSpec((8, LANES), lambda u, i: (u, 0)))
    win = hb + CHUNK + hb
    big = lambda: pltpu.VMEM((4, CHUNK if r > 1 else 8, LANES), F32)
    res = pl.pallas_call(
        body, name=name, grid=(n_units, nsteps), out_shape=tuple(out_shape), in_specs=in_specs, out_specs=tuple(out_specs),
        scratch_shapes=[big(), pltpu.VMEM((win, LANES), F32), pltpu.VMEM((win, LANES), F32), big(), big(), big(), big(),
                        pltpu.VMEM((S + 2 * hb, LANES), F32), pltpu.VMEM((S + 2 * hb, LANES), F32), pltpu.SemaphoreType.DMA((2,))],
        compiler_params=_params(("arbitrary", "arbitrary"), VMEM_LIMIT),
    )(*args)
    return res[0], res[1], res[2], (res[3] if use_sink else None)


def _merge_groups(os_, lses):
    S, W = os_[0].shape
    tm = 512

    def body(o0, o1, o2, l0, l1, l2, o_ref, lse_ref):
        ls = [l0[...], l1[...], l2[...]]
        mx = jnp.maximum(jnp.maximum(ls[0], ls[1]), ls[2])
        es = [jnp.exp(l - mx) for l in ls]
        den = es[0] + es[1] + es[2]
        o = (es[0] / den) * o0[...] + (es[1] / den) * o1[...] + (es[2] / den) * o2[...]
        o_ref[...] = o.astype(BF16)
        lse_ref[...] = mx + jnp.log(den)

    row = pl.BlockSpec((tm, W), lambda i: (i, 0))
    return pl.pallas_call(
        body, name="merge_groups", grid=(S // tm,),
        out_shape=(jax.ShapeDtypeStruct((S, W), BF16), jax.ShapeDtypeStruct((S, W), F32)),
        in_specs=[row] * 6, out_specs=(row, row), compiler_params=_params(("parallel",), VMEM_LIMIT),
    )(*os_, *lses)


def _columns(part_refs, rows):
    return jnp.concatenate([r[rows, :].astype(BF16) for r in part_refs], axis=1)


def _piece_specs(parts, rows, row_axis):
    arrays, specs = [], []
    for p in parts:
        if isinstance(p, tuple):
            arr, u = p
            index = (lambda *g, u=u: (u, g[row_axis], 0))
            arrays.append(arr)
            specs.append(pl.BlockSpec((None, rows, arr.shape[2]), index))
        else:
            arrays.append(p)
            specs.append(pl.BlockSpec((rows, p.shape[1]), lambda *g: (g[row_axis], 0)))
    return arrays, specs


def _norm_bwd(dy_parts, wt, x, dres, nw, sc, *, name):
    S, N = x.shape[0], wt.shape[0]
    npart = len(dy_parts)
    tm = 512
    part_arrays, part_specs = _piece_specs(dy_parts, tm, 0)

    def body(*refs):
        part_refs = refs[:npart]
        w_ref, x_ref, dres_ref, nw_ref, sc_ref, dx_ref, st_ref = refs[npart:]

        @pl.when(pl.program_id(0) == 0)
        def _():
            st_ref[...] = jnp.zeros_like(st_ref)

        nwv, scale = nw_ref[...], 1.0 + sc_ref[...]
        sums = [jnp.zeros((1, D), F32)] * 3
        for half in range(SPLIT):
            rows = pl.ds(half * (tm // SPLIT), tm // SPLIT)
            dh = _nn(_columns(part_refs, rows), w_ref[...])
            xv = x_ref[rows, :]
            rstd = lax.rsqrt(jnp.mean(xv * xv, axis=-1, keepdims=True) + RMS_EPS)
            xh = xv * rstd
            dxh = dh * (nwv * scale)
            dx_ref[rows, :] = dres_ref[rows, :] + rstd * (dxh - xh * jnp.mean(dxh * xh, axis=-1, keepdims=True))
            dhx = dh * xh
            sums = [sums[0] + jnp.sum(dh, axis=0, keepdims=True), sums[1] + jnp.sum(dhx * nwv, axis=0, keepdims=True),
                    sums[2] + jnp.sum(dhx * scale, axis=0, keepdims=True)]
        for q in range(3):
            st_ref[pl.ds(q, 1), :] = st_ref[pl.ds(q, 1), :] + sums[q]

    row = lambda w_: pl.BlockSpec((tm, w_), lambda i: (i, 0))
    vec = _const_spec((1, D))
    return pl.pallas_call(
        body, name=name, grid=(S // tm,),
        out_shape=(jax.ShapeDtypeStruct((S, D), F32), jax.ShapeDtypeStruct((8, D), F32)),
        in_specs=part_specs + [_const_spec((N, D)), row(D), row(D), vec, vec],
        out_specs=(row(D), _const_spec((8, D))), compiler_params=_params(("arbitrary",), VMEM_LIMIT),
    )(*part_arrays, wt, x, dres, nw, sc)


def _ffn_bwd(dx, y, g, w_out, gu, wt_in, x, nw, sc, y1, g1, w_mix, *, mix_is_transposed, name):
    S = dx.shape[0]
    K = w_out.shape[0]
    Km = w_mix.shape[1] if mix_is_transposed else w_mix.shape[0]
    tm = 256

    def body(dx_ref, y_ref, g_ref, wo_ref, gu_ref, wi_ref, x_ref, nw_ref, sc_ref, y1_ref, g1_ref, wm_ref,
             dgu_ref, dyb_ref, dxo_ref, da_ref, dy1_ref, stg_ref, stf_ref, stm_ref):
        @pl.when(pl.program_id(0) == 0)
        def _():
            stg_ref[...] = jnp.zeros_like(stg_ref)
            stf_ref[...] = jnp.zeros_like(stf_ref)
            stm_ref[...] = jnp.zeros_like(stm_ref)

        dxv = dx_ref[...]
        stg_ref[pl.ds(0, 1), :] = stg_ref[pl.ds(0, 1), :] + jnp.sum(dxv * y_ref[...].astype(F32), axis=0, keepdims=True)
        dyb = (dxv * g_ref[...]).astype(BF16)
        dyb_ref[...] = dyb
        da = _nt(dyb, wo_ref[...])
        dgate = (da * gu_ref[:, pl.ds(0, K)].astype(F32)).astype(BF16)
        dup = (da * gu_ref[:, pl.ds(K, K)].astype(F32)).astype(BF16)
        dgu_ref[:, pl.ds(0, K)] = dgate
        dgu_ref[:, pl.ds(K, K)] = dup
        dh = _nn(dgate, wi_ref[pl.ds(0, K), :]) + _nn(dup, wi_ref[pl.ds(K, K), :])
        xv = x_ref[...]
        rstd = lax.rsqrt(jnp.mean(xv * xv, axis=-1, keepdims=True) + RMS_EPS)
        xh = xv * rstd
        nwv, scale = nw_ref[...], 1.0 + sc_ref[...]
        dxh = dh * (nwv * scale)
        dx1 = dxv + rstd * (dxh - xh * jnp.mean(dxh * xh, axis=-1, keepdims=True))
        dxo_ref[...] = dx1
        dhx = dh * xh
        stf_ref[pl.ds(0, 1), :] = stf_ref[pl.ds(0, 1), :] + jnp.sum(dh, axis=0, keepdims=True)
        stf_ref[pl.ds(1, 1), :] = stf_ref[pl.ds(1, 1), :] + jnp.sum(dhx * nwv, axis=0, keepdims=True)
        stf_ref[pl.ds(2, 1), :] = stf_ref[pl.ds(2, 1), :] + jnp.sum(dhx * scale, axis=0, keepdims=True)
        stm_ref[pl.ds(0, 1), :] = stm_ref[pl.ds(0, 1), :] + jnp.sum(dx1 * y1_ref[...].astype(F32), axis=0, keepdims=True)
        dy1 = (dx1 * g1_ref[...]).astype(BF16)
        dy1_ref[...] = dy1
        da_ref[...] = (_nn(dy1, wm_ref[...]) if mix_is_transposed else _nt(dy1, wm_ref[...])).astype(BF16)

    row = lambda w_: pl.BlockSpec((tm, w_), lambda i: (i, 0))
    vec = _const_spec((1, D))
    st = jax.ShapeDtypeStruct((8, D), F32)
    act = lambda w_: jax.ShapeDtypeStruct((S, w_), BF16)
    return pl.pallas_call(
        body, name=name, grid=(S // tm,),
        out_shape=(act(2 * K), act(D), jax.ShapeDtypeStruct((S, D), F32), act(Km), act(D), st, st, st),
        in_specs=[row(D), row(D), vec, _const_spec(w_out.shape), row(2 * K), _const_spec(wt_in.shape), row(D), vec, vec,
                  row(D), vec, _const_spec(w_mix.shape)],
        out_specs=(row(2 * K), row(D), row(D), row(Km), row(D), _const_spec((8, D)), _const_spec((8, D)), _const_spec((8, D))),
        compiler_params=_params(("arbitrary",), VMEM_LIMIT),
    )(dx, y, g, w_out, gu, wt_in, x, nw, sc, y1, g1, w_mix)


def _weight_grad(a, b_parts, *, transpose_out, name):
    S = a.shape[0]
    N = sum(p[0].shape[2] if isinstance(p, tuple) else p.shape[1] for p in b_parts)
    npart = len(b_parts)
    nb = N // 2 if N > 4096 else N
    assert nb == N or npart == 1
    tk = 512
    if npart == 1:
        part_arrays, part_specs = list(b_parts), [pl.BlockSpec((tk, nb), lambda n, k: (k, n))]
    else:
        part_arrays, part_specs = _piece_specs(b_parts, tk, 1)

    def body(*refs):
        a_ref, part_refs = refs[0], refs[1:1 + npart]
        out_ref, acc = refs[1 + npart:]
        k = pl.program_id(1)

        @pl.when(k == 0)
        def _():
            acc[...] = jnp.zeros_like(acc)

        acc[...] += _tn(a_ref[...], _columns(part_refs, slice(None)))

        @pl.when(k == pl.num_programs(1) - 1)
        def _():
            out_ref[...] = (acc[...].T if transpose_out else acc[...]).astype(BF16)

    out_block = pl.BlockSpec((nb, D), lambda n, k: (n, 0)) if transpose_out else pl.BlockSpec((D, nb), lambda n, k: (0, n))
    return pl.pallas_call(
        body, name=name, grid=(N // nb, S // tk),
        out_shape=jax.ShapeDtypeStruct((N, D) if transpose_out else (D, N), BF16),
        in_specs=[pl.BlockSpec((tk, D), lambda n, k: (k, 0))] + part_specs,
        out_specs=out_block, scratch_shapes=[pltpu.VMEM((D, nb), F32)],
        compiler_params=_params(("parallel", "arbitrary"), VMEM_LIMIT),
    )(a, *part_arrays)


def _adamw(w, g, m, v):
    m = ADAM_B1 * m + (1.0 - ADAM_B1) * g
    v = ADAM_B2 * v + (1.0 - ADAM_B2) * (g * g)
    m_hat = m / (1.0 - ADAM_B1 ** ADAM_STEP)
    v_hat = v / (1.0 - ADAM_B2 ** ADAM_STEP)
    delta = -ADAM_LR * (m_hat / (jnp.sqrt(v_hat) + ADAM_EPS) + ADAM_WD * w)
    return delta, m, v


def _adam_segment(parts, own, w, m, v, outs, layer, *, transposed, name):
    R, C = own.shape
    if transposed:
        tr, tc, steps = R, min(C, LANES), C // min(C, LANES)
        mine = lambda i: (0, i)
        theirs = lambda i: (0, 0, i)
        block = pl.BlockSpec((1, tc, R), lambda i: (layer, i, 0))
    else:
        tr, tc, steps = 32, C, R // 32
        mine = lambda i: (i, 0)
        theirs = lambda i: (0, i, 0)
        block = pl.BlockSpec((1, tr, C), lambda i: (layer, i, 0))

    def body(p_ref, o_ref, w_ref, m_ref, v_ref, *rest):
        g_out, d_out, m_out, v_out = rest[-4:]
        me = _my_index()
        g = jnp.zeros((tr, tc), F32)
        for j in range(N_DEV):
            g = g + jnp.where(me == j, o_ref[...], p_ref[j]).astype(F32)
        if transposed:
            g = g.T
        delta, mn, vn = _adamw(w_ref[0], g, m_ref[0], v_ref[0])
        g_out[0] = g
        d_out[0] = delta
        m_out[0] = mn
        v_out[0] = vn

    passed = [] if outs is None else [pl.BlockSpec(memory_space=pl.ANY)] * 4
    shp = jax.ShapeDtypeStruct(w.shape, F32)
    return pl.pallas_call(
        body, name=name, grid=(steps,), out_shape=(shp,) * 4,
        in_specs=[pl.BlockSpec((N_DEV, tr, tc), theirs), pl.BlockSpec((tr, tc), mine), block, block, block] + passed,
        out_specs=(block,) * 4, input_output_aliases={5 + q: q for q in range(len(passed))},
        compiler_params=_params(("parallel",), VMEM_LIMIT),
    )(parts, own, w, m, v, *(outs or ()))


def _adam_ada_w(cond_t, dmod, w, m, v):
    ncol = w.shape[-1]
    tr = 512

    def body(c_ref, d_ref, w_ref, m_ref, v_ref, g_out, d_out, m_out, v_out):
        g = _nn(c_ref[...], d_ref[0])
        delta, mn, vn = _adamw(w_ref[0], g, m_ref[0], v_ref[0])
        g_out[0] = g
        d_out[0] = delta
        m_out[0] = mn
        v_out[0] = vn

    blk = pl.BlockSpec((1, tr, ncol), lambda l, i: (l, i, 0))
    shp = jax.ShapeDtypeStruct(w.shape, F32)
    return pl.pallas_call(
        body, name="adam_ada_w", grid=(DEPTH, D // tr), out_shape=(shp,) * 4,
        in_specs=[pl.BlockSpec((tr, LANES), lambda l, i: (i, 0)), pl.BlockSpec((1, LANES, ncol), lambda l, i: (l, 0, 0)), blk, blk, blk],
        out_specs=(blk,) * 4, compiler_params=_params(("parallel", "parallel"), VMEM_LIMIT),
    )(cond_t, dmod, w, m, v)


TILE_ROWS = 168


def _stat_sources():
    pairs = []
    for i in range(DEPTH):
        b = 32 * i
        for q, src in enumerate((b, b + 1, b + 8, b + 16, b + 17, b + 24)):
            pairs.append((6 * i + q, src))
        pairs.append((24 + i, b + 2))
        pairs.append((32 + i, b + 18))
    pairs += [(40, 128), (41, 129)]
    return pairs


def _small_exchange(tiles, w, m, v):
    loss_row, sink_row, sink_src = 41, 48, 136

    def body(s_ref, w_ref, m_ref, v_ref, dmod_out, g_out, d_out, m_out, v_out, loss_out, all_ref, tot_ref, send_sems, recv_sems):
        me = _my_index()
        all_ref[me] = s_ref[...]
        copies = []
        for k in range(1, N_DEV):
            dev, _ = _peer(k)
            cp = pltpu.make_async_remote_copy(src_ref=s_ref, dst_ref=all_ref.at[me], send_sem=send_sems.at[k - 1],
                                              recv_sem=recv_sems.at[k - 1], device_id=dev, device_id_type=MESH)
            cp.start()
            copies.append(cp)
        for k in range(1, N_DEV):
            dev, pidx = _peer(k)
            pltpu.make_async_remote_copy(src_ref=s_ref, dst_ref=all_ref.at[pidx], send_sem=send_sems.at[k - 1],
                                         recv_sem=recv_sems.at[k - 1], device_id=dev, device_id_type=MESH).wait_recv()
        for cp in copies:
            cp.wait_send()
        tot = all_ref[0]
        for j in range(1, N_DEV):
            tot = tot + all_ref[j]
        tot_ref[...] = tot
        g_out[...] = jnp.zeros_like(g_out)
        for dst, src in _stat_sources():
            g_out[pl.ds(dst, 1), :] = tot_ref[pl.ds(src, 1), :]
            if dst < 24:
                for j in range(N_DEV):
                    dmod_out[j, pl.ds(dst, 1), :] = all_ref[j, pl.ds(src, 1), :]
        lane = lax.broadcasted_iota(jnp.int32, (1, D), 1)
        sink = jnp.zeros((1, D), F32)
        for h in range(32):
            sink = jnp.where(lane == h, tot_ref[pl.ds(sink_src + h, 1), :], sink)
        g_out[pl.ds(sink_row, 1), :] = sink
        g = g_out[...]
        delta, mn, vn = _adamw(w_ref[...], g, m_ref[...], v_ref[...])
        d_out[...] = delta
        m_out[...] = mn
        v_out[...] = vn
        loss = jnp.sum(g[loss_row:loss_row + 1, :], axis=-1, keepdims=True) * (0.5 / D)
        loss_out[...] = jnp.broadcast_to(loss, loss_out.shape)

    vm = pl.BlockSpec(memory_space=pltpu.VMEM)
    shp = jax.ShapeDtypeStruct((STAT_ROWS, D), F32)
    return pl.pallas_call(
        body, name="small_exchange",
        out_shape=(jax.ShapeDtypeStruct((N_DEV, 24, D), F32), shp, shp, shp, shp, jax.ShapeDtypeStruct((8, LANES), F32)),
        in_specs=[vm] * 4, out_specs=(vm,) * 6,
        scratch_shapes=[pltpu.VMEM((N_DEV, TILE_ROWS, D), F32), pltpu.VMEM((TILE_ROWS, D), F32),
                        pltpu.SemaphoreType.DMA((N_DEV - 1,)), pltpu.SemaphoreType.DMA((N_DEV - 1,))],
        compiler_params=_params(vmem=VMEM_LIMIT),
    )(tiles, w, m, v)


def _to_rows(name, a):
    if name in ("ffn_in", "a_in", "b_in"):
        return a.T
    if name == "b_out":
        return a.T.reshape(-1, D)
    return a


def _rows8(a):
    return jnp.pad(a, ((0, 8 - a.shape[0]), (0, 0)))


def _pack_small(ada_b, norm_mix, norm_ffn, final_norm, sink):
    sink_row = jnp.pad(sink.reshape(1, -1), ((0, 0), (0, D - sink.size)))
    return jnp.concatenate([ada_b.reshape(24, D), _rows8(norm_mix), _rows8(norm_ffn), _rows8(final_norm.reshape(1, D)),
                            _rows8(sink_row)], axis=0)


def _unpack_small(a):
    return a[0:24].reshape(4, 6 * D), a[24:28], a[32:36], a[40], a[48, :32].reshape(2, 16)


def kernel(x, c, ada_w, ada_b, norm_mix, norm_ffn, ffn_w_in, ffn_w_out, a_w_in, a_w_out, a_sink, b_w_in, b_w_out, final_norm, loss_target, m_ada_w, m_ada_b, m_norm_mix, m_norm_ffn, m_ffn_w_in, m_ffn_w_out, m_a_w_in, m_a_w_out, m_a_sink, m_b_w_in, m_b_w_out, m_final_norm, v_ada_w, v_ada_b, v_norm_mix, v_norm_ffn, v_ffn_w_in, v_ffn_w_out, v_a_w_in, v_a_w_out, v_a_sink, v_b_w_in, v_b_w_out, v_final_norm):
    S = x.shape[1]
    x0 = x.reshape(S, D)
    target = loss_target.reshape(S, D)
    me = _my_index()
    ncol = ada_w.shape[-1]

    ada_b_mine = lax.dynamic_slice_in_dim(ada_b, me * ncol, ncol, axis=1)
    cond_all, parts = _cond_exchange(jnp.broadcast_to(c.reshape(1, D), (8, D)), ada_w, ada_b_mine)
    mod = lax.dynamic_index_in_dim(parts, me, axis=2, keepdims=False)
    mod = jnp.transpose(mod, (1, 0, 2)).reshape(DEPTH, 6, 1, D)

    weights = {"ffn_in": ffn_w_in, "ffn_out": ffn_w_out, "a_in": a_w_in, "a_out": a_w_out, "b_in": b_w_in, "b_out": b_w_out}
    shard = {(n, l): _to_rows(n, weights[n][l]).astype(BF16) for n, l, _ in SEGMENTS}
    first = [sg for sg in _layer_segments(0) if not sg[0].startswith("ffn")]
    gathered0 = _all_gather_weights([shard[(n, l)] for n, l, _ in first])
    W = {(n, l): g for (n, l, _), g in zip(first, gathered0)}
    groups = [[sg for sg in _layer_segments(0) if sg[0].startswith("ffn")]] + [_layer_segments(i) for i in range(1, DEPTH)]
    gathers, order = [], gathered0[0]
    for q, segs in enumerate(groups):
        zones = [(N_DEV, rows, D) for _, _, rows in segs]
        gathers.append(_exchange_start([shard[(n, l)] for n, l, _ in segs], zones, [(s, 0) for s in range(len(segs))],
                                       [sg[2] for sg in segs], False, order, "weight_gather_start_%d" % q))
        order = gathers[-1][-1]
    gather_token = order[0:1, 0:1]

    def finish_gather(q, after):
        segs = groups[q]
        zones = _exchange_wait(gathers[q], len(segs), [(s, 0) for s in range(len(segs))], [sg[2] for sg in segs], True, after,
                               "weight_gather_wait_%d" % q)
        for (n, l, rows), zone in zip(segs, zones):
            W[(n, l)] = zone.reshape(D, 512) if n == "b_out" else zone.reshape(N_DEV * rows, D)

    a_slopes, b_slopes = _slopes(16), _slopes(24)
    bias_a = _alibi_bias(a_slopes, A_HALF, 1)
    bias_b = [_alibi_bias(b_slopes[8 * g:8 * g + 8], B_HALF, dil) for g, dil in enumerate(B_DILS)]
    bias_b_fwd = [_alibi_bias(b_slopes[8 * g:8 * g + 8], B_HALF, dil, max(CHUNK, TQ * dil), both=True) for g, dil in enumerate(B_DILS)]
    a_geom = dict(C=A_QKV, r=1, half=A_HALF, qoff=0, koff=1024, voff=1280, n_units=2)
    b_geom = [dict(C=B_QKV, r=dil, half=B_HALF, qoff=512 * g, koff=1536 + 128 * g, voff=1920 + 128 * g, n_units=1)
              for g, dil in enumerate(B_DILS)]

    saved = []
    xcur = x0
    for i in range(DEPTH):
        j = i // 2
        sh1, sc1, g1, sh2, sc2, g2 = [mod[i, q] for q in range(6)]
        nm, nf = norm_mix[i].reshape(1, D), norm_ffn[i].reshape(1, D)
        if i == 0:
            nm = nm + gather_token
        if i > 0:
            finish_gather(i, xcur)
        if i % 2 == 0:
            sink_rep = jnp.repeat(jnp.repeat(a_sink[j], TQ).reshape(2, 1, 8 * TQ), 8, axis=1).reshape(16, 8 * TQ)
            h1, qkv = _proj(xcur, nm, sc1, sh1, W[("a_in", j)], name="proj_a")
            o, lse = _attn_fwd(qkv, bias_a, sink_rep, out_dtype=BF16, name="attn_a_fwd", **a_geom)
            if i == 0:
                finish_gather(0, o)
            x1, y1, h2, gu, act = _out_ffn_in(o, W[("a_out", j)], xcur, g1, nf, sc2, sh2, W[("ffn_in", i)],
                                              w_is_transposed=False, name="out_a_ffn_in")
        else:
            sink_rep = None
            h1, qkv = _proj(xcur, nm, sc1, sh1, W[("b_in", j)], name="proj_b")
            outs = [_attn_fwd(qkv, bias_b_fwd[g], None, out_dtype=F32, name="attn_b%d_fwd" % g, **b_geom[g]) for g in range(3)]
            o, lse = _merge_groups([t[0] for t in outs], [t[1] for t in outs])
            x1, y1, h2, gu, act = _out_ffn_in(o, W[("b_out", j)], xcur, g1, nf, sc2, sh2, W[("ffn_in", i)],
                                              w_is_transposed=True, name="out_b_ffn_in")
        if i < DEPTH - 1:
            x2, y2 = _ffn_out(act, W[("ffn_out", i)], x1, g2, None, None, name="ffn_out")
        else:
            x2, y2, head_stats = _ffn_out(act, W[("ffn_out", i)], x1, g2, target, final_norm.reshape(1, D), name="ffn_out_loss")
        saved.append(dict(x0=xcur, h1=h1, qkv=qkv, o=o, lse=lse, y1=y1, x1=x1, h2=h2, gu=gu, act=act, y2=y2, sink=sink_rep))
        xcur = x2

    dx = xcur

    dW = {}
    stat_tiles, dsink = [None] * DEPTH, [None] * 2
    exchanges = []
    start_token = None

    def start_exchange(segs):
        own = [lax.dynamic_slice_in_dim(dW[(n, l)], me * rows, rows, axis=0) for n, l, rows in segs]
        zones = [(N_DEV, rows, D) for _, _, rows in segs]
        started = _exchange_start([dW[(n, l)] for n, l, _ in segs], zones, [(s, 0) for s in range(len(segs))],
                                  [sg[2] for sg in segs], True, own[0], "grad_exchange_start_%d" % len(exchanges))
        exchanges.append((segs, started, own))
        return started[-1][0:1, 0:1]

    for i in reversed(range(DEPTH)):
        j = i // 2
        sv = saved[i]
        sh1, sc1, g1, sh2, sc2, g2 = [mod[i, q] for q in range(6)]
        if start_token is not None:
            g2 = g2 + start_token
            start_token = None
        nm, nf = norm_mix[i].reshape(1, D), norm_ffn[i].reshape(1, D)
        mix = "a_out" if i % 2 == 0 else "b_out"
        dgu, dy2, dx1, do, dy1, st_g2, st_f, st_g1 = _ffn_bwd(
            dx, sv["y2"], g2, W[("ffn_out", i)], sv["gu"], W[("ffn_in", i)], sv["x1"], nf, sc2, sv["y1"], g1, W[(mix, j)],
            mix_is_transposed=(i % 2 == 1), name="ffn_bwd_" + mix)
        dW[("ffn_out", i)] = _weight_grad(dy2, [sv["act"]], transpose_out=True, name="dw_ffn_out")
        dW[("ffn_in", i)] = _weight_grad(sv["h2"], [dgu], transpose_out=True, name="dw_ffn_in")
        sink_bwd = sv["sink"]
        if i == 0:
            sink_bwd = sink_bwd + start_exchange([sg for sg in _layer_segments(0) if sg[0].startswith("ffn")])
        if i % 2 == 0:
            dW[("a_out", j)] = _weight_grad(dy1, [sv["o"]], transpose_out=True, name="dw_a_out")
            dq, dk, dv, ds = _attn_bwd(sv["qkv"], bias_a, sink_bwd, sv["o"], do, sv["lse"], name="attn_a_bwd", **a_geom)
            dsink[j] = ds
            dqkv = [dq, (dk, 0), (dk, 1), (dv, 0), (dv, 1)]
            dW[("a_in", j)] = _weight_grad(sv["h1"], dqkv, transpose_out=True, name="dw_a_in")
            dx0, st_m = _norm_bwd(dqkv, W[("a_in", j)], sv["x0"], dx1, nm, sc1, name="proj_a_bwd")
        else:
            dW[("b_out", j)] = _weight_grad(dy1, [sv["o"]], transpose_out=False, name="dw_b_out").reshape(N_DEV * 64, D)
            gr = [_attn_bwd(sv["qkv"], bias_b[g], None, sv["o"], do, sv["lse"], name="attn_b%d_bwd" % g, **b_geom[g]) for g in range(3)]
            dqkv = [t[0] for t in gr] + [(t[1], 0) for t in gr] + [(t[2], 0) for t in gr]
            dW[("b_in", j)] = _weight_grad(sv["h1"], dqkv, transpose_out=True, name="dw_b_in")
            dx0, st_m = _norm_bwd(dqkv, W[("b_in", j)], sv["x0"], dx1, nm, sc1, name="proj_b_bwd")
        stat_tiles[i] = [st_m, st_g1, st_f, st_g2]
        if i > 0:
            start_token = start_exchange(_layer_segments(i))
        else:
            start_exchange([sg for sg in _layer_segments(0) if not sg[0].startswith("ffn")])
        dx = dx0
    grad_x = dx.reshape(1, S, D)

    masters = {"ffn_in": (ffn_w_in, m_ffn_w_in, v_ffn_w_in), "ffn_out": (ffn_w_out, m_ffn_w_out, v_ffn_w_out),
               "a_in": (a_w_in, m_a_w_in, v_a_w_in), "a_out": (a_w_out, m_a_w_out, v_a_w_out),
               "b_in": (b_w_in, m_b_w_in, v_b_w_in), "b_out": (b_w_out, m_b_w_out, v_b_w_out)}
    results = {n: None for n in masters}
    after = dx
    for e, (segs, started, own) in enumerate(exchanges):
        zones = _exchange_wait(started, len(segs), [(s, 0) for s in range(len(segs))], [sg[2] for sg in segs], False, after,
                               "grad_exchange_wait_%d" % e)
        for (n, l, rows), zone, mine in zip(segs, zones, own):
            if n == "b_out":
                zone, mine = zone.reshape(N_DEV, LANES, 512), mine.reshape(LANES, 512)
            results[n] = _adam_segment(zone, mine, *masters[n], results[n], l, transposed=n in ("ffn_in", "a_in", "b_in", "b_out"),
                                       name="adam_" + n)
            after = results[n][0]
    big = {(kind, n): results[n][q] for q, kind in enumerate(("grad", "delta", "m", "v")) for n in masters}

    tiles = jnp.concatenate([t for i in range(DEPTH) for t in stat_tiles[i]] + [head_stats]
                            + [jnp.pad(ds, ((0, 0), (0, D - LANES))) for ds in dsink], axis=0)
    small = [_pack_small(*t) for t in ((ada_b, norm_mix, norm_ffn, final_norm, a_sink),
                                       (m_ada_b, m_norm_mix, m_norm_ffn, m_final_norm, m_a_sink),
                                       (v_ada_b, v_norm_mix, v_norm_ffn, v_final_norm, v_a_sink))]
    dmod_all, sg, sd, sm, sv_, loss_tile = _small_exchange(tiles, *small)
    loss = loss_tile[0, 0]
    dmod_all = dmod_all.reshape(N_DEV, DEPTH, 6 * D)
    dmod_mine = lax.dynamic_slice_in_dim(dmod_all, me * ncol, ncol, axis=2)
    dmod_pad = jnp.pad(jnp.transpose(dmod_mine, (1, 0, 2)), ((0, 0), (0, LANES - N_DEV), (0, 0))).astype(BF16)
    cond_t = jnp.pad(cond_all.T, ((0, 0), (0, LANES - N_DEV))).astype(BF16)
    ada = _adam_ada_w(cond_t, dmod_pad, ada_w, m_ada_w, v_ada_w)

    outs = [loss, grad_x]
    small_res = [_unpack_small(t) for t in (sg, sd, sm, sv_)]
    for q, kind in enumerate(("grad", "delta", "m", "v")):
        ab, nm_, nf_, fn, sk = small_res[q]
        outs += [ada[q], ab, nm_, nf_, big[(kind, "ffn_in")], big[(kind, "ffn_out")], big[(kind, "a_in")], big[(kind, "a_out")],
                 sk, big[(kind, "b_in")], big[(kind, "b_out")], fn]
    return tuple(outs)
```

```python
import functools
import math

import numpy as np
import jax
import jax.numpy as jnp
from jax import lax
from jax.experimental import pallas as pl
from jax.experimental.pallas import tpu as pltpu

D = 1024
HEAD_DIM = 64
D_FF = 2816
DEPTH = 4
N_DEV = 8
A_QKV = 1536
B_QKV = 2304
A_HALF = 128
B_HALF = 64
B_DILS = (1, 4, 16)
RMS_EPS = 1e-6
NEG = -1e30
ADAM_LR = 0.001
ADAM_B1 = 0.9
ADAM_B2 = 0.999
ADAM_EPS = 1e-08
ADAM_WD = 0.01
ADAM_STEP = 10

LANES = 128
SPLIT = 2
TQ = 128
VMEM_LIMIT = 56 * 1024 * 1024
MESH = pl.DeviceIdType.MESH
F32 = jnp.float32
BF16 = jnp.bfloat16

SEGMENTS = ([("ffn_in", l, 704) for l in range(4)] + [("ffn_out", l, 352) for l in range(4)]
            + [("a_in", j, 192) for j in range(2)] + [("a_out", j, 128) for j in range(2)]
            + [("b_in", j, 288) for j in range(2)] + [("b_out", j, 64) for j in range(2)])
def _layer_segments(i):
    mixer = "a" if i % 2 == 0 else "b"
    return [s for s in SEGMENTS if (s[0].startswith("ffn") and s[1] == i) or (s[0].startswith(mixer + "_") and s[1] == i // 2)]


def _offsets(segs):
    rows = [s[2] for s in segs]
    return [sum(rows[:k]) for k in range(len(rows))], sum(rows)
STAT_ROWS = 56


def _nn(a, b):
    return jnp.dot(a, b, preferred_element_type=F32)


def _nt(a, b):
    return lax.dot_general(a, b, (((1,), (1,)), ((), ())), preferred_element_type=F32)


def _tn(a, b):
    return lax.dot_general(a, b, (((0,), (0,)), ((), ())), preferred_element_type=F32)


def _params(dims=None, vmem=None):
    kw = {}
    if dims is not None:
        kw["dimension_semantics"] = dims
    if vmem is not None:
        kw["vmem_limit_bytes"] = vmem
    return pltpu.CompilerParams(**kw)


def _my_index():
    return 4 * lax.axis_index("x") + 2 * lax.axis_index("y") + lax.axis_index("c")


def _peer(k):
    x, y, c = lax.axis_index("x"), lax.axis_index("y"), lax.axis_index("c")
    px, py, pc = x ^ ((k >> 2) & 1), y ^ ((k >> 1) & 1), c ^ (k & 1)
    return (px, py, pc), 4 * px + 2 * py + pc


def _const_spec(shape):
    nd = len(shape)
    return pl.BlockSpec(shape, lambda *_: (0,) * nd)


def _cond_exchange(c_tile, ada_w, ada_b_mine):
    ncol = ada_w.shape[-1]

    def body(c_ref, w_ref, b_ref, cond_ref, parts_ref, call_ref, mine_ref, send_sems, recv_sems):
        me = _my_index()
        call_ref[me] = c_ref[...]
        copies = []
        for k in range(1, N_DEV):
            dev, _ = _peer(k)
            cp = pltpu.make_async_remote_copy(src_ref=c_ref, dst_ref=call_ref.at[me], send_sem=send_sems.at[0, k - 1],
                                              recv_sem=recv_sems.at[0, k - 1], device_id=dev, device_id_type=MESH)
            cp.start()
            copies.append(cp)
        for k in range(1, N_DEV):
            _, pidx = _peer(k)
            pltpu.make_async_remote_copy(src_ref=c_ref, dst_ref=call_ref.at[pidx], send_sem=send_sems.at[0, k - 1],
                                         recv_sem=recv_sems.at[0, k - 1], device_id=_peer(k)[0], device_id_type=MESH).wait_recv()
        for cp in copies:
            cp.wait_send()
        row = lax.broadcasted_iota(jnp.int32, (N_DEV, D), 0)
        cmat = jnp.zeros((N_DEV, D), F32)
        for j in range(N_DEV):
            cmat = jnp.where(row == j, call_ref[j], cmat)
        cond = cmat * jax.nn.sigmoid(cmat)
        cond_ref[...] = cond
        cb = cond.astype(BF16)
        for l in range(DEPTH):
            mine_ref[l] = _nn(cb, w_ref[l].astype(BF16)) + b_ref[pl.ds(l, 1), :]
        parts_ref[me] = mine_ref[...]
        copies = []
        for k in range(1, N_DEV):
            dev, _ = _peer(k)
            cp = pltpu.make_async_remote_copy(src_ref=mine_ref, dst_ref=parts_ref.at[me], send_sem=send_sems.at[1, k - 1],
                                              recv_sem=recv_sems.at[1, k - 1], device_id=dev, device_id_type=MESH)
            cp.start()
            copies.append(cp)
        for k in range(1, N_DEV):
            dev, pidx = _peer(k)
            pltpu.make_async_remote_copy(src_ref=mine_ref, dst_ref=parts_ref.at[pidx], send_sem=send_sems.at[1, k - 1],
                                         recv_sem=recv_sems.at[1, k - 1], device_id=dev, device_id_type=MESH).wait_recv()
        for cp in copies:
            cp.wait_send()

    vm = pl.BlockSpec(memory_space=pltpu.VMEM)
    return pl.pallas_call(
        body, name="cond_exchange",
        out_shape=(jax.ShapeDtypeStruct((N_DEV, D), F32), jax.ShapeDtypeStruct((N_DEV, DEPTH, N_DEV, ncol), F32)),
        in_specs=[vm, vm, vm], out_specs=(vm, vm),
        scratch_shapes=[pltpu.VMEM((N_DEV, N_DEV, D), F32), pltpu.VMEM((DEPTH, N_DEV, ncol), F32),
                        pltpu.SemaphoreType.DMA((2, N_DEV - 1)), pltpu.SemaphoreType.DMA((2, N_DEV - 1))],
        compiler_params=_params(vmem=VMEM_LIMIT),
    )(c_tile, ada_w, ada_b_mine)[:2]


def _all_gather_weights(shards):
    n = len(shards)
    big = max(range(n), key=lambda s: shards[s].shape[0])
    total = sum(sh.shape[0] for sh in shards)
    assert N_DEV * shards[big].shape[0] >= total

    def body(*refs):
        ins, outs = refs[:n], refs[n:2 * n]
        local_sems, send_sems, recv_sems = refs[2 * n:]
        me = _my_index()
        local = []
        for s in range(n):
            rows = ins[s].shape[0]
            cp = pltpu.make_async_copy(ins[s], outs[s].at[pl.ds(me * rows, rows)], local_sems.at[s])
            cp.start()
            local.append(cp)
        for k in range(1, N_DEV):
            dev, _ = _peer(k)
            for s in range(n):
                rows = ins[s].shape[0]
                pltpu.make_async_remote_copy(src_ref=ins[s], dst_ref=outs[s].at[pl.ds(me * rows, rows)],
                                             send_sem=send_sems.at[k - 1], recv_sem=recv_sems.at[k - 1],
                                             device_id=dev, device_id_type=MESH).start()
        whole = outs[big].at[pl.ds(0, total)]
        for k in range(1, N_DEV):
            dev, _ = _peer(k)
            w = pltpu.make_async_remote_copy(src_ref=whole, dst_ref=whole, send_sem=send_sems.at[k - 1],
                                             recv_sem=recv_sems.at[k - 1], device_id=dev, device_id_type=MESH)
            w.wait_send()
            w.wait_recv()
        for cp in local:
            cp.wait()

    hbm = pl.BlockSpec(memory_space=pl.ANY)
    return pl.pallas_call(
        body, name="weight_all_gather",
        out_shape=tuple(jax.ShapeDtypeStruct((N_DEV * s.shape[0], D), s.dtype) for s in shards),
        in_specs=[hbm] * n, out_specs=tuple([hbm] * n),
        scratch_shapes=[pltpu.SemaphoreType.DMA((n,)), pltpu.SemaphoreType.DMA((N_DEV - 1,)),
                        pltpu.SemaphoreType.DMA((N_DEV - 1,))],
    )(*shards)


HBM = pl.BlockSpec(memory_space=pltpu.HBM)
SEM = pl.BlockSpec(memory_space=pltpu.SEMAPHORE)
EFFECT = pltpu.SideEffectType.DATAFLOW_SIDE_EFFECTING


def _exchange_start(srcs, landings, dst, rows, to_peer_rows, after, name):
    n, nl = len(srcs), len(landings)

    def body(*refs):
        src_refs = refs[:n]
        send_sems, recv_sems = refs[n + 1], refs[n + 2]
        land_refs = refs[2 * n + 3:2 * n + 3 + nl]
        token = refs[-1]
        me = _my_index()
        for k in range(1, N_DEV):
            dev, pidx = _peer(k)
            for q in range(n):
                src = src_refs[q].at[pl.ds(pidx * rows[q], rows[q])] if to_peer_rows else src_refs[q]
                pltpu.make_async_remote_copy(src_ref=src, dst_ref=land_refs[dst[q][0]].at[me, pl.ds(dst[q][1], rows[q])],
                                             send_sem=send_sems.at[k * n + q], recv_sem=recv_sems.at[k * n + q],
                                             device_id=dev, device_id_type=MESH).start()
        if not to_peer_rows:
            for q in range(n):
                pltpu.make_async_copy(src_refs[q], land_refs[dst[q][0]].at[me, pl.ds(dst[q][1], rows[q])], send_sems.at[q]).start()
        token[...] = jnp.zeros_like(token)

    sems = pltpu.SemaphoreType.DMA((N_DEV * n,))
    return pl.pallas_call(
        body, name=name,
        out_shape=(sems, sems, *[pltpu.HBM(a.shape, a.dtype) for a in srcs], *[pltpu.HBM(shape, BF16) for shape in landings],
                   jax.ShapeDtypeStruct((8, LANES), F32)),
        in_specs=[HBM] * n + [pl.BlockSpec(memory_space=pl.ANY)],
        out_specs=(SEM, SEM, *[HBM] * (n + nl), pl.BlockSpec(memory_space=pltpu.VMEM)),
        input_output_aliases={q: 2 + q for q in range(n)},
        compiler_params=pltpu.CompilerParams(has_side_effects=EFFECT),
    )(*[pltpu.with_memory_space_constraint(a, pltpu.HBM) for a in srcs], after)


def _exchange_wait(started, n, dst, rows, own_slot, after, name):
    send_sems, recv_sems = started[0], started[1]
    arrays = list(started[2:-1])
    n1 = len(arrays)

    def body(*refs):
        land_refs = refs[n:n1]
        sends, recvs = refs[n1], refs[n1 + 1]
        for k in range(1, N_DEV):
            dev, _ = _peer(k)
            for q in range(n):
                slot = land_refs[dst[q][0]].at[0, pl.ds(dst[q][1], rows[q])]
                w = pltpu.make_async_remote_copy(src_ref=slot, dst_ref=slot, send_sem=sends.at[k * n + q],
                                                 recv_sem=recvs.at[k * n + q], device_id=dev, device_id_type=MESH)
                w.wait_send()
                w.wait_recv()
        if own_slot:
            for q in range(n):
                slot = land_refs[dst[q][0]].at[0, pl.ds(dst[q][1], rows[q])]
                pltpu.make_async_copy(slot, slot, sends.at[q]).wait()

    return pl.pallas_call(
        body, name=name, out_shape=tuple(pltpu.HBM(a.shape, a.dtype) for a in arrays),
        in_specs=[HBM] * n1 + [SEM, SEM, pl.BlockSpec(memory_space=pl.ANY)], out_specs=tuple([HBM] * n1),
        input_output_aliases={q: q for q in range(n1)},
        compiler_params=pltpu.CompilerParams(has_side_effects=EFFECT),
    )(*arrays, send_sems, recv_sems, after)[n:]


def _norm_mod(x, nw, sc, sh):
    ms = jnp.mean(x * x, axis=-1, keepdims=True)
    xh = x * lax.rsqrt(ms + RMS_EPS)
    return xh, (xh * nw) * (1.0 + sc) + sh


def _proj(x, nw, sc, sh, wt, *, name):
    S, N = x.shape[0], wt.shape[0]
    tm = 512

    def body(x_ref, nw_ref, sc_ref, sh_ref, w_ref, h_ref, out_ref):
        for half in range(SPLIT):
            rows = pl.ds(half * (tm // SPLIT), tm // SPLIT)
            _, h = _norm_mod(x_ref[rows, :], nw_ref[...], sc_ref[...], sh_ref[...])
            hb = h.astype(BF16)
            h_ref[rows, :] = hb
            out_ref[rows, :] = _nt(hb, w_ref[...]).astype(BF16)

    row = lambda w: pl.BlockSpec((tm, w), lambda i: (i, 0))
    vec = _const_spec((1, D))
    return pl.pallas_call(
        body, name=name, grid=(S // tm,), out_shape=(jax.ShapeDtypeStruct((S, D), BF16), jax.ShapeDtypeStruct((S, N), BF16)),
        in_specs=[row(D), vec, vec, vec, _const_spec((N, D))], out_specs=(row(D), row(N)),
        compiler_params=_params(("parallel",), VMEM_LIMIT),
    )(x, nw, sc, sh, wt)


def _ffn_out(a, w, x, g, target, fnw, *, name):
    S, K = a.shape
    tm = 512
    last = target is not None

    def body(a_ref, w_ref, x_ref, g_ref, *rest):
        y = _nn(a_ref[...], w_ref[...])
        xv = x_ref[...] + g_ref[...] * y
        if not last:
            xo_ref, y_ref = rest
            y_ref[...] = y.astype(BF16)
            xo_ref[...] = xv
            return
        t_ref, fw_ref, dx_ref, y_ref, st_ref = rest
        y_ref[...] = y.astype(BF16)

        @pl.when(pl.program_id(0) == 0)
        def _():
            st_ref[...] = jnp.zeros_like(st_ref)

        rstd = lax.rsqrt(jnp.mean(xv * xv, axis=-1, keepdims=True) + RMS_EPS)
        xh = xv * rstd
        err = xh * fw_ref[...] - t_ref[...]
        dy = err * (1.0 / D)
        dxh = dy * fw_ref[...]
        dx_ref[...] = rstd * (dxh - xh * jnp.mean(dxh * xh, axis=-1, keepdims=True))
        st_ref[pl.ds(0, 1), :] = st_ref[pl.ds(0, 1), :] + jnp.sum(dy * xh, axis=0, keepdims=True)
        st_ref[pl.ds(1, 1), :] = st_ref[pl.ds(1, 1), :] + jnp.sum(err * err, axis=0, keepdims=True)

    row = lambda w_: pl.BlockSpec((tm, w_), lambda i: (i, 0))
    in_specs = [row(K), _const_spec(w.shape), row(D), _const_spec((1, D))]
    args = [a, w, x, g]
    out_shape = [jax.ShapeDtypeStruct((S, D), F32), jax.ShapeDtypeStruct((S, D), BF16)]
    out_specs = [row(D), row(D)]
    if last:
        in_specs += [row(D), _const_spec((1, D))]
        args += [target, fnw]
        out_shape.append(jax.ShapeDtypeStruct((8, D), F32))
        out_specs.append(_const_spec((8, D)))
    return pl.pallas_call(
        body, name=name, grid=(S // tm,), out_shape=tuple(out_shape), in_specs=in_specs, out_specs=tuple(out_specs),
        compiler_params=_params(("arbitrary",) if last else ("parallel",), VMEM_LIMIT),
    )(*args)


CHUNK = 1024


def _tile_rows(r, chunk=CHUNK):
    return min(TQ, chunk // r)


def _out_ffn_in(a, w_mix, x, g, nw, sc, sh, wt, *, w_is_transposed, name):
    S, K = a.shape
    tm = 256

    def body(a_ref, wm_ref, x_ref, g_ref, nw_ref, sc_ref, sh_ref, w_ref, x1_ref, y_ref, h_ref, gu_ref, act_ref):
        y = _nt(a_ref[...], wm_ref[...]) if w_is_transposed else _nn(a_ref[...], wm_ref[...])
        y_ref[...] = y.astype(BF16)
        x1 = x_ref[...] + g_ref[...] * y
        x1_ref[...] = x1
        _, h = _norm_mod(x1, nw_ref[...], sc_ref[...], sh_ref[...])
        hb = h.astype(BF16)
        h_ref[...] = hb
        gate = _nt(hb, w_ref[pl.ds(0, D_FF), :])
        up = _nt(hb, w_ref[pl.ds(D_FF, D_FF), :])
        sig = jax.nn.sigmoid(gate)
        silu = gate * sig
        gu_ref[:, pl.ds(0, D_FF)] = (up * (sig * (1.0 + gate * (1.0 - sig)))).astype(BF16)
        gu_ref[:, pl.ds(D_FF, D_FF)] = silu.astype(BF16)
        act_ref[...] = (silu * up).astype(BF16)

    row = lambda w_: pl.BlockSpec((tm, w_), lambda i: (i, 0))
    vec = _const_spec((1, D))
    return pl.pallas_call(
        body, name=name, grid=(S // tm,),
        out_shape=(jax.ShapeDtypeStruct((S, D), F32), jax.ShapeDtypeStruct((S, D), BF16), jax.ShapeDtypeStruct((S, D), BF16),
                   jax.ShapeDtypeStruct((S, 2 * D_FF), BF16), jax.ShapeDtypeStruct((S, D_FF), BF16)),
        in_specs=[row(K), _const_spec(w_mix.shape), row(D), vec, vec, vec, vec, _const_spec(wt.shape)],
        out_specs=(row(D), row(D), row(D), row(2 * D_FF), row(D_FF)),
        compiler_params=_params(("parallel",), VMEM_LIMIT),
    )(a, w_mix, x, g, nw, sc, sh, wt)


def _alibi_bias(slopes, half, dil, chunk=CHUNK, both=False):
    tq = _tile_rows(dil, chunk)
    tk = tq + 2 * half
    rel = np.arange(tk)[:, None] - half - np.arange(tq)[None, :]
    band = np.abs(rel) <= half
    dist = (dil * np.abs(rel)).astype(np.float32)
    tabs = [np.where(band, -np.float32(s) * dist, np.float32(NEG)).astype(np.float32) for s in slopes]
    out = []
    for u in range(0, len(tabs), 8):
        tab = np.concatenate(tabs[u:u + 8], axis=1)
        first, last = tab.copy(), tab.copy()
        first[:half] = NEG
        last[tk - half:] = NEG
        out += [tab, first, last]
        if both:
            last = last.copy()
            last[:half] = NEG
            out.append(last)
    return jnp.asarray(np.concatenate(out, axis=0))


def _slopes(n):
    return (2.0 ** (-8.0 * np.arange(1, n + 1) / n)).astype(np.float32)


def _head_masks(tq):
    lane = lax.broadcasted_iota(jnp.int32, (tq, LANES), 1)
    lo = lane < HEAD_DIM
    return lo, jnp.logical_not(lo)


def _stack_heads(tiles, lo, hi, scale):
    blocks = []
    for t in range(4):
        xf = tiles[t] if scale == 1.0 else tiles[t] * scale
        for a in range(2):
            xm = jnp.where(lo if a == 0 else hi, xf, 0.0)
            if a != t // 2:
                xm = pltpu.roll(xm, HEAD_DIM, 1)
            blocks.append(xm.astype(BF16))
    return jnp.concatenate(blocks, axis=0)


def _tile_from_columns(x8t, t, tq):
    r0 = HEAD_DIM * (t // 2)
    top = x8t[r0:r0 + HEAD_DIM, 2 * t * tq:(2 * t + 1) * tq]
    bot = x8t[r0:r0 + HEAD_DIM, (2 * t + 1) * tq:(2 * t + 2) * tq]
    return jnp.concatenate([top, bot], axis=0).T


def _attn_layout(S, C, r, half, qoff, koff, voff, chunk):
    hb = half * r
    per = chunk // hb
    nhb = S // hb
    main = lambda off: pl.BlockSpec((chunk, LANES), lambda u, i: (i, off // LANES + u))
    prev = lambda off: pl.BlockSpec((hb, LANES), lambda u, i: (jnp.maximum(i * per - 1, 0), off // LANES + u))
    nxt = lambda off: pl.BlockSpec((hb, LANES), lambda u, i: (jnp.minimum((i + 1) * per, nhb - 1), off // LANES + u))
    specs = [pl.BlockSpec((chunk, 4 * LANES), lambda u, i: (i, qoff // (4 * LANES) + u))]
    specs += [prev(koff), main(koff), nxt(koff), prev(voff), main(voff), nxt(voff)]
    return specs, hb


def _stage(dst, srcs):
    row = 0
    for src in srcs:
        n = src.shape[0]
        dst[pl.ds(row, n), :] = src[...].astype(F32)
        row += n


def _rows(start, n, r):
    return pl.ds(start, n, stride=r) if r > 1 else pl.ds(start, n)


def _attn_fwd(qkv, bias, sink, *, C, r, half, qoff, koff, voff, n_units, out_dtype, name):
    S = qkv.shape[0]
    chunk = max(CHUNK, TQ * r)
    tq = _tile_rows(r, chunk)
    tk = tq + 2 * half
    tiles = chunk // (r * tq)
    nsteps = S // chunk
    specs, hb = _attn_layout(S, C, r, half, qoff, koff, voff, chunk)
    use_sink = sink is not None

    def body(*refs):
        q_ref, kp, km, kn, vp, vm, vn, bias_ref = refs[:8]
        rest = list(refs[8:])
        sink_ref = rest.pop(0) if use_sink else None
        o_ref, lse_ref, qs, ks, vs, os_, ls = rest
        i = pl.program_id(1)
        if r > 1:
            for t in range(4):
                qs[t] = q_ref[:, pl.ds(t * LANES, LANES)].astype(F32)
        _stage(ks, [kp, km, kn])
        _stage(vs, [vp, vm, vn])
        lo, hi = _head_masks(tq)

        def tile_in(staged, ref, t, start):
            if r > 1:
                return staged[t, _rows(start, tq, r), :]
            return ref[pl.ds(start, tq), pl.ds(t * LANES, LANES)].astype(F32)

        ones = jnp.ones((16, tk), BF16)
        if use_sink:
            sk = sink_ref[pl.ds(0, 1), :]

        def chain(n, carry):
            rho, c = n // tiles, n % tiles
            start = c * (tq * r) + rho
            if r == 1:
                start = pl.multiple_of(start, tq)
            variant = jnp.where(jnp.logical_and(i == 0, c == 0), 1, 0) + jnp.where(
                jnp.logical_and(i == nsteps - 1, c == tiles - 1), 2, 0)
            k2 = ks[_rows(start, tk, r), :].astype(BF16)
            v2t = jnp.concatenate([vs[_rows(start, tk, r), :].T.astype(BF16), ones], axis=0)
            q8 = _stack_heads([tile_in(qs, q_ref, t, start) for t in range(4)], lo, hi, HEAD_DIM ** -0.5)
            s = _nt(k2, q8) + bias_ref[pl.ds(pl.multiple_of(variant * tk, 8), tk), :]
            m = jnp.max(s, axis=0, keepdims=True)
            if use_sink:
                m = jnp.maximum(m, sk)
            pv = _nn(v2t, jnp.exp(s - m).astype(BF16))
            l = pv[LANES:LANES + 1]
            if use_sink:
                l = l + jnp.exp(sk - m)
            o8t = pv[:LANES] / l
            lse8 = jnp.broadcast_to(m + jnp.log(l), (LANES, 8 * tq))
            for t in range(4):
                if r > 1:
                    os_[t, _rows(start, tq, r), :] = _tile_from_columns(o8t, t, tq)
                    ls[t, _rows(start, tq, r), :] = _tile_from_columns(lse8, t, tq)
                else:
                    o_ref[pl.ds(start, tq), pl.ds(t * LANES, LANES)] = _tile_from_columns(o8t, t, tq).astype(out_dtype)
                    lse_ref[pl.ds(start, tq), pl.ds(t * LANES, LANES)] = _tile_from_columns(lse8, t, tq)
            return carry

        lax.fori_loop(0, r * tiles, chain, 0, unroll=4)
        if r > 1:
            for t in range(4):
                o_ref[:, pl.ds(t * LANES, LANES)] = os_[t].astype(out_dtype)
                lse_ref[:, pl.ds(t * LANES, LANES)] = ls[t]

    in_specs = specs + [pl.BlockSpec((bias.shape[0] // n_units, 8 * tq), lambda u, i: (u, 0))]
    args = [qkv] * 7 + [bias]
    if use_sink:
        in_specs.append(pl.BlockSpec((8, 8 * tq), lambda u, i: (u, 0)))
        args.append(sink)
    wide = pl.BlockSpec((chunk, 4 * LANES), lambda u, i: (i, u))
    win = hb + chunk + hb
    big = lambda: pltpu.VMEM((4, chunk if r > 1 else 8, LANES), F32)
    return pl.pallas_call(
        body, name=name, grid=(n_units, nsteps),
        out_shape=(jax.ShapeDtypeStruct((S, n_units * 512), out_dtype), jax.ShapeDtypeStruct((S, n_units * 512), F32)),
        in_specs=in_specs, out_specs=(wide, wide),
        scratch_shapes=[big(), pltpu.VMEM((win, LANES), F32), pltpu.VMEM((win, LANES), F32), big(), big()],
        compiler_params=_params(("parallel", "parallel"), VMEM_LIMIT),
    )(*args)


def _attn_bwd(qkv, bias, sink, o, do, lse, *, C, r, half, qoff, koff, voff, n_units, name):
    S = qkv.shape[0]
    tq = _tile_rows(r)
    tk = tq + 2 * half
    tiles = CHUNK // (r * tq)
    nsteps = S // CHUNK
    specs, hb = _attn_layout(S, C, r, half, qoff, koff, voff, CHUNK)
    use_sink = sink is not None

    def body(*refs):
        q_ref, kp, km, kn, vp, vm, vn, bias_ref = refs[:8]
        rest = list(refs[8:])
        sink_ref = rest.pop(0) if use_sink else None
        o_ref, do_ref, lse_ref, dq_ref, dk_hbm, dv_hbm = rest[:6]
        rest = rest[6:]
        dsink_ref = rest.pop(0) if use_sink else None
        qs, ks, vs, os_, dos, ls, dqs, acck, accv, sem = rest
        u, i = pl.program_id(0), pl.program_id(1)

        @pl.when(i == 0)
        def _():
            acck[...] = jnp.zeros_like(acck)
            accv[...] = jnp.zeros_like(accv)
            if use_sink:
                dsink_ref[...] = jnp.zeros_like(dsink_ref)

        if r > 1:
            for t in range(4):
                cols = pl.ds(t * LANES, LANES)
                qs[t] = q_ref[:, cols].astype(F32)
                os_[t] = o_ref[:, cols].astype(F32)
                dos[t] = do_ref[:, cols].astype(F32)
                ls[t] = lse_ref[:, cols]
        _stage(ks, [kp, km, kn])
        _stage(vs, [vp, vm, vn])
        lo, hi = _head_masks(tq)

        def tile_in(staged, ref, t, start):
            if r > 1:
                return staged[t, _rows(start, tq, r), :]
            return ref[pl.ds(start, tq), pl.ds(t * LANES, LANES)].astype(F32)

        base = pl.multiple_of(i * CHUNK, CHUNK)
        if use_sink:
            sk = sink_ref[pl.ds(0, 1), :]

        def chain(n, carry):
            rho, c = n // tiles, n % tiles
            start = c * (tq * r) + rho
            if r == 1:
                start = pl.multiple_of(start, tq)
            variant = jnp.where(jnp.logical_and(i == 0, c == 0), 1, 0) + jnp.where(
                jnp.logical_and(i == nsteps - 1, c == tiles - 1), 2, 0)
            k2 = ks[_rows(start, tk, r), :].astype(BF16)
            v2 = vs[_rows(start, tk, r), :].astype(BF16)
            k2t = ks[_rows(start, tk, r), :].T.astype(BF16)
            q8 = _stack_heads([tile_in(qs, q_ref, t, start) for t in range(4)], lo, hi, HEAD_DIM ** -0.5)
            do_tiles = [tile_in(dos, do_ref, t, start) for t in range(4)]
            do8 = _stack_heads(do_tiles, lo, hi, 1.0)
            deltas, lses = [], []
            for t in range(4):
                prod_t = (do_tiles[t] * tile_in(os_, o_ref, t, start)).T
                lse_t = tile_in(ls, lse_ref, t, start).T
                for a in range(2):
                    deltas.append(jnp.sum(prod_t[a * HEAD_DIM:(a + 1) * HEAD_DIM], axis=0, keepdims=True))
                    lses.append(lse_t[a * HEAD_DIM:a * HEAD_DIM + 1])
            delta8 = jnp.concatenate(deltas, axis=1)
            lse8 = jnp.concatenate(lses, axis=1)
            s = _nt(k2, q8) + bias_ref[pl.ds(pl.multiple_of(variant * tk, 8), tk), :]
            p = jnp.exp(s - lse8)
            dp = _nt(v2, do8)
            dsb = (p * (dp - delta8)).astype(BF16)
            dq8t = _nn(k2t, dsb)
            for t in range(4):
                dq_t = _tile_from_columns(dq8t, t, tq) * (HEAD_DIM ** -0.5)
                if r > 1:
                    dqs[t, _rows(start, tq, r), :] = dq_t
                else:
                    dq_ref[pl.ds(start, tq), pl.ds(t * LANES, LANES)] = dq_t.astype(BF16)
            arow = base + start
            if r == 1:
                arow = pl.multiple_of(arow, tq)
            acck[_rows(arow, tk, r), :] = acck[_rows(arow, tk, r), :] + _nn(dsb, q8)
            accv[_rows(arow, tk, r), :] = accv[_rows(arow, tk, r), :] + _nn(p.astype(BF16), do8)
            if use_sink:
                e = jnp.exp(sk - lse8) * delta8
                for h in range(8):
                    part = -jnp.sum(e[:, h * tq:(h + 1) * tq], axis=1, keepdims=True)
                    dsink_ref[pl.ds(h, 1), :] = dsink_ref[pl.ds(h, 1), :] + part
            return carry

        lax.fori_loop(0, r * tiles, chain, 0, unroll=2)
        if r > 1:
            for t in range(4):
                dq_ref[:, pl.ds(t * LANES, LANES)] = dqs[t].astype(BF16)

        @pl.when(i == nsteps - 1)
        def _():
            ck = pltpu.make_async_copy(acck.at[pl.ds(hb, S)], dk_hbm.at[u], sem.at[0])
            cv = pltpu.make_async_copy(accv.at[pl.ds(hb, S)], dv_hbm.at[u], sem.at[1])
            ck.start()
            cv.start()
            ck.wait()
            cv.wait()

    wide = pl.BlockSpec((CHUNK, 4 * LANES), lambda u, i: (i, u))
    hbm = pl.BlockSpec(memory_space=pl.ANY)
    in_specs = specs + [pl.BlockSpec((3 * tk, 8 * tq), lambda u, i: (u, 0))]
    args = [qkv] * 7 + [bias]
    if use_sink:
        in_specs.append(pl.BlockSpec((8, 8 * tq), lambda u, i: (u, 0)))
        args.append(sink)
    in_specs += [wide, wide, wide]
    args += [o, do, lse]
    out_shape = [jax.ShapeDtypeStruct((S, n_units * 512), BF16), jax.ShapeDtypeStruct((n_units, S, LANES), F32),
                 jax.ShapeDtypeStruct((n_units, S, LANES), F32)]
    out_specs = [wide, hbm, hbm]
    if use_sink:
        out_shape.append(jax.ShapeDtypeStruct((n_units * 8, LANES), F32))
        out_specs.append(pl.BlockSpec((8, LANES), lambda u, i: (u, 0)))
    win = hb + CHUNK + hb
    big = lambda: pltpu.VMEM((4, CHUNK if r > 1 else 8, LANES), F32)
    res = pl.pallas_call(
        body, name=name, grid=(n_units, nsteps), out_shape=tuple(out_shape), in_specs=in_specs, out_specs=tuple(out_specs),
        scratch_shapes=[big(), pltpu.VMEM((win, LANES), F32), pltpu.VMEM((win, LANES), F32), big(), big(), big(), big(),
                        pltpu.VMEM((S + 2 * hb, LANES), F32), pltpu.VMEM((S + 2 * hb, LANES), F32), pltpu.SemaphoreType.DMA((2,))],
        compiler_params=_params(("arbitrary", "arbitrary"), VMEM_LIMIT),
    )(*args)
    return res[0], res[1], res[2], (res[3] if use_sink else None)


def _merge_groups(os_, lses):
    S, W = os_[0].shape
    tm = 512

    def body(o0, o1, o2, l0, l1, l2, o_ref, lse_ref):
        ls = [l0[...], l1[...], l2[...]]
        mx = jnp.maximum(jnp.maximum(ls[0], ls[1]), ls[2])
        es = [jnp.exp(l - mx) for l in ls]
        den = es[0] + es[1] + es[2]
        o = (es[0] / den) * o0[...] + (es[1] / den) * o1[...] + (es[2] / den) * o2[...]
        o_ref[...] = o.astype(BF16)
        lse_ref[...] = mx + jnp.log(den)

    row = pl.BlockSpec((tm, W), lambda i: (i, 0))
    return pl.pallas_call(
        body, name="merge_groups", grid=(S // tm,),
        out_shape=(jax.ShapeDtypeStruct((S, W), BF16), jax.ShapeDtypeStruct((S, W), F32)),
        in_specs=[row] * 6, out_specs=(row, row), compiler_params=_params(("parallel",), VMEM_LIMIT),
    )(*os_, *lses)


def _columns(part_refs, rows):
    return jnp.concatenate([r[rows, :].astype(BF16) for r in part_refs], axis=1)


def _piece_specs(parts, rows, row_axis):
    arrays, specs = [], []
    for p in parts:
        if isinstance(p, tuple):
            arr, u = p
            index = (lambda *g, u=u: (u, g[row_axis], 0))
            arrays.append(arr)
            specs.append(pl.BlockSpec((None, rows, arr.shape[2]), index))
        else:
            arrays.append(p)
            specs.append(pl.BlockSpec((rows, p.shape[1]), lambda *g: (g[row_axis], 0)))
    return arrays, specs


def _norm_bwd(dy_parts, wt, x, dres, nw, sc, *, name):
    S, N = x.shape[0], wt.shape[0]
    npart = len(dy_parts)
    tm = 512
    part_arrays, part_specs = _piece_specs(dy_parts, tm, 0)

    def body(*refs):
        part_refs = refs[:npart]
        w_ref, x_ref, dres_ref, nw_ref, sc_ref, dx_ref, st_ref = refs[npart:]

        @pl.when(pl.program_id(0) == 0)
        def _():
            st_ref[...] = jnp.zeros_like(st_ref)

        nwv, scale = nw_ref[...], 1.0 + sc_ref[...]
        sums = [jnp.zeros((1, D), F32)] * 3
        for half in range(SPLIT):
            rows = pl.ds(half * (tm // SPLIT), tm // SPLIT)
            dh = _nn(_columns(part_refs, rows), w_ref[...])
            xv = x_ref[rows, :]
            rstd = lax.rsqrt(jnp.mean(xv * xv, axis=-1, keepdims=True) + RMS_EPS)
            xh = xv * rstd
            dxh = dh * (nwv * scale)
            dx_ref[rows, :] = dres_ref[rows, :] + rstd * (dxh - xh * jnp.mean(dxh * xh, axis=-1, keepdims=True))
            dhx = dh * xh
            sums = [sums[0] + jnp.sum(dh, axis=0, keepdims=True), sums[1] + jnp.sum(dhx * nwv, axis=0, keepdims=True),
                    sums[2] + jnp.sum(dhx * scale, axis=0, keepdims=True)]
        for q in range(3):
            st_ref[pl.ds(q, 1), :] = st_ref[pl.ds(q, 1), :] + sums[q]

    row = lambda w_: pl.BlockSpec((tm, w_), lambda i: (i, 0))
    vec = _const_spec((1, D))
    return pl.pallas_call(
        body, name=name, grid=(S // tm,),
        out_shape=(jax.ShapeDtypeStruct((S, D), F32), jax.ShapeDtypeStruct((8, D), F32)),
        in_specs=part_specs + [_const_spec((N, D)), row(D), row(D), vec, vec],
        out_specs=(row(D), _const_spec((8, D))), compiler_params=_params(("arbitrary",), VMEM_LIMIT),
    )(*part_arrays, wt, x, dres, nw, sc)


def _ffn_bwd(dx, y, g, w_out, gu, wt_in, x, nw, sc, y1, g1, w_mix, *, mix_is_transposed, name):
    S = dx.shape[0]
    K = w_out.shape[0]
    Km = w_mix.shape[1] if mix_is_transposed else w_mix.shape[0]
    tm = 256

    def body(dx_ref, y_ref, g_ref, wo_ref, gu_ref, wi_ref, x_ref, nw_ref, sc_ref, y1_ref, g1_ref, wm_ref,
             dgu_ref, dyb_ref, dxo_ref, da_ref, dy1_ref, stg_ref, stf_ref, stm_ref):
        @pl.when(pl.program_id(0) == 0)
        def _():
            stg_ref[...] = jnp.zeros_like(stg_ref)
            stf_ref[...] = jnp.zeros_like(stf_ref)
            stm_ref[...] = jnp.zeros_like(stm_ref)

        dxv = dx_ref[...]
        stg_ref[pl.ds(0, 1), :] = stg_ref[pl.ds(0, 1), :] + jnp.sum(dxv * y_ref[...].astype(F32), axis=0, keepdims=True)
        dyb = (dxv * g_ref[...]).astype(BF16)
        dyb_ref[...] = dyb
        da = _nt(dyb, wo_ref[...])
        dgate = (da * gu_ref[:, pl.ds(0, K)].astype(F32)).astype(BF16)
        dup = (da * gu_ref[:, pl.ds(K, K)].astype(F32)).astype(BF16)
        dgu_ref[:, pl.ds(0, K)] = dgate
        dgu_ref[:, pl.ds(K, K)] = dup
        dh = _nn(dgate, wi_ref[pl.ds(0, K), :]) + _nn(dup, wi_ref[pl.ds(K, K), :])
        xv = x_ref[...]
        rstd = lax.rsqrt(jnp.mean(xv * xv, axis=-1, keepdims=True) + RMS_EPS)
        xh = xv * rstd
        nwv, scale = nw_ref[...], 1.0 + sc_ref[...]
        dxh = dh * (nwv * scale)
        dx1 = dxv + rstd * (dxh - xh * jnp.mean(dxh * xh, axis=-1, keepdims=True))
        dxo_ref[...] = dx1
        dhx = dh * xh
        stf_ref[pl.ds(0, 1), :] = stf_ref[pl.ds(0, 1), :] + jnp.sum(dh, axis=0, keepdims=True)
        stf_ref[pl.ds(1, 1), :] = stf_ref[pl.ds(1, 1), :] + jnp.sum(dhx * nwv, axis=0, keepdims=True)
        stf_ref[pl.ds(2, 1), :] = stf_ref[pl.ds(2, 1), :] + jnp.sum(dhx * scale, axis=0, keepdims=True)
        stm_ref[pl.ds(0, 1), :] = stm_ref[pl.ds(0, 1), :] + jnp.sum(dx1 * y1_ref[...].astype(F32), axis=0, keepdims=True)
        dy1 = (dx1 * g1_ref[...]).astype(BF16)
        dy1_ref[...] = dy1
        da_ref[...] = (_nn(dy1, wm_ref[...]) if mix_is_transposed else _nt(dy1, wm_ref[...])).astype(BF16)

    row = lambda w_: pl.BlockSpec((tm, w_), lambda i: (i, 0))
    vec = _const_spec((1, D))
    st = jax.ShapeDtypeStruct((8, D), F32)
    act = lambda w_: jax.ShapeDtypeStruct((S, w_), BF16)
    return pl.pallas_call(
        body, name=name, grid=(S // tm,),
        out_shape=(act(2 * K), act(D), jax.ShapeDtypeStruct((S, D), F32), act(Km), act(D), st, st, st),
        in_specs=[row(D), row(D), vec, _const_spec(w_out.shape), row(2 * K), _const_spec(wt_in.shape), row(D), vec, vec,
                  row(D), vec, _const_spec(w_mix.shape)],
        out_specs=(row(2 * K), row(D), row(D), row(Km), row(D), _const_spec((8, D)), _const_spec((8, D)), _const_spec((8, D))),
        compiler_params=_params(("arbitrary",), VMEM_LIMIT),
    )(dx, y, g, w_out, gu, wt_in, x, nw, sc, y1, g1, w_mix)


def _weight_grad(a, b_parts, *, transpose_out, name):
    S = a.shape[0]
    N = sum(p[0].shape[2] if isinstance(p, tuple) else p.shape[1] for p in b_parts)
    npart = len(b_parts)
    nb = N // 2 if N > 4096 else N
    assert nb == N or npart == 1
    tk = 512
    if npart == 1:
        part_arrays, part_specs = list(b_parts), [pl.BlockSpec((tk, nb), lambda n, k: (k, n))]
    else:
        part_arrays, part_specs = _piece_specs(b_parts, tk, 1)

    def body(*refs):
        a_ref, part_refs = refs[0], refs[1:1 + npart]
        out_ref, acc = refs[1 + npart:]
        k = pl.program_id(1)

        @pl.when(k == 0)
        def _():
            acc[...] = jnp.zeros_like(acc)

        acc[...] += _tn(a_ref[...], _columns(part_refs, slice(None)))

        @pl.when(k == pl.num_programs(1) - 1)
        def _():
            out_ref[...] = (acc[...].T if transpose_out else acc[...]).astype(BF16)

    out_block = pl.BlockSpec((nb, D), lambda n, k: (n, 0)) if transpose_out else pl.BlockSpec((D, nb), lambda n, k: (0, n))
    return pl.pallas_call(
        body, name=name, grid=(N // nb, S // tk),
        out_shape=jax.ShapeDtypeStruct((N, D) if transpose_out else (D, N), BF16),
        in_specs=[pl.BlockSpec((tk, D), lambda n, k: (k, 0))] + part_specs,
        out_specs=out_block, scratch_shapes=[pltpu.VMEM((D, nb), F32)],
        compiler_params=_params(("parallel", "arbitrary"), VMEM_LIMIT),
    )(a, *part_arrays)


def _adamw(w, g, m, v):
    m = ADAM_B1 * m + (1.0 - ADAM_B1) * g
    v = ADAM_B2 * v + (1.0 - ADAM_B2) * (g * g)
    m_hat = m / (1.0 - ADAM_B1 ** ADAM_STEP)
    v_hat = v / (1.0 - ADAM_B2 ** ADAM_STEP)
    delta = -ADAM_LR * (m_hat / (jnp.sqrt(v_hat) + ADAM_EPS) + ADAM_WD * w)
    return delta, m, v


def _adam_segment(parts, own, w, m, v, outs, layer, *, name):
    R, C = own.shape
    tr = 32
    block = pl.BlockSpec((1, tr, C), lambda i: (layer, i, 0))

    def body(p_ref, o_ref, w_ref, m_ref, v_ref, *rest):
        g_out, d_out, m_out, v_out = rest[-4:]
        me = _my_index()
        g = jnp.zeros((tr, C), F32)
        for j in range(N_DEV):
            g = g + jnp.where(me == j, o_ref[...], p_ref[j]).astype(F32)
        delta, mn, vn = _adamw(w_ref[0], g, m_ref[0], v_ref[0])
        g_out[0] = g
        d_out[0] = delta
        m_out[0] = mn
        v_out[0] = vn

    passed = [] if outs is None else [pl.BlockSpec(memory_space=pl.ANY)] * 4
    shp = jax.ShapeDtypeStruct(w.shape, F32)
    return pl.pallas_call(
        body, name=name, grid=(R // tr,), out_shape=(shp,) * 4,
        in_specs=[pl.BlockSpec((N_DEV, tr, C), lambda i: (0, i, 0)), pl.BlockSpec((tr, C), lambda i: (i, 0)), block, block, block] + passed,
        out_specs=(block,) * 4, input_output_aliases={5 + q: q for q in range(len(passed))},
        compiler_params=_params(("parallel",), VMEM_LIMIT),
    )(parts, own, w, m, v, *(outs or ()))


def _adam_ada_w(cond_t, dmod, w, m, v):
    ncol = w.shape[-1]
    tr = 512

    def body(c_ref, d_ref, w_ref, m_ref, v_ref, g_out, d_out, m_out, v_out):
        g = _nn(c_ref[...], d_ref[0])
        delta, mn, vn = _adamw(w_ref[0], g, m_ref[0], v_ref[0])
        g_out[0] = g
        d_out[0] = delta
        m_out[0] = mn
        v_out[0] = vn

    blk = pl.BlockSpec((1, tr, ncol), lambda l, i: (l, i, 0))
    shp = jax.ShapeDtypeStruct(w.shape, F32)
    return pl.pallas_call(
        body, name="adam_ada_w", grid=(DEPTH, D // tr), out_shape=(shp,) * 4,
        in_specs=[pl.BlockSpec((tr, LANES), lambda l, i: (i, 0)), pl.BlockSpec((1, LANES, ncol), lambda l, i: (l, 0, 0)), blk, blk, blk],
        out_specs=(blk,) * 4, compiler_params=_params(("parallel", "parallel"), VMEM_LIMIT),
    )(cond_t, dmod, w, m, v)


TILE_ROWS = 168


def _stat_sources():
    pairs = []
    for i in range(DEPTH):
        b = 32 * i
        for q, src in enumerate((b, b + 1, b + 8, b + 16, b + 17, b + 24)):
            pairs.append((6 * i + q, src))
        pairs.append((24 + i, b + 2))
        pairs.append((32 + i, b + 18))
    pairs += [(40, 128), (41, 129)]
    return pairs


def _small_exchange(tiles, w, m, v):
    loss_row, sink_row, sink_src = 41, 48, 136

    def body(s_ref, w_ref, m_ref, v_ref, dmod_out, g_out, d_out, m_out, v_out, loss_out, all_ref, tot_ref, send_sems, recv_sems):
        me = _my_index()
        all_ref[me] = s_ref[...]
        copies = []
        for k in range(1, N_DEV):
            dev, _ = _peer(k)
            cp = pltpu.make_async_remote_copy(src_ref=s_ref, dst_ref=all_ref.at[me], send_sem=send_sems.at[k - 1],
                                              recv_sem=recv_sems.at[k - 1], device_id=dev, device_id_type=MESH)
            cp.start()
            copies.append(cp)
        for k in range(1, N_DEV):
            dev, pidx = _peer(k)
            pltpu.make_async_remote_copy(src_ref=s_ref, dst_ref=all_ref.at[pidx], send_sem=send_sems.at[k - 1],
                                         recv_sem=recv_sems.at[k - 1], device_id=dev, device_id_type=MESH).wait_recv()
        for cp in copies:
            cp.wait_send()
        tot = all_ref[0]
        for j in range(1, N_DEV):
            tot = tot + all_ref[j]
        tot_ref[...] = tot
        g_out[...] = jnp.zeros_like(g_out)
        for dst, src in _stat_sources():
            g_out[pl.ds(dst, 1), :] = tot_ref[pl.ds(src, 1), :]
            if dst < 24:
                for j in range(N_DEV):
                    dmod_out[j, pl.ds(dst, 1), :] = all_ref[j, pl.ds(src, 1), :]
        lane = lax.broadcasted_iota(jnp.int32, (1, D), 1)
        sink = jnp.zeros((1, D), F32)
        for h in range(32):
            sink = jnp.where(lane == h, tot_ref[pl.ds(sink_src + h, 1), :], sink)
        g_out[pl.ds(sink_row, 1), :] = sink
        g = g_out[...]
        delta, mn, vn = _adamw(w_ref[...], g, m_ref[...], v_ref[...])
        d_out[...] = delta
        m_out[...] = mn
        v_out[...] = vn
        loss = jnp.sum(g[loss_row:loss_row + 1, :], axis=-1, keepdims=True) * (0.5 / D)
        loss_out[...] = jnp.broadcast_to(loss, loss_out.shape)

    vm = pl.BlockSpec(memory_space=pltpu.VMEM)
    shp = jax.ShapeDtypeStruct((STAT_ROWS, D), F32)
    return pl.pallas_call(
        body, name="small_exchange",
        out_shape=(jax.ShapeDtypeStruct((N_DEV, 24, D), F32), shp, shp, shp, shp, jax.ShapeDtypeStruct((8, LANES), F32)),
        in_specs=[vm] * 4, out_specs=(vm,) * 6,
        scratch_shapes=[pltpu.VMEM((N_DEV, TILE_ROWS, D), F32), pltpu.VMEM((TILE_ROWS, D), F32),
                        pltpu.SemaphoreType.DMA((N_DEV - 1,)), pltpu.SemaphoreType.DMA((N_DEV - 1,))],
        compiler_params=_params(vmem=VMEM_LIMIT),
    )(tiles, w, m, v)


def _to_rows(name, a):
    if name in ("ffn_in", "a_in", "b_in"):
        return a.T
    if name == "b_out":
        return a.T.reshape(-1, D)
    return a


def _rows8(a):
    return jnp.pad(a, ((0, 8 - a.shape[0]), (0, 0)))


def _pack_small(ada_b, norm_mix, norm_ffn, final_norm, sink):
    sink_row = jnp.pad(sink.reshape(1, -1), ((0, 0), (0, D - sink.size)))
    return jnp.concatenate([ada_b.reshape(24, D), _rows8(norm_mix), _rows8(norm_ffn), _rows8(final_norm.reshape(1, D)),
                            _rows8(sink_row)], axis=0)


def _unpack_small(a):
    return a[0:24].reshape(4, 6 * D), a[24:28], a[32:36], a[40], a[48, :32].reshape(2, 16)


def kernel(x, c, ada_w, ada_b, norm_mix, norm_ffn, ffn_w_in, ffn_w_out, a_w_in, a_w_out, a_sink, b_w_in, b_w_out, final_norm, loss_target, m_ada_w, m_ada_b, m_norm_mix, m_norm_ffn, m_ffn_w_in, m_ffn_w_out, m_a_w_in, m_a_w_out, m_a_sink, m_b_w_in, m_b_w_out, m_final_norm, v_ada_w, v_ada_b, v_norm_mix, v_norm_ffn, v_ffn_w_in, v_ffn_w_out, v_a_w_in, v_a_w_out, v_a_sink, v_b_w_in, v_b_w_out, v_final_norm):
    S = x.shape[1]
    x0 = x.reshape(S, D)
    target = loss_target.reshape(S, D)
    me = _my_index()
    ncol = ada_w.shape[-1]

    ada_b_mine = lax.dynamic_slice_in_dim(ada_b, me * ncol, ncol, axis=1)
    cond_all, parts = _cond_exchange(jnp.broadcast_to(c.reshape(1, D), (8, D)), ada_w, ada_b_mine)
    mod = lax.dynamic_index_in_dim(parts, me, axis=2, keepdims=False)
    mod = jnp.transpose(mod, (1, 0, 2)).reshape(DEPTH, 6, 1, D)

    weights = {"ffn_in": ffn_w_in, "ffn_out": ffn_w_out, "a_in": a_w_in, "a_out": a_w_out, "b_in": b_w_in, "b_out": b_w_out}
    shard = {(n, l): _to_rows(n, weights[n][l]).astype(BF16) for n, l, _ in SEGMENTS}
    first = [sg for sg in _layer_segments(0) if not sg[0].startswith("ffn")]
    gathered0 = _all_gather_weights([shard[(n, l)] for n, l, _ in first])
    W = {(n, l): g for (n, l, _), g in zip(first, gathered0)}
    groups = [[sg for sg in _layer_segments(0) if sg[0].startswith("ffn")]] + [_layer_segments(i) for i in range(1, DEPTH)]
    gathers, order = [], gathered0[0]
    for q, segs in enumerate(groups):
        zones = [(N_DEV, rows, D) for _, _, rows in segs]
        gathers.append(_exchange_start([shard[(n, l)] for n, l, _ in segs], zones, [(s, 0) for s in range(len(segs))],
                                       [sg[2] for sg in segs], False, order, "weight_gather_start_%d" % q))
        order = gathers[-1][-1]
    gather_token = order[0:1, 0:1]

    def finish_gather(q, after):
        segs = groups[q]
        zones = _exchange_wait(gathers[q], len(segs), [(s, 0) for s in range(len(segs))], [sg[2] for sg in segs], True, after,
                               "weight_gather_wait_%d" % q)
        for (n, l, rows), zone in zip(segs, zones):
            W[(n, l)] = zone.reshape(D, 512) if n == "b_out" else zone.reshape(N_DEV * rows, D)

    a_slopes, b_slopes = _slopes(16), _slopes(24)
    bias_a = _alibi_bias(a_slopes, A_HALF, 1)
    bias_b = [_alibi_bias(b_slopes[8 * g:8 * g + 8], B_HALF, dil) for g, dil in enumerate(B_DILS)]
    bias_b_fwd = [_alibi_bias(b_slopes[8 * g:8 * g + 8], B_HALF, dil, max(CHUNK, TQ * dil), both=True) for g, dil in enumerate(B_DILS)]
    a_geom = dict(C=A_QKV, r=1, half=A_HALF, qoff=0, koff=1024, voff=1280, n_units=2)
    b_geom = [dict(C=B_QKV, r=dil, half=B_HALF, qoff=512 * g, koff=1536 + 128 * g, voff=1920 + 128 * g, n_units=1)
              for g, dil in enumerate(B_DILS)]

    saved = []
    xcur = x0
    for i in range(DEPTH):
        j = i // 2
        sh1, sc1, g1, sh2, sc2, g2 = [mod[i, q] for q in range(6)]
        nm, nf = norm_mix[i].reshape(1, D), norm_ffn[i].reshape(1, D)
        if i == 0:
            nm = nm + gather_token
        if i > 0:
            finish_gather(i, xcur)
        if i % 2 == 0:
            sink_rep = jnp.repeat(jnp.repeat(a_sink[j], TQ).reshape(2, 1, 8 * TQ), 8, axis=1).reshape(16, 8 * TQ)
            h1, qkv = _proj(xcur, nm, sc1, sh1, W[("a_in", j)], name="proj_a")
            o, lse = _attn_fwd(qkv, bias_a, sink_rep, out_dtype=BF16, name="attn_a_fwd", **a_geom)
            if i == 0:
                finish_gather(0, o)
            x1, y1, h2, gu, act = _out_ffn_in(o, W[("a_out", j)], xcur, g1, nf, sc2, sh2, W[("ffn_in", i)],
                                              w_is_transposed=False, name="out_a_ffn_in")
        else:
            sink_rep = None
            h1, qkv = _proj(xcur, nm, sc1, sh1, W[("b_in", j)], name="proj_b")
            outs = [_attn_fwd(qkv, bias_b_fwd[g], None, out_dtype=F32, name="attn_b%d_fwd" % g, **b_geom[g]) for g in range(3)]
            o, lse = _merge_groups([t[0] for t in outs], [t[1] for t in outs])
            x1, y1, h2, gu, act = _out_ffn_in(o, W[("b_out", j)], xcur, g1, nf, sc2, sh2, W[("ffn_in", i)],
                                              w_is_transposed=True, name="out_b_ffn_in")
        if i < DEPTH - 1:
            x2, y2 = _ffn_out(act, W[("ffn_out", i)], x1, g2, None, None, name="ffn_out")
        else:
            x2, y2, head_stats = _ffn_out(act, W[("ffn_out", i)], x1, g2, target, final_norm.reshape(1, D), name="ffn_out_loss")
        saved.append(dict(x0=xcur, h1=h1, qkv=qkv, o=o, lse=lse, y1=y1, x1=x1, h2=h2, gu=gu, act=act, y2=y2, sink=sink_rep))
        xcur = x2

    dx = xcur

    dW = {}
    stat_tiles, dsink = [None] * DEPTH, [None] * 2
    exchanges = []
    start_token = None

    def start_exchange(segs):
        own = [lax.dynamic_slice_in_dim(dW[(n, l)], me * rows, rows, axis=0) for n, l, rows in segs]
        zones = [(N_DEV, rows, D) for _, _, rows in segs]
        started = _exchange_start([dW[(n, l)] for n, l, _ in segs], zones, [(s, 0) for s in range(len(segs))],
                                  [sg[2] for sg in segs], True, own[0], "grad_exchange_start_%d" % len(exchanges))
        exchanges.append((segs, started, own))
        return started[-1][0:1, 0:1]

    for i in reversed(range(DEPTH)):
        j = i // 2
        sv = saved[i]
        sh1, sc1, g1, sh2, sc2, g2 = [mod[i, q] for q in range(6)]
        if start_token is not None:
            g2 = g2 + start_token
            start_token = None
        nm, nf = norm_mix[i].reshape(1, D), norm_ffn[i].reshape(1, D)
        mix = "a_out" if i % 2 == 0 else "b_out"
        dgu, dy2, dx1, do, dy1, st_g2, st_f, st_g1 = _ffn_bwd(
            dx, sv["y2"], g2, W[("ffn_out", i)], sv["gu"], W[("ffn_in", i)], sv["x1"], nf, sc2, sv["y1"], g1, W[(mix, j)],
            mix_is_transposed=(i % 2 == 1), name="ffn_bwd_" + mix)
        dW[("ffn_out", i)] = _weight_grad(dy2, [sv["act"]], transpose_out=True, name="dw_ffn_out")
        dW[("ffn_in", i)] = _weight_grad(sv["h2"], [dgu], transpose_out=True, name="dw_ffn_in")
        sink_bwd = sv["sink"]
        if i == 0:
            sink_bwd = sink_bwd + start_exchange([sg for sg in _layer_segments(0) if sg[0].startswith("ffn")])
        if i % 2 == 0:
            dW[("a_out", j)] = _weight_grad(dy1, [sv["o"]], transpose_out=True, name="dw_a_out")
            dq, dk, dv, ds = _attn_bwd(sv["qkv"], bias_a, sink_bwd, sv["o"], do, sv["lse"], name="attn_a_bwd", **a_geom)
            dsink[j] = ds
            dqkv = [dq, (dk, 0), (dk, 1), (dv, 0), (dv, 1)]
            dW[("a_in", j)] = _weight_grad(sv["h1"], dqkv, transpose_out=True, name="dw_a_in")
            dx0, st_m = _norm_bwd(dqkv, W[("a_in", j)], sv["x0"], dx1, nm, sc1, name="proj_a_bwd")
        else:
            dW[("b_out", j)] = _weight_grad(dy1, [sv["o"]], transpose_out=False, name="dw_b_out").reshape(N_DEV * 64, D)
            gr = [_attn_bwd(sv["qkv"], bias_b[g], None, sv["o"], do, sv["lse"], name="attn_b%d_bwd" % g, **b_geom[g]) for g in range(3)]
            dqkv = [t[0] for t in gr] + [(t[1], 0) for t in gr] + [(t[2], 0) for t in gr]
            dW[("b_in", j)] = _weight_grad(sv["h1"], dqkv, transpose_out=True, name="dw_b_in")
            dx0, st_m = _norm_bwd(dqkv, W[("b_in", j)], sv["x0"], dx1, nm, sc1, name="proj_b_bwd")
        stat_tiles[i] = [st_m, st_g1, st_f, st_g2]
        if i > 0:
            start_token = start_exchange(_layer_segments(i))
        else:
            start_exchange([sg for sg in _layer_segments(0) if not sg[0].startswith("ffn")])
        dx = dx0
    grad_x = dx.reshape(1, S, D)

    masters = {"ffn_in": (ffn_w_in, m_ffn_w_in, v_ffn_w_in), "ffn_out": (ffn_w_out, m_ffn_w_out, v_ffn_w_out),
               "a_in": (a_w_in, m_a_w_in, v_a_w_in), "a_out": (a_w_out, m_a_w_out, v_a_w_out),
               "b_in": (b_w_in, m_b_w_in, v_b_w_in), "b_out": (b_w_out, m_b_w_out, v_b_w_out)}
    by_rows = ("ffn_in", "a_in", "b_in", "b_out")
    masters = {n: tuple(jnp.swapaxes(t, 1, 2) for t in wmv) if n in by_rows else wmv for n, wmv in masters.items()}
    results = {n: None for n in masters}
    after = dx
    for e, (segs, started, own) in enumerate(exchanges):
        zones = _exchange_wait(started, len(segs), [(s, 0) for s in range(len(segs))], [sg[2] for sg in segs], False, after,
                               "grad_exchange_wait_%d" % e)
        for (n, l, rows), zone, mine in zip(segs, zones, own):
            if n == "b_out":
                zone, mine = zone.reshape(N_DEV, LANES, 512), mine.reshape(LANES, 512)
            results[n] = _adam_segment(zone, mine, *masters[n], results[n], l, name="adam_" + n)
            after = results[n][0]
    big = {(kind, n): jnp.swapaxes(results[n][q], 1, 2) if n in by_rows else results[n][q]
           for q, kind in enumerate(("grad", "delta", "m", "v")) for n in masters}

    tiles = jnp.concatenate([t for i in range(DEPTH) for t in stat_tiles[i]] + [head_stats]
                            + [jnp.pad(ds, ((0, 0), (0, D - LANES))) for ds in dsink], axis=0)
    small = [_pack_small(*t) for t in ((ada_b, norm_mix, norm_ffn, final_norm, a_sink),
                                       (m_ada_b, m_norm_mix, m_norm_ffn, m_final_norm, m_a_sink),
                                       (v_ada_b, v_norm_mix, v_norm_ffn, v_final_norm, v_a_sink))]
    dmod_all, sg, sd, sm, sv_, loss_tile = _small_exchange(tiles, *small)
    loss = loss_tile[0, 0]
    dmod_all = dmod_all.reshape(N_DEV, DEPTH, 6 * D)
    dmod_mine = lax.dynamic_slice_in_dim(dmod_all, me * ncol, ncol, axis=2)
    dmod_pad = jnp.pad(jnp.transpose(dmod_mine, (1, 0, 2)), ((0, 0), (0, LANES - N_DEV), (0, 0))).astype(BF16)
    cond_t = jnp.pad(cond_all.T, ((0, 0), (0, LANES - N_DEV))).astype(BF16)
    ada = _adam_ada_w(cond_t, dmod_pad, ada_w, m_ada_w, v_ada_w)

    outs = [loss, grad_x]
    small_res = [_unpack_small(t) for t in (sg, sd, sm, sv_)]
    for q, kind in enumerate(("grad", "delta", "m", "v")):
        ab, nm_, nf_, fn, sk = small_res[q]
        outs += [ada[q], ab, nm_, nf_, big[(kind, "ffn_in")], big[(kind, "ffn_out")], big[(kind, "a_in")], big[(kind, "a_out")],
                 sk, big[(kind, "b_in")], big[(kind, "b_out")], fn]
    return tuple(outs)
```

```python
import math

import numpy as np
import jax
import jax.numpy as jnp
from jax import lax
from jax.experimental import pallas as pl
from jax.experimental.pallas import tpu as pltpu

D = 1024
HEAD_DIM = 64
D_FF = 2816
DEPTH = 4
N_DEV = 8
A_QKV = 1536
B_QKV = 2304
A_HALF = 128
B_HALF = 64
B_DILS = (1, 4, 16)
RMS_EPS = 1e-6
NEG = -1e30
ADAM_LR = 0.001
ADAM_B1 = 0.9
ADAM_B2 = 0.999
ADAM_EPS = 1e-08
ADAM_WD = 0.01
ADAM_STEP = 10

LANES = 128
SPLIT = 2
TQ = 128
VMEM_LIMIT = 56 * 1024 * 1024
MESH = pl.DeviceIdType.MESH
F32 = jnp.float32
BF16 = jnp.bfloat16

SEGMENTS = ([("ffn_in", l, 704) for l in range(4)] + [("ffn_out", l, 352) for l in range(4)]
            + [("a_in", j, 192) for j in range(2)] + [("a_out", j, 128) for j in range(2)]
            + [("b_in", j, 288) for j in range(2)] + [("b_out", j, 64) for j in range(2)])
def _layer_segments(i):
    mixer = "a" if i % 2 == 0 else "b"
    return [s for s in SEGMENTS if (s[0].startswith("ffn") and s[1] == i) or (s[0].startswith(mixer + "_") and s[1] == i // 2)]


def _offsets(segs):
    rows = [s[2] for s in segs]
    return [sum(rows[:k]) for k in range(len(rows))], sum(rows)
STAT_ROWS = 56


def _nn(a, b):
    return jnp.dot(a, b, preferred_element_type=F32)


def _nt(a, b):
    return lax.dot_general(a, b, (((1,), (1,)), ((), ())), preferred_element_type=F32)


def _tn(a, b):
    return lax.dot_general(a, b, (((0,), (0,)), ((), ())), preferred_element_type=F32)


def _params(dims=None, vmem=None):
    kw = {}
    if dims is not None:
        kw["dimension_semantics"] = dims
    if vmem is not None:
        kw["vmem_limit_bytes"] = vmem
    return pltpu.CompilerParams(**kw)


def _my_index():
    return 4 * lax.axis_index("x") + 2 * lax.axis_index("y") + lax.axis_index("c")


def _peer(k):
    x, y, c = lax.axis_index("x"), lax.axis_index("y"), lax.axis_index("c")
    px, py, pc = x ^ ((k >> 2) & 1), y ^ ((k >> 1) & 1), c ^ (k & 1)
    return (px, py, pc), 4 * px + 2 * py + pc


def _const_spec(shape):
    nd = len(shape)
    return pl.BlockSpec(shape, lambda *_: (0,) * nd)


def _cond_exchange(c_tile, ada_w, ada_b_mine):
    ncol = ada_w.shape[-1]

    def body(c_ref, w_ref, b_ref, cond_ref, parts_ref, call_ref, mine_ref, send_sems, recv_sems):
        me = _my_index()
        call_ref[me] = c_ref[...]
        copies = []
        for k in range(1, N_DEV):
            dev, _ = _peer(k)
            cp = pltpu.make_async_remote_copy(src_ref=c_ref, dst_ref=call_ref.at[me], send_sem=send_sems.at[0, k - 1],
                                              recv_sem=recv_sems.at[0, k - 1], device_id=dev, device_id_type=MESH)
            cp.start()
            copies.append(cp)
        for k in range(1, N_DEV):
            _, pidx = _peer(k)
            pltpu.make_async_remote_copy(src_ref=c_ref, dst_ref=call_ref.at[pidx], send_sem=send_sems.at[0, k - 1],
                                         recv_sem=recv_sems.at[0, k - 1], device_id=_peer(k)[0], device_id_type=MESH).wait_recv()
        for cp in copies:
            cp.wait_send()
        row = lax.broadcasted_iota(jnp.int32, (N_DEV, D), 0)
        cmat = jnp.zeros((N_DEV, D), F32)
        for j in range(N_DEV):
            cmat = jnp.where(row == j, call_ref[j], cmat)
        cond = cmat * jax.nn.sigmoid(cmat)
        cond_ref[...] = cond
        cb = cond.astype(BF16)
        for l in range(DEPTH):
            mine_ref[l] = _nn(cb, w_ref[l].astype(BF16)) + b_ref[pl.ds(l, 1), :]
        parts_ref[me] = mine_ref[...]
        copies = []
        for k in range(1, N_DEV):
            dev, _ = _peer(k)
            cp = pltpu.make_async_remote_copy(src_ref=mine_ref, dst_ref=parts_ref.at[me], send_sem=send_sems.at[1, k - 1],
                                              recv_sem=recv_sems.at[1, k - 1], device_id=dev, device_id_type=MESH)
            cp.start()
            copies.append(cp)
        for k in range(1, N_DEV):
            dev, pidx = _peer(k)
            pltpu.make_async_remote_copy(src_ref=mine_ref, dst_ref=parts_ref.at[pidx], send_sem=send_sems.at[1, k - 1],
                                         recv_sem=recv_sems.at[1, k - 1], device_id=dev, device_id_type=MESH).wait_recv()
        for cp in copies:
            cp.wait_send()

    vm = pl.BlockSpec(memory_space=pltpu.VMEM)
    return pl.pallas_call(
        body, name="cond_exchange",
        out_shape=(jax.ShapeDtypeStruct((N_DEV, D), F32), jax.ShapeDtypeStruct((N_DEV, DEPTH, N_DEV, ncol), F32)),
        in_specs=[vm, vm, vm], out_specs=(vm, vm),
        scratch_shapes=[pltpu.VMEM((N_DEV, N_DEV, D), F32), pltpu.VMEM((DEPTH, N_DEV, ncol), F32),
                        pltpu.SemaphoreType.DMA((2, N_DEV - 1)), pltpu.SemaphoreType.DMA((2, N_DEV - 1))],
        compiler_params=_params(vmem=VMEM_LIMIT),
    )(c_tile, ada_w, ada_b_mine)[:2]


def _all_gather_weights(shards):
    n = len(shards)
    big = max(range(n), key=lambda s: shards[s].shape[0])
    total = sum(sh.shape[0] for sh in shards)
    assert N_DEV * shards[big].shape[0] >= total

    def body(*refs):
        ins, outs = refs[:n], refs[n:2 * n]
        local_sems, send_sems, recv_sems = refs[2 * n:]
        me = _my_index()
        local = []
        for s in range(n):
            rows = ins[s].shape[0]
            cp = pltpu.make_async_copy(ins[s], outs[s].at[pl.ds(me * rows, rows)], local_sems.at[s])
            cp.start()
            local.append(cp)
        for k in range(1, N_DEV):
            dev, _ = _peer(k)
            for s in range(n):
                rows = ins[s].shape[0]
                pltpu.make_async_remote_copy(src_ref=ins[s], dst_ref=outs[s].at[pl.ds(me * rows, rows)],
                                             send_sem=send_sems.at[k - 1], recv_sem=recv_sems.at[k - 1],
                                             device_id=dev, device_id_type=MESH).start()
        whole = outs[big].at[pl.ds(0, total)]
        for k in range(1, N_DEV):
            dev, _ = _peer(k)
            w = pltpu.make_async_remote_copy(src_ref=whole, dst_ref=whole, send_sem=send_sems.at[k - 1],
                                             recv_sem=recv_sems.at[k - 1], device_id=dev, device_id_type=MESH)
            w.wait_send()
            w.wait_recv()
        for cp in local:
            cp.wait()

    hbm = pl.BlockSpec(memory_space=pl.ANY)
    return pl.pallas_call(
        body, name="weight_all_gather",
        out_shape=tuple(jax.ShapeDtypeStruct((N_DEV * s.shape[0], D), s.dtype) for s in shards),
        in_specs=[hbm] * n, out_specs=tuple([hbm] * n),
        scratch_shapes=[pltpu.SemaphoreType.DMA((n,)), pltpu.SemaphoreType.DMA((N_DEV - 1,)),
                        pltpu.SemaphoreType.DMA((N_DEV - 1,))],
    )(*shards)


HBM = pl.BlockSpec(memory_space=pltpu.HBM)
SEM = pl.BlockSpec(memory_space=pltpu.SEMAPHORE)
EFFECT = pltpu.SideEffectType.DATAFLOW_SIDE_EFFECTING


def _exchange_start(srcs, landings, dst, rows, to_peer_rows, after, name):
    n, nl = len(srcs), len(landings)

    def body(*refs):
        src_refs = refs[:n]
        send_sems, recv_sems = refs[n + 1], refs[n + 2]
        land_refs = refs[2 * n + 3:2 * n + 3 + nl]
        token = refs[-1]
        me = _my_index()
        for k in range(1, N_DEV):
            dev, pidx = _peer(k)
            for q in range(n):
                src = src_refs[q].at[pl.ds(pidx * rows[q], rows[q])] if to_peer_rows else src_refs[q]
                pltpu.make_async_remote_copy(src_ref=src, dst_ref=land_refs[dst[q][0]].at[me, pl.ds(dst[q][1], rows[q])],
                                             send_sem=send_sems.at[k * n + q], recv_sem=recv_sems.at[k * n + q],
                                             device_id=dev, device_id_type=MESH).start()
        if not to_peer_rows:
            for q in range(n):
                pltpu.make_async_copy(src_refs[q], land_refs[dst[q][0]].at[me, pl.ds(dst[q][1], rows[q])], send_sems.at[q]).start()
        token[...] = jnp.zeros_like(token)

    sems = pltpu.SemaphoreType.DMA((N_DEV * n,))
    return pl.pallas_call(
        body, name=name,
        out_shape=(sems, sems, *[pltpu.HBM(a.shape, a.dtype) for a in srcs], *[pltpu.HBM(shape, BF16) for shape in landings],
                   jax.ShapeDtypeStruct((8, LANES), F32)),
        in_specs=[HBM] * n + [pl.BlockSpec(memory_space=pl.ANY)],
        out_specs=(SEM, SEM, *[HBM] * (n + nl), pl.BlockSpec(memory_space=pltpu.VMEM)),
        input_output_aliases={q: 2 + q for q in range(n)},
        compiler_params=pltpu.CompilerParams(has_side_effects=EFFECT),
    )(*[pltpu.with_memory_space_constraint(a, pltpu.HBM) for a in srcs], after)


def _exchange_wait(started, n, dst, rows, own_slot, after, name):
    send_sems, recv_sems = started[0], started[1]
    arrays = list(started[2:-1])
    n1 = len(arrays)

    def body(*refs):
        land_refs = refs[n:n1]
        sends, recvs = refs[n1], refs[n1 + 1]
        for k in range(1, N_DEV):
            dev, _ = _peer(k)
            for q in range(n):
                slot = land_refs[dst[q][0]].at[0, pl.ds(dst[q][1], rows[q])]
                w = pltpu.make_async_remote_copy(src_ref=slot, dst_ref=slot, send_sem=sends.at[k * n + q],
                                                 recv_sem=recvs.at[k * n + q], device_id=dev, device_id_type=MESH)
                w.wait_send()
                w.wait_recv()
        if own_slot:
            for q in range(n):
                slot = land_refs[dst[q][0]].at[0, pl.ds(dst[q][1], rows[q])]
                pltpu.make_async_copy(slot, slot, sends.at[q]).wait()

    return pl.pallas_call(
        body, name=name, out_shape=tuple(pltpu.HBM(a.shape, a.dtype) for a in arrays),
        in_specs=[HBM] * n1 + [SEM, SEM, pl.BlockSpec(memory_space=pl.ANY)], out_specs=tuple([HBM] * n1),
        input_output_aliases={q: q for q in range(n1)},
        compiler_params=pltpu.CompilerParams(has_side_effects=EFFECT),
    )(*arrays, send_sems, recv_sems, after)[n:]


def _norm_mod(x, nw, sc, sh):
    ms = jnp.mean(x * x, axis=-1, keepdims=True)
    xh = x * lax.rsqrt(ms + RMS_EPS)
    return xh, (xh * nw) * (1.0 + sc) + sh


def _proj(x, nw, sc, sh, wt, *, name):
    S, N = x.shape[0], wt.shape[0]
    tm = 512

    def body(x_ref, nw_ref, sc_ref, sh_ref, w_ref, h_ref, out_ref):
        for half in range(SPLIT):
            rows = pl.ds(half * (tm // SPLIT), tm // SPLIT)
            _, h = _norm_mod(x_ref[rows, :], nw_ref[...], sc_ref[...], sh_ref[...])
            hb = h.astype(BF16)
            h_ref[rows, :] = hb
            out_ref[rows, :] = _nt(hb, w_ref[...]).astype(BF16)

    row = lambda w: pl.BlockSpec((tm, w), lambda i: (i, 0))
    vec = _const_spec((1, D))
    return pl.pallas_call(
        body, name=name, grid=(S // tm,), out_shape=(jax.ShapeDtypeStruct((S, D), BF16), jax.ShapeDtypeStruct((S, N), BF16)),
        in_specs=[row(D), vec, vec, vec, _const_spec((N, D))], out_specs=(row(D), row(N)),
        compiler_params=_params(("parallel",), VMEM_LIMIT),
    )(x, nw, sc, sh, wt)


def _ffn_out(a, w, x, g, target, fnw, *, name):
    S, K = a.shape
    tm = 512
    last = target is not None

    def body(a_ref, w_ref, x_ref, g_ref, *rest):
        y = _nn(a_ref[...], w_ref[...])
        xv = x_ref[...] + g_ref[...] * y
        if not last:
            xo_ref, y_ref = rest
            y_ref[...] = y.astype(BF16)
            xo_ref[...] = xv
            return
        t_ref, fw_ref, dx_ref, y_ref, st_ref = rest
        y_ref[...] = y.astype(BF16)

        @pl.when(pl.program_id(0) == 0)
        def _():
            st_ref[...] = jnp.zeros_like(st_ref)

        rstd = lax.rsqrt(jnp.mean(xv * xv, axis=-1, keepdims=True) + RMS_EPS)
        xh = xv * rstd
        err = xh * fw_ref[...] - t_ref[...]
        dy = err * (1.0 / D)
        dxh = dy * fw_ref[...]
        dx_ref[...] = rstd * (dxh - xh * jnp.mean(dxh * xh, axis=-1, keepdims=True))
        st_ref[pl.ds(0, 1), :] = st_ref[pl.ds(0, 1), :] + jnp.sum(dy * xh, axis=0, keepdims=True)
        st_ref[pl.ds(1, 1), :] = st_ref[pl.ds(1, 1), :] + jnp.sum(err * err, axis=0, keepdims=True)

    row = lambda w_: pl.BlockSpec((tm, w_), lambda i: (i, 0))
    in_specs = [row(K), _const_spec(w.shape), row(D), _const_spec((1, D))]
    args = [a, w, x, g]
    out_shape = [jax.ShapeDtypeStruct((S, D), F32), jax.ShapeDtypeStruct((S, D), BF16)]
    out_specs = [row(D), row(D)]
    if last:
        in_specs += [row(D), _const_spec((1, D))]
        args += [target, fnw]
        out_shape.append(jax.ShapeDtypeStruct((8, D), F32))
        out_specs.append(_const_spec((8, D)))
    return pl.pallas_call(
        body, name=name, grid=(S // tm,), out_shape=tuple(out_shape), in_specs=in_specs, out_specs=tuple(out_specs),
        compiler_params=_params(("arbitrary",) if last else ("parallel",), VMEM_LIMIT),
    )(*args)


CHUNK = 1024


def _tile_rows(r, chunk=CHUNK):
    return min(TQ, chunk // r)


def _out_ffn_in(a, w_mix, x, g, nw, sc, sh, wt, *, w_is_transposed, name):
    S, K = a.shape
    tm = 256

    def body(a_ref, wm_ref, x_ref, g_ref, nw_ref, sc_ref, sh_ref, w_ref, x1_ref, y_ref, h_ref, gu_ref, act_ref):
        y = _nt(a_ref[...], wm_ref[...]) if w_is_transposed else _nn(a_ref[...], wm_ref[...])
        y_ref[...] = y.astype(BF16)
        x1 = x_ref[...] + g_ref[...] * y
        x1_ref[...] = x1
        _, h = _norm_mod(x1, nw_ref[...], sc_ref[...], sh_ref[...])
        hb = h.astype(BF16)
        h_ref[...] = hb
        gate = _nt(hb, w_ref[pl.ds(0, D_FF), :])
        up = _nt(hb, w_ref[pl.ds(D_FF, D_FF), :])
        sig = jax.nn.sigmoid(gate)
        silu = gate * sig
        gu_ref[:, pl.ds(0, D_FF)] = (up * (sig * (1.0 + gate * (1.0 - sig)))).astype(BF16)
        gu_ref[:, pl.ds(D_FF, D_FF)] = silu.astype(BF16)
        act_ref[...] = (silu * up).astype(BF16)

    row = lambda w_: pl.BlockSpec((tm, w_), lambda i: (i, 0))
    vec = _const_spec((1, D))
    return pl.pallas_call(
        body, name=name, grid=(S // tm,),
        out_shape=(jax.ShapeDtypeStruct((S, D), F32), jax.ShapeDtypeStruct((S, D), BF16), jax.ShapeDtypeStruct((S, D), BF16),
                   jax.ShapeDtypeStruct((S, 2 * D_FF), BF16), jax.ShapeDtypeStruct((S, D_FF), BF16)),
        in_specs=[row(K), _const_spec(w_mix.shape), row(D), vec, vec, vec, vec, _const_spec(wt.shape)],
        out_specs=(row(D), row(D), row(D), row(2 * D_FF), row(D_FF)),
        compiler_params=_params(("parallel",), VMEM_LIMIT),
    )(a, w_mix, x, g, nw, sc, sh, wt)


def _alibi_bias(slopes, half, dil, chunk=CHUNK, both=False):
    tq = _tile_rows(dil, chunk)
    tk = tq + 2 * half
    rel = np.arange(tk)[:, None] - half - np.arange(tq)[None, :]
    band = np.abs(rel) <= half
    dist = (dil * np.abs(rel)).astype(np.float32)
    tabs = [np.where(band, -np.float32(s) * dist, np.float32(NEG)).astype(np.float32) for s in slopes]
    out = []
    for u in range(0, len(tabs), 8):
        tab = np.concatenate(tabs[u:u + 8], axis=1)
        first, last = tab.copy(), tab.copy()
        first[:half] = NEG
        last[tk - half:] = NEG
        out += [tab, first, last]
        if both:
            last = last.copy()
            last[:half] = NEG
            out.append(last)
    return jnp.asarray(np.concatenate(out, axis=0))


def _slopes(n):
    return (2.0 ** (-8.0 * np.arange(1, n + 1) / n)).astype(np.float32)


def _head_masks(tq):
    lane = lax.broadcasted_iota(jnp.int32, (tq, LANES), 1)
    lo = lane < HEAD_DIM
    return lo, jnp.logical_not(lo)


def _stack_heads(tiles, lo, hi, scale):
    blocks = []
    for t in range(4):
        xf = tiles[t] if scale == 1.0 else tiles[t] * scale
        for a in range(2):
            xm = jnp.where(lo if a == 0 else hi, xf, 0.0)
            if a != t // 2:
                xm = pltpu.roll(xm, HEAD_DIM, 1)
            blocks.append(xm.astype(BF16))
    return jnp.concatenate(blocks, axis=0)


def _tile_from_columns(x8t, t, tq):
    r0 = HEAD_DIM * (t // 2)
    top = x8t[r0:r0 + HEAD_DIM, 2 * t * tq:(2 * t + 1) * tq]
    bot = x8t[r0:r0 + HEAD_DIM, (2 * t + 1) * tq:(2 * t + 2) * tq]
    return jnp.concatenate([top, bot], axis=0).T


def _attn_layout(S, C, r, half, qoff, koff, voff, chunk):
    hb = half * r
    per = chunk // hb
    nhb = S // hb
    main = lambda off: pl.BlockSpec((chunk, LANES), lambda u, i: (i, off // LANES + u))
    prev = lambda off: pl.BlockSpec((hb, LANES), lambda u, i: (jnp.maximum(i * per - 1, 0), off // LANES + u))
    nxt = lambda off: pl.BlockSpec((hb, LANES), lambda u, i: (jnp.minimum((i + 1) * per, nhb - 1), off // LANES + u))
    specs = [pl.BlockSpec((chunk, 4 * LANES), lambda u, i: (i, qoff // (4 * LANES) + u))]
    specs += [prev(koff), main(koff), nxt(koff), prev(voff), main(voff), nxt(voff)]
    return specs, hb


def _stage(dst, srcs):
    row = 0
    for src in srcs:
        n = src.shape[0]
        dst[pl.ds(row, n), :] = src[...].astype(F32)
        row += n


def _rows(start, n, r):
    return pl.ds(start, n, stride=r) if r > 1 else pl.ds(start, n)


def _attn_fwd(qkv, bias, sink, *, C, r, half, qoff, koff, voff, n_units, out_dtype, name):
    S = qkv.shape[0]
    chunk = max(CHUNK, TQ * r)
    tq = _tile_rows(r, chunk)
    tk = tq + 2 * half
    tiles = chunk // (r * tq)
    nsteps = S // chunk
    specs, hb = _attn_layout(S, C, r, half, qoff, koff, voff, chunk)
    use_sink = sink is not None

    def body(*refs):
        q_ref, kp, km, kn, vp, vm, vn, bias_ref = refs[:8]
        rest = list(refs[8:])
        sink_ref = rest.pop(0) if use_sink else None
        o_ref, lse_ref, qs, ks, vs, os_, ls = rest
        i = pl.program_id(1)
        if r > 1:
            for t in range(4):
                qs[t] = q_ref[:, pl.ds(t * LANES, LANES)].astype(F32)
        _stage(ks, [kp, km, kn])
        _stage(vs, [vp, vm, vn])
        lo, hi = _head_masks(tq)

        def tile_in(staged, ref, t, start):
            if r > 1:
                return staged[t, _rows(start, tq, r), :]
            return ref[pl.ds(start, tq), pl.ds(t * LANES, LANES)].astype(F32)

        ones = jnp.ones((16, tk), BF16)
        if use_sink:
            sk = sink_ref[pl.ds(0, 1), :]

        def chain(n, carry):
            rho, c = n // tiles, n % tiles
            start = c * (tq * r) + rho
            if r == 1:
                start = pl.multiple_of(start, tq)
            variant = jnp.where(jnp.logical_and(i == 0, c == 0), 1, 0) + jnp.where(
                jnp.logical_and(i == nsteps - 1, c == tiles - 1), 2, 0)
            k2 = ks[_rows(start, tk, r), :].astype(BF16)
            v2t = jnp.concatenate([vs[_rows(start, tk, r), :].T.astype(BF16), ones], axis=0)
            q8 = _stack_heads([tile_in(qs, q_ref, t, start) for t in range(4)], lo, hi, HEAD_DIM ** -0.5)
            s = _nt(k2, q8) + bias_ref[pl.ds(pl.multiple_of(variant * tk, 8), tk), :]
            m = jnp.max(s, axis=0, keepdims=True)
            if use_sink:
                m = jnp.maximum(m, sk)
            pv = _nn(v2t, jnp.exp(s - m).astype(BF16))
            l = pv[LANES:LANES + 1]
            if use_sink:
                l = l + jnp.exp(sk - m)
            o8t = pv[:LANES] / l
            lse8 = jnp.broadcast_to(m + jnp.log(l), (LANES, 8 * tq))
            for t in range(4):
                if r > 1:
                    os_[t, _rows(start, tq, r), :] = _tile_from_columns(o8t, t, tq)
                    ls[t, _rows(start, tq, r), :] = _tile_from_columns(lse8, t, tq)
                else:
                    o_ref[pl.ds(start, tq), pl.ds(t * LANES, LANES)] = _tile_from_columns(o8t, t, tq).astype(out_dtype)
                    lse_ref[pl.ds(start, tq), pl.ds(t * LANES, LANES)] = _tile_from_columns(lse8, t, tq)
            return carry

        lax.fori_loop(0, r * tiles, chain, 0, unroll=4)
        if r > 1:
            for t in range(4):
                o_ref[:, pl.ds(t * LANES, LANES)] = os_[t].astype(out_dtype)
                lse_ref[:, pl.ds(t * LANES, LANES)] = ls[t]

    in_specs = specs + [pl.BlockSpec((bias.shape[0] // n_units, 8 * tq), lambda u, i: (u, 0))]
    args = [qkv] * 7 + [bias]
    if use_sink:
        in_specs.append(pl.BlockSpec((8, 8 * tq), lambda u, i: (u, 0)))
        args.append(sink)
    wide = pl.BlockSpec((chunk, 4 * LANES), lambda u, i: (i, u))
    win = hb + chunk + hb
    big = lambda: pltpu.VMEM((4, chunk if r > 1 else 8, LANES), F32)
    return pl.pallas_call(
        body, name=name, grid=(n_units, nsteps),
        out_shape=(jax.ShapeDtypeStruct((S, n_units * 512), out_dtype), jax.ShapeDtypeStruct((S, n_units * 512), F32)),
        in_specs=in_specs, out_specs=(wide, wide),
        scratch_shapes=[big(), pltpu.VMEM((win, LANES), F32), pltpu.VMEM((win, LANES), F32), big(), big()],
        compiler_params=_params(("parallel", "parallel"), VMEM_LIMIT),
    )(*args)


def _attn_bwd(qkv, bias, sink, o, do, lse, *, C, r, half, qoff, koff, voff, n_units, name):
    S = qkv.shape[0]
    tq = _tile_rows(r)
    tk = tq + 2 * half
    tiles = CHUNK // (r * tq)
    nsteps = S // CHUNK
    specs, hb = _attn_layout(S, C, r, half, qoff, koff, voff, CHUNK)
    use_sink = sink is not None

    def body(*refs):
        q_ref, kp, km, kn, vp, vm, vn, bias_ref = refs[:8]
        rest = list(refs[8:])
        sink_ref = rest.pop(0) if use_sink else None
        o_ref, do_ref, lse_ref, dq_ref, dk_hbm, dv_hbm = rest[:6]
        rest = rest[6:]
        dsink_ref = rest.pop(0) if use_sink else None
        qs, ks, vs, os_, dos, ls, dqs, acck, accv, sem = rest
        u, i = pl.program_id(0), pl.program_id(1)

        @pl.when(i == 0)
        def _():
            acck[...] = jnp.zeros_like(acck)
            accv[...] = jnp.zeros_like(accv)
            if use_sink:
                dsink_ref[...] = jnp.zeros_like(dsink_ref)

        if r > 1:
            for t in range(4):
                cols = pl.ds(t * LANES, LANES)
                qs[t] = q_ref[:, cols].astype(F32)
                os_[t] = o_ref[:, cols].astype(F32)
                dos[t] = do_ref[:, cols].astype(F32)
                ls[t] = lse_ref[:, cols]
        _stage(ks, [kp, km, kn])
        _stage(vs, [vp, vm, vn])
        lo, hi = _head_masks(tq)

        def tile_in(staged, ref, t, start):
            if r > 1:
                return staged[t, _rows(start, tq, r), :]
            return ref[pl.ds(start, tq), pl.ds(t * LANES, LANES)].astype(F32)

        base = pl.multiple_of(i * CHUNK, CHUNK)
        if use_sink:
            sk = sink_ref[pl.ds(0, 1), :]

        def chain(n, carry):
            rho, c = n // tiles, n % tiles
            start = c * (tq * r) + rho
            if r == 1:
                start = pl.multiple_of(start, tq)
            variant = jnp.where(jnp.logical_and(i == 0, c == 0), 1, 0) + jnp.where(
                jnp.logical_and(i == nsteps - 1, c == tiles - 1), 2, 0)
            k2 = ks[_rows(start, tk, r), :].astype(BF16)
            v2 = vs[_rows(start, tk, r), :].astype(BF16)
            k2t = ks[_rows(start, tk, r), :].T.astype(BF16)
            q8 = _stack_heads([tile_in(qs, q_ref, t, start) for t in range(4)], lo, hi, HEAD_DIM ** -0.5)
            do_tiles = [tile_in(dos, do_ref, t, start) for t in range(4)]
            do8 = _stack_heads(do_tiles, lo, hi, 1.0)
            deltas, lses = [], []
            for t in range(4):
                prod_t = (do_tiles[t] * tile_in(os_, o_ref, t, start)).T
                lse_t = tile_in(ls, lse_ref, t, start).T
                for a in range(2):
                    deltas.append(jnp.sum(prod_t[a * HEAD_DIM:(a + 1) * HEAD_DIM], axis=0, keepdims=True))
                    lses.append(lse_t[a * HEAD_DIM:a * HEAD_DIM + 1])
            delta8 = jnp.concatenate(deltas, axis=1)
            lse8 = jnp.concatenate(lses, axis=1)
            s = _nt(k2, q8) + bias_ref[pl.ds(pl.multiple_of(variant * tk, 8), tk), :]
            p = jnp.exp(s - lse8)
            dp = _nt(v2, do8)
            dsb = (p * (dp - delta8)).astype(BF16)
            dq8t = _nn(k2t, dsb)
            for t in range(4):
                dq_t = _tile_from_columns(dq8t, t, tq) * (HEAD_DIM ** -0.5)
                if r > 1:
                    dqs[t, _rows(start, tq, r), :] = dq_t
                else:
                    dq_ref[pl.ds(start, tq), pl.ds(t * LANES, LANES)] = dq_t.astype(BF16)
            arow = base + start
            if r == 1:
                arow = pl.multiple_of(arow, tq)
            acck[_rows(arow, tk, r), :] = acck[_rows(arow, tk, r), :] + _nn(dsb, q8)
            accv[_rows(arow, tk, r), :] = accv[_rows(arow, tk, r), :] + _nn(p.astype(BF16), do8)
            if use_sink:
                e = jnp.exp(sk - lse8) * delta8
                for h in range(8):
                    part = -jnp.sum(e[:, h * tq:(h + 1) * tq], axis=1, keepdims=True)
                    dsink_ref[pl.ds(h, 1), :] = dsink_ref[pl.ds(h, 1), :] + part
            return carry

        lax.fori_loop(0, r * tiles, chain, 0, unroll=2)
        if r > 1:
            for t in range(4):
                dq_ref[:, pl.ds(t * LANES, LANES)] = dqs[t].astype(BF16)

        @pl.when(i == nsteps - 1)
        def _():
            ck = pltpu.make_async_copy(acck.at[pl.ds(hb, S)], dk_hbm.at[u], sem.at[0])
            cv = pltpu.make_async_copy(accv.at[pl.ds(hb, S)], dv_hbm.at[u], sem.at[1])
            ck.start()
            cv.start()
            ck.wait()
            cv.wait()

    wide = pl.BlockSpec((CHUNK, 4 * LANES), lambda u, i: (i, u))
    hbm = pl.BlockSpec(memory_space=pl.ANY)
    in_specs = specs + [pl.BlockSpec((3 * tk, 8 * tq), lambda u, i: (u, 0))]
    args = [qkv] * 7 + [bias]
    if use_sink:
        in_specs.append(pl.BlockSpec((8, 8 * tq), lambda u, i: (u, 0)))
        args.append(sink)
    in_specs += [wide, wide, wide]
    args += [o, do, lse]
    out_shape = [jax.ShapeDtypeStruct((S, n_units * 512), BF16), jax.ShapeDtypeStruct((n_units, S, LANES), F32),
                 jax.ShapeDtypeStruct((n_units, S, LANES), F32)]
    out_specs = [wide, hbm, hbm]
    if use_sink:
        out_shape.append(jax.ShapeDtypeStruct((n_units * 8, LANES), F32))
        out_specs.append(pl.BlockSpec((8, LANES), lambda u, i: (u, 0)))
    win = hb + CHUNK + hb
    big = lambda: pltpu.VMEM((4, CHUNK if r > 1 else 8, LANES), F32)
    res = pl.pallas_call(
        body, name=name, grid=(n_units, nsteps), out_shape=tuple(out_shape), in_specs=in_specs, out_specs=tuple(out_specs),
        scratch_shapes=[big(), pltpu.VMEM((win, LANES), F32), pltpu.VMEM((win, LANES), F32), big(), big(), big(), big(),
                        pltpu.VMEM((S + 2 * hb, LANES), F32), pltpu.VMEM((S + 2 * hb, LANES), F32), pltpu.SemaphoreType.DMA((2,))],
        compiler_params=_params(("arbitrary", "arbitrary"), VMEM_LIMIT),
    )(*args)
    return res[0], res[1], res[2], (res[3] if use_sink else None)


def _merge_groups(os_, lses):
    S, W = os_[0].shape
    tm = 512

    def body(o0, o1, o2, l0, l1, l2, o_ref, lse_ref):
        ls = [l0[...], l1[...], l2[...]]
        mx = jnp.maximum(jnp.maximum(ls[0], ls[1]), ls[2])
        es = [jnp.exp(l - mx) for l in ls]
        den = es[0] + es[1] + es[2]
        o = (es[0] / den) * o0[...] + (es[1] / den) * o1[...] + (es[2] / den) * o2[...]
        o_ref[...] = o.astype(BF16)
        lse_ref[...] = mx + jnp.log(den)

    row = pl.BlockSpec((tm, W), lambda i: (i, 0))
    return pl.pallas_call(
        body, name="merge_groups", grid=(S // tm,),
        out_shape=(jax.ShapeDtypeStruct((S, W), BF16), jax.ShapeDtypeStruct((S, W), F32)),
        in_specs=[row] * 6, out_specs=(row, row), compiler_params=_params(("parallel",), VMEM_LIMIT),
    )(*os_, *lses)


def _columns(part_refs, rows):
    return jnp.concatenate([r[rows, :].astype(BF16) for r in part_refs], axis=1)


def _piece_specs(parts, rows, row_axis):
    arrays, specs = [], []
    for p in parts:
        if isinstance(p, tuple):
            arr, u = p
            index = (lambda *g, u=u: (u, g[row_axis], 0))
            arrays.append(arr)
            specs.append(pl.BlockSpec((None, rows, arr.shape[2]), index))
        else:
            arrays.append(p)
            specs.append(pl.BlockSpec((rows, p.shape[1]), lambda *g: (g[row_axis], 0)))
    return arrays, specs


def _norm_bwd(dy_parts, wt, x, dres, nw, sc, h, *, name):
    S, N = x.shape[0], wt.shape[0]
    npart = len(dy_parts)
    tm = 512
    part_arrays, part_specs = _piece_specs(dy_parts, tm, 0)

    def body(*refs):
        part_refs = refs[:npart]
        w_ref, x_ref, dres_ref, nw_ref, sc_ref, h_ref, dx_ref, st_ref, dw_ref, acc = refs[npart:]
        k = pl.program_id(0)

        @pl.when(k == 0)
        def _():
            st_ref[...] = jnp.zeros_like(st_ref)
            acc[...] = jnp.zeros_like(acc)

        dy = _columns(part_refs, slice(None))
        acc[...] += _tn(h_ref[...], dy)
        nwv, scale = nw_ref[...], 1.0 + sc_ref[...]
        sums = [jnp.zeros((1, D), F32)] * 3
        for half in range(SPLIT):
            rows = pl.ds(half * (tm // SPLIT), tm // SPLIT)
            dh = _nn(dy[half * (tm // SPLIT):(half + 1) * (tm // SPLIT)], w_ref[...])
            xv = x_ref[rows, :]
            rstd = lax.rsqrt(jnp.mean(xv * xv, axis=-1, keepdims=True) + RMS_EPS)
            xh = xv * rstd
            dxh = dh * (nwv * scale)
            dx_ref[rows, :] = dres_ref[rows, :] + rstd * (dxh - xh * jnp.mean(dxh * xh, axis=-1, keepdims=True))
            dhx = dh * xh
            sums = [sums[0] + jnp.sum(dh, axis=0, keepdims=True), sums[1] + jnp.sum(dhx * nwv, axis=0, keepdims=True),
                    sums[2] + jnp.sum(dhx * scale, axis=0, keepdims=True)]
        for q in range(3):
            st_ref[pl.ds(q, 1), :] = st_ref[pl.ds(q, 1), :] + sums[q]

        @pl.when(k == S // tm - 1)
        def _():
            dw_ref[...] = acc[...].T.astype(BF16)

    row = lambda w_: pl.BlockSpec((tm, w_), lambda i: (i, 0))
    vec = _const_spec((1, D))
    return pl.pallas_call(
        body, name=name, grid=(S // tm,),
        out_shape=(jax.ShapeDtypeStruct((S, D), F32), jax.ShapeDtypeStruct((8, D), F32), jax.ShapeDtypeStruct((N, D), BF16)),
        in_specs=part_specs + [_const_spec((N, D)), row(D), row(D), vec, vec, row(D)],
        out_specs=(row(D), _const_spec((8, D)), _const_spec((N, D))), scratch_shapes=[pltpu.VMEM((D, N), F32)],
        compiler_params=_params(("arbitrary",), VMEM_LIMIT),
    )(*part_arrays, wt, x, dres, nw, sc, h)


def _ffn_bwd(dx, y, g, w_out, gu, wt_in, x, nw, sc, y1, g1, w_mix, *, mix_is_transposed, name):
    S = dx.shape[0]
    K = w_out.shape[0]
    Km = w_mix.shape[1] if mix_is_transposed else w_mix.shape[0]
    tm = 256

    def body(dx_ref, y_ref, g_ref, wo_ref, gu_ref, wi_ref, x_ref, nw_ref, sc_ref, y1_ref, g1_ref, wm_ref,
             dgu_ref, dyb_ref, dxo_ref, da_ref, dy1_ref, stg_ref, stf_ref, stm_ref):
        @pl.when(pl.program_id(0) == 0)
        def _():
            stg_ref[...] = jnp.zeros_like(stg_ref)
            stf_ref[...] = jnp.zeros_like(stf_ref)
            stm_ref[...] = jnp.zeros_like(stm_ref)

        dxv = dx_ref[...]
        stg_ref[pl.ds(0, 1), :] = stg_ref[pl.ds(0, 1), :] + jnp.sum(dxv * y_ref[...].astype(F32), axis=0, keepdims=True)
        dyb = (dxv * g_ref[...]).astype(BF16)
        dyb_ref[...] = dyb
        da = _nt(dyb, wo_ref[...])
        dgate = (da * gu_ref[:, pl.ds(0, K)].astype(F32)).astype(BF16)
        dup = (da * gu_ref[:, pl.ds(K, K)].astype(F32)).astype(BF16)
        dgu_ref[:, pl.ds(0, K)] = dgate
        dgu_ref[:, pl.ds(K, K)] = dup
        dh = _nn(dgate, wi_ref[pl.ds(0, K), :]) + _nn(dup, wi_ref[pl.ds(K, K), :])
        xv = x_ref[...]
        rstd = lax.rsqrt(jnp.mean(xv * xv, axis=-1, keepdims=True) + RMS_EPS)
        xh = xv * rstd
        nwv, scale = nw_ref[...], 1.0 + sc_ref[...]
        dxh = dh * (nwv * scale)
        dx1 = dxv + rstd * (dxh - xh * jnp.mean(dxh * xh, axis=-1, keepdims=True))
        dxo_ref[...] = dx1
        dhx = dh * xh
        stf_ref[pl.ds(0, 1), :] = stf_ref[pl.ds(0, 1), :] + jnp.sum(dh, axis=0, keepdims=True)
        stf_ref[pl.ds(1, 1), :] = stf_ref[pl.ds(1, 1), :] + jnp.sum(dhx * nwv, axis=0, keepdims=True)
        stf_ref[pl.ds(2, 1), :] = stf_ref[pl.ds(2, 1), :] + jnp.sum(dhx * scale, axis=0, keepdims=True)
        stm_ref[pl.ds(0, 1), :] = stm_ref[pl.ds(0, 1), :] + jnp.sum(dx1 * y1_ref[...].astype(F32), axis=0, keepdims=True)
        dy1 = (dx1 * g1_ref[...]).astype(BF16)
        dy1_ref[...] = dy1
        da_ref[...] = (_nn(dy1, wm_ref[...]) if mix_is_transposed else _nt(dy1, wm_ref[...])).astype(BF16)

    row = lambda w_: pl.BlockSpec((tm, w_), lambda i: (i, 0))
    vec = _const_spec((1, D))
    st = jax.ShapeDtypeStruct((8, D), F32)
    act = lambda w_: jax.ShapeDtypeStruct((S, w_), BF16)
    return pl.pallas_call(
        body, name=name, grid=(S // tm,),
        out_shape=(act(2 * K), act(D), jax.ShapeDtypeStruct((S, D), F32), act(Km), act(D), st, st, st),
        in_specs=[row(D), row(D), vec, _const_spec(w_out.shape), row(2 * K), _const_spec(wt_in.shape), row(D), vec, vec,
                  row(D), vec, _const_spec(w_mix.shape)],
        out_specs=(row(2 * K), row(D), row(D), row(Km), row(D), _const_spec((8, D)), _const_spec((8, D)), _const_spec((8, D))),
        compiler_params=_params(("arbitrary",), VMEM_LIMIT),
    )(dx, y, g, w_out, gu, wt_in, x, nw, sc, y1, g1, w_mix)


def _weight_grad(a, b_parts, *, transpose_out, name):
    S = a.shape[0]
    N = sum(p[0].shape[2] if isinstance(p, tuple) else p.shape[1] for p in b_parts)
    npart = len(b_parts)
    nb = N // 2 if N > 4096 else N
    assert nb == N or npart == 1
    tk = 512
    if npart == 1:
        part_arrays, part_specs = list(b_parts), [pl.BlockSpec((tk, nb), lambda n, k: (k, n))]
    else:
        part_arrays, part_specs = _piece_specs(b_parts, tk, 1)

    def body(*refs):
        a_ref, part_refs = refs[0], refs[1:1 + npart]
        out_ref, acc = refs[1 + npart:]
        k = pl.program_id(1)

        @pl.when(k == 0)
        def _():
            acc[...] = jnp.zeros_like(acc)

        acc[...] += _tn(a_ref[...], _columns(part_refs, slice(None)))

        @pl.when(k == pl.num_programs(1) - 1)
        def _():
            out_ref[...] = (acc[...].T if transpose_out else acc[...]).astype(BF16)

    out_block = pl.BlockSpec((nb, D), lambda n, k: (n, 0)) if transpose_out else pl.BlockSpec((D, nb), lambda n, k: (0, n))
    return pl.pallas_call(
        body, name=name, grid=(N // nb, S // tk),
        out_shape=jax.ShapeDtypeStruct((N, D) if transpose_out else (D, N), BF16),
        in_specs=[pl.BlockSpec((tk, D), lambda n, k: (k, 0))] + part_specs,
        out_specs=out_block, scratch_shapes=[pltpu.VMEM((D, nb), F32)],
        compiler_params=_params(("parallel", "arbitrary"), VMEM_LIMIT),
    )(a, *part_arrays)


def _adamw(w, g, m, v):
    m = ADAM_B1 * m + (1.0 - ADAM_B1) * g
    v = ADAM_B2 * v + (1.0 - ADAM_B2) * (g * g)
    m_hat = m / (1.0 - ADAM_B1 ** ADAM_STEP)
    v_hat = v / (1.0 - ADAM_B2 ** ADAM_STEP)
    delta = -ADAM_LR * (m_hat / (jnp.sqrt(v_hat) + ADAM_EPS) + ADAM_WD * w)
    return delta, m, v


def _adam_segment(parts, own, w, m, v, outs, layer, *, name):
    R, C = own.shape
    tr = 32
    block = pl.BlockSpec((1, tr, C), lambda i: (layer, i, 0))

    def body(p_ref, o_ref, w_ref, m_ref, v_ref, *rest):
        g_out, d_out, m_out, v_out = rest[-4:]
        me = _my_index()
        g = jnp.zeros((tr, C), F32)
        for j in range(N_DEV):
            g = g + jnp.where(me == j, o_ref[...], p_ref[j]).astype(F32)
        delta, mn, vn = _adamw(w_ref[0], g, m_ref[0], v_ref[0])
        g_out[0] = g
        d_out[0] = delta
        m_out[0] = mn
        v_out[0] = vn

    passed = [] if outs is None else [pl.BlockSpec(memory_space=pl.ANY)] * 4
    shp = jax.ShapeDtypeStruct(w.shape, F32)
    return pl.pallas_call(
        body, name=name, grid=(R // tr,), out_shape=(shp,) * 4,
        in_specs=[pl.BlockSpec((N_DEV, tr, C), lambda i: (0, i, 0)), pl.BlockSpec((tr, C), lambda i: (i, 0)), block, block, block] + passed,
        out_specs=(block,) * 4, input_output_aliases={5 + q: q for q in range(len(passed))},
        compiler_params=_params(("parallel",), VMEM_LIMIT),
    )(parts, own, w, m, v, *(outs or ()))


def _adam_ada_w(cond_t, dmod, w, m, v):
    ncol = w.shape[-1]
    tr = 512

    def body(c_ref, d_ref, w_ref, m_ref, v_ref, g_out, d_out, m_out, v_out):
        g = _nn(c_ref[...], d_ref[0])
        delta, mn, vn = _adamw(w_ref[0], g, m_ref[0], v_ref[0])
        g_out[0] = g
        d_out[0] = delta
        m_out[0] = mn
        v_out[0] = vn

    blk = pl.BlockSpec((1, tr, ncol), lambda l, i: (l, i, 0))
    shp = jax.ShapeDtypeStruct(w.shape, F32)
    return pl.pallas_call(
        body, name="adam_ada_w", grid=(DEPTH, D // tr), out_shape=(shp,) * 4,
        in_specs=[pl.BlockSpec((tr, LANES), lambda l, i: (i, 0)), pl.BlockSpec((1, LANES, ncol), lambda l, i: (l, 0, 0)), blk, blk, blk],
        out_specs=(blk,) * 4, compiler_params=_params(("parallel", "parallel"), VMEM_LIMIT),
    )(cond_t, dmod, w, m, v)


TILE_ROWS = 168


def _stat_sources():
    pairs = []
    for i in range(DEPTH):
        b = 32 * i
        for q, src in enumerate((b, b + 1, b + 8, b + 16, b + 17, b + 24)):
            pairs.append((6 * i + q, src))
        pairs.append((24 + i, b + 2))
        pairs.append((32 + i, b + 18))
    pairs += [(40, 128), (41, 129)]
    return pairs


def _small_exchange(tiles, w, m, v):
    loss_row, sink_row, sink_src = 41, 48, 136

    def body(s_ref, w_ref, m_ref, v_ref, dmod_out, g_out, d_out, m_out, v_out, loss_out, all_ref, tot_ref, send_sems, recv_sems):
        me = _my_index()
        all_ref[me] = s_ref[...]
        copies = []
        for k in range(1, N_DEV):
            dev, _ = _peer(k)
            cp = pltpu.make_async_remote_copy(src_ref=s_ref, dst_ref=all_ref.at[me], send_sem=send_sems.at[k - 1],
                                              recv_sem=recv_sems.at[k - 1], device_id=dev, device_id_type=MESH)
            cp.start()
            copies.append(cp)
        for k in range(1, N_DEV):
            dev, pidx = _peer(k)
            pltpu.make_async_remote_copy(src_ref=s_ref, dst_ref=all_ref.at[pidx], send_sem=send_sems.at[k - 1],
                                         recv_sem=recv_sems.at[k - 1], device_id=dev, device_id_type=MESH).wait_recv()
        for cp in copies:
            cp.wait_send()
        tot = all_ref[0]
        for j in range(1, N_DEV):
            tot = tot + all_ref[j]
        tot_ref[...] = tot
        g_out[...] = jnp.zeros_like(g_out)
        for dst, src in _stat_sources():
            g_out[pl.ds(dst, 1), :] = tot_ref[pl.ds(src, 1), :]
            if dst < 24:
                for j in range(N_DEV):
                    dmod_out[j, pl.ds(dst, 1), :] = all_ref[j, pl.ds(src, 1), :]
        lane = lax.broadcasted_iota(jnp.int32, (1, D), 1)
        sink = jnp.zeros((1, D), F32)
        for h in range(32):
            sink = jnp.where(lane == h, tot_ref[pl.ds(sink_src + h, 1), :], sink)
        g_out[pl.ds(sink_row, 1), :] = sink
        g = g_out[...]
        delta, mn, vn = _adamw(w_ref[...], g, m_ref[...], v_ref[...])
        d_out[...] = delta
        m_out[...] = mn
        v_out[...] = vn
        loss = jnp.sum(g[loss_row:loss_row + 1, :], axis=-1, keepdims=True) * (0.5 / D)
        loss_out[...] = jnp.broadcast_to(loss, loss_out.shape)

    vm = pl.BlockSpec(memory_space=pltpu.VMEM)
    shp = jax.ShapeDtypeStruct((STAT_ROWS, D), F32)
    return pl.pallas_call(
        body, name="small_exchange",
        out_shape=(jax.ShapeDtypeStruct((N_DEV, 24, D), F32), shp, shp, shp, shp, jax.ShapeDtypeStruct((8, LANES), F32)),
        in_specs=[vm] * 4, out_specs=(vm,) * 6,
        scratch_shapes=[pltpu.VMEM((N_DEV, TILE_ROWS, D), F32), pltpu.VMEM((TILE_ROWS, D), F32),
                        pltpu.SemaphoreType.DMA((N_DEV - 1,)), pltpu.SemaphoreType.DMA((N_DEV - 1,))],
        compiler_params=_params(vmem=VMEM_LIMIT),
    )(tiles, w, m, v)


def _to_rows(name, a):
    if name in ("ffn_in", "a_in", "b_in"):
        return a.T
    if name == "b_out":
        return a.T.reshape(-1, D)
    return a


def _rows8(a):
    return jnp.pad(a, ((0, 8 - a.shape[0]), (0, 0)))


def _pack_small(ada_b, norm_mix, norm_ffn, final_norm, sink):
    sink_row = jnp.pad(sink.reshape(1, -1), ((0, 0), (0, D - sink.size)))
    return jnp.concatenate([ada_b.reshape(24, D), _rows8(norm_mix), _rows8(norm_ffn), _rows8(final_norm.reshape(1, D)),
                            _rows8(sink_row)], axis=0)


def _unpack_small(a):
    return a[0:24].reshape(4, 6 * D), a[24:28], a[32:36], a[40], a[48, :32].reshape(2, 16)


def kernel(x, c, ada_w, ada_b, norm_mix, norm_ffn, ffn_w_in, ffn_w_out, a_w_in, a_w_out, a_sink, b_w_in, b_w_out, final_norm, loss_target, m_ada_w, m_ada_b, m_norm_mix, m_norm_ffn, m_ffn_w_in, m_ffn_w_out, m_a_w_in, m_a_w_out, m_a_sink, m_b_w_in, m_b_w_out, m_final_norm, v_ada_w, v_ada_b, v_norm_mix, v_norm_ffn, v_ffn_w_in, v_ffn_w_out, v_a_w_in, v_a_w_out, v_a_sink, v_b_w_in, v_b_w_out, v_final_norm):
    S = x.shape[1]
    x0 = x.reshape(S, D)
    target = loss_target.reshape(S, D)
    me = _my_index()
    ncol = ada_w.shape[-1]

    ada_b_mine = lax.dynamic_slice_in_dim(ada_b, me * ncol, ncol, axis=1)
    cond_all, parts = _cond_exchange(jnp.broadcast_to(c.reshape(1, D), (8, D)), ada_w, ada_b_mine)
    mod = lax.dynamic_index_in_dim(parts, me, axis=2, keepdims=False)
    mod = jnp.transpose(mod, (1, 0, 2)).reshape(DEPTH, 6, 1, D)

    weights = {"ffn_in": ffn_w_in, "ffn_out": ffn_w_out, "a_in": a_w_in, "a_out": a_w_out, "b_in": b_w_in, "b_out": b_w_out}
    shard = {(n, l): _to_rows(n, weights[n][l]).astype(BF16) for n, l, _ in SEGMENTS}
    first = [sg for sg in _layer_segments(0) if not sg[0].startswith("ffn")]
    gathered0 = _all_gather_weights([shard[(n, l)] for n, l, _ in first])
    W = {(n, l): g for (n, l, _), g in zip(first, gathered0)}
    groups = [[sg for sg in _layer_segments(0) if sg[0].startswith("ffn")]] + [_layer_segments(i) for i in range(1, DEPTH)]
    gathers, order = [], gathered0[0]
    for q, segs in enumerate(groups):
        zones = [(N_DEV, rows, D) for _, _, rows in segs]
        gathers.append(_exchange_start([shard[(n, l)] for n, l, _ in segs], zones, [(s, 0) for s in range(len(segs))],
                                       [sg[2] for sg in segs], False, order, "weight_gather_start_%d" % q))
        order = gathers[-1][-1]
    gather_token = order[0:1, 0:1]

    def finish_gather(q, after):
        segs = groups[q]
        zones = _exchange_wait(gathers[q], len(segs), [(s, 0) for s in range(len(segs))], [sg[2] for sg in segs], True, after,
                               "weight_gather_wait_%d" % q)
        for (n, l, rows), zone in zip(segs, zones):
            W[(n, l)] = zone.reshape(D, 512) if n == "b_out" else zone.reshape(N_DEV * rows, D)

    a_slopes, b_slopes = _slopes(16), _slopes(24)
    bias_a = _alibi_bias(a_slopes, A_HALF, 1)
    bias_b = [_alibi_bias(b_slopes[8 * g:8 * g + 8], B_HALF, dil) for g, dil in enumerate(B_DILS)]
    bias_b_fwd = [_alibi_bias(b_slopes[8 * g:8 * g + 8], B_HALF, dil, max(CHUNK, TQ * dil), both=True) for g, dil in enumerate(B_DILS)]
    a_geom = dict(C=A_QKV, r=1, half=A_HALF, qoff=0, koff=1024, voff=1280, n_units=2)
    b_geom = [dict(C=B_QKV, r=dil, half=B_HALF, qoff=512 * g, koff=1536 + 128 * g, voff=1920 + 128 * g, n_units=1)
              for g, dil in enumerate(B_DILS)]

    saved = []
    xcur = x0
    for i in range(DEPTH):
        j = i // 2
        sh1, sc1, g1, sh2, sc2, g2 = [mod[i, q] for q in range(6)]
        nm, nf = norm_mix[i].reshape(1, D), norm_ffn[i].reshape(1, D)
        if i == 0:
            nm = nm + gather_token
        if i > 0:
            finish_gather(i, xcur)
        if i % 2 == 0:
            sink_rep = jnp.repeat(jnp.repeat(a_sink[j], TQ).reshape(2, 1, 8 * TQ), 8, axis=1).reshape(16, 8 * TQ)
            h1, qkv = _proj(xcur, nm, sc1, sh1, W[("a_in", j)], name="proj_a")
            o, lse = _attn_fwd(qkv, bias_a, sink_rep, out_dtype=BF16, name="attn_a_fwd", **a_geom)
            if i == 0:
                finish_gather(0, o)
            x1, y1, h2, gu, act = _out_ffn_in(o, W[("a_out", j)], xcur, g1, nf, sc2, sh2, W[("ffn_in", i)],
                                              w_is_transposed=False, name="out_a_ffn_in")
        else:
            sink_rep = None
            h1, qkv = _proj(xcur, nm, sc1, sh1, W[("b_in", j)], name="proj_b")
            outs = [_attn_fwd(qkv, bias_b_fwd[g], None, out_dtype=F32, name="attn_b%d_fwd" % g, **b_geom[g]) for g in range(3)]
            o, lse = _merge_groups([t[0] for t in outs], [t[1] for t in outs])
            x1, y1, h2, gu, act = _out_ffn_in(o, W[("b_out", j)], xcur, g1, nf, sc2, sh2, W[("ffn_in", i)],
                                              w_is_transposed=True, name="out_b_ffn_in")
        if i < DEPTH - 1:
            x2, y2 = _ffn_out(act, W[("ffn_out", i)], x1, g2, None, None, name="ffn_out")
        else:
            x2, y2, head_stats = _ffn_out(act, W[("ffn_out", i)], x1, g2, target, final_norm.reshape(1, D), name="ffn_out_loss")
        saved.append(dict(x0=xcur, h1=h1, qkv=qkv, o=o, lse=lse, y1=y1, x1=x1, h2=h2, gu=gu, act=act, y2=y2, sink=sink_rep))
        xcur = x2

    dx = xcur

    dW = {}
    stat_tiles, dsink = [None] * DEPTH, [None] * 2
    exchanges = []
    start_token = None

    def start_exchange(segs):
        own = [lax.dynamic_slice_in_dim(dW[(n, l)], me * rows, rows, axis=0) for n, l, rows in segs]
        zones = [(N_DEV, rows, D) for _, _, rows in segs]
        started = _exchange_start([dW[(n, l)] for n, l, _ in segs], zones, [(s, 0) for s in range(len(segs))],
                                  [sg[2] for sg in segs], True, own[0], "grad_exchange_start_%d" % len(exchanges))
        exchanges.append((segs, started, own))
        return started[-1][0:1, 0:1]

    for i in reversed(range(DEPTH)):
        j = i // 2
        sv = saved[i]
        sh1, sc1, g1, sh2, sc2, g2 = [mod[i, q] for q in range(6)]
        if start_token is not None:
            g2 = g2 + start_token
            start_token = None
        nm, nf = norm_mix[i].reshape(1, D), norm_ffn[i].reshape(1, D)
        mix = "a_out" if i % 2 == 0 else "b_out"
        dgu, dy2, dx1, do, dy1, st_g2, st_f, st_g1 = _ffn_bwd(
            dx, sv["y2"], g2, W[("ffn_out", i)], sv["gu"], W[("ffn_in", i)], sv["x1"], nf, sc2, sv["y1"], g1, W[(mix, j)],
            mix_is_transposed=(i % 2 == 1), name="ffn_bwd_" + mix)
        dW[("ffn_out", i)] = _weight_grad(dy2, [sv["act"]], transpose_out=True, name="dw_ffn_out")
        dW[("ffn_in", i)] = _weight_grad(sv["h2"], [dgu], transpose_out=True, name="dw_ffn_in")
        sink_bwd = sv["sink"]
        if i == 0:
            sink_bwd = sink_bwd + start_exchange([sg for sg in _layer_segments(0) if sg[0].startswith("ffn")])
        if i % 2 == 0:
            dW[("a_out", j)] = _weight_grad(dy1, [sv["o"]], transpose_out=True, name="dw_a_out")
            dq, dk, dv, ds = _attn_bwd(sv["qkv"], bias_a, sink_bwd, sv["o"], do, sv["lse"], name="attn_a_bwd", **a_geom)
            dsink[j] = ds
            dqkv = [dq, (dk, 0), (dk, 1), (dv, 0), (dv, 1)]
            dx0, st_m, dW[("a_in", j)] = _norm_bwd(dqkv, W[("a_in", j)], sv["x0"], dx1, nm, sc1, sv["h1"], name="proj_a_bwd")
        else:
            dW[("b_out", j)] = _weight_grad(dy1, [sv["o"]], transpose_out=False, name="dw_b_out").reshape(N_DEV * 64, D)
            gr = [_attn_bwd(sv["qkv"], bias_b[g], None, sv["o"], do, sv["lse"], name="attn_b%d_bwd" % g, **b_geom[g]) for g in range(3)]
            dqkv = [t[0] for t in gr] + [(t[1], 0) for t in gr] + [(t[2], 0) for t in gr]
            dx0, st_m, dW[("b_in", j)] = _norm_bwd(dqkv, W[("b_in", j)], sv["x0"], dx1, nm, sc1, sv["h1"], name="proj_b_bwd")
        stat_tiles[i] = [st_m, st_g1, st_f, st_g2]
        if i > 0:
            start_token = start_exchange(_layer_segments(i))
        else:
            start_exchange([sg for sg in _layer_segments(0) if not sg[0].startswith("ffn")])
        dx = dx0
    grad_x = dx.reshape(1, S, D)

    masters = {"ffn_in": (ffn_w_in, m_ffn_w_in, v_ffn_w_in), "ffn_out": (ffn_w_out, m_ffn_w_out, v_ffn_w_out),
               "a_in": (a_w_in, m_a_w_in, v_a_w_in), "a_out": (a_w_out, m_a_w_out, v_a_w_out),
               "b_in": (b_w_in, m_b_w_in, v_b_w_in), "b_out": (b_w_out, m_b_w_out, v_b_w_out)}
    by_rows = ("ffn_in", "a_in", "b_in", "b_out")
    masters = {n: tuple(jnp.swapaxes(t, 1, 2) for t in wmv) if n in by_rows else wmv for n, wmv in masters.items()}
    results = {n: None for n in masters}
    after = dx
    for e, (segs, started, own) in enumerate(exchanges):
        zones = _exchange_wait(started, len(segs), [(s, 0) for s in range(len(segs))], [sg[2] for sg in segs], False, after,
                               "grad_exchange_wait_%d" % e)
        for (n, l, rows), zone, mine in zip(segs, zones, own):
            if n == "b_out":
                zone, mine = zone.reshape(N_DEV, LANES, 512), mine.reshape(LANES, 512)
            results[n] = _adam_segment(zone, mine, *masters[n], results[n], l, name="adam_" + n)
            after = results[n][0]
    big = {(kind, n): jnp.swapaxes(results[n][q], 1, 2) if n in by_rows else results[n][q]
           for q, kind in enumerate(("grad", "delta", "m", "v")) for n in masters}

    tiles = jnp.concatenate([t for i in range(DEPTH) for t in stat_tiles[i]] + [head_stats]
                            + [jnp.pad(ds, ((0, 0), (0, D - LANES))) for ds in dsink], axis=0)
    small = [_pack_small(*t) for t in ((ada_b, norm_mix, norm_ffn, final_norm, a_sink),
                                       (m_ada_b, m_norm_mix, m_norm_ffn, m_final_norm, m_a_sink),
                                       (v_ada_b, v_norm_mix, v_norm_ffn, v_final_norm, v_a_sink))]
    dmod_all, sg, sd, sm, sv_, loss_tile = _small_exchange(tiles, *small)
    loss = loss_tile[0, 0]
    dmod_all = dmod_all.reshape(N_DEV, DEPTH, 6 * D)
    dmod_mine = lax.dynamic_slice_in_dim(dmod_all, me * ncol, ncol, axis=2)
    dmod_pad = jnp.pad(jnp.transpose(dmod_mine, (1, 0, 2)), ((0, 0), (0, LANES - N_DEV), (0, 0))).astype(BF16)
    cond_t = jnp.pad(cond_all.T, ((0, 0), (0, LANES - N_DEV))).astype(BF16)
    ada = _adam_ada_w(cond_t, dmod_pad, ada_w, m_ada_w, v_ada_w)

    outs = [loss, grad_x]
    small_res = [_unpack_small(t) for t in (sg, sd, sm, sv_)]
    for q, kind in enumerate(("grad", "delta", "m", "v")):
        ab, nm_, nf_, fn, sk = small_res[q]
        outs += [ada[q], ab, nm_, nf_, big[(kind, "ffn_in")], big[(kind, "ffn_out")], big[(kind, "a_in")], big[(kind, "a_out")],
                 sk, big[(kind, "b_in")], big[(kind, "b_out")], fn]
    return tuple(outs)
```

```python
import math

import numpy as np
import jax
import jax.numpy as jnp
from jax import lax
from jax.experimental import pallas as pl
from jax.experimental.pallas import tpu as pltpu

D = 1024
HEAD_DIM = 64
D_FF = 2816
DEPTH = 4
N_DEV = 8
A_QKV = 1536
B_QKV = 2304
A_HALF = 128
B_HALF = 64
B_DILS = (1, 4, 16)
RMS_EPS = 1e-6
NEG = -1e30
ADAM_LR = 0.001
ADAM_B1 = 0.9
ADAM_B2 = 0.999
ADAM_EPS = 1e-08
ADAM_WD = 0.01
ADAM_STEP = 10

LANES = 128
SPLIT = 2
TQ = 128
VMEM_LIMIT = 56 * 1024 * 1024
MESH = pl.DeviceIdType.MESH
F32 = jnp.float32
BF16 = jnp.bfloat16

SEGMENTS = ([("ffn_in", l, 704) for l in range(4)] + [("ffn_out", l, 352) for l in range(4)]
            + [("a_in", j, 192) for j in range(2)] + [("a_out", j, 128) for j in range(2)]
            + [("b_in", j, 288) for j in range(2)] + [("b_out", j, 64) for j in range(2)])
def _layer_segments(i):
    mixer = "a" if i % 2 == 0 else "b"
    return [s for s in SEGMENTS if (s[0].startswith("ffn") and s[1] == i) or (s[0].startswith(mixer + "_") and s[1] == i // 2)]


def _offsets(segs):
    rows = [s[2] for s in segs]
    return [sum(rows[:k]) for k in range(len(rows))], sum(rows)
STAT_ROWS = 56


def _nn(a, b):
    return jnp.dot(a, b, preferred_element_type=F32)


def _nt(a, b):
    return lax.dot_general(a, b, (((1,), (1,)), ((), ())), preferred_element_type=F32)


def _tn(a, b):
    return lax.dot_general(a, b, (((0,), (0,)), ((), ())), preferred_element_type=F32)


def _params(dims=None, vmem=None):
    kw = {}
    if dims is not None:
        kw["dimension_semantics"] = dims
    if vmem is not None:
        kw["vmem_limit_bytes"] = vmem
    return pltpu.CompilerParams(**kw)


def _my_index():
    return 4 * lax.axis_index("x") + 2 * lax.axis_index("y") + lax.axis_index("c")


def _peer(k):
    x, y, c = lax.axis_index("x"), lax.axis_index("y"), lax.axis_index("c")
    px, py, pc = x ^ ((k >> 2) & 1), y ^ ((k >> 1) & 1), c ^ (k & 1)
    return (px, py, pc), 4 * px + 2 * py + pc


def _const_spec(shape):
    nd = len(shape)
    return pl.BlockSpec(shape, lambda *_: (0,) * nd)


def _cond_exchange(c_tile, ada_w, ada_b_mine):
    ncol = ada_w.shape[-1]

    def body(c_ref, w_ref, b_ref, cond_ref, parts_ref, call_ref, mine_ref, send_sems, recv_sems):
        me = _my_index()
        call_ref[me] = c_ref[...]
        copies = []
        for k in range(1, N_DEV):
            dev, _ = _peer(k)
            cp = pltpu.make_async_remote_copy(src_ref=c_ref, dst_ref=call_ref.at[me], send_sem=send_sems.at[0, k - 1],
                                              recv_sem=recv_sems.at[0, k - 1], device_id=dev, device_id_type=MESH)
            cp.start()
            copies.append(cp)
        for k in range(1, N_DEV):
            _, pidx = _peer(k)
            pltpu.make_async_remote_copy(src_ref=c_ref, dst_ref=call_ref.at[pidx], send_sem=send_sems.at[0, k - 1],
                                         recv_sem=recv_sems.at[0, k - 1], device_id=_peer(k)[0], device_id_type=MESH).wait_recv()
        for cp in copies:
            cp.wait_send()
        row = lax.broadcasted_iota(jnp.int32, (N_DEV, D), 0)
        cmat = jnp.zeros((N_DEV, D), F32)
        for j in range(N_DEV):
            cmat = jnp.where(row == j, call_ref[j], cmat)
        cond = cmat * jax.nn.sigmoid(cmat)
        cond_ref[...] = cond
        cb = cond.astype(BF16)
        for l in range(DEPTH):
            mine_ref[l] = _nn(cb, w_ref[l].astype(BF16)) + b_ref[pl.ds(l, 1), :]
        parts_ref[me] = mine_ref[...]
        copies = []
        for k in range(1, N_DEV):
            dev, _ = _peer(k)
            cp = pltpu.make_async_remote_copy(src_ref=mine_ref, dst_ref=parts_ref.at[me], send_sem=send_sems.at[1, k - 1],
                                              recv_sem=recv_sems.at[1, k - 1], device_id=dev, device_id_type=MESH)
            cp.start()
            copies.append(cp)
        for k in range(1, N_DEV):
            dev, pidx = _peer(k)
            pltpu.make_async_remote_copy(src_ref=mine_ref, dst_ref=parts_ref.at[pidx], send_sem=send_sems.at[1, k - 1],
                                         recv_sem=recv_sems.at[1, k - 1], device_id=dev, device_id_type=MESH).wait_recv()
        for cp in copies:
            cp.wait_send()

    vm = pl.BlockSpec(memory_space=pltpu.VMEM)
    return pl.pallas_call(
        body, name="cond_exchange",
        out_shape=(jax.ShapeDtypeStruct((N_DEV, D), F32), jax.ShapeDtypeStruct((N_DEV, DEPTH, N_DEV, ncol), F32)),
        in_specs=[vm, vm, vm], out_specs=(vm, vm),
        scratch_shapes=[pltpu.VMEM((N_DEV, N_DEV, D), F32), pltpu.VMEM((DEPTH, N_DEV, ncol), F32),
                        pltpu.SemaphoreType.DMA((2, N_DEV - 1)), pltpu.SemaphoreType.DMA((2, N_DEV - 1))],
        compiler_params=_params(vmem=VMEM_LIMIT),
    )(c_tile, ada_w, ada_b_mine)[:2]


def _all_gather_weights(shards):
    n = len(shards)
    big = max(range(n), key=lambda s: shards[s].shape[0])
    total = sum(sh.shape[0] for sh in shards)
    assert N_DEV * shards[big].shape[0] >= total

    def body(*refs):
        ins, outs = refs[:n], refs[n:2 * n]
        local_sems, send_sems, recv_sems = refs[2 * n:]
        me = _my_index()
        local = []
        for s in range(n):
            rows = ins[s].shape[0]
            cp = pltpu.make_async_copy(ins[s], outs[s].at[pl.ds(me * rows, rows)], local_sems.at[s])
            cp.start()
            local.append(cp)
        for k in range(1, N_DEV):
            dev, _ = _peer(k)
            for s in range(n):
                rows = ins[s].shape[0]
                pltpu.make_async_remote_copy(src_ref=ins[s], dst_ref=outs[s].at[pl.ds(me * rows, rows)],
                                             send_sem=send_sems.at[k - 1], recv_sem=recv_sems.at[k - 1],
                                             device_id=dev, device_id_type=MESH).start()
        whole = outs[big].at[pl.ds(0, total)]
        for k in range(1, N_DEV):
            dev, _ = _peer(k)
            w = pltpu.make_async_remote_copy(src_ref=whole, dst_ref=whole, send_sem=send_sems.at[k - 1],
                                             recv_sem=recv_sems.at[k - 1], device_id=dev, device_id_type=MESH)
            w.wait_send()
            w.wait_recv()
        for cp in local:
            cp.wait()

    hbm = pl.BlockSpec(memory_space=pl.ANY)
    return pl.pallas_call(
        body, name="weight_all_gather",
        out_shape=tuple(jax.ShapeDtypeStruct((N_DEV * s.shape[0], D), s.dtype) for s in shards),
        in_specs=[hbm] * n, out_specs=tuple([hbm] * n),
        scratch_shapes=[pltpu.SemaphoreType.DMA((n,)), pltpu.SemaphoreType.DMA((N_DEV - 1,)),
                        pltpu.SemaphoreType.DMA((N_DEV - 1,))],
    )(*shards)


HBM = pl.BlockSpec(memory_space=pltpu.HBM)
SEM = pl.BlockSpec(memory_space=pltpu.SEMAPHORE)
EFFECT = pltpu.SideEffectType.DATAFLOW_SIDE_EFFECTING


def _exchange_start(srcs, landings, dst, rows, to_peer_rows, after, name):
    n, nl = len(srcs), len(landings)

    def body(*refs):
        src_refs = refs[:n]
        send_sems, recv_sems = refs[n + 1], refs[n + 2]
        land_refs = refs[2 * n + 3:2 * n + 3 + nl]
        token = refs[-1]
        me = _my_index()
        for k in range(1, N_DEV):
            dev, pidx = _peer(k)
            for q in range(n):
                src = src_refs[q].at[pl.ds(pidx * rows[q], rows[q])] if to_peer_rows else src_refs[q]
                pltpu.make_async_remote_copy(src_ref=src, dst_ref=land_refs[dst[q][0]].at[me, pl.ds(dst[q][1], rows[q])],
                                             send_sem=send_sems.at[k * n + q], recv_sem=recv_sems.at[k * n + q],
                                             device_id=dev, device_id_type=MESH).start()
        if not to_peer_rows:
            for q in range(n):
                pltpu.make_async_copy(src_refs[q], land_refs[dst[q][0]].at[me, pl.ds(dst[q][1], rows[q])], send_sems.at[q]).start()
        token[...] = jnp.zeros_like(token)

    sems = pltpu.SemaphoreType.DMA((N_DEV * n,))
    return pl.pallas_call(
        body, name=name,
        out_shape=(sems, sems, *[pltpu.HBM(a.shape, a.dtype) for a in srcs], *[pltpu.HBM(shape, BF16) for shape in landings],
                   jax.ShapeDtypeStruct((8, LANES), F32)),
        in_specs=[HBM] * n + [pl.BlockSpec(memory_space=pl.ANY)],
        out_specs=(SEM, SEM, *[HBM] * (n + nl), pl.BlockSpec(memory_space=pltpu.VMEM)),
        input_output_aliases={q: 2 + q for q in range(n)},
        compiler_params=pltpu.CompilerParams(has_side_effects=EFFECT),
    )(*[pltpu.with_memory_space_constraint(a, pltpu.HBM) for a in srcs], after)


def _exchange_wait(started, n, dst, rows, own_slot, after, name):
    send_sems, recv_sems = started[0], started[1]
    arrays = list(started[2:-1])
    n1 = len(arrays)

    def body(*refs):
        land_refs = refs[n:n1]
        sends, recvs = refs[n1], refs[n1 + 1]
        for k in range(1, N_DEV):
            dev, _ = _peer(k)
            for q in range(n):
                slot = land_refs[dst[q][0]].at[0, pl.ds(dst[q][1], rows[q])]
                w = pltpu.make_async_remote_copy(src_ref=slot, dst_ref=slot, send_sem=sends.at[k * n + q],
                                                 recv_sem=recvs.at[k * n + q], device_id=dev, device_id_type=MESH)
                w.wait_send()
                w.wait_recv()
        if own_slot:
            for q in range(n):
                slot = land_refs[dst[q][0]].at[0, pl.ds(dst[q][1], rows[q])]
                pltpu.make_async_copy(slot, slot, sends.at[q]).wait()

    return pl.pallas_call(
        body, name=name, out_shape=tuple(pltpu.HBM(a.shape, a.dtype) for a in arrays),
        in_specs=[HBM] * n1 + [SEM, SEM, pl.BlockSpec(memory_space=pl.ANY)], out_specs=tuple([HBM] * n1),
        input_output_aliases={q: q for q in range(n1)},
        compiler_params=pltpu.CompilerParams(has_side_effects=EFFECT),
    )(*arrays, send_sems, recv_sems, after)[n:]


def _norm_mod(x, nw, sc, sh):
    ms = jnp.mean(x * x, axis=-1, keepdims=True)
    xh = x * lax.rsqrt(ms + RMS_EPS)
    return xh, (xh * nw) * (1.0 + sc) + sh


def _proj(x, nw, sc, sh, wt, *, name):
    S, N = x.shape[0], wt.shape[0]
    tm = 512

    def body(x_ref, nw_ref, sc_ref, sh_ref, w_ref, h_ref, out_ref):
        for half in range(SPLIT):
            rows = pl.ds(half * (tm // SPLIT), tm // SPLIT)
            _, h = _norm_mod(x_ref[rows, :], nw_ref[...], sc_ref[...], sh_ref[...])
            hb = h.astype(BF16)
            h_ref[rows, :] = hb
            out_ref[rows, :] = _nt(hb, w_ref[...]).astype(BF16)

    row = lambda w: pl.BlockSpec((tm, w), lambda i: (i, 0))
    vec = _const_spec((1, D))
    return pl.pallas_call(
        body, name=name, grid=(S // tm,), out_shape=(jax.ShapeDtypeStruct((S, D), BF16), jax.ShapeDtypeStruct((S, N), BF16)),
        in_specs=[row(D), vec, vec, vec, _const_spec((N, D))], out_specs=(row(D), row(N)),
        compiler_params=_params(("parallel",), VMEM_LIMIT),
    )(x, nw, sc, sh, wt)


def _ffn_out(a, w, x, g, target, fnw, *, name):
    S, K = a.shape
    tm = 512
    last = target is not None

    def body(a_ref, w_ref, x_ref, g_ref, *rest):
        y = _nn(a_ref[...], w_ref[...])
        xv = x_ref[...] + g_ref[...] * y
        if not last:
            xo_ref, y_ref = rest
            y_ref[...] = y.astype(BF16)
            xo_ref[...] = xv
            return
        t_ref, fw_ref, dx_ref, y_ref, st_ref = rest
        y_ref[...] = y.astype(BF16)

        @pl.when(pl.program_id(0) == 0)
        def _():
            st_ref[...] = jnp.zeros_like(st_ref)

        rstd = lax.rsqrt(jnp.mean(xv * xv, axis=-1, keepdims=True) + RMS_EPS)
        xh = xv * rstd
        err = xh * fw_ref[...] - t_ref[...]
        dy = err * (1.0 / D)
        dxh = dy * fw_ref[...]
        dx_ref[...] = rstd * (dxh - xh * jnp.mean(dxh * xh, axis=-1, keepdims=True))
        st_ref[pl.ds(0, 1), :] = st_ref[pl.ds(0, 1), :] + jnp.sum(dy * xh, axis=0, keepdims=True)
        st_ref[pl.ds(1, 1), :] = st_ref[pl.ds(1, 1), :] + jnp.sum(err * err, axis=0, keepdims=True)

    row = lambda w_: pl.BlockSpec((tm, w_), lambda i: (i, 0))
    in_specs = [row(K), _const_spec(w.shape), row(D), _const_spec((1, D))]
    args = [a, w, x, g]
    out_shape = [jax.ShapeDtypeStruct((S, D), F32), jax.ShapeDtypeStruct((S, D), BF16)]
    out_specs = [row(D), row(D)]
    if last:
        in_specs += [row(D), _const_spec((1, D))]
        args += [target, fnw]
        out_shape.append(jax.ShapeDtypeStruct((8, D), F32))
        out_specs.append(_const_spec((8, D)))
    return pl.pallas_call(
        body, name=name, grid=(S // tm,), out_shape=tuple(out_shape), in_specs=in_specs, out_specs=tuple(out_specs),
        compiler_params=_params(("arbitrary",) if last else ("parallel",), VMEM_LIMIT),
    )(*args)


CHUNK = 1024


def _tile_rows(r, chunk=CHUNK):
    return min(TQ, chunk // r)


def _out_ffn_in(a, w_mix, x, g, nw, sc, sh, wt, *, w_is_transposed, name):
    S, K = a.shape
    tm = 256

    def body(a_ref, wm_ref, x_ref, g_ref, nw_ref, sc_ref, sh_ref, w_ref, x1_ref, y_ref, h_ref, gu_ref, act_ref):
        y = _nt(a_ref[...], wm_ref[...]) if w_is_transposed else _nn(a_ref[...], wm_ref[...])
        y_ref[...] = y.astype(BF16)
        x1 = x_ref[...] + g_ref[...] * y
        x1_ref[...] = x1
        _, h = _norm_mod(x1, nw_ref[...], sc_ref[...], sh_ref[...])
        hb = h.astype(BF16)
        h_ref[...] = hb
        gate = _nt(hb, w_ref[pl.ds(0, D_FF), :])
        up = _nt(hb, w_ref[pl.ds(D_FF, D_FF), :])
        sig = jax.nn.sigmoid(gate)
        silu = gate * sig
        gu_ref[:, pl.ds(0, D_FF)] = (up * (sig * (1.0 + gate * (1.0 - sig)))).astype(BF16)
        gu_ref[:, pl.ds(D_FF, D_FF)] = silu.astype(BF16)
        act_ref[...] = (silu * up).astype(BF16)

    row = lambda w_: pl.BlockSpec((tm, w_), lambda i: (i, 0))
    vec = _const_spec((1, D))
    return pl.pallas_call(
        body, name=name, grid=(S // tm,),
        out_shape=(jax.ShapeDtypeStruct((S, D), F32), jax.ShapeDtypeStruct((S, D), BF16), jax.ShapeDtypeStruct((S, D), BF16),
                   jax.ShapeDtypeStruct((S, 2 * D_FF), BF16), jax.ShapeDtypeStruct((S, D_FF), BF16)),
        in_specs=[row(K), _const_spec(w_mix.shape), row(D), vec, vec, vec, vec, _const_spec(wt.shape)],
        out_specs=(row(D), row(D), row(D), row(2 * D_FF), row(D_FF)),
        compiler_params=_params(("parallel",), VMEM_LIMIT),
    )(a, w_mix, x, g, nw, sc, sh, wt)


def _alibi_bias(slopes, half, dil, chunk=CHUNK, both=False):
    tq = _tile_rows(dil, chunk)
    tk = tq + 2 * half
    rel = np.arange(tk)[:, None] - half - np.arange(tq)[None, :]
    band = np.abs(rel) <= half
    dist = (dil * np.abs(rel)).astype(np.float32)
    tabs = [np.where(band, -np.float32(s) * dist, np.float32(NEG)).astype(np.float32) for s in slopes]
    out = []
    for u in range(0, len(tabs), 8):
        tab = np.concatenate(tabs[u:u + 8], axis=1)
        first, last = tab.copy(), tab.copy()
        first[:half] = NEG
        last[tk - half:] = NEG
        out += [tab, first, last]
        if both:
            last = last.copy()
            last[:half] = NEG
            out.append(last)
    return jnp.asarray(np.concatenate(out, axis=0))


def _slopes(n):
    return (2.0 ** (-8.0 * np.arange(1, n + 1) / n)).astype(np.float32)


def _head_masks(tq):
    lane = lax.broadcasted_iota(jnp.int32, (tq, LANES), 1)
    lo = lane < HEAD_DIM
    return lo, jnp.logical_not(lo)


def _stack_heads(tiles, lo, hi, scale):
    blocks = []
    for t in range(4):
        xf = tiles[t] if scale == 1.0 else tiles[t] * scale
        for a in range(2):
            xm = jnp.where(lo if a == 0 else hi, xf, 0.0)
            if a != t // 2:
                xm = pltpu.roll(xm, HEAD_DIM, 1)
            blocks.append(xm.astype(BF16))
    return jnp.concatenate(blocks, axis=0)


def _tile_from_columns(x8t, t, tq):
    r0 = HEAD_DIM * (t // 2)
    top = x8t[r0:r0 + HEAD_DIM, 2 * t * tq:(2 * t + 1) * tq]
    bot = x8t[r0:r0 + HEAD_DIM, (2 * t + 1) * tq:(2 * t + 2) * tq]
    return jnp.concatenate([top, bot], axis=0).T


def _attn_layout(S, C, r, half, qoff, koff, voff, chunk):
    hb = half * r
    per = chunk // hb
    nhb = S // hb
    main = lambda off: pl.BlockSpec((chunk, LANES), lambda u, i: (i, off // LANES + u))
    prev = lambda off: pl.BlockSpec((hb, LANES), lambda u, i: (jnp.maximum(i * per - 1, 0), off // LANES + u))
    nxt = lambda off: pl.BlockSpec((hb, LANES), lambda u, i: (jnp.minimum((i + 1) * per, nhb - 1), off // LANES + u))
    specs = [pl.BlockSpec((chunk, 4 * LANES), lambda u, i: (i, qoff // (4 * LANES) + u))]
    specs += [prev(koff), main(koff), nxt(koff), prev(voff), main(voff), nxt(voff)]
    return specs, hb


def _stage(dst, srcs):
    row = 0
    for src in srcs:
        n = src.shape[0]
        dst[pl.ds(row, n), :] = src[...].astype(F32)
        row += n


def _rows(start, n, r):
    return pl.ds(start, n, stride=r) if r > 1 else pl.ds(start, n)


def _attn_fwd(qkv, bias, sink, *, C, r, half, qoff, koff, voff, n_units, out_dtype, name):
    S = qkv.shape[0]
    chunk = max(CHUNK, TQ * r)
    tq = _tile_rows(r, chunk)
    tk = tq + 2 * half
    tiles = chunk // (r * tq)
    nsteps = S // chunk
    specs, hb = _attn_layout(S, C, r, half, qoff, koff, voff, chunk)
    use_sink = sink is not None

    def body(*refs):
        q_ref, kp, km, kn, vp, vm, vn, bias_ref = refs[:8]
        rest = list(refs[8:])
        sink_ref = rest.pop(0) if use_sink else None
        o_ref, lse_ref, qs, ks, vs, os_, ls = rest
        i = pl.program_id(1)
        if r > 1:
            for t in range(4):
                qs[t] = q_ref[:, pl.ds(t * LANES, LANES)].astype(F32)
        _stage(ks, [kp, km, kn])
        _stage(vs, [vp, vm, vn])
        lo, hi = _head_masks(tq)

        def tile_in(staged, ref, t, start):
            if r > 1:
                return staged[t, _rows(start, tq, r), :]
            return ref[pl.ds(start, tq), pl.ds(t * LANES, LANES)].astype(F32)

        ones = jnp.ones((16, tk), BF16)
        if use_sink:
            sk = sink_ref[pl.ds(0, 1), :]

        def chain(n, carry):
            rho, c = n // tiles, n % tiles
            start = c * (tq * r) + rho
            if r == 1:
                start = pl.multiple_of(start, tq)
            variant = jnp.where(jnp.logical_and(i == 0, c == 0), 1, 0) + jnp.where(
                jnp.logical_and(i == nsteps - 1, c == tiles - 1), 2, 0)
            k2 = ks[_rows(start, tk, r), :].astype(BF16)
            v2t = jnp.concatenate([vs[_rows(start, tk, r), :].T.astype(BF16), ones], axis=0)
            q8 = _stack_heads([tile_in(qs, q_ref, t, start) for t in range(4)], lo, hi, HEAD_DIM ** -0.5)
            s = _nt(k2, q8) + bias_ref[pl.ds(pl.multiple_of(variant * tk, 8), tk), :]
            m = jnp.max(s, axis=0, keepdims=True)
            if use_sink:
                m = jnp.maximum(m, sk)
            pv = _nn(v2t, jnp.exp(s - m).astype(BF16))
            l = pv[LANES:LANES + 1]
            if use_sink:
                l = l + jnp.exp(sk - m)
            o8t = pv[:LANES] / l
            lse8 = jnp.broadcast_to(m + jnp.log(l), (LANES, 8 * tq))
            for t in range(4):
                if r > 1:
                    os_[t, _rows(start, tq, r), :] = _tile_from_columns(o8t, t, tq)
                    ls[t, _rows(start, tq, r), :] = _tile_from_columns(lse8, t, tq)
                else:
                    o_ref[pl.ds(start, tq), pl.ds(t * LANES, LANES)] = _tile_from_columns(o8t, t, tq).astype(out_dtype)
                    lse_ref[pl.ds(start, tq), pl.ds(t * LANES, LANES)] = _tile_from_columns(lse8, t, tq)
            return carry

        lax.fori_loop(0, r * tiles, chain, 0, unroll=4)
        if r > 1:
            for t in range(4):
                o_ref[:, pl.ds(t * LANES, LANES)] = os_[t].astype(out_dtype)
                lse_ref[:, pl.ds(t * LANES, LANES)] = ls[t]

    in_specs = specs + [pl.BlockSpec((bias.shape[0] // n_units, 8 * tq), lambda u, i: (u, 0))]
    args = [qkv] * 7 + [bias]
    if use_sink:
        in_specs.append(pl.BlockSpec((8, 8 * tq), lambda u, i: (u, 0)))
        args.append(sink)
    wide = pl.BlockSpec((chunk, 4 * LANES), lambda u, i: (i, u))
    win = hb + chunk + hb
    big = lambda: pltpu.VMEM((4, chunk if r > 1 else 8, LANES), F32)
    return pl.pallas_call(
        body, name=name, grid=(n_units, nsteps),
        out_shape=(jax.ShapeDtypeStruct((S, n_units * 512), out_dtype), jax.ShapeDtypeStruct((S, n_units * 512), F32)),
        in_specs=in_specs, out_specs=(wide, wide),
        scratch_shapes=[big(), pltpu.VMEM((win, LANES), F32), pltpu.VMEM((win, LANES), F32), big(), big()],
        compiler_params=_params(("parallel", "parallel"), VMEM_LIMIT),
    )(*args)


def _attn_bwd(qkv, bias, sink, o, do, lse, *, C, r, half, qoff, koff, voff, n_units, name):
    S = qkv.shape[0]
    tq = _tile_rows(r)
    tk = tq + 2 * half
    tiles = CHUNK // (r * tq)
    nsteps = S // CHUNK
    specs, hb = _attn_layout(S, C, r, half, qoff, koff, voff, CHUNK)
    use_sink = sink is not None

    def body(*refs):
        q_ref, kp, km, kn, vp, vm, vn, bias_ref = refs[:8]
        rest = list(refs[8:])
        sink_ref = rest.pop(0) if use_sink else None
        o_ref, do_ref, lse_ref, dq_ref, dk_hbm, dv_hbm = rest[:6]
        rest = rest[6:]
        dsink_ref = rest.pop(0) if use_sink else None
        qs, ks, vs, os_, dos, ls, dqs, acck, accv, sem = rest
        u, i = pl.program_id(0), pl.program_id(1)

        @pl.when(i == 0)
        def _():
            acck[...] = jnp.zeros_like(acck)
            accv[...] = jnp.zeros_like(accv)
            if use_sink:
                dsink_ref[...] = jnp.zeros_like(dsink_ref)

        if r > 1:
            for t in range(4):
                cols = pl.ds(t * LANES, LANES)
                qs[t] = q_ref[:, cols].astype(F32)
                os_[t] = o_ref[:, cols].astype(F32)
                dos[t] = do_ref[:, cols].astype(F32)
                ls[t] = lse_ref[:, cols]
        _stage(ks, [kp, km, kn])
        _stage(vs, [vp, vm, vn])
        lo, hi = _head_masks(tq)

        def tile_in(staged, ref, t, start):
            if r > 1:
                return staged[t, _rows(start, tq, r), :]
            return ref[pl.ds(start, tq), pl.ds(t * LANES, LANES)].astype(F32)

        base = pl.multiple_of(i * CHUNK, CHUNK)
        if use_sink:
            sk = sink_ref[pl.ds(0, 1), :]

        def chain(n, carry):
            rho, c = n // tiles, n % tiles
            start = c * (tq * r) + rho
            if r == 1:
                start = pl.multiple_of(start, tq)
            variant = jnp.where(jnp.logical_and(i == 0, c == 0), 1, 0) + jnp.where(
                jnp.logical_and(i == nsteps - 1, c == tiles - 1), 2, 0)
            k2 = ks[_rows(start, tk, r), :].astype(BF16)
            v2 = vs[_rows(start, tk, r), :].astype(BF16)
            k2t = ks[_rows(start, tk, r), :].T.astype(BF16)
            q8 = _stack_heads([tile_in(qs, q_ref, t, start) for t in range(4)], lo, hi, HEAD_DIM ** -0.5)
            do_tiles = [tile_in(dos, do_ref, t, start) for t in range(4)]
            do8 = _stack_heads(do_tiles, lo, hi, 1.0)
            deltas, lses = [], []
            for t in range(4):
                prod_t = (do_tiles[t] * tile_in(os_, o_ref, t, start)).T
                lse_t = tile_in(ls, lse_ref, t, start).T
                for a in range(2):
                    deltas.append(jnp.sum(prod_t[a * HEAD_DIM:(a + 1) * HEAD_DIM], axis=0, keepdims=True))
                    lses.append(lse_t[a * HEAD_DIM:a * HEAD_DIM + 1])
            delta8 = jnp.concatenate(deltas, axis=1)
            lse8 = jnp.concatenate(lses, axis=1)
            s = _nt(k2, q8) + bias_ref[pl.ds(pl.multiple_of(variant * tk, 8), tk), :]
            p = jnp.exp(s - lse8)
            dp = _nt(v2, do8)
            dsb = (p * (dp - delta8)).astype(BF16)
            dq8t = _nn(k2t, dsb)
            for t in range(4):
                dq_t = _tile_from_columns(dq8t, t, tq) * (HEAD_DIM ** -0.5)
                if r > 1:
                    dqs[t, _rows(start, tq, r), :] = dq_t
                else:
                    dq_ref[pl.ds(start, tq), pl.ds(t * LANES, LANES)] = dq_t.astype(BF16)
            arow = base + start
            if r == 1:
                arow = pl.multiple_of(arow, tq)
            acck[_rows(arow, tk, r), :] = acck[_rows(arow, tk, r), :] + _nn(dsb, q8)
            accv[_rows(arow, tk, r), :] = accv[_rows(arow, tk, r), :] + _nn(p.astype(BF16), do8)
            if use_sink:
                e = jnp.exp(sk - lse8) * delta8
                for h in range(8):
                    part = -jnp.sum(e[:, h * tq:(h + 1) * tq], axis=1, keepdims=True)
                    dsink_ref[pl.ds(h, 1), :] = dsink_ref[pl.ds(h, 1), :] + part
            return carry

        lax.fori_loop(0, r * tiles, chain, 0, unroll=2)
        if r > 1:
            for t in range(4):
                dq_ref[:, pl.ds(t * LANES, LANES)] = dqs[t].astype(BF16)

        @pl.when(i == nsteps - 1)
        def _():
            ck = pltpu.make_async_copy(acck.at[pl.ds(hb, S)], dk_hbm.at[u], sem.at[0])
            cv = pltpu.make_async_copy(accv.at[pl.ds(hb, S)], dv_hbm.at[u], sem.at[1])
            ck.start()
            cv.start()
            ck.wait()
            cv.wait()

    wide = pl.BlockSpec((CHUNK, 4 * LANES), lambda u, i: (i, u))
    hbm = pl.BlockSpec(memory_space=pl.ANY)
    in_specs = specs + [pl.BlockSpec((3 * tk, 8 * tq), lambda u, i: (u, 0))]
    args = [qkv] * 7 + [bias]
    if use_sink:
        in_specs.append(pl.BlockSpec((8, 8 * tq), lambda u, i: (u, 0)))
        args.append(sink)
    in_specs += [wide, wide, wide]
    args += [o, do, lse]
    out_shape = [jax.ShapeDtypeStruct((S, n_units * 512), BF16), jax.ShapeDtypeStruct((n_units, S, LANES), F32),
                 jax.ShapeDtypeStruct((n_units, S, LANES), F32)]
    out_specs = [wide, hbm, hbm]
    if use_sink:
        out_shape.append(jax.ShapeDtypeStruct((n_units * 8, LANES), F32))
        out_specs.append(pl.BlockSpec((8, LANES), lambda u, i: (u, 0)))
    win = hb + CHUNK + hb
    big = lambda: pltpu.VMEM((4, CHUNK if r > 1 else 8, LANES), F32)
    res = pl.pallas_call(
        body, name=name, grid=(n_units, nsteps), out_shape=tuple(out_shape), in_specs=in_specs, out_specs=tuple(out_specs),
        scratch_shapes=[big(), pltpu.VMEM((win, LANES), F32), pltpu.VMEM((win, LANES), F32), big(), big(), big(), big(),
                        pltpu.VMEM((S + 2 * hb, LANES), F32), pltpu.VMEM((S + 2 * hb, LANES), F32), pltpu.SemaphoreType.DMA((2,))],
        compiler_params=_params(("arbitrary", "arbitrary"), VMEM_LIMIT),
    )(*args)
    return res[0], res[1], res[2], (res[3] if use_sink else None)


def _merge_groups(os_, lses):
    S, W = os_[0].shape
    tm = 512

    def body(o0, o1, o2, l0, l1, l2, o_ref, lse_ref):
        ls = [l0[...], l1[...], l2[...]]
        mx = jnp.maximum(jnp.maximum(ls[0], ls[1]), ls[2])
        es = [jnp.exp(l - mx) for l in ls]
        den = es[0] + es[1] + es[2]
        o = (es[0] / den) * o0[...] + (es[1] / den) * o1[...] + (es[2] / den) * o2[...]
        o_ref[...] = o.astype(BF16)
        lse_ref[...] = mx + jnp.log(den)

    row = pl.BlockSpec((tm, W), lambda i: (i, 0))
    return pl.pallas_call(
        body, name="merge_groups", grid=(S // tm,),
        out_shape=(jax.ShapeDtypeStruct((S, W), BF16), jax.ShapeDtypeStruct((S, W), F32)),
        in_specs=[row] * 6, out_specs=(row, row), compiler_params=_params(("parallel",), VMEM_LIMIT),
    )(*os_, *lses)


def _columns(part_refs, rows):
    return jnp.concatenate([r[rows, :].astype(BF16) for r in part_refs], axis=1)


def _piece_specs(parts, rows, row_axis):
    arrays, specs = [], []
    for p in parts:
        if isinstance(p, tuple):
            arr, u = p
            index = (lambda *g, u=u: (u, g[row_axis], 0))
            arrays.append(arr)
            specs.append(pl.BlockSpec((None, rows, arr.shape[2]), index))
        else:
            arrays.append(p)
            specs.append(pl.BlockSpec((rows, p.shape[1]), lambda *g: (g[row_axis], 0)))
    return arrays, specs


def _norm_bwd(dy_parts, wt, x, dres, nw, sc, h, *, name):
    S, N = x.shape[0], wt.shape[0]
    npart = len(dy_parts)
    tm = 512
    part_arrays, part_specs = _piece_specs(dy_parts, tm, 0)

    def body(*refs):
        part_refs = refs[:npart]
        w_ref, x_ref, dres_ref, nw_ref, sc_ref, h_ref, dx_ref, st_ref, dw_ref, acc = refs[npart:]
        k = pl.program_id(0)

        @pl.when(k == 0)
        def _():
            st_ref[...] = jnp.zeros_like(st_ref)
            acc[...] = jnp.zeros_like(acc)

        dy = _columns(part_refs, slice(None))
        acc[...] += _tn(h_ref[...], dy)
        nwv, scale = nw_ref[...], 1.0 + sc_ref[...]
        sums = [jnp.zeros((1, D), F32)] * 3
        for half in range(SPLIT):
            rows = pl.ds(half * (tm // SPLIT), tm // SPLIT)
            dh = _nn(dy[half * (tm // SPLIT):(half + 1) * (tm // SPLIT)], w_ref[...])
            xv = x_ref[rows, :]
            rstd = lax.rsqrt(jnp.mean(xv * xv, axis=-1, keepdims=True) + RMS_EPS)
            xh = xv * rstd
            dxh = dh * (nwv * scale)
            dx_ref[rows, :] = dres_ref[rows, :] + rstd * (dxh - xh * jnp.mean(dxh * xh, axis=-1, keepdims=True))
            dhx = dh * xh
            sums = [sums[0] + jnp.sum(dh, axis=0, keepdims=True), sums[1] + jnp.sum(dhx * nwv, axis=0, keepdims=True),
                    sums[2] + jnp.sum(dhx * scale, axis=0, keepdims=True)]
        for q in range(3):
            st_ref[pl.ds(q, 1), :] = st_ref[pl.ds(q, 1), :] + sums[q]

        @pl.when(k == S // tm - 1)
        def _():
            dw_ref[...] = acc[...].T.astype(BF16)

    row = lambda w_: pl.BlockSpec((tm, w_), lambda i: (i, 0))
    vec = _const_spec((1, D))
    return pl.pallas_call(
        body, name=name, grid=(S // tm,),
        out_shape=(jax.ShapeDtypeStruct((S, D), F32), jax.ShapeDtypeStruct((8, D), F32), jax.ShapeDtypeStruct((N, D), BF16)),
        in_specs=part_specs + [_const_spec((N, D)), row(D), row(D), vec, vec, row(D)],
        out_specs=(row(D), _const_spec((8, D)), _const_spec((N, D))), scratch_shapes=[pltpu.VMEM((D, N), F32)],
        compiler_params=_params(("arbitrary",), VMEM_LIMIT),
    )(*part_arrays, wt, x, dres, nw, sc, h)


def _ffn_bwd(dx, y, g, w_out, gu, wt_in, x, nw, sc, y1, g1, w_mix, *, mix_is_transposed, name):
    S = dx.shape[0]
    K = w_out.shape[0]
    Km = w_mix.shape[1] if mix_is_transposed else w_mix.shape[0]
    tm = 256

    def body(dx_ref, y_ref, g_ref, wo_ref, gu_ref, wi_ref, x_ref, nw_ref, sc_ref, y1_ref, g1_ref, wm_ref,
             dgu_ref, dyb_ref, dxo_ref, da_ref, dy1_ref, stg_ref, stf_ref, stm_ref):
        @pl.when(pl.program_id(0) == 0)
        def _():
            stg_ref[...] = jnp.zeros_like(stg_ref)
            stf_ref[...] = jnp.zeros_like(stf_ref)
            stm_ref[...] = jnp.zeros_like(stm_ref)

        dxv = dx_ref[...]
        stg_ref[pl.ds(0, 1), :] = stg_ref[pl.ds(0, 1), :] + jnp.sum(dxv * y_ref[...].astype(F32), axis=0, keepdims=True)
        dyb = (dxv * g_ref[...]).astype(BF16)
        dyb_ref[...] = dyb
        da = _nt(dyb, wo_ref[...])
        dgate = (da * gu_ref[:, pl.ds(0, K)].astype(F32)).astype(BF16)
        dup = (da * gu_ref[:, pl.ds(K, K)].astype(F32)).astype(BF16)
        dgu_ref[:, pl.ds(0, K)] = dgate
        dgu_ref[:, pl.ds(K, K)] = dup
        dh = _nn(dgate, wi_ref[pl.ds(0, K), :]) + _nn(dup, wi_ref[pl.ds(K, K), :])
        xv = x_ref[...]
        rstd = lax.rsqrt(jnp.mean(xv * xv, axis=-1, keepdims=True) + RMS_EPS)
        xh = xv * rstd
        nwv, scale = nw_ref[...], 1.0 + sc_ref[...]
        dxh = dh * (nwv * scale)
        dx1 = dxv + rstd * (dxh - xh * jnp.mean(dxh * xh, axis=-1, keepdims=True))
        dxo_ref[...] = dx1
        dhx = dh * xh
        stf_ref[pl.ds(0, 1), :] = stf_ref[pl.ds(0, 1), :] + jnp.sum(dh, axis=0, keepdims=True)
        stf_ref[pl.ds(1, 1), :] = stf_ref[pl.ds(1, 1), :] + jnp.sum(dhx * nwv, axis=0, keepdims=True)
        stf_ref[pl.ds(2, 1), :] = stf_ref[pl.ds(2, 1), :] + jnp.sum(dhx * scale, axis=0, keepdims=True)
        stm_ref[pl.ds(0, 1), :] = stm_ref[pl.ds(0, 1), :] + jnp.sum(dx1 * y1_ref[...].astype(F32), axis=0, keepdims=True)
        dy1 = (dx1 * g1_ref[...]).astype(BF16)
        dy1_ref[...] = dy1
        da_ref[...] = (_nn(dy1, wm_ref[...]) if mix_is_transposed else _nt(dy1, wm_ref[...])).astype(BF16)

    row = lambda w_: pl.BlockSpec((tm, w_), lambda i: (i, 0))
    vec = _const_spec((1, D))
    st = jax.ShapeDtypeStruct((8, D), F32)
    act = lambda w_: jax.ShapeDtypeStruct((S, w_), BF16)
    return pl.pallas_call(
        body, name=name, grid=(S // tm,),
        out_shape=(act(2 * K), act(D), jax.ShapeDtypeStruct((S, D), F32), act(Km), act(D), st, st, st),
        in_specs=[row(D), row(D), vec, _const_spec(w_out.shape), row(2 * K), _const_spec(wt_in.shape), row(D), vec, vec,
                  row(D), vec, _const_spec(w_mix.shape)],
        out_specs=(row(2 * K), row(D), row(D), row(Km), row(D), _const_spec((8, D)), _const_spec((8, D)), _const_spec((8, D))),
        compiler_params=_params(("arbitrary",), VMEM_LIMIT),
    )(dx, y, g, w_out, gu, wt_in, x, nw, sc, y1, g1, w_mix)


def _weight_grad(a, b_parts, *, transpose_out, name):
    S = a.shape[0]
    N = sum(p[0].shape[2] if isinstance(p, tuple) else p.shape[1] for p in b_parts)
    npart = len(b_parts)
    nb = N // 2 if N > 4096 else N
    assert nb == N or npart == 1
    tk = 512
    if npart == 1:
        part_arrays, part_specs = list(b_parts), [pl.BlockSpec((tk, nb), lambda n, k: (k, n))]
    else:
        part_arrays, part_specs = _piece_specs(b_parts, tk, 1)

    def body(*refs):
        a_ref, part_refs = refs[0], refs[1:1 + npart]
        out_ref, acc = refs[1 + npart:]
        k = pl.program_id(1)

        @pl.when(k == 0)
        def _():
            acc[...] = jnp.zeros_like(acc)

        acc[...] += _tn(a_ref[...], _columns(part_refs, slice(None)))

        @pl.when(k == pl.num_programs(1) - 1)
        def _():
            out_ref[...] = (acc[...].T if transpose_out else acc[...]).astype(BF16)

    out_block = pl.BlockSpec((nb, D), lambda n, k: (n, 0)) if transpose_out else pl.BlockSpec((D, nb), lambda n, k: (0, n))
    return pl.pallas_call(
        body, name=name, grid=(N // nb, S // tk),
        out_shape=jax.ShapeDtypeStruct((N, D) if transpose_out else (D, N), BF16),
        in_specs=[pl.BlockSpec((tk, D), lambda n, k: (k, 0))] + part_specs,
        out_specs=out_block, scratch_shapes=[pltpu.VMEM((D, nb), F32)],
        compiler_params=_params(("parallel", "arbitrary"), VMEM_LIMIT),
    )(a, *part_arrays)


def _adamw(w, g, m, v):
    m = ADAM_B1 * m + (1.0 - ADAM_B1) * g
    v = ADAM_B2 * v + (1.0 - ADAM_B2) * (g * g)
    m_hat = m / (1.0 - ADAM_B1 ** ADAM_STEP)
    v_hat = v / (1.0 - ADAM_B2 ** ADAM_STEP)
    delta = -ADAM_LR * (m_hat / (jnp.sqrt(v_hat) + ADAM_EPS) + ADAM_WD * w)
    return delta, m, v


def _adam_segment(parts, own, w, m, v, outs, layer, *, name):
    R, C = own.shape
    nc = 2 if R % 32 == 0 else 1
    ch = R // nc

    def body(*refs):
        p_hbm, o_hbm, w_hbm, m_hbm, v_hbm = refs[:5]
        res_hbm = refs[-14:-10]
        p_buf, o_buf, w_buf, m_buf, v_buf, g_buf, d_buf, mo_buf, vo_buf, sems = refs[-10:]
        ins = []
        for c in range(nc):
            rows = pl.ds(c * ch, ch)
            pairs = [(p_hbm.at[:, rows], p_buf.at[:, rows]), (o_hbm.at[rows], o_buf.at[rows]), (w_hbm.at[layer, rows], w_buf.at[rows]),
                     (m_hbm.at[layer, rows], m_buf.at[rows]), (v_hbm.at[layer, rows], v_buf.at[rows])]
            copies = [pltpu.make_async_copy(src, dst, sems.at[c * 9 + q]) for q, (src, dst) in enumerate(pairs)]
            for cp in copies:
                cp.start()
            ins.append(copies)
        me = _my_index()
        outs_started = []
        for c in range(nc):
            rows = pl.ds(c * ch, ch)
            for cp in ins[c]:
                cp.wait()
            g = jnp.zeros((ch, C), F32)
            for j in range(N_DEV):
                g = g + jnp.where(me == j, o_buf[rows, :], p_buf[j, rows, :]).astype(F32)
            delta, mn, vn = _adamw(w_buf[rows, :], g, m_buf[rows, :], v_buf[rows, :])
            for q, (buf, val) in enumerate(((g_buf, g), (d_buf, delta), (mo_buf, mn), (vo_buf, vn))):
                buf[rows, :] = val
                cp = pltpu.make_async_copy(buf.at[rows], res_hbm[q].at[layer, rows], sems.at[c * 9 + 5 + q])
                cp.start()
                outs_started.append(cp)
        for cp in outs_started:
            cp.wait()

    hbm = pl.BlockSpec(memory_space=pl.ANY)
    passed = [] if outs is None else [hbm] * 4
    shp = jax.ShapeDtypeStruct(w.shape, F32)
    f32_buf = lambda: pltpu.VMEM((R, C), F32)
    return pl.pallas_call(
        body, name=name, out_shape=(shp,) * 4, in_specs=[hbm] * 5 + passed, out_specs=(hbm,) * 4,
        input_output_aliases={5 + q: q for q in range(len(passed))},
        scratch_shapes=[pltpu.VMEM((N_DEV, R, C), BF16), pltpu.VMEM((R, C), BF16)] + [f32_buf() for _ in range(7)]
        + [pltpu.SemaphoreType.DMA((nc * 9,))],
        compiler_params=_params(vmem=VMEM_LIMIT),
    )(parts, own, w, m, v, *(outs or ()))


def _adam_ada_w(cond_t, dmod, w, m, v):
    ncol = w.shape[-1]
    tr = 512

    def body(c_ref, d_ref, w_ref, m_ref, v_ref, g_out, d_out, m_out, v_out):
        g = _nn(c_ref[...], d_ref[0])
        delta, mn, vn = _adamw(w_ref[0], g, m_ref[0], v_ref[0])
        g_out[0] = g
        d_out[0] = delta
        m_out[0] = mn
        v_out[0] = vn

    blk = pl.BlockSpec((1, tr, ncol), lambda l, i: (l, i, 0))
    shp = jax.ShapeDtypeStruct(w.shape, F32)
    return pl.pallas_call(
        body, name="adam_ada_w", grid=(DEPTH, D // tr), out_shape=(shp,) * 4,
        in_specs=[pl.BlockSpec((tr, LANES), lambda l, i: (i, 0)), pl.BlockSpec((1, LANES, ncol), lambda l, i: (l, 0, 0)), blk, blk, blk],
        out_specs=(blk,) * 4, compiler_params=_params(("parallel", "parallel"), VMEM_LIMIT),
    )(cond_t, dmod, w, m, v)


TILE_ROWS = 168


def _stat_sources():
    pairs = []
    for i in range(DEPTH):
        b = 32 * i
        for q, src in enumerate((b, b + 1, b + 8, b + 16, b + 17, b + 24)):
            pairs.append((6 * i + q, src))
        pairs.append((24 + i, b + 2))
        pairs.append((32 + i, b + 18))
    pairs += [(40, 128), (41, 129)]
    return pairs


def _small_exchange(tiles, w, m, v):
    loss_row, sink_row, sink_src = 41, 48, 136

    def body(s_ref, w_ref, m_ref, v_ref, dmod_out, g_out, d_out, m_out, v_out, loss_out, all_ref, tot_ref, send_sems, recv_sems):
        me = _my_index()
        all_ref[me] = s_ref[...]
        copies = []
        for k in range(1, N_DEV):
            dev, _ = _peer(k)
            cp = pltpu.make_async_remote_copy(src_ref=s_ref, dst_ref=all_ref.at[me], send_sem=send_sems.at[k - 1],
                                              recv_sem=recv_sems.at[k - 1], device_id=dev, device_id_type=MESH)
            cp.start()
            copies.append(cp)
        for k in range(1, N_DEV):
            dev, pidx = _peer(k)
            pltpu.make_async_remote_copy(src_ref=s_ref, dst_ref=all_ref.at[pidx], send_sem=send_sems.at[k - 1],
                                         recv_sem=recv_sems.at[k - 1], device_id=dev, device_id_type=MESH).wait_recv()
        for cp in copies:
            cp.wait_send()
        tot = all_ref[0]
        for j in range(1, N_DEV):
            tot = tot + all_ref[j]
        tot_ref[...] = tot
        g_out[...] = jnp.zeros_like(g_out)
        for dst, src in _stat_sources():
            g_out[pl.ds(dst, 1), :] = tot_ref[pl.ds(src, 1), :]
            if dst < 24:
                for j in range(N_DEV):
                    dmod_out[j, pl.ds(dst, 1), :] = all_ref[j, pl.ds(src, 1), :]
        lane = lax.broadcasted_iota(jnp.int32, (1, D), 1)
        sink = jnp.zeros((1, D), F32)
        for h in range(32):
            sink = jnp.where(lane == h, tot_ref[pl.ds(sink_src + h, 1), :], sink)
        g_out[pl.ds(sink_row, 1), :] = sink
        g = g_out[...]
        delta, mn, vn = _adamw(w_ref[...], g, m_ref[...], v_ref[...])
        d_out[...] = delta
        m_out[...] = mn
        v_out[...] = vn
        loss = jnp.sum(g[loss_row:loss_row + 1, :], axis=-1, keepdims=True) * (0.5 / D)
        loss_out[...] = jnp.broadcast_to(loss, loss_out.shape)

    vm = pl.BlockSpec(memory_space=pltpu.VMEM)
    shp = jax.ShapeDtypeStruct((STAT_ROWS, D), F32)
    return pl.pallas_call(
        body, name="small_exchange",
        out_shape=(jax.ShapeDtypeStruct((N_DEV, 24, D), F32), shp, shp, shp, shp, jax.ShapeDtypeStruct((8, LANES), F32)),
        in_specs=[vm] * 4, out_specs=(vm,) * 6,
        scratch_shapes=[pltpu.VMEM((N_DEV, TILE_ROWS, D), F32), pltpu.VMEM((TILE_ROWS, D), F32),
                        pltpu.SemaphoreType.DMA((N_DEV - 1,)), pltpu.SemaphoreType.DMA((N_DEV - 1,))],
        compiler_params=_params(vmem=VMEM_LIMIT),
    )(tiles, w, m, v)


def _to_rows(name, a):
    if name in ("ffn_in", "a_in", "b_in"):
        return a.T
    if name == "b_out":
        return a.T.reshape(-1, D)
    return a


def _rows8(a):
    return jnp.pad(a, ((0, 8 - a.shape[0]), (0, 0)))


def _pack_small(ada_b, norm_mix, norm_ffn, final_norm, sink):
    sink_row = jnp.pad(sink.reshape(1, -1), ((0, 0), (0, D - sink.size)))
    return jnp.concatenate([ada_b.reshape(24, D), _rows8(norm_mix), _rows8(norm_ffn), _rows8(final_norm.reshape(1, D)),
                            _rows8(sink_row)], axis=0)


def _unpack_small(a):
    return a[0:24].reshape(4, 6 * D), a[24:28], a[32:36], a[40], a[48, :32].reshape(2, 16)


def kernel(x, c, ada_w, ada_b, norm_mix, norm_ffn, ffn_w_in, ffn_w_out, a_w_in, a_w_out, a_sink, b_w_in, b_w_out, final_norm, loss_target, m_ada_w, m_ada_b, m_norm_mix, m_norm_ffn, m_ffn_w_in, m_ffn_w_out, m_a_w_in, m_a_w_out, m_a_sink, m_b_w_in, m_b_w_out, m_final_norm, v_ada_w, v_ada_b, v_norm_mix, v_norm_ffn, v_ffn_w_in, v_ffn_w_out, v_a_w_in, v_a_w_out, v_a_sink, v_b_w_in, v_b_w_out, v_final_norm):
    S = x.shape[1]
    x0 = x.reshape(S, D)
    target = loss_target.reshape(S, D)
    me = _my_index()
    ncol = ada_w.shape[-1]

    ada_b_mine = lax.dynamic_slice_in_dim(ada_b, me * ncol, ncol, axis=1)
    cond_all, parts = _cond_exchange(jnp.broadcast_to(c.reshape(1, D), (8, D)), ada_w, ada_b_mine)
    mod = lax.dynamic_index_in_dim(parts, me, axis=2, keepdims=False)
    mod = jnp.transpose(mod, (1, 0, 2)).reshape(DEPTH, 6, 1, D)

    weights = {"ffn_in": ffn_w_in, "ffn_out": ffn_w_out, "a_in": a_w_in, "a_out": a_w_out, "b_in": b_w_in, "b_out": b_w_out}
    shard = {(n, l): _to_rows(n, weights[n][l]).astype(BF16) for n, l, _ in SEGMENTS}
    first = [sg for sg in _layer_segments(0) if not sg[0].startswith("ffn")]
    gathered0 = _all_gather_weights([shard[(n, l)] for n, l, _ in first])
    W = {(n, l): g for (n, l, _), g in zip(first, gathered0)}
    groups = [[sg for sg in _layer_segments(0) if sg[0].startswith("ffn")]] + [_layer_segments(i) for i in range(1, DEPTH)]
    gathers, order = [], gathered0[0]
    for q, segs in enumerate(groups):
        zones = [(N_DEV, rows, D) for _, _, rows in segs]
        gathers.append(_exchange_start([shard[(n, l)] for n, l, _ in segs], zones, [(s, 0) for s in range(len(segs))],
                                       [sg[2] for sg in segs], False, order, "weight_gather_start_%d" % q))
        order = gathers[-1][-1]
    gather_token = order[0:1, 0:1]

    def finish_gather(q, after):
        segs = groups[q]
        zones = _exchange_wait(gathers[q], len(segs), [(s, 0) for s in range(len(segs))], [sg[2] for sg in segs], True, after,
                               "weight_gather_wait_%d" % q)
        for (n, l, rows), zone in zip(segs, zones):
            W[(n, l)] = zone.reshape(D, 512) if n == "b_out" else zone.reshape(N_DEV * rows, D)

    a_slopes, b_slopes = _slopes(16), _slopes(24)
    bias_a = _alibi_bias(a_slopes, A_HALF, 1)
    bias_b = [_alibi_bias(b_slopes[8 * g:8 * g + 8], B_HALF, dil) for g, dil in enumerate(B_DILS)]
    bias_b_fwd = [_alibi_bias(b_slopes[8 * g:8 * g + 8], B_HALF, dil, max(CHUNK, TQ * dil), both=True) for g, dil in enumerate(B_DILS)]
    a_geom = dict(C=A_QKV, r=1, half=A_HALF, qoff=0, koff=1024, voff=1280, n_units=2)
    b_geom = [dict(C=B_QKV, r=dil, half=B_HALF, qoff=512 * g, koff=1536 + 128 * g, voff=1920 + 128 * g, n_units=1)
              for g, dil in enumerate(B_DILS)]

    saved = []
    xcur = x0
    for i in range(DEPTH):
        j = i // 2
        sh1, sc1, g1, sh2, sc2, g2 = [mod[i, q] for q in range(6)]
        nm, nf = norm_mix[i].reshape(1, D), norm_ffn[i].reshape(1, D)
        if i == 0:
            nm = nm + gather_token
        if i > 0:
            finish_gather(i, xcur)
        if i % 2 == 0:
            sink_rep = jnp.repeat(jnp.repeat(a_sink[j], TQ).reshape(2, 1, 8 * TQ), 8, axis=1).reshape(16, 8 * TQ)
            h1, qkv = _proj(xcur, nm, sc1, sh1, W[("a_in", j)], name="proj_a")
            o, lse = _attn_fwd(qkv, bias_a, sink_rep, out_dtype=BF16, name="attn_a_fwd", **a_geom)
            if i == 0:
                finish_gather(0, o)
            x1, y1, h2, gu, act = _out_ffn_in(o, W[("a_out", j)], xcur, g1, nf, sc2, sh2, W[("ffn_in", i)],
                                              w_is_transposed=False, name="out_a_ffn_in")
        else:
            sink_rep = None
            h1, qkv = _proj(xcur, nm, sc1, sh1, W[("b_in", j)], name="proj_b")
            outs = [_attn_fwd(qkv, bias_b_fwd[g], None, out_dtype=F32, name="attn_b%d_fwd" % g, **b_geom[g]) for g in range(3)]
            o, lse = _merge_groups([t[0] for t in outs], [t[1] for t in outs])
            x1, y1, h2, gu, act = _out_ffn_in(o, W[("b_out", j)], xcur, g1, nf, sc2, sh2, W[("ffn_in", i)],
                                              w_is_transposed=True, name="out_b_ffn_in")
        if i < DEPTH - 1:
            x2, y2 = _ffn_out(act, W[("ffn_out", i)], x1, g2, None, None, name="ffn_out")
        else:
            x2, y2, head_stats = _ffn_out(act, W[("ffn_out", i)], x1, g2, target, final_norm.reshape(1, D), name="ffn_out_loss")
        saved.append(dict(x0=xcur, h1=h1, qkv=qkv, o=o, lse=lse, y1=y1, x1=x1, h2=h2, gu=gu, act=act, y2=y2, sink=sink_rep))
        xcur = x2

    dx = xcur

    dW = {}
    stat_tiles, dsink = [None] * DEPTH, [None] * 2
    exchanges = []
    start_token = None

    def start_exchange(segs):
        own = [lax.dynamic_slice_in_dim(dW[(n, l)], me * rows, rows, axis=0) for n, l, rows in segs]
        zones = [(N_DEV, rows, D) for _, _, rows in segs]
        started = _exchange_start([dW[(n, l)] for n, l, _ in segs], zones, [(s, 0) for s in range(len(segs))],
                                  [sg[2] for sg in segs], True, own[0], "grad_exchange_start_%d" % len(exchanges))
        exchanges.append((segs, started, own))
        return started[-1][0:1, 0:1]

    for i in reversed(range(DEPTH)):
        j = i // 2
        sv = saved[i]
        sh1, sc1, g1, sh2, sc2, g2 = [mod[i, q] for q in range(6)]
        if start_token is not None:
            g2 = g2 + start_token
            start_token = None
        nm, nf = norm_mix[i].reshape(1, D), norm_ffn[i].reshape(1, D)
        mix = "a_out" if i % 2 == 0 else "b_out"
        dgu, dy2, dx1, do, dy1, st_g2, st_f, st_g1 = _ffn_bwd(
            dx, sv["y2"], g2, W[("ffn_out", i)], sv["gu"], W[("ffn_in", i)], sv["x1"], nf, sc2, sv["y1"], g1, W[(mix, j)],
            mix_is_transposed=(i % 2 == 1), name="ffn_bwd_" + mix)
        dW[("ffn_out", i)] = _weight_grad(dy2, [sv["act"]], transpose_out=True, name="dw_ffn_out")
        dW[("ffn_in", i)] = _weight_grad(sv["h2"], [dgu], transpose_out=True, name="dw_ffn_in")
        sink_bwd = sv["sink"]
        if i == 0:
            sink_bwd = sink_bwd + start_exchange([sg for sg in _layer_segments(0) if sg[0].startswith("ffn")])
        if i % 2 == 0:
            dW[("a_out", j)] = _weight_grad(dy1, [sv["o"]], transpose_out=True, name="dw_a_out")
            dq, dk, dv, ds = _attn_bwd(sv["qkv"], bias_a, sink_bwd, sv["o"], do, sv["lse"], name="attn_a_bwd", **a_geom)
            dsink[j] = ds
            dqkv = [dq, (dk, 0), (dk, 1), (dv, 0), (dv, 1)]
            dx0, st_m, dW[("a_in", j)] = _norm_bwd(dqkv, W[("a_in", j)], sv["x0"], dx1, nm, sc1, sv["h1"], name="proj_a_bwd")
        else:
            dW[("b_out", j)] = _weight_grad(dy1, [sv["o"]], transpose_out=False, name="dw_b_out").reshape(N_DEV * 64, D)
            gr = [_attn_bwd(sv["qkv"], bias_b[g], None, sv["o"], do, sv["lse"], name="attn_b%d_bwd" % g, **b_geom[g]) for g in range(3)]
            dqkv = [t[0] for t in gr] + [(t[1], 0) for t in gr] + [(t[2], 0) for t in gr]
            dx0, st_m, dW[("b_in", j)] = _norm_bwd(dqkv, W[("b_in", j)], sv["x0"], dx1, nm, sc1, sv["h1"], name="proj_b_bwd")
        stat_tiles[i] = [st_m, st_g1, st_f, st_g2]
        if i > 0:
            start_token = start_exchange(_layer_segments(i))
        else:
            start_exchange([sg for sg in _layer_segments(0) if not sg[0].startswith("ffn")])
        dx = dx0
    grad_x = dx.reshape(1, S, D)

    masters = {"ffn_in": (ffn_w_in, m_ffn_w_in, v_ffn_w_in), "ffn_out": (ffn_w_out, m_ffn_w_out, v_ffn_w_out),
               "a_in": (a_w_in, m_a_w_in, v_a_w_in), "a_out": (a_w_out, m_a_w_out, v_a_w_out),
               "b_in": (b_w_in, m_b_w_in, v_b_w_in), "b_out": (b_w_out, m_b_w_out, v_b_w_out)}
    by_rows = ("ffn_in", "a_in", "b_in", "b_out")
    masters = {n: tuple(jnp.swapaxes(t, 1, 2) for t in wmv) if n in by_rows else wmv for n, wmv in masters.items()}
    results = {n: None for n in masters}
    after = dx
    for e, (segs, started, own) in enumerate(exchanges):
        zones = _exchange_wait(started, len(segs), [(s, 0) for s in range(len(segs))], [sg[2] for sg in segs], False, after,
                               "grad_exchange_wait_%d" % e)
        for (n, l, rows), zone, mine in zip(segs, zones, own):
            if n == "b_out":
                zone, mine = zone.reshape(N_DEV, LANES, 512), mine.reshape(LANES, 512)
            results[n] = _adam_segment(zone, mine, *masters[n], results[n], l, name="adam_" + n)
            after = results[n][0]
    big = {(kind, n): jnp.swapaxes(results[n][q], 1, 2) if n in by_rows else results[n][q]
           for q, kind in enumerate(("grad", "delta", "m", "v")) for n in masters}

    tiles = jnp.concatenate([t for i in range(DEPTH) for t in stat_tiles[i]] + [head_stats]
                            + [jnp.pad(ds, ((0, 0), (0, D - LANES))) for ds in dsink], axis=0)
    small = [_pack_small(*t) for t in ((ada_b, norm_mix, norm_ffn, final_norm, a_sink),
                                       (m_ada_b, m_norm_mix, m_norm_ffn, m_final_norm, m_a_sink),
                                       (v_ada_b, v_norm_mix, v_norm_ffn, v_final_norm, v_a_sink))]
    dmod_all, sg, sd, sm, sv_, loss_tile = _small_exchange(tiles, *small)
    loss = loss_tile[0, 0]
    dmod_all = dmod_all.reshape(N_DEV, DEPTH, 6 * D)
    dmod_mine = lax.dynamic_slice_in_dim(dmod_all, me * ncol, ncol, axis=2)
    dmod_pad = jnp.pad(jnp.transpose(dmod_mine, (1, 0, 2)), ((0, 0), (0, LANES - N_DEV), (0, 0))).astype(BF16)
    cond_t = jnp.pad(cond_all.T, ((0, 0), (0, LANES - N_DEV))).astype(BF16)
    ada = _adam_ada_w(cond_t, dmod_pad, ada_w, m_ada_w, v_ada_w)

    outs = [loss, grad_x]
    small_res = [_unpack_small(t) for t in (sg, sd, sm, sv_)]
    for q, kind in enumerate(("grad", "delta", "m", "v")):
        ab, nm_, nf_, fn, sk = small_res[q]
        outs += [ada[q], ab, nm_, nf_, big[(kind, "ffn_in")], big[(kind, "ffn_out")], big[(kind, "a_in")], big[(kind, "a_out")],
                 sk, big[(kind, "b_in")], big[(kind, "b_out")], fn]
    return tuple(outs)
```

```python
import math

import numpy as np
import jax
import jax.numpy as jnp
from jax import lax
from jax.experimental import pallas as pl
from jax.experimental.pallas import tpu as pltpu

D = 1024
HEAD_DIM = 64
D_FF = 2816
DEPTH = 4
N_DEV = 8
A_QKV = 1536
B_QKV = 2304
A_HALF = 128
B_HALF = 64
B_DILS = (1, 4, 16)
RMS_EPS = 1e-6
NEG = -1e30
ADAM_LR = 0.001
ADAM_B1 = 0.9
ADAM_B2 = 0.999
ADAM_EPS = 1e-08
ADAM_WD = 0.01
ADAM_STEP = 10

LANES = 128
SPLIT = 2
TQ = 128
VMEM_LIMIT = 56 * 1024 * 1024
MESH = pl.DeviceIdType.MESH
F32 = jnp.float32
BF16 = jnp.bfloat16

SEGMENTS = ([("ffn_in", l, 704) for l in range(4)] + [("ffn_out", l, 352) for l in range(4)]
            + [("a_in", j, 192) for j in range(2)] + [("a_out", j, 128) for j in range(2)]
            + [("b_in", j, 288) for j in range(2)] + [("b_out", j, 64) for j in range(2)])
def _layer_segments(i):
    mixer = "a" if i % 2 == 0 else "b"
    return [s for s in SEGMENTS if (s[0].startswith("ffn") and s[1] == i) or (s[0].startswith(mixer + "_") and s[1] == i // 2)]


def _offsets(segs):
    rows = [s[2] for s in segs]
    return [sum(rows[:k]) for k in range(len(rows))], sum(rows)
STAT_ROWS = 56


def _nn(a, b):
    return jnp.dot(a, b, preferred_element_type=F32)


def _nt(a, b):
    return lax.dot_general(a, b, (((1,), (1,)), ((), ())), preferred_element_type=F32)


def _tn(a, b):
    return lax.dot_general(a, b, (((0,), (0,)), ((), ())), preferred_element_type=F32)


def _params(dims=None, vmem=None):
    kw = {}
    if dims is not None:
        kw["dimension_semantics"] = dims
    if vmem is not None:
        kw["vmem_limit_bytes"] = vmem
    return pltpu.CompilerParams(**kw)


def _my_index():
    return 4 * lax.axis_index("x") + 2 * lax.axis_index("y") + lax.axis_index("c")


def _peer(k):
    x, y, c = lax.axis_index("x"), lax.axis_index("y"), lax.axis_index("c")
    px, py, pc = x ^ ((k >> 2) & 1), y ^ ((k >> 1) & 1), c ^ (k & 1)
    return (px, py, pc), 4 * px + 2 * py + pc


def _const_spec(shape):
    nd = len(shape)
    return pl.BlockSpec(shape, lambda *_: (0,) * nd)


def _cond_exchange(c_tile, ada_w, ada_b_mine):
    ncol = ada_w.shape[-1]

    def body(c_ref, w_ref, b_ref, cond_ref, parts_ref, call_ref, mine_ref, send_sems, recv_sems):
        me = _my_index()
        call_ref[me] = c_ref[...]
        copies = []
        for k in range(1, N_DEV):
            dev, _ = _peer(k)
            cp = pltpu.make_async_remote_copy(src_ref=c_ref, dst_ref=call_ref.at[me], send_sem=send_sems.at[0, k - 1],
                                              recv_sem=recv_sems.at[0, k - 1], device_id=dev, device_id_type=MESH)
            cp.start()
            copies.append(cp)
        for k in range(1, N_DEV):
            _, pidx = _peer(k)
            pltpu.make_async_remote_copy(src_ref=c_ref, dst_ref=call_ref.at[pidx], send_sem=send_sems.at[0, k - 1],
                                         recv_sem=recv_sems.at[0, k - 1], device_id=_peer(k)[0], device_id_type=MESH).wait_recv()
        for cp in copies:
            cp.wait_send()
        row = lax.broadcasted_iota(jnp.int32, (N_DEV, D), 0)
        cmat = jnp.zeros((N_DEV, D), F32)
        for j in range(N_DEV):
            cmat = jnp.where(row == j, call_ref[j], cmat)
        cond = cmat * jax.nn.sigmoid(cmat)
        cond_ref[...] = cond
        cb = cond.astype(BF16)
        for l in range(DEPTH):
            mine_ref[l] = _nn(cb, w_ref[l].astype(BF16)) + b_ref[pl.ds(l, 1), :]
        parts_ref[me] = mine_ref[...]
        copies = []
        for k in range(1, N_DEV):
            dev, _ = _peer(k)
            cp = pltpu.make_async_remote_copy(src_ref=mine_ref, dst_ref=parts_ref.at[me], send_sem=send_sems.at[1, k - 1],
                                              recv_sem=recv_sems.at[1, k - 1], device_id=dev, device_id_type=MESH)
            cp.start()
            copies.append(cp)
        for k in range(1, N_DEV):
            dev, pidx = _peer(k)
            pltpu.make_async_remote_copy(src_ref=mine_ref, dst_ref=parts_ref.at[pidx], send_sem=send_sems.at[1, k - 1],
                                         recv_sem=recv_sems.at[1, k - 1], device_id=dev, device_id_type=MESH).wait_recv()
        for cp in copies:
            cp.wait_send()

    vm = pl.BlockSpec(memory_space=pltpu.VMEM)
    return pl.pallas_call(
        body, name="cond_exchange",
        out_shape=(jax.ShapeDtypeStruct((N_DEV, D), F32), jax.ShapeDtypeStruct((N_DEV, DEPTH, N_DEV, ncol), F32)),
        in_specs=[vm, vm, vm], out_specs=(vm, vm),
        scratch_shapes=[pltpu.VMEM((N_DEV, N_DEV, D), F32), pltpu.VMEM((DEPTH, N_DEV, ncol), F32),
                        pltpu.SemaphoreType.DMA((2, N_DEV - 1)), pltpu.SemaphoreType.DMA((2, N_DEV - 1))],
        compiler_params=_params(vmem=VMEM_LIMIT),
    )(c_tile, ada_w, ada_b_mine)[:2]


def _all_gather_weights(shards):
    n = len(shards)
    big = max(range(n), key=lambda s: shards[s].shape[0])
    total = sum(sh.shape[0] for sh in shards)
    assert N_DEV * shards[big].shape[0] >= total

    def body(*refs):
        ins, outs = refs[:n], refs[n:2 * n]
        local_sems, send_sems, recv_sems = refs[2 * n:]
        me = _my_index()
        local = []
        for s in range(n):
            rows = ins[s].shape[0]
            cp = pltpu.make_async_copy(ins[s], outs[s].at[pl.ds(me * rows, rows)], local_sems.at[s])
            cp.start()
            local.append(cp)
        for k in range(1, N_DEV):
            dev, _ = _peer(k)
            for s in range(n):
                rows = ins[s].shape[0]
                pltpu.make_async_remote_copy(src_ref=ins[s], dst_ref=outs[s].at[pl.ds(me * rows, rows)],
                                             send_sem=send_sems.at[k - 1], recv_sem=recv_sems.at[k - 1],
                                             device_id=dev, device_id_type=MESH).start()
        whole = outs[big].at[pl.ds(0, total)]
        for k in range(1, N_DEV):
            dev, _ = _peer(k)
            w = pltpu.make_async_remote_copy(src_ref=whole, dst_ref=whole, send_sem=send_sems.at[k - 1],
                                             recv_sem=recv_sems.at[k - 1], device_id=dev, device_id_type=MESH)
            w.wait_send()
            w.wait_recv()
        for cp in local:
            cp.wait()

    hbm = pl.BlockSpec(memory_space=pl.ANY)
    return pl.pallas_call(
        body, name="weight_all_gather",
        out_shape=tuple(jax.ShapeDtypeStruct((N_DEV * s.shape[0], D), s.dtype) for s in shards),
        in_specs=[hbm] * n, out_specs=tuple([hbm] * n),
        scratch_shapes=[pltpu.SemaphoreType.DMA((n,)), pltpu.SemaphoreType.DMA((N_DEV - 1,)),
                        pltpu.SemaphoreType.DMA((N_DEV - 1,))],
    )(*shards)


HBM = pl.BlockSpec(memory_space=pltpu.HBM)
SEM = pl.BlockSpec(memory_space=pltpu.SEMAPHORE)
EFFECT = pltpu.SideEffectType.DATAFLOW_SIDE_EFFECTING


def _exchange_start(srcs, landings, dst, rows, to_peer_rows, after, name):
    n, nl = len(srcs), len(landings)

    def body(*refs):
        src_refs = refs[:n]
        send_sems, recv_sems = refs[n + 1], refs[n + 2]
        land_refs = refs[2 * n + 3:2 * n + 3 + nl]
        token = refs[-1]
        me = _my_index()
        for k in range(1, N_DEV):
            dev, pidx = _peer(k)
            for q in range(n):
                src = src_refs[q].at[pl.ds(pidx * rows[q], rows[q])] if to_peer_rows else src_refs[q]
                pltpu.make_async_remote_copy(src_ref=src, dst_ref=land_refs[dst[q][0]].at[me, pl.ds(dst[q][1], rows[q])],
                                             send_sem=send_sems.at[k * n + q], recv_sem=recv_sems.at[k * n + q],
                                             device_id=dev, device_id_type=MESH).start()
        if not to_peer_rows:
            for q in range(n):
                pltpu.make_async_copy(src_refs[q], land_refs[dst[q][0]].at[me, pl.ds(dst[q][1], rows[q])], send_sems.at[q]).start()
        token[...] = jnp.zeros_like(token)

    sems = pltpu.SemaphoreType.DMA((N_DEV * n,))
    return pl.pallas_call(
        body, name=name,
        out_shape=(sems, sems, *[pltpu.HBM(a.shape, a.dtype) for a in srcs], *[pltpu.HBM(shape, BF16) for shape in landings],
                   jax.ShapeDtypeStruct((8, LANES), F32)),
        in_specs=[HBM] * n + [pl.BlockSpec(memory_space=pl.ANY)],
        out_specs=(SEM, SEM, *[HBM] * (n + nl), pl.BlockSpec(memory_space=pltpu.VMEM)),
        input_output_aliases={q: 2 + q for q in range(n)},
        compiler_params=pltpu.CompilerParams(has_side_effects=EFFECT),
    )(*[pltpu.with_memory_space_constraint(a, pltpu.HBM) for a in srcs], after)


def _exchange_wait(started, n, dst, rows, own_slot, after, name):
    send_sems, recv_sems = started[0], started[1]
    arrays = list(started[2:-1])
    n1 = len(arrays)

    def body(*refs):
        land_refs = refs[n:n1]
        sends, recvs = refs[n1], refs[n1 + 1]
        for k in range(1, N_DEV):
            dev, _ = _peer(k)
            for q in range(n):
                slot = land_refs[dst[q][0]].at[0, pl.ds(dst[q][1], rows[q])]
                w = pltpu.make_async_remote_copy(src_ref=slot, dst_ref=slot, send_sem=sends.at[k * n + q],
                                                 recv_sem=recvs.at[k * n + q], device_id=dev, device_id_type=MESH)
                w.wait_send()
                w.wait_recv()
        if own_slot:
            for q in range(n):
                slot = land_refs[dst[q][0]].at[0, pl.ds(dst[q][1], rows[q])]
                pltpu.make_async_copy(slot, slot, sends.at[q]).wait()

    return pl.pallas_call(
        body, name=name, out_shape=tuple(pltpu.HBM(a.shape, a.dtype) for a in arrays),
        in_specs=[HBM] * n1 + [SEM, SEM, pl.BlockSpec(memory_space=pl.ANY)], out_specs=tuple([HBM] * n1),
        input_output_aliases={q: q for q in range(n1)},
        compiler_params=pltpu.CompilerParams(has_side_effects=EFFECT),
    )(*arrays, send_sems, recv_sems, after)[n:]


def _norm_mod(x, nw, sc, sh):
    ms = jnp.mean(x * x, axis=-1, keepdims=True)
    xh = x * lax.rsqrt(ms + RMS_EPS)
    return xh, (xh * nw) * (1.0 + sc) + sh


def _proj(x, nw, sc, sh, wt, *, name):
    S, N = x.shape[0], wt.shape[0]
    tm = 512

    def body(x_ref, nw_ref, sc_ref, sh_ref, w_ref, h_ref, out_ref):
        for half in range(SPLIT):
            rows = pl.ds(half * (tm // SPLIT), tm // SPLIT)
            _, h = _norm_mod(x_ref[rows, :], nw_ref[...], sc_ref[...], sh_ref[...])
            hb = h.astype(BF16)
            h_ref[rows, :] = hb
            out_ref[rows, :] = _nt(hb, w_ref[...]).astype(BF16)

    row = lambda w: pl.BlockSpec((tm, w), lambda i: (i, 0))
    vec = _const_spec((1, D))
    return pl.pallas_call(
        body, name=name, grid=(S // tm,), out_shape=(jax.ShapeDtypeStruct((S, D), BF16), jax.ShapeDtypeStruct((S, N), BF16)),
        in_specs=[row(D), vec, vec, vec, _const_spec((N, D))], out_specs=(row(D), row(N)),
        compiler_params=_params(("parallel",), VMEM_LIMIT),
    )(x, nw, sc, sh, wt)


def _ffn_out(a, w, x, g, target, fnw, *, name):
    S, K = a.shape
    tm = 512
    last = target is not None

    def body(a_ref, w_ref, x_ref, g_ref, *rest):
        y = _nn(a_ref[...], w_ref[...])
        xv = x_ref[...] + g_ref[...] * y
        if not last:
            xo_ref, y_ref = rest
            y_ref[...] = y.astype(BF16)
            xo_ref[...] = xv
            return
        t_ref, fw_ref, dx_ref, y_ref, st_ref = rest
        y_ref[...] = y.astype(BF16)

        @pl.when(pl.program_id(0) == 0)
        def _():
            st_ref[...] = jnp.zeros_like(st_ref)

        rstd = lax.rsqrt(jnp.mean(xv * xv, axis=-1, keepdims=True) + RMS_EPS)
        xh = xv * rstd
        err = xh * fw_ref[...] - t_ref[...]
        dy = err * (1.0 / D)
        dxh = dy * fw_ref[...]
        dx_ref[...] = rstd * (dxh - xh * jnp.mean(dxh * xh, axis=-1, keepdims=True))
        st_ref[pl.ds(0, 1), :] = st_ref[pl.ds(0, 1), :] + jnp.sum(dy * xh, axis=0, keepdims=True)
        st_ref[pl.ds(1, 1), :] = st_ref[pl.ds(1, 1), :] + jnp.sum(err * err, axis=0, keepdims=True)

    row = lambda w_: pl.BlockSpec((tm, w_), lambda i: (i, 0))
    in_specs = [row(K), _const_spec(w.shape), row(D), _const_spec((1, D))]
    args = [a, w, x, g]
    out_shape = [jax.ShapeDtypeStruct((S, D), F32), jax.ShapeDtypeStruct((S, D), BF16)]
    out_specs = [row(D), row(D)]
    if last:
        in_specs += [row(D), _const_spec((1, D))]
        args += [target, fnw]
        out_shape.append(jax.ShapeDtypeStruct((8, D), F32))
        out_specs.append(_const_spec((8, D)))
    return pl.pallas_call(
        body, name=name, grid=(S // tm,), out_shape=tuple(out_shape), in_specs=in_specs, out_specs=tuple(out_specs),
        compiler_params=_params(("arbitrary",) if last else ("parallel",), VMEM_LIMIT),
    )(*args)


CHUNK = 1024


def _tile_rows(r, chunk=CHUNK):
    return min(TQ, chunk // r)


def _out_ffn_in(a, w_mix, x, g, nw, sc, sh, wt, *, w_is_transposed, name):
    S, K = a.shape
    tm = 256

    def body(a_ref, wm_ref, x_ref, g_ref, nw_ref, sc_ref, sh_ref, w_ref, x1_ref, y_ref, h_ref, gu_ref, act_ref):
        y = _nt(a_ref[...], wm_ref[...]) if w_is_transposed else _nn(a_ref[...], wm_ref[...])
        y_ref[...] = y.astype(BF16)
        x1 = x_ref[...] + g_ref[...] * y
        x1_ref[...] = x1
        _, h = _norm_mod(x1, nw_ref[...], sc_ref[...], sh_ref[...])
        hb = h.astype(BF16)
        h_ref[...] = hb
        gate = _nt(hb, w_ref[pl.ds(0, D_FF), :])
        up = _nt(hb, w_ref[pl.ds(D_FF, D_FF), :])
        sig = jax.nn.sigmoid(gate)
        silu = gate * sig
        gu_ref[:, pl.ds(0, D_FF)] = (up * (sig * (1.0 + gate * (1.0 - sig)))).astype(BF16)
        gu_ref[:, pl.ds(D_FF, D_FF)] = silu.astype(BF16)
        act_ref[...] = (silu * up).astype(BF16)

    row = lambda w_: pl.BlockSpec((tm, w_), lambda i: (i, 0))
    vec = _const_spec((1, D))
    return pl.pallas_call(
        body, name=name, grid=(S // tm,),
        out_shape=(jax.ShapeDtypeStruct((S, D), F32), jax.ShapeDtypeStruct((S, D), BF16), jax.ShapeDtypeStruct((S, D), BF16),
                   jax.ShapeDtypeStruct((S, 2 * D_FF), BF16), jax.ShapeDtypeStruct((S, D_FF), BF16)),
        in_specs=[row(K), _const_spec(w_mix.shape), row(D), vec, vec, vec, vec, _const_spec(wt.shape)],
        out_specs=(row(D), row(D), row(D), row(2 * D_FF), row(D_FF)),
        compiler_params=_params(("parallel",), VMEM_LIMIT),
    )(a, w_mix, x, g, nw, sc, sh, wt)


def _alibi_bias(slopes, half, dil, chunk=CHUNK, both=False):
    tq = _tile_rows(dil, chunk)
    tk = tq + 2 * half
    rel = np.arange(tk)[:, None] - half - np.arange(tq)[None, :]
    band = np.abs(rel) <= half
    dist = (dil * np.abs(rel)).astype(np.float32)
    tabs = [np.where(band, -np.float32(s) * dist, np.float32(NEG)).astype(np.float32) for s in slopes]
    out = []
    for u in range(0, len(tabs), 8):
        tab = np.concatenate(tabs[u:u + 8], axis=1)
        first, last = tab.copy(), tab.copy()
        first[:half] = NEG
        last[tk - half:] = NEG
        out += [tab, first, last]
        if both:
            last = last.copy()
            last[:half] = NEG
            out.append(last)
    return jnp.asarray(np.concatenate(out, axis=0))


def _slopes(n):
    return (2.0 ** (-8.0 * np.arange(1, n + 1) / n)).astype(np.float32)


def _head_masks(tq):
    lane = lax.broadcasted_iota(jnp.int32, (tq, LANES), 1)
    lo = lane < HEAD_DIM
    return lo, jnp.logical_not(lo)


def _stack_heads(tiles, lo, hi, scale):
    blocks = []
    for t in range(4):
        xf = tiles[t] if scale == 1.0 else tiles[t] * scale
        for a in range(2):
            xm = jnp.where(lo if a == 0 else hi, xf, 0.0)
            if a != t // 2:
                xm = pltpu.roll(xm, HEAD_DIM, 1)
            blocks.append(xm.astype(BF16))
    return jnp.concatenate(blocks, axis=0)


def _tile_from_columns(x8t, t, tq):
    r0 = HEAD_DIM * (t // 2)
    top = x8t[r0:r0 + HEAD_DIM, 2 * t * tq:(2 * t + 1) * tq]
    bot = x8t[r0:r0 + HEAD_DIM, (2 * t + 1) * tq:(2 * t + 2) * tq]
    return jnp.concatenate([top, bot], axis=0).T


def _attn_layout(S, C, r, half, qoff, koff, voff, chunk):
    hb = half * r
    per = chunk // hb
    nhb = S // hb
    main = lambda off: pl.BlockSpec((chunk, LANES), lambda u, i: (i, off // LANES + u))
    prev = lambda off: pl.BlockSpec((hb, LANES), lambda u, i: (jnp.maximum(i * per - 1, 0), off // LANES + u))
    nxt = lambda off: pl.BlockSpec((hb, LANES), lambda u, i: (jnp.minimum((i + 1) * per, nhb - 1), off // LANES + u))
    specs = [pl.BlockSpec((chunk, 4 * LANES), lambda u, i: (i, qoff // (4 * LANES) + u))]
    specs += [prev(koff), main(koff), nxt(koff), prev(voff), main(voff), nxt(voff)]
    return specs, hb


def _stage(dst, srcs):
    row = 0
    for src in srcs:
        n = src.shape[0]
        dst[pl.ds(row, n), :] = src[...].astype(F32)
        row += n


def _rows(start, n, r):
    return pl.ds(start, n, stride=r) if r > 1 else pl.ds(start, n)


def _attn_fwd(qkv, bias, sink, *, C, r, half, qoff, koff, voff, n_units, out_dtype, name, prev=None):
    S = qkv.shape[0]
    chunk = max(CHUNK, TQ * r)
    tq = _tile_rows(r, chunk)
    tk = tq + 2 * half
    tiles = chunk // (r * tq)
    nsteps = S // chunk
    specs, hb = _attn_layout(S, C, r, half, qoff, koff, voff, chunk)
    use_sink = sink is not None

    def body(*refs):
        q_ref, kp, km, kn, vp, vm, vn, bias_ref = refs[:8]
        rest = list(refs[8:])
        sink_ref = rest.pop(0) if use_sink else None
        op_ref, lp_ref = (rest.pop(0), rest.pop(0)) if prev is not None else (None, None)
        o_ref, lse_ref, qs, ks, vs, os_, ls = rest
        i = pl.program_id(1)

        def merged(o_new, l_new, rows, cols):
            if prev is None:
                return o_new, l_new
            o_old, l_old = op_ref[rows, cols], lp_ref[rows, cols]
            mx = jnp.maximum(l_new, l_old)
            e_new, e_old = jnp.exp(l_new - mx), jnp.exp(l_old - mx)
            den = e_new + e_old
            return (e_new / den) * o_new + (e_old / den) * o_old, mx + jnp.log(den)

        if r > 1:
            for t in range(4):
                qs[t] = q_ref[:, pl.ds(t * LANES, LANES)].astype(F32)
        _stage(ks, [kp, km, kn])
        _stage(vs, [vp, vm, vn])
        lo, hi = _head_masks(tq)

        def tile_in(staged, ref, t, start):
            if r > 1:
                return staged[t, _rows(start, tq, r), :]
            return ref[pl.ds(start, tq), pl.ds(t * LANES, LANES)].astype(F32)

        ones = jnp.ones((16, tk), BF16)
        if use_sink:
            sk = sink_ref[pl.ds(0, 1), :]

        def chain(n, carry):
            rho, c = n // tiles, n % tiles
            start = c * (tq * r) + rho
            if r == 1:
                start = pl.multiple_of(start, tq)
            variant = jnp.where(jnp.logical_and(i == 0, c == 0), 1, 0) + jnp.where(
                jnp.logical_and(i == nsteps - 1, c == tiles - 1), 2, 0)
            k2 = ks[_rows(start, tk, r), :].astype(BF16)
            v2t = jnp.concatenate([vs[_rows(start, tk, r), :].T.astype(BF16), ones], axis=0)
            q8 = _stack_heads([tile_in(qs, q_ref, t, start) for t in range(4)], lo, hi, HEAD_DIM ** -0.5)
            s = _nt(k2, q8) + bias_ref[pl.ds(pl.multiple_of(variant * tk, 8), tk), :]
            m = jnp.max(s, axis=0, keepdims=True)
            if use_sink:
                m = jnp.maximum(m, sk)
            pv = _nn(v2t, jnp.exp(s - m).astype(BF16))
            l = pv[LANES:LANES + 1]
            if use_sink:
                l = l + jnp.exp(sk - m)
            o8t = pv[:LANES] / l
            lse8 = jnp.broadcast_to(m + jnp.log(l), (LANES, 8 * tq))
            for t in range(4):
                if r > 1:
                    os_[t, _rows(start, tq, r), :] = _tile_from_columns(o8t, t, tq)
                    ls[t, _rows(start, tq, r), :] = _tile_from_columns(lse8, t, tq)
                else:
                    rows, cols = pl.ds(start, tq), pl.ds(t * LANES, LANES)
                    o_t, l_t = merged(_tile_from_columns(o8t, t, tq), _tile_from_columns(lse8, t, tq), rows, cols)
                    o_ref[rows, cols] = o_t.astype(out_dtype)
                    lse_ref[rows, cols] = l_t
            return carry

        lax.fori_loop(0, r * tiles, chain, 0, unroll=4)
        if r > 1:
            for t in range(4):
                cols = pl.ds(t * LANES, LANES)
                o_t, l_t = merged(os_[t], ls[t], slice(None), cols)
                o_ref[:, cols] = o_t.astype(out_dtype)
                lse_ref[:, cols] = l_t

    in_specs = specs + [pl.BlockSpec((bias.shape[0] // n_units, 8 * tq), lambda u, i: (u, 0))]
    args = [qkv] * 7 + [bias]
    if use_sink:
        in_specs.append(pl.BlockSpec((8, 8 * tq), lambda u, i: (u, 0)))
        args.append(sink)
    wide = pl.BlockSpec((chunk, 4 * LANES), lambda u, i: (i, u))
    if prev is not None:
        in_specs += [wide, wide]
        args += list(prev)
    win = hb + chunk + hb
    big = lambda: pltpu.VMEM((4, chunk if r > 1 else 8, LANES), F32)
    return pl.pallas_call(
        body, name=name, grid=(n_units, nsteps),
        out_shape=(jax.ShapeDtypeStruct((S, n_units * 512), out_dtype), jax.ShapeDtypeStruct((S, n_units * 512), F32)),
        in_specs=in_specs, out_specs=(wide, wide),
        scratch_shapes=[big(), pltpu.VMEM((win, LANES), F32), pltpu.VMEM((win, LANES), F32), big(), big()],
        compiler_params=_params(("parallel", "parallel"), VMEM_LIMIT),
    )(*args)


def _attn_bwd(qkv, bias, sink, o, do, lse, *, C, r, half, qoff, koff, voff, n_units, name):
    S = qkv.shape[0]
    tq = _tile_rows(r)
    tk = tq + 2 * half
    tiles = CHUNK // (r * tq)
    nsteps = S // CHUNK
    specs, hb = _attn_layout(S, C, r, half, qoff, koff, voff, CHUNK)
    use_sink = sink is not None

    def body(*refs):
        q_ref, kp, km, kn, vp, vm, vn, bias_ref = refs[:8]
        rest = list(refs[8:])
        sink_ref = rest.pop(0) if use_sink else None
        o_ref, do_ref, lse_ref, dq_ref, dk_hbm, dv_hbm = rest[:6]
        rest = rest[6:]
        dsink_ref = rest.pop(0) if use_sink else None
        qs, ks, vs, os_, dos, ls, dqs, acck, accv, sem = rest
        u, i = pl.program_id(0), pl.program_id(1)

        @pl.when(i == 0)
        def _():
            acck[...] = jnp.zeros_like(acck)
            accv[...] = jnp.zeros_like(accv)
            if use_sink:
                dsink_ref[...] = jnp.zeros_like(dsink_ref)

        if r > 1:
            for t in range(4):
                cols = pl.ds(t * LANES, LANES)
                qs[t] = q_ref[:, cols].astype(F32)
                os_[t] = o_ref[:, cols].astype(F32)
                dos[t] = do_ref[:, cols].astype(F32)
                ls[t] = lse_ref[:, cols]
        _stage(ks, [kp, km, kn])
        _stage(vs, [vp, vm, vn])
        lo, hi = _head_masks(tq)

        def tile_in(staged, ref, t, start):
            if r > 1:
                return staged[t, _rows(start, tq, r), :]
            return ref[pl.ds(start, tq), pl.ds(t * LANES, LANES)].astype(F32)

        base = pl.multiple_of(i * CHUNK, CHUNK)
        if use_sink:
            sk = sink_ref[pl.ds(0, 1), :]

        def chain(n, carry):
            rho, c = n // tiles, n % tiles
            start = c * (tq * r) + rho
            if r == 1:
                start = pl.multiple_of(start, tq)
            variant = jnp.where(jnp.logical_and(i == 0, c == 0), 1, 0) + jnp.where(
                jnp.logical_and(i == nsteps - 1, c == tiles - 1), 2, 0)
            k2 = ks[_rows(start, tk, r), :].astype(BF16)
            v2 = vs[_rows(start, tk, r), :].astype(BF16)
            k2t = ks[_rows(start, tk, r), :].T.astype(BF16)
            q8 = _stack_heads([tile_in(qs, q_ref, t, start) for t in range(4)], lo, hi, HEAD_DIM ** -0.5)
            do_tiles = [tile_in(dos, do_ref, t, start) for t in range(4)]
            do8 = _stack_heads(do_tiles, lo, hi, 1.0)
            deltas, lses = [], []
            for t in range(4):
                prod_t = (do_tiles[t] * tile_in(os_, o_ref, t, start)).T
                lse_t = tile_in(ls, lse_ref, t, start).T
                for a in range(2):
                    deltas.append(jnp.sum(prod_t[a * HEAD_DIM:(a + 1) * HEAD_DIM], axis=0, keepdims=True))
                    lses.append(lse_t[a * HEAD_DIM:a * HEAD_DIM + 1])
            delta8 = jnp.concatenate(deltas, axis=1)
            lse8 = jnp.concatenate(lses, axis=1)
            s = _nt(k2, q8) + bias_ref[pl.ds(pl.multiple_of(variant * tk, 8), tk), :]
            p = jnp.exp(s - lse8)
            dp = _nt(v2, do8)
            dsb = (p * (dp - delta8)).astype(BF16)
            dq8t = _nn(k2t, dsb)
            for t in range(4):
                dq_t = _tile_from_columns(dq8t, t, tq) * (HEAD_DIM ** -0.5)
                if r > 1:
                    dqs[t, _rows(start, tq, r), :] = dq_t
                else:
                    dq_ref[pl.ds(start, tq), pl.ds(t * LANES, LANES)] = dq_t.astype(BF16)
            arow = base + start
            if r == 1:
                arow = pl.multiple_of(arow, tq)
            acck[_rows(arow, tk, r), :] = acck[_rows(arow, tk, r), :] + _nn(dsb, q8)
            accv[_rows(arow, tk, r), :] = accv[_rows(arow, tk, r), :] + _nn(p.astype(BF16), do8)
            if use_sink:
                e = jnp.exp(sk - lse8) * delta8
                for h in range(8):
                    part = -jnp.sum(e[:, h * tq:(h + 1) * tq], axis=1, keepdims=True)
                    dsink_ref[pl.ds(h, 1), :] = dsink_ref[pl.ds(h, 1), :] + part
            return carry

        lax.fori_loop(0, r * tiles, chain, 0, unroll=2)
        if r > 1:
            for t in range(4):
                dq_ref[:, pl.ds(t * LANES, LANES)] = dqs[t].astype(BF16)

        @pl.when(i == nsteps - 1)
        def _():
            ck = pltpu.make_async_copy(acck.at[pl.ds(hb, S)], dk_hbm.at[u], sem.at[0])
            cv = pltpu.make_async_copy(accv.at[pl.ds(hb, S)], dv_hbm.at[u], sem.at[1])
            ck.start()
            cv.start()
            ck.wait()
            cv.wait()

    wide = pl.BlockSpec((CHUNK, 4 * LANES), lambda u, i: (i, u))
    hbm = pl.BlockSpec(memory_space=pl.ANY)
    in_specs = specs + [pl.BlockSpec((3 * tk, 8 * tq), lambda u, i: (u, 0))]
    args = [qkv] * 7 + [bias]
    if use_sink:
        in_specs.append(pl.BlockSpec((8, 8 * tq), lambda u, i: (u, 0)))
        args.append(sink)
    in_specs += [wide, wide, wide]
    args += [o, do, lse]
    out_shape = [jax.ShapeDtypeStruct((S, n_units * 512), BF16), jax.ShapeDtypeStruct((n_units, S, LANES), F32),
                 jax.ShapeDtypeStruct((n_units, S, LANES), F32)]
    out_specs = [wide, hbm, hbm]
    if use_sink:
        out_shape.append(jax.ShapeDtypeStruct((n_units * 8, LANES), F32))
        out_specs.append(pl.BlockSpec((8, LANES), lambda u, i: (u, 0)))
    win = hb + CHUNK + hb
    big = lambda: pltpu.VMEM((4, CHUNK if r > 1 else 8, LANES), F32)
    res = pl.pallas_call(
        body, name=name, grid=(n_units, nsteps), out_shape=tuple(out_shape), in_specs=in_specs, out_specs=tuple(out_specs),
        scratch_shapes=[big(), pltpu.VMEM((win, LANES), F32), pltpu.VMEM((win, LANES), F32), big(), big(), big(), big(),
                        pltpu.VMEM((S + 2 * hb, LANES), F32), pltpu.VMEM((S + 2 * hb, LANES), F32), pltpu.SemaphoreType.DMA((2,))],
        compiler_params=_params(("arbitrary", "arbitrary"), VMEM_LIMIT),
    )(*args)
    return res[0], res[1], res[2], (res[3] if use_sink else None)


def _columns(part_refs, rows):
    return jnp.concatenate([r[rows, :].astype(BF16) for r in part_refs], axis=1)


def _piece_specs(parts, rows, row_axis):
    arrays, specs = [], []
    for p in parts:
        if isinstance(p, tuple):
            arr, u = p
            index = (lambda *g, u=u: (u, g[row_axis], 0))
            arrays.append(arr)
            specs.append(pl.BlockSpec((None, rows, arr.shape[2]), index))
        else:
            arrays.append(p)
            specs.append(pl.BlockSpec((rows, p.shape[1]), lambda *g: (g[row_axis], 0)))
    return arrays, specs


def _norm_bwd(dy_parts, wt, x, dres, nw, sc, h, *, name):
    S, N = x.shape[0], wt.shape[0]
    npart = len(dy_parts)
    tm = 512
    part_arrays, part_specs = _piece_specs(dy_parts, tm, 0)

    def body(*refs):
        part_refs = refs[:npart]
        w_ref, x_ref, dres_ref, nw_ref, sc_ref, h_ref, dx_ref, st_ref, dw_ref, acc = refs[npart:]
        k = pl.program_id(0)

        @pl.when(k == 0)
        def _():
            st_ref[...] = jnp.zeros_like(st_ref)
            acc[...] = jnp.zeros_like(acc)

        dy = _columns(part_refs, slice(None))
        acc[...] += _tn(h_ref[...], dy)
        nwv, scale = nw_ref[...], 1.0 + sc_ref[...]
        sums = [jnp.zeros((1, D), F32)] * 3
        for half in range(SPLIT):
            rows = pl.ds(half * (tm // SPLIT), tm // SPLIT)
            dh = _nn(dy[half * (tm // SPLIT):(half + 1) * (tm // SPLIT)], w_ref[...])
            xv = x_ref[rows, :]
            rstd = lax.rsqrt(jnp.mean(xv * xv, axis=-1, keepdims=True) + RMS_EPS)
            xh = xv * rstd
            dxh = dh * (nwv * scale)
            dx_ref[rows, :] = dres_ref[rows, :] + rstd * (dxh - xh * jnp.mean(dxh * xh, axis=-1, keepdims=True))
            dhx = dh * xh
            sums = [sums[0] + jnp.sum(dh, axis=0, keepdims=True), sums[1] + jnp.sum(dhx * nwv, axis=0, keepdims=True),
                    sums[2] + jnp.sum(dhx * scale, axis=0, keepdims=True)]
        for q in range(3):
            st_ref[pl.ds(q, 1), :] = st_ref[pl.ds(q, 1), :] + sums[q]

        @pl.when(k == S // tm - 1)
        def _():
            dw_ref[...] = acc[...].T.astype(BF16)

    row = lambda w_: pl.BlockSpec((tm, w_), lambda i: (i, 0))
    vec = _const_spec((1, D))
    return pl.pallas_call(
        body, name=name, grid=(S // tm,),
        out_shape=(jax.ShapeDtypeStruct((S, D), F32), jax.ShapeDtypeStruct((8, D), F32), jax.ShapeDtypeStruct((N, D), BF16)),
        in_specs=part_specs + [_const_spec((N, D)), row(D), row(D), vec, vec, row(D)],
        out_specs=(row(D), _const_spec((8, D)), _const_spec((N, D))), scratch_shapes=[pltpu.VMEM((D, N), F32)],
        compiler_params=_params(("arbitrary",), VMEM_LIMIT),
    )(*part_arrays, wt, x, dres, nw, sc, h)


def _ffn_bwd(dx, y, g, w_out, gu, wt_in, x, nw, sc, y1, g1, w_mix, *, mix_is_transposed, name):
    S = dx.shape[0]
    K = w_out.shape[0]
    Km = w_mix.shape[1] if mix_is_transposed else w_mix.shape[0]
    tm = 256

    def body(dx_ref, y_ref, g_ref, wo_ref, gu_ref, wi_ref, x_ref, nw_ref, sc_ref, y1_ref, g1_ref, wm_ref,
             dgu_ref, dyb_ref, dxo_ref, da_ref, dy1_ref, stg_ref, stf_ref, stm_ref):
        @pl.when(pl.program_id(0) == 0)
        def _():
            stg_ref[...] = jnp.zeros_like(stg_ref)
            stf_ref[...] = jnp.zeros_like(stf_ref)
            stm_ref[...] = jnp.zeros_like(stm_ref)

        dxv = dx_ref[...]
        stg_ref[pl.ds(0, 1), :] = stg_ref[pl.ds(0, 1), :] + jnp.sum(dxv * y_ref[...].astype(F32), axis=0, keepdims=True)
        dyb = (dxv * g_ref[...]).astype(BF16)
        dyb_ref[...] = dyb
        da = _nt(dyb, wo_ref[...])
        dgate = (da * gu_ref[:, pl.ds(0, K)].astype(F32)).astype(BF16)
        dup = (da * gu_ref[:, pl.ds(K, K)].astype(F32)).astype(BF16)
        dgu_ref[:, pl.ds(0, K)] = dgate
        dgu_ref[:, pl.ds(K, K)] = dup
        dh = _nn(dgate, wi_ref[pl.ds(0, K), :]) + _nn(dup, wi_ref[pl.ds(K, K), :])
        xv = x_ref[...]
        rstd = lax.rsqrt(jnp.mean(xv * xv, axis=-1, keepdims=True) + RMS_EPS)
        xh = xv * rstd
        nwv, scale = nw_ref[...], 1.0 + sc_ref[...]
        dxh = dh * (nwv * scale)
        dx1 = dxv + rstd * (dxh - xh * jnp.mean(dxh * xh, axis=-1, keepdims=True))
        dxo_ref[...] = dx1
        dhx = dh * xh
        stf_ref[pl.ds(0, 1), :] = stf_ref[pl.ds(0, 1), :] + jnp.sum(dh, axis=0, keepdims=True)
        stf_ref[pl.ds(1, 1), :] = stf_ref[pl.ds(1, 1), :] + jnp.sum(dhx * nwv, axis=0, keepdims=True)
        stf_ref[pl.ds(2, 1), :] = stf_ref[pl.ds(2, 1), :] + jnp.sum(dhx * scale, axis=0, keepdims=True)
        stm_ref[pl.ds(0, 1), :] = stm_ref[pl.ds(0, 1), :] + jnp.sum(dx1 * y1_ref[...].astype(F32), axis=0, keepdims=True)
        dy1 = (dx1 * g1_ref[...]).astype(BF16)
        dy1_ref[...] = dy1
        da_ref[...] = (_nn(dy1, wm_ref[...]) if mix_is_transposed else _nt(dy1, wm_ref[...])).astype(BF16)

    row = lambda w_: pl.BlockSpec((tm, w_), lambda i: (i, 0))
    vec = _const_spec((1, D))
    st = jax.ShapeDtypeStruct((8, D), F32)
    act = lambda w_: jax.ShapeDtypeStruct((S, w_), BF16)
    return pl.pallas_call(
        body, name=name, grid=(S // tm,),
        out_shape=(act(2 * K), act(D), jax.ShapeDtypeStruct((S, D), F32), act(Km), act(D), st, st, st),
        in_specs=[row(D), row(D), vec, _const_spec(w_out.shape), row(2 * K), _const_spec(wt_in.shape), row(D), vec, vec,
                  row(D), vec, _const_spec(w_mix.shape)],
        out_specs=(row(2 * K), row(D), row(D), row(Km), row(D), _const_spec((8, D)), _const_spec((8, D)), _const_spec((8, D))),
        compiler_params=_params(("arbitrary",), VMEM_LIMIT),
    )(dx, y, g, w_out, gu, wt_in, x, nw, sc, y1, g1, w_mix)


def _weight_grad(a, b_parts, *, transpose_out, name):
    S = a.shape[0]
    N = sum(p[0].shape[2] if isinstance(p, tuple) else p.shape[1] for p in b_parts)
    npart = len(b_parts)
    nb = N // 2 if N > 4096 else N
    assert nb == N or npart == 1
    tk = 512
    if npart == 1:
        part_arrays, part_specs = list(b_parts), [pl.BlockSpec((tk, nb), lambda n, k: (k, n))]
    else:
        part_arrays, part_specs = _piece_specs(b_parts, tk, 1)

    def body(*refs):
        a_ref, part_refs = refs[0], refs[1:1 + npart]
        out_ref, acc = refs[1 + npart:]
        k = pl.program_id(1)

        @pl.when(k == 0)
        def _():
            acc[...] = jnp.zeros_like(acc)

        acc[...] += _tn(a_ref[...], _columns(part_refs, slice(None)))

        @pl.when(k == pl.num_programs(1) - 1)
        def _():
            out_ref[...] = (acc[...].T if transpose_out else acc[...]).astype(BF16)

    out_block = pl.BlockSpec((nb, D), lambda n, k: (n, 0)) if transpose_out else pl.BlockSpec((D, nb), lambda n, k: (0, n))
    return pl.pallas_call(
        body, name=name, grid=(N // nb, S // tk),
        out_shape=jax.ShapeDtypeStruct((N, D) if transpose_out else (D, N), BF16),
        in_specs=[pl.BlockSpec((tk, D), lambda n, k: (k, 0))] + part_specs,
        out_specs=out_block, scratch_shapes=[pltpu.VMEM((D, nb), F32)],
        compiler_params=_params(("parallel", "arbitrary"), VMEM_LIMIT),
    )(a, *part_arrays)


def _adamw(w, g, m, v):
    m = ADAM_B1 * m + (1.0 - ADAM_B1) * g
    v = ADAM_B2 * v + (1.0 - ADAM_B2) * (g * g)
    m_hat = m / (1.0 - ADAM_B1 ** ADAM_STEP)
    v_hat = v / (1.0 - ADAM_B2 ** ADAM_STEP)
    delta = -ADAM_LR * (m_hat / (jnp.sqrt(v_hat) + ADAM_EPS) + ADAM_WD * w)
    return delta, m, v


def _adam_segment(parts, own, w, m, v, outs, layer, *, name):
    R, C = own.shape
    nc = 2 if R % 32 == 0 else 1
    ch = R // nc

    def body(*refs):
        p_hbm, o_hbm, w_hbm, m_hbm, v_hbm = refs[:5]
        res_hbm = refs[-14:-10]
        p_buf, o_buf, w_buf, m_buf, v_buf, g_buf, d_buf, mo_buf, vo_buf, sems = refs[-10:]
        ins = []
        for c in range(nc):
            rows = pl.ds(c * ch, ch)
            pairs = [(p_hbm.at[:, rows], p_buf.at[:, rows]), (o_hbm.at[rows], o_buf.at[rows]), (w_hbm.at[layer, rows], w_buf.at[rows]),
                     (m_hbm.at[layer, rows], m_buf.at[rows]), (v_hbm.at[layer, rows], v_buf.at[rows])]
            copies = [pltpu.make_async_copy(src, dst, sems.at[c * 9 + q]) for q, (src, dst) in enumerate(pairs)]
            for cp in copies:
                cp.start()
            ins.append(copies)
        me = _my_index()
        outs_started = []
        for c in range(nc):
            rows = pl.ds(c * ch, ch)
            for cp in ins[c]:
                cp.wait()
            g = jnp.zeros((ch, C), F32)
            for j in range(N_DEV):
                g = g + jnp.where(me == j, o_buf[rows, :], p_buf[j, rows, :]).astype(F32)
            delta, mn, vn = _adamw(w_buf[rows, :], g, m_buf[rows, :], v_buf[rows, :])
            for q, (buf, val) in enumerate(((g_buf, g), (d_buf, delta), (mo_buf, mn), (vo_buf, vn))):
                buf[rows, :] = val
                cp = pltpu.make_async_copy(buf.at[rows], res_hbm[q].at[layer, rows], sems.at[c * 9 + 5 + q])
                cp.start()
                outs_started.append(cp)
        for cp in outs_started:
            cp.wait()

    hbm = pl.BlockSpec(memory_space=pl.ANY)
    passed = [] if outs is None else [hbm] * 4
    shp = jax.ShapeDtypeStruct(w.shape, F32)
    f32_buf = lambda: pltpu.VMEM((R, C), F32)
    return pl.pallas_call(
        body, name=name, out_shape=(shp,) * 4, in_specs=[hbm] * 5 + passed, out_specs=(hbm,) * 4,
        input_output_aliases={5 + q: q for q in range(len(passed))},
        scratch_shapes=[pltpu.VMEM((N_DEV, R, C), BF16), pltpu.VMEM((R, C), BF16)] + [f32_buf() for _ in range(7)]
        + [pltpu.SemaphoreType.DMA((nc * 9,))],
        compiler_params=_params(vmem=VMEM_LIMIT),
    )(parts, own, w, m, v, *(outs or ()))


def _adam_ada_w(cond_t, dmod, w, m, v):
    ncol = w.shape[-1]
    tr = 512

    def body(c_ref, d_ref, w_ref, m_ref, v_ref, g_out, d_out, m_out, v_out):
        g = _nn(c_ref[...], d_ref[0])
        delta, mn, vn = _adamw(w_ref[0], g, m_ref[0], v_ref[0])
        g_out[0] = g
        d_out[0] = delta
        m_out[0] = mn
        v_out[0] = vn

    blk = pl.BlockSpec((1, tr, ncol), lambda l, i: (l, i, 0))
    shp = jax.ShapeDtypeStruct(w.shape, F32)
    return pl.pallas_call(
        body, name="adam_ada_w", grid=(DEPTH, D // tr), out_shape=(shp,) * 4,
        in_specs=[pl.BlockSpec((tr, LANES), lambda l, i: (i, 0)), pl.BlockSpec((1, LANES, ncol), lambda l, i: (l, 0, 0)), blk, blk, blk],
        out_specs=(blk,) * 4, compiler_params=_params(("parallel", "parallel"), VMEM_LIMIT),
    )(cond_t, dmod, w, m, v)


TILE_ROWS = 168


def _stat_sources():
    pairs = []
    for i in range(DEPTH):
        b = 32 * i
        for q, src in enumerate((b, b + 1, b + 8, b + 16, b + 17, b + 24)):
            pairs.append((6 * i + q, src))
        pairs.append((24 + i, b + 2))
        pairs.append((32 + i, b + 18))
    pairs += [(40, 128), (41, 129)]
    return pairs


def _small_exchange(tiles, w, m, v):
    loss_row, sink_row, sink_src = 41, 48, 136

    def body(s_ref, w_ref, m_ref, v_ref, dmod_out, g_out, d_out, m_out, v_out, loss_out, all_ref, tot_ref, send_sems, recv_sems):
        me = _my_index()
        all_ref[me] = s_ref[...]
        copies = []
        for k in range(1, N_DEV):
            dev, _ = _peer(k)
            cp = pltpu.make_async_remote_copy(src_ref=s_ref, dst_ref=all_ref.at[me], send_sem=send_sems.at[k - 1],
                                              recv_sem=recv_sems.at[k - 1], device_id=dev, device_id_type=MESH)
            cp.start()
            copies.append(cp)
        for k in range(1, N_DEV):
            dev, pidx = _peer(k)
            pltpu.make_async_remote_copy(src_ref=s_ref, dst_ref=all_ref.at[pidx], send_sem=send_sems.at[k - 1],
                                         recv_sem=recv_sems.at[k - 1], device_id=dev, device_id_type=MESH).wait_recv()
        for cp in copies:
            cp.wait_send()
        tot = all_ref[0]
        for j in range(1, N_DEV):
            tot = tot + all_ref[j]
        tot_ref[...] = tot
        g_out[...] = jnp.zeros_like(g_out)
        for dst, src in _stat_sources():
            g_out[pl.ds(dst, 1), :] = tot_ref[pl.ds(src, 1), :]
            if dst < 24:
                for j in range(N_DEV):
                    dmod_out[j, pl.ds(dst, 1), :] = all_ref[j, pl.ds(src, 1), :]
        lane = lax.broadcasted_iota(jnp.int32, (1, D), 1)
        sink = jnp.zeros((1, D), F32)
        for h in range(32):
            sink = jnp.where(lane == h, tot_ref[pl.ds(sink_src + h, 1), :], sink)
        g_out[pl.ds(sink_row, 1), :] = sink
        g = g_out[...]
        delta, mn, vn = _adamw(w_ref[...], g, m_ref[...], v_ref[...])
        d_out[...] = delta
        m_out[...] = mn
        v_out[...] = vn
        loss = jnp.sum(g[loss_row:loss_row + 1, :], axis=-1, keepdims=True) * (0.5 / D)
        loss_out[...] = jnp.broadcast_to(loss, loss_out.shape)

    vm = pl.BlockSpec(memory_space=pltpu.VMEM)
    shp = jax.ShapeDtypeStruct((STAT_ROWS, D), F32)
    return pl.pallas_call(
        body, name="small_exchange",
        out_shape=(jax.ShapeDtypeStruct((N_DEV, 24, D), F32), shp, shp, shp, shp, jax.ShapeDtypeStruct((8, LANES), F32)),
        in_specs=[vm] * 4, out_specs=(vm,) * 6,
        scratch_shapes=[pltpu.VMEM((N_DEV, TILE_ROWS, D), F32), pltpu.VMEM((TILE_ROWS, D), F32),
                        pltpu.SemaphoreType.DMA((N_DEV - 1,)), pltpu.SemaphoreType.DMA((N_DEV - 1,))],
        compiler_params=_params(vmem=VMEM_LIMIT),
    )(tiles, w, m, v)


def _to_rows(name, a):
    if name in ("ffn_in", "a_in", "b_in"):
        return a.T
    if name == "b_out":
        return a.T.reshape(-1, D)
    return a


def _rows8(a):
    return jnp.pad(a, ((0, 8 - a.shape[0]), (0, 0)))


def _pack_small(ada_b, norm_mix, norm_ffn, final_norm, sink):
    sink_row = jnp.pad(sink.reshape(1, -1), ((0, 0), (0, D - sink.size)))
    return jnp.concatenate([ada_b.reshape(24, D), _rows8(norm_mix), _rows8(norm_ffn), _rows8(final_norm.reshape(1, D)),
                            _rows8(sink_row)], axis=0)


def _unpack_small(a):
    return a[0:24].reshape(4, 6 * D), a[24:28], a[32:36], a[40], a[48, :32].reshape(2, 16)


def kernel(x, c, ada_w, ada_b, norm_mix, norm_ffn, ffn_w_in, ffn_w_out, a_w_in, a_w_out, a_sink, b_w_in, b_w_out, final_norm, loss_target, m_ada_w, m_ada_b, m_norm_mix, m_norm_ffn, m_ffn_w_in, m_ffn_w_out, m_a_w_in, m_a_w_out, m_a_sink, m_b_w_in, m_b_w_out, m_final_norm, v_ada_w, v_ada_b, v_norm_mix, v_norm_ffn, v_ffn_w_in, v_ffn_w_out, v_a_w_in, v_a_w_out, v_a_sink, v_b_w_in, v_b_w_out, v_final_norm):
    S = x.shape[1]
    x0 = x.reshape(S, D)
    target = loss_target.reshape(S, D)
    me = _my_index()
    ncol = ada_w.shape[-1]

    ada_b_mine = lax.dynamic_slice_in_dim(ada_b, me * ncol, ncol, axis=1)
    cond_all, parts = _cond_exchange(jnp.broadcast_to(c.reshape(1, D), (8, D)), ada_w, ada_b_mine)
    mod = lax.dynamic_index_in_dim(parts, me, axis=2, keepdims=False)
    mod = jnp.transpose(mod, (1, 0, 2)).reshape(DEPTH, 6, 1, D)

    weights = {"ffn_in": ffn_w_in, "ffn_out": ffn_w_out, "a_in": a_w_in, "a_out": a_w_out, "b_in": b_w_in, "b_out": b_w_out}
    shard = {(n, l): _to_rows(n, weights[n][l]).astype(BF16) for n, l, _ in SEGMENTS}
    first = [sg for sg in _layer_segments(0) if not sg[0].startswith("ffn")]
    gathered0 = _all_gather_weights([shard[(n, l)] for n, l, _ in first])
    W = {(n, l): g for (n, l, _), g in zip(first, gathered0)}
    groups = [[sg for sg in _layer_segments(0) if sg[0].startswith("ffn")]] + [_layer_segments(i) for i in range(1, DEPTH)]
    gathers, order = [], gathered0[0]
    for q, segs in enumerate(groups):
        zones = [(N_DEV, rows, D) for _, _, rows in segs]
        gathers.append(_exchange_start([shard[(n, l)] for n, l, _ in segs], zones, [(s, 0) for s in range(len(segs))],
                                       [sg[2] for sg in segs], False, order, "weight_gather_start_%d" % q))
        order = gathers[-1][-1]
    gather_token = order[0:1, 0:1]

    def finish_gather(q, after):
        segs = groups[q]
        zones = _exchange_wait(gathers[q], len(segs), [(s, 0) for s in range(len(segs))], [sg[2] for sg in segs], True, after,
                               "weight_gather_wait_%d" % q)
        for (n, l, rows), zone in zip(segs, zones):
            W[(n, l)] = zone.reshape(D, 512) if n == "b_out" else zone.reshape(N_DEV * rows, D)

    a_slopes, b_slopes = _slopes(16), _slopes(24)
    bias_a = _alibi_bias(a_slopes, A_HALF, 1)
    bias_b = [_alibi_bias(b_slopes[8 * g:8 * g + 8], B_HALF, dil) for g, dil in enumerate(B_DILS)]
    bias_b_fwd = [_alibi_bias(b_slopes[8 * g:8 * g + 8], B_HALF, dil, max(CHUNK, TQ * dil), both=True) for g, dil in enumerate(B_DILS)]
    a_geom = dict(C=A_QKV, r=1, half=A_HALF, qoff=0, koff=1024, voff=1280, n_units=2)
    b_geom = [dict(C=B_QKV, r=dil, half=B_HALF, qoff=512 * g, koff=1536 + 128 * g, voff=1920 + 128 * g, n_units=1)
              for g, dil in enumerate(B_DILS)]

    saved = []
    xcur = x0
    for i in range(DEPTH):
        j = i // 2
        sh1, sc1, g1, sh2, sc2, g2 = [mod[i, q] for q in range(6)]
        nm, nf = norm_mix[i].reshape(1, D), norm_ffn[i].reshape(1, D)
        if i == 0:
            nm = nm + gather_token
        if i > 0:
            finish_gather(i, xcur)
        if i % 2 == 0:
            sink_rep = jnp.repeat(jnp.repeat(a_sink[j], TQ).reshape(2, 1, 8 * TQ), 8, axis=1).reshape(16, 8 * TQ)
            h1, qkv = _proj(xcur, nm, sc1, sh1, W[("a_in", j)], name="proj_a")
            o, lse = _attn_fwd(qkv, bias_a, sink_rep, out_dtype=BF16, name="attn_a_fwd", **a_geom)
            if i == 0:
                finish_gather(0, o)
            x1, y1, h2, gu, act = _out_ffn_in(o, W[("a_out", j)], xcur, g1, nf, sc2, sh2, W[("ffn_in", i)],
                                              w_is_transposed=False, name="out_a_ffn_in")
        else:
            sink_rep = None
            h1, qkv = _proj(xcur, nm, sc1, sh1, W[("b_in", j)], name="proj_b")
            merged = None
            for g in (2, 1, 0):
                merged = _attn_fwd(qkv, bias_b_fwd[g], None, out_dtype=BF16 if g == 0 else F32, name="attn_b%d_fwd" % g,
                                   prev=merged, **b_geom[g])
            o, lse = merged
            x1, y1, h2, gu, act = _out_ffn_in(o, W[("b_out", j)], xcur, g1, nf, sc2, sh2, W[("ffn_in", i)],
                                              w_is_transposed=True, name="out_b_ffn_in")
        if i < DEPTH - 1:
            x2, y2 = _ffn_out(act, W[("ffn_out", i)], x1, g2, None, None, name="ffn_out")
        else:
            x2, y2, head_stats = _ffn_out(act, W[("ffn_out", i)], x1, g2, target, final_norm.reshape(1, D), name="ffn_out_loss")
        saved.append(dict(x0=xcur, h1=h1, qkv=qkv, o=o, lse=lse, y1=y1, x1=x1, h2=h2, gu=gu, act=act, y2=y2, sink=sink_rep))
        xcur = x2

    dx = xcur

    dW = {}
    stat_tiles, dsink = [None] * DEPTH, [None] * 2
    exchanges = []
    start_token = None

    def start_exchange(segs):
        own = [lax.dynamic_slice_in_dim(dW[(n, l)], me * rows, rows, axis=0) for n, l, rows in segs]
        zones = [(N_DEV, rows, D) for _, _, rows in segs]
        started = _exchange_start([dW[(n, l)] for n, l, _ in segs], zones, [(s, 0) for s in range(len(segs))],
                                  [sg[2] for sg in segs], True, own[0], "grad_exchange_start_%d" % len(exchanges))
        exchanges.append((segs, started, own))
        return started[-1][0:1, 0:1]

    for i in reversed(range(DEPTH)):
        j = i // 2
        sv = saved[i]
        sh1, sc1, g1, sh2, sc2, g2 = [mod[i, q] for q in range(6)]
        if start_token is not None:
            g2 = g2 + start_token
            start_token = None
        nm, nf = norm_mix[i].reshape(1, D), norm_ffn[i].reshape(1, D)
        mix = "a_out" if i % 2 == 0 else "b_out"
        dgu, dy2, dx1, do, dy1, st_g2, st_f, st_g1 = _ffn_bwd(
            dx, sv["y2"], g2, W[("ffn_out", i)], sv["gu"], W[("ffn_in", i)], sv["x1"], nf, sc2, sv["y1"], g1, W[(mix, j)],
            mix_is_transposed=(i % 2 == 1), name="ffn_bwd_" + mix)
        dW[("ffn_out", i)] = _weight_grad(dy2, [sv["act"]], transpose_out=True, name="dw_ffn_out")
        dW[("ffn_in", i)] = _weight_grad(sv["h2"], [dgu], transpose_out=True, name="dw_ffn_in")
        sink_bwd = sv["sink"]
        if i == 0:
            sink_bwd = sink_bwd + start_exchange([sg for sg in _layer_segments(0) if sg[0].startswith("ffn")])
        if i % 2 == 0:
            dW[("a_out", j)] = _weight_grad(dy1, [sv["o"]], transpose_out=True, name="dw_a_out")
            dq, dk, dv, ds = _attn_bwd(sv["qkv"], bias_a, sink_bwd, sv["o"], do, sv["lse"], name="attn_a_bwd", **a_geom)
            dsink[j] = ds
            dqkv = [dq, (dk, 0), (dk, 1), (dv, 0), (dv, 1)]
            dx0, st_m, dW[("a_in", j)] = _norm_bwd(dqkv, W[("a_in", j)], sv["x0"], dx1, nm, sc1, sv["h1"], name="proj_a_bwd")
        else:
            dW[("b_out", j)] = _weight_grad(dy1, [sv["o"]], transpose_out=False, name="dw_b_out").reshape(N_DEV * 64, D)
            gr = [_attn_bwd(sv["qkv"], bias_b[g], None, sv["o"], do, sv["lse"], name="attn_b%d_bwd" % g, **b_geom[g]) for g in range(3)]
            dqkv = [t[0] for t in gr] + [(t[1], 0) for t in gr] + [(t[2], 0) for t in gr]
            dx0, st_m, dW[("b_in", j)] = _norm_bwd(dqkv, W[("b_in", j)], sv["x0"], dx1, nm, sc1, sv["h1"], name="proj_b_bwd")
        stat_tiles[i] = [st_m, st_g1, st_f, st_g2]
        if i > 0:
            start_token = start_exchange(_layer_segments(i))
        else:
            start_exchange([sg for sg in _layer_segments(0) if not sg[0].startswith("ffn")])
        dx = dx0
    grad_x = dx.reshape(1, S, D)

    masters = {"ffn_in": (ffn_w_in, m_ffn_w_in, v_ffn_w_in), "ffn_out": (ffn_w_out, m_ffn_w_out, v_ffn_w_out),
               "a_in": (a_w_in, m_a_w_in, v_a_w_in), "a_out": (a_w_out, m_a_w_out, v_a_w_out),
               "b_in": (b_w_in, m_b_w_in, v_b_w_in), "b_out": (b_w_out, m_b_w_out, v_b_w_out)}
    by_rows = ("ffn_in", "a_in", "b_in", "b_out")
    masters = {n: tuple(jnp.swapaxes(t, 1, 2) for t in wmv) if n in by_rows else wmv for n, wmv in masters.items()}
    results = {n: None for n in masters}
    after = dx
    for e, (segs, started, own) in enumerate(exchanges):
        zones = _exchange_wait(started, len(segs), [(s, 0) for s in range(len(segs))], [sg[2] for sg in segs], False, after,
                               "grad_exchange_wait_%d" % e)
        for (n, l, rows), zone, mine in zip(segs, zones, own):
            if n == "b_out":
                zone, mine = zone.reshape(N_DEV, LANES, 512), mine.reshape(LANES, 512)
            results[n] = _adam_segment(zone, mine, *masters[n], results[n], l, name="adam_" + n)
            after = results[n][0]
    big = {(kind, n): jnp.swapaxes(results[n][q], 1, 2) if n in by_rows else results[n][q]
           for q, kind in enumerate(("grad", "delta", "m", "v")) for n in masters}

    tiles = jnp.concatenate([t for i in range(DEPTH) for t in stat_tiles[i]] + [head_stats]
                            + [jnp.pad(ds, ((0, 0), (0, D - LANES))) for ds in dsink], axis=0)
    small = [_pack_small(*t) for t in ((ada_b, norm_mix, norm_ffn, final_norm, a_sink),
                                       (m_ada_b, m_norm_mix, m_norm_ffn, m_final_norm, m_a_sink),
                                       (v_ada_b, v_norm_mix, v_norm_ffn, v_final_norm, v_a_sink))]
    dmod_all, sg, sd, sm, sv_, loss_tile = _small_exchange(tiles, *small)
    loss = loss_tile[0, 0]
    dmod_all = dmod_all.reshape(N_DEV, DEPTH, 6 * D)
    dmod_mine = lax.dynamic_slice_in_dim(dmod_all, me * ncol, ncol, axis=2)
    dmod_pad = jnp.pad(jnp.transpose(dmod_mine, (1, 0, 2)), ((0, 0), (0, LANES - N_DEV), (0, 0))).astype(BF16)
    cond_t = jnp.pad(cond_all.T, ((0, 0), (0, LANES - N_DEV))).astype(BF16)
    ada = _adam_ada_w(cond_t, dmod_pad, ada_w, m_ada_w, v_ada_w)

    outs = [loss, grad_x]
    small_res = [_unpack_small(t) for t in (sg, sd, sm, sv_)]
    for q, kind in enumerate(("grad", "delta", "m", "v")):
        ab, nm_, nf_, fn, sk = small_res[q]
        outs += [ada[q], ab, nm_, nf_, big[(kind, "ffn_in")], big[(kind, "ffn_out")], big[(kind, "a_in")], big[(kind, "a_out")],
                 sk, big[(kind, "b_in")], big[(kind, "b_out")], fn]
    return tuple(outs)
```

```python
import math

import numpy as np
import jax
import jax.numpy as jnp
from jax import lax
from jax.experimental import pallas as pl
from jax.experimental.pallas import tpu as pltpu

D = 1024
HEAD_DIM = 64
D_FF = 2816
DEPTH = 4
N_DEV = 8
A_QKV = 1536
B_QKV = 2304
A_HALF = 128
B_HALF = 64
B_DILS = (1, 4, 16)
RMS_EPS = 1e-6
NEG = -1e30
ADAM_LR = 0.001
ADAM_B1 = 0.9
ADAM_B2 = 0.999
ADAM_EPS = 1e-08
ADAM_WD = 0.01
ADAM_STEP = 10

LANES = 128
SPLIT = 2
TQ = 128
VMEM_LIMIT = 56 * 1024 * 1024
MESH = pl.DeviceIdType.MESH
F32 = jnp.float32
BF16 = jnp.bfloat16

SEGMENTS = ([("ffn_in", l, 704) for l in range(4)] + [("ffn_out", l, 352) for l in range(4)]
            + [("a_in", j, 192) for j in range(2)] + [("a_out", j, 128) for j in range(2)]
            + [("b_in", j, 288) for j in range(2)] + [("b_out", j, 64) for j in range(2)])
def _layer_segments(i):
    mixer = "a" if i % 2 == 0 else "b"
    return [s for s in SEGMENTS if (s[0].startswith("ffn") and s[1] == i) or (s[0].startswith(mixer + "_") and s[1] == i // 2)]


def _offsets(segs):
    rows = [s[2] for s in segs]
    return [sum(rows[:k]) for k in range(len(rows))], sum(rows)
STAT_ROWS = 56


def _nn(a, b):
    return jnp.dot(a, b, preferred_element_type=F32)


def _nt(a, b):
    return lax.dot_general(a, b, (((1,), (1,)), ((), ())), preferred_element_type=F32)


def _tn(a, b):
    return lax.dot_general(a, b, (((0,), (0,)), ((), ())), preferred_element_type=F32)


def _params(dims=None, vmem=None):
    kw = {}
    if dims is not None:
        kw["dimension_semantics"] = dims
    if vmem is not None:
        kw["vmem_limit_bytes"] = vmem
    return pltpu.CompilerParams(**kw)


def _my_index():
    return 4 * lax.axis_index("x") + 2 * lax.axis_index("y") + lax.axis_index("c")


def _peer(k):
    x, y, c = lax.axis_index("x"), lax.axis_index("y"), lax.axis_index("c")
    px, py, pc = x ^ ((k >> 2) & 1), y ^ ((k >> 1) & 1), c ^ (k & 1)
    return (px, py, pc), 4 * px + 2 * py + pc


def _const_spec(shape):
    nd = len(shape)
    return pl.BlockSpec(shape, lambda *_: (0,) * nd)


def _cond_exchange(c_tile, ada_w, ada_b_mine):
    ncol = ada_w.shape[-1]

    def body(c_ref, w_ref, b_ref, cond_ref, parts_ref, call_ref, mine_ref, send_sems, recv_sems):
        me = _my_index()
        call_ref[me] = c_ref[...]
        copies = []
        for k in range(1, N_DEV):
            dev, _ = _peer(k)
            cp = pltpu.make_async_remote_copy(src_ref=c_ref, dst_ref=call_ref.at[me], send_sem=send_sems.at[0, k - 1],
                                              recv_sem=recv_sems.at[0, k - 1], device_id=dev, device_id_type=MESH)
            cp.start()
            copies.append(cp)
        for k in range(1, N_DEV):
            _, pidx = _peer(k)
            pltpu.make_async_remote_copy(src_ref=c_ref, dst_ref=call_ref.at[pidx], send_sem=send_sems.at[0, k - 1],
                                         recv_sem=recv_sems.at[0, k - 1], device_id=_peer(k)[0], device_id_type=MESH).wait_recv()
        for cp in copies:
            cp.wait_send()
        row = lax.broadcasted_iota(jnp.int32, (N_DEV, D), 0)
        cmat = jnp.zeros((N_DEV, D), F32)
        for j in range(N_DEV):
            cmat = jnp.where(row == j, call_ref[j], cmat)
        cond = cmat * jax.nn.sigmoid(cmat)
        cond_ref[...] = cond
        cb = cond.astype(BF16)
        for l in range(DEPTH):
            mine_ref[l] = _nn(cb, w_ref[l].astype(BF16)) + b_ref[pl.ds(l, 1), :]
        parts_ref[me] = mine_ref[...]
        copies = []
        for k in range(1, N_DEV):
            dev, _ = _peer(k)
            cp = pltpu.make_async_remote_copy(src_ref=mine_ref, dst_ref=parts_ref.at[me], send_sem=send_sems.at[1, k - 1],
                                              recv_sem=recv_sems.at[1, k - 1], device_id=dev, device_id_type=MESH)
            cp.start()
            copies.append(cp)
        for k in range(1, N_DEV):
            dev, pidx = _peer(k)
            pltpu.make_async_remote_copy(src_ref=mine_ref, dst_ref=parts_ref.at[pidx], send_sem=send_sems.at[1, k - 1],
                                         recv_sem=recv_sems.at[1, k - 1], device_id=dev, device_id_type=MESH).wait_recv()
        for cp in copies:
            cp.wait_send()

    vm = pl.BlockSpec(memory_space=pltpu.VMEM)
    return pl.pallas_call(
        body, name="cond_exchange",
        out_shape=(jax.ShapeDtypeStruct((N_DEV, D), F32), jax.ShapeDtypeStruct((N_DEV, DEPTH, N_DEV, ncol), F32)),
        in_specs=[vm, vm, vm], out_specs=(vm, vm),
        scratch_shapes=[pltpu.VMEM((N_DEV, N_DEV, D), F32), pltpu.VMEM((DEPTH, N_DEV, ncol), F32),
                        pltpu.SemaphoreType.DMA((2, N_DEV - 1)), pltpu.SemaphoreType.DMA((2, N_DEV - 1))],
        compiler_params=_params(vmem=VMEM_LIMIT),
    )(c_tile, ada_w, ada_b_mine)[:2]


def _all_gather_weights(shards):
    n = len(shards)
    big = max(range(n), key=lambda s: shards[s].shape[0])
    total = sum(sh.shape[0] for sh in shards)
    assert N_DEV * shards[big].shape[0] >= total

    def body(*refs):
        ins, outs = refs[:n], refs[n:2 * n]
        local_sems, send_sems, recv_sems = refs[2 * n:]
        me = _my_index()
        local = []
        for s in range(n):
            rows = ins[s].shape[0]
            cp = pltpu.make_async_copy(ins[s], outs[s].at[pl.ds(me * rows, rows)], local_sems.at[s])
            cp.start()
            local.append(cp)
        for k in range(1, N_DEV):
            dev, _ = _peer(k)
            for s in range(n):
                rows = ins[s].shape[0]
                pltpu.make_async_remote_copy(src_ref=ins[s], dst_ref=outs[s].at[pl.ds(me * rows, rows)],
                                             send_sem=send_sems.at[k - 1], recv_sem=recv_sems.at[k - 1],
                                             device_id=dev, device_id_type=MESH).start()
        whole = outs[big].at[pl.ds(0, total)]
        for k in range(1, N_DEV):
            dev, _ = _peer(k)
            w = pltpu.make_async_remote_copy(src_ref=whole, dst_ref=whole, send_sem=send_sems.at[k - 1],
                                             recv_sem=recv_sems.at[k - 1], device_id=dev, device_id_type=MESH)
            w.wait_send()
            w.wait_recv()
        for cp in local:
            cp.wait()

    hbm = pl.BlockSpec(memory_space=pl.ANY)
    return pl.pallas_call(
        body, name="weight_all_gather",
        out_shape=tuple(jax.ShapeDtypeStruct((N_DEV * s.shape[0], D), s.dtype) for s in shards),
        in_specs=[hbm] * n, out_specs=tuple([hbm] * n),
        scratch_shapes=[pltpu.SemaphoreType.DMA((n,)), pltpu.SemaphoreType.DMA((N_DEV - 1,)),
                        pltpu.SemaphoreType.DMA((N_DEV - 1,))],
    )(*shards)


HBM = pl.BlockSpec(memory_space=pltpu.HBM)
SEM = pl.BlockSpec(memory_space=pltpu.SEMAPHORE)
EFFECT = pltpu.SideEffectType.DATAFLOW_SIDE_EFFECTING


def _exchange_start(srcs, landings, dst, rows, to_peer_rows, after, name):
    n, nl = len(srcs), len(landings)

    def body(*refs):
        src_refs = refs[:n]
        send_sems, recv_sems = refs[n + 1], refs[n + 2]
        land_refs = refs[2 * n + 3:2 * n + 3 + nl]
        token = refs[-1]
        me = _my_index()
        for k in range(1, N_DEV):
            dev, pidx = _peer(k)
            for q in range(n):
                src = src_refs[q].at[pl.ds(pidx * rows[q], rows[q])] if to_peer_rows else src_refs[q]
                pltpu.make_async_remote_copy(src_ref=src, dst_ref=land_refs[dst[q][0]].at[me, pl.ds(dst[q][1], rows[q])],
                                             send_sem=send_sems.at[k * n + q], recv_sem=recv_sems.at[k * n + q],
                                             device_id=dev, device_id_type=MESH).start()
        if not to_peer_rows:
            for q in range(n):
                pltpu.make_async_copy(src_refs[q], land_refs[dst[q][0]].at[me, pl.ds(dst[q][1], rows[q])], send_sems.at[q]).start()
        token[...] = jnp.zeros_like(token)

    sems = pltpu.SemaphoreType.DMA((N_DEV * n,))
    return pl.pallas_call(
        body, name=name,
        out_shape=(sems, sems, *[pltpu.HBM(a.shape, a.dtype) for a in srcs], *[pltpu.HBM(shape, BF16) for shape in landings],
                   jax.ShapeDtypeStruct((8, LANES), F32)),
        in_specs=[HBM] * n + [pl.BlockSpec(memory_space=pl.ANY)],
        out_specs=(SEM, SEM, *[HBM] * (n + nl), pl.BlockSpec(memory_space=pltpu.VMEM)),
        input_output_aliases={q: 2 + q for q in range(n)},
        compiler_params=pltpu.CompilerParams(has_side_effects=EFFECT),
    )(*[pltpu.with_memory_space_constraint(a, pltpu.HBM) for a in srcs], after)


def _exchange_wait(started, n, dst, rows, own_slot, after, name):
    send_sems, recv_sems = started[0], started[1]
    arrays = list(started[2:-1])
    n1 = len(arrays)

    def body(*refs):
        land_refs = refs[n:n1]
        sends, recvs = refs[n1], refs[n1 + 1]
        for k in range(1, N_DEV):
            dev, _ = _peer(k)
            for q in range(n):
                slot = land_refs[dst[q][0]].at[0, pl.ds(dst[q][1], rows[q])]
                w = pltpu.make_async_remote_copy(src_ref=slot, dst_ref=slot, send_sem=sends.at[k * n + q],
                                                 recv_sem=recvs.at[k * n + q], device_id=dev, device_id_type=MESH)
                w.wait_send()
                w.wait_recv()
        if own_slot:
            for q in range(n):
                slot = land_refs[dst[q][0]].at[0, pl.ds(dst[q][1], rows[q])]
                pltpu.make_async_copy(slot, slot, sends.at[q]).wait()

    return pl.pallas_call(
        body, name=name, out_shape=tuple(pltpu.HBM(a.shape, a.dtype) for a in arrays),
        in_specs=[HBM] * n1 + [SEM, SEM, pl.BlockSpec(memory_space=pl.ANY)], out_specs=tuple([HBM] * n1),
        input_output_aliases={q: q for q in range(n1)},
        compiler_params=pltpu.CompilerParams(has_side_effects=EFFECT),
    )(*arrays, send_sems, recv_sems, after)[n:]


def _norm_mod(x, nw, sc, sh):
    ms = jnp.mean(x * x, axis=-1, keepdims=True)
    xh = x * lax.rsqrt(ms + RMS_EPS)
    return xh, (xh * nw) * (1.0 + sc) + sh


def _proj(x, nw, sc, sh, wt, *, name):
    S, N = x.shape[0], wt.shape[0]
    tm = 512

    def body(x_ref, nw_ref, sc_ref, sh_ref, w_ref, h_ref, out_ref):
        for half in range(SPLIT):
            rows = pl.ds(half * (tm // SPLIT), tm // SPLIT)
            _, h = _norm_mod(x_ref[rows, :], nw_ref[...], sc_ref[...], sh_ref[...])
            hb = h.astype(BF16)
            h_ref[rows, :] = hb
            out_ref[rows, :] = _nt(hb, w_ref[...]).astype(BF16)

    row = lambda w: pl.BlockSpec((tm, w), lambda i: (i, 0))
    vec = _const_spec((1, D))
    return pl.pallas_call(
        body, name=name, grid=(S // tm,), out_shape=(jax.ShapeDtypeStruct((S, D), BF16), jax.ShapeDtypeStruct((S, N), BF16)),
        in_specs=[row(D), vec, vec, vec, _const_spec((N, D))], out_specs=(row(D), row(N)),
        compiler_params=_params(("parallel",), VMEM_LIMIT),
    )(x, nw, sc, sh, wt)


def _ffn_out(a, w, x, g, target, fnw, *, name, nxt=None):
    S, K = a.shape
    tm = 512
    last = target is not None

    def body(a_ref, w_ref, x_ref, g_ref, *rest):
        y = _nn(a_ref[...], w_ref[...])
        xv = x_ref[...] + g_ref[...] * y
        if not last:
            if nxt is None:
                xo_ref, y_ref = rest
            else:
                nw_ref, sc_ref, sh_ref, wn_ref, xo_ref, y_ref, h_ref, p_ref = rest
                _, h = _norm_mod(xv, nw_ref[...], sc_ref[...], sh_ref[...])
                hb = h.astype(BF16)
                h_ref[...] = hb
                p_ref[...] = _nt(hb, wn_ref[...]).astype(BF16)
            y_ref[...] = y.astype(BF16)
            xo_ref[...] = xv
            return
        t_ref, fw_ref, dx_ref, y_ref, st_ref = rest
        y_ref[...] = y.astype(BF16)

        @pl.when(pl.program_id(0) == 0)
        def _():
            st_ref[...] = jnp.zeros_like(st_ref)

        rstd = lax.rsqrt(jnp.mean(xv * xv, axis=-1, keepdims=True) + RMS_EPS)
        xh = xv * rstd
        err = xh * fw_ref[...] - t_ref[...]
        dy = err * (1.0 / D)
        dxh = dy * fw_ref[...]
        dx_ref[...] = rstd * (dxh - xh * jnp.mean(dxh * xh, axis=-1, keepdims=True))
        st_ref[pl.ds(0, 1), :] = st_ref[pl.ds(0, 1), :] + jnp.sum(dy * xh, axis=0, keepdims=True)
        st_ref[pl.ds(1, 1), :] = st_ref[pl.ds(1, 1), :] + jnp.sum(err * err, axis=0, keepdims=True)

    row = lambda w_: pl.BlockSpec((tm, w_), lambda i: (i, 0))
    in_specs = [row(K), _const_spec(w.shape), row(D), _const_spec((1, D))]
    args = [a, w, x, g]
    out_shape = [jax.ShapeDtypeStruct((S, D), F32), jax.ShapeDtypeStruct((S, D), BF16)]
    out_specs = [row(D), row(D)]
    if last:
        in_specs += [row(D), _const_spec((1, D))]
        args += [target, fnw]
        out_shape.append(jax.ShapeDtypeStruct((8, D), F32))
        out_specs.append(_const_spec((8, D)))
    if nxt is not None:
        vec = _const_spec((1, D))
        in_specs += [vec, vec, vec, _const_spec(nxt[3].shape)]
        args += list(nxt)
        out_shape += [jax.ShapeDtypeStruct((S, D), BF16), jax.ShapeDtypeStruct((S, nxt[3].shape[0]), BF16)]
        out_specs += [row(D), row(nxt[3].shape[0])]
    return pl.pallas_call(
        body, name=name, grid=(S // tm,), out_shape=tuple(out_shape), in_specs=in_specs, out_specs=tuple(out_specs),
        compiler_params=_params(("arbitrary",) if last else ("parallel",), VMEM_LIMIT),
    )(*args)


CHUNK = 1024


def _tile_rows(r, chunk=CHUNK):
    return min(TQ, chunk // r)


def _out_ffn_in(a, w_mix, x, g, nw, sc, sh, wt, *, w_is_transposed, name):
    S, K = a.shape
    tm = 256

    def body(a_ref, wm_ref, x_ref, g_ref, nw_ref, sc_ref, sh_ref, w_ref, x1_ref, y_ref, h_ref, gu_ref, act_ref):
        y = _nt(a_ref[...], wm_ref[...]) if w_is_transposed else _nn(a_ref[...], wm_ref[...])
        y_ref[...] = y.astype(BF16)
        x1 = x_ref[...] + g_ref[...] * y
        x1_ref[...] = x1
        _, h = _norm_mod(x1, nw_ref[...], sc_ref[...], sh_ref[...])
        hb = h.astype(BF16)
        h_ref[...] = hb
        gate = _nt(hb, w_ref[pl.ds(0, D_FF), :])
        up = _nt(hb, w_ref[pl.ds(D_FF, D_FF), :])
        sig = jax.nn.sigmoid(gate)
        silu = gate * sig
        gu_ref[:, pl.ds(0, D_FF)] = (up * (sig * (1.0 + gate * (1.0 - sig)))).astype(BF16)
        gu_ref[:, pl.ds(D_FF, D_FF)] = silu.astype(BF16)
        act_ref[...] = (silu * up).astype(BF16)

    row = lambda w_: pl.BlockSpec((tm, w_), lambda i: (i, 0))
    vec = _const_spec((1, D))
    return pl.pallas_call(
        body, name=name, grid=(S // tm,),
        out_shape=(jax.ShapeDtypeStruct((S, D), F32), jax.ShapeDtypeStruct((S, D), BF16), jax.ShapeDtypeStruct((S, D), BF16),
                   jax.ShapeDtypeStruct((S, 2 * D_FF), BF16), jax.ShapeDtypeStruct((S, D_FF), BF16)),
        in_specs=[row(K), _const_spec(w_mix.shape), row(D), vec, vec, vec, vec, _const_spec(wt.shape)],
        out_specs=(row(D), row(D), row(D), row(2 * D_FF), row(D_FF)),
        compiler_params=_params(("parallel",), VMEM_LIMIT),
    )(a, w_mix, x, g, nw, sc, sh, wt)


def _alibi_bias(slopes, half, dil, chunk=CHUNK, both=False):
    tq = _tile_rows(dil, chunk)
    tk = tq + 2 * half
    rel = np.arange(tk)[:, None] - half - np.arange(tq)[None, :]
    band = np.abs(rel) <= half
    dist = (dil * np.abs(rel)).astype(np.float32)
    tabs = [np.where(band, -np.float32(s) * dist, np.float32(NEG)).astype(np.float32) for s in slopes]
    out = []
    for u in range(0, len(tabs), 8):
        tab = np.concatenate(tabs[u:u + 8], axis=1)
        first, last = tab.copy(), tab.copy()
        first[:half] = NEG
        last[tk - half:] = NEG
        out += [tab, first, last]
        if both:
            last = last.copy()
            last[:half] = NEG
            out.append(last)
    return jnp.asarray(np.concatenate(out, axis=0))


def _slopes(n):
    return (2.0 ** (-8.0 * np.arange(1, n + 1) / n)).astype(np.float32)


def _head_masks(tq):
    lane = lax.broadcasted_iota(jnp.int32, (tq, LANES), 1)
    lo = lane < HEAD_DIM
    return lo, jnp.logical_not(lo)


def _stack_heads(tiles, lo, hi, scale):
    blocks = []
    for t in range(4):
        xf = tiles[t] if scale == 1.0 else tiles[t] * scale
        for a in range(2):
            xm = jnp.where(lo if a == 0 else hi, xf, 0.0)
            if a != t // 2:
                xm = pltpu.roll(xm, HEAD_DIM, 1)
            blocks.append(xm.astype(BF16))
    return jnp.concatenate(blocks, axis=0)


def _tile_from_columns(x8t, t, tq):
    r0 = HEAD_DIM * (t // 2)
    top = x8t[r0:r0 + HEAD_DIM, 2 * t * tq:(2 * t + 1) * tq]
    bot = x8t[r0:r0 + HEAD_DIM, (2 * t + 1) * tq:(2 * t + 2) * tq]
    return jnp.concatenate([top, bot], axis=0).T


def _attn_layout(S, C, r, half, qoff, koff, voff, chunk):
    hb = half * r
    per = chunk // hb
    nhb = S // hb
    main = lambda off: pl.BlockSpec((chunk, LANES), lambda u, i: (i, off // LANES + u))
    prev = lambda off: pl.BlockSpec((hb, LANES), lambda u, i: (jnp.maximum(i * per - 1, 0), off // LANES + u))
    nxt = lambda off: pl.BlockSpec((hb, LANES), lambda u, i: (jnp.minimum((i + 1) * per, nhb - 1), off // LANES + u))
    specs = [pl.BlockSpec((chunk, 4 * LANES), lambda u, i: (i, qoff // (4 * LANES) + u))]
    specs += [prev(koff), main(koff), nxt(koff), prev(voff), main(voff), nxt(voff)]
    return specs, hb


def _stage(dst, srcs):
    row = 0
    for src in srcs:
        n = src.shape[0]
        dst[pl.ds(row, n), :] = src[...].astype(F32)
        row += n


def _rows(start, n, r):
    return pl.ds(start, n, stride=r) if r > 1 else pl.ds(start, n)


def _attn_fwd(qkv, bias, sink, *, C, r, half, qoff, koff, voff, n_units, out_dtype, name, prev=None):
    S = qkv.shape[0]
    chunk = max(CHUNK, TQ * r)
    tq = _tile_rows(r, chunk)
    tk = tq + 2 * half
    tiles = chunk // (r * tq)
    nsteps = S // chunk
    specs, hb = _attn_layout(S, C, r, half, qoff, koff, voff, chunk)
    use_sink = sink is not None

    def body(*refs):
        q_ref, kp, km, kn, vp, vm, vn, bias_ref = refs[:8]
        rest = list(refs[8:])
        sink_ref = rest.pop(0) if use_sink else None
        op_ref, lp_ref = (rest.pop(0), rest.pop(0)) if prev is not None else (None, None)
        o_ref, lse_ref, qs, ks, vs, os_, ls = rest
        i = pl.program_id(1)

        def merged(o_new, l_new, rows, cols):
            if prev is None:
                return o_new, l_new
            o_old, l_old = op_ref[rows, cols], lp_ref[rows, cols]
            mx = jnp.maximum(l_new, l_old)
            e_new, e_old = jnp.exp(l_new - mx), jnp.exp(l_old - mx)
            den = e_new + e_old
            return (e_new / den) * o_new + (e_old / den) * o_old, mx + jnp.log(den)

        if r > 1:
            for t in range(4):
                qs[t] = q_ref[:, pl.ds(t * LANES, LANES)].astype(F32)
        _stage(ks, [kp, km, kn])
        _stage(vs, [vp, vm, vn])
        lo, hi = _head_masks(tq)

        def tile_in(staged, ref, t, start):
            if r > 1:
                return staged[t, _rows(start, tq, r), :]
            return ref[pl.ds(start, tq), pl.ds(t * LANES, LANES)].astype(F32)

        ones = jnp.ones((16, tk), BF16)
        if use_sink:
            sk = sink_ref[pl.ds(0, 1), :]

        def chain(n, carry):
            rho, c = n // tiles, n % tiles
            start = c * (tq * r) + rho
            if r == 1:
                start = pl.multiple_of(start, tq)
            variant = jnp.where(jnp.logical_and(i == 0, c == 0), 1, 0) + jnp.where(
                jnp.logical_and(i == nsteps - 1, c == tiles - 1), 2, 0)
            k2 = ks[_rows(start, tk, r), :].astype(BF16)
            v2t = jnp.concatenate([vs[_rows(start, tk, r), :].T.astype(BF16), ones], axis=0)
            q8 = _stack_heads([tile_in(qs, q_ref, t, start) for t in range(4)], lo, hi, HEAD_DIM ** -0.5)
            s = _nt(k2, q8) + bias_ref[pl.ds(pl.multiple_of(variant * tk, 8), tk), :]
            m = jnp.max(s, axis=0, keepdims=True)
            if use_sink:
                m = jnp.maximum(m, sk)
            pv = _nn(v2t, jnp.exp(s - m).astype(BF16))
            l = pv[LANES:LANES + 1]
            if use_sink:
                l = l + jnp.exp(sk - m)
            o8t = pv[:LANES] / l
            lse8 = jnp.broadcast_to(m + jnp.log(l), (LANES, 8 * tq))
            for t in range(4):
                if r > 1:
                    os_[t, _rows(start, tq, r), :] = _tile_from_columns(o8t, t, tq)
                    ls[t, _rows(start, tq, r), :] = _tile_from_columns(lse8, t, tq)
                else:
                    rows, cols = pl.ds(start, tq), pl.ds(t * LANES, LANES)
                    o_t, l_t = merged(_tile_from_columns(o8t, t, tq), _tile_from_columns(lse8, t, tq), rows, cols)
                    o_ref[rows, cols] = o_t.astype(out_dtype)
                    lse_ref[rows, cols] = l_t
            return carry

        lax.fori_loop(0, r * tiles, chain, 0, unroll=4)
        if r > 1:
            for t in range(4):
                cols = pl.ds(t * LANES, LANES)
                o_t, l_t = merged(os_[t], ls[t], slice(None), cols)
                o_ref[:, cols] = o_t.astype(out_dtype)
                lse_ref[:, cols] = l_t

    in_specs = specs + [pl.BlockSpec((bias.shape[0] // n_units, 8 * tq), lambda u, i: (u, 0))]
    args = [qkv] * 7 + [bias]
    if use_sink:
        in_specs.append(pl.BlockSpec((8, 8 * tq), lambda u, i: (u, 0)))
        args.append(sink)
    wide = pl.BlockSpec((chunk, 4 * LANES), lambda u, i: (i, u))
    if prev is not None:
        in_specs += [wide, wide]
        args += list(prev)
    win = hb + chunk + hb
    big = lambda: pltpu.VMEM((4, chunk if r > 1 else 8, LANES), F32)
    return pl.pallas_call(
        body, name=name, grid=(n_units, nsteps),
        out_shape=(jax.ShapeDtypeStruct((S, n_units * 512), out_dtype), jax.ShapeDtypeStruct((S, n_units * 512), F32)),
        in_specs=in_specs, out_specs=(wide, wide),
        scratch_shapes=[big(), pltpu.VMEM((win, LANES), F32), pltpu.VMEM((win, LANES), F32), big(), big()],
        compiler_params=_params(("parallel", "parallel"), VMEM_LIMIT),
    )(*args)


def _attn_bwd(qkv, bias, sink, o, do, lse, *, C, r, half, qoff, koff, voff, n_units, name):
    S = qkv.shape[0]
    tq = _tile_rows(r)
    tk = tq + 2 * half
    tiles = CHUNK // (r * tq)
    nsteps = S // CHUNK
    specs, hb = _attn_layout(S, C, r, half, qoff, koff, voff, CHUNK)
    use_sink = sink is not None

    def body(*refs):
        q_ref, kp, km, kn, vp, vm, vn, bias_ref = refs[:8]
        rest = list(refs[8:])
        sink_ref = rest.pop(0) if use_sink else None
        o_ref, do_ref, lse_ref, dq_ref, dk_hbm, dv_hbm = rest[:6]
        rest = rest[6:]
        dsink_ref = rest.pop(0) if use_sink else None
        qs, ks, vs, os_, dos, ls, dqs, acck, accv, sem = rest
        u, i = pl.program_id(0), pl.program_id(1)

        @pl.when(i == 0)
        def _():
            acck[...] = jnp.zeros_like(acck)
            accv[...] = jnp.zeros_like(accv)
            if use_sink:
                dsink_ref[...] = jnp.zeros_like(dsink_ref)

        if r > 1:
            for t in range(4):
                cols = pl.ds(t * LANES, LANES)
                qs[t] = q_ref[:, cols].astype(F32)
                os_[t] = o_ref[:, cols].astype(F32)
                dos[t] = do_ref[:, cols].astype(F32)
                ls[t] = lse_ref[:, cols]
        _stage(ks, [kp, km, kn])
        _stage(vs, [vp, vm, vn])
        lo, hi = _head_masks(tq)

        def tile_in(staged, ref, t, start):
            if r > 1:
                return staged[t, _rows(start, tq, r), :]
            return ref[pl.ds(start, tq), pl.ds(t * LANES, LANES)].astype(F32)

        base = pl.multiple_of(i * CHUNK, CHUNK)
        if use_sink:
            sk = sink_ref[pl.ds(0, 1), :]

        def chain(n, carry):
            rho, c = n // tiles, n % tiles
            start = c * (tq * r) + rho
            if r == 1:
                start = pl.multiple_of(start, tq)
            variant = jnp.where(jnp.logical_and(i == 0, c == 0), 1, 0) + jnp.where(
                jnp.logical_and(i == nsteps - 1, c == tiles - 1), 2, 0)
            k2 = ks[_rows(start, tk, r), :].astype(BF16)
            v2 = vs[_rows(start, tk, r), :].astype(BF16)
            k2t = ks[_rows(start, tk, r), :].T.astype(BF16)
            q8 = _stack_heads([tile_in(qs, q_ref, t, start) for t in range(4)], lo, hi, HEAD_DIM ** -0.5)
            do_tiles = [tile_in(dos, do_ref, t, start) for t in range(4)]
            do8 = _stack_heads(do_tiles, lo, hi, 1.0)
            deltas, lses = [], []
            for t in range(4):
                prod_t = (do_tiles[t] * tile_in(os_, o_ref, t, start)).T
                lse_t = tile_in(ls, lse_ref, t, start).T
                for a in range(2):
                    deltas.append(jnp.sum(prod_t[a * HEAD_DIM:(a + 1) * HEAD_DIM], axis=0, keepdims=True))
                    lses.append(lse_t[a * HEAD_DIM:a * HEAD_DIM + 1])
            delta8 = jnp.concatenate(deltas, axis=1)
            lse8 = jnp.concatenate(lses, axis=1)
            s = _nt(k2, q8) + bias_ref[pl.ds(pl.multiple_of(variant * tk, 8), tk), :]
            p = jnp.exp(s - lse8)
            dp = _nt(v2, do8)
            dsb = (p * (dp - delta8)).astype(BF16)
            dq8t = _nn(k2t, dsb)
            for t in range(4):
                dq_t = _tile_from_columns(dq8t, t, tq) * (HEAD_DIM ** -0.5)
                if r > 1:
                    dqs[t, _rows(start, tq, r), :] = dq_t
                else:
                    dq_ref[pl.ds(start, tq), pl.ds(t * LANES, LANES)] = dq_t.astype(BF16)
            arow = base + start
            if r == 1:
                arow = pl.multiple_of(arow, tq)
            acck[_rows(arow, tk, r), :] = acck[_rows(arow, tk, r), :] + _nn(dsb, q8)
            accv[_rows(arow, tk, r), :] = accv[_rows(arow, tk, r), :] + _nn(p.astype(BF16), do8)
            if use_sink:
                e = jnp.exp(sk - lse8) * delta8
                for h in range(8):
                    part = -jnp.sum(e[:, h * tq:(h + 1) * tq], axis=1, keepdims=True)
                    dsink_ref[pl.ds(h, 1), :] = dsink_ref[pl.ds(h, 1), :] + part
            return carry

        lax.fori_loop(0, r * tiles, chain, 0, unroll=2)
        if r > 1:
            for t in range(4):
                dq_ref[:, pl.ds(t * LANES, LANES)] = dqs[t].astype(BF16)

        @pl.when(i == nsteps - 1)
        def _():
            ck = pltpu.make_async_copy(acck.at[pl.ds(hb, S)], dk_hbm.at[u], sem.at[0])
            cv = pltpu.make_async_copy(accv.at[pl.ds(hb, S)], dv_hbm.at[u], sem.at[1])
            ck.start()
            cv.start()
            ck.wait()
            cv.wait()

    wide = pl.BlockSpec((CHUNK, 4 * LANES), lambda u, i: (i, u))
    hbm = pl.BlockSpec(memory_space=pl.ANY)
    in_specs = specs + [pl.BlockSpec((3 * tk, 8 * tq), lambda u, i: (u, 0))]
    args = [qkv] * 7 + [bias]
    if use_sink:
        in_specs.append(pl.BlockSpec((8, 8 * tq), lambda u, i: (u, 0)))
        args.append(sink)
    in_specs += [wide, wide, wide]
    args += [o, do, lse]
    out_shape = [jax.ShapeDtypeStruct((S, n_units * 512), BF16), jax.ShapeDtypeStruct((n_units, S, LANES), F32),
                 jax.ShapeDtypeStruct((n_units, S, LANES), F32)]
    out_specs = [wide, hbm, hbm]
    if use_sink:
        out_shape.append(jax.ShapeDtypeStruct((n_units * 8, LANES), F32))
        out_specs.append(pl.BlockSpec((8, LANES), lambda u, i: (u, 0)))
    win = hb + CHUNK + hb
    big = lambda: pltpu.VMEM((4, CHUNK if r > 1 else 8, LANES), F32)
    res = pl.pallas_call(
        body, name=name, grid=(n_units, nsteps), out_shape=tuple(out_shape), in_specs=in_specs, out_specs=tuple(out_specs),
        scratch_shapes=[big(), pltpu.VMEM((win, LANES), F32), pltpu.VMEM((win, LANES), F32), big(), big(), big(), big(),
                        pltpu.VMEM((S + 2 * hb, LANES), F32), pltpu.VMEM((S + 2 * hb, LANES), F32), pltpu.SemaphoreType.DMA((2,))],
        compiler_params=_params(("arbitrary", "arbitrary"), VMEM_LIMIT),
    )(*args)
    return res[0], res[1], res[2], (res[3] if use_sink else None)


def _columns(part_refs, rows):
    return jnp.concatenate([r[rows, :].astype(BF16) for r in part_refs], axis=1)


def _piece_specs(parts, rows, row_axis):
    arrays, specs = [], []
    for p in parts:
        if isinstance(p, tuple):
            arr, u = p
            index = (lambda *g, u=u: (u, g[row_axis], 0))
            arrays.append(arr)
            specs.append(pl.BlockSpec((None, rows, arr.shape[2]), index))
        else:
            arrays.append(p)
            specs.append(pl.BlockSpec((rows, p.shape[1]), lambda *g: (g[row_axis], 0)))
    return arrays, specs


def _norm_bwd(dy_parts, wt, x, dres, nw, sc, h, *, name):
    S, N = x.shape[0], wt.shape[0]
    npart = len(dy_parts)
    tm = 512
    part_arrays, part_specs = _piece_specs(dy_parts, tm, 0)

    def body(*refs):
        part_refs = refs[:npart]
        w_ref, x_ref, dres_ref, nw_ref, sc_ref, h_ref, dx_ref, st_ref, dw_ref, acc = refs[npart:]
        k = pl.program_id(0)

        @pl.when(k == 0)
        def _():
            st_ref[...] = jnp.zeros_like(st_ref)
            acc[...] = jnp.zeros_like(acc)

        dy = _columns(part_refs, slice(None))
        acc[...] += _tn(h_ref[...], dy)
        nwv, scale = nw_ref[...], 1.0 + sc_ref[...]
        sums = [jnp.zeros((1, D), F32)] * 3
        for half in range(SPLIT):
            rows = pl.ds(half * (tm // SPLIT), tm // SPLIT)
            dh = _nn(dy[half * (tm // SPLIT):(half + 1) * (tm // SPLIT)], w_ref[...])
            xv = x_ref[rows, :]
            rstd = lax.rsqrt(jnp.mean(xv * xv, axis=-1, keepdims=True) + RMS_EPS)
            xh = xv * rstd
            dxh = dh * (nwv * scale)
            dx_ref[rows, :] = dres_ref[rows, :] + rstd * (dxh - xh * jnp.mean(dxh * xh, axis=-1, keepdims=True))
            dhx = dh * xh
            sums = [sums[0] + jnp.sum(dh, axis=0, keepdims=True), sums[1] + jnp.sum(dhx * nwv, axis=0, keepdims=True),
                    sums[2] + jnp.sum(dhx * scale, axis=0, keepdims=True)]
        for q in range(3):
            st_ref[pl.ds(q, 1), :] = st_ref[pl.ds(q, 1), :] + sums[q]

        @pl.when(k == S // tm - 1)
        def _():
            dw_ref[...] = acc[...].T.astype(BF16)

    row = lambda w_: pl.BlockSpec((tm, w_), lambda i: (i, 0))
    vec = _const_spec((1, D))
    return pl.pallas_call(
        body, name=name, grid=(S // tm,),
        out_shape=(jax.ShapeDtypeStruct((S, D), F32), jax.ShapeDtypeStruct((8, D), F32), jax.ShapeDtypeStruct((N, D), BF16)),
        in_specs=part_specs + [_const_spec((N, D)), row(D), row(D), vec, vec, row(D)],
        out_specs=(row(D), _const_spec((8, D)), _const_spec((N, D))), scratch_shapes=[pltpu.VMEM((D, N), F32)],
        compiler_params=_params(("arbitrary",), VMEM_LIMIT),
    )(*part_arrays, wt, x, dres, nw, sc, h)


def _ffn_bwd(dx, y, g, w_out, gu, wt_in, x, nw, sc, y1, g1, w_mix, *, mix_is_transposed, name):
    S = dx.shape[0]
    K = w_out.shape[0]
    Km = w_mix.shape[1] if mix_is_transposed else w_mix.shape[0]
    tm = 256

    def body(dx_ref, y_ref, g_ref, wo_ref, gu_ref, wi_ref, x_ref, nw_ref, sc_ref, y1_ref, g1_ref, wm_ref,
             dgu_ref, dyb_ref, dxo_ref, da_ref, dy1_ref, stg_ref, stf_ref, stm_ref):
        @pl.when(pl.program_id(0) == 0)
        def _():
            stg_ref[...] = jnp.zeros_like(stg_ref)
            stf_ref[...] = jnp.zeros_like(stf_ref)
            stm_ref[...] = jnp.zeros_like(stm_ref)

        dxv = dx_ref[...]
        stg_ref[pl.ds(0, 1), :] = stg_ref[pl.ds(0, 1), :] + jnp.sum(dxv * y_ref[...].astype(F32), axis=0, keepdims=True)
        dyb = (dxv * g_ref[...]).astype(BF16)
        dyb_ref[...] = dyb
        da = _nt(dyb, wo_ref[...])
        dgate = (da * gu_ref[:, pl.ds(0, K)].astype(F32)).astype(BF16)
        dup = (da * gu_ref[:, pl.ds(K, K)].astype(F32)).astype(BF16)
        dgu_ref[:, pl.ds(0, K)] = dgate
        dgu_ref[:, pl.ds(K, K)] = dup
        dh = _nn(dgate, wi_ref[pl.ds(0, K), :]) + _nn(dup, wi_ref[pl.ds(K, K), :])
        xv = x_ref[...]
        rstd = lax.rsqrt(jnp.mean(xv * xv, axis=-1, keepdims=True) + RMS_EPS)
        xh = xv * rstd
        nwv, scale = nw_ref[...], 1.0 + sc_ref[...]
        dxh = dh * (nwv * scale)
        dx1 = dxv + rstd * (dxh - xh * jnp.mean(dxh * xh, axis=-1, keepdims=True))
        dxo_ref[...] = dx1
        dhx = dh * xh
        stf_ref[pl.ds(0, 1), :] = stf_ref[pl.ds(0, 1), :] + jnp.sum(dh, axis=0, keepdims=True)
        stf_ref[pl.ds(1, 1), :] = stf_ref[pl.ds(1, 1), :] + jnp.sum(dhx * nwv, axis=0, keepdims=True)
        stf_ref[pl.ds(2, 1), :] = stf_ref[pl.ds(2, 1), :] + jnp.sum(dhx * scale, axis=0, keepdims=True)
        stm_ref[pl.ds(0, 1), :] = stm_ref[pl.ds(0, 1), :] + jnp.sum(dx1 * y1_ref[...].astype(F32), axis=0, keepdims=True)
        dy1 = (dx1 * g1_ref[...]).astype(BF16)
        dy1_ref[...] = dy1
        da_ref[...] = (_nn(dy1, wm_ref[...]) if mix_is_transposed else _nt(dy1, wm_ref[...])).astype(BF16)

    row = lambda w_: pl.BlockSpec((tm, w_), lambda i: (i, 0))
    vec = _const_spec((1, D))
    st = jax.ShapeDtypeStruct((8, D), F32)
    act = lambda w_: jax.ShapeDtypeStruct((S, w_), BF16)
    return pl.pallas_call(
        body, name=name, grid=(S // tm,),
        out_shape=(act(2 * K), act(D), jax.ShapeDtypeStruct((S, D), F32), act(Km), act(D), st, st, st),
        in_specs=[row(D), row(D), vec, _const_spec(w_out.shape), row(2 * K), _const_spec(wt_in.shape), row(D), vec, vec,
                  row(D), vec, _const_spec(w_mix.shape)],
        out_specs=(row(2 * K), row(D), row(D), row(Km), row(D), _const_spec((8, D)), _const_spec((8, D)), _const_spec((8, D))),
        compiler_params=_params(("arbitrary",), VMEM_LIMIT),
    )(dx, y, g, w_out, gu, wt_in, x, nw, sc, y1, g1, w_mix)


def _weight_grad(a, b_parts, *, transpose_out, name):
    S = a.shape[0]
    N = sum(p[0].shape[2] if isinstance(p, tuple) else p.shape[1] for p in b_parts)
    npart = len(b_parts)
    nb = N // 2 if N > 4096 else N
    assert nb == N or npart == 1
    tk = 512
    if npart == 1:
        part_arrays, part_specs = list(b_parts), [pl.BlockSpec((tk, nb), lambda n, k: (k, n))]
    else:
        part_arrays, part_specs = _piece_specs(b_parts, tk, 1)

    def body(*refs):
        a_ref, part_refs = refs[0], refs[1:1 + npart]
        out_ref, acc = refs[1 + npart:]
        k = pl.program_id(1)

        @pl.when(k == 0)
        def _():
            acc[...] = jnp.zeros_like(acc)

        acc[...] += _tn(a_ref[...], _columns(part_refs, slice(None)))

        @pl.when(k == pl.num_programs(1) - 1)
        def _():
            out_ref[...] = (acc[...].T if transpose_out else acc[...]).astype(BF16)

    out_block = pl.BlockSpec((nb, D), lambda n, k: (n, 0)) if transpose_out else pl.BlockSpec((D, nb), lambda n, k: (0, n))
    return pl.pallas_call(
        body, name=name, grid=(N // nb, S // tk),
        out_shape=jax.ShapeDtypeStruct((N, D) if transpose_out else (D, N), BF16),
        in_specs=[pl.BlockSpec((tk, D), lambda n, k: (k, 0))] + part_specs,
        out_specs=out_block, scratch_shapes=[pltpu.VMEM((D, nb), F32)],
        compiler_params=_params(("parallel", "arbitrary"), VMEM_LIMIT),
    )(a, *part_arrays)


def _adamw(w, g, m, v):
    m = ADAM_B1 * m + (1.0 - ADAM_B1) * g
    v = ADAM_B2 * v + (1.0 - ADAM_B2) * (g * g)
    m_hat = m / (1.0 - ADAM_B1 ** ADAM_STEP)
    v_hat = v / (1.0 - ADAM_B2 ** ADAM_STEP)
    delta = -ADAM_LR * (m_hat / (jnp.sqrt(v_hat) + ADAM_EPS) + ADAM_WD * w)
    return delta, m, v


def _adam_segment(parts, own, w, m, v, outs, layer, *, name):
    R, C = own.shape
    nc = 2 if R % 32 == 0 else 1
    ch = R // nc

    def body(*refs):
        p_hbm, o_hbm, w_hbm, m_hbm, v_hbm = refs[:5]
        res_hbm = refs[-14:-10]
        p_buf, o_buf, w_buf, m_buf, v_buf, g_buf, d_buf, mo_buf, vo_buf, sems = refs[-10:]
        ins = []
        for c in range(nc):
            rows = pl.ds(c * ch, ch)
            pairs = [(p_hbm.at[:, rows], p_buf.at[:, rows]), (o_hbm.at[rows], o_buf.at[rows]), (w_hbm.at[layer, rows], w_buf.at[rows]),
                     (m_hbm.at[layer, rows], m_buf.at[rows]), (v_hbm.at[layer, rows], v_buf.at[rows])]
            copies = [pltpu.make_async_copy(src, dst, sems.at[c * 9 + q]) for q, (src, dst) in enumerate(pairs)]
            for cp in copies:
                cp.start()
            ins.append(copies)
        me = _my_index()
        outs_started = []
        for c in range(nc):
            rows = pl.ds(c * ch, ch)
            for cp in ins[c]:
                cp.wait()
            g = jnp.zeros((ch, C), F32)
            for j in range(N_DEV):
                g = g + jnp.where(me == j, o_buf[rows, :], p_buf[j, rows, :]).astype(F32)
            delta, mn, vn = _adamw(w_buf[rows, :], g, m_buf[rows, :], v_buf[rows, :])
            for q, (buf, val) in enumerate(((g_buf, g), (d_buf, delta), (mo_buf, mn), (vo_buf, vn))):
                buf[rows, :] = val
                cp = pltpu.make_async_copy(buf.at[rows], res_hbm[q].at[layer, rows], sems.at[c * 9 + 5 + q])
                cp.start()
                outs_started.append(cp)
        for cp in outs_started:
            cp.wait()

    hbm = pl.BlockSpec(memory_space=pl.ANY)
    passed = [] if outs is None else [hbm] * 4
    shp = jax.ShapeDtypeStruct(w.shape, F32)
    f32_buf = lambda: pltpu.VMEM((R, C), F32)
    return pl.pallas_call(
        body, name=name, out_shape=(shp,) * 4, in_specs=[hbm] * 5 + passed, out_specs=(hbm,) * 4,
        input_output_aliases={5 + q: q for q in range(len(passed))},
        scratch_shapes=[pltpu.VMEM((N_DEV, R, C), BF16), pltpu.VMEM((R, C), BF16)] + [f32_buf() for _ in range(7)]
        + [pltpu.SemaphoreType.DMA((nc * 9,))],
        compiler_params=_params(vmem=VMEM_LIMIT),
    )(parts, own, w, m, v, *(outs or ()))


def _adam_ada_w(cond_t, dmod, w, m, v):
    ncol = w.shape[-1]
    tr = 512

    def body(c_ref, d_ref, w_ref, m_ref, v_ref, g_out, d_out, m_out, v_out):
        g = _nn(c_ref[...], d_ref[0])
        delta, mn, vn = _adamw(w_ref[0], g, m_ref[0], v_ref[0])
        g_out[0] = g
        d_out[0] = delta
        m_out[0] = mn
        v_out[0] = vn

    blk = pl.BlockSpec((1, tr, ncol), lambda l, i: (l, i, 0))
    shp = jax.ShapeDtypeStruct(w.shape, F32)
    return pl.pallas_call(
        body, name="adam_ada_w", grid=(DEPTH, D // tr), out_shape=(shp,) * 4,
        in_specs=[pl.BlockSpec((tr, LANES), lambda l, i: (i, 0)), pl.BlockSpec((1, LANES, ncol), lambda l, i: (l, 0, 0)), blk, blk, blk],
        out_specs=(blk,) * 4, compiler_params=_params(("parallel", "parallel"), VMEM_LIMIT),
    )(cond_t, dmod, w, m, v)


TILE_ROWS = 168


def _stat_sources():
    pairs = []
    for i in range(DEPTH):
        b = 32 * i
        for q, src in enumerate((b, b + 1, b + 8, b + 16, b + 17, b + 24)):
            pairs.append((6 * i + q, src))
        pairs.append((24 + i, b + 2))
        pairs.append((32 + i, b + 18))
    pairs += [(40, 128), (41, 129)]
    return pairs


def _small_exchange(tiles, w, m, v):
    loss_row, sink_row, sink_src = 41, 48, 136

    def body(s_ref, w_ref, m_ref, v_ref, dmod_out, g_out, d_out, m_out, v_out, loss_out, all_ref, tot_ref, send_sems, recv_sems):
        me = _my_index()
        all_ref[me] = s_ref[...]
        copies = []
        for k in range(1, N_DEV):
            dev, _ = _peer(k)
            cp = pltpu.make_async_remote_copy(src_ref=s_ref, dst_ref=all_ref.at[me], send_sem=send_sems.at[k - 1],
                                              recv_sem=recv_sems.at[k - 1], device_id=dev, device_id_type=MESH)
            cp.start()
            copies.append(cp)
        for k in range(1, N_DEV):
            dev, pidx = _peer(k)
            pltpu.make_async_remote_copy(src_ref=s_ref, dst_ref=all_ref.at[pidx], send_sem=send_sems.at[k - 1],
                                         recv_sem=recv_sems.at[k - 1], device_id=dev, device_id_type=MESH).wait_recv()
        for cp in copies:
            cp.wait_send()
        tot = all_ref[0]
        for j in range(1, N_DEV):
            tot = tot + all_ref[j]
        tot_ref[...] = tot
        g_out[...] = jnp.zeros_like(g_out)
        for dst, src in _stat_sources():
            g_out[pl.ds(dst, 1), :] = tot_ref[pl.ds(src, 1), :]
            if dst < 24:
                for j in range(N_DEV):
                    dmod_out[j, pl.ds(dst, 1), :] = all_ref[j, pl.ds(src, 1), :]
        lane = lax.broadcasted_iota(jnp.int32, (1, D), 1)
        sink = jnp.zeros((1, D), F32)
        for h in range(32):
            sink = jnp.where(lane == h, tot_ref[pl.ds(sink_src + h, 1), :], sink)
        g_out[pl.ds(sink_row, 1), :] = sink
        g = g_out[...]
        delta, mn, vn = _adamw(w_ref[...], g, m_ref[...], v_ref[...])
        d_out[...] = delta
        m_out[...] = mn
        v_out[...] = vn
        loss = jnp.sum(g[loss_row:loss_row + 1, :], axis=-1, keepdims=True) * (0.5 / D)
        loss_out[...] = jnp.broadcast_to(loss, loss_out.shape)

    vm = pl.BlockSpec(memory_space=pltpu.VMEM)
    shp = jax.ShapeDtypeStruct((STAT_ROWS, D), F32)
    return pl.pallas_call(
        body, name="small_exchange",
        out_shape=(jax.ShapeDtypeStruct((N_DEV, 24, D), F32), shp, shp, shp, shp, jax.ShapeDtypeStruct((8, LANES), F32)),
        in_specs=[vm] * 4, out_specs=(vm,) * 6,
        scratch_shapes=[pltpu.VMEM((N_DEV, TILE_ROWS, D), F32), pltpu.VMEM((TILE_ROWS, D), F32),
                        pltpu.SemaphoreType.DMA((N_DEV - 1,)), pltpu.SemaphoreType.DMA((N_DEV - 1,))],
        compiler_params=_params(vmem=VMEM_LIMIT),
    )(tiles, w, m, v)


def _to_rows(name, a):
    if name in ("ffn_in", "a_in", "b_in"):
        return a.T
    if name == "b_out":
        return a.T.reshape(-1, D)
    return a


def _rows8(a):
    return jnp.pad(a, ((0, 8 - a.shape[0]), (0, 0)))


def _pack_small(ada_b, norm_mix, norm_ffn, final_norm, sink):
    sink_row = jnp.pad(sink.reshape(1, -1), ((0, 0), (0, D - sink.size)))
    return jnp.concatenate([ada_b.reshape(24, D), _rows8(norm_mix), _rows8(norm_ffn), _rows8(final_norm.reshape(1, D)),
                            _rows8(sink_row)], axis=0)


def _unpack_small(a):
    return a[0:24].reshape(4, 6 * D), a[24:28], a[32:36], a[40], a[48, :32].reshape(2, 16)


def kernel(x, c, ada_w, ada_b, norm_mix, norm_ffn, ffn_w_in, ffn_w_out, a_w_in, a_w_out, a_sink, b_w_in, b_w_out, final_norm, loss_target, m_ada_w, m_ada_b, m_norm_mix, m_norm_ffn, m_ffn_w_in, m_ffn_w_out, m_a_w_in, m_a_w_out, m_a_sink, m_b_w_in, m_b_w_out, m_final_norm, v_ada_w, v_ada_b, v_norm_mix, v_norm_ffn, v_ffn_w_in, v_ffn_w_out, v_a_w_in, v_a_w_out, v_a_sink, v_b_w_in, v_b_w_out, v_final_norm):
    S = x.shape[1]
    x0 = x.reshape(S, D)
    target = loss_target.reshape(S, D)
    me = _my_index()
    ncol = ada_w.shape[-1]

    ada_b_mine = lax.dynamic_slice_in_dim(ada_b, me * ncol, ncol, axis=1)
    cond_all, parts = _cond_exchange(jnp.broadcast_to(c.reshape(1, D), (8, D)), ada_w, ada_b_mine)
    mod = lax.dynamic_index_in_dim(parts, me, axis=2, keepdims=False)
    mod = jnp.transpose(mod, (1, 0, 2)).reshape(DEPTH, 6, 1, D)

    weights = {"ffn_in": ffn_w_in, "ffn_out": ffn_w_out, "a_in": a_w_in, "a_out": a_w_out, "b_in": b_w_in, "b_out": b_w_out}
    shard = {(n, l): _to_rows(n, weights[n][l]).astype(BF16) for n, l, _ in SEGMENTS}
    first = [sg for sg in _layer_segments(0) if not sg[0].startswith("ffn")]
    gathered0 = _all_gather_weights([shard[(n, l)] for n, l, _ in first])
    W = {(n, l): g for (n, l, _), g in zip(first, gathered0)}
    groups = [[sg for sg in _layer_segments(0) if sg[0].startswith("ffn")]] + [_layer_segments(i) for i in range(1, DEPTH)]
    gathers, order = [], gathered0[0]
    for q, segs in enumerate(groups):
        zones = [(N_DEV, rows, D) for _, _, rows in segs]
        gathers.append(_exchange_start([shard[(n, l)] for n, l, _ in segs], zones, [(s, 0) for s in range(len(segs))],
                                       [sg[2] for sg in segs], False, order, "weight_gather_start_%d" % q))
        order = gathers[-1][-1]
    gather_token = order[0:1, 0:1]

    def finish_gather(q, after):
        segs = groups[q]
        zones = _exchange_wait(gathers[q], len(segs), [(s, 0) for s in range(len(segs))], [sg[2] for sg in segs], True, after,
                               "weight_gather_wait_%d" % q)
        for (n, l, rows), zone in zip(segs, zones):
            W[(n, l)] = zone.reshape(D, 512) if n == "b_out" else zone.reshape(N_DEV * rows, D)

    a_slopes, b_slopes = _slopes(16), _slopes(24)
    bias_a = _alibi_bias(a_slopes, A_HALF, 1)
    bias_b = [_alibi_bias(b_slopes[8 * g:8 * g + 8], B_HALF, dil) for g, dil in enumerate(B_DILS)]
    bias_b_fwd = [_alibi_bias(b_slopes[8 * g:8 * g + 8], B_HALF, dil, max(CHUNK, TQ * dil), both=True) for g, dil in enumerate(B_DILS)]
    a_geom = dict(C=A_QKV, r=1, half=A_HALF, qoff=0, koff=1024, voff=1280, n_units=2)
    b_geom = [dict(C=B_QKV, r=dil, half=B_HALF, qoff=512 * g, koff=1536 + 128 * g, voff=1920 + 128 * g, n_units=1)
              for g, dil in enumerate(B_DILS)]

    saved = []
    xcur = x0
    for i in range(DEPTH):
        j = i // 2
        sh1, sc1, g1, sh2, sc2, g2 = [mod[i, q] for q in range(6)]
        nm, nf = norm_mix[i].reshape(1, D), norm_ffn[i].reshape(1, D)
        if i == 0:
            nm = nm + gather_token
        if i % 2 == 0:
            sink_rep = jnp.repeat(jnp.repeat(a_sink[j], TQ).reshape(2, 1, 8 * TQ), 8, axis=1).reshape(16, 8 * TQ)
            if i == 0:
                h1, qkv = _proj(xcur, nm, sc1, sh1, W[("a_in", j)], name="proj_a")
            o, lse = _attn_fwd(qkv, bias_a, sink_rep, out_dtype=BF16, name="attn_a_fwd", **a_geom)
            if i == 0:
                finish_gather(0, o)
            x1, y1, h2, gu, act = _out_ffn_in(o, W[("a_out", j)], xcur, g1, nf, sc2, sh2, W[("ffn_in", i)],
                                              w_is_transposed=False, name="out_a_ffn_in")
        else:
            sink_rep = None
            merged = None
            for g in (2, 1, 0):
                merged = _attn_fwd(qkv, bias_b_fwd[g], None, out_dtype=BF16 if g == 0 else F32, name="attn_b%d_fwd" % g,
                                   prev=merged, **b_geom[g])
            o, lse = merged
            x1, y1, h2, gu, act = _out_ffn_in(o, W[("b_out", j)], xcur, g1, nf, sc2, sh2, W[("ffn_in", i)],
                                              w_is_transposed=True, name="out_b_ffn_in")
        layer = dict(x0=xcur, h1=h1, qkv=qkv)
        if i < DEPTH - 1:
            finish_gather(i + 1, x1)
            nxt = (norm_mix[i + 1].reshape(1, D), mod[i + 1, 1], mod[i + 1, 0], W[("b_in" if i % 2 == 0 else "a_in", (i + 1) // 2)])
            x2, y2, h1, qkv = _ffn_out(act, W[("ffn_out", i)], x1, g2, None, None, name="ffn_out_proj_%d" % (i % 2), nxt=nxt)
        else:
            x2, y2, head_stats = _ffn_out(act, W[("ffn_out", i)], x1, g2, target, final_norm.reshape(1, D), name="ffn_out_loss")
        saved.append(dict(layer, o=o, lse=lse, y1=y1, x1=x1, h2=h2, gu=gu, act=act, y2=y2, sink=sink_rep))
        xcur = x2

    dx = xcur

    dW = {}
    stat_tiles, dsink = [None] * DEPTH, [None] * 2
    exchanges = []
    start_token = None

    def start_exchange(segs):
        own = [lax.dynamic_slice_in_dim(dW[(n, l)], me * rows, rows, axis=0) for n, l, rows in segs]
        zones = [(N_DEV, rows, D) for _, _, rows in segs]
        started = _exchange_start([dW[(n, l)] for n, l, _ in segs], zones, [(s, 0) for s in range(len(segs))],
                                  [sg[2] for sg in segs], True, own[0], "grad_exchange_start_%d" % len(exchanges))
        exchanges.append((segs, started, own))
        return started[-1][0:1, 0:1]

    for i in reversed(range(DEPTH)):
        j = i // 2
        sv = saved[i]
        sh1, sc1, g1, sh2, sc2, g2 = [mod[i, q] for q in range(6)]
        if start_token is not None:
            g2 = g2 + start_token
            start_token = None
        nm, nf = norm_mix[i].reshape(1, D), norm_ffn[i].reshape(1, D)
        mix = "a_out" if i % 2 == 0 else "b_out"
        dgu, dy2, dx1, do, dy1, st_g2, st_f, st_g1 = _ffn_bwd(
            dx, sv["y2"], g2, W[("ffn_out", i)], sv["gu"], W[("ffn_in", i)], sv["x1"], nf, sc2, sv["y1"], g1, W[(mix, j)],
            mix_is_transposed=(i % 2 == 1), name="ffn_bwd_" + mix)
        dW[("ffn_out", i)] = _weight_grad(dy2, [sv["act"]], transpose_out=True, name="dw_ffn_out")
        dW[("ffn_in", i)] = _weight_grad(sv["h2"], [dgu], transpose_out=True, name="dw_ffn_in")
        sink_bwd = sv["sink"]
        if i == 0:
            sink_bwd = sink_bwd + start_exchange([sg for sg in _layer_segments(0) if sg[0].startswith("ffn")])
        if i % 2 == 0:
            dW[("a_out", j)] = _weight_grad(dy1, [sv["o"]], transpose_out=True, name="dw_a_out")
            dq, dk, dv, ds = _attn_bwd(sv["qkv"], bias_a, sink_bwd, sv["o"], do, sv["lse"], name="attn_a_bwd", **a_geom)
            dsink[j] = ds
            dqkv = [dq, (dk, 0), (dk, 1), (dv, 0), (dv, 1)]
            dx0, st_m, dW[("a_in", j)] = _norm_bwd(dqkv, W[("a_in", j)], sv["x0"], dx1, nm, sc1, sv["h1"], name="proj_a_bwd")
        else:
            dW[("b_out", j)] = _weight_grad(dy1, [sv["o"]], transpose_out=False, name="dw_b_out").reshape(N_DEV * 64, D)
            gr = [_attn_bwd(sv["qkv"], bias_b[g], None, sv["o"], do, sv["lse"], name="attn_b%d_bwd" % g, **b_geom[g]) for g in range(3)]
            dqkv = [t[0] for t in gr] + [(t[1], 0) for t in gr] + [(t[2], 0) for t in gr]
            dx0, st_m, dW[("b_in", j)] = _norm_bwd(dqkv, W[("b_in", j)], sv["x0"], dx1, nm, sc1, sv["h1"], name="proj_b_bwd")
        stat_tiles[i] = [st_m, st_g1, st_f, st_g2]
        if i > 0:
            start_token = start_exchange(_layer_segments(i))
        else:
            start_exchange([sg for sg in _layer_segments(0) if not sg[0].startswith("ffn")])
        dx = dx0
    grad_x = dx.reshape(1, S, D)

    masters = {"ffn_in": (ffn_w_in, m_ffn_w_in, v_ffn_w_in), "ffn_out": (ffn_w_out, m_ffn_w_out, v_ffn_w_out),
               "a_in": (a_w_in, m_a_w_in, v_a_w_in), "a_out": (a_w_out, m_a_w_out, v_a_w_out),
               "b_in": (b_w_in, m_b_w_in, v_b_w_in), "b_out": (b_w_out, m_b_w_out, v_b_w_out)}
    by_rows = ("ffn_in", "a_in", "b_in", "b_out")
    masters = {n: tuple(jnp.swapaxes(t, 1, 2) for t in wmv) if n in by_rows else wmv for n, wmv in masters.items()}
    results = {n: None for n in masters}
    after = dx
    for e, (segs, started, own) in enumerate(exchanges):
        zones = _exchange_wait(started, len(segs), [(s, 0) for s in range(len(segs))], [sg[2] for sg in segs], False, after,
                               "grad_exchange_wait_%d" % e)
        for (n, l, rows), zone, mine in zip(segs, zones, own):
            if n == "b_out":
                zone, mine = zone.reshape(N_DEV, LANES, 512), mine.reshape(LANES, 512)
            results[n] = _adam_segment(zone, mine, *masters[n], results[n], l, name="adam_" + n)
            after = results[n][0]
    big = {(kind, n): jnp.swapaxes(results[n][q], 1, 2) if n in by_rows else results[n][q]
           for q, kind in enumerate(("grad", "delta", "m", "v")) for n in masters}

    tiles = jnp.concatenate([t for i in range(DEPTH) for t in stat_tiles[i]] + [head_stats]
                            + [jnp.pad(ds, ((0, 0), (0, D - LANES))) for ds in dsink], axis=0)
    small = [_pack_small(*t) for t in ((ada_b, norm_mix, norm_ffn, final_norm, a_sink),
                                       (m_ada_b, m_norm_mix, m_norm_ffn, m_final_norm, m_a_sink),
                                       (v_ada_b, v_norm_mix, v_norm_ffn, v_final_norm, v_a_sink))]
    dmod_all, sg, sd, sm, sv_, loss_tile = _small_exchange(tiles, *small)
    loss = loss_tile[0, 0]
    dmod_all = dmod_all.reshape(N_DEV, DEPTH, 6 * D)
    dmod_mine = lax.dynamic_slice_in_dim(dmod_all, me * ncol, ncol, axis=2)
    dmod_pad = jnp.pad(jnp.transpose(dmod_mine, (1, 0, 2)), ((0, 0), (0, LANES - N_DEV), (0, 0))).astype(BF16)
    cond_t = jnp.pad(cond_all.T, ((0, 0), (0, LANES - N_DEV))).astype(BF16)
    ada = _adam_ada_w(cond_t, dmod_pad, ada_w, m_ada_w, v_ada_w)

    outs = [loss, grad_x]
    small_res = [_unpack_small(t) for t in (sg, sd, sm, sv_)]
    for q, kind in enumerate(("grad", "delta", "m", "v")):
        ab, nm_, nf_, fn, sk = small_res[q]
        outs += [ada[q], ab, nm_, nf_, big[(kind, "ffn_in")], big[(kind, "ffn_out")], big[(kind, "a_in")], big[(kind, "a_out")],
                 sk, big[(kind, "b_in")], big[(kind, "b_out")], fn]
    return tuple(outs)
```
